```python
import math
import jax, jax.numpy as jnp
from jax import lax
import numpy as np

D_MODEL = 1024
BATCH = 8
SEQ = 8192
DEPTH = 4

N_MIXERS = 4
RMS_EPS = 1e-6
LN_EPS = 1e-5

SSD_EXPAND = 2
SSD_D_INNER = SSD_EXPAND * D_MODEL
SSD_HEAD_DIM = 64
SSD_N_HEADS = SSD_D_INNER // SSD_HEAD_DIM
SSD_N_GROUPS = 8
SSD_HEADS_PER_GROUP = SSD_N_HEADS // SSD_N_GROUPS
SSD_D_STATE = 128
SSD_CONV = 4
SSD_CHUNK = 128
SSD_BC_DIM = SSD_N_GROUPS * SSD_D_STATE
SSD_CONV_DIM = SSD_D_INNER + 2 * SSD_BC_DIM
SSD_IN_DIM = SSD_D_INNER + SSD_CONV_DIM + SSD_N_HEADS

CONF_WIDTH = D_MODEL
CONF_KERNEL = 31

LRU_WIDTH = 1280
LRU_BLOCK = 256
LRU_N_BLOCKS = LRU_WIDTH // LRU_BLOCK
LRU_CONV = 4
LRU_C = 8.0

SGU_CHUNK = 128
SGU_FFN = 4 * D_MODEL
SGU_HALF = SGU_FFN // 2
SGU_GROUPS = 8
SGU_GROUP_DIM = SGU_HALF // SGU_GROUPS

FFN_HIDDEN = 2816
FFN_CONV = 3

kernel_name = 'hybrid_interleaved_ssd_conformer_rglru_gmlp'


def rms_norm(x, g):
    xf = x.astype(jnp.float32)
    y = xf * lax.rsqrt(jnp.mean(xf * xf, axis=-1, keepdims=True) + RMS_EPS)
    return (y * g.astype(jnp.float32)).astype(x.dtype)


def layer_norm(x, g, b):
    xf = x.astype(jnp.float32)
    mu = jnp.mean(xf, axis=-1, keepdims=True)
    xc = xf - mu
    y = xc * lax.rsqrt(jnp.mean(xc * xc, axis=-1, keepdims=True) + LN_EPS)
    return (y * g.astype(jnp.float32) + b.astype(jnp.float32)).astype(x.dtype)


def causal_dwconv(x, w, b):
    k, c = w.shape
    y = lax.conv_general_dilated(
        x, w[:, None, :].astype(x.dtype), window_strides=(1,), padding=[(k - 1, 0)],
        dimension_numbers=('NWC', 'WIO', 'NWC'), feature_group_count=c)
    return y + b.astype(x.dtype)


def ssd_mixer(x, in_proj, conv_w, conv_b, dt_bias, a_log, d_skip, norm_g, out_proj):
    bsz, seqlen, _ = x.shape
    f32 = jnp.float32
    nc = seqlen // SSD_CHUNK
    g, j, p, n, q = SSD_N_GROUPS, SSD_HEADS_PER_GROUP, SSD_HEAD_DIM, SSD_D_STATE, SSD_CHUNK
    zxbcdt = x @ in_proj
    z, xbc, dt = jnp.split(zxbcdt, [SSD_D_INNER, SSD_D_INNER + SSD_CONV_DIM], axis=-1)
    xbc = jax.nn.silu(causal_dwconv(xbc, conv_w, conv_b))
    xs, bm, cm = jnp.split(xbc, [SSD_D_INNER, SSD_D_INNER + SSD_BC_DIM], axis=-1)
    xs = jnp.moveaxis(xs.astype(f32).reshape(bsz, nc, q, g, j, p), 1, 0)
    bm = jnp.moveaxis(bm.astype(f32).reshape(bsz, nc, q, g, n), 1, 0)
    cm = jnp.moveaxis(cm.astype(f32).reshape(bsz, nc, q, g, n), 1, 0)
    dt = jax.nn.softplus(dt.astype(f32) + dt_bias.astype(f32))
    dt = jnp.moveaxis(dt.reshape(bsz, nc, q, g, j), 1, 0)
    a = -jnp.exp(a_log.astype(f32)).reshape(g, j)
    dsk = d_skip.astype(f32).reshape(g, j)
    causal = jnp.tril(jnp.ones((q, q), dtype=bool))[None, :, :, None, None]

    def chunk_step(state, inp):
        xc, bc, cc, dtc = inp
        acs = jnp.cumsum(dtc * a, axis=1)
        seg = acs[:, :, None] - acs[:, None, :]
        decay = jnp.exp(jnp.where(causal, seg, -jnp.inf))
        cb = jnp.einsum('btgn,bsgn->btsg', cc, bc)
        scores = cb[..., None] * decay * dtc[:, None]
        y_diag = jnp.einsum('btsgj,bsgjp->btgjp', scores, xc)
        y_off = jnp.einsum('btgn,bgjpn->btgjp', cc, state) * jnp.exp(acs)[..., None]
        decay_end = jnp.exp(acs[:, -1:] - acs) * dtc
        new_state = (state * jnp.exp(acs[:, -1])[..., None, None]
                     + jnp.einsum('bsgn,bsgj,bsgjp->bgjpn', bc, decay_end, xc))
        return new_state, y_diag + y_off + xc * dsk[..., None]

    state0 = jnp.zeros((bsz, g, j, p, n), f32)
    _, ys = lax.scan(chunk_step, state0, (xs, bm, cm, dt))
    y = jnp.moveaxis(ys, 0, 1).reshape(bsz, seqlen, SSD_D_INNER)
    y = rms_norm(y * jax.nn.silu(z.astype(f32)), norm_g)
    return y.astype(x.dtype) @ out_proj


def conformer_conv(x, pw1_w, pw1_b, dw_w, dw_b, ln_g, ln_b, pw2_w, pw2_b):
    h = jax.nn.glu(x @ pw1_w + pw1_b, axis=-1)
    h = causal_dwconv(h, dw_w, dw_b)
    h = jax.nn.silu(layer_norm(h, ln_g, ln_b))
    return h @ pw2_w + pw2_b


def rglru_block(x, in_w, in_b, conv_w, conv_b, ga_w, ga_b, gx_w, gx_b, lam, out_w, out_b):
    bsz, seqlen, _ = x.shape
    f32 = jnp.float32
    gate, xr = jnp.split(x @ in_w + in_b, 2, axis=-1)
    xr = causal_dwconv(xr, conv_w, conv_b)
    xb = xr.reshape(bsz, seqlen, LRU_N_BLOCKS, LRU_BLOCK)
    r = jax.nn.sigmoid(jnp.einsum('blhi,hij->blhj', xb, ga_w) + ga_b).reshape(bsz, seqlen, LRU_WIDTH)
    i = jax.nn.sigmoid(jnp.einsum('blhi,hij->blhj', xb, gx_w) + gx_b).reshape(bsz, seqlen, LRU_WIDTH)
    log_a = -LRU_C * r.astype(f32) * jax.nn.softplus(-lam.astype(f32))
    a = jnp.exp(log_a)
    bterm = jnp.sqrt(-jnp.expm1(2.0 * log_a)) * (i.astype(f32) * xr.astype(f32))

    def combine(lhs, rhs):
        a1, b1 = lhs
        a2, b2 = rhs
        return a1 * a2, a2 * b1 + b2

    _, h = lax.associative_scan(combine, (a, bterm), axis=1)
    y = jax.nn.gelu(gate) * h.astype(x.dtype)
    return y @ out_w + out_b


def chunked_sgu(x, in_w, in_b, ln_g, ln_b, sp_w, sp_b, out_w, out_b):
    bsz, seqlen, _ = x.shape
    nc = seqlen // SGU_CHUNK
    z = jax.nn.gelu(x @ in_w + in_b)
    u, v = jnp.split(z, 2, axis=-1)
    v = layer_norm(v, ln_g, ln_b).reshape(bsz, nc, SGU_CHUNK, SGU_GROUPS, SGU_GROUP_DIM)
    w = sp_w * jnp.tril(jnp.ones((SGU_CHUNK, SGU_CHUNK), sp_w.dtype))
    mixed = jnp.einsum('gts,bcsgk->bctgk', w, v) + jnp.swapaxes(sp_b, 0, 1)[None, None, :, :, None]
    return (u * mixed.reshape(bsz, seqlen, SGU_HALF)) @ out_w + out_b


def conv_ffn(x, up_w, conv_w, conv_b, down_w):
    h = causal_dwconv(x @ up_w, conv_w, conv_b)
    gt, val = jnp.split(h, 2, axis=-1)
    return (jax.nn.silu(gt) * val) @ down_w


def _fwd_setup_inputs(seed: int = 0) -> dict:
    key = jax.random.key(seed)
    keys = iter(jax.random.split(key, 64))
    f32 = jnp.float32

    def nrm(shape, scale):
        return jax.random.normal(next(keys), shape, f32) * scale

    def gain(shape):
        return 1.0 + nrm(shape, 0.02)

    def unif(shape, lo, hi):
        return jax.random.uniform(next(keys), shape, f32, minval=lo, maxval=hi)

    n_a = (DEPTH + 3) // N_MIXERS
    n_b = (DEPTH + 2) // N_MIXERS
    n_c = (DEPTH + 1) // N_MIXERS
    n_d = DEPTH // N_MIXERS
    D = D_MODEL

    x = jax.random.normal(next(keys), (BATCH, SEQ, D), f32)
    norm_mix = gain((DEPTH, D))
    norm_ffn = gain((DEPTH, D))
    norm_final = gain((D,))

    a_in_proj = nrm((n_a, D, SSD_IN_DIM), D ** -0.5)
    a_conv_w = nrm((n_a, SSD_CONV, SSD_CONV_DIM), SSD_CONV ** -0.5)
    a_conv_b = nrm((n_a, SSD_CONV_DIM), 0.02)
    dt0 = jnp.exp(unif((n_a, SSD_N_HEADS), math.log(1e-3), math.log(1e-1)))
    a_dt_bias = dt0 + jnp.log(-jnp.expm1(-dt0))
    a_log = jnp.log(unif((n_a, SSD_N_HEADS), 1.0, 16.0))
    a_d_skip = gain((n_a, SSD_N_HEADS))
    a_norm = gain((n_a, SSD_D_INNER))
    a_out_proj = nrm((n_a, SSD_D_INNER, D), SSD_D_INNER ** -0.5)

    b_pw1_w = nrm((n_b, D, 2 * CONF_WIDTH), D ** -0.5)
    b_pw1_b = nrm((n_b, 2 * CONF_WIDTH), 0.02)
    b_dw_w = nrm((n_b, CONF_KERNEL, CONF_WIDTH), CONF_KERNEL ** -0.5)
    b_dw_b = nrm((n_b, CONF_WIDTH), 0.02)
    b_ln_g = gain((n_b, CONF_WIDTH))
    b_ln_b = nrm((n_b, CONF_WIDTH), 0.02)
    b_pw2_w = nrm((n_b, CONF_WIDTH, D), CONF_WIDTH ** -0.5)
    b_pw2_b = nrm((n_b, D), 0.02)

    c_in_w = nrm((n_c, D, 2 * LRU_WIDTH), D ** -0.5)
    c_in_b = nrm((n_c, 2 * LRU_WIDTH), 0.02)
    c_conv_w = nrm((n_c, LRU_CONV, LRU_WIDTH), LRU_CONV ** -0.5)
    c_conv_b = nrm((n_c, LRU_WIDTH), 0.02)
    c_ga_w = nrm((n_c, LRU_N_BLOCKS, LRU_BLOCK, LRU_BLOCK), LRU_BLOCK ** -0.5)
    c_ga_b = nrm((n_c, LRU_N_BLOCKS, LRU_BLOCK), 0.02)
    c_gx_w = nrm((n_c, LRU_N_BLOCKS, LRU_BLOCK, LRU_BLOCK), LRU_BLOCK ** -0.5)
    c_gx_b = nrm((n_c, LRU_N_BLOCKS, LRU_BLOCK), 0.02)
    s = unif((n_c, LRU_WIDTH), 0.9, 0.999) ** (1.0 / LRU_C)
    c_lambda = jnp.log(s) - jnp.log1p(-s)
    c_out_w = nrm((n_c, LRU_WIDTH, D), LRU_WIDTH ** -0.5)
    c_out_b = nrm((n_c, D), 0.02)

    d_in_w = nrm((n_d, D, SGU_FFN), D ** -0.5)
    d_in_b = nrm((n_d, SGU_FFN), 0.02)
    d_ln_g = gain((n_d, SGU_HALF))
    d_ln_b = nrm((n_d, SGU_HALF), 0.02)
    d_sp_w = nrm((n_d, SGU_GROUPS, SGU_CHUNK, SGU_CHUNK), SGU_CHUNK ** -0.5)
    d_sp_b = gain((n_d, SGU_GROUPS, SGU_CHUNK))
    d_out_w = nrm((n_d, SGU_HALF, D), SGU_HALF ** -0.5)
    d_out_b = nrm((n_d, D), 0.02)

    f_up_w = nrm((DEPTH, D, 2 * FFN_HIDDEN), D ** -0.5)
    f_conv_w = nrm((DEPTH, FFN_CONV, 2 * FFN_HIDDEN), FFN_CONV ** -0.5)
    f_conv_b = nrm((DEPTH, 2 * FFN_HIDDEN), 0.02)
    f_down_w = nrm((DEPTH, FFN_HIDDEN, D), FFN_HIDDEN ** -0.5)

    return {'x': x, 'norm_mix': norm_mix, 'norm_ffn': norm_ffn, 'norm_final': norm_final,
            'a_in_proj': a_in_proj, 'a_conv_w': a_conv_w, 'a_conv_b': a_conv_b, 'a_dt_bias': a_dt_bias,
            'a_log': a_log, 'a_d_skip': a_d_skip, 'a_norm': a_norm, 'a_out_proj': a_out_proj,
            'b_pw1_w': b_pw1_w, 'b_pw1_b': b_pw1_b, 'b_dw_w': b_dw_w, 'b_dw_b': b_dw_b,
            'b_ln_g': b_ln_g, 'b_ln_b': b_ln_b, 'b_pw2_w': b_pw2_w, 'b_pw2_b': b_pw2_b,
            'c_in_w': c_in_w, 'c_in_b': c_in_b, 'c_conv_w': c_conv_w, 'c_conv_b': c_conv_b,
            'c_ga_w': c_ga_w, 'c_ga_b': c_ga_b, 'c_gx_w': c_gx_w, 'c_gx_b': c_gx_b,
            'c_lambda': c_lambda, 'c_out_w': c_out_w, 'c_out_b': c_out_b,
            'd_in_w': d_in_w, 'd_in_b': d_in_b, 'd_ln_g': d_ln_g, 'd_ln_b': d_ln_b,
            'd_sp_w': d_sp_w, 'd_sp_b': d_sp_b, 'd_out_w': d_out_w, 'd_out_b': d_out_b,
            'f_up_w': f_up_w, 'f_conv_w': f_conv_w, 'f_conv_b': f_conv_b, 'f_down_w': f_down_w}


def _fwd_reference(x, norm_mix, norm_ffn, norm_final,
              a_in_proj, a_conv_w, a_conv_b, a_dt_bias, a_log, a_d_skip, a_norm, a_out_proj,
              b_pw1_w, b_pw1_b, b_dw_w, b_dw_b, b_ln_g, b_ln_b, b_pw2_w, b_pw2_b,
              c_in_w, c_in_b, c_conv_w, c_conv_b, c_ga_w, c_ga_b, c_gx_w, c_gx_b, c_lambda, c_out_w, c_out_b,
              d_in_w, d_in_b, d_ln_g, d_ln_b, d_sp_w, d_sp_b, d_out_w, d_out_b,
              f_up_w, f_conv_w, f_conv_b, f_down_w):
    h = x
    for i in range(DEPTH):
        kind, j = i % N_MIXERS, i // N_MIXERS
        u = rms_norm(h, norm_mix[i])
        if kind == 0:
            m = ssd_mixer(u, a_in_proj[j], a_conv_w[j], a_conv_b[j], a_dt_bias[j], a_log[j],
                          a_d_skip[j], a_norm[j], a_out_proj[j])
        elif kind == 1:
            m = conformer_conv(u, b_pw1_w[j], b_pw1_b[j], b_dw_w[j], b_dw_b[j], b_ln_g[j], b_ln_b[j],
                               b_pw2_w[j], b_pw2_b[j])
        elif kind == 2:
            m = rglru_block(u, c_in_w[j], c_in_b[j], c_conv_w[j], c_conv_b[j], c_ga_w[j], c_ga_b[j],
                            c_gx_w[j], c_gx_b[j], c_lambda[j], c_out_w[j], c_out_b[j])
        else:
            m = chunked_sgu(u, d_in_w[j], d_in_b[j], d_ln_g[j], d_ln_b[j], d_sp_w[j], d_sp_b[j],
                            d_out_w[j], d_out_b[j])
        h = h + m
        h = h + conv_ffn(rms_norm(h, norm_ffn[i]), f_up_w[i], f_conv_w[i], f_conv_b[i], f_down_w[i])
    return rms_norm(h, norm_final)


import jax as _jax
import jax.numpy as _jnp

TWIN_FORMAT = 'train_step'
FWD_PARAMS = ['x', 'norm_mix', 'norm_ffn', 'norm_final', 'a_in_proj', 'a_conv_w', 'a_conv_b', 'a_dt_bias', 'a_log', 'a_d_skip', 'a_norm', 'a_out_proj', 'b_pw1_w', 'b_pw1_b', 'b_dw_w', 'b_dw_b', 'b_ln_g', 'b_ln_b', 'b_pw2_w', 'b_pw2_b', 'c_in_w', 'c_in_b', 'c_conv_w', 'c_conv_b', 'c_ga_w', 'c_ga_b', 'c_gx_w', 'c_gx_b', 'c_lambda', 'c_out_w', 'c_out_b', 'd_in_w', 'd_in_b', 'd_ln_g', 'd_ln_b', 'd_sp_w', 'd_sp_b', 'd_out_w', 'd_out_b', 'f_up_w', 'f_conv_w', 'f_conv_b', 'f_down_w']
TWIN_WEIGHTS = ['norm_mix', 'norm_ffn', 'norm_final', 'a_in_proj', 'a_conv_w', 'a_conv_b', 'a_dt_bias', 'a_log', 'a_d_skip', 'a_norm', 'a_out_proj', 'b_pw1_w', 'b_pw1_b', 'b_dw_w', 'b_dw_b', 'b_ln_g', 'b_ln_b', 'b_pw2_w', 'b_pw2_b', 'c_in_w', 'c_in_b', 'c_conv_w', 'c_conv_b', 'c_ga_w', 'c_ga_b', 'c_gx_w', 'c_gx_b', 'c_lambda', 'c_out_w', 'c_out_b', 'd_in_w', 'd_in_b', 'd_ln_g', 'd_ln_b', 'd_sp_w', 'd_sp_b', 'd_out_w', 'd_out_b', 'f_up_w', 'f_conv_w', 'f_conv_b', 'f_down_w']
TWIN_DIFF_INPUT = 'x'
TWIN_INPUTS = ['x', 'norm_mix', 'norm_ffn', 'norm_final', 'a_in_proj', 'a_conv_w', 'a_conv_b', 'a_dt_bias', 'a_log', 'a_d_skip', 'a_norm', 'a_out_proj', 'b_pw1_w', 'b_pw1_b', 'b_dw_w', 'b_dw_b', 'b_ln_g', 'b_ln_b', 'b_pw2_w', 'b_pw2_b', 'c_in_w', 'c_in_b', 'c_conv_w', 'c_conv_b', 'c_ga_w', 'c_ga_b', 'c_gx_w', 'c_gx_b', 'c_lambda', 'c_out_w', 'c_out_b', 'd_in_w', 'd_in_b', 'd_ln_g', 'd_ln_b', 'd_sp_w', 'd_sp_b', 'd_out_w', 'd_out_b', 'f_up_w', 'f_conv_w', 'f_conv_b', 'f_down_w', 'loss_target', 'm_norm_mix', 'm_norm_ffn', 'm_norm_final', 'm_a_in_proj', 'm_a_conv_w', 'm_a_conv_b', 'm_a_dt_bias', 'm_a_log', 'm_a_d_skip', 'm_a_norm', 'm_a_out_proj', 'm_b_pw1_w', 'm_b_pw1_b', 'm_b_dw_w', 'm_b_dw_b', 'm_b_ln_g', 'm_b_ln_b', 'm_b_pw2_w', 'm_b_pw2_b', 'm_c_in_w', 'm_c_in_b', 'm_c_conv_w', 'm_c_conv_b', 'm_c_ga_w', 'm_c_ga_b', 'm_c_gx_w', 'm_c_gx_b', 'm_c_lambda', 'm_c_out_w', 'm_c_out_b', 'm_d_in_w', 'm_d_in_b', 'm_d_ln_g', 'm_d_ln_b', 'm_d_sp_w', 'm_d_sp_b', 'm_d_out_w', 'm_d_out_b', 'm_f_up_w', 'm_f_conv_w', 'm_f_conv_b', 'm_f_down_w', 'v_norm_mix', 'v_norm_ffn', 'v_norm_final', 'v_a_in_proj', 'v_a_conv_w', 'v_a_conv_b', 'v_a_dt_bias', 'v_a_log', 'v_a_d_skip', 'v_a_norm', 'v_a_out_proj', 'v_b_pw1_w', 'v_b_pw1_b', 'v_b_dw_w', 'v_b_dw_b', 'v_b_ln_g', 'v_b_ln_b', 'v_b_pw2_w', 'v_b_pw2_b', 'v_c_in_w', 'v_c_in_b', 'v_c_conv_w', 'v_c_conv_b', 'v_c_ga_w', 'v_c_ga_b', 'v_c_gx_w', 'v_c_gx_b', 'v_c_lambda', 'v_c_out_w', 'v_c_out_b', 'v_d_in_w', 'v_d_in_b', 'v_d_ln_g', 'v_d_ln_b', 'v_d_sp_w', 'v_d_sp_b', 'v_d_out_w', 'v_d_out_b', 'v_f_up_w', 'v_f_conv_w', 'v_f_conv_b', 'v_f_down_w']
TWIN_OUTPUTS = ['loss', 'grad_x', 'grad_norm_mix', 'grad_norm_ffn', 'grad_norm_final', 'grad_a_in_proj', 'grad_a_conv_w', 'grad_a_conv_b', 'grad_a_dt_bias', 'grad_a_log', 'grad_a_d_skip', 'grad_a_norm', 'grad_a_out_proj', 'grad_b_pw1_w', 'grad_b_pw1_b', 'grad_b_dw_w', 'grad_b_dw_b', 'grad_b_ln_g', 'grad_b_ln_b', 'grad_b_pw2_w', 'grad_b_pw2_b', 'grad_c_in_w', 'grad_c_in_b', 'grad_c_conv_w', 'grad_c_conv_b', 'grad_c_ga_w', 'grad_c_ga_b', 'grad_c_gx_w', 'grad_c_gx_b', 'grad_c_lambda', 'grad_c_out_w', 'grad_c_out_b', 'grad_d_in_w', 'grad_d_in_b', 'grad_d_ln_g', 'grad_d_ln_b', 'grad_d_sp_w', 'grad_d_sp_b', 'grad_d_out_w', 'grad_d_out_b', 'grad_f_up_w', 'grad_f_conv_w', 'grad_f_conv_b', 'grad_f_down_w', 'delta_norm_mix', 'delta_norm_ffn', 'delta_norm_final', 'delta_a_in_proj', 'delta_a_conv_w', 'delta_a_conv_b', 'delta_a_dt_bias', 'delta_a_log', 'delta_a_d_skip', 'delta_a_norm', 'delta_a_out_proj', 'delta_b_pw1_w', 'delta_b_pw1_b', 'delta_b_dw_w', 'delta_b_dw_b', 'delta_b_ln_g', 'delta_b_ln_b', 'delta_b_pw2_w', 'delta_b_pw2_b', 'delta_c_in_w', 'delta_c_in_b', 'delta_c_conv_w', 'delta_c_conv_b', 'delta_c_ga_w', 'delta_c_ga_b', 'delta_c_gx_w', 'delta_c_gx_b', 'delta_c_lambda', 'delta_c_out_w', 'delta_c_out_b', 'delta_d_in_w', 'delta_d_in_b', 'delta_d_ln_g', 'delta_d_ln_b', 'delta_d_sp_w', 'delta_d_sp_b', 'delta_d_out_w', 'delta_d_out_b', 'delta_f_up_w', 'delta_f_conv_w', 'delta_f_conv_b', 'delta_f_down_w', 'new_m_norm_mix', 'new_m_norm_ffn', 'new_m_norm_final', 'new_m_a_in_proj', 'new_m_a_conv_w', 'new_m_a_conv_b', 'new_m_a_dt_bias', 'new_m_a_log', 'new_m_a_d_skip', 'new_m_a_norm', 'new_m_a_out_proj', 'new_m_b_pw1_w', 'new_m_b_pw1_b', 'new_m_b_dw_w', 'new_m_b_dw_b', 'new_m_b_ln_g', 'new_m_b_ln_b', 'new_m_b_pw2_w', 'new_m_b_pw2_b', 'new_m_c_in_w', 'new_m_c_in_b', 'new_m_c_conv_w', 'new_m_c_conv_b', 'new_m_c_ga_w', 'new_m_c_ga_b', 'new_m_c_gx_w', 'new_m_c_gx_b', 'new_m_c_lambda', 'new_m_c_out_w', 'new_m_c_out_b', 'new_m_d_in_w', 'new_m_d_in_b', 'new_m_d_ln_g', 'new_m_d_ln_b', 'new_m_d_sp_w', 'new_m_d_sp_b', 'new_m_d_out_w', 'new_m_d_out_b', 'new_m_f_up_w', 'new_m_f_conv_w', 'new_m_f_conv_b', 'new_m_f_down_w', 'new_v_norm_mix', 'new_v_norm_ffn', 'new_v_norm_final', 'new_v_a_in_proj', 'new_v_a_conv_w', 'new_v_a_conv_b', 'new_v_a_dt_bias', 'new_v_a_log', 'new_v_a_d_skip', 'new_v_a_norm', 'new_v_a_out_proj', 'new_v_b_pw1_w', 'new_v_b_pw1_b', 'new_v_b_dw_w', 'new_v_b_dw_b', 'new_v_b_ln_g', 'new_v_b_ln_b', 'new_v_b_pw2_w', 'new_v_b_pw2_b', 'new_v_c_in_w', 'new_v_c_in_b', 'new_v_c_conv_w', 'new_v_c_conv_b', 'new_v_c_ga_w', 'new_v_c_ga_b', 'new_v_c_gx_w', 'new_v_c_gx_b', 'new_v_c_lambda', 'new_v_c_out_w', 'new_v_c_out_b', 'new_v_d_in_w', 'new_v_d_in_b', 'new_v_d_ln_g', 'new_v_d_ln_b', 'new_v_d_sp_w', 'new_v_d_sp_b', 'new_v_d_out_w', 'new_v_d_out_b', 'new_v_f_up_w', 'new_v_f_conv_w', 'new_v_f_conv_b', 'new_v_f_down_w']
TWIN_LEAF_KINDS = {'loss': 'loss', 'grad_x': 'grad_x', 'grad_norm_mix': 'grad_w', 'grad_norm_ffn': 'grad_w', 'grad_norm_final': 'grad_w', 'grad_a_in_proj': 'grad_w', 'grad_a_conv_w': 'grad_w', 'grad_a_conv_b': 'grad_w', 'grad_a_dt_bias': 'grad_w', 'grad_a_log': 'grad_w', 'grad_a_d_skip': 'grad_w', 'grad_a_norm': 'grad_w', 'grad_a_out_proj': 'grad_w', 'grad_b_pw1_w': 'grad_w', 'grad_b_pw1_b': 'grad_w', 'grad_b_dw_w': 'grad_w', 'grad_b_dw_b': 'grad_w', 'grad_b_ln_g': 'grad_w', 'grad_b_ln_b': 'grad_w', 'grad_b_pw2_w': 'grad_w', 'grad_b_pw2_b': 'grad_w', 'grad_c_in_w': 'grad_w', 'grad_c_in_b': 'grad_w', 'grad_c_conv_w': 'grad_w', 'grad_c_conv_b': 'grad_w', 'grad_c_ga_w': 'grad_w', 'grad_c_ga_b': 'grad_w', 'grad_c_gx_w': 'grad_w', 'grad_c_gx_b': 'grad_w', 'grad_c_lambda': 'grad_w', 'grad_c_out_w': 'grad_w', 'grad_c_out_b': 'grad_w', 'grad_d_in_w': 'grad_w', 'grad_d_in_b': 'grad_w', 'grad_d_ln_g': 'grad_w', 'grad_d_ln_b': 'grad_w', 'grad_d_sp_w': 'grad_w', 'grad_d_sp_b': 'grad_w', 'grad_d_out_w': 'grad_w', 'grad_d_out_b': 'grad_w', 'grad_f_up_w': 'grad_w', 'grad_f_conv_w': 'grad_w', 'grad_f_conv_b': 'grad_w', 'grad_f_down_w': 'grad_w', 'delta_norm_mix': 'delta_w', 'delta_norm_ffn': 'delta_w', 'delta_norm_final': 'delta_w', 'delta_a_in_proj': 'delta_w', 'delta_a_conv_w': 'delta_w', 'delta_a_conv_b': 'delta_w', 'delta_a_dt_bias': 'delta_w', 'delta_a_log': 'delta_w', 'delta_a_d_skip': 'delta_w', 'delta_a_norm': 'delta_w', 'delta_a_out_proj': 'delta_w', 'delta_b_pw1_w': 'delta_w', 'delta_b_pw1_b': 'delta_w', 'delta_b_dw_w': 'delta_w', 'delta_b_dw_b': 'delta_w', 'delta_b_ln_g': 'delta_w', 'delta_b_ln_b': 'delta_w', 'delta_b_pw2_w': 'delta_w', 'delta_b_pw2_b': 'delta_w', 'delta_c_in_w': 'delta_w', 'delta_c_in_b': 'delta_w', 'delta_c_conv_w': 'delta_w', 'delta_c_conv_b': 'delta_w', 'delta_c_ga_w': 'delta_w', 'delta_c_ga_b': 'delta_w', 'delta_c_gx_w': 'delta_w', 'delta_c_gx_b': 'delta_w', 'delta_c_lambda': 'delta_w', 'delta_c_out_w': 'delta_w', 'delta_c_out_b': 'delta_w', 'delta_d_in_w': 'delta_w', 'delta_d_in_b': 'delta_w', 'delta_d_ln_g': 'delta_w', 'delta_d_ln_b': 'delta_w', 'delta_d_sp_w': 'delta_w', 'delta_d_sp_b': 'delta_w', 'delta_d_out_w': 'delta_w', 'delta_d_out_b': 'delta_w', 'delta_f_up_w': 'delta_w', 'delta_f_conv_w': 'delta_w', 'delta_f_conv_b': 'delta_w', 'delta_f_down_w': 'delta_w', 'new_m_norm_mix': 'new_m', 'new_m_norm_ffn': 'new_m', 'new_m_norm_final': 'new_m', 'new_m_a_in_proj': 'new_m', 'new_m_a_conv_w': 'new_m', 'new_m_a_conv_b': 'new_m', 'new_m_a_dt_bias': 'new_m', 'new_m_a_log': 'new_m', 'new_m_a_d_skip': 'new_m', 'new_m_a_norm': 'new_m', 'new_m_a_out_proj': 'new_m', 'new_m_b_pw1_w': 'new_m', 'new_m_b_pw1_b': 'new_m', 'new_m_b_dw_w': 'new_m', 'new_m_b_dw_b': 'new_m', 'new_m_b_ln_g': 'new_m', 'new_m_b_ln_b': 'new_m', 'new_m_b_pw2_w': 'new_m', 'new_m_b_pw2_b': 'new_m', 'new_m_c_in_w': 'new_m', 'new_m_c_in_b': 'new_m', 'new_m_c_conv_w': 'new_m', 'new_m_c_conv_b': 'new_m', 'new_m_c_ga_w': 'new_m', 'new_m_c_ga_b': 'new_m', 'new_m_c_gx_w': 'new_m', 'new_m_c_gx_b': 'new_m', 'new_m_c_lambda': 'new_m', 'new_m_c_out_w': 'new_m', 'new_m_c_out_b': 'new_m', 'new_m_d_in_w': 'new_m', 'new_m_d_in_b': 'new_m', 'new_m_d_ln_g': 'new_m', 'new_m_d_ln_b': 'new_m', 'new_m_d_sp_w': 'new_m', 'new_m_d_sp_b': 'new_m', 'new_m_d_out_w': 'new_m', 'new_m_d_out_b': 'new_m', 'new_m_f_up_w': 'new_m', 'new_m_f_conv_w': 'new_m', 'new_m_f_conv_b': 'new_m', 'new_m_f_down_w': 'new_m', 'new_v_norm_mix': 'new_v', 'new_v_norm_ffn': 'new_v', 'new_v_norm_final': 'new_v', 'new_v_a_in_proj': 'new_v', 'new_v_a_conv_w': 'new_v', 'new_v_a_conv_b': 'new_v', 'new_v_a_dt_bias': 'new_v', 'new_v_a_log': 'new_v', 'new_v_a_d_skip': 'new_v', 'new_v_a_norm': 'new_v', 'new_v_a_out_proj': 'new_v', 'new_v_b_pw1_w': 'new_v', 'new_v_b_pw1_b': 'new_v', 'new_v_b_dw_w': 'new_v', 'new_v_b_dw_b': 'new_v', 'new_v_b_ln_g': 'new_v', 'new_v_b_ln_b': 'new_v', 'new_v_b_pw2_w': 'new_v', 'new_v_b_pw2_b': 'new_v', 'new_v_c_in_w': 'new_v', 'new_v_c_in_b': 'new_v', 'new_v_c_conv_w': 'new_v', 'new_v_c_conv_b': 'new_v', 'new_v_c_ga_w': 'new_v', 'new_v_c_ga_b': 'new_v', 'new_v_c_gx_w': 'new_v', 'new_v_c_gx_b': 'new_v', 'new_v_c_lambda': 'new_v', 'new_v_c_out_w': 'new_v', 'new_v_c_out_b': 'new_v', 'new_v_d_in_w': 'new_v', 'new_v_d_in_b': 'new_v', 'new_v_d_ln_g': 'new_v', 'new_v_d_ln_b': 'new_v', 'new_v_d_sp_w': 'new_v', 'new_v_d_sp_b': 'new_v', 'new_v_d_out_w': 'new_v', 'new_v_d_out_b': 'new_v', 'new_v_f_up_w': 'new_v', 'new_v_f_conv_w': 'new_v', 'new_v_f_conv_b': 'new_v', 'new_v_f_down_w': 'new_v'}


def _forward(args):
    return _fwd_reference(*[args[k] for k in FWD_PARAMS])


def _output_shape():
    def fwd():
        inp = _fwd_setup_inputs(0)
        return _fwd_reference(*[inp[k] for k in FWD_PARAMS])
    out = _jax.eval_shape(fwd)
    return out.shape, out.dtype

N_MICROBATCH = 1
ADAM_LR = 0.001
ADAM_B1 = 0.9
ADAM_B2 = 0.999
ADAM_EPS = 1e-08
ADAM_WD = 0.01
ADAM_STEP = 10
PER_EXAMPLE_BATCH_AXIS = {'x': 0, 'loss_target': 0}
SHARED_INPUTS = []
_WEIGHT_DTYPES = {'norm_mix': _jnp.float32, 'norm_ffn': _jnp.float32, 'norm_final': _jnp.float32, 'a_in_proj': _jnp.float32, 'a_conv_w': _jnp.float32, 'a_conv_b': _jnp.float32, 'a_dt_bias': _jnp.float32, 'a_log': _jnp.float32, 'a_d_skip': _jnp.float32, 'a_norm': _jnp.float32, 'a_out_proj': _jnp.float32, 'b_pw1_w': _jnp.float32, 'b_pw1_b': _jnp.float32, 'b_dw_w': _jnp.float32, 'b_dw_b': _jnp.float32, 'b_ln_g': _jnp.float32, 'b_ln_b': _jnp.float32, 'b_pw2_w': _jnp.float32, 'b_pw2_b': _jnp.float32, 'c_in_w': _jnp.float32, 'c_in_b': _jnp.float32, 'c_conv_w': _jnp.float32, 'c_conv_b': _jnp.float32, 'c_ga_w': _jnp.float32, 'c_ga_b': _jnp.float32, 'c_gx_w': _jnp.float32, 'c_gx_b': _jnp.float32, 'c_lambda': _jnp.float32, 'c_out_w': _jnp.float32, 'c_out_b': _jnp.float32, 'd_in_w': _jnp.float32, 'd_in_b': _jnp.float32, 'd_ln_g': _jnp.float32, 'd_ln_b': _jnp.float32, 'd_sp_w': _jnp.float32, 'd_sp_b': _jnp.float32, 'd_out_w': _jnp.float32, 'd_out_b': _jnp.float32, 'f_up_w': _jnp.float32, 'f_conv_w': _jnp.float32, 'f_conv_b': _jnp.float32, 'f_down_w': _jnp.float32}
MOMENT_SCALE = {'norm_mix': 2.170527e-01, 'norm_ffn': 1.418791e-01, 'norm_final': 6.416752e+01, 'a_in_proj': 1.454025e-01, 'a_conv_w': 1.279374e-01, 'a_conv_b': 2.030591e-01, 'a_dt_bias': 4.370241e-01, 'a_log': 4.198384e-01, 'a_d_skip': 1.326938e+00, 'a_norm': 1.855934e-01, 'a_out_proj': 2.582355e-01, 'b_pw1_w': 9.025664e-02, 'b_pw1_b': 1.869056e-01, 'b_dw_w': 1.235554e-01, 'b_dw_b': 4.489961e-01, 'b_ln_g': 2.071356e-01, 'b_ln_b': 2.794236e-01, 'b_pw2_w': 1.454217e-01, 'b_pw2_b': 5.072747e-01, 'c_in_w': 1.319049e-01, 'c_in_b': 5.290294e-01, 'c_conv_w': 1.348887e-01, 'c_conv_b': 7.284237e-01, 'c_ga_w': 1.689462e-02, 'c_ga_b': 3.128602e-02, 'c_gx_w': 3.212975e-02, 'c_gx_b': 7.050755e-02, 'c_lambda': 8.533143e-02, 'c_out_w': 1.521020e-01, 'c_out_b': 2.043162e-01, 'd_in_w': 6.048343e-02, 'd_in_b': 6.827009e-02, 'd_ln_g': 4.074085e-02, 'd_ln_b': 3.954296e-02, 'd_sp_w': 5.687600e-02, 'd_sp_b': 7.950067e-02, 'd_out_w': 1.027763e-01, 'd_out_b': 1.914690e-01, 'f_up_w': 6.007786e-02, 'f_conv_w': 6.023963e-02, 'f_conv_b': 7.280286e-02, 'f_down_w': 9.805405e-02}


def _to_microbatches(a, axis):
    t = _jnp.moveaxis(a, axis, 0)
    t = t.reshape((N_MICROBATCH, t.shape[0] // N_MICROBATCH) + t.shape[1:])
    return _jnp.moveaxis(t, 1, axis + 1)


def setup_inputs(seed: int = 0) -> dict:
    inp = _fwd_setup_inputs(seed)
    key = _jax.random.fold_in(_jax.random.key(seed), 7919)
    shape, _ = _output_shape()
    out = dict(inp)
    out["loss_target"] = _jax.random.normal(_jax.random.fold_in(key, 0), shape, _jnp.float32)
    for i, name in enumerate(TWIN_WEIGHTS):
        w = inp[name].astype(_jnp.float32)
        if MOMENT_SCALE is None:
            s = _jnp.sqrt(_jnp.mean(_jnp.square(w)) + 1e-30)
        else:
            s = MOMENT_SCALE[name]
        km, kv = _jax.random.split(_jax.random.fold_in(key, i + 1))
        out[name] = w
        out["m_" + name] = s * _jax.random.normal(km, w.shape, _jnp.float32)
        out["v_" + name] = (s * s) * _jax.random.uniform(kv, w.shape, _jnp.float32, 0.5, 1.5)
    if N_MICROBATCH > 1:
        for name, axis in PER_EXAMPLE_BATCH_AXIS.items():
            out[name] = _to_microbatches(out[name], axis)
    return {'x': out['x'], 'norm_mix': out['norm_mix'], 'norm_ffn': out['norm_ffn'], 'norm_final': out['norm_final'], 'a_in_proj': out['a_in_proj'], 'a_conv_w': out['a_conv_w'], 'a_conv_b': out['a_conv_b'], 'a_dt_bias': out['a_dt_bias'], 'a_log': out['a_log'], 'a_d_skip': out['a_d_skip'], 'a_norm': out['a_norm'], 'a_out_proj': out['a_out_proj'], 'b_pw1_w': out['b_pw1_w'], 'b_pw1_b': out['b_pw1_b'], 'b_dw_w': out['b_dw_w'], 'b_dw_b': out['b_dw_b'], 'b_ln_g': out['b_ln_g'], 'b_ln_b': out['b_ln_b'], 'b_pw2_w': out['b_pw2_w'], 'b_pw2_b': out['b_pw2_b'], 'c_in_w': out['c_in_w'], 'c_in_b': out['c_in_b'], 'c_conv_w': out['c_conv_w'], 'c_conv_b': out['c_conv_b'], 'c_ga_w': out['c_ga_w'], 'c_ga_b': out['c_ga_b'], 'c_gx_w': out['c_gx_w'], 'c_gx_b': out['c_gx_b'], 'c_lambda': out['c_lambda'], 'c_out_w': out['c_out_w'], 'c_out_b': out['c_out_b'], 'd_in_w': out['d_in_w'], 'd_in_b': out['d_in_b'], 'd_ln_g': out['d_ln_g'], 'd_ln_b': out['d_ln_b'], 'd_sp_w': out['d_sp_w'], 'd_sp_b': out['d_sp_b'], 'd_out_w': out['d_out_w'], 'd_out_b': out['d_out_b'], 'f_up_w': out['f_up_w'], 'f_conv_w': out['f_conv_w'], 'f_conv_b': out['f_conv_b'], 'f_down_w': out['f_down_w'], 'loss_target': out['loss_target'], 'm_norm_mix': out['m_norm_mix'], 'm_norm_ffn': out['m_norm_ffn'], 'm_norm_final': out['m_norm_final'], 'm_a_in_proj': out['m_a_in_proj'], 'm_a_conv_w': out['m_a_conv_w'], 'm_a_conv_b': out['m_a_conv_b'], 'm_a_dt_bias': out['m_a_dt_bias'], 'm_a_log': out['m_a_log'], 'm_a_d_skip': out['m_a_d_skip'], 'm_a_norm': out['m_a_norm'], 'm_a_out_proj': out['m_a_out_proj'], 'm_b_pw1_w': out['m_b_pw1_w'], 'm_b_pw1_b': out['m_b_pw1_b'], 'm_b_dw_w': out['m_b_dw_w'], 'm_b_dw_b': out['m_b_dw_b'], 'm_b_ln_g': out['m_b_ln_g'], 'm_b_ln_b': out['m_b_ln_b'], 'm_b_pw2_w': out['m_b_pw2_w'], 'm_b_pw2_b': out['m_b_pw2_b'], 'm_c_in_w': out['m_c_in_w'], 'm_c_in_b': out['m_c_in_b'], 'm_c_conv_w': out['m_c_conv_w'], 'm_c_conv_b': out['m_c_conv_b'], 'm_c_ga_w': out['m_c_ga_w'], 'm_c_ga_b': out['m_c_ga_b'], 'm_c_gx_w': out['m_c_gx_w'], 'm_c_gx_b': out['m_c_gx_b'], 'm_c_lambda': out['m_c_lambda'], 'm_c_out_w': out['m_c_out_w'], 'm_c_out_b': out['m_c_out_b'], 'm_d_in_w': out['m_d_in_w'], 'm_d_in_b': out['m_d_in_b'], 'm_d_ln_g': out['m_d_ln_g'], 'm_d_ln_b': out['m_d_ln_b'], 'm_d_sp_w': out['m_d_sp_w'], 'm_d_sp_b': out['m_d_sp_b'], 'm_d_out_w': out['m_d_out_w'], 'm_d_out_b': out['m_d_out_b'], 'm_f_up_w': out['m_f_up_w'], 'm_f_conv_w': out['m_f_conv_w'], 'm_f_conv_b': out['m_f_conv_b'], 'm_f_down_w': out['m_f_down_w'], 'v_norm_mix': out['v_norm_mix'], 'v_norm_ffn': out['v_norm_ffn'], 'v_norm_final': out['v_norm_final'], 'v_a_in_proj': out['v_a_in_proj'], 'v_a_conv_w': out['v_a_conv_w'], 'v_a_conv_b': out['v_a_conv_b'], 'v_a_dt_bias': out['v_a_dt_bias'], 'v_a_log': out['v_a_log'], 'v_a_d_skip': out['v_a_d_skip'], 'v_a_norm': out['v_a_norm'], 'v_a_out_proj': out['v_a_out_proj'], 'v_b_pw1_w': out['v_b_pw1_w'], 'v_b_pw1_b': out['v_b_pw1_b'], 'v_b_dw_w': out['v_b_dw_w'], 'v_b_dw_b': out['v_b_dw_b'], 'v_b_ln_g': out['v_b_ln_g'], 'v_b_ln_b': out['v_b_ln_b'], 'v_b_pw2_w': out['v_b_pw2_w'], 'v_b_pw2_b': out['v_b_pw2_b'], 'v_c_in_w': out['v_c_in_w'], 'v_c_in_b': out['v_c_in_b'], 'v_c_conv_w': out['v_c_conv_w'], 'v_c_conv_b': out['v_c_conv_b'], 'v_c_ga_w': out['v_c_ga_w'], 'v_c_ga_b': out['v_c_ga_b'], 'v_c_gx_w': out['v_c_gx_w'], 'v_c_gx_b': out['v_c_gx_b'], 'v_c_lambda': out['v_c_lambda'], 'v_c_out_w': out['v_c_out_w'], 'v_c_out_b': out['v_c_out_b'], 'v_d_in_w': out['v_d_in_w'], 'v_d_in_b': out['v_d_in_b'], 'v_d_ln_g': out['v_d_ln_g'], 'v_d_ln_b': out['v_d_ln_b'], 'v_d_sp_w': out['v_d_sp_w'], 'v_d_sp_b': out['v_d_sp_b'], 'v_d_out_w': out['v_d_out_w'], 'v_d_out_b': out['v_d_out_b'], 'v_f_up_w': out['v_f_up_w'], 'v_f_conv_w': out['v_f_conv_w'], 'v_f_conv_b': out['v_f_conv_b'], 'v_f_down_w': out['v_f_down_w']}


def _loss(weights, diff, rest, loss_target):
    with _jax.named_scope("forward"):
        args = {**rest, TWIN_DIFF_INPUT: diff, **{k: w.astype(_WEIGHT_DTYPES[k]) for k, w in weights.items()}}
        y = _forward(args)
    with _jax.named_scope("loss_head"):
        err = _jnp.square(y.astype(_jnp.float32) - loss_target)
        return 0.5 * _jnp.sum(_jnp.mean(err, axis=-1)) if err.ndim else 0.5 * err


def _adamw(w, g, m, v):
    m = ADAM_B1 * m + (1.0 - ADAM_B1) * g
    v = ADAM_B2 * v + (1.0 - ADAM_B2) * _jnp.square(g)
    m_hat = m / (1.0 - ADAM_B1 ** ADAM_STEP)
    v_hat = v / (1.0 - ADAM_B2 ** ADAM_STEP)
    delta = -ADAM_LR * (m_hat / (_jnp.sqrt(v_hat) + ADAM_EPS) + ADAM_WD * w)
    return delta, m, v


def reference(x, norm_mix, norm_ffn, norm_final, a_in_proj, a_conv_w, a_conv_b, a_dt_bias, a_log, a_d_skip, a_norm, a_out_proj, b_pw1_w, b_pw1_b, b_dw_w, b_dw_b, b_ln_g, b_ln_b, b_pw2_w, b_pw2_b, c_in_w, c_in_b, c_conv_w, c_conv_b, c_ga_w, c_ga_b, c_gx_w, c_gx_b, c_lambda, c_out_w, c_out_b, d_in_w, d_in_b, d_ln_g, d_ln_b, d_sp_w, d_sp_b, d_out_w, d_out_b, f_up_w, f_conv_w, f_conv_b, f_down_w, loss_target, m_norm_mix, m_norm_ffn, m_norm_final, m_a_in_proj, m_a_conv_w, m_a_conv_b, m_a_dt_bias, m_a_log, m_a_d_skip, m_a_norm, m_a_out_proj, m_b_pw1_w, m_b_pw1_b, m_b_dw_w, m_b_dw_b, m_b_ln_g, m_b_ln_b, m_b_pw2_w, m_b_pw2_b, m_c_in_w, m_c_in_b, m_c_conv_w, m_c_conv_b, m_c_ga_w, m_c_ga_b, m_c_gx_w, m_c_gx_b, m_c_lambda, m_c_out_w, m_c_out_b, m_d_in_w, m_d_in_b, m_d_ln_g, m_d_ln_b, m_d_sp_w, m_d_sp_b, m_d_out_w, m_d_out_b, m_f_up_w, m_f_conv_w, m_f_conv_b, m_f_down_w, v_norm_mix, v_norm_ffn, v_norm_final, v_a_in_proj, v_a_conv_w, v_a_conv_b, v_a_dt_bias, v_a_log, v_a_d_skip, v_a_norm, v_a_out_proj, v_b_pw1_w, v_b_pw1_b, v_b_dw_w, v_b_dw_b, v_b_ln_g, v_b_ln_b, v_b_pw2_w, v_b_pw2_b, v_c_in_w, v_c_in_b, v_c_conv_w, v_c_conv_b, v_c_ga_w, v_c_ga_b, v_c_gx_w, v_c_gx_b, v_c_lambda, v_c_out_w, v_c_out_b, v_d_in_w, v_d_in_b, v_d_ln_g, v_d_ln_b, v_d_sp_w, v_d_sp_b, v_d_out_w, v_d_out_b, v_f_up_w, v_f_conv_w, v_f_conv_b, v_f_down_w):
    given = dict(x=x, norm_mix=norm_mix, norm_ffn=norm_ffn, norm_final=norm_final, a_in_proj=a_in_proj, a_conv_w=a_conv_w, a_conv_b=a_conv_b, a_dt_bias=a_dt_bias, a_log=a_log, a_d_skip=a_d_skip, a_norm=a_norm, a_out_proj=a_out_proj, b_pw1_w=b_pw1_w, b_pw1_b=b_pw1_b, b_dw_w=b_dw_w, b_dw_b=b_dw_b, b_ln_g=b_ln_g, b_ln_b=b_ln_b, b_pw2_w=b_pw2_w, b_pw2_b=b_pw2_b, c_in_w=c_in_w, c_in_b=c_in_b, c_conv_w=c_conv_w, c_conv_b=c_conv_b, c_ga_w=c_ga_w, c_ga_b=c_ga_b, c_gx_w=c_gx_w, c_gx_b=c_gx_b, c_lambda=c_lambda, c_out_w=c_out_w, c_out_b=c_out_b, d_in_w=d_in_w, d_in_b=d_in_b, d_ln_g=d_ln_g, d_ln_b=d_ln_b, d_sp_w=d_sp_w, d_sp_b=d_sp_b, d_out_w=d_out_w, d_out_b=d_out_b, f_up_w=f_up_w, f_conv_w=f_conv_w, f_conv_b=f_conv_b, f_down_w=f_down_w, loss_target=loss_target, m_norm_mix=m_norm_mix, m_norm_ffn=m_norm_ffn, m_norm_final=m_norm_final, m_a_in_proj=m_a_in_proj, m_a_conv_w=m_a_conv_w, m_a_conv_b=m_a_conv_b, m_a_dt_bias=m_a_dt_bias, m_a_log=m_a_log, m_a_d_skip=m_a_d_skip, m_a_norm=m_a_norm, m_a_out_proj=m_a_out_proj, m_b_pw1_w=m_b_pw1_w, m_b_pw1_b=m_b_pw1_b, m_b_dw_w=m_b_dw_w, m_b_dw_b=m_b_dw_b, m_b_ln_g=m_b_ln_g, m_b_ln_b=m_b_ln_b, m_b_pw2_w=m_b_pw2_w, m_b_pw2_b=m_b_pw2_b, m_c_in_w=m_c_in_w, m_c_in_b=m_c_in_b, m_c_conv_w=m_c_conv_w, m_c_conv_b=m_c_conv_b, m_c_ga_w=m_c_ga_w, m_c_ga_b=m_c_ga_b, m_c_gx_w=m_c_gx_w, m_c_gx_b=m_c_gx_b, m_c_lambda=m_c_lambda, m_c_out_w=m_c_out_w, m_c_out_b=m_c_out_b, m_d_in_w=m_d_in_w, m_d_in_b=m_d_in_b, m_d_ln_g=m_d_ln_g, m_d_ln_b=m_d_ln_b, m_d_sp_w=m_d_sp_w, m_d_sp_b=m_d_sp_b, m_d_out_w=m_d_out_w, m_d_out_b=m_d_out_b, m_f_up_w=m_f_up_w, m_f_conv_w=m_f_conv_w, m_f_conv_b=m_f_conv_b, m_f_down_w=m_f_down_w, v_norm_mix=v_norm_mix, v_norm_ffn=v_norm_ffn, v_norm_final=v_norm_final, v_a_in_proj=v_a_in_proj, v_a_conv_w=v_a_conv_w, v_a_conv_b=v_a_conv_b, v_a_dt_bias=v_a_dt_bias, v_a_log=v_a_log, v_a_d_skip=v_a_d_skip, v_a_norm=v_a_norm, v_a_out_proj=v_a_out_proj, v_b_pw1_w=v_b_pw1_w, v_b_pw1_b=v_b_pw1_b, v_b_dw_w=v_b_dw_w, v_b_dw_b=v_b_dw_b, v_b_ln_g=v_b_ln_g, v_b_ln_b=v_b_ln_b, v_b_pw2_w=v_b_pw2_w, v_b_pw2_b=v_b_pw2_b, v_c_in_w=v_c_in_w, v_c_in_b=v_c_in_b, v_c_conv_w=v_c_conv_w, v_c_conv_b=v_c_conv_b, v_c_ga_w=v_c_ga_w, v_c_ga_b=v_c_ga_b, v_c_gx_w=v_c_gx_w, v_c_gx_b=v_c_gx_b, v_c_lambda=v_c_lambda, v_c_out_w=v_c_out_w, v_c_out_b=v_c_out_b, v_d_in_w=v_d_in_w, v_d_in_b=v_d_in_b, v_d_ln_g=v_d_ln_g, v_d_ln_b=v_d_ln_b, v_d_sp_w=v_d_sp_w, v_d_sp_b=v_d_sp_b, v_d_out_w=v_d_out_w, v_d_out_b=v_d_out_b, v_f_up_w=v_f_up_w, v_f_conv_w=v_f_conv_w, v_f_conv_b=v_f_conv_b, v_f_down_w=v_f_down_w)
    weights = {n: given[n] for n in TWIN_WEIGHTS}
    shared = {n: given[n] for n in SHARED_INPUTS}
    per_example = {n: given[n] for n in ['x']}
    grad_fn = _jax.value_and_grad(_loss, argnums=(0, 1))

    def one_microbatch(ex, loss_target):
        ex = dict(ex)
        diff = ex.pop(TWIN_DIFF_INPUT)
        return grad_fn(weights, diff, {**shared, **ex}, loss_target)

    if N_MICROBATCH == 1:
        loss, (grad_w, grad_x) = one_microbatch(per_example, given["loss_target"])
    else:
        def body(carry, xs):
            loss_sum, grad_sum = carry
            l_k, (gw_k, gx_k) = one_microbatch(xs[0], xs[1])
            with _jax.named_scope("update"):
                return (loss_sum + l_k, _jax.tree.map(_jnp.add, grad_sum, gw_k)), gx_k

        init = (_jnp.zeros((), _jnp.float32), _jax.tree.map(_jnp.zeros_like, weights))
        (loss, grad_w), grad_x = _jax.lax.scan(body, init, (per_example, given["loss_target"]))
    with _jax.named_scope("update"):
        delta_w, new_m, new_v = {}, {}, {}
        for n in TWIN_WEIGHTS:
            delta_w[n], new_m[n], new_v[n] = _adamw(weights[n], grad_w[n], given["m_" + n], given["v_" + n])
    return (loss, grad_x, *[grad_w[n] for n in TWIN_WEIGHTS], *[delta_w[n] for n in TWIN_WEIGHTS],
            *[new_m[n] for n in TWIN_WEIGHTS], *[new_v[n] for n in TWIN_WEIGHTS])
```

```python
import functools
import math

import jax
import jax.numpy as jnp
import numpy as np
from jax import lax
from jax.experimental import pallas as pl
from jax.experimental.pallas import tpu as pltpu

f32 = jnp.float32
bf16 = jnp.bfloat16
MESH = pl.DeviceIdType.MESH
HIGHEST = lax.Precision.HIGHEST

D_MODEL = 1024
DEPTH = 4
RMS_EPS = 1e-6
LN_EPS = 1e-5
SSD_D_INNER = 2048
SSD_HEADS = 32
SSD_BC = 1024
SSD_CONV_DIM = 4096
SSD_CHUNK = 128
SSD_GROUPS = 8
LRU_W = 1280
LRU_BLOCK = 256
LRU_C = 8.0
SGU_HALF = 2048
SGU_GROUPS = 8
SGU_CHUNK = 128
FFN_H = 2816
ADAM_LR, ADAM_B1, ADAM_B2, ADAM_EPS, ADAM_WD, ADAM_STEP = 0.001, 0.9, 0.999, 1e-08, 0.01, 10

LANE = 128
SUBLANE = 8
VMEM_LIMIT = 56 * 1024 * 1024
FLAT_COLS = 1024

WEIGHTS = ['norm_mix', 'norm_ffn', 'norm_final', 'a_in_proj', 'a_conv_w', 'a_conv_b', 'a_dt_bias', 'a_log',
           'a_d_skip', 'a_norm', 'a_out_proj', 'b_pw1_w', 'b_pw1_b', 'b_dw_w', 'b_dw_b', 'b_ln_g', 'b_ln_b',
           'b_pw2_w', 'b_pw2_b', 'c_in_w', 'c_in_b', 'c_conv_w', 'c_conv_b', 'c_ga_w', 'c_ga_b', 'c_gx_w',
           'c_gx_b', 'c_lambda', 'c_out_w', 'c_out_b', 'd_in_w', 'd_in_b', 'd_ln_g', 'd_ln_b', 'd_sp_w',
           'd_sp_b', 'd_out_w', 'd_out_b', 'f_up_w', 'f_conv_w', 'f_conv_b', 'f_down_w']
SHARD_AXIS = {
    'norm_mix': None, 'norm_ffn': None, 'norm_final': None, 'a_in_proj': 2, 'a_conv_w': 2, 'a_conv_b': None,
    'a_dt_bias': None, 'a_log': None, 'a_d_skip': None, 'a_norm': None, 'a_out_proj': 1, 'b_pw1_w': 2,
    'b_pw1_b': 1, 'b_dw_w': 2, 'b_dw_b': 1, 'b_ln_g': 1, 'b_ln_b': 1, 'b_pw2_w': 1, 'b_pw2_b': 1, 'c_in_w': 2,
    'c_in_b': 1, 'c_conv_w': 2, 'c_conv_b': 1, 'c_ga_w': 2, 'c_ga_b': 2, 'c_gx_w': 2, 'c_gx_b': 2,
    'c_lambda': 1, 'c_out_w': 1, 'c_out_b': 1, 'd_in_w': 2, 'd_in_b': 1, 'd_ln_g': 1, 'd_ln_b': 1,
    'd_sp_w': None, 'd_sp_b': None, 'd_out_w': 1, 'd_out_b': 1, 'f_up_w': 2, 'f_conv_w': 2, 'f_conv_b': None,
    'f_down_w': 1}
MATMUL_WEIGHTS = ['a_in_proj', 'a_out_proj', 'b_pw1_w', 'b_pw2_w', 'c_in_w', 'c_ga_w', 'c_gx_w', 'c_out_w',
                  'd_in_w', 'd_out_w', 'f_up_w', 'f_down_w']
SHARDED = [n for n in WEIGHTS if SHARD_AXIS[n] is not None]
SHARDED_VEC = [n for n in SHARDED if n not in MATMUL_WEIGHTS]
REPLICATED = [n for n in WEIGHTS if SHARD_AXIS[n] is None]
N_CHIPS = 4
N_DEV = 8


def _tile(n, cap, mult):
    if n <= cap:
        return n
    t = (cap // mult) * mult
    while t >= mult:
        if n % t == 0:
            return t
        t -= mult
    raise ValueError(f"no tile for {n} under {cap} in steps of {mult}")


def _cparams(sem=None):
    if sem is None:
        return pltpu.CompilerParams(vmem_limit_bytes=VMEM_LIMIT)
    return pltpu.CompilerParams(dimension_semantics=sem, vmem_limit_bytes=VMEM_LIMIT)


def _dg(a, b, ca, cb):
    return lax.dot_general(a.astype(bf16), b.astype(bf16), (((ca,), (cb,)), ((), ())), preferred_element_type=f32)


@jax.custom_vjp
def _dot_nn(a, b):
    return _dg(a, b, 1, 0)


def _dot_nn_fwd(a, b):
    return _dg(a, b, 1, 0), (a, b)


def _dot_nn_bwd(res, g):
    a, b = res
    return _dg(g, b, 1, 1).astype(a.dtype), _dg(a, g, 0, 0).astype(b.dtype)


_dot_nn.defvjp(_dot_nn_fwd, _dot_nn_bwd)


@jax.custom_vjp
def _dot_nt(a, b):
    return _dg(a, b, 1, 1)


def _dot_nt_fwd(a, b):
    return _dg(a, b, 1, 1), (a, b)


def _dot_nt_bwd(res, g):
    a, b = res
    return _dg(g, b, 1, 0).astype(a.dtype), _dg(g, a, 0, 0).astype(b.dtype)


_dot_nt.defvjp(_dot_nt_fwd, _dot_nt_bwd)


@jax.custom_vjp
def _dot_tn(a, b):
    return _dg(a, b, 0, 0)


def _dot_tn_fwd(a, b):
    return _dg(a, b, 0, 0), (a, b)


def _dot_tn_bwd(res, g):
    a, b = res
    return _dg(b, g, 1, 1).astype(a.dtype), _dg(a, g, 1, 0).astype(b.dtype)


_dot_tn.defvjp(_dot_tn_fwd, _dot_tn_bwd)


def _expm1(x):
    small = jnp.abs(x) < 0.03
    xs = jnp.where(small, x, 0.0)
    series = xs * (1.0 + xs * (0.5 + xs * (1.0 / 6.0 + xs * (1.0 / 24.0 + xs * (1.0 / 120.0)))))
    return jnp.where(small, series, jnp.exp(x) - 1.0)


def _rms(x, g):
    return x * lax.rsqrt(jnp.mean(x * x, axis=-1, keepdims=True) + RMS_EPS) * g


def _layer_norm(x, g, b):
    mu = jnp.mean(x, axis=-1, keepdims=True)
    xc = x - mu
    return xc * lax.rsqrt(jnp.mean(xc * xc, axis=-1, keepdims=True) + LN_EPS) * g + b


def _causal_taps(ext, w, halo, rows):
    k_taps = w.shape[0]
    acc = None
    for k in range(k_taps):
        lo = halo - (k_taps - 1) + k
        term = w[k:k + 1, :] * ext[lo:lo + rows, :]
        acc = term if acc is None else acc + term
    return acc


def _mm(a, b, mode, name, *, bias=None, add=None, out_dtype=f32, tm_cap=1024, tn_cap=1408, tk_cap=1024):
    if mode == 'nn':
        (m, k), (k2, n) = a.shape, b.shape
    elif mode == 'nt':
        (m, k), (n, k2) = a.shape, b.shape
    else:
        (k, m), (k2, n) = a.shape, b.shape
    assert k == k2, (name, a.shape, b.shape)
    tm = _tile(m, tm_cap, LANE if mode == 'tn' else SUBLANE)
    tn = _tile(n, tn_cap, LANE)
    tk = _tile(k, tk_cap, LANE if mode != 'tn' else SUBLANE)
    nk = k // tk
    if mode == 'nn':
        a_spec = pl.BlockSpec((tm, tk), lambda i, j, kk: (i, kk))
        b_spec = pl.BlockSpec((tk, tn), lambda i, j, kk: (kk, j))
        ca, cb = 1, 0
    elif mode == 'nt':
        a_spec = pl.BlockSpec((tm, tk), lambda i, j, kk: (i, kk))
        b_spec = pl.BlockSpec((tn, tk), lambda i, j, kk: (j, kk))
        ca, cb = 1, 1
    else:
        a_spec = pl.BlockSpec((tk, tm), lambda i, j, kk: (kk, i))
        b_spec = pl.BlockSpec((tk, tn), lambda i, j, kk: (kk, j))
        ca, cb = 0, 0
    in_specs, operands = [a_spec, b_spec], [a, b]
    if bias is not None:
        in_specs.append(pl.BlockSpec((1, tn), lambda i, j, kk: (0, j)))
        operands.append(bias)
    if add is not None:
        in_specs.append(pl.BlockSpec((tm, tn), lambda i, j, kk: (i, j)))
        operands.append(add)

    def body(*refs):
        a_ref, b_ref = refs[0], refs[1]
        pos = 2
        bias_ref = add_ref = None
        if bias is not None:
            bias_ref = refs[pos]
            pos += 1
        if add is not None:
            add_ref = refs[pos]
            pos += 1
        o_ref, acc_ref = refs[pos], refs[pos + 1]
        kk = pl.program_id(2)

        @pl.when(kk == 0)
        def _():
            acc_ref[...] = jnp.zeros_like(acc_ref)

        acc_ref[...] += _dg(a_ref[...], b_ref[...], ca, cb)

        @pl.when(kk == nk - 1)
        def _():
            r = acc_ref[...]
            if bias_ref is not None:
                r = r + bias_ref[...]
            if add_ref is not None:
                r = r + add_ref[...].astype(f32)
            o_ref[...] = r.astype(out_dtype)

    return pl.pallas_call(
        body, name=name, out_shape=jax.ShapeDtypeStruct((m, n), out_dtype),
        grid=(m // tm, n // tn, nk), in_specs=in_specs,
        out_specs=pl.BlockSpec((tm, tn), lambda i, j, kk: (i, j)),
        scratch_shapes=[pltpu.VMEM((tm, tn), f32)],
        compiler_params=_cparams(("parallel", "parallel", "arbitrary")),
    )(*operands)


def _row_specs(tiles, halo_of, rows, halo, n_tiles, reverse):
    def tile_index(i):
        return n_tiles - 1 - i if reverse else i

    specs, operands = [], []
    for arr, has_halo in zip(tiles, halo_of):
        cols = arr.shape[1]
        specs.append(pl.BlockSpec((rows, cols), lambda i: (tile_index(i), 0)))
        operands.append(arr)
        if has_halo:
            per = rows // halo
            specs.append(pl.BlockSpec((halo, cols), lambda i: (jnp.maximum(tile_index(i) * per - 1, 0), 0)))
            operands.append(arr)
    return specs, operands, tile_index


def _load_tiles(refs, halo_of, tile_id, rows, halo):
    vals, pos = [], 0
    for has_halo in halo_of:
        cur = refs[pos][...].astype(f32)
        pos += 1
        if has_halo:
            before = refs[pos][...].astype(f32)
            pos += 1
            before = jnp.where(tile_id > 0, before, jnp.zeros_like(before))
            cur = jnp.concatenate([before, cur], axis=0)
        vals.append(cur)
    return vals, pos


def _valid_rows(tile_id, rows, halo):
    r = lax.broadcasted_iota(jnp.int32, (halo + rows, 1), 0)
    return jnp.logical_or(r >= halo, tile_id > 0).astype(f32)


def _row_fwd(f, tiles, params, outs, *, rows, name, halo=0, halo_of=None):
    t_len = tiles[0].shape[0]
    rows = min(rows, t_len)
    n_tiles = t_len // rows
    halo_of = halo_of or [False] * len(tiles)
    specs, operands, _ = _row_specs(tiles, halo_of, rows, halo, n_tiles, False)
    for p in params:
        specs.append(pl.BlockSpec(p.shape, lambda i: (0, 0)))
        operands.append(p)

    def body(*refs):
        i = pl.program_id(0)
        vals, pos = _load_tiles(refs, halo_of, i, rows, halo)
        pvals = [refs[pos + j][...] for j in range(len(params))]
        pos += len(params)
        kw = {'valid': _valid_rows(i, rows, halo)} if halo else {}
        res = f(*vals, *pvals, **kw)
        for o_ref, o in zip(refs[pos:], res):
            o_ref[...] = o.astype(o_ref.dtype)

    return pl.pallas_call(
        body, name=name,
        out_shape=[jax.ShapeDtypeStruct((t_len, c), d) for c, d in outs],
        grid=(n_tiles,), in_specs=specs,
        out_specs=[pl.BlockSpec((rows, c), lambda i: (i, 0)) for c, _ in outs],
        compiler_params=_cparams(("parallel",)),
    )(*operands)


def _row_bwd(f, tiles, params, cots, *, rows, name, halo=0, halo_of=None, tile_dtypes=None):
    t_len = tiles[0].shape[0]
    rows = min(rows, t_len)
    n_tiles = t_len // rows
    halo_of = halo_of or [False] * len(tiles)
    tile_dtypes = tile_dtypes or [f32] * len(tiles)
    specs, operands, tile_index = _row_specs(tiles, halo_of, rows, halo, n_tiles, True)
    for p in params:
        specs.append(pl.BlockSpec(p.shape, lambda i: (0, 0)))
        operands.append(p)
    for ct in cots:
        specs.append(pl.BlockSpec((rows, ct.shape[1]), lambda i: (tile_index(i), 0)))
        operands.append(ct)
    n_t, n_p, n_c = len(tiles), len(params), len(cots)
    out_shape = [jax.ShapeDtypeStruct(t.shape, d) for t, d in zip(tiles, tile_dtypes)]
    out_shape += [jax.ShapeDtypeStruct(p.shape, f32) for p in params]
    out_specs = [pl.BlockSpec((rows, t.shape[1]), lambda i: (tile_index(i), 0)) for t in tiles]
    out_specs += [pl.BlockSpec(p.shape, lambda i: (0, 0)) for p in params]
    scratch = [pltpu.VMEM((halo, t.shape[1]), f32) for t, h in zip(tiles, halo_of) if h]

    def body(*refs):
        i = pl.program_id(0)
        tile_id = tile_index(i)
        vals, pos = _load_tiles(refs, halo_of, tile_id, rows, halo)
        pvals = [refs[pos + j][...] for j in range(n_p)]
        pos += n_p
        cvals = [refs[pos + j][...].astype(f32) for j in range(n_c)]
        pos += n_c
        d_tile_refs = refs[pos:pos + n_t]
        d_param_refs = refs[pos + n_t:pos + n_t + n_p]
        carries = list(refs[pos + n_t + n_p:])
        kw = {'valid': _valid_rows(tile_id, rows, halo)} if halo else {}
        _, vjp = jax.vjp(lambda *args: tuple(f(*args, **kw)), *vals, *pvals)
        grads = vjp(tuple(cvals))

        @pl.when(i == 0)
        def _():
            for cr in carries:
                cr[...] = jnp.zeros_like(cr)
            for dp in d_param_refs:
                dp[...] = jnp.zeros_like(dp)

        ci = 0
        for t in range(n_t):
            g = grads[t]
            if halo_of[t]:
                cr = carries[ci]
                ci += 1
                d_tile_refs[t][0:rows - halo, :] = g[halo:rows, :].astype(d_tile_refs[t].dtype)
                d_tile_refs[t][rows - halo:rows, :] = (g[rows:rows + halo, :] + cr[...]).astype(d_tile_refs[t].dtype)
                cr[...] = g[0:halo, :]
            else:
                d_tile_refs[t][...] = g.astype(d_tile_refs[t].dtype)
        for j in range(n_p):
            d_param_refs[j][...] += grads[n_t + j]

    return pl.pallas_call(
        body, name=name, out_shape=out_shape, grid=(n_tiles,), in_specs=specs, out_specs=out_specs,
        scratch_shapes=scratch, compiler_params=_cparams(("arbitrary",)),
    )(*operands)


def _f_rms(h, g):
    return (_rms(h, g),)


def _f_rms_res(h, g, bz):
    hh = h + bz
    return _rms(hh, g), hh


def _f_ffn_gate(a_ext, cw, cb, *, valid):
    rows = a_ext.shape[0] - SUBLANE
    hc = _causal_taps(a_ext, cw, SUBLANE, rows) + cb
    return (jax.nn.silu(hc[:, :FFN_H]) * hc[:, FFN_H:],)


def _f_ssd_pre(xbc_ext, dtr, cw, cb, dtb, *, valid):
    rows = dtr.shape[0]
    xc = jax.nn.silu(_causal_taps(xbc_ext, cw, SUBLANE, rows) + cb)
    real = lax.broadcasted_iota(jnp.int32, (1, LANE), 1) < SSD_HEADS
    dt = jnp.where(real, jax.nn.softplus(dtr + dtb), 0.0)
    return xc[:, :SSD_D_INNER], xc[:, SSD_D_INNER:SSD_D_INNER + SSD_BC], xc[:, SSD_D_INNER + SSD_BC:], dt


def _f_ssd_post(y, z, g):
    return (_rms(y * jax.nn.silu(z), g),)


CONF_HALO = 32


def _f_conf(g2_ext, b1, dw_w, dw_b, ln_g, ln_b, *, valid):
    rows = g2_ext.shape[0] - CONF_HALO
    g2 = g2_ext + b1
    glu = g2[:, :D_MODEL] * jax.nn.sigmoid(g2[:, D_MODEL:]) * valid
    conv = _causal_taps(glu, dw_w, CONF_HALO, rows) + dw_b
    return (jax.nn.silu(_layer_norm(conv, ln_g, ln_b)),)


def _f_lru(io_ext, in_b, cw, cb, ga_w, ga_b, gx_w, gx_b, lam, *, valid):
    rows = io_ext.shape[0] - SUBLANE
    io = (io_ext + in_b) * valid
    gate = io[SUBLANE:, :LRU_W]
    xr = _causal_taps(io[:, LRU_W:], cw, SUBLANE, rows) + cb
    rs, iis = [], []
    for blk in range(LRU_W // LRU_BLOCK):
        sl = slice(blk * LRU_BLOCK, (blk + 1) * LRU_BLOCK)
        xb = xr[:, sl]
        rs.append(jax.nn.sigmoid(_dot_nn(xb, ga_w[sl, :]) + ga_b[:, sl]))
        iis.append(jax.nn.sigmoid(_dot_nn(xb, gx_w[sl, :]) + gx_b[:, sl]))
    r = jnp.concatenate(rs, axis=1)
    ig = jnp.concatenate(iis, axis=1)
    log_a = -LRU_C * r * jax.nn.softplus(-lam)
    a = jnp.exp(log_a)
    bterm = jnp.sqrt(-_expm1(2.0 * log_a)) * (ig * xr)
    return a, bterm, jax.nn.gelu(gate)


def _f_sgu(z, in_b, ln_g, ln_b, sp_w, sp_bt):
    rows = z.shape[0]
    zz = jax.nn.gelu(z + in_b)
    u, v = zz[:, :SGU_HALF], zz[:, SGU_HALF:]
    v = _layer_norm(v, ln_g, ln_b)
    tri = lax.broadcasted_iota(jnp.int32, (SGU_CHUNK, SGU_CHUNK), 0) >= lax.broadcasted_iota(
        jnp.int32, (SGU_CHUNK, SGU_CHUNK), 1)
    gdim = SGU_HALF // SGU_GROUPS
    row_blocks = []
    for ci in range(rows // SGU_CHUNK):
        col_blocks = []
        for g in range(SGU_GROUPS):
            w = jnp.where(tri, sp_w[g * SGU_CHUNK:(g + 1) * SGU_CHUNK, :], 0.0)
            vb = v[ci * SGU_CHUNK:(ci + 1) * SGU_CHUNK, g * gdim:(g + 1) * gdim]
            col_blocks.append(_dot_nn(w, vb) + sp_bt[:, g:g + 1])
        row_blocks.append(jnp.concatenate(col_blocks, axis=1))
    mixed = row_blocks[0] if len(row_blocks) == 1 else jnp.concatenate(row_blocks, axis=0)
    return (u * mixed,)


HEADS_PER_GROUP = 4
GROUP_COLS = 256
HEAD_DIM = 64


def _ssd_group(x, bm, cm, dt, st, a_log, dsk, g):
    q = x.shape[0]
    tri = lax.broadcasted_iota(jnp.int32, (q, q), 0) >= lax.broadcasted_iota(jnp.int32, (q, q), 1)
    d_a = dt * (-jnp.exp(a_log))
    acs = jnp.dot(tri.astype(f32), d_a, precision=HIGHEST, preferred_element_type=f32)
    acs_t = acs.T
    lane = lax.broadcasted_iota(jnp.int32, (1, LANE), 1)
    sub = lax.broadcasted_iota(jnp.int32, (LANE, 1), 0)
    col_idx = lax.broadcasted_iota(jnp.int32, (1, GROUP_COLS), 1)
    last_row = (lax.broadcasted_iota(jnp.int32, (q, 1), 0) == q - 1).astype(f32)
    cb = _dot_nt(cm, bm)
    y = jnp.zeros((q, GROUP_COLS), f32)
    e_in = jnp.zeros((q, GROUP_COLS), f32)
    d_end = jnp.zeros((q, GROUP_COLS), f32)
    d_last = jnp.zeros((1, GROUP_COLS), f32)
    d_skip = jnp.zeros((1, GROUP_COLS), f32)
    for j in range(HEADS_PER_GROUP):
        head = HEADS_PER_GROUP * g + j
        on_lane = (lane == head).astype(f32)
        on_sub = (sub == head).astype(f32)
        col = jnp.sum(acs * on_lane, axis=1, keepdims=True)
        row = jnp.sum(acs_t * on_sub, axis=0, keepdims=True)
        dtc = jnp.sum(dt * on_lane, axis=1, keepdims=True)
        last = jnp.sum(col * last_row, axis=0, keepdims=True)
        dsk_j = jnp.sum(dsk * on_lane, axis=1, keepdims=True)
        decay = jnp.where(tri, jnp.exp(jnp.where(tri, col - row, 0.0)), 0.0)
        mine = jnp.logical_and(col_idx >= j * HEAD_DIM, col_idx < (j + 1) * HEAD_DIM)
        y = y + _dot_nn(cb * decay, jnp.where(mine, x * dtc, 0.0))
        e_in = e_in + jnp.where(mine, jnp.exp(col), 0.0)
        d_end = d_end + jnp.where(mine, jnp.exp(last - col) * dtc, 0.0)
        d_last = d_last + jnp.where(mine, jnp.exp(last), 0.0)
        d_skip = d_skip + jnp.where(mine, dsk_j, 0.0)
    y = y + _dot_nn(cm, st) * e_in + x * d_skip
    st_new = st * d_last + _dot_tn(bm, x * d_end)
    return y, st_new


def _ssd_specs(rev, nc):
    def ch(c):
        return nc - 1 - c if rev else c

    x_spec = pl.BlockSpec((SSD_CHUNK, GROUP_COLS), lambda c, g: (ch(c), g))
    bc_spec = pl.BlockSpec((SSD_CHUNK, LANE), lambda c, g: (ch(c), g))
    dt_spec = pl.BlockSpec((SSD_CHUNK, LANE), lambda c, g: (ch(c), 0))
    row_spec = pl.BlockSpec((1, LANE), lambda c, g: (0, 0))
    st_spec = pl.BlockSpec((1, 1, LANE, GROUP_COLS), lambda c, g: (ch(c), g, 0, 0))
    return x_spec, bc_spec, dt_spec, row_spec, st_spec


def _ssd_fwd(xs, bm, cm, dt, a_log, dsk):
    t_len = xs.shape[0]
    nc = t_len // SSD_CHUNK
    x_spec, bc_spec, dt_spec, row_spec, st_spec = _ssd_specs(False, nc)

    def body(x_ref, b_ref, c_ref, dt_ref, al_ref, dk_ref, y_ref, st_out_ref, st_ref):
        c, g = pl.program_id(0), pl.program_id(1)

        @pl.when(c == 0)
        def _():
            st_ref[g] = jnp.zeros((LANE, GROUP_COLS), f32)

        st = st_ref[g]
        st_out_ref[0, 0] = st
        y, st_new = _ssd_group(x_ref[...], b_ref[...], c_ref[...], dt_ref[...], st, al_ref[...], dk_ref[...], g)
        y_ref[...] = y
        st_ref[g] = st_new

    return pl.pallas_call(
        body, name="ssd_scan_fwd",
        out_shape=[jax.ShapeDtypeStruct((t_len, SSD_D_INNER), f32),
                   jax.ShapeDtypeStruct((nc, SSD_GROUPS, LANE, GROUP_COLS), f32)],
        grid=(nc, SSD_GROUPS), in_specs=[x_spec, bc_spec, bc_spec, dt_spec, row_spec, row_spec],
        out_specs=[x_spec, st_spec],
        scratch_shapes=[pltpu.VMEM((SSD_GROUPS, LANE, GROUP_COLS), f32)],
        compiler_params=_cparams(("arbitrary", "arbitrary")),
    )(xs, bm, cm, dt, a_log, dsk)


def _ssd_bwd(xs, bm, cm, dt, a_log, dsk, states, dy):
    t_len = xs.shape[0]
    nc = t_len // SSD_CHUNK
    x_spec, bc_spec, dt_spec, row_spec, st_spec = _ssd_specs(True, nc)

    def body(x_ref, b_ref, c_ref, dt_ref, al_ref, dk_ref, st_in_ref, dy_ref,
             dx_ref, db_ref, dc_ref, ddt_ref, dal_ref, ddk_ref, dst_ref):
        c, g = pl.program_id(0), pl.program_id(1)

        @pl.when(c == 0)
        def _():
            dst_ref[g] = jnp.zeros((LANE, GROUP_COLS), f32)

        @pl.when(jnp.logical_and(c == 0, g == 0))
        def _():
            dal_ref[...] = jnp.zeros_like(dal_ref)
            ddk_ref[...] = jnp.zeros_like(ddk_ref)

        @pl.when(g == 0)
        def _():
            ddt_ref[...] = jnp.zeros_like(ddt_ref)

        _, vjp = jax.vjp(lambda *args: _ssd_group(*args, g), x_ref[...], b_ref[...], c_ref[...], dt_ref[...],
                         st_in_ref[0, 0], al_ref[...], dk_ref[...])
        dx, db, dc, ddt, dst, dal, ddk = vjp((dy_ref[...], dst_ref[g]))
        dx_ref[...] = dx
        db_ref[...] = db
        dc_ref[...] = dc
        ddt_ref[...] += ddt
        dst_ref[g] = dst
        dal_ref[...] += dal
        ddk_ref[...] += ddk

    return pl.pallas_call(
        body, name="ssd_scan_bwd",
        out_shape=[jax.ShapeDtypeStruct((t_len, SSD_D_INNER), f32), jax.ShapeDtypeStruct((t_len, SSD_BC), f32),
                   jax.ShapeDtypeStruct((t_len, SSD_BC), f32), jax.ShapeDtypeStruct((t_len, LANE), f32),
                   jax.ShapeDtypeStruct((1, LANE), f32), jax.ShapeDtypeStruct((1, LANE), f32)],
        grid=(nc, SSD_GROUPS),
        in_specs=[x_spec, bc_spec, bc_spec, dt_spec, row_spec, row_spec, st_spec, x_spec],
        out_specs=[x_spec, bc_spec, bc_spec, dt_spec, row_spec, row_spec],
        scratch_shapes=[pltpu.VMEM((SSD_GROUPS, LANE, GROUP_COLS), f32)],
        compiler_params=_cparams(("arbitrary", "arbitrary")),
    )(xs, bm, cm, dt, a_log, dsk, states, dy)


LRU_ROWS = 256


def _lru_fwd(a, b, gg):
    t_len, cols = a.shape
    rows = min(LRU_ROWS, t_len)
    spec = pl.BlockSpec((rows, cols), lambda i: (i, 0))

    def body(a_ref, b_ref, g_ref, y_ref, h_ref, carry):
        i = pl.program_id(0)

        @pl.when(i == 0)
        def _():
            carry[...] = jnp.zeros_like(carry)

        av, bv = a_ref[...], b_ref[...]
        row = lax.broadcasted_iota(jnp.int32, av.shape, 0)
        s = 1
        while s < rows:
            a_prev = pltpu.roll(av, s, axis=0)
            b_prev = pltpu.roll(bv, s, axis=0)
            m = row >= s
            bv = jnp.where(m, av * b_prev + bv, bv)
            av = jnp.where(m, av * a_prev, av)
            s *= 2
        h = av * carry[0:1, :] + bv
        h_ref[...] = h
        y_ref[...] = g_ref[...] * h
        carry[0:1, :] = h[rows - 1:rows, :]

    return pl.pallas_call(
        body, name="lru_scan_fwd",
        out_shape=[jax.ShapeDtypeStruct((t_len, cols), f32), jax.ShapeDtypeStruct((t_len, cols), f32)],
        grid=(t_len // rows,), in_specs=[spec, spec, spec], out_specs=[spec, spec],
        scratch_shapes=[pltpu.VMEM((SUBLANE, cols), f32)],
        compiler_params=_cparams(("arbitrary",)),
    )(a, b, gg)


def _lru_bwd(dy, gg, a, h):
    t_len, cols = a.shape
    rows = min(LRU_ROWS, t_len)
    n_tiles = t_len // rows
    per = rows // SUBLANE
    spec = pl.BlockSpec((rows, cols), lambda i: (n_tiles - 1 - i, 0))
    prev_spec = pl.BlockSpec((SUBLANE, cols), lambda i: (jnp.maximum((n_tiles - 1 - i) * per - 1, 0), 0))

    def body(dy_ref, g_ref, a_ref, h_ref, hp_ref, da_ref, db_ref, dg_ref, carry_dh, carry_a):
        i = pl.program_id(0)
        tile_id = n_tiles - 1 - i

        @pl.when(i == 0)
        def _():
            carry_dh[...] = jnp.zeros_like(carry_dh)
            carry_a[...] = jnp.zeros_like(carry_a)

        av, hv, dyv = a_ref[...], h_ref[...], dy_ref[...]
        row = lax.broadcasted_iota(jnp.int32, av.shape, 0)
        dg_ref[...] = dyv * hv
        bv = dyv * g_ref[...]
        cv = jnp.where(row == rows - 1, carry_a[0:1, :], pltpu.roll(av, rows - 1, axis=0))
        s = 1
        while s < rows:
            c_next = pltpu.roll(cv, rows - s, axis=0)
            b_next = pltpu.roll(bv, rows - s, axis=0)
            m = row < rows - s
            bv = jnp.where(m, cv * b_next + bv, bv)
            cv = jnp.where(m, cv * c_next, cv)
            s *= 2
        dh = cv * carry_dh[0:1, :] + bv
        h_before = jnp.where(tile_id > 0, hp_ref[SUBLANE - 1:SUBLANE, :], jnp.zeros((1, cols), f32))
        h_prev = jnp.where(row == 0, h_before, pltpu.roll(hv, 1, axis=0))
        da_ref[...] = dh * h_prev
        db_ref[...] = dh
        carry_dh[0:1, :] = dh[0:1, :]
        carry_a[0:1, :] = av[0:1, :]

    return pl.pallas_call(
        body, name="lru_scan_bwd",
        out_shape=[jax.ShapeDtypeStruct((t_len, cols), f32)] * 3,
        grid=(n_tiles,), in_specs=[spec, spec, spec, spec, prev_spec], out_specs=[spec, spec, spec],
        scratch_shapes=[pltpu.VMEM((SUBLANE, cols), f32), pltpu.VMEM((SUBLANE, cols), f32)],
        compiler_params=_cparams(("arbitrary",)),
    )(dy, gg, a, h, h)


def _loss_head(h, target, g):
    t_len = h.shape[0]
    rows = min(512, t_len)

    def f(hv, gv, tv):
        err = _rms(hv, gv) - tv
        return 0.5 * jnp.sum(jnp.mean(err * err, axis=-1, keepdims=True), axis=0, keepdims=True)

    def body(h_ref, t_ref, g_ref, dh_ref, dg_ref, loss_ref):
        i = pl.program_id(0)

        @pl.when(i == 0)
        def _():
            dg_ref[...] = jnp.zeros_like(dg_ref)
            loss_ref[...] = jnp.zeros_like(loss_ref)

        tv = t_ref[...]
        part, vjp = jax.vjp(lambda hv, gv: f(hv, gv, tv), h_ref[...], g_ref[...])
        dh, dg = vjp(jnp.ones((1, 1), f32))
        dh_ref[...] = dh
        dg_ref[...] += dg
        loss_ref[...] += jnp.broadcast_to(part, loss_ref.shape)

    spec = pl.BlockSpec((rows, D_MODEL), lambda i: (i, 0))
    return pl.pallas_call(
        body, name="loss_head",
        out_shape=[jax.ShapeDtypeStruct((t_len, D_MODEL), f32), jax.ShapeDtypeStruct((1, D_MODEL), f32),
                   jax.ShapeDtypeStruct((1, LANE), f32)],
        grid=(t_len // rows,), in_specs=[spec, spec, pl.BlockSpec((1, D_MODEL), lambda i: (0, 0))],
        out_specs=[spec, pl.BlockSpec((1, D_MODEL), lambda i: (0, 0)), pl.BlockSpec((1, LANE), lambda i: (0, 0))],
        compiler_params=_cparams(("arbitrary",)),
    )(h, target, g)


def _as2d(a):
    return a.reshape((-1, a.shape[-1])) if a.ndim > 1 else a.reshape((1, -1))


def _row_block(rows, cols, bytes_cap=1 << 20):
    if rows * cols * 4 <= bytes_cap or rows % SUBLANE:
        return rows
    return _tile(rows, max(SUBLANE, (bytes_cap // (cols * 4)) // SUBLANE * SUBLANE), SUBLANE)


def _adamw(w, g, m, v, name):
    shape = w.shape
    w2, g2, m2, v2 = _as2d(w), _as2d(g), _as2d(m), _as2d(v)
    rows, cols = w2.shape
    rb = _row_block(rows, cols)

    def body(w_ref, g_ref, m_ref, v_ref, d_ref, nm_ref, nv_ref):
        gv = g_ref[...]
        nm = ADAM_B1 * m_ref[...] + (1.0 - ADAM_B1) * gv
        nv = ADAM_B2 * v_ref[...] + (1.0 - ADAM_B2) * jnp.square(gv)
        m_hat = nm / (1.0 - ADAM_B1 ** ADAM_STEP)
        v_hat = nv / (1.0 - ADAM_B2 ** ADAM_STEP)
        d_ref[...] = -ADAM_LR * (m_hat / (jnp.sqrt(v_hat) + ADAM_EPS) + ADAM_WD * w_ref[...])
        nm_ref[...] = nm
        nv_ref[...] = nv

    spec = pl.BlockSpec((rb, cols), lambda i: (i, 0))
    d, nm, nv = pl.pallas_call(
        body, name=name, out_shape=[jax.ShapeDtypeStruct((rows, cols), f32)] * 3,
        grid=(rows // rb,), in_specs=[spec] * 4, out_specs=[spec] * 3,
        compiler_params=_cparams(("parallel",)),
    )(w2, g2, m2, v2)
    return d.reshape(shape), nm.reshape(shape), nv.reshape(shape)


def _sum_with_sibling(g_halves, theirs, c_idx):
    _, n_sh, rows, cols = g_halves.shape
    rb = _tile(rows, 512, SUBLANE)

    def body(c_ref, mine_ref, theirs_ref, o_ref):
        o_ref[...] = (mine_ref[...] + theirs_ref[...]).astype(bf16)

    grid_spec = pltpu.PrefetchScalarGridSpec(
        num_scalar_prefetch=1, grid=(n_sh, rows // rb),
        in_specs=[pl.BlockSpec((None, None, rb, cols), lambda k, i, c_ref: (c_ref[0], k, i, 0)),
                  pl.BlockSpec((None, rb, cols), lambda k, i, c_ref: (k, i, 0))],
        out_specs=pl.BlockSpec((None, rb, cols), lambda k, i, c_ref: (k, i, 0)))
    return pl.pallas_call(
        body, name="grad_sum_sibling", out_shape=jax.ShapeDtypeStruct((n_sh, rows, cols), bf16),
        grid_spec=grid_spec, compiler_params=_cparams(("parallel", "parallel")),
    )(c_idx, g_halves, theirs)


def _sum_chips(partial, received, k_idx):
    _, rows, cols = partial.shape
    rb = _tile(rows, 512, SUBLANE)

    def body(k_ref, mine_ref, r_ref, o_ref):
        acc = mine_ref[...].astype(f32)
        for j in range(N_CHIPS - 1):
            acc = acc + r_ref[j].astype(f32)
        o_ref[...] = acc

    grid_spec = pltpu.PrefetchScalarGridSpec(
        num_scalar_prefetch=1, grid=(rows // rb,),
        in_specs=[pl.BlockSpec((None, rb, cols), lambda i, k_ref: (k_ref[0], i, 0)),
                  pl.BlockSpec((N_CHIPS - 1, rb, cols), lambda i, k_ref: (0, i, 0))],
        out_specs=pl.BlockSpec((rb, cols), lambda i, k_ref: (i, 0)))
    return pl.pallas_call(
        body, name="grad_sum_chips", out_shape=jax.ShapeDtypeStruct((rows, cols), f32),
        grid_spec=grid_spec, compiler_params=_cparams(("parallel",)),
    )(k_idx, partial, received)


HBM_SPEC = pl.BlockSpec(memory_space=pltpu.HBM)
CHIP_FLIPS = ((0, 1), (1, 0), (1, 1))


def _position():
    return lax.axis_index("x"), lax.axis_index("y"), lax.axis_index("c")


def _gather_weights(flats):
    n = len(flats)
    n_far = len(CHIP_FLIPS)

    def body(*refs):
        srcs, outs = refs[:n], refs[n:2 * n]
        send_sems, recv_sems, local_sems = refs[2 * n:]
        x, y, c = _position()
        k = 2 * x + y
        sibling = (x, y, 1 - c)
        local, first, passed = [], [], []
        for a in range(n):
            cp = pltpu.make_async_copy(srcs[a], outs[a].at[k], local_sems.at[a])
            cp.start()
            local.append(cp)
        for a in range(n):
            for j, (fx, fy) in enumerate(CHIP_FLIPS):
                s = a * 2 * n_far + j
                cp = pltpu.make_async_remote_copy(
                    src_ref=srcs[a].at[c], dst_ref=outs[a].at[k, c], send_sem=send_sems.at[s],
                    recv_sem=recv_sems.at[s], device_id=(x ^ fx, y ^ fy, c), device_id_type=MESH)
                cp.start()
                first.append(cp)
        for a in range(n):
            for j, (fx, fy) in enumerate(CHIP_FLIPS):
                s = a * 2 * n_far + j
                kk = 2 * (x ^ fx) + (y ^ fy)
                first[a * n_far + j].wait_recv()
                cp = pltpu.make_async_remote_copy(
                    src_ref=outs[a].at[kk, c], dst_ref=outs[a].at[kk, c], send_sem=send_sems.at[s + n_far],
                    recv_sem=recv_sems.at[s + n_far], device_id=sibling, device_id_type=MESH)
                cp.start()
                passed.append(cp)
        for cp in passed:
            cp.wait_recv()
        for cp in first + passed:
            cp.wait_send()
        for cp in local:
            cp.wait()

    return pl.pallas_call(
        body, name="gather_weights",
        out_shape=[jax.ShapeDtypeStruct((N_CHIPS,) + f.shape, f.dtype) for f in flats],
        in_specs=[HBM_SPEC] * n, out_specs=[HBM_SPEC] * n,
        scratch_shapes=[pltpu.SemaphoreType.DMA((2 * n_far * n,)), pltpu.SemaphoreType.DMA((2 * n_far * n,)),
                        pltpu.SemaphoreType.DMA((n,))],
    )(*flats)


def _swap_with_sibling(src, name):
    def body(src_ref, out_ref, send_sem, recv_sem):
        x, y, c = _position()
        cp = pltpu.make_async_remote_copy(
            src_ref=src_ref.at[1 - c], dst_ref=out_ref, send_sem=send_sem, recv_sem=recv_sem,
            device_id=(x, y, 1 - c), device_id_type=MESH)
        cp.start()
        cp.wait()

    return pl.pallas_call(
        body, name=name, out_shape=jax.ShapeDtypeStruct(src.shape[1:], src.dtype),
        in_specs=[HBM_SPEC], out_specs=HBM_SPEC,
        scratch_shapes=[pltpu.SemaphoreType.DMA, pltpu.SemaphoreType.DMA],
    )(src)


def _send_to_chips(partial):
    n_far = len(CHIP_FLIPS)

    def body(src_ref, out_ref, send_sems, recv_sems):
        x, y, c = _position()
        copies = []
        for j, (fx, fy) in enumerate(CHIP_FLIPS):
            kk = 2 * (x ^ fx) + (y ^ fy)
            cp = pltpu.make_async_remote_copy(
                src_ref=src_ref.at[kk], dst_ref=out_ref.at[j], send_sem=send_sems.at[j], recv_sem=recv_sems.at[j],
                device_id=(x ^ fx, y ^ fy, c), device_id_type=MESH)
            cp.start()
            copies.append(cp)
        for cp in copies:
            cp.wait()

    return pl.pallas_call(
        body, name="grad_to_chips", out_shape=jax.ShapeDtypeStruct((n_far,) + partial.shape[1:], partial.dtype),
        in_specs=[HBM_SPEC], out_specs=HBM_SPEC,
        scratch_shapes=[pltpu.SemaphoreType.DMA((n_far,)), pltpu.SemaphoreType.DMA((n_far,))],
    )(partial)


def _join_halves(half):
    def body(src_ref, out_ref, send_sem, recv_sem, local_sem):
        x, y, c = _position()
        own = pltpu.make_async_copy(src_ref, out_ref.at[c], local_sem)
        own.start()
        cp = pltpu.make_async_remote_copy(
            src_ref=src_ref, dst_ref=out_ref.at[c], send_sem=send_sem, recv_sem=recv_sem,
            device_id=(x, y, 1 - c), device_id_type=MESH)
        cp.start()
        cp.wait()
        own.wait()

    return pl.pallas_call(
        body, name="grad_join_halves", out_shape=jax.ShapeDtypeStruct((2,) + half.shape, half.dtype),
        in_specs=[HBM_SPEC], out_specs=HBM_SPEC,
        scratch_shapes=[pltpu.SemaphoreType.DMA, pltpu.SemaphoreType.DMA, pltpu.SemaphoreType.DMA],
    )(half)


def _all_sum_small(vec):
    rows, cols = vec.shape

    def body(v_ref, o_ref, buf, send_sems, recv_sems):
        x, y, c = _position()
        me = 4 * x + 2 * y + c
        buf[me] = v_ref[...]
        copies = []
        for m in range(1, N_DEV):
            fx, fy, fc = (m >> 2) & 1, (m >> 1) & 1, m & 1
            cp = pltpu.make_async_remote_copy(
                src_ref=v_ref, dst_ref=buf.at[me], send_sem=send_sems.at[m - 1], recv_sem=recv_sems.at[m - 1],
                device_id=(x ^ fx, y ^ fy, c ^ fc), device_id_type=MESH)
            cp.start()
            copies.append(cp)
        for cp in copies:
            cp.wait()
        acc = buf[0]
        for d in range(1, N_DEV):
            acc = acc + buf[d]
        o_ref[...] = acc

    return pl.pallas_call(
        body, name="all_sum_small", out_shape=jax.ShapeDtypeStruct((rows, cols), f32),
        in_specs=[pl.BlockSpec(memory_space=pltpu.VMEM)], out_specs=pl.BlockSpec(memory_space=pltpu.VMEM),
        scratch_shapes=[pltpu.VMEM((N_DEV, rows, cols), f32), pltpu.SemaphoreType.DMA((N_DEV - 1,)),
                        pltpu.SemaphoreType.DMA((N_DEV - 1,))],
        compiler_params=_cparams(),
    )(vec)


FLAT_QUANTUM = 2 * 2 * SUBLANE * FLAT_COLS


def _pack(arrays, dtype):
    flat = jnp.concatenate([a.astype(dtype).reshape(-1) for a in arrays])
    n = flat.shape[0]
    n_pad = -(-n // FLAT_QUANTUM) * FLAT_QUANTUM
    return jnp.pad(flat, (0, n_pad - n))


def _unpack(flat, shapes):
    out, off = [], 0
    for s in shapes:
        n = int(np.prod(s))
        out.append(flat[..., off:off + n].reshape(flat.shape[:-1] + tuple(s)))
        off += n
    return out


def _full_from_shards(stacked, axis):
    return jnp.concatenate([stacked[k] for k in range(N_CHIPS)], axis=axis)


def _shards_of(full, axis):
    return jnp.stack(jnp.split(full, N_CHIPS, axis=axis))


def _ffn_fwd(h, p):
    u = _row_fwd(_f_rms, [h], [p['g']], [(D_MODEL, bf16)], rows=512, name="ffn_norm")[0]
    a = _mm(u, p['up'], 'nn', "ffn_up")
    gated = _row_fwd(_f_ffn_gate, [a], [p['cw'], p['cb']], [(FFN_H, bf16)], rows=256, name="ffn_gate",
                     halo=SUBLANE, halo_of=[True])[0]
    h_out = _mm(gated, p['down'], 'nn', "ffn_down", add=h)
    return h_out, (h, u, a, gated)


def _ffn_bwd(dh_out, p, saved, bias_zero):
    h, u, a, gated = saved
    d_gated = _mm(dh_out, p['down'], 'nt', "ffn_down_dx", out_dtype=bf16)
    d_down = _mm(gated, dh_out, 'tn', "ffn_down_dw")
    da, d_cw, d_cb = _row_bwd(_f_ffn_gate, [a], [p['cw'], p['cb']], [d_gated], rows=128, name="ffn_gate_bwd",
                              halo=SUBLANE, halo_of=[True], tile_dtypes=[bf16])
    d_up = _mm(u, da, 'tn', "ffn_up_dw")
    du = _mm(da, p['up'], 'nt', "ffn_up_dx", out_dtype=bf16)
    dh, d_g, d_bias = _row_bwd(_f_rms_res, [h], [p['g'], bias_zero], [du, dh_out], rows=512, name="ffn_norm_bwd")
    return dh, {'g': d_g, 'up': d_up, 'down': d_down, 'cw': d_cw, 'cb': d_cb}, d_bias


def _mixer_norm_bwd(h, g, du, dh_res, name):
    def f(hv, gv):
        return _rms(hv, gv), hv

    dh, d_g = _row_bwd(f, [h], [g], [du, dh_res], rows=512, name=name)
    return dh, d_g


def _ssd_layer_fwd(h, p):
    u = _row_fwd(_f_rms, [h], [p['g']], [(D_MODEL, bf16)], rows=512, name="ssd_norm")[0]
    z = _mm(u, p['w_z'], 'nn', "ssd_in_z")
    xbc = _mm(u, p['w_xbc'], 'nn', "ssd_in_xbc")
    dtr = _mm(u, p['w_dt'], 'nn', "ssd_in_dt")
    xs, bm, cm, dt = _row_fwd(_f_ssd_pre, [xbc, dtr], [p['cw'], p['cb'], p['dtb']],
                              [(SSD_D_INNER, f32), (SSD_BC, f32), (SSD_BC, f32), (LANE, f32)], rows=256,
                              name="ssd_conv", halo=SUBLANE, halo_of=[True, False])
    y, states = _ssd_fwd(xs, bm, cm, dt, p['a_log'], p['dsk'])
    yn = _row_fwd(_f_ssd_post, [y, z], [p['norm']], [(SSD_D_INNER, bf16)], rows=256, name="ssd_gate_norm")[0]
    h_out = _mm(yn, p['out'], 'nn', "ssd_out", add=h)
    return h_out, (h, u, z, xbc, dtr, xs, bm, cm, dt, states, y, yn)


def _ssd_layer_bwd(dh_out, p, saved):
    h, u, z, xbc, dtr, xs, bm, cm, dt, states, y, yn = saved
    d_yn = _mm(dh_out, p['out'], 'nt', "ssd_out_dx", out_dtype=bf16)
    d_out = _mm(yn, dh_out, 'tn', "ssd_out_dw")
    dy, dz, d_norm = _row_bwd(_f_ssd_post, [y, z], [p['norm']], [d_yn], rows=256, name="ssd_gate_norm_bwd",
                              tile_dtypes=[f32, bf16])
    dxs, dbm, dcm, ddt, d_alog, d_dsk = _ssd_bwd(xs, bm, cm, dt, p['a_log'], p['dsk'], states, dy)
    dxbc, ddtr, d_cw, d_cb, d_dtb = _row_bwd(
        _f_ssd_pre, [xbc, dtr], [p['cw'], p['cb'], p['dtb']], [dxs, dbm, dcm, ddt], rows=128, name="ssd_conv_bwd",
        halo=SUBLANE, halo_of=[True, False], tile_dtypes=[bf16, bf16])
    d_wz = _mm(u, dz, 'tn', "ssd_in_z_dw")
    d_wxbc = _mm(u, dxbc, 'tn', "ssd_in_xbc_dw")
    d_wdt = _mm(u, ddtr, 'tn', "ssd_in_dt_dw")
    du = _mm(dz, p['w_z'], 'nt', "ssd_in_z_dx")
    du = _mm(dxbc, p['w_xbc'], 'nt', "ssd_in_xbc_dx", add=du)
    du = _mm(ddtr, p['w_dt'], 'nt', "ssd_in_dt_dx", add=du, out_dtype=bf16)
    dh, d_g = _mixer_norm_bwd(h, p['g'], du, dh_out, "ssd_norm_bwd")
    grads = {'g': d_g, 'w_z': d_wz, 'w_xbc': d_wxbc, 'w_dt': d_wdt, 'cw': d_cw, 'cb': d_cb, 'dtb': d_dtb,
             'a_log': d_alog, 'dsk': d_dsk, 'norm': d_norm, 'out': d_out}
    return dh, grads


def _conf_layer_fwd(h, p):
    u = _row_fwd(_f_rms, [h], [p['g']], [(D_MODEL, bf16)], rows=512, name="conf_norm")[0]
    g2 = _mm(u, p['pw1'], 'nn', "conf_pw1")
    s = _row_fwd(_f_conf, [g2], [p['b1'], p['dw_w'], p['dw_b'], p['ln_g'], p['ln_b']], [(D_MODEL, bf16)],
                 rows=256, name="conf_conv", halo=CONF_HALO, halo_of=[True])[0]
    h_out = _mm(s, p['pw2'], 'nn', "conf_pw2", bias=p['b2'], add=h)
    return h_out, (h, u, g2, s)


def _conf_layer_bwd(dh_out, p, saved):
    h, u, g2, s = saved
    ds = _mm(dh_out, p['pw2'], 'nt', "conf_pw2_dx", out_dtype=bf16)
    d_pw2 = _mm(s, dh_out, 'tn', "conf_pw2_dw")
    dg2, d_b1, d_dww, d_dwb, d_lng, d_lnb = _row_bwd(
        _f_conf, [g2], [p['b1'], p['dw_w'], p['dw_b'], p['ln_g'], p['ln_b']], [ds], rows=256, name="conf_conv_bwd",
        halo=CONF_HALO, halo_of=[True], tile_dtypes=[bf16])
    d_pw1 = _mm(u, dg2, 'tn', "conf_pw1_dw")
    du = _mm(dg2, p['pw1'], 'nt', "conf_pw1_dx", out_dtype=bf16)
    dh, d_g = _mixer_norm_bwd(h, p['g'], du, dh_out, "conf_norm_bwd")
    grads = {'g': d_g, 'pw1': d_pw1, 'b1': d_b1, 'dw_w': d_dww, 'dw_b': d_dwb, 'ln_g': d_lng, 'ln_b': d_lnb,
             'pw2': d_pw2}
    return dh, grads


def _lru_params(p):
    return [p['in_b'], p['cw'], p['cb'], p['ga_w'], p['ga_b'], p['gx_w'], p['gx_b'], p['lam']]


def _lru_layer_fwd(h, p):
    u = _row_fwd(_f_rms, [h], [p['g']], [(D_MODEL, bf16)], rows=512, name="lru_norm")[0]
    io = _mm(u, p['in_w'], 'nn', "lru_in")
    a, b, gg = _row_fwd(_f_lru, [io], _lru_params(p), [(LRU_W, f32)] * 3, rows=256, name="lru_gates",
                        halo=SUBLANE, halo_of=[True])
    y, hs = _lru_fwd(a, b, gg)
    h_out = _mm(y, p['out'], 'nn', "lru_out", bias=p['out_b'], add=h)
    return h_out, (h, u, io, a, gg, hs, y)


def _lru_layer_bwd(dh_out, p, saved):
    h, u, io, a, gg, hs, y = saved
    dy = _mm(dh_out, p['out'], 'nt', "lru_out_dx")
    d_out = _mm(y, dh_out, 'tn', "lru_out_dw")
    da, db, dgg = _lru_bwd(dy, gg, a, hs)
    res = _row_bwd(_f_lru, [io], _lru_params(p), [da, db, dgg], rows=256, name="lru_gates_bwd",
                   halo=SUBLANE, halo_of=[True], tile_dtypes=[bf16])
    dio, d_inb, d_cw, d_cb, d_gaw, d_gab, d_gxw, d_gxb, d_lam = res
    d_inw = _mm(u, dio, 'tn', "lru_in_dw")
    du = _mm(dio, p['in_w'], 'nt', "lru_in_dx", out_dtype=bf16)
    dh, d_g = _mixer_norm_bwd(h, p['g'], du, dh_out, "lru_norm_bwd")
    grads = {'g': d_g, 'in_w': d_inw, 'in_b': d_inb, 'cw': d_cw, 'cb': d_cb, 'ga_w': d_gaw, 'ga_b': d_gab,
             'gx_w': d_gxw, 'gx_b': d_gxb, 'lam': d_lam, 'out': d_out}
    return dh, grads


def _sgu_params(p):
    return [p['in_b'], p['ln_g'], p['ln_b'], p['sp_w'], p['sp_bt']]


def _sgu_layer_fwd(h, p):
    u = _row_fwd(_f_rms, [h], [p['g']], [(D_MODEL, bf16)], rows=512, name="sgu_norm")[0]
    z = _mm(u, p['in_w'], 'nn', "sgu_in")
    s = _row_fwd(_f_sgu, [z], _sgu_params(p), [(SGU_HALF, bf16)], rows=SGU_CHUNK, name="sgu_mix")[0]
    h_out = _mm(s, p['out'], 'nn', "sgu_out", bias=p['out_b'], add=h)
    return h_out, (h, u, z, s)


def _sgu_layer_bwd(dh_out, p, saved):
    h, u, z, s = saved
    ds = _mm(dh_out, p['out'], 'nt', "sgu_out_dx", out_dtype=bf16)
    d_out = _mm(s, dh_out, 'tn', "sgu_out_dw")
    dz, d_inb, d_lng, d_lnb, d_spw, d_spbt = _row_bwd(_f_sgu, [z], _sgu_params(p), [ds], rows=SGU_CHUNK,
                                                      name="sgu_mix_bwd", tile_dtypes=[bf16])
    d_inw = _mm(u, dz, 'tn', "sgu_in_dw")
    du = _mm(dz, p['in_w'], 'nt', "sgu_in_dx", out_dtype=bf16)
    dh, d_g = _mixer_norm_bwd(h, p['g'], du, dh_out, "sgu_norm_bwd")
    grads = {'g': d_g, 'in_w': d_inw, 'in_b': d_inb, 'ln_g': d_lng, 'ln_b': d_lnb, 'sp_w': d_spw, 'sp_bt': d_spbt,
             'out': d_out}
    return dh, grads


def _row(v):
    return v.reshape((1, -1)).astype(f32)


def _pad_lanes(v, n=LANE):
    v = _row(v)
    return jnp.pad(v, ((0, 0), (0, n - v.shape[1])))


def _local_step(x, target, full):
    w = full
    ffn = [{'g': _row(w['norm_ffn'][i]), 'up': w['f_up_w'][i], 'down': w['f_down_w'][i],
            'cw': w['f_conv_w'][i].astype(f32), 'cb': _row(w['f_conv_b'][i])} for i in range(DEPTH)]
    a_in = w['a_in_proj'][0]
    pa = {'g': _row(w['norm_mix'][0]), 'w_z': a_in[:, :SSD_D_INNER],
          'w_xbc': a_in[:, SSD_D_INNER:SSD_D_INNER + SSD_CONV_DIM],
          'w_dt': jnp.pad(a_in[:, SSD_D_INNER + SSD_CONV_DIM:], ((0, 0), (0, LANE - SSD_HEADS))),
          'cw': w['a_conv_w'][0].astype(f32), 'cb': _row(w['a_conv_b'][0]), 'dtb': _pad_lanes(w['a_dt_bias'][0]),
          'a_log': _pad_lanes(w['a_log'][0]), 'dsk': _pad_lanes(w['a_d_skip'][0]), 'norm': _row(w['a_norm'][0]),
          'out': w['a_out_proj'][0]}
    pb = {'g': _row(w['norm_mix'][1]), 'pw1': w['b_pw1_w'][0], 'b1': _row(w['b_pw1_b'][0]),
          'dw_w': w['b_dw_w'][0].astype(f32), 'dw_b': _row(w['b_dw_b'][0]), 'ln_g': _row(w['b_ln_g'][0]),
          'ln_b': _row(w['b_ln_b'][0]), 'pw2': w['b_pw2_w'][0], 'b2': _row(w['b_pw2_b'][0])}
    pc = {'g': _row(w['norm_mix'][2]), 'in_w': w['c_in_w'][0], 'in_b': _row(w['c_in_b'][0]),
          'cw': w['c_conv_w'][0].astype(f32), 'cb': _row(w['c_conv_b'][0]),
          'ga_w': w['c_ga_w'][0].reshape(LRU_W, LRU_BLOCK).astype(f32), 'ga_b': _row(w['c_ga_b'][0]),
          'gx_w': w['c_gx_w'][0].reshape(LRU_W, LRU_BLOCK).astype(f32), 'gx_b': _row(w['c_gx_b'][0]),
          'lam': _row(w['c_lambda'][0]), 'out': w['c_out_w'][0], 'out_b': _row(w['c_out_b'][0])}
    pd = {'g': _row(w['norm_mix'][3]), 'in_w': w['d_in_w'][0], 'in_b': _row(w['d_in_b'][0]),
          'ln_g': _row(w['d_ln_g'][0]), 'ln_b': _row(w['d_ln_b'][0]),
          'sp_w': w['d_sp_w'][0].reshape(SGU_GROUPS * SGU_CHUNK, SGU_CHUNK).astype(f32),
          'sp_bt': w['d_sp_b'][0].astype(f32).T, 'out': w['d_out_w'][0], 'out_b': _row(w['d_out_b'][0])}
    mixers = [(_ssd_layer_fwd, _ssd_layer_bwd, pa), (_conf_layer_fwd, _conf_layer_bwd, pb),
              (_lru_layer_fwd, _lru_layer_bwd, pc), (_sgu_layer_fwd, _sgu_layer_bwd, pd)]

    h = x
    saved = []
    for i in range(DEPTH):
        fwd, _, p = mixers[i]
        h, s_mix = fwd(h, p)
        h, s_ffn = _ffn_fwd(h, ffn[i])
        saved.append((s_mix, s_ffn))
    dh, d_final, loss = _loss_head(h, target, _row(w['norm_final']))

    bias_zero = jnp.zeros((1, D_MODEL), f32)
    g_ffn, g_mix, d_out_bias = [None] * DEPTH, [None] * DEPTH, [None] * DEPTH
    for i in reversed(range(DEPTH)):
        _, bwd, p = mixers[i]
        dh, g_ffn[i], d_out_bias[i] = _ffn_bwd(dh, ffn[i], saved[i][1], bias_zero)
        dh, g_mix[i] = bwd(dh, p, saved[i][0])
    ga, gb, gc, gd = g_mix

    grads = {
        'norm_mix': jnp.concatenate([g['g'] for g in g_mix], axis=0),
        'norm_ffn': jnp.concatenate([g['g'] for g in g_ffn], axis=0),
        'norm_final': d_final.reshape(-1),
        'a_in_proj': jnp.concatenate([ga['w_z'], ga['w_xbc'], ga['w_dt'][:, :SSD_HEADS]], axis=1)[None],
        'a_conv_w': ga['cw'][None], 'a_conv_b': ga['cb'], 'a_dt_bias': ga['dtb'][:, :SSD_HEADS],
        'a_log': ga['a_log'][:, :SSD_HEADS], 'a_d_skip': ga['dsk'][:, :SSD_HEADS], 'a_norm': ga['norm'],
        'a_out_proj': ga['out'][None],
        'b_pw1_w': gb['pw1'][None], 'b_pw1_b': gb['b1'], 'b_dw_w': gb['dw_w'][None], 'b_dw_b': gb['dw_b'],
        'b_ln_g': gb['ln_g'], 'b_ln_b': gb['ln_b'], 'b_pw2_w': gb['pw2'][None], 'b_pw2_b': d_out_bias[1],
        'c_in_w': gc['in_w'][None], 'c_in_b': gc['in_b'], 'c_conv_w': gc['cw'][None], 'c_conv_b': gc['cb'],
        'c_ga_w': gc['ga_w'].reshape(1, LRU_W // LRU_BLOCK, LRU_BLOCK, LRU_BLOCK),
        'c_ga_b': gc['ga_b'].reshape(1, LRU_W // LRU_BLOCK, LRU_BLOCK),
        'c_gx_w': gc['gx_w'].reshape(1, LRU_W // LRU_BLOCK, LRU_BLOCK, LRU_BLOCK),
        'c_gx_b': gc['gx_b'].reshape(1, LRU_W // LRU_BLOCK, LRU_BLOCK),
        'c_lambda': gc['lam'], 'c_out_w': gc['out'][None], 'c_out_b': d_out_bias[2],
        'd_in_w': gd['in_w'][None], 'd_in_b': gd['in_b'], 'd_ln_g': gd['ln_g'], 'd_ln_b': gd['ln_b'],
        'd_sp_w': gd['sp_w'].reshape(1, SGU_GROUPS, SGU_CHUNK, SGU_CHUNK), 'd_sp_b': gd['sp_bt'].T[None],
        'd_out_w': gd['out'][None], 'd_out_b': d_out_bias[3],
        'f_up_w': jnp.stack([g['up'] for g in g_ffn]), 'f_conv_w': jnp.stack([g['cw'] for g in g_ffn]),
        'f_conv_b': jnp.concatenate([g['cb'] for g in g_ffn], axis=0),
        'f_down_w': jnp.stack([g['down'] for g in g_ffn]),
    }
    return loss, dh, grads


def _global_shape(name, shard_shape):
    ax = SHARD_AXIS[name]
    if ax is None:
        return tuple(shard_shape)
    s = list(shard_shape)
    s[ax] *= N_CHIPS
    return tuple(s)


def _step(x, target, weights, moments_m, moments_v):
    x2, t2 = x[0], target[0]
    shard_shapes = {n: weights[n].shape for n in WEIGHTS}

    flat_mm = _pack([weights[n] for n in MATMUL_WEIGHTS], bf16)
    flat_vec = _pack([weights[n] for n in SHARDED_VEC], f32)
    g_mm, g_vec = _gather_weights([flat_mm.reshape(2, -1, FLAT_COLS), flat_vec.reshape(2, -1, FLAT_COLS)])
    all_mm = _unpack(g_mm.reshape(N_CHIPS, -1), [shard_shapes[n] for n in MATMUL_WEIGHTS])
    all_vec = _unpack(g_vec.reshape(N_CHIPS, -1), [shard_shapes[n] for n in SHARDED_VEC])
    full = {n: weights[n] for n in REPLICATED}
    for n, st in zip(MATMUL_WEIGHTS, all_mm):
        full[n] = _full_from_shards(st, SHARD_AXIS[n])
    for n, st in zip(SHARDED_VEC, all_vec):
        full[n] = _full_from_shards(st, SHARD_AXIS[n])

    loss_part, dx, grads = _local_step(x2, t2, full)
    grads = {n: grads[n].reshape(_global_shape(n, shard_shapes[n])) for n in WEIGHTS}

    c_idx = lax.axis_index("c").astype(jnp.int32).reshape(1)
    k_idx = (2 * lax.axis_index("x") + lax.axis_index("y")).astype(jnp.int32).reshape(1)
    per_shard = [_shards_of(grads[n], SHARD_AXIS[n]).reshape(N_CHIPS, -1) for n in SHARDED]
    flat = jnp.concatenate(per_shard, axis=1)
    n_flat = flat.shape[1]
    n_pad = -(-n_flat // FLAT_QUANTUM) * FLAT_QUANTUM
    flat = jnp.pad(flat, ((0, 0), (0, n_pad - n_flat)))
    halves = jnp.swapaxes(flat.reshape(N_CHIPS, 2, -1, FLAT_COLS), 0, 1)
    theirs = _swap_with_sibling(halves, "grad_swap_sibling")
    partial = _sum_with_sibling(halves, theirs, c_idx)
    received = _send_to_chips(partial)
    my_half = _sum_chips(partial, received, k_idx)
    shard_grad = _join_halves(my_half).reshape(-1)
    g_shard = dict(zip(SHARDED, _unpack(shard_grad, [shard_shapes[n] for n in SHARDED])))

    small = jnp.concatenate([grads[n].reshape(-1) for n in REPLICATED] + [loss_part.reshape(-1)[:1]])
    n_small = small.shape[0]
    n_small_pad = -(-n_small // (SUBLANE * FLAT_COLS)) * (SUBLANE * FLAT_COLS)
    small = jnp.pad(small, (0, n_small_pad - n_small)).reshape(-1, FLAT_COLS)
    small = _all_sum_small(small).reshape(-1)
    g_rep = dict(zip(REPLICATED, _unpack(small, [shard_shapes[n] for n in REPLICATED])))
    loss = small[n_small - 1]

    g_all = {**g_shard, **g_rep}
    delta, new_m, new_v = {}, {}, {}
    for n in WEIGHTS:
        delta[n], new_m[n], new_v[n] = _adamw(weights[n], g_all[n], moments_m[n], moments_v[n], "adamw_" + n)
    return loss, dx[None], g_all, delta, new_m, new_v


def kernel(x, norm_mix, norm_ffn, norm_final, a_in_proj, a_conv_w, a_conv_b, a_dt_bias, a_log, a_d_skip, a_norm, a_out_proj, b_pw1_w, b_pw1_b, b_dw_w, b_dw_b, b_ln_g, b_ln_b, b_pw2_w, b_pw2_b, c_in_w, c_in_b, c_conv_w, c_conv_b, c_ga_w, c_ga_b, c_gx_w, c_gx_b, c_lambda, c_out_w, c_out_b, d_in_w, d_in_b, d_ln_g, d_ln_b, d_sp_w, d_sp_b, d_out_w, d_out_b, f_up_w, f_conv_w, f_conv_b, f_down_w, loss_target, m_norm_mix, m_norm_ffn, m_norm_final, m_a_in_proj, m_a_conv_w, m_a_conv_b, m_a_dt_bias, m_a_log, m_a_d_skip, m_a_norm, m_a_out_proj, m_b_pw1_w, m_b_pw1_b, m_b_dw_w, m_b_dw_b, m_b_ln_g, m_b_ln_b, m_b_pw2_w, m_b_pw2_b, m_c_in_w, m_c_in_b, m_c_conv_w, m_c_conv_b, m_c_ga_w, m_c_ga_b, m_c_gx_w, m_c_gx_b, m_c_lambda, m_c_out_w, m_c_out_b, m_d_in_w, m_d_in_b, m_d_ln_g, m_d_ln_b, m_d_sp_w, m_d_sp_b, m_d_out_w, m_d_out_b, m_f_up_w, m_f_conv_w, m_f_conv_b, m_f_down_w, v_norm_mix, v_norm_ffn, v_norm_final, v_a_in_proj, v_a_conv_w, v_a_conv_b, v_a_dt_bias, v_a_log, v_a_d_skip, v_a_norm, v_a_out_proj, v_b_pw1_w, v_b_pw1_b, v_b_dw_w, v_b_dw_b, v_b_ln_g, v_b_ln_b, v_b_pw2_w, v_b_pw2_b, v_c_in_w, v_c_in_b, v_c_conv_w, v_c_conv_b, v_c_ga_w, v_c_ga_b, v_c_gx_w, v_c_gx_b, v_c_lambda, v_c_out_w, v_c_out_b, v_d_in_w, v_d_in_b, v_d_ln_g, v_d_ln_b, v_d_sp_w, v_d_sp_b, v_d_out_w, v_d_out_b, v_f_up_w, v_f_conv_w, v_f_conv_b, v_f_down_w):
    args = locals()
    weights = {n: args[n] for n in WEIGHTS}
    moments_m = {n: args['m_' + n] for n in WEIGHTS}
    moments_v = {n: args['v_' + n] for n in WEIGHTS}
    loss, dx, grad, delta, new_m, new_v = _step(x, loss_target, weights, moments_m, moments_v)
    return (loss, dx, *[grad[n] for n in WEIGHTS], *[delta[n] for n in WEIGHTS],
            *[new_m[n] for n in WEIGHTS], *[new_v[n] for n in WEIGHTS])
```

```python
import functools
import math

import jax
import jax.numpy as jnp
import numpy as np
from jax import lax
from jax.experimental import pallas as pl
from jax.experimental.pallas import tpu as pltpu

f32 = jnp.float32
bf16 = jnp.bfloat16
MESH = pl.DeviceIdType.MESH
HIGHEST = lax.Precision.HIGHEST

D_MODEL = 1024
DEPTH = 4
RMS_EPS = 1e-6
LN_EPS = 1e-5
SSD_D_INNER = 2048
SSD_HEADS = 32
SSD_BC = 1024
SSD_CONV_DIM = 4096
SSD_CHUNK = 128
SSD_GROUPS = 8
LRU_W = 1280
LRU_BLOCK = 256
LRU_C = 8.0
SGU_HALF = 2048
SGU_GROUPS = 8
SGU_CHUNK = 128
FFN_H = 2816
ADAM_LR, ADAM_B1, ADAM_B2, ADAM_EPS, ADAM_WD, ADAM_STEP = 0.001, 0.9, 0.999, 1e-08, 0.01, 10

LANE = 128
SUBLANE = 8
VMEM_LIMIT = 56 * 1024 * 1024
FLAT_COLS = 1024

WEIGHTS = ['norm_mix', 'norm_ffn', 'norm_final', 'a_in_proj', 'a_conv_w', 'a_conv_b', 'a_dt_bias', 'a_log',
           'a_d_skip', 'a_norm', 'a_out_proj', 'b_pw1_w', 'b_pw1_b', 'b_dw_w', 'b_dw_b', 'b_ln_g', 'b_ln_b',
           'b_pw2_w', 'b_pw2_b', 'c_in_w', 'c_in_b', 'c_conv_w', 'c_conv_b', 'c_ga_w', 'c_ga_b', 'c_gx_w',
           'c_gx_b', 'c_lambda', 'c_out_w', 'c_out_b', 'd_in_w', 'd_in_b', 'd_ln_g', 'd_ln_b', 'd_sp_w',
           'd_sp_b', 'd_out_w', 'd_out_b', 'f_up_w', 'f_conv_w', 'f_conv_b', 'f_down_w']
SHARD_AXIS = {
    'norm_mix': None, 'norm_ffn': None, 'norm_final': None, 'a_in_proj': 2, 'a_conv_w': 2, 'a_conv_b': None,
    'a_dt_bias': None, 'a_log': None, 'a_d_skip': None, 'a_norm': None, 'a_out_proj': 1, 'b_pw1_w': 2,
    'b_pw1_b': 1, 'b_dw_w': 2, 'b_dw_b': 1, 'b_ln_g': 1, 'b_ln_b': 1, 'b_pw2_w': 1, 'b_pw2_b': 1, 'c_in_w': 2,
    'c_in_b': 1, 'c_conv_w': 2, 'c_conv_b': 1, 'c_ga_w': 2, 'c_ga_b': 2, 'c_gx_w': 2, 'c_gx_b': 2,
    'c_lambda': 1, 'c_out_w': 1, 'c_out_b': 1, 'd_in_w': 2, 'd_in_b': 1, 'd_ln_g': 1, 'd_ln_b': 1,
    'd_sp_w': None, 'd_sp_b': None, 'd_out_w': 1, 'd_out_b': 1, 'f_up_w': 2, 'f_conv_w': 2, 'f_conv_b': None,
    'f_down_w': 1}
MATMUL_WEIGHTS = ['a_in_proj', 'a_out_proj', 'b_pw1_w', 'b_pw2_w', 'c_in_w', 'c_ga_w', 'c_gx_w', 'c_out_w',
                  'd_in_w', 'd_out_w', 'f_up_w', 'f_down_w']
DIRECT_COLS = ['b_pw1_w', 'c_in_w', 'd_in_w', 'f_up_w']
DIRECT_ROWS = ['a_out_proj', 'b_pw2_w', 'c_out_w', 'd_out_w', 'f_down_w']
DIRECT = DIRECT_COLS + DIRECT_ROWS
PACKED_MM = [n for n in MATMUL_WEIGHTS if n not in DIRECT]
SHARDED = [n for n in WEIGHTS if SHARD_AXIS[n] is not None]
SHARDED_VEC = [n for n in SHARDED if n not in MATMUL_WEIGHTS]
REPLICATED = [n for n in WEIGHTS if SHARD_AXIS[n] is None]
N_CHIPS = 4
N_DEV = 8


def _tile(n, cap, mult):
    if n <= cap:
        return n
    t = (cap // mult) * mult
    while t >= mult:
        if n % t == 0:
            return t
        t -= mult
    raise ValueError(f"no tile for {n} under {cap} in steps of {mult}")


def _cparams(sem=None):
    if sem is None:
        return pltpu.CompilerParams(vmem_limit_bytes=VMEM_LIMIT)
    return pltpu.CompilerParams(dimension_semantics=sem, vmem_limit_bytes=VMEM_LIMIT)


def _dg(a, b, ca, cb):
    return lax.dot_general(a.astype(bf16), b.astype(bf16), (((ca,), (cb,)), ((), ())), preferred_element_type=f32)


@jax.custom_vjp
def _dot_nn(a, b):
    return _dg(a, b, 1, 0)


def _dot_nn_fwd(a, b):
    return _dg(a, b, 1, 0), (a, b)


def _dot_nn_bwd(res, g):
    a, b = res
    return _dg(g, b, 1, 1).astype(a.dtype), _dg(a, g, 0, 0).astype(b.dtype)


_dot_nn.defvjp(_dot_nn_fwd, _dot_nn_bwd)


@jax.custom_vjp
def _dot_nt(a, b):
    return _dg(a, b, 1, 1)


def _dot_nt_fwd(a, b):
    return _dg(a, b, 1, 1), (a, b)


def _dot_nt_bwd(res, g):
    a, b = res
    return _dg(g, b, 1, 0).astype(a.dtype), _dg(g, a, 0, 0).astype(b.dtype)


_dot_nt.defvjp(_dot_nt_fwd, _dot_nt_bwd)


@jax.custom_vjp
def _dot_tn(a, b):
    return _dg(a, b, 0, 0)


def _dot_tn_fwd(a, b):
    return _dg(a, b, 0, 0), (a, b)


def _dot_tn_bwd(res, g):
    a, b = res
    return _dg(b, g, 1, 1).astype(a.dtype), _dg(a, g, 1, 0).astype(b.dtype)


_dot_tn.defvjp(_dot_tn_fwd, _dot_tn_bwd)


def _expm1(x):
    small = jnp.abs(x) < 0.03
    xs = jnp.where(small, x, 0.0)
    series = xs * (1.0 + xs * (0.5 + xs * (1.0 / 6.0 + xs * (1.0 / 24.0 + xs * (1.0 / 120.0)))))
    return jnp.where(small, series, jnp.exp(x) - 1.0)


def _rms(x, g):
    return x * lax.rsqrt(jnp.mean(x * x, axis=-1, keepdims=True) + RMS_EPS) * g


def _layer_norm(x, g, b):
    mu = jnp.mean(x, axis=-1, keepdims=True)
    xc = x - mu
    return xc * lax.rsqrt(jnp.mean(xc * xc, axis=-1, keepdims=True) + LN_EPS) * g + b


def _causal_taps(ext, w, halo, rows):
    k_taps = w.shape[0]
    acc = None
    for k in range(k_taps):
        lo = halo - (k_taps - 1) + k
        term = w[k:k + 1, :] * ext[lo:lo + rows, :]
        acc = term if acc is None else acc + term
    return acc


def _mm(a, b, mode, name, *, bias=None, add=None, out_dtype=f32, tm_cap=1024, tn_cap=1408, tk_cap=1024,
        b_cols_sharded=False, b_layer=None, out_cols_sharded=False):
    shard_cols = None
    if b_cols_sharded:
        shard_cols = b.shape[-1]
        b_dims = (b.shape[-2], N_CHIPS * shard_cols)
    else:
        b_dims = b.shape
    if mode == 'nn':
        (m, k), (k2, n) = a.shape, b_dims
    elif mode == 'nt':
        (m, k), (n, k2) = a.shape, b_dims
    else:
        (k, m), (k2, n) = a.shape, b_dims
    assert k == k2, (name, a.shape, b.shape)
    tm = _tile(m, tm_cap, LANE if mode == 'tn' else SUBLANE)
    tn = _tile(n, tn_cap, LANE)
    tk = _tile(k, tk_cap, LANE if mode != 'tn' else SUBLANE)
    if b_cols_sharded and mode == 'nn':
        tn = shard_cols
    if b_cols_sharded and mode == 'nt':
        tk = shard_cols
    if out_cols_sharded:
        assert mode == 'tn' and n % N_CHIPS == 0
        tn = n // N_CHIPS
    nk = k // tk

    def shard_block(rows):
        lead = (None,) * (b.ndim - 2)
        return lead + (rows, shard_cols)

    def shard_index(shard, row_block):
        return (shard, row_block, 0) if b_layer is None else (shard, b_layer, row_block, 0)

    if mode == 'nn':
        a_spec = pl.BlockSpec((tm, tk), lambda i, j, kk: (i, kk))
        if b_cols_sharded:
            b_spec = pl.BlockSpec(shard_block(tk), lambda i, j, kk: shard_index(j, kk))
        else:
            b_spec = pl.BlockSpec((tk, tn), lambda i, j, kk: (kk, j))
        ca, cb = 1, 0
    elif mode == 'nt':
        a_spec = pl.BlockSpec((tm, tk), lambda i, j, kk: (i, kk))
        if b_cols_sharded:
            b_spec = pl.BlockSpec(shard_block(tn), lambda i, j, kk: shard_index(kk, j))
        else:
            b_spec = pl.BlockSpec((tn, tk), lambda i, j, kk: (j, kk))
        ca, cb = 1, 1
    else:
        a_spec = pl.BlockSpec((tk, tm), lambda i, j, kk: (kk, i))
        b_spec = pl.BlockSpec((tk, tn), lambda i, j, kk: (kk, j))
        ca, cb = 0, 0
    in_specs, operands = [a_spec, b_spec], [a, b]
    if bias is not None:
        in_specs.append(pl.BlockSpec((1, tn), lambda i, j, kk: (0, j)))
        operands.append(bias)
    if add is not None:
        in_specs.append(pl.BlockSpec((tm, tn), lambda i, j, kk: (i, j)))
        operands.append(add)

    def body(*refs):
        a_ref, b_ref = refs[0], refs[1]
        pos = 2
        bias_ref = add_ref = None
        if bias is not None:
            bias_ref = refs[pos]
            pos += 1
        if add is not None:
            add_ref = refs[pos]
            pos += 1
        o_ref, acc_ref = refs[pos], refs[pos + 1]
        kk = pl.program_id(2)

        @pl.when(kk == 0)
        def _():
            acc_ref[...] = jnp.zeros_like(acc_ref)

        acc_ref[...] += _dg(a_ref[...], b_ref[...], ca, cb)

        @pl.when(kk == nk - 1)
        def _():
            r = acc_ref[...]
            if bias_ref is not None:
                r = r + bias_ref[...]
            if add_ref is not None:
                r = r + add_ref[...].astype(f32)
            o_ref[...] = r.astype(out_dtype)

    if out_cols_sharded:
        out_shape = jax.ShapeDtypeStruct((N_CHIPS, m, tn), out_dtype)
        out_spec = pl.BlockSpec((None, tm, tn), lambda i, j, kk: (j, i, 0))
    else:
        out_shape = jax.ShapeDtypeStruct((m, n), out_dtype)
        out_spec = pl.BlockSpec((tm, tn), lambda i, j, kk: (i, j))
    return pl.pallas_call(
        body, name=name, out_shape=out_shape,
        grid=(m // tm, n // tn, nk), in_specs=in_specs, out_specs=out_spec,
        scratch_shapes=[pltpu.VMEM((tm, tn), f32)],
        compiler_params=_cparams(("parallel", "parallel", "arbitrary")),
    )(*operands)


def _mm_w(a, w, mode, name, **kw):
    shards, layer = w
    return _mm(a, shards, mode, name, b_cols_sharded=True, b_layer=layer, **kw)


def _row_specs(tiles, halo_of, rows, halo, n_tiles, reverse):
    def tile_index(i):
        return n_tiles - 1 - i if reverse else i

    specs, operands = [], []
    for arr, has_halo in zip(tiles, halo_of):
        cols = arr.shape[1]
        specs.append(pl.BlockSpec((rows, cols), lambda i: (tile_index(i), 0)))
        operands.append(arr)
        if has_halo:
            per = rows // halo
            specs.append(pl.BlockSpec((halo, cols), lambda i: (jnp.maximum(tile_index(i) * per - 1, 0), 0)))
            operands.append(arr)
    return specs, operands, tile_index


def _load_tiles(refs, halo_of, tile_id, rows, halo):
    vals, pos = [], 0
    for has_halo in halo_of:
        cur = refs[pos][...].astype(f32)
        pos += 1
        if has_halo:
            before = refs[pos][...].astype(f32)
            pos += 1
            before = jnp.where(tile_id > 0, before, jnp.zeros_like(before))
            cur = jnp.concatenate([before, cur], axis=0)
        vals.append(cur)
    return vals, pos


def _valid_rows(tile_id, rows, halo):
    r = lax.broadcasted_iota(jnp.int32, (halo + rows, 1), 0)
    return jnp.logical_or(r >= halo, tile_id > 0).astype(f32)


def _row_fwd(f, tiles, params, outs, *, rows, name, halo=0, halo_of=None):
    t_len = tiles[0].shape[0]
    rows = min(rows, t_len)
    n_tiles = t_len // rows
    halo_of = halo_of or [False] * len(tiles)
    specs, operands, _ = _row_specs(tiles, halo_of, rows, halo, n_tiles, False)
    for p in params:
        specs.append(pl.BlockSpec(p.shape, lambda i: (0, 0)))
        operands.append(p)

    def body(*refs):
        i = pl.program_id(0)
        vals, pos = _load_tiles(refs, halo_of, i, rows, halo)
        pvals = [refs[pos + j][...] for j in range(len(params))]
        pos += len(params)
        kw = {'valid': _valid_rows(i, rows, halo)} if halo else {}
        res = f(*vals, *pvals, **kw)
        for o_ref, o in zip(refs[pos:], res):
            o_ref[...] = o.astype(o_ref.dtype)

    return pl.pallas_call(
        body, name=name,
        out_shape=[jax.ShapeDtypeStruct((t_len, c), d) for c, d in outs],
        grid=(n_tiles,), in_specs=specs,
        out_specs=[pl.BlockSpec((rows, c), lambda i: (i, 0)) for c, _ in outs],
        compiler_params=_cparams(("parallel",)),
    )(*operands)


def _row_bwd(f, tiles, params, cots, *, rows, name, halo=0, halo_of=None, tile_dtypes=None):
    t_len = tiles[0].shape[0]
    rows = min(rows, t_len)
    n_tiles = t_len // rows
    halo_of = halo_of or [False] * len(tiles)
    tile_dtypes = tile_dtypes or [f32] * len(tiles)
    specs, operands, tile_index = _row_specs(tiles, halo_of, rows, halo, n_tiles, True)
    for p in params:
        specs.append(pl.BlockSpec(p.shape, lambda i: (0, 0)))
        operands.append(p)
    for ct in cots:
        specs.append(pl.BlockSpec((rows, ct.shape[1]), lambda i: (tile_index(i), 0)))
        operands.append(ct)
    n_t, n_p, n_c = len(tiles), len(params), len(cots)
    out_shape = [jax.ShapeDtypeStruct(t.shape, d) for t, d in zip(tiles, tile_dtypes)]
    out_shape += [jax.ShapeDtypeStruct(p.shape, f32) for p in params]
    out_specs = [pl.BlockSpec((rows, t.shape[1]), lambda i: (tile_index(i), 0)) for t in tiles]
    out_specs += [pl.BlockSpec(p.shape, lambda i: (0, 0)) for p in params]
    scratch = [pltpu.VMEM((halo, t.shape[1]), f32) for t, h in zip(tiles, halo_of) if h]

    def body(*refs):
        i = pl.program_id(0)
        tile_id = tile_index(i)
        vals, pos = _load_tiles(refs, halo_of, tile_id, rows, halo)
        pvals = [refs[pos + j][...] for j in range(n_p)]
        pos += n_p
        cvals = [refs[pos + j][...].astype(f32) for j in range(n_c)]
        pos += n_c
        d_tile_refs = refs[pos:pos + n_t]
        d_param_refs = refs[pos + n_t:pos + n_t + n_p]
        carries = list(refs[pos + n_t + n_p:])
        kw = {'valid': _valid_rows(tile_id, rows, halo)} if halo else {}
        _, vjp = jax.vjp(lambda *args: tuple(f(*args, **kw)), *vals, *pvals)
        grads = vjp(tuple(cvals))

        @pl.when(i == 0)
        def _():
            for cr in carries:
                cr[...] = jnp.zeros_like(cr)
            for dp in d_param_refs:
                dp[...] = jnp.zeros_like(dp)

        ci = 0
        for t in range(n_t):
            g = grads[t]
            if halo_of[t]:
                cr = carries[ci]
                ci += 1
                d_tile_refs[t][0:rows - halo, :] = g[halo:rows, :].astype(d_tile_refs[t].dtype)
                d_tile_refs[t][rows - halo:rows, :] = (g[rows:rows + halo, :] + cr[...]).astype(d_tile_refs[t].dtype)
                cr[...] = g[0:halo, :]
            else:
                d_tile_refs[t][...] = g.astype(d_tile_refs[t].dtype)
        for j in range(n_p):
            d_param_refs[j][...] += grads[n_t + j]

    return pl.pallas_call(
        body, name=name, out_shape=out_shape, grid=(n_tiles,), in_specs=specs, out_specs=out_specs,
        scratch_shapes=scratch, compiler_params=_cparams(("arbitrary",)),
    )(*operands)


def _f_rms(h, g):
    return (_rms(h, g),)


def _f_rms_res(h, g, bz):
    hh = h + bz
    return _rms(hh, g), hh


def _f_ffn_gate(a_ext, cw, cb, *, valid):
    rows = a_ext.shape[0] - SUBLANE
    hc = _causal_taps(a_ext, cw, SUBLANE, rows) + cb
    return (jax.nn.silu(hc[:, :FFN_H]) * hc[:, FFN_H:],)


def _f_ssd_pre(xbc_ext, dtr, cw, cb, dtb, *, valid):
    rows = dtr.shape[0]
    xc = jax.nn.silu(_causal_taps(xbc_ext, cw, SUBLANE, rows) + cb)
    real = lax.broadcasted_iota(jnp.int32, (1, LANE), 1) < SSD_HEADS
    dt = jnp.where(real, jax.nn.softplus(dtr + dtb), 0.0)
    return xc[:, :SSD_D_INNER], xc[:, SSD_D_INNER:SSD_D_INNER + SSD_BC], xc[:, SSD_D_INNER + SSD_BC:], dt


def _f_ssd_post(y, z, g):
    return (_rms(y * jax.nn.silu(z), g),)


CONF_HALO = 32


def _f_conf(g2_ext, b1, dw_w, dw_b, ln_g, ln_b, *, valid):
    rows = g2_ext.shape[0] - CONF_HALO
    g2 = g2_ext + b1
    glu = g2[:, :D_MODEL] * jax.nn.sigmoid(g2[:, D_MODEL:]) * valid
    conv = _causal_taps(glu, dw_w, CONF_HALO, rows) + dw_b
    return (jax.nn.silu(_layer_norm(conv, ln_g, ln_b)),)


def _f_lru(io_ext, in_b, cw, cb, ga_w, ga_b, gx_w, gx_b, lam, *, valid):
    rows = io_ext.shape[0] - SUBLANE
    io = (io_ext + in_b) * valid
    gate = io[SUBLANE:, :LRU_W]
    xr = _causal_taps(io[:, LRU_W:], cw, SUBLANE, rows) + cb
    rs, iis = [], []
    for blk in range(LRU_W // LRU_BLOCK):
        sl = slice(blk * LRU_BLOCK, (blk + 1) * LRU_BLOCK)
        xb = xr[:, sl]
        rs.append(jax.nn.sigmoid(_dot_nn(xb, ga_w[sl, :]) + ga_b[:, sl]))
        iis.append(jax.nn.sigmoid(_dot_nn(xb, gx_w[sl, :]) + gx_b[:, sl]))
    r = jnp.concatenate(rs, axis=1)
    ig = jnp.concatenate(iis, axis=1)
    log_a = -LRU_C * r * jax.nn.softplus(-lam)
    a = jnp.exp(log_a)
    bterm = jnp.sqrt(-_expm1(2.0 * log_a)) * (ig * xr)
    return a, bterm, jax.nn.gelu(gate)


def _f_sgu(z, in_b, ln_g, ln_b, sp_w, sp_bt):
    rows = z.shape[0]
    zz = jax.nn.gelu(z + in_b)
    u, v = zz[:, :SGU_HALF], zz[:, SGU_HALF:]
    v = _layer_norm(v, ln_g, ln_b)
    tri = lax.broadcasted_iota(jnp.int32, (SGU_CHUNK, SGU_CHUNK), 0) >= lax.broadcasted_iota(
        jnp.int32, (SGU_CHUNK, SGU_CHUNK), 1)
    gdim = SGU_HALF // SGU_GROUPS
    row_blocks = []
    for ci in range(rows // SGU_CHUNK):
        col_blocks = []
        for g in range(SGU_GROUPS):
            w = jnp.where(tri, sp_w[g * SGU_CHUNK:(g + 1) * SGU_CHUNK, :], 0.0)
            vb = v[ci * SGU_CHUNK:(ci + 1) * SGU_CHUNK, g * gdim:(g + 1) * gdim]
            col_blocks.append(_dot_nn(w, vb) + sp_bt[:, g:g + 1])
        row_blocks.append(jnp.concatenate(col_blocks, axis=1))
    mixed = row_blocks[0] if len(row_blocks) == 1 else jnp.concatenate(row_blocks, axis=0)
    return (u * mixed,)


HEADS_PER_GROUP = 4
GROUP_COLS = 256
HEAD_DIM = 64


def _ssd_group(x, bm, cm, dt, st, a_log, dsk, g):
    q = x.shape[0]
    tri = lax.broadcasted_iota(jnp.int32, (q, q), 0) >= lax.broadcasted_iota(jnp.int32, (q, q), 1)
    d_a = dt * (-jnp.exp(a_log))
    acs = jnp.dot(tri.astype(f32), d_a, precision=HIGHEST, preferred_element_type=f32)
    acs_t = acs.T
    lane = lax.broadcasted_iota(jnp.int32, (1, LANE), 1)
    sub = lax.broadcasted_iota(jnp.int32, (LANE, 1), 0)
    col_idx = lax.broadcasted_iota(jnp.int32, (1, GROUP_COLS), 1)
    last_row = (lax.broadcasted_iota(jnp.int32, (q, 1), 0) == q - 1).astype(f32)
    cb = _dot_nt(cm, bm)
    y = jnp.zeros((q, GROUP_COLS), f32)
    e_in = jnp.zeros((q, GROUP_COLS), f32)
    d_end = jnp.zeros((q, GROUP_COLS), f32)
    d_last = jnp.zeros((1, GROUP_COLS), f32)
    d_skip = jnp.zeros((1, GROUP_COLS), f32)
    for j in range(HEADS_PER_GROUP):
        head = HEADS_PER_GROUP * g + j
        on_lane = (lane == head).astype(f32)
        on_sub = (sub == head).astype(f32)
        col = jnp.sum(acs * on_lane, axis=1, keepdims=True)
        row = jnp.sum(acs_t * on_sub, axis=0, keepdims=True)
        dtc = jnp.sum(dt * on_lane, axis=1, keepdims=True)
        last = jnp.sum(col * last_row, axis=0, keepdims=True)
        dsk_j = jnp.sum(dsk * on_lane, axis=1, keepdims=True)
        decay = jnp.where(tri, jnp.exp(jnp.where(tri, col - row, 0.0)), 0.0)
        mine = jnp.logical_and(col_idx >= j * HEAD_DIM, col_idx < (j + 1) * HEAD_DIM)
        y = y + _dot_nn(cb * decay, jnp.where(mine, x * dtc, 0.0))
        e_in = e_in + jnp.where(mine, jnp.exp(col), 0.0)
        d_end = d_end + jnp.where(mine, jnp.exp(last - col) * dtc, 0.0)
        d_last = d_last + jnp.where(mine, jnp.exp(last), 0.0)
        d_skip = d_skip + jnp.where(mine, dsk_j, 0.0)
    y = y + _dot_nn(cm, st) * e_in + x * d_skip
    st_new = st * d_last + _dot_tn(bm, x * d_end)
    return y, st_new


def _ssd_specs(rev, nc):
    def ch(c):
        return nc - 1 - c if rev else c

    x_spec = pl.BlockSpec((SSD_CHUNK, GROUP_COLS), lambda c, g: (ch(c), g))
    bc_spec = pl.BlockSpec((SSD_CHUNK, LANE), lambda c, g: (ch(c), g))
    dt_spec = pl.BlockSpec((SSD_CHUNK, LANE), lambda c, g: (ch(c), 0))
    row_spec = pl.BlockSpec((1, LANE), lambda c, g: (0, 0))
    st_spec = pl.BlockSpec((1, 1, LANE, GROUP_COLS), lambda c, g: (ch(c), g, 0, 0))
    return x_spec, bc_spec, dt_spec, row_spec, st_spec


def _ssd_fwd(xs, bm, cm, dt, a_log, dsk):
    t_len = xs.shape[0]
    nc = t_len // SSD_CHUNK
    x_spec, bc_spec, dt_spec, row_spec, st_spec = _ssd_specs(False, nc)

    def body(x_ref, b_ref, c_ref, dt_ref, al_ref, dk_ref, y_ref, st_out_ref, st_ref):
        c, g = pl.program_id(0), pl.program_id(1)

        @pl.when(c == 0)
        def _():
            st_ref[g] = jnp.zeros((LANE, GROUP_COLS), f32)

        st = st_ref[g]
        st_out_ref[0, 0] = st
        y, st_new = _ssd_group(x_ref[...], b_ref[...], c_ref[...], dt_ref[...], st, al_ref[...], dk_ref[...], g)
        y_ref[...] = y
        st_ref[g] = st_new

    return pl.pallas_call(
        body, name="ssd_scan_fwd",
        out_shape=[jax.ShapeDtypeStruct((t_len, SSD_D_INNER), f32),
                   jax.ShapeDtypeStruct((nc, SSD_GROUPS, LANE, GROUP_COLS), f32)],
        grid=(nc, SSD_GROUPS), in_specs=[x_spec, bc_spec, bc_spec, dt_spec, row_spec, row_spec],
        out_specs=[x_spec, st_spec],
        scratch_shapes=[pltpu.VMEM((SSD_GROUPS, LANE, GROUP_COLS), f32)],
        compiler_params=_cparams(("arbitrary", "arbitrary")),
    )(xs, bm, cm, dt, a_log, dsk)


def _ssd_bwd(xs, bm, cm, dt, a_log, dsk, states, dy):
    t_len = xs.shape[0]
    nc = t_len // SSD_CHUNK
    x_spec, bc_spec, dt_spec, row_spec, st_spec = _ssd_specs(True, nc)

    def body(x_ref, b_ref, c_ref, dt_ref, al_ref, dk_ref, st_in_ref, dy_ref,
             dx_ref, db_ref, dc_ref, ddt_ref, dal_ref, ddk_ref, dst_ref):
        c, g = pl.program_id(0), pl.program_id(1)

        @pl.when(c == 0)
        def _():
            dst_ref[g] = jnp.zeros((LANE, GROUP_COLS), f32)

        @pl.when(jnp.logical_and(c == 0, g == 0))
        def _():
            dal_ref[...] = jnp.zeros_like(dal_ref)
            ddk_ref[...] = jnp.zeros_like(ddk_ref)

        @pl.when(g == 0)
        def _():
            ddt_ref[...] = jnp.zeros_like(ddt_ref)

        _, vjp = jax.vjp(lambda *args: _ssd_group(*args, g), x_ref[...], b_ref[...], c_ref[...], dt_ref[...],
                         st_in_ref[0, 0], al_ref[...], dk_ref[...])
        dx, db, dc, ddt, dst, dal, ddk = vjp((dy_ref[...], dst_ref[g]))
        dx_ref[...] = dx
        db_ref[...] = db
        dc_ref[...] = dc
        ddt_ref[...] += ddt
        dst_ref[g] = dst
        dal_ref[...] += dal
        ddk_ref[...] += ddk

    return pl.pallas_call(
        body, name="ssd_scan_bwd",
        out_shape=[jax.ShapeDtypeStruct((t_len, SSD_D_INNER), f32), jax.ShapeDtypeStruct((t_len, SSD_BC), f32),
                   jax.ShapeDtypeStruct((t_len, SSD_BC), f32), jax.ShapeDtypeStruct((t_len, LANE), f32),
                   jax.ShapeDtypeStruct((1, LANE), f32), jax.ShapeDtypeStruct((1, LANE), f32)],
        grid=(nc, SSD_GROUPS),
        in_specs=[x_spec, bc_spec, bc_spec, dt_spec, row_spec, row_spec, st_spec, x_spec],
        out_specs=[x_spec, bc_spec, bc_spec, dt_spec, row_spec, row_spec],
        scratch_shapes=[pltpu.VMEM((SSD_GROUPS, LANE, GROUP_COLS), f32)],
        compiler_params=_cparams(("arbitrary", "arbitrary")),
    )(xs, bm, cm, dt, a_log, dsk, states, dy)


LRU_ROWS = 256


def _lru_fwd(a, b, gg):
    t_len, cols = a.shape
    rows = min(LRU_ROWS, t_len)
    spec = pl.BlockSpec((rows, cols), lambda i: (i, 0))

    def body(a_ref, b_ref, g_ref, y_ref, h_ref, carry):
        i = pl.program_id(0)

        @pl.when(i == 0)
        def _():
            carry[...] = jnp.zeros_like(carry)

        av, bv = a_ref[...], b_ref[...]
        row = lax.broadcasted_iota(jnp.int32, av.shape, 0)
        s = 1
        while s < rows:
            a_prev = pltpu.roll(av, s, axis=0)
            b_prev = pltpu.roll(bv, s, axis=0)
            m = row >= s
            bv = jnp.where(m, av * b_prev + bv, bv)
            av = jnp.where(m, av * a_prev, av)
            s *= 2
        h = av * carry[0:1, :] + bv
        h_ref[...] = h
        y_ref[...] = g_ref[...] * h
        carry[0:1, :] = h[rows - 1:rows, :]

    return pl.pallas_call(
        body, name="lru_scan_fwd",
        out_shape=[jax.ShapeDtypeStruct((t_len, cols), f32), jax.ShapeDtypeStruct((t_len, cols), f32)],
        grid=(t_len // rows,), in_specs=[spec, spec, spec], out_specs=[spec, spec],
        scratch_shapes=[pltpu.VMEM((SUBLANE, cols), f32)],
        compiler_params=_cparams(("arbitrary",)),
    )(a, b, gg)


def _lru_bwd(dy, gg, a, h):
    t_len, cols = a.shape
    rows = min(LRU_ROWS, t_len)
    n_tiles = t_len // rows
    per = rows // SUBLANE
    spec = pl.BlockSpec((rows, cols), lambda i: (n_tiles - 1 - i, 0))
    prev_spec = pl.BlockSpec((SUBLANE, cols), lambda i: (jnp.maximum((n_tiles - 1 - i) * per - 1, 0), 0))

    def body(dy_ref, g_ref, a_ref, h_ref, hp_ref, da_ref, db_ref, dg_ref, carry_dh, carry_a):
        i = pl.program_id(0)
        tile_id = n_tiles - 1 - i

        @pl.when(i == 0)
        def _():
            carry_dh[...] = jnp.zeros_like(carry_dh)
            carry_a[...] = jnp.zeros_like(carry_a)

        av, hv, dyv = a_ref[...], h_ref[...], dy_ref[...]
        row = lax.broadcasted_iota(jnp.int32, av.shape, 0)
        dg_ref[...] = dyv * hv
        bv = dyv * g_ref[...]
        cv = jnp.where(row == rows - 1, carry_a[0:1, :], pltpu.roll(av, rows - 1, axis=0))
        s = 1
        while s < rows:
            c_next = pltpu.roll(cv, rows - s, axis=0)
            b_next = pltpu.roll(bv, rows - s, axis=0)
            m = row < rows - s
            bv = jnp.where(m, cv * b_next + bv, bv)
            cv = jnp.where(m, cv * c_next, cv)
            s *= 2
        dh = cv * carry_dh[0:1, :] + bv
        h_before = jnp.where(tile_id > 0, hp_ref[SUBLANE - 1:SUBLANE, :], jnp.zeros((1, cols), f32))
        h_prev = jnp.where(row == 0, h_before, pltpu.roll(hv, 1, axis=0))
        da_ref[...] = dh * h_prev
        db_ref[...] = dh
        carry_dh[0:1, :] = dh[0:1, :]
        carry_a[0:1, :] = av[0:1, :]

    return pl.pallas_call(
        body, name="lru_scan_bwd",
        out_shape=[jax.ShapeDtypeStruct((t_len, cols), f32)] * 3,
        grid=(n_tiles,), in_specs=[spec, spec, spec, spec, prev_spec], out_specs=[spec, spec, spec],
        scratch_shapes=[pltpu.VMEM((SUBLANE, cols), f32), pltpu.VMEM((SUBLANE, cols), f32)],
        compiler_params=_cparams(("arbitrary",)),
    )(dy, gg, a, h, h)


def _loss_head(h, target, g):
    t_len = h.shape[0]
    rows = min(512, t_len)

    def f(hv, gv, tv):
        err = _rms(hv, gv) - tv
        return 0.5 * jnp.sum(jnp.mean(err * err, axis=-1, keepdims=True), axis=0, keepdims=True)

    def body(h_ref, t_ref, g_ref, dh_ref, dg_ref, loss_ref):
        i = pl.program_id(0)

        @pl.when(i == 0)
        def _():
            dg_ref[...] = jnp.zeros_like(dg_ref)
            loss_ref[...] = jnp.zeros_like(loss_ref)

        tv = t_ref[...]
        part, vjp = jax.vjp(lambda hv, gv: f(hv, gv, tv), h_ref[...], g_ref[...])
        dh, dg = vjp(jnp.ones((1, 1), f32))
        dh_ref[...] = dh
        dg_ref[...] += dg
        loss_ref[...] += jnp.broadcast_to(part, loss_ref.shape)

    spec = pl.BlockSpec((rows, D_MODEL), lambda i: (i, 0))
    return pl.pallas_call(
        body, name="loss_head",
        out_shape=[jax.ShapeDtypeStruct((t_len, D_MODEL), f32), jax.ShapeDtypeStruct((1, D_MODEL), f32),
                   jax.ShapeDtypeStruct((1, LANE), f32)],
        grid=(t_len // rows,), in_specs=[spec, spec, pl.BlockSpec((1, D_MODEL), lambda i: (0, 0))],
        out_specs=[spec, pl.BlockSpec((1, D_MODEL), lambda i: (0, 0)), pl.BlockSpec((1, LANE), lambda i: (0, 0))],
        compiler_params=_cparams(("arbitrary",)),
    )(h, target, g)


def _as2d(a):
    return a.reshape((-1, a.shape[-1])) if a.ndim > 1 else a.reshape((1, -1))


def _row_block(rows, cols, bytes_cap=1 << 20):
    if rows * cols * 4 <= bytes_cap or rows % SUBLANE:
        return rows
    return _tile(rows, max(SUBLANE, (bytes_cap // (cols * 4)) // SUBLANE * SUBLANE), SUBLANE)


def _adamw(w, g, m, v, name):
    shape = w.shape
    w2, g2, m2, v2 = _as2d(w), _as2d(g), _as2d(m), _as2d(v)
    rows, cols = w2.shape
    rb = _row_block(rows, cols)

    def body(w_ref, g_ref, m_ref, v_ref, d_ref, nm_ref, nv_ref):
        gv = g_ref[...]
        nm = ADAM_B1 * m_ref[...] + (1.0 - ADAM_B1) * gv
        nv = ADAM_B2 * v_ref[...] + (1.0 - ADAM_B2) * jnp.square(gv)
        m_hat = nm / (1.0 - ADAM_B1 ** ADAM_STEP)
        v_hat = nv / (1.0 - ADAM_B2 ** ADAM_STEP)
        d_ref[...] = -ADAM_LR * (m_hat / (jnp.sqrt(v_hat) + ADAM_EPS) + ADAM_WD * w_ref[...])
        nm_ref[...] = nm
        nv_ref[...] = nv

    spec = pl.BlockSpec((rb, cols), lambda i: (i, 0))
    d, nm, nv = pl.pallas_call(
        body, name=name, out_shape=[jax.ShapeDtypeStruct((rows, cols), f32)] * 3,
        grid=(rows // rb,), in_specs=[spec] * 4, out_specs=[spec] * 3,
        compiler_params=_cparams(("parallel",)),
    )(w2, g2, m2, v2)
    return d.reshape(shape), nm.reshape(shape), nv.reshape(shape)


def _sum_with_sibling(g_halves, theirs, c_idx):
    n_sh, _, rows, cols = g_halves.shape
    rb = _tile(rows, 512, 2 * SUBLANE)

    def body(c_ref, mine_ref, theirs_ref, o_ref):
        o_ref[...] = (mine_ref[...] + theirs_ref[...]).astype(bf16)

    grid_spec = pltpu.PrefetchScalarGridSpec(
        num_scalar_prefetch=1, grid=(n_sh, rows // rb),
        in_specs=[pl.BlockSpec((None, None, rb, cols), lambda k, i, c_ref: (k, c_ref[0], i, 0)),
                  pl.BlockSpec((None, rb, cols), lambda k, i, c_ref: (k, i, 0))],
        out_specs=pl.BlockSpec((None, rb, cols), lambda k, i, c_ref: (k, i, 0)))
    return pl.pallas_call(
        body, name="grad_sum_sibling", out_shape=jax.ShapeDtypeStruct((n_sh, rows, cols), bf16),
        grid_spec=grid_spec, compiler_params=_cparams(("parallel", "parallel")),
    )(c_idx, g_halves, theirs)


def _sum_chips(partial, received, k_idx):
    _, rows, cols = partial.shape
    rb = _tile(rows, 512, 2 * SUBLANE)

    def body(k_ref, mine_ref, r_ref, o_ref):
        acc = mine_ref[...].astype(f32)
        for j in range(N_CHIPS - 1):
            acc = acc + r_ref[j].astype(f32)
        o_ref[...] = acc

    grid_spec = pltpu.PrefetchScalarGridSpec(
        num_scalar_prefetch=1, grid=(rows // rb,),
        in_specs=[pl.BlockSpec((None, rb, cols), lambda i, k_ref: (k_ref[0], i, 0)),
                  pl.BlockSpec((N_CHIPS - 1, rb, cols), lambda i, k_ref: (0, i, 0))],
        out_specs=pl.BlockSpec((rb, cols), lambda i, k_ref: (i, 0)))
    return pl.pallas_call(
        body, name="grad_sum_chips", out_shape=jax.ShapeDtypeStruct((rows, cols), f32),
        grid_spec=grid_spec, compiler_params=_cparams(("parallel",)),
    )(k_idx, partial, received)


HBM_SPEC = pl.BlockSpec(memory_space=pltpu.HBM)
CHIP_FLIPS = ((0, 1), (1, 0), (1, 1))


def _position():
    return lax.axis_index("x"), lax.axis_index("y"), lax.axis_index("c")


def _own_slot(gathered, mine, index):
    return [lax.dynamic_update_index_in_dim(g, m, index, 0) for g, m in zip(gathered, mine)]


def _gather_weights(blocks):
    n = len(blocks)
    n_far = len(CHIP_FLIPS)

    def body(*refs):
        srcs, outs = refs[:n], refs[n:2 * n]
        send_sems, recv_sems = refs[2 * n:]
        x, y, c = _position()
        k = 2 * x + y
        sibling = (x, y, 1 - c)
        first, passed = [], []
        for a in range(n):
            for j, (fx, fy) in enumerate(CHIP_FLIPS):
                s = a * 2 * n_far + j
                cp = pltpu.make_async_remote_copy(
                    src_ref=srcs[a].at[c], dst_ref=outs[a].at[k, c], send_sem=send_sems.at[s],
                    recv_sem=recv_sems.at[s], device_id=(x ^ fx, y ^ fy, c), device_id_type=MESH)
                cp.start()
                first.append(cp)
        for a in range(n):
            for j, (fx, fy) in enumerate(CHIP_FLIPS):
                s = a * 2 * n_far + j
                kk = 2 * (x ^ fx) + (y ^ fy)
                first[a * n_far + j].wait_recv()
                cp = pltpu.make_async_remote_copy(
                    src_ref=outs[a].at[kk, c], dst_ref=outs[a].at[kk, c], send_sem=send_sems.at[s + n_far],
                    recv_sem=recv_sems.at[s + n_far], device_id=sibling, device_id_type=MESH)
                cp.start()
                passed.append(cp)
        for cp in passed:
            cp.wait_recv()
        for cp in first + passed:
            cp.wait_send()

    return pl.pallas_call(
        body, name="gather_weights",
        out_shape=[jax.ShapeDtypeStruct((N_CHIPS,) + b.shape, b.dtype) for b in blocks],
        in_specs=[HBM_SPEC] * n, out_specs=[HBM_SPEC] * n,
        scratch_shapes=[pltpu.SemaphoreType.DMA((2 * n_far * n,)), pltpu.SemaphoreType.DMA((2 * n_far * n,))],
    )(*blocks)


def _swap_with_sibling(grads):
    n = len(grads)

    def body(*refs):
        srcs, outs = refs[:n], refs[n:2 * n]
        send_sems, recv_sems = refs[2 * n:]
        x, y, c = _position()
        copies = []
        for a in range(n):
            for kk in range(N_CHIPS):
                s = a * N_CHIPS + kk
                cp = pltpu.make_async_remote_copy(
                    src_ref=srcs[a].at[kk, 1 - c], dst_ref=outs[a].at[kk], send_sem=send_sems.at[s],
                    recv_sem=recv_sems.at[s], device_id=(x, y, 1 - c), device_id_type=MESH)
                cp.start()
                copies.append(cp)
        for cp in copies:
            cp.wait()

    return pl.pallas_call(
        body, name="grad_swap_sibling",
        out_shape=[jax.ShapeDtypeStruct((N_CHIPS,) + g.shape[2:], g.dtype) for g in grads],
        in_specs=[HBM_SPEC] * n, out_specs=[HBM_SPEC] * n,
        scratch_shapes=[pltpu.SemaphoreType.DMA((N_CHIPS * n,)), pltpu.SemaphoreType.DMA((N_CHIPS * n,))],
    )(*grads)


def _send_to_chips(partials):
    n = len(partials)
    n_far = len(CHIP_FLIPS)

    def body(*refs):
        srcs, outs = refs[:n], refs[n:2 * n]
        send_sems, recv_sems = refs[2 * n:]
        x, y, c = _position()
        copies = []
        for a in range(n):
            for j, (fx, fy) in enumerate(CHIP_FLIPS):
                s = a * n_far + j
                kk = 2 * (x ^ fx) + (y ^ fy)
                cp = pltpu.make_async_remote_copy(
                    src_ref=srcs[a].at[kk], dst_ref=outs[a].at[j], send_sem=send_sems.at[s],
                    recv_sem=recv_sems.at[s], device_id=(x ^ fx, y ^ fy, c), device_id_type=MESH)
                cp.start()
                copies.append(cp)
        for cp in copies:
            cp.wait()

    return pl.pallas_call(
        body, name="grad_to_chips",
        out_shape=[jax.ShapeDtypeStruct((n_far,) + p.shape[1:], p.dtype) for p in partials],
        in_specs=[HBM_SPEC] * n, out_specs=[HBM_SPEC] * n,
        scratch_shapes=[pltpu.SemaphoreType.DMA((n_far * n,)), pltpu.SemaphoreType.DMA((n_far * n,))],
    )(*partials)


def _join_halves(halves):
    n = len(halves)

    def body(*refs):
        srcs, outs = refs[:n], refs[n:2 * n]
        send_sems, recv_sems = refs[2 * n:]
        x, y, c = _position()
        copies = []
        for a in range(n):
            cp = pltpu.make_async_remote_copy(
                src_ref=srcs[a], dst_ref=outs[a].at[c], send_sem=send_sems.at[a], recv_sem=recv_sems.at[a],
                device_id=(x, y, 1 - c), device_id_type=MESH)
            cp.start()
            copies.append(cp)
        for cp in copies:
            cp.wait()

    return pl.pallas_call(
        body, name="grad_join_halves",
        out_shape=[jax.ShapeDtypeStruct((2,) + h.shape, h.dtype) for h in halves],
        in_specs=[HBM_SPEC] * n, out_specs=[HBM_SPEC] * n,
        scratch_shapes=[pltpu.SemaphoreType.DMA((n,)), pltpu.SemaphoreType.DMA((n,))],
    )(*halves)


def _all_sum_small(vec):
    rows, cols = vec.shape

    def body(v_ref, o_ref, buf, send_sems, recv_sems):
        x, y, c = _position()
        me = 4 * x + 2 * y + c
        buf[me] = v_ref[...]
        copies = []
        for m in range(1, N_DEV):
            fx, fy, fc = (m >> 2) & 1, (m >> 1) & 1, m & 1
            cp = pltpu.make_async_remote_copy(
                src_ref=v_ref, dst_ref=buf.at[me], send_sem=send_sems.at[m - 1], recv_sem=recv_sems.at[m - 1],
                device_id=(x ^ fx, y ^ fy, c ^ fc), device_id_type=MESH)
            cp.start()
            copies.append(cp)
        for cp in copies:
            cp.wait()
        acc = buf[0]
        for d in range(1, N_DEV):
            acc = acc + buf[d]
        o_ref[...] = acc

    return pl.pallas_call(
        body, name="all_sum_small", out_shape=jax.ShapeDtypeStruct((rows, cols), f32),
        in_specs=[pl.BlockSpec(memory_space=pltpu.VMEM)], out_specs=pl.BlockSpec(memory_space=pltpu.VMEM),
        scratch_shapes=[pltpu.VMEM((N_DEV, rows, cols), f32), pltpu.SemaphoreType.DMA((N_DEV - 1,)),
                        pltpu.SemaphoreType.DMA((N_DEV - 1,))],
        compiler_params=_cparams(),
    )(vec)


FLAT_QUANTUM = 2 * 2 * SUBLANE * FLAT_COLS


def _pack(arrays, dtype):
    flat = jnp.concatenate([a.astype(dtype).reshape(-1) for a in arrays])
    n = flat.shape[0]
    n_pad = -(-n // FLAT_QUANTUM) * FLAT_QUANTUM
    return jnp.pad(flat, (0, n_pad - n))


def _unpack(flat, shapes):
    out, off = [], 0
    for s in shapes:
        n = int(np.prod(s))
        out.append(flat[..., off:off + n].reshape(flat.shape[:-1] + tuple(s)))
        off += n
    return out


def _full_from_shards(stacked, axis):
    return jnp.concatenate([stacked[k] for k in range(N_CHIPS)], axis=axis)


def _shards_of(full, axis):
    return jnp.stack(jnp.split(full, N_CHIPS, axis=axis))


def _ffn_fwd(h, p):
    u = _row_fwd(_f_rms, [h], [p['g']], [(D_MODEL, bf16)], rows=512, name="ffn_norm")[0]
    a = _mm_w(u, p['up'], 'nn', "ffn_up")
    gated = _row_fwd(_f_ffn_gate, [a], [p['cw'], p['cb']], [(FFN_H, bf16)], rows=256, name="ffn_gate",
                     halo=SUBLANE, halo_of=[True])[0]
    h_out = _mm(gated, p['down'], 'nn', "ffn_down", add=h)
    return h_out, (h, u, a, gated)


def _ffn_bwd(dh_out, p, saved, bias_zero):
    h, u, a, gated = saved
    d_gated = _mm(dh_out, p['down'], 'nt', "ffn_down_dx", out_dtype=bf16)
    d_down = _mm(gated, dh_out, 'tn', "ffn_down_dw")
    da, d_cw, d_cb = _row_bwd(_f_ffn_gate, [a], [p['cw'], p['cb']], [d_gated], rows=128, name="ffn_gate_bwd",
                              halo=SUBLANE, halo_of=[True], tile_dtypes=[bf16])
    d_up = _mm(u, da, 'tn', "ffn_up_dw", out_cols_sharded=True)
    du = _mm_w(da, p['up'], 'nt', "ffn_up_dx", out_dtype=bf16)
    dh, d_g, d_bias = _row_bwd(_f_rms_res, [h], [p['g'], bias_zero], [du, dh_out], rows=512, name="ffn_norm_bwd")
    return dh, {'g': d_g, 'up': d_up, 'down': d_down, 'cw': d_cw, 'cb': d_cb}, d_bias


def _mixer_norm_bwd(h, g, du, dh_res, name):
    def f(hv, gv):
        return _rms(hv, gv), hv

    dh, d_g = _row_bwd(f, [h], [g], [du, dh_res], rows=512, name=name)
    return dh, d_g


def _ssd_layer_fwd(h, p):
    u = _row_fwd(_f_rms, [h], [p['g']], [(D_MODEL, bf16)], rows=512, name="ssd_norm")[0]
    z = _mm(u, p['w_z'], 'nn', "ssd_in_z")
    xbc = _mm(u, p['w_xbc'], 'nn', "ssd_in_xbc")
    dtr = _mm(u, p['w_dt'], 'nn', "ssd_in_dt")
    xs, bm, cm, dt = _row_fwd(_f_ssd_pre, [xbc, dtr], [p['cw'], p['cb'], p['dtb']],
                              [(SSD_D_INNER, f32), (SSD_BC, f32), (SSD_BC, f32), (LANE, f32)], rows=256,
                              name="ssd_conv", halo=SUBLANE, halo_of=[True, False])
    y, states = _ssd_fwd(xs, bm, cm, dt, p['a_log'], p['dsk'])
    yn = _row_fwd(_f_ssd_post, [y, z], [p['norm']], [(SSD_D_INNER, bf16)], rows=256, name="ssd_gate_norm")[0]
    h_out = _mm(yn, p['out'], 'nn', "ssd_out", add=h)
    return h_out, (h, u, z, xbc, dtr, xs, bm, cm, dt, states, y, yn)


def _ssd_layer_bwd(dh_out, p, saved):
    h, u, z, xbc, dtr, xs, bm, cm, dt, states, y, yn = saved
    d_yn = _mm(dh_out, p['out'], 'nt', "ssd_out_dx", out_dtype=bf16)
    d_out = _mm(yn, dh_out, 'tn', "ssd_out_dw")
    dy, dz, d_norm = _row_bwd(_f_ssd_post, [y, z], [p['norm']], [d_yn], rows=256, name="ssd_gate_norm_bwd",
                              tile_dtypes=[f32, bf16])
    dxs, dbm, dcm, ddt, d_alog, d_dsk = _ssd_bwd(xs, bm, cm, dt, p['a_log'], p['dsk'], states, dy)
    dxbc, ddtr, d_cw, d_cb, d_dtb = _row_bwd(
        _f_ssd_pre, [xbc, dtr], [p['cw'], p['cb'], p['dtb']], [dxs, dbm, dcm, ddt], rows=128, name="ssd_conv_bwd",
        halo=SUBLANE, halo_of=[True, False], tile_dtypes=[bf16, bf16])
    d_wz = _mm(u, dz, 'tn', "ssd_in_z_dw")
    d_wxbc = _mm(u, dxbc, 'tn', "ssd_in_xbc_dw")
    d_wdt = _mm(u, ddtr, 'tn', "ssd_in_dt_dw")
    du = _mm(dz, p['w_z'], 'nt', "ssd_in_z_dx")
    du = _mm(dxbc, p['w_xbc'], 'nt', "ssd_in_xbc_dx", add=du)
    du = _mm(ddtr, p['w_dt'], 'nt', "ssd_in_dt_dx", add=du, out_dtype=bf16)
    dh, d_g = _mixer_norm_bwd(h, p['g'], du, dh_out, "ssd_norm_bwd")
    grads = {'g': d_g, 'w_z': d_wz, 'w_xbc': d_wxbc, 'w_dt': d_wdt, 'cw': d_cw, 'cb': d_cb, 'dtb': d_dtb,
             'a_log': d_alog, 'dsk': d_dsk, 'norm': d_norm, 'out': d_out}
    return dh, grads


def _conf_layer_fwd(h, p):
    u = _row_fwd(_f_rms, [h], [p['g']], [(D_MODEL, bf16)], rows=512, name="conf_norm")[0]
    g2 = _mm_w(u, p['pw1'], 'nn', "conf_pw1")
    s = _row_fwd(_f_conf, [g2], [p['b1'], p['dw_w'], p['dw_b'], p['ln_g'], p['ln_b']], [(D_MODEL, bf16)],
                 rows=256, name="conf_conv", halo=CONF_HALO, halo_of=[True])[0]
    h_out = _mm(s, p['pw2'], 'nn', "conf_pw2", bias=p['b2'], add=h)
    return h_out, (h, u, g2, s)


def _conf_layer_bwd(dh_out, p, saved):
    h, u, g2, s = saved
    ds = _mm(dh_out, p['pw2'], 'nt', "conf_pw2_dx", out_dtype=bf16)
    d_pw2 = _mm(s, dh_out, 'tn', "conf_pw2_dw")
    dg2, d_b1, d_dww, d_dwb, d_lng, d_lnb = _row_bwd(
        _f_conf, [g2], [p['b1'], p['dw_w'], p['dw_b'], p['ln_g'], p['ln_b']], [ds], rows=256, name="conf_conv_bwd",
        halo=CONF_HALO, halo_of=[True], tile_dtypes=[bf16])
    d_pw1 = _mm(u, dg2, 'tn', "conf_pw1_dw", out_cols_sharded=True)
    du = _mm_w(dg2, p['pw1'], 'nt', "conf_pw1_dx", out_dtype=bf16)
    dh, d_g = _mixer_norm_bwd(h, p['g'], du, dh_out, "conf_norm_bwd")
    grads = {'g': d_g, 'pw1': d_pw1, 'b1': d_b1, 'dw_w': d_dww, 'dw_b': d_dwb, 'ln_g': d_lng, 'ln_b': d_lnb,
             'pw2': d_pw2}
    return dh, grads


def _lru_params(p):
    return [p['in_b'], p['cw'], p['cb'], p['ga_w'], p['ga_b'], p['gx_w'], p['gx_b'], p['lam']]


def _lru_layer_fwd(h, p):
    u = _row_fwd(_f_rms, [h], [p['g']], [(D_MODEL, bf16)], rows=512, name="lru_norm")[0]
    io = _mm_w(u, p['in_w'], 'nn', "lru_in")
    a, b, gg = _row_fwd(_f_lru, [io], _lru_params(p), [(LRU_W, f32)] * 3, rows=256, name="lru_gates",
                        halo=SUBLANE, halo_of=[True])
    y, hs = _lru_fwd(a, b, gg)
    h_out = _mm(y, p['out'], 'nn', "lru_out", bias=p['out_b'], add=h)
    return h_out, (h, u, io, a, gg, hs, y)


def _lru_layer_bwd(dh_out, p, saved):
    h, u, io, a, gg, hs, y = saved
    dy = _mm(dh_out, p['out'], 'nt', "lru_out_dx")
    d_out = _mm(y, dh_out, 'tn', "lru_out_dw")
    da, db, dgg = _lru_bwd(dy, gg, a, hs)
    res = _row_bwd(_f_lru, [io], _lru_params(p), [da, db, dgg], rows=256, name="lru_gates_bwd",
                   halo=SUBLANE, halo_of=[True], tile_dtypes=[bf16])
    dio, d_inb, d_cw, d_cb, d_gaw, d_gab, d_gxw, d_gxb, d_lam = res
    d_inw = _mm(u, dio, 'tn', "lru_in_dw", out_cols_sharded=True)
    du = _mm_w(dio, p['in_w'], 'nt', "lru_in_dx", out_dtype=bf16)
    dh, d_g = _mixer_norm_bwd(h, p['g'], du, dh_out, "lru_norm_bwd")
    grads = {'g': d_g, 'in_w': d_inw, 'in_b': d_inb, 'cw': d_cw, 'cb': d_cb, 'ga_w': d_gaw, 'ga_b': d_gab,
             'gx_w': d_gxw, 'gx_b': d_gxb, 'lam': d_lam, 'out': d_out}
    return dh, grads


def _sgu_params(p):
    return [p['in_b'], p['ln_g'], p['ln_b'], p['sp_w'], p['sp_bt']]


def _sgu_layer_fwd(h, p):
    u = _row_fwd(_f_rms, [h], [p['g']], [(D_MODEL, bf16)], rows=512, name="sgu_norm")[0]
    z = _mm_w(u, p['in_w'], 'nn', "sgu_in")
    s = _row_fwd(_f_sgu, [z], _sgu_params(p), [(SGU_HALF, bf16)], rows=SGU_CHUNK, name="sgu_mix")[0]
    h_out = _mm(s, p['out'], 'nn', "sgu_out", bias=p['out_b'], add=h)
    return h_out, (h, u, z, s)


def _sgu_layer_bwd(dh_out, p, saved):
    h, u, z, s = saved
    ds = _mm(dh_out, p['out'], 'nt', "sgu_out_dx", out_dtype=bf16)
    d_out = _mm(s, dh_out, 'tn', "sgu_out_dw")
    dz, d_inb, d_lng, d_lnb, d_spw, d_spbt = _row_bwd(_f_sgu, [z], _sgu_params(p), [ds], rows=SGU_CHUNK,
                                                      name="sgu_mix_bwd", tile_dtypes=[bf16])
    d_inw = _mm(u, dz, 'tn', "sgu_in_dw", out_cols_sharded=True)
    du = _mm_w(dz, p['in_w'], 'nt', "sgu_in_dx", out_dtype=bf16)
    dh, d_g = _mixer_norm_bwd(h, p['g'], du, dh_out, "sgu_norm_bwd")
    grads = {'g': d_g, 'in_w': d_inw, 'in_b': d_inb, 'ln_g': d_lng, 'ln_b': d_lnb, 'sp_w': d_spw, 'sp_bt': d_spbt,
             'out': d_out}
    return dh, grads


def _row(v):
    return v.reshape((1, -1)).astype(f32)


def _pad_lanes(v, n=LANE):
    v = _row(v)
    return jnp.pad(v, ((0, 0), (0, n - v.shape[1])))


def _local_step(x, target, w):
    ffn = [{'g': _row(w['norm_ffn'][i]), 'up': (w['f_up_w'], i), 'down': w['f_down_w'][i],
            'cw': w['f_conv_w'][i].astype(f32), 'cb': _row(w['f_conv_b'][i])} for i in range(DEPTH)]
    a_in = w['a_in_proj'][0]
    pa = {'g': _row(w['norm_mix'][0]), 'w_z': a_in[:, :SSD_D_INNER],
          'w_xbc': a_in[:, SSD_D_INNER:SSD_D_INNER + SSD_CONV_DIM],
          'w_dt': jnp.pad(a_in[:, SSD_D_INNER + SSD_CONV_DIM:], ((0, 0), (0, LANE - SSD_HEADS))),
          'cw': w['a_conv_w'][0].astype(f32), 'cb': _row(w['a_conv_b'][0]), 'dtb': _pad_lanes(w['a_dt_bias'][0]),
          'a_log': _pad_lanes(w['a_log'][0]), 'dsk': _pad_lanes(w['a_d_skip'][0]), 'norm': _row(w['a_norm'][0]),
          'out': w['a_out_proj']}
    pb = {'g': _row(w['norm_mix'][1]), 'pw1': (w['b_pw1_w'], 0), 'b1': _row(w['b_pw1_b'][0]),
          'dw_w': w['b_dw_w'][0].astype(f32), 'dw_b': _row(w['b_dw_b'][0]), 'ln_g': _row(w['b_ln_g'][0]),
          'ln_b': _row(w['b_ln_b'][0]), 'pw2': w['b_pw2_w'], 'b2': _row(w['b_pw2_b'][0])}
    pc = {'g': _row(w['norm_mix'][2]), 'in_w': (w['c_in_w'], 0), 'in_b': _row(w['c_in_b'][0]),
          'cw': w['c_conv_w'][0].astype(f32), 'cb': _row(w['c_conv_b'][0]),
          'ga_w': w['c_ga_w'][0].reshape(LRU_W, LRU_BLOCK).astype(f32), 'ga_b': _row(w['c_ga_b'][0]),
          'gx_w': w['c_gx_w'][0].reshape(LRU_W, LRU_BLOCK).astype(f32), 'gx_b': _row(w['c_gx_b'][0]),
          'lam': _row(w['c_lambda'][0]), 'out': w['c_out_w'], 'out_b': _row(w['c_out_b'][0])}
    pd = {'g': _row(w['norm_mix'][3]), 'in_w': (w['d_in_w'], 0), 'in_b': _row(w['d_in_b'][0]),
          'ln_g': _row(w['d_ln_g'][0]), 'ln_b': _row(w['d_ln_b'][0]),
          'sp_w': w['d_sp_w'][0].reshape(SGU_GROUPS * SGU_CHUNK, SGU_CHUNK).astype(f32),
          'sp_bt': w['d_sp_b'][0].astype(f32).T, 'out': w['d_out_w'], 'out_b': _row(w['d_out_b'][0])}
    mixers = [(_ssd_layer_fwd, _ssd_layer_bwd, pa), (_conf_layer_fwd, _conf_layer_bwd, pb),
              (_lru_layer_fwd, _lru_layer_bwd, pc), (_sgu_layer_fwd, _sgu_layer_bwd, pd)]

    h = x
    saved = []
    for i in range(DEPTH):
        fwd, _, p = mixers[i]
        h, s_mix = fwd(h, p)
        h, s_ffn = _ffn_fwd(h, ffn[i])
        saved.append((s_mix, s_ffn))
    dh, d_final, loss = _loss_head(h, target, _row(w['norm_final']))

    bias_zero = jnp.zeros((1, D_MODEL), f32)
    g_ffn, g_mix, d_out_bias = [None] * DEPTH, [None] * DEPTH, [None] * DEPTH
    for i in reversed(range(DEPTH)):
        _, bwd, p = mixers[i]
        dh, g_ffn[i], d_out_bias[i] = _ffn_bwd(dh, ffn[i], saved[i][1], bias_zero)
        dh, g_mix[i] = bwd(dh, p, saved[i][0])
    ga, gb, gc, gd = g_mix

    def rows_sharded(g):
        return g.reshape(N_CHIPS, g.shape[0] // N_CHIPS, g.shape[1])

    grads = {
        'norm_mix': jnp.concatenate([g['g'] for g in g_mix], axis=0),
        'norm_ffn': jnp.concatenate([g['g'] for g in g_ffn], axis=0),
        'norm_final': d_final.reshape(-1),
        'a_in_proj': jnp.concatenate([ga['w_z'], ga['w_xbc'], ga['w_dt'][:, :SSD_HEADS]], axis=1)[None],
        'a_conv_w': ga['cw'][None], 'a_conv_b': ga['cb'], 'a_dt_bias': ga['dtb'][:, :SSD_HEADS],
        'a_log': ga['a_log'][:, :SSD_HEADS], 'a_d_skip': ga['dsk'][:, :SSD_HEADS], 'a_norm': ga['norm'],
        'a_out_proj': rows_sharded(ga['out']),
        'b_pw1_w': gb['pw1'], 'b_pw1_b': gb['b1'], 'b_dw_w': gb['dw_w'][None], 'b_dw_b': gb['dw_b'],
        'b_ln_g': gb['ln_g'], 'b_ln_b': gb['ln_b'], 'b_pw2_w': rows_sharded(gb['pw2']), 'b_pw2_b': d_out_bias[1],
        'c_in_w': gc['in_w'], 'c_in_b': gc['in_b'], 'c_conv_w': gc['cw'][None], 'c_conv_b': gc['cb'],
        'c_ga_w': gc['ga_w'].reshape(1, LRU_W // LRU_BLOCK, LRU_BLOCK, LRU_BLOCK),
        'c_ga_b': gc['ga_b'].reshape(1, LRU_W // LRU_BLOCK, LRU_BLOCK),
        'c_gx_w': gc['gx_w'].reshape(1, LRU_W // LRU_BLOCK, LRU_BLOCK, LRU_BLOCK),
        'c_gx_b': gc['gx_b'].reshape(1, LRU_W // LRU_BLOCK, LRU_BLOCK),
        'c_lambda': gc['lam'], 'c_out_w': rows_sharded(gc['out']), 'c_out_b': d_out_bias[2],
        'd_in_w': gd['in_w'], 'd_in_b': gd['in_b'], 'd_ln_g': gd['ln_g'], 'd_ln_b': gd['ln_b'],
        'd_sp_w': gd['sp_w'].reshape(1, SGU_GROUPS, SGU_CHUNK, SGU_CHUNK), 'd_sp_b': gd['sp_bt'].T[None],
        'd_out_w': rows_sharded(gd['out']), 'd_out_b': d_out_bias[3],
        'f_up_w': [g['up'] for g in g_ffn], 'f_conv_w': jnp.stack([g['cw'] for g in g_ffn]),
        'f_conv_b': jnp.concatenate([g['cb'] for g in g_ffn], axis=0),
        'f_down_w': [rows_sharded(g['down']) for g in g_ffn],
    }
    return loss, dh, grads


def _global_shape(name, shard_shape):
    ax = SHARD_AXIS[name]
    if ax is None:
        return tuple(shard_shape)
    s = list(shard_shape)
    s[ax] *= N_CHIPS
    return tuple(s)


def _step(x, target, weights, moments_m, moments_v):
    x2, t2 = x[0], target[0]
    shard_shapes = {n: weights[n].shape for n in WEIGHTS}
    c_pos = lax.axis_index("c")
    k_pos = 2 * lax.axis_index("x") + lax.axis_index("y")
    c_idx = c_pos.astype(jnp.int32).reshape(1)
    k_idx = k_pos.astype(jnp.int32).reshape(1)

    def halves_of(a):
        a2 = _as2d(a)
        return a2.reshape(2, a2.shape[0] // 2, a2.shape[1])

    mine = [halves_of(weights[n].astype(bf16)) for n in DIRECT]
    mine.append(_pack([weights[n] for n in PACKED_MM], bf16).reshape(2, -1, FLAT_COLS))
    mine.append(_pack([weights[n] for n in SHARDED_VEC], f32).reshape(2, -1, FLAT_COLS))
    gathered = _own_slot(_gather_weights(mine), mine, k_pos)
    w = {n: weights[n] for n in REPLICATED}
    for n, g in zip(DIRECT, gathered):
        g = g.reshape((N_CHIPS,) + shard_shapes[n])
        if n in DIRECT_COLS:
            w[n] = g
        elif n == 'f_down_w':
            w[n] = [g[:, i].reshape(-1, g.shape[-1]) for i in range(DEPTH)]
        else:
            w[n] = g.reshape(-1, g.shape[-1])
    all_mm = _unpack(gathered[-2].reshape(N_CHIPS, -1), [shard_shapes[n] for n in PACKED_MM])
    all_vec = _unpack(gathered[-1].reshape(N_CHIPS, -1), [shard_shapes[n] for n in SHARDED_VEC])
    for n, st in zip(PACKED_MM + SHARDED_VEC, all_mm + all_vec):
        w[n] = _full_from_shards(st, SHARD_AXIS[n])

    loss_part, dx, grads = _local_step(x2, t2, w)

    direct = []
    for n in DIRECT:
        direct += grads[n] if isinstance(grads[n], list) else [grads[n]]
    packed = [_shards_of(grads[n].reshape(_global_shape(n, shard_shapes[n])), SHARD_AXIS[n]).reshape(N_CHIPS, -1)
              for n in PACKED_MM + SHARDED_VEC]
    flat = jnp.concatenate(packed, axis=1)
    n_flat = flat.shape[1]
    n_pad = -(-n_flat // FLAT_QUANTUM) * FLAT_QUANTUM
    flat = jnp.pad(flat, ((0, 0), (0, n_pad - n_flat))).reshape(N_CHIPS, -1, FLAT_COLS)
    mine_g = [g.reshape(N_CHIPS, 2, g.shape[1] // 2, g.shape[2]) for g in direct + [flat]]
    theirs = _swap_with_sibling(mine_g)
    partials = [_sum_with_sibling(g, t, c_idx) for g, t in zip(mine_g, theirs)]
    received = _send_to_chips(partials)
    my_halves = [_sum_chips(p, r, k_idx) for p, r in zip(partials, received)]
    joined = _own_slot(_join_halves(my_halves), my_halves, c_pos)
    g_shard, pos = {}, 0
    for n in DIRECT:
        layers = shard_shapes[n][0]
        g_shard[n] = jnp.stack([j.reshape(shard_shapes[n][1:]) for j in joined[pos:pos + layers]])
        pos += layers
    flat_shapes = [shard_shapes[n] for n in PACKED_MM + SHARDED_VEC]
    g_shard.update(zip(PACKED_MM + SHARDED_VEC, _unpack(joined[-1].reshape(-1), flat_shapes)))

    small = jnp.concatenate([grads[n].reshape(-1) for n in REPLICATED] + [loss_part.reshape(-1)[:1]])
    n_small = small.shape[0]
    n_small_pad = -(-n_small // (SUBLANE * FLAT_COLS)) * (SUBLANE * FLAT_COLS)
    small = jnp.pad(small, (0, n_small_pad - n_small)).reshape(-1, FLAT_COLS)
    small = _all_sum_small(small).reshape(-1)
    g_rep = dict(zip(REPLICATED, _unpack(small, [shard_shapes[n] for n in REPLICATED])))
    loss = small[n_small - 1]

    g_all = {**g_shard, **g_rep}
    delta, new_m, new_v = {}, {}, {}
    for n in WEIGHTS:
        delta[n], new_m[n], new_v[n] = _adamw(weights[n], g_all[n], moments_m[n], moments_v[n], "adamw_" + n)
    return loss, dx[None], g_all, delta, new_m, new_v


def kernel(x, norm_mix, norm_ffn, norm_final, a_in_proj, a_conv_w, a_conv_b, a_dt_bias, a_log, a_d_skip, a_norm, a_out_proj, b_pw1_w, b_pw1_b, b_dw_w, b_dw_b, b_ln_g, b_ln_b, b_pw2_w, b_pw2_b, c_in_w, c_in_b, c_conv_w, c_conv_b, c_ga_w, c_ga_b, c_gx_w, c_gx_b, c_lambda, c_out_w, c_out_b, d_in_w, d_in_b, d_ln_g, d_ln_b, d_sp_w, d_sp_b, d_out_w, d_out_b, f_up_w, f_conv_w, f_conv_b, f_down_w, loss_target, m_norm_mix, m_norm_ffn, m_norm_final, m_a_in_proj, m_a_conv_w, m_a_conv_b, m_a_dt_bias, m_a_log, m_a_d_skip, m_a_norm, m_a_out_proj, m_b_pw1_w, m_b_pw1_b, m_b_dw_w, m_b_dw_b, m_b_ln_g, m_b_ln_b, m_b_pw2_w, m_b_pw2_b, m_c_in_w, m_c_in_b, m_c_conv_w, m_c_conv_b, m_c_ga_w, m_c_ga_b, m_c_gx_w, m_c_gx_b, m_c_lambda, m_c_out_w, m_c_out_b, m_d_in_w, m_d_in_b, m_d_ln_g, m_d_ln_b, m_d_sp_w, m_d_sp_b, m_d_out_w, m_d_out_b, m_f_up_w, m_f_conv_w, m_f_conv_b, m_f_down_w, v_norm_mix, v_norm_ffn, v_norm_final, v_a_in_proj, v_a_conv_w, v_a_conv_b, v_a_dt_bias, v_a_log, v_a_d_skip, v_a_norm, v_a_out_proj, v_b_pw1_w, v_b_pw1_b, v_b_dw_w, v_b_dw_b, v_b_ln_g, v_b_ln_b, v_b_pw2_w, v_b_pw2_b, v_c_in_w, v_c_in_b, v_c_conv_w, v_c_conv_b, v_c_ga_w, v_c_ga_b, v_c_gx_w, v_c_gx_b, v_c_lambda, v_c_out_w, v_c_out_b, v_d_in_w, v_d_in_b, v_d_ln_g, v_d_ln_b, v_d_sp_w, v_d_sp_b, v_d_out_w, v_d_out_b, v_f_up_w, v_f_conv_w, v_f_conv_b, v_f_down_w):
    args = locals()
    weights = {n: args[n] for n in WEIGHTS}
    moments_m = {n: args['m_' + n] for n in WEIGHTS}
    moments_v = {n: args['v_' + n] for n in WEIGHTS}
    loss, dx, grad, delta, new_m, new_v = _step(x, loss_target, weights, moments_m, moments_v)
    return (loss, dx, *[grad[n] for n in WEIGHTS], *[delta[n] for n in WEIGHTS],
            *[new_m[n] for n in WEIGHTS], *[new_v[n] for n in WEIGHTS])
```

```python
import functools
import math

import jax
import jax.numpy as jnp
import numpy as np
from jax import lax
from jax.experimental import pallas as pl
from jax.experimental.pallas import tpu as pltpu

f32 = jnp.float32
bf16 = jnp.bfloat16
MESH = pl.DeviceIdType.MESH
HIGHEST = lax.Precision.HIGHEST

D_MODEL = 1024
DEPTH = 4
RMS_EPS = 1e-6
LN_EPS = 1e-5
SSD_D_INNER = 2048
SSD_HEADS = 32
SSD_BC = 1024
SSD_CONV_DIM = 4096
SSD_CHUNK = 128
SSD_GROUPS = 8
LRU_W = 1280
LRU_BLOCK = 256
LRU_C = 8.0
SGU_HALF = 2048
SGU_GROUPS = 8
SGU_CHUNK = 128
FFN_H = 2816
ADAM_LR, ADAM_B1, ADAM_B2, ADAM_EPS, ADAM_WD, ADAM_STEP = 0.001, 0.9, 0.999, 1e-08, 0.01, 10

LANE = 128
SUBLANE = 8
VMEM_LIMIT = 56 * 1024 * 1024
FLAT_COLS = 1024

WEIGHTS = ['norm_mix', 'norm_ffn', 'norm_final', 'a_in_proj', 'a_conv_w', 'a_conv_b', 'a_dt_bias', 'a_log',
           'a_d_skip', 'a_norm', 'a_out_proj', 'b_pw1_w', 'b_pw1_b', 'b_dw_w', 'b_dw_b', 'b_ln_g', 'b_ln_b',
           'b_pw2_w', 'b_pw2_b', 'c_in_w', 'c_in_b', 'c_conv_w', 'c_conv_b', 'c_ga_w', 'c_ga_b', 'c_gx_w',
           'c_gx_b', 'c_lambda', 'c_out_w', 'c_out_b', 'd_in_w', 'd_in_b', 'd_ln_g', 'd_ln_b', 'd_sp_w',
           'd_sp_b', 'd_out_w', 'd_out_b', 'f_up_w', 'f_conv_w', 'f_conv_b', 'f_down_w']
SHARD_AXIS = {
    'norm_mix': None, 'norm_ffn': None, 'norm_final': None, 'a_in_proj': 2, 'a_conv_w': 2, 'a_conv_b': None,
    'a_dt_bias': None, 'a_log': None, 'a_d_skip': None, 'a_norm': None, 'a_out_proj': 1, 'b_pw1_w': 2,
    'b_pw1_b': 1, 'b_dw_w': 2, 'b_dw_b': 1, 'b_ln_g': 1, 'b_ln_b': 1, 'b_pw2_w': 1, 'b_pw2_b': 1, 'c_in_w': 2,
    'c_in_b': 1, 'c_conv_w': 2, 'c_conv_b': 1, 'c_ga_w': 2, 'c_ga_b': 2, 'c_gx_w': 2, 'c_gx_b': 2,
    'c_lambda': 1, 'c_out_w': 1, 'c_out_b': 1, 'd_in_w': 2, 'd_in_b': 1, 'd_ln_g': 1, 'd_ln_b': 1,
    'd_sp_w': None, 'd_sp_b': None, 'd_out_w': 1, 'd_out_b': 1, 'f_up_w': 2, 'f_conv_w': 2, 'f_conv_b': None,
    'f_down_w': 1}
MATMUL_WEIGHTS = ['a_in_proj', 'a_out_proj', 'b_pw1_w', 'b_pw2_w', 'c_in_w', 'c_ga_w', 'c_gx_w', 'c_out_w',
                  'd_in_w', 'd_out_w', 'f_up_w', 'f_down_w']
DIRECT_COLS = ['b_pw1_w', 'c_in_w', 'd_in_w', 'f_up_w']
DIRECT_ROWS = ['a_out_proj', 'b_pw2_w', 'c_out_w', 'd_out_w', 'f_down_w']
DIRECT = DIRECT_COLS + DIRECT_ROWS
PACKED_MM = [n for n in MATMUL_WEIGHTS if n not in DIRECT]
SHARDED = [n for n in WEIGHTS if SHARD_AXIS[n] is not None]
SHARDED_VEC = [n for n in SHARDED if n not in MATMUL_WEIGHTS]
REPLICATED = [n for n in WEIGHTS if SHARD_AXIS[n] is None]
N_CHIPS = 4
N_DEV = 8


def _tile(n, cap, mult):
    if n <= cap:
        return n
    t = (cap // mult) * mult
    while t >= mult:
        if n % t == 0:
            return t
        t -= mult
    raise ValueError(f"no tile for {n} under {cap} in steps of {mult}")


def _cparams(sem=None):
    if sem is None:
        return pltpu.CompilerParams(vmem_limit_bytes=VMEM_LIMIT)
    return pltpu.CompilerParams(dimension_semantics=sem, vmem_limit_bytes=VMEM_LIMIT)


def _dg(a, b, ca, cb):
    return lax.dot_general(a.astype(bf16), b.astype(bf16), (((ca,), (cb,)), ((), ())), preferred_element_type=f32)


@jax.custom_vjp
def _dot_nn(a, b):
    return _dg(a, b, 1, 0)


def _dot_nn_fwd(a, b):
    return _dg(a, b, 1, 0), (a, b)


def _dot_nn_bwd(res, g):
    a, b = res
    return _dg(g, b, 1, 1).astype(a.dtype), _dg(a, g, 0, 0).astype(b.dtype)


_dot_nn.defvjp(_dot_nn_fwd, _dot_nn_bwd)


@jax.custom_vjp
def _dot_nt(a, b):
    return _dg(a, b, 1, 1)


def _dot_nt_fwd(a, b):
    return _dg(a, b, 1, 1), (a, b)


def _dot_nt_bwd(res, g):
    a, b = res
    return _dg(g, b, 1, 0).astype(a.dtype), _dg(g, a, 0, 0).astype(b.dtype)


_dot_nt.defvjp(_dot_nt_fwd, _dot_nt_bwd)


@jax.custom_vjp
def _dot_tn(a, b):
    return _dg(a, b, 0, 0)


def _dot_tn_fwd(a, b):
    return _dg(a, b, 0, 0), (a, b)


def _dot_tn_bwd(res, g):
    a, b = res
    return _dg(b, g, 1, 1).astype(a.dtype), _dg(a, g, 1, 0).astype(b.dtype)


_dot_tn.defvjp(_dot_tn_fwd, _dot_tn_bwd)


def _expm1(x):
    small = jnp.abs(x) < 0.03
    xs = jnp.where(small, x, 0.0)
    series = xs * (1.0 + xs * (0.5 + xs * (1.0 / 6.0 + xs * (1.0 / 24.0 + xs * (1.0 / 120.0)))))
    return jnp.where(small, series, jnp.exp(x) - 1.0)


def _rms(x, g):
    return x * lax.rsqrt(jnp.mean(x * x, axis=-1, keepdims=True) + RMS_EPS) * g


def _layer_norm(x, g, b):
    mu = jnp.mean(x, axis=-1, keepdims=True)
    xc = x - mu
    return xc * lax.rsqrt(jnp.mean(xc * xc, axis=-1, keepdims=True) + LN_EPS) * g + b


def _causal_taps(ext, w, halo, rows):
    k_taps = w.shape[0]
    acc = None
    for k in range(k_taps):
        lo = halo - (k_taps - 1) + k
        term = w[k:k + 1, :] * ext[lo:lo + rows, :]
        acc = term if acc is None else acc + term
    return acc


def _mm(a, b, mode, name, *, bias=None, add=None, out_dtype=f32, tm_cap=1408, tn_cap=1408, tk_cap=1408,
        b_cols_sharded=False, b_layer=None, out_cols_sharded=False):
    shard_cols = None
    if b_cols_sharded:
        shard_cols = b.shape[-1]
        b_dims = (b.shape[-2], N_CHIPS * shard_cols)
    else:
        b_dims = b.shape
    if mode == 'nn':
        (m, k), (k2, n) = a.shape, b_dims
    elif mode == 'nt':
        (m, k), (n, k2) = a.shape, b_dims
    else:
        (k, m), (k2, n) = a.shape, b_dims
    assert k == k2, (name, a.shape, b.shape)
    tm = _tile(m, tm_cap, LANE if mode == 'tn' else SUBLANE)
    tn = _tile(n, tn_cap, LANE)
    tk = _tile(k, tk_cap, LANE if mode != 'tn' else SUBLANE)
    if b_cols_sharded and mode == 'nn':
        tn = shard_cols
    if b_cols_sharded and mode == 'nt':
        tk = shard_cols
    if out_cols_sharded:
        assert mode == 'tn' and n % N_CHIPS == 0
        tn = n // N_CHIPS
    nk = k // tk

    def shard_block(rows):
        lead = (None,) * (b.ndim - 2)
        return lead + (rows, shard_cols)

    def shard_index(shard, row_block):
        return (shard, row_block, 0) if b_layer is None else (shard, b_layer, row_block, 0)

    if mode == 'nn':
        a_spec = pl.BlockSpec((tm, tk), lambda i, j, kk: (i, kk))
        if b_cols_sharded:
            b_spec = pl.BlockSpec(shard_block(tk), lambda i, j, kk: shard_index(j, kk))
        else:
            b_spec = pl.BlockSpec((tk, tn), lambda i, j, kk: (kk, j))
        ca, cb = 1, 0
    elif mode == 'nt':
        a_spec = pl.BlockSpec((tm, tk), lambda i, j, kk: (i, kk))
        if b_cols_sharded:
            b_spec = pl.BlockSpec(shard_block(tn), lambda i, j, kk: shard_index(kk, j))
        else:
            b_spec = pl.BlockSpec((tn, tk), lambda i, j, kk: (j, kk))
        ca, cb = 1, 1
    else:
        a_spec = pl.BlockSpec((tk, tm), lambda i, j, kk: (kk, i))
        b_spec = pl.BlockSpec((tk, tn), lambda i, j, kk: (kk, j))
        ca, cb = 0, 0
    in_specs, operands = [a_spec, b_spec], [a, b]
    if bias is not None:
        in_specs.append(pl.BlockSpec((1, tn), lambda i, j, kk: (0, j)))
        operands.append(bias)
    if add is not None:
        in_specs.append(pl.BlockSpec((tm, tn), lambda i, j, kk: (i, j)))
        operands.append(add)

    def body(*refs):
        a_ref, b_ref = refs[0], refs[1]
        pos = 2
        bias_ref = add_ref = None
        if bias is not None:
            bias_ref = refs[pos]
            pos += 1
        if add is not None:
            add_ref = refs[pos]
            pos += 1
        o_ref, acc_ref = refs[pos], refs[pos + 1]
        kk = pl.program_id(2)

        @pl.when(kk == 0)
        def _():
            acc_ref[...] = jnp.zeros_like(acc_ref)

        acc_ref[...] += _dg(a_ref[...], b_ref[...], ca, cb)

        @pl.when(kk == nk - 1)
        def _():
            r = acc_ref[...]
            if bias_ref is not None:
                r = r + bias_ref[...]
            if add_ref is not None:
                r = r + add_ref[...].astype(f32)
            o_ref[...] = r.astype(out_dtype)

    if out_cols_sharded:
        out_shape = jax.ShapeDtypeStruct((N_CHIPS, m, tn), out_dtype)
        out_spec = pl.BlockSpec((None, tm, tn), lambda i, j, kk: (j, i, 0))
    else:
        out_shape = jax.ShapeDtypeStruct((m, n), out_dtype)
        out_spec = pl.BlockSpec((tm, tn), lambda i, j, kk: (i, j))
    return pl.pallas_call(
        body, name=name, out_shape=out_shape,
        grid=(m // tm, n // tn, nk), in_specs=in_specs, out_specs=out_spec,
        scratch_shapes=[pltpu.VMEM((tm, tn), f32)],
        compiler_params=_cparams(("parallel", "parallel", "arbitrary")),
    )(*operands)


def _mm_w(a, w, mode, name, **kw):
    shards, layer = w
    return _mm(a, shards, mode, name, b_cols_sharded=True, b_layer=layer, **kw)


def _row_specs(tiles, halo_of, rows, halo, n_tiles, reverse):
    def tile_index(i):
        return n_tiles - 1 - i if reverse else i

    specs, operands = [], []
    for arr, has_halo in zip(tiles, halo_of):
        cols = arr.shape[1]
        specs.append(pl.BlockSpec((rows, cols), lambda i: (tile_index(i), 0)))
        operands.append(arr)
        if has_halo:
            per = rows // halo
            specs.append(pl.BlockSpec((halo, cols), lambda i: (jnp.maximum(tile_index(i) * per - 1, 0), 0)))
            operands.append(arr)
    return specs, operands, tile_index


def _load_tiles(refs, halo_of, tile_id, rows, halo):
    vals, pos = [], 0
    for has_halo in halo_of:
        cur = refs[pos][...].astype(f32)
        pos += 1
        if has_halo:
            before = refs[pos][...].astype(f32)
            pos += 1
            before = jnp.where(tile_id > 0, before, jnp.zeros_like(before))
            cur = jnp.concatenate([before, cur], axis=0)
        vals.append(cur)
    return vals, pos


def _valid_rows(tile_id, rows, halo):
    r = lax.broadcasted_iota(jnp.int32, (halo + rows, 1), 0)
    return jnp.logical_or(r >= halo, tile_id > 0).astype(f32)


def _row_fwd(f, tiles, params, outs, *, rows, name, halo=0, halo_of=None):
    t_len = tiles[0].shape[0]
    rows = min(rows, t_len)
    n_tiles = t_len // rows
    halo_of = halo_of or [False] * len(tiles)
    specs, operands, _ = _row_specs(tiles, halo_of, rows, halo, n_tiles, False)
    for p in params:
        specs.append(pl.BlockSpec(p.shape, lambda i: (0, 0)))
        operands.append(p)

    def body(*refs):
        i = pl.program_id(0)
        vals, pos = _load_tiles(refs, halo_of, i, rows, halo)
        pvals = [refs[pos + j][...] for j in range(len(params))]
        pos += len(params)
        kw = {'valid': _valid_rows(i, rows, halo)} if halo else {}
        res = f(*vals, *pvals, **kw)
        for o_ref, o in zip(refs[pos:], res):
            o_ref[...] = o.astype(o_ref.dtype)

    return pl.pallas_call(
        body, name=name,
        out_shape=[jax.ShapeDtypeStruct((t_len, c), d) for c, d in outs],
        grid=(n_tiles,), in_specs=specs,
        out_specs=[pl.BlockSpec((rows, c), lambda i: (i, 0)) for c, _ in outs],
        compiler_params=_cparams(("parallel",)),
    )(*operands)


def _row_bwd(f, tiles, params, cots, *, rows, name, halo=0, halo_of=None, tile_dtypes=None):
    t_len = tiles[0].shape[0]
    rows = min(rows, t_len)
    n_tiles = t_len // rows
    halo_of = halo_of or [False] * len(tiles)
    tile_dtypes = tile_dtypes or [f32] * len(tiles)
    specs, operands, tile_index = _row_specs(tiles, halo_of, rows, halo, n_tiles, True)
    for p in params:
        specs.append(pl.BlockSpec(p.shape, lambda i: (0, 0)))
        operands.append(p)
    for ct in cots:
        specs.append(pl.BlockSpec((rows, ct.shape[1]), lambda i: (tile_index(i), 0)))
        operands.append(ct)
    n_t, n_p, n_c = len(tiles), len(params), len(cots)
    out_shape = [jax.ShapeDtypeStruct(t.shape, d) for t, d in zip(tiles, tile_dtypes)]
    out_shape += [jax.ShapeDtypeStruct(p.shape, f32) for p in params]
    out_specs = [pl.BlockSpec((rows, t.shape[1]), lambda i: (tile_index(i), 0)) for t in tiles]
    out_specs += [pl.BlockSpec(p.shape, lambda i: (0, 0)) for p in params]
    scratch = [pltpu.VMEM((halo, t.shape[1]), f32) for t, h in zip(tiles, halo_of) if h]

    def body(*refs):
        i = pl.program_id(0)
        tile_id = tile_index(i)
        vals, pos = _load_tiles(refs, halo_of, tile_id, rows, halo)
        pvals = [refs[pos + j][...] for j in range(n_p)]
        pos += n_p
        cvals = [refs[pos + j][...].astype(f32) for j in range(n_c)]
        pos += n_c
        d_tile_refs = refs[pos:pos + n_t]
        d_param_refs = refs[pos + n_t:pos + n_t + n_p]
        carries = list(refs[pos + n_t + n_p:])
        kw = {'valid': _valid_rows(tile_id, rows, halo)} if halo else {}
        _, vjp = jax.vjp(lambda *args: tuple(f(*args, **kw)), *vals, *pvals)
        grads = vjp(tuple(cvals))

        @pl.when(i == 0)
        def _():
            for cr in carries:
                cr[...] = jnp.zeros_like(cr)
            for dp in d_param_refs:
                dp[...] = jnp.zeros_like(dp)

        ci = 0
        for t in range(n_t):
            g = grads[t]
            if halo_of[t]:
                cr = carries[ci]
                ci += 1
                d_tile_refs[t][0:rows - halo, :] = g[halo:rows, :].astype(d_tile_refs[t].dtype)
                d_tile_refs[t][rows - halo:rows, :] = (g[rows:rows + halo, :] + cr[...]).astype(d_tile_refs[t].dtype)
                cr[...] = g[0:halo, :]
            else:
                d_tile_refs[t][...] = g.astype(d_tile_refs[t].dtype)
        for j in range(n_p):
            d_param_refs[j][...] += grads[n_t + j]

    return pl.pallas_call(
        body, name=name, out_shape=out_shape, grid=(n_tiles,), in_specs=specs, out_specs=out_specs,
        scratch_shapes=scratch, compiler_params=_cparams(("arbitrary",)),
    )(*operands)


def _strip_specs(tiles, rows, halo, n_tiles, reverse):
    def tile_index(i):
        return n_tiles - 1 - i if reverse else i

    specs, operands = [], []
    for arr, _, has_halo in tiles:
        cols = arr.shape[1]
        specs.append(pl.BlockSpec((rows, cols), lambda i: (tile_index(i), 0)))
        operands.append(arr)
        if has_halo:
            per = rows // halo
            specs.append(pl.BlockSpec((halo, cols), lambda i: (jnp.maximum(tile_index(i) * per - 1, 0), 0)))
            operands.append(arr)
    return specs, operands, tile_index


def _strip_sources(refs, tiles, ext_scratch, tile_id, rows, halo):
    srcs, pos, si = [], 0, 0
    for _, _, has_halo in tiles:
        cur = refs[pos]
        pos += 1
        if has_halo:
            before = refs[pos]
            pos += 1
            scr = ext_scratch[si]
            si += 1
            scr[0:halo, :] = jnp.where(tile_id > 0, before[...].astype(f32), 0.0)
            scr[halo:halo + rows, :] = cur[...].astype(f32)
            srcs.append(scr)
        else:
            srcs.append(cur)
    return srcs, pos


def _cols(base, c0, cs):
    return pl.ds(pl.multiple_of(base + c0, LANE), cs)


def _strip_inputs(tiles, srcs, params, p_refs, r0, c0, rs, cs, halo):
    vals = []
    for (_, bases, has_halo), src in zip(tiles, srcs):
        n_rows = halo + rs if has_halo else rs
        for b in bases:
            vals.append(src[pl.ds(r0, n_rows), _cols(b, c0, cs)].astype(f32))
    for (_, bases), p_ref in zip(params, p_refs):
        for b in bases:
            vals.append(p_ref[:, _cols(b, c0, cs)])
    return vals


def _strip_valid(tile_id, r0, rs, halo):
    r = lax.broadcasted_iota(jnp.int32, (halo + rs, 1), 0) + r0
    return jnp.logical_or(r >= halo, tile_id > 0).astype(f32)


def _strip_fwd(f, tiles, params, outs, *, rows, rs, cs, width, name, halo=0):
    t_len = tiles[0][0].shape[0]
    rows = min(rows, t_len)
    n_tiles, n_rs, n_cs = t_len // rows, rows // rs, width // cs
    specs, operands, _ = _strip_specs(tiles, rows, halo, n_tiles, False)
    for p, _ in params:
        specs.append(pl.BlockSpec(p.shape, lambda i: (0, 0)))
        operands.append(p)
    n_p, n_o = len(params), len(outs)
    scratch = [pltpu.VMEM((halo + rows, arr.shape[1]), f32) for arr, _, hh in tiles if hh]

    def body(*refs):
        i = pl.program_id(0)
        ext_scratch = refs[len(refs) - len(scratch):]
        srcs, pos = _strip_sources(refs, tiles, ext_scratch, i, rows, halo)
        p_refs = refs[pos:pos + n_p]
        o_refs = refs[pos + n_p:pos + n_p + n_o]

        def row_loop(r, carry):
            r0 = pl.multiple_of(r * rs, rs)
            kw = {'valid': _strip_valid(i, r0, rs, halo)} if halo else {}

            def col_loop(c, carry2):
                c0 = c * cs
                res = f(*_strip_inputs(tiles, srcs, params, p_refs, r0, c0, rs, cs, halo), **kw)
                k = 0
                for (_, dt, bases), o_ref in zip(outs, o_refs):
                    for b in bases:
                        o_ref[pl.ds(r0, rs), _cols(b, c0, cs)] = res[k].astype(dt)
                        k += 1
                return carry2

            return lax.fori_loop(0, n_cs, col_loop, carry)

        lax.fori_loop(0, n_rs, row_loop, 0)

    return pl.pallas_call(
        body, name=name,
        out_shape=[jax.ShapeDtypeStruct((t_len, c), d) for c, d, _ in outs],
        grid=(n_tiles,), in_specs=specs,
        out_specs=[pl.BlockSpec((rows, c), lambda i: (i, 0)) for c, _, _ in outs],
        scratch_shapes=scratch, compiler_params=_cparams(("parallel",)),
    )(*operands)


def _strip_bwd(f, tiles, params, cots, *, rows, rs, cs, width, name, halo=0, tile_dtypes=None):
    t_len = tiles[0][0].shape[0]
    rows = min(rows, t_len)
    n_tiles, n_rs, n_cs = t_len // rows, rows // rs, width // cs
    tile_dtypes = tile_dtypes or [f32] * len(tiles)
    specs, operands, tile_index = _strip_specs(tiles, rows, halo, n_tiles, True)
    for p, _ in params:
        specs.append(pl.BlockSpec(p.shape, lambda i: (0, 0)))
        operands.append(p)
    for ct, _ in cots:
        specs.append(pl.BlockSpec((rows, ct.shape[1]), lambda i: (tile_index(i), 0)))
        operands.append(ct)
    n_t, n_p, n_c = len(tiles), len(params), len(cots)
    out_shape = [jax.ShapeDtypeStruct(t[0].shape, d) for t, d in zip(tiles, tile_dtypes)]
    out_shape += [jax.ShapeDtypeStruct(p.shape, f32) for p, _ in params]
    out_specs = [pl.BlockSpec((rows, t[0].shape[1]), lambda i: (tile_index(i), 0)) for t in tiles]
    out_specs += [pl.BlockSpec(p.shape, lambda i: (0, 0)) for p, _ in params]
    halo_tiles = [t for t in tiles if t[2]]
    scratch = [pltpu.VMEM((halo + rows, arr.shape[1]), f32) for arr, _, _ in halo_tiles]
    scratch += [pltpu.VMEM((halo + rows, arr.shape[1]), f32) for arr, _, _ in halo_tiles]
    scratch += [pltpu.VMEM((halo, arr.shape[1]), f32) for arr, _, _ in halo_tiles]
    n_h = len(halo_tiles)

    def body(*refs):
        i = pl.program_id(0)
        tile_id = tile_index(i)
        scr = refs[len(refs) - 3 * n_h:]
        ext_scratch, grad_scratch, carries = scr[:n_h], scr[n_h:2 * n_h], scr[2 * n_h:]
        srcs, pos = _strip_sources(refs, tiles, ext_scratch, tile_id, rows, halo)
        p_refs = refs[pos:pos + n_p]
        c_refs = refs[pos + n_p:pos + n_p + n_c]
        pos += n_p + n_c
        d_tile_refs = refs[pos:pos + n_t]
        d_param_refs = refs[pos + n_t:pos + n_t + n_p]

        @pl.when(i == 0)
        def _():
            for cr in carries:
                cr[...] = jnp.zeros_like(cr)
            for dp in d_param_refs:
                dp[...] = jnp.zeros_like(dp)

        for gs in grad_scratch:
            gs[...] = jnp.zeros_like(gs)

        def row_loop(r, carry):
            r0 = pl.multiple_of(r * rs, rs)
            kw = {'valid': _strip_valid(tile_id, r0, rs, halo)} if halo else {}

            def col_loop(c, carry2):
                c0 = c * cs
                vals = _strip_inputs(tiles, srcs, params, p_refs, r0, c0, rs, cs, halo)
                cvals = []
                for (_, bases), c_ref in zip(cots, c_refs):
                    for b in bases:
                        cvals.append(c_ref[pl.ds(r0, rs), _cols(b, c0, cs)].astype(f32))
                _, vjp = jax.vjp(lambda *args: tuple(f(*args, **kw)), *vals)
                grads = vjp(tuple(cvals))
                k, hi = 0, 0
                for t, (_, bases, has_halo) in enumerate(tiles):
                    for b in bases:
                        if has_halo:
                            grad_scratch[hi][pl.ds(r0, halo + rs), _cols(b, c0, cs)] += grads[k]
                        else:
                            d_tile_refs[t][pl.ds(r0, rs), _cols(b, c0, cs)] = grads[k].astype(d_tile_refs[t].dtype)
                        k += 1
                    hi += has_halo
                for (_, bases), dp in zip(params, d_param_refs):
                    for b in bases:
                        dp[:, _cols(b, c0, cs)] += grads[k]
                        k += 1
                return carry2

            return lax.fori_loop(0, n_cs, col_loop, carry)

        lax.fori_loop(0, n_rs, row_loop, 0)

        hi = 0
        for t, (_, _, has_halo) in enumerate(tiles):
            if has_halo:
                gs, cr, d_ref = grad_scratch[hi], carries[hi], d_tile_refs[t]
                hi += 1
                d_ref[0:rows - halo, :] = gs[halo:rows, :].astype(d_ref.dtype)
                d_ref[rows - halo:rows, :] = (gs[rows:rows + halo, :] + cr[...]).astype(d_ref.dtype)
                cr[...] = gs[0:halo, :]

    return pl.pallas_call(
        body, name=name, out_shape=out_shape, grid=(n_tiles,), in_specs=specs, out_specs=out_specs,
        scratch_shapes=scratch, compiler_params=_cparams(("arbitrary",)),
    )(*operands)


def _f_rms(h, g):
    return (_rms(h, g),)


def _f_rms_res(h, g, bz):
    hh = h + bz
    return _rms(hh, g), hh


def _f_ffn_gate(a_gate, a_val, cw_gate, cw_val, cb_gate, cb_val, *, valid):
    rows = a_gate.shape[0] - SUBLANE
    gate = _causal_taps(a_gate, cw_gate, SUBLANE, rows) + cb_gate
    val = _causal_taps(a_val, cw_val, SUBLANE, rows) + cb_val
    return (jax.nn.silu(gate) * val,)


def _f_ssd_conv(xbc_ext, cw, cb, *, valid):
    rows = xbc_ext.shape[0] - SUBLANE
    return (jax.nn.silu(_causal_taps(xbc_ext, cw, SUBLANE, rows) + cb),)


def _f_ssd_dt(dtr, dtb):
    real = lax.broadcasted_iota(jnp.int32, (1, LANE), 1) < SSD_HEADS
    return (jnp.where(real, jax.nn.softplus(dtr + dtb), 0.0),)


def _f_ssd_post(y, z, g):
    return (_rms(y * jax.nn.silu(z), g),)


CONF_HALO = 32
FFN_STRIP_ROWS = 64
CONF_STRIP_ROWS = 128
SSD_STRIP_ROWS = 64


def _f_conf_conv(g2_a, g2_b, b1_a, b1_b, dw_w, dw_b, *, valid):
    rows = g2_a.shape[0] - CONF_HALO
    glu = (g2_a + b1_a) * jax.nn.sigmoid(g2_b + b1_b) * valid
    return (_causal_taps(glu, dw_w, CONF_HALO, rows) + dw_b,)


def _f_ln_silu(x, g, b):
    return (jax.nn.silu(_layer_norm(x, g, b)),)


def _f_lru(io_ext, in_b, cw, cb, ga_w, ga_b, gx_w, gx_b, lam, *, valid):
    rows = io_ext.shape[0] - SUBLANE
    io = (io_ext + in_b) * valid
    gate = io[SUBLANE:, :LRU_W]
    xr = _causal_taps(io[:, LRU_W:], cw, SUBLANE, rows) + cb
    rs, iis = [], []
    for blk in range(LRU_W // LRU_BLOCK):
        sl = slice(blk * LRU_BLOCK, (blk + 1) * LRU_BLOCK)
        xb = xr[:, sl]
        rs.append(jax.nn.sigmoid(_dot_nn(xb, ga_w[sl, :]) + ga_b[:, sl]))
        iis.append(jax.nn.sigmoid(_dot_nn(xb, gx_w[sl, :]) + gx_b[:, sl]))
    r = jnp.concatenate(rs, axis=1)
    ig = jnp.concatenate(iis, axis=1)
    log_a = -LRU_C * r * jax.nn.softplus(-lam)
    a = jnp.exp(log_a)
    bterm = jnp.sqrt(-_expm1(2.0 * log_a)) * (ig * xr)
    return a, bterm, jax.nn.gelu(gate)


def _f_sgu(z, in_b, ln_g, ln_b, sp_w, sp_bt):
    rows = z.shape[0]
    zz = jax.nn.gelu(z + in_b)
    u, v = zz[:, :SGU_HALF], zz[:, SGU_HALF:]
    v = _layer_norm(v, ln_g, ln_b)
    tri = lax.broadcasted_iota(jnp.int32, (SGU_CHUNK, SGU_CHUNK), 0) >= lax.broadcasted_iota(
        jnp.int32, (SGU_CHUNK, SGU_CHUNK), 1)
    gdim = SGU_HALF // SGU_GROUPS
    row_blocks = []
    for ci in range(rows // SGU_CHUNK):
        col_blocks = []
        for g in range(SGU_GROUPS):
            w = jnp.where(tri, sp_w[g * SGU_CHUNK:(g + 1) * SGU_CHUNK, :], 0.0)
            vb = v[ci * SGU_CHUNK:(ci + 1) * SGU_CHUNK, g * gdim:(g + 1) * gdim]
            col_blocks.append(_dot_nn(w, vb) + sp_bt[:, g:g + 1])
        row_blocks.append(jnp.concatenate(col_blocks, axis=1))
    mixed = row_blocks[0] if len(row_blocks) == 1 else jnp.concatenate(row_blocks, axis=0)
    return (u * mixed,)


HEADS_PER_GROUP = 4
GROUP_COLS = 256
HEAD_DIM = 64


def _ssd_group(x, bm, cm, dt, st, a_log, dsk, g):
    q = x.shape[0]
    tri = lax.broadcasted_iota(jnp.int32, (q, q), 0) >= lax.broadcasted_iota(jnp.int32, (q, q), 1)
    d_a = dt * (-jnp.exp(a_log))
    acs = jnp.dot(tri.astype(f32), d_a, precision=HIGHEST, preferred_element_type=f32)
    acs_t = acs.T
    lane = lax.broadcasted_iota(jnp.int32, (1, LANE), 1)
    sub = lax.broadcasted_iota(jnp.int32, (LANE, 1), 0)
    col_idx = lax.broadcasted_iota(jnp.int32, (1, GROUP_COLS), 1)
    last_row = (lax.broadcasted_iota(jnp.int32, (q, 1), 0) == q - 1).astype(f32)
    cb = _dot_nt(cm, bm)
    y = jnp.zeros((q, GROUP_COLS), f32)
    e_in = jnp.zeros((q, GROUP_COLS), f32)
    d_end = jnp.zeros((q, GROUP_COLS), f32)
    d_last = jnp.zeros((1, GROUP_COLS), f32)
    d_skip = jnp.zeros((1, GROUP_COLS), f32)
    for j in range(HEADS_PER_GROUP):
        head = HEADS_PER_GROUP * g + j
        on_lane = (lane == head).astype(f32)
        on_sub = (sub == head).astype(f32)
        col = jnp.sum(acs * on_lane, axis=1, keepdims=True)
        row = jnp.sum(acs_t * on_sub, axis=0, keepdims=True)
        dtc = jnp.sum(dt * on_lane, axis=1, keepdims=True)
        last = jnp.sum(col * last_row, axis=0, keepdims=True)
        dsk_j = jnp.sum(dsk * on_lane, axis=1, keepdims=True)
        decay = jnp.where(tri, jnp.exp(jnp.where(tri, col - row, 0.0)), 0.0)
        mine = jnp.logical_and(col_idx >= j * HEAD_DIM, col_idx < (j + 1) * HEAD_DIM)
        y = y + _dot_nn(cb * decay, jnp.where(mine, x * dtc, 0.0))
        e_in = e_in + jnp.where(mine, jnp.exp(col), 0.0)
        d_end = d_end + jnp.where(mine, jnp.exp(last - col) * dtc, 0.0)
        d_last = d_last + jnp.where(mine, jnp.exp(last), 0.0)
        d_skip = d_skip + jnp.where(mine, dsk_j, 0.0)
    y = y + _dot_nn(cm, st) * e_in + x * d_skip
    st_new = st * d_last + _dot_tn(bm, x * d_end)
    return y, st_new


def _ssd_specs(rev, nc):
    def ch(c):
        return nc - 1 - c if rev else c

    x_spec = pl.BlockSpec((SSD_CHUNK, GROUP_COLS), lambda c, g: (ch(c), g))
    b_spec = pl.BlockSpec((SSD_CHUNK, LANE), lambda c, g: (ch(c), SSD_D_INNER // LANE + g))
    c_spec = pl.BlockSpec((SSD_CHUNK, LANE), lambda c, g: (ch(c), (SSD_D_INNER + SSD_BC) // LANE + g))
    dt_spec = pl.BlockSpec((SSD_CHUNK, LANE), lambda c, g: (ch(c), 0))
    row_spec = pl.BlockSpec((1, LANE), lambda c, g: (0, 0))
    st_spec = pl.BlockSpec((1, 1, LANE, GROUP_COLS), lambda c, g: (ch(c), g, 0, 0))
    wide_spec = pl.BlockSpec((SSD_CHUNK, SSD_CONV_DIM), lambda c, g: (ch(c), 0))
    return x_spec, b_spec, c_spec, dt_spec, row_spec, st_spec, wide_spec


def _ssd_fwd(xc, dt, a_log, dsk):
    t_len = xc.shape[0]
    nc = t_len // SSD_CHUNK
    x_spec, b_spec, c_spec, dt_spec, row_spec, st_spec, _ = _ssd_specs(False, nc)

    def body(x_ref, b_ref, c_ref, dt_ref, al_ref, dk_ref, y_ref, st_out_ref, st_ref):
        c, g = pl.program_id(0), pl.program_id(1)

        @pl.when(c == 0)
        def _():
            st_ref[g] = jnp.zeros((LANE, GROUP_COLS), f32)

        st = st_ref[g]
        st_out_ref[0, 0] = st
        y, st_new = _ssd_group(x_ref[...], b_ref[...], c_ref[...], dt_ref[...], st, al_ref[...], dk_ref[...], g)
        y_ref[...] = y
        st_ref[g] = st_new

    return pl.pallas_call(
        body, name="ssd_scan_fwd",
        out_shape=[jax.ShapeDtypeStruct((t_len, SSD_D_INNER), f32),
                   jax.ShapeDtypeStruct((nc, SSD_GROUPS, LANE, GROUP_COLS), f32)],
        grid=(nc, SSD_GROUPS), in_specs=[x_spec, b_spec, c_spec, dt_spec, row_spec, row_spec],
        out_specs=[x_spec, st_spec],
        scratch_shapes=[pltpu.VMEM((SSD_GROUPS, LANE, GROUP_COLS), f32)],
        compiler_params=_cparams(("arbitrary", "arbitrary")),
    )(xc, xc, xc, dt, a_log, dsk)


def _ssd_bwd(xc, dt, a_log, dsk, states, dy):
    t_len = xc.shape[0]
    nc = t_len // SSD_CHUNK
    x_spec, b_spec, c_spec, dt_spec, row_spec, st_spec, wide_spec = _ssd_specs(True, nc)

    def body(x_ref, b_ref, c_ref, dt_ref, al_ref, dk_ref, st_in_ref, dy_ref,
             dxc_ref, ddt_ref, dal_ref, ddk_ref, dst_ref):
        c, g = pl.program_id(0), pl.program_id(1)

        @pl.when(c == 0)
        def _():
            dst_ref[g] = jnp.zeros((LANE, GROUP_COLS), f32)

        @pl.when(jnp.logical_and(c == 0, g == 0))
        def _():
            dal_ref[...] = jnp.zeros_like(dal_ref)
            ddk_ref[...] = jnp.zeros_like(ddk_ref)

        @pl.when(g == 0)
        def _():
            ddt_ref[...] = jnp.zeros_like(ddt_ref)

        _, vjp = jax.vjp(lambda *args: _ssd_group(*args, g), x_ref[...], b_ref[...], c_ref[...], dt_ref[...],
                         st_in_ref[0, 0], al_ref[...], dk_ref[...])
        dx, db, dc, ddt, dst, dal, ddk = vjp((dy_ref[...], dst_ref[g]))
        dxc_ref[:, pl.ds(pl.multiple_of(g * GROUP_COLS, GROUP_COLS), GROUP_COLS)] = dx
        dxc_ref[:, pl.ds(pl.multiple_of(SSD_D_INNER + g * LANE, LANE), LANE)] = db
        dxc_ref[:, pl.ds(pl.multiple_of(SSD_D_INNER + SSD_BC + g * LANE, LANE), LANE)] = dc
        ddt_ref[...] += ddt
        dst_ref[g] = dst
        dal_ref[...] += dal
        ddk_ref[...] += ddk

    return pl.pallas_call(
        body, name="ssd_scan_bwd",
        out_shape=[jax.ShapeDtypeStruct((t_len, SSD_CONV_DIM), f32), jax.ShapeDtypeStruct((t_len, LANE), f32),
                   jax.ShapeDtypeStruct((1, LANE), f32), jax.ShapeDtypeStruct((1, LANE), f32)],
        grid=(nc, SSD_GROUPS),
        in_specs=[x_spec, b_spec, c_spec, dt_spec, row_spec, row_spec, st_spec, x_spec],
        out_specs=[wide_spec, dt_spec, row_spec, row_spec],
        scratch_shapes=[pltpu.VMEM((SSD_GROUPS, LANE, GROUP_COLS), f32)],
        compiler_params=_cparams(("arbitrary", "arbitrary")),
    )(xc, xc, xc, dt, a_log, dsk, states, dy)


LRU_ROWS = 256


def _lru_fwd(a, b, gg):
    t_len, cols = a.shape
    rows = min(LRU_ROWS, t_len)
    spec = pl.BlockSpec((rows, cols), lambda i: (i, 0))

    def body(a_ref, b_ref, g_ref, y_ref, h_ref, carry):
        i = pl.program_id(0)

        @pl.when(i == 0)
        def _():
            carry[...] = jnp.zeros_like(carry)

        av, bv = a_ref[...], b_ref[...]
        row = lax.broadcasted_iota(jnp.int32, av.shape, 0)
        s = 1
        while s < rows:
            a_prev = pltpu.roll(av, s, axis=0)
            b_prev = pltpu.roll(bv, s, axis=0)
            m = row >= s
            bv = jnp.where(m, av * b_prev + bv, bv)
            av = jnp.where(m, av * a_prev, av)
            s *= 2
        h = av * carry[0:1, :] + bv
        h_ref[...] = h
        y_ref[...] = g_ref[...] * h
        carry[0:1, :] = h[rows - 1:rows, :]

    return pl.pallas_call(
        body, name="lru_scan_fwd",
        out_shape=[jax.ShapeDtypeStruct((t_len, cols), f32), jax.ShapeDtypeStruct((t_len, cols), f32)],
        grid=(t_len // rows,), in_specs=[spec, spec, spec], out_specs=[spec, spec],
        scratch_shapes=[pltpu.VMEM((SUBLANE, cols), f32)],
        compiler_params=_cparams(("arbitrary",)),
    )(a, b, gg)


def _lru_bwd(dy, gg, a, h):
    t_len, cols = a.shape
    rows = min(LRU_ROWS, t_len)
    n_tiles = t_len // rows
    per = rows // SUBLANE
    spec = pl.BlockSpec((rows, cols), lambda i: (n_tiles - 1 - i, 0))
    prev_spec = pl.BlockSpec((SUBLANE, cols), lambda i: (jnp.maximum((n_tiles - 1 - i) * per - 1, 0), 0))

    def body(dy_ref, g_ref, a_ref, h_ref, hp_ref, da_ref, db_ref, dg_ref, carry_dh, carry_a):
        i = pl.program_id(0)
        tile_id = n_tiles - 1 - i

        @pl.when(i == 0)
        def _():
            carry_dh[...] = jnp.zeros_like(carry_dh)
            carry_a[...] = jnp.zeros_like(carry_a)

        av, hv, dyv = a_ref[...], h_ref[...], dy_ref[...]
        row = lax.broadcasted_iota(jnp.int32, av.shape, 0)
        dg_ref[...] = dyv * hv
        bv = dyv * g_ref[...]
        cv = jnp.where(row == rows - 1, carry_a[0:1, :], pltpu.roll(av, rows - 1, axis=0))
        s = 1
        while s < rows:
            c_next = pltpu.roll(cv, rows - s, axis=0)
            b_next = pltpu.roll(bv, rows - s, axis=0)
            m = row < rows - s
            bv = jnp.where(m, cv * b_next + bv, bv)
            cv = jnp.where(m, cv * c_next, cv)
            s *= 2
        dh = cv * carry_dh[0:1, :] + bv
        h_before = jnp.where(tile_id > 0, hp_ref[SUBLANE - 1:SUBLANE, :], jnp.zeros((1, cols), f32))
        h_prev = jnp.where(row == 0, h_before, pltpu.roll(hv, 1, axis=0))
        da_ref[...] = dh * h_prev
        db_ref[...] = dh
        carry_dh[0:1, :] = dh[0:1, :]
        carry_a[0:1, :] = av[0:1, :]

    return pl.pallas_call(
        body, name="lru_scan_bwd",
        out_shape=[jax.ShapeDtypeStruct((t_len, cols), f32)] * 3,
        grid=(n_tiles,), in_specs=[spec, spec, spec, spec, prev_spec], out_specs=[spec, spec, spec],
        scratch_shapes=[pltpu.VMEM((SUBLANE, cols), f32), pltpu.VMEM((SUBLANE, cols), f32)],
        compiler_params=_cparams(("arbitrary",)),
    )(dy, gg, a, h, h)


def _loss_head(h, target, g):
    t_len = h.shape[0]
    rows = min(512, t_len)

    def f(hv, gv, tv):
        err = _rms(hv, gv) - tv
        return 0.5 * jnp.sum(jnp.mean(err * err, axis=-1, keepdims=True), axis=0, keepdims=True)

    def body(h_ref, t_ref, g_ref, dh_ref, dg_ref, loss_ref):
        i = pl.program_id(0)

        @pl.when(i == 0)
        def _():
            dg_ref[...] = jnp.zeros_like(dg_ref)
            loss_ref[...] = jnp.zeros_like(loss_ref)

        tv = t_ref[...]
        part, vjp = jax.vjp(lambda hv, gv: f(hv, gv, tv), h_ref[...], g_ref[...])
        dh, dg = vjp(jnp.ones((1, 1), f32))
        dh_ref[...] = dh
        dg_ref[...] += dg
        loss_ref[...] += jnp.broadcast_to(part, loss_ref.shape)

    spec = pl.BlockSpec((rows, D_MODEL), lambda i: (i, 0))
    return pl.pallas_call(
        body, name="loss_head",
        out_shape=[jax.ShapeDtypeStruct((t_len, D_MODEL), f32), jax.ShapeDtypeStruct((1, D_MODEL), f32),
                   jax.ShapeDtypeStruct((1, LANE), f32)],
        grid=(t_len // rows,), in_specs=[spec, spec, pl.BlockSpec((1, D_MODEL), lambda i: (0, 0))],
        out_specs=[spec, pl.BlockSpec((1, D_MODEL), lambda i: (0, 0)), pl.BlockSpec((1, LANE), lambda i: (0, 0))],
        compiler_params=_cparams(("arbitrary",)),
    )(h, target, g)


def _as2d(a):
    return a.reshape((-1, a.shape[-1])) if a.ndim > 1 else a.reshape((1, -1))


def _row_block(rows, cols, bytes_cap=1 << 20):
    if rows * cols * 4 <= bytes_cap or rows % SUBLANE:
        return rows
    return _tile(rows, max(SUBLANE, (bytes_cap // (cols * 4)) // SUBLANE * SUBLANE), SUBLANE)


def _adamw(w, g, m, v, name):
    shape = w.shape
    w2, g2, m2, v2 = _as2d(w), _as2d(g), _as2d(m), _as2d(v)
    rows, cols = w2.shape
    rb = _row_block(rows, cols)

    def body(w_ref, g_ref, m_ref, v_ref, d_ref, nm_ref, nv_ref):
        gv = g_ref[...]
        nm = ADAM_B1 * m_ref[...] + (1.0 - ADAM_B1) * gv
        nv = ADAM_B2 * v_ref[...] + (1.0 - ADAM_B2) * jnp.square(gv)
        m_hat = nm / (1.0 - ADAM_B1 ** ADAM_STEP)
        v_hat = nv / (1.0 - ADAM_B2 ** ADAM_STEP)
        d_ref[...] = -ADAM_LR * (m_hat / (jnp.sqrt(v_hat) + ADAM_EPS) + ADAM_WD * w_ref[...])
        nm_ref[...] = nm
        nv_ref[...] = nv

    spec = pl.BlockSpec((rb, cols), lambda i: (i, 0))
    d, nm, nv = pl.pallas_call(
        body, name=name, out_shape=[jax.ShapeDtypeStruct((rows, cols), f32)] * 3,
        grid=(rows // rb,), in_specs=[spec] * 4, out_specs=[spec] * 3,
        compiler_params=_cparams(("parallel",)),
    )(w2, g2, m2, v2)
    return d.reshape(shape), nm.reshape(shape), nv.reshape(shape)


def _sum_with_sibling(g_halves, theirs, c_idx):
    n_sh, _, rows, cols = g_halves.shape
    rb = _tile(rows, 512, 2 * SUBLANE)

    def body(c_ref, mine_ref, theirs_ref, o_ref):
        o_ref[...] = (mine_ref[...] + theirs_ref[...]).astype(bf16)

    grid_spec = pltpu.PrefetchScalarGridSpec(
        num_scalar_prefetch=1, grid=(n_sh, rows // rb),
        in_specs=[pl.BlockSpec((None, None, rb, cols), lambda k, i, c_ref: (k, c_ref[0], i, 0)),
                  pl.BlockSpec((None, rb, cols), lambda k, i, c_ref: (k, i, 0))],
        out_specs=pl.BlockSpec((None, rb, cols), lambda k, i, c_ref: (k, i, 0)))
    return pl.pallas_call(
        body, name="grad_sum_sibling", out_shape=jax.ShapeDtypeStruct((n_sh, rows, cols), bf16),
        grid_spec=grid_spec, compiler_params=_cparams(("parallel", "parallel")),
    )(c_idx, g_halves, theirs)


def _sum_chips(partial, received, k_idx):
    _, rows, cols = partial.shape
    rb = _tile(rows, 512, 2 * SUBLANE)

    def body(k_ref, mine_ref, r_ref, o_ref):
        acc = mine_ref[...].astype(f32)
        for j in range(N_CHIPS - 1):
            acc = acc + r_ref[j].astype(f32)
        o_ref[...] = acc

    grid_spec = pltpu.PrefetchScalarGridSpec(
        num_scalar_prefetch=1, grid=(rows // rb,),
        in_specs=[pl.BlockSpec((None, rb, cols), lambda i, k_ref: (k_ref[0], i, 0)),
                  pl.BlockSpec((N_CHIPS - 1, rb, cols), lambda i, k_ref: (0, i, 0))],
        out_specs=pl.BlockSpec((rb, cols), lambda i, k_ref: (i, 0)))
    return pl.pallas_call(
        body, name="grad_sum_chips", out_shape=jax.ShapeDtypeStruct((rows, cols), f32),
        grid_spec=grid_spec, compiler_params=_cparams(("parallel",)),
    )(k_idx, partial, received)


HBM_SPEC = pl.BlockSpec(memory_space=pltpu.HBM)
CHIP_FLIPS = ((0, 1), (1, 0), (1, 1))


def _position():
    return lax.axis_index("x"), lax.axis_index("y"), lax.axis_index("c")


def _own_slot(gathered, mine, index):
    return [lax.dynamic_update_index_in_dim(g, m, index, 0) for g, m in zip(gathered, mine)]


def _gather_weights(blocks):
    n = len(blocks)
    n_far = len(CHIP_FLIPS)

    def body(*refs):
        srcs, outs = refs[:n], refs[n:2 * n]
        send_sems, recv_sems = refs[2 * n:]
        x, y, c = _position()
        k = 2 * x + y
        sibling = (x, y, 1 - c)
        first, passed = [], []
        for a in range(n):
            for j, (fx, fy) in enumerate(CHIP_FLIPS):
                s = a * 2 * n_far + j
                cp = pltpu.make_async_remote_copy(
                    src_ref=srcs[a].at[c], dst_ref=outs[a].at[k, c], send_sem=send_sems.at[s],
                    recv_sem=recv_sems.at[s], device_id=(x ^ fx, y ^ fy, c), device_id_type=MESH)
                cp.start()
                first.append(cp)
        for a in range(n):
            for j, (fx, fy) in enumerate(CHIP_FLIPS):
                s = a * 2 * n_far + j
                kk = 2 * (x ^ fx) + (y ^ fy)
                first[a * n_far + j].wait_recv()
                cp = pltpu.make_async_remote_copy(
                    src_ref=outs[a].at[kk, c], dst_ref=outs[a].at[kk, c], send_sem=send_sems.at[s + n_far],
                    recv_sem=recv_sems.at[s + n_far], device_id=sibling, device_id_type=MESH)
                cp.start()
                passed.append(cp)
        for cp in passed:
            cp.wait_recv()
        for cp in first + passed:
            cp.wait_send()

    return pl.pallas_call(
        body, name="gather_weights",
        out_shape=[jax.ShapeDtypeStruct((N_CHIPS,) + b.shape, b.dtype) for b in blocks],
        in_specs=[HBM_SPEC] * n, out_specs=[HBM_SPEC] * n,
        scratch_shapes=[pltpu.SemaphoreType.DMA((2 * n_far * n,)), pltpu.SemaphoreType.DMA((2 * n_far * n,))],
    )(*blocks)


def _swap_with_sibling(grads):
    n = len(grads)

    def body(*refs):
        srcs, outs = refs[:n], refs[n:2 * n]
        send_sems, recv_sems = refs[2 * n:]
        x, y, c = _position()
        copies = []
        for a in range(n):
            for kk in range(N_CHIPS):
                s = a * N_CHIPS + kk
                cp = pltpu.make_async_remote_copy(
                    src_ref=srcs[a].at[kk, 1 - c], dst_ref=outs[a].at[kk], send_sem=send_sems.at[s],
                    recv_sem=recv_sems.at[s], device_id=(x, y, 1 - c), device_id_type=MESH)
                cp.start()
                copies.append(cp)
        for cp in copies:
            cp.wait()

    return pl.pallas_call(
        body, name="grad_swap_sibling",
        out_shape=[jax.ShapeDtypeStruct((N_CHIPS,) + g.shape[2:], g.dtype) for g in grads],
        in_specs=[HBM_SPEC] * n, out_specs=[HBM_SPEC] * n,
        scratch_shapes=[pltpu.SemaphoreType.DMA((N_CHIPS * n,)), pltpu.SemaphoreType.DMA((N_CHIPS * n,))],
    )(*grads)


def _send_to_chips(partials):
    n = len(partials)
    n_far = len(CHIP_FLIPS)

    def body(*refs):
        srcs, outs = refs[:n], refs[n:2 * n]
        send_sems, recv_sems = refs[2 * n:]
        x, y, c = _position()
        copies = []
        for a in range(n):
            for j, (fx, fy) in enumerate(CHIP_FLIPS):
                s = a * n_far + j
                kk = 2 * (x ^ fx) + (y ^ fy)
                cp = pltpu.make_async_remote_copy(
                    src_ref=srcs[a].at[kk], dst_ref=outs[a].at[j], send_sem=send_sems.at[s],
                    recv_sem=recv_sems.at[s], device_id=(x ^ fx, y ^ fy, c), device_id_type=MESH)
                cp.start()
                copies.append(cp)
        for cp in copies:
            cp.wait()

    return pl.pallas_call(
        body, name="grad_to_chips",
        out_shape=[jax.ShapeDtypeStruct((n_far,) + p.shape[1:], p.dtype) for p in partials],
        in_specs=[HBM_SPEC] * n, out_specs=[HBM_SPEC] * n,
        scratch_shapes=[pltpu.SemaphoreType.DMA((n_far * n,)), pltpu.SemaphoreType.DMA((n_far * n,))],
    )(*partials)


def _join_halves(halves):
    n = len(halves)

    def body(*refs):
        srcs, outs = refs[:n], refs[n:2 * n]
        send_sems, recv_sems = refs[2 * n:]
        x, y, c = _position()
        copies = []
        for a in range(n):
            cp = pltpu.make_async_remote_copy(
                src_ref=srcs[a], dst_ref=outs[a].at[c], send_sem=send_sems.at[a], recv_sem=recv_sems.at[a],
                device_id=(x, y, 1 - c), device_id_type=MESH)
            cp.start()
            copies.append(cp)
        for cp in copies:
            cp.wait()

    return pl.pallas_call(
        body, name="grad_join_halves",
        out_shape=[jax.ShapeDtypeStruct((2,) + h.shape, h.dtype) for h in halves],
        in_specs=[HBM_SPEC] * n, out_specs=[HBM_SPEC] * n,
        scratch_shapes=[pltpu.SemaphoreType.DMA((n,)), pltpu.SemaphoreType.DMA((n,))],
    )(*halves)


def _all_sum_small(vec):
    rows, cols = vec.shape

    def body(v_ref, o_ref, buf, send_sems, recv_sems):
        x, y, c = _position()
        me = 4 * x + 2 * y + c
        buf[me] = v_ref[...]
        copies = []
        for m in range(1, N_DEV):
            fx, fy, fc = (m >> 2) & 1, (m >> 1) & 1, m & 1
            cp = pltpu.make_async_remote_copy(
                src_ref=v_ref, dst_ref=buf.at[me], send_sem=send_sems.at[m - 1], recv_sem=recv_sems.at[m - 1],
                device_id=(x ^ fx, y ^ fy, c ^ fc), device_id_type=MESH)
            cp.start()
            copies.append(cp)
        for cp in copies:
            cp.wait()
        acc = buf[0]
        for d in range(1, N_DEV):
            acc = acc + buf[d]
        o_ref[...] = acc

    return pl.pallas_call(
        body, name="all_sum_small", out_shape=jax.ShapeDtypeStruct((rows, cols), f32),
        in_specs=[pl.BlockSpec(memory_space=pltpu.VMEM)], out_specs=pl.BlockSpec(memory_space=pltpu.VMEM),
        scratch_shapes=[pltpu.VMEM((N_DEV, rows, cols), f32), pltpu.SemaphoreType.DMA((N_DEV - 1,)),
                        pltpu.SemaphoreType.DMA((N_DEV - 1,))],
        compiler_params=_cparams(),
    )(vec)


FLAT_QUANTUM = 2 * 2 * SUBLANE * FLAT_COLS


def _pack(arrays, dtype):
    flat = jnp.concatenate([a.astype(dtype).reshape(-1) for a in arrays])
    n = flat.shape[0]
    n_pad = -(-n // FLAT_QUANTUM) * FLAT_QUANTUM
    return jnp.pad(flat, (0, n_pad - n))


def _unpack(flat, shapes):
    out, off = [], 0
    for s in shapes:
        n = int(np.prod(s))
        out.append(flat[..., off:off + n].reshape(flat.shape[:-1] + tuple(s)))
        off += n
    return out


def _full_from_shards(stacked, axis):
    return jnp.concatenate([stacked[k] for k in range(N_CHIPS)], axis=axis)


def _shards_of(full, axis):
    return jnp.stack(jnp.split(full, N_CHIPS, axis=axis))


def _ffn_fwd(h, p):
    u = _row_fwd(_f_rms, [h], [p['g']], [(D_MODEL, bf16)], rows=512, name="ffn_norm")[0]
    a = _mm_w(u, p['up'], 'nn', "ffn_up")
    both = (0, FFN_H)
    gated = _strip_fwd(_f_ffn_gate, [(a, both, True)], [(p['cw'], both), (p['cb'], both)], [(FFN_H, bf16, (0,))],
                       rows=256, rs=FFN_STRIP_ROWS, cs=LANE, width=FFN_H, name="ffn_gate", halo=SUBLANE)[0]
    h_out = _mm(gated, p['down'], 'nn', "ffn_down", add=h)
    return h_out, (h, u, a, gated)


def _ffn_bwd(dh_out, p, saved, bias_zero):
    h, u, a, gated = saved
    d_gated = _mm(dh_out, p['down'], 'nt', "ffn_down_dx", out_dtype=bf16)
    d_down = _mm(gated, dh_out, 'tn', "ffn_down_dw")
    both = (0, FFN_H)
    da, d_cw, d_cb = _strip_bwd(_f_ffn_gate, [(a, both, True)], [(p['cw'], both), (p['cb'], both)],
                                [(d_gated, (0,))], rows=256, rs=FFN_STRIP_ROWS, cs=LANE, width=FFN_H,
                                name="ffn_gate_bwd", halo=SUBLANE, tile_dtypes=[bf16])
    d_up = _mm(u, da, 'tn', "ffn_up_dw", out_cols_sharded=True)
    du = _mm_w(da, p['up'], 'nt', "ffn_up_dx", out_dtype=bf16)
    dh, d_g, d_bias = _row_bwd(_f_rms_res, [h], [p['g'], bias_zero], [du, dh_out], rows=512, name="ffn_norm_bwd")
    return dh, {'g': d_g, 'up': d_up, 'down': d_down, 'cw': d_cw, 'cb': d_cb}, d_bias


def _mixer_norm_bwd(h, g, du, dh_res, name):
    def f(hv, gv):
        return _rms(hv, gv), hv

    dh, d_g = _row_bwd(f, [h], [g], [du, dh_res], rows=512, name=name)
    return dh, d_g


def _ssd_layer_fwd(h, p):
    u = _row_fwd(_f_rms, [h], [p['g']], [(D_MODEL, bf16)], rows=512, name="ssd_norm")[0]
    z = _mm(u, p['w_z'], 'nn', "ssd_in_z")
    xbc = _mm(u, p['w_xbc'], 'nn', "ssd_in_xbc")
    dtr = _mm(u, p['w_dt'], 'nn', "ssd_in_dt")
    xc = _strip_fwd(_f_ssd_conv, *_ssd_conv_args(xbc, p), [(SSD_CONV_DIM, f32, (0,))], rows=256, rs=SSD_STRIP_ROWS,
                    cs=LANE, width=SSD_CONV_DIM, name="ssd_conv", halo=SUBLANE)[0]
    dt = _row_fwd(_f_ssd_dt, [dtr], [p['dtb']], [(LANE, f32)], rows=1024, name="ssd_dt")[0]
    y, states = _ssd_fwd(xc, dt, p['a_log'], p['dsk'])
    yn = _row_fwd(_f_ssd_post, [y, z], [p['norm']], [(SSD_D_INNER, bf16)], rows=256, name="ssd_gate_norm")[0]
    h_out = _mm(yn, p['out'], 'nn', "ssd_out", add=h)
    return h_out, (h, u, z, xbc, dtr, xc, dt, states, y, yn)


def _ssd_conv_args(xbc, p):
    return [(xbc, (0,), True)], [(p['cw'], (0,)), (p['cb'], (0,))]


def _ssd_layer_bwd(dh_out, p, saved):
    h, u, z, xbc, dtr, xc, dt, states, y, yn = saved
    d_yn = _mm(dh_out, p['out'], 'nt', "ssd_out_dx", out_dtype=bf16)
    d_out = _mm(yn, dh_out, 'tn', "ssd_out_dw")
    dy, dz, d_norm = _row_bwd(_f_ssd_post, [y, z], [p['norm']], [d_yn], rows=256, name="ssd_gate_norm_bwd",
                              tile_dtypes=[f32, bf16])
    dxc, ddt, d_alog, d_dsk = _ssd_bwd(xc, dt, p['a_log'], p['dsk'], states, dy)
    dxbc, d_cw, d_cb = _strip_bwd(_f_ssd_conv, *_ssd_conv_args(xbc, p), [(dxc, (0,))], rows=256, rs=SSD_STRIP_ROWS,
                                  cs=LANE, width=SSD_CONV_DIM, name="ssd_conv_bwd", halo=SUBLANE,
                                  tile_dtypes=[bf16])
    ddtr, d_dtb = _row_bwd(_f_ssd_dt, [dtr], [p['dtb']], [ddt], rows=1024, name="ssd_dt_bwd", tile_dtypes=[bf16])
    d_wz = _mm(u, dz, 'tn', "ssd_in_z_dw")
    d_wxbc = _mm(u, dxbc, 'tn', "ssd_in_xbc_dw")
    d_wdt = _mm(u, ddtr, 'tn', "ssd_in_dt_dw")
    du = _mm(dz, p['w_z'], 'nt', "ssd_in_z_dx")
    du = _mm(dxbc, p['w_xbc'], 'nt', "ssd_in_xbc_dx", add=du)
    du = _mm(ddtr, p['w_dt'], 'nt', "ssd_in_dt_dx", add=du, out_dtype=bf16)
    dh, d_g = _mixer_norm_bwd(h, p['g'], du, dh_out, "ssd_norm_bwd")
    grads = {'g': d_g, 'w_z': d_wz, 'w_xbc': d_wxbc, 'w_dt': d_wdt, 'cw': d_cw, 'cb': d_cb, 'dtb': d_dtb,
             'a_log': d_alog, 'dsk': d_dsk, 'norm': d_norm, 'out': d_out}
    return dh, grads


def _conf_layer_fwd(h, p):
    u = _row_fwd(_f_rms, [h], [p['g']], [(D_MODEL, bf16)], rows=512, name="conf_norm")[0]
    g2 = _mm_w(u, p['pw1'], 'nn', "conf_pw1")
    conv = _strip_fwd(_f_conf_conv, *_conf_conv_args(g2, p), [(D_MODEL, f32, (0,))], rows=256, rs=CONF_STRIP_ROWS,
                      cs=LANE, width=D_MODEL, name="conf_conv", halo=CONF_HALO)[0]
    s = _row_fwd(_f_ln_silu, [conv], [p['ln_g'], p['ln_b']], [(D_MODEL, bf16)], rows=256, name="conf_ln")[0]
    h_out = _mm(s, p['pw2'], 'nn', "conf_pw2", bias=p['b2'], add=h)
    return h_out, (h, u, g2, conv, s)


def _conf_conv_args(g2, p):
    halves = (0, D_MODEL)
    return [(g2, halves, True)], [(p['b1'], halves), (p['dw_w'], (0,)), (p['dw_b'], (0,))]


def _conf_layer_bwd(dh_out, p, saved):
    h, u, g2, conv, s = saved
    ds = _mm(dh_out, p['pw2'], 'nt', "conf_pw2_dx", out_dtype=bf16)
    d_pw2 = _mm(s, dh_out, 'tn', "conf_pw2_dw")
    d_conv, d_lng, d_lnb = _row_bwd(_f_ln_silu, [conv], [p['ln_g'], p['ln_b']], [ds], rows=256, name="conf_ln_bwd")
    dg2, d_b1, d_dww, d_dwb = _strip_bwd(_f_conf_conv, *_conf_conv_args(g2, p), [(d_conv, (0,))], rows=256,
                                         rs=CONF_STRIP_ROWS, cs=LANE, width=D_MODEL, name="conf_conv_bwd",
                                         halo=CONF_HALO, tile_dtypes=[bf16])
    d_pw1 = _mm(u, dg2, 'tn', "conf_pw1_dw", out_cols_sharded=True)
    du = _mm_w(dg2, p['pw1'], 'nt', "conf_pw1_dx", out_dtype=bf16)
    dh, d_g = _mixer_norm_bwd(h, p['g'], du, dh_out, "conf_norm_bwd")
    grads = {'g': d_g, 'pw1': d_pw1, 'b1': d_b1, 'dw_w': d_dww, 'dw_b': d_dwb, 'ln_g': d_lng, 'ln_b': d_lnb,
             'pw2': d_pw2}
    return dh, grads


def _lru_params(p):
    return [p['in_b'], p['cw'], p['cb'], p['ga_w'], p['ga_b'], p['gx_w'], p['gx_b'], p['lam']]


def _lru_layer_fwd(h, p):
    u = _row_fwd(_f_rms, [h], [p['g']], [(D_MODEL, bf16)], rows=512, name="lru_norm")[0]
    io = _mm_w(u, p['in_w'], 'nn', "lru_in")
    a, b, gg = _row_fwd(_f_lru, [io], _lru_params(p), [(LRU_W, f32)] * 3, rows=256, name="lru_gates",
                        halo=SUBLANE, halo_of=[True])
    y, hs = _lru_fwd(a, b, gg)
    h_out = _mm(y, p['out'], 'nn', "lru_out", bias=p['out_b'], add=h)
    return h_out, (h, u, io, a, gg, hs, y)


def _lru_layer_bwd(dh_out, p, saved):
    h, u, io, a, gg, hs, y = saved
    dy = _mm(dh_out, p['out'], 'nt', "lru_out_dx")
    d_out = _mm(y, dh_out, 'tn', "lru_out_dw")
    da, db, dgg = _lru_bwd(dy, gg, a, hs)
    res = _row_bwd(_f_lru, [io], _lru_params(p), [da, db, dgg], rows=256, name="lru_gates_bwd",
                   halo=SUBLANE, halo_of=[True], tile_dtypes=[bf16])
    dio, d_inb, d_cw, d_cb, d_gaw, d_gab, d_gxw, d_gxb, d_lam = res
    d_inw = _mm(u, dio, 'tn', "lru_in_dw", out_cols_sharded=True)
    du = _mm_w(dio, p['in_w'], 'nt', "lru_in_dx", out_dtype=bf16)
    dh, d_g = _mixer_norm_bwd(h, p['g'], du, dh_out, "lru_norm_bwd")
    grads = {'g': d_g, 'in_w': d_inw, 'in_b': d_inb, 'cw': d_cw, 'cb': d_cb, 'ga_w': d_gaw, 'ga_b': d_gab,
             'gx_w': d_gxw, 'gx_b': d_gxb, 'lam': d_lam, 'out': d_out}
    return dh, grads


def _sgu_params(p):
    return [p['in_b'], p['ln_g'], p['ln_b'], p['sp_w'], p['sp_bt']]


def _sgu_layer_fwd(h, p):
    u = _row_fwd(_f_rms, [h], [p['g']], [(D_MODEL, bf16)], rows=512, name="sgu_norm")[0]
    z = _mm_w(u, p['in_w'], 'nn', "sgu_in")
    s = _row_fwd(_f_sgu, [z], _sgu_params(p), [(SGU_HALF, bf16)], rows=SGU_CHUNK, name="sgu_mix")[0]
    h_out = _mm(s, p['out'], 'nn', "sgu_out", bias=p['out_b'], add=h)
    return h_out, (h, u, z, s)


def _sgu_layer_bwd(dh_out, p, saved):
    h, u, z, s = saved
    ds = _mm(dh_out, p['out'], 'nt', "sgu_out_dx", out_dtype=bf16)
    d_out = _mm(s, dh_out, 'tn', "sgu_out_dw")
    dz, d_inb, d_lng, d_lnb, d_spw, d_spbt = _row_bwd(_f_sgu, [z], _sgu_params(p), [ds], rows=SGU_CHUNK,
                                                      name="sgu_mix_bwd", tile_dtypes=[bf16])
    d_inw = _mm(u, dz, 'tn', "sgu_in_dw", out_cols_sharded=True)
    du = _mm_w(dz, p['in_w'], 'nt', "sgu_in_dx", out_dtype=bf16)
    dh, d_g = _mixer_norm_bwd(h, p['g'], du, dh_out, "sgu_norm_bwd")
    grads = {'g': d_g, 'in_w': d_inw, 'in_b': d_inb, 'ln_g': d_lng, 'ln_b': d_lnb, 'sp_w': d_spw, 'sp_bt': d_spbt,
             'out': d_out}
    return dh, grads


def _row(v):
    return v.reshape((1, -1)).astype(f32)


def _pad_lanes(v, n=LANE):
    v = _row(v)
    return jnp.pad(v, ((0, 0), (0, n - v.shape[1])))


def _local_step(x, target, w):
    ffn = [{'g': _row(w['norm_ffn'][i]), 'up': (w['f_up_w'], i), 'down': w['f_down_w'][i],
            'cw': w['f_conv_w'][i].astype(f32), 'cb': _row(w['f_conv_b'][i])} for i in range(DEPTH)]
    a_in = w['a_in_proj'][0]
    pa = {'g': _row(w['norm_mix'][0]), 'w_z': a_in[:, :SSD_D_INNER],
          'w_xbc': a_in[:, SSD_D_INNER:SSD_D_INNER + SSD_CONV_DIM],
          'w_dt': jnp.pad(a_in[:, SSD_D_INNER + SSD_CONV_DIM:], ((0, 0), (0, LANE - SSD_HEADS))),
          'cw': w['a_conv_w'][0].astype(f32), 'cb': _row(w['a_conv_b'][0]), 'dtb': _pad_lanes(w['a_dt_bias'][0]),
          'a_log': _pad_lanes(w['a_log'][0]), 'dsk': _pad_lanes(w['a_d_skip'][0]), 'norm': _row(w['a_norm'][0]),
          'out': w['a_out_proj']}
    pb = {'g': _row(w['norm_mix'][1]), 'pw1': (w['b_pw1_w'], 0), 'b1': _row(w['b_pw1_b'][0]),
          'dw_w': w['b_dw_w'][0].astype(f32), 'dw_b': _row(w['b_dw_b'][0]), 'ln_g': _row(w['b_ln_g'][0]),
          'ln_b': _row(w['b_ln_b'][0]), 'pw2': w['b_pw2_w'], 'b2': _row(w['b_pw2_b'][0])}
    pc = {'g': _row(w['norm_mix'][2]), 'in_w': (w['c_in_w'], 0), 'in_b': _row(w['c_in_b'][0]),
          'cw': w['c_conv_w'][0].astype(f32), 'cb': _row(w['c_conv_b'][0]),
          'ga_w': w['c_ga_w'][0].reshape(LRU_W, LRU_BLOCK).astype(f32), 'ga_b': _row(w['c_ga_b'][0]),
          'gx_w': w['c_gx_w'][0].reshape(LRU_W, LRU_BLOCK).astype(f32), 'gx_b': _row(w['c_gx_b'][0]),
          'lam': _row(w['c_lambda'][0]), 'out': w['c_out_w'], 'out_b': _row(w['c_out_b'][0])}
    pd = {'g': _row(w['norm_mix'][3]), 'in_w': (w['d_in_w'], 0), 'in_b': _row(w['d_in_b'][0]),
          'ln_g': _row(w['d_ln_g'][0]), 'ln_b': _row(w['d_ln_b'][0]),
          'sp_w': w['d_sp_w'][0].reshape(SGU_GROUPS * SGU_CHUNK, SGU_CHUNK).astype(f32),
          'sp_bt': w['d_sp_b'][0].astype(f32).T, 'out': w['d_out_w'], 'out_b': _row(w['d_out_b'][0])}
    mixers = [(_ssd_layer_fwd, _ssd_layer_bwd, pa), (_conf_layer_fwd, _conf_layer_bwd, pb),
              (_lru_layer_fwd, _lru_layer_bwd, pc), (_sgu_layer_fwd, _sgu_layer_bwd, pd)]

    h = x
    saved = []
    for i in range(DEPTH):
        fwd, _, p = mixers[i]
        h, s_mix = fwd(h, p)
        h, s_ffn = _ffn_fwd(h, ffn[i])
        saved.append((s_mix, s_ffn))
    dh, d_final, loss = _loss_head(h, target, _row(w['norm_final']))

    bias_zero = jnp.zeros((1, D_MODEL), f32)
    g_ffn, g_mix, d_out_bias = [None] * DEPTH, [None] * DEPTH, [None] * DEPTH
    for i in reversed(range(DEPTH)):
        _, bwd, p = mixers[i]
        dh, g_ffn[i], d_out_bias[i] = _ffn_bwd(dh, ffn[i], saved[i][1], bias_zero)
        dh, g_mix[i] = bwd(dh, p, saved[i][0])
    ga, gb, gc, gd = g_mix

    def rows_sharded(g):
        return g.reshape(N_CHIPS, g.shape[0] // N_CHIPS, g.shape[1])

    grads = {
        'norm_mix': jnp.concatenate([g['g'] for g in g_mix], axis=0),
        'norm_ffn': jnp.concatenate([g['g'] for g in g_ffn], axis=0),
        'norm_final': d_final.reshape(-1),
        'a_in_proj': jnp.concatenate([ga['w_z'], ga['w_xbc'], ga['w_dt'][:, :SSD_HEADS]], axis=1)[None],
        'a_conv_w': ga['cw'][None], 'a_conv_b': ga['cb'], 'a_dt_bias': ga['dtb'][:, :SSD_HEADS],
        'a_log': ga['a_log'][:, :SSD_HEADS], 'a_d_skip': ga['dsk'][:, :SSD_HEADS], 'a_norm': ga['norm'],
        'a_out_proj': rows_sharded(ga['out']),
        'b_pw1_w': gb['pw1'], 'b_pw1_b': gb['b1'], 'b_dw_w': gb['dw_w'][None], 'b_dw_b': gb['dw_b'],
        'b_ln_g': gb['ln_g'], 'b_ln_b': gb['ln_b'], 'b_pw2_w': rows_sharded(gb['pw2']), 'b_pw2_b': d_out_bias[1],
        'c_in_w': gc['in_w'], 'c_in_b': gc['in_b'], 'c_conv_w': gc['cw'][None], 'c_conv_b': gc['cb'],
        'c_ga_w': gc['ga_w'].reshape(1, LRU_W // LRU_BLOCK, LRU_BLOCK, LRU_BLOCK),
        'c_ga_b': gc['ga_b'].reshape(1, LRU_W // LRU_BLOCK, LRU_BLOCK),
        'c_gx_w': gc['gx_w'].reshape(1, LRU_W // LRU_BLOCK, LRU_BLOCK, LRU_BLOCK),
        'c_gx_b': gc['gx_b'].reshape(1, LRU_W // LRU_BLOCK, LRU_BLOCK),
        'c_lambda': gc['lam'], 'c_out_w': rows_sharded(gc['out']), 'c_out_b': d_out_bias[2],
        'd_in_w': gd['in_w'], 'd_in_b': gd['in_b'], 'd_ln_g': gd['ln_g'], 'd_ln_b': gd['ln_b'],
        'd_sp_w': gd['sp_w'].reshape(1, SGU_GROUPS, SGU_CHUNK, SGU_CHUNK), 'd_sp_b': gd['sp_bt'].T[None],
        'd_out_w': rows_sharded(gd['out']), 'd_out_b': d_out_bias[3],
        'f_up_w': [g['up'] for g in g_ffn], 'f_conv_w': jnp.stack([g['cw'] for g in g_ffn]),
        'f_conv_b': jnp.concatenate([g['cb'] for g in g_ffn], axis=0),
        'f_down_w': [rows_sharded(g['down']) for g in g_ffn],
    }
    return loss, dh, grads


def _global_shape(name, shard_shape):
    ax = SHARD_AXIS[name]
    if ax is None:
        return tuple(shard_shape)
    s = list(shard_shape)
    s[ax] *= N_CHIPS
    return tuple(s)


def _step(x, target, weights, moments_m, moments_v):
    x2, t2 = x[0], target[0]
    shard_shapes = {n: weights[n].shape for n in WEIGHTS}
    c_pos = lax.axis_index("c")
    k_pos = 2 * lax.axis_index("x") + lax.axis_index("y")
    c_idx = c_pos.astype(jnp.int32).reshape(1)
    k_idx = k_pos.astype(jnp.int32).reshape(1)

    def halves_of(a):
        a2 = _as2d(a)
        return a2.reshape(2, a2.shape[0] // 2, a2.shape[1])

    mine = [halves_of(weights[n].astype(bf16)) for n in DIRECT]
    mine.append(_pack([weights[n] for n in PACKED_MM], bf16).reshape(2, -1, FLAT_COLS))
    mine.append(_pack([weights[n] for n in SHARDED_VEC], f32).reshape(2, -1, FLAT_COLS))
    gathered = _own_slot(_gather_weights(mine), mine, k_pos)
    w = {n: weights[n] for n in REPLICATED}
    for n, g in zip(DIRECT, gathered):
        g = g.reshape((N_CHIPS,) + shard_shapes[n])
        if n in DIRECT_COLS:
            w[n] = g
        elif n == 'f_down_w':
            w[n] = [g[:, i].reshape(-1, g.shape[-1]) for i in range(DEPTH)]
        else:
            w[n] = g.reshape(-1, g.shape[-1])
    all_mm = _unpack(gathered[-2].reshape(N_CHIPS, -1), [shard_shapes[n] for n in PACKED_MM])
    all_vec = _unpack(gathered[-1].reshape(N_CHIPS, -1), [shard_shapes[n] for n in SHARDED_VEC])
    for n, st in zip(PACKED_MM + SHARDED_VEC, all_mm + all_vec):
        w[n] = _full_from_shards(st, SHARD_AXIS[n])

    loss_part, dx, grads = _local_step(x2, t2, w)

    direct = []
    for n in DIRECT:
        direct += grads[n] if isinstance(grads[n], list) else [grads[n]]
    packed = [_shards_of(grads[n].reshape(_global_shape(n, shard_shapes[n])), SHARD_AXIS[n]).reshape(N_CHIPS, -1)
              for n in PACKED_MM + SHARDED_VEC]
    flat = jnp.concatenate(packed, axis=1)
    n_flat = flat.shape[1]
    n_pad = -(-n_flat // FLAT_QUANTUM) * FLAT_QUANTUM
    flat = jnp.pad(flat, ((0, 0), (0, n_pad - n_flat))).reshape(N_CHIPS, -1, FLAT_COLS)
    mine_g = [g.reshape(N_CHIPS, 2, g.shape[1] // 2, g.shape[2]) for g in direct + [flat]]
    theirs = _swap_with_sibling(mine_g)
    partials = [_sum_with_sibling(g, t, c_idx) for g, t in zip(mine_g, theirs)]
    received = _send_to_chips(partials)
    my_halves = [_sum_chips(p, r, k_idx) for p, r in zip(partials, received)]
    joined = _own_slot(_join_halves(my_halves), my_halves, c_pos)
    g_shard, pos = {}, 0
    for n in DIRECT:
        layers = shard_shapes[n][0]
        g_shard[n] = jnp.stack([j.reshape(shard_shapes[n][1:]) for j in joined[pos:pos + layers]])
        pos += layers
    flat_shapes = [shard_shapes[n] for n in PACKED_MM + SHARDED_VEC]
    g_shard.update(zip(PACKED_MM + SHARDED_VEC, _unpack(joined[-1].reshape(-1), flat_shapes)))

    small = jnp.concatenate([grads[n].reshape(-1) for n in REPLICATED] + [loss_part.reshape(-1)[:1]])
    n_small = small.shape[0]
    n_small_pad = -(-n_small // (SUBLANE * FLAT_COLS)) * (SUBLANE * FLAT_COLS)
    small = jnp.pad(small, (0, n_small_pad - n_small)).reshape(-1, FLAT_COLS)
    small = _all_sum_small(small).reshape(-1)
    g_rep = dict(zip(REPLICATED, _unpack(small, [shard_shapes[n] for n in REPLICATED])))
    loss = small[n_small - 1]

    g_all = {**g_shard, **g_rep}
    delta, new_m, new_v = {}, {}, {}
    for n in WEIGHTS:
        delta[n], new_m[n], new_v[n] = _adamw(weights[n], g_all[n], moments_m[n], moments_v[n], "adamw_" + n)
    return loss, dx[None], g_all, delta, new_m, new_v


def kernel(x, norm_mix, norm_ffn, norm_final, a_in_proj, a_conv_w, a_conv_b, a_dt_bias, a_log, a_d_skip, a_norm, a_out_proj, b_pw1_w, b_pw1_b, b_dw_w, b_dw_b, b_ln_g, b_ln_b, b_pw2_w, b_pw2_b, c_in_w, c_in_b, c_conv_w, c_conv_b, c_ga_w, c_ga_b, c_gx_w, c_gx_b, c_lambda, c_out_w, c_out_b, d_in_w, d_in_b, d_ln_g, d_ln_b, d_sp_w, d_sp_b, d_out_w, d_out_b, f_up_w, f_conv_w, f_conv_b, f_down_w, loss_target, m_norm_mix, m_norm_ffn, m_norm_final, m_a_in_proj, m_a_conv_w, m_a_conv_b, m_a_dt_bias, m_a_log, m_a_d_skip, m_a_norm, m_a_out_proj, m_b_pw1_w, m_b_pw1_b, m_b_dw_w, m_b_dw_b, m_b_ln_g, m_b_ln_b, m_b_pw2_w, m_b_pw2_b, m_c_in_w, m_c_in_b, m_c_conv_w, m_c_conv_b, m_c_ga_w, m_c_ga_b, m_c_gx_w, m_c_gx_b, m_c_lambda, m_c_out_w, m_c_out_b, m_d_in_w, m_d_in_b, m_d_ln_g, m_d_ln_b, m_d_sp_w, m_d_sp_b, m_d_out_w, m_d_out_b, m_f_up_w, m_f_conv_w, m_f_conv_b, m_f_down_w, v_norm_mix, v_norm_ffn, v_norm_final, v_a_in_proj, v_a_conv_w, v_a_conv_b, v_a_dt_bias, v_a_log, v_a_d_skip, v_a_norm, v_a_out_proj, v_b_pw1_w, v_b_pw1_b, v_b_dw_w, v_b_dw_b, v_b_ln_g, v_b_ln_b, v_b_pw2_w, v_b_pw2_b, v_c_in_w, v_c_in_b, v_c_conv_w, v_c_conv_b, v_c_ga_w, v_c_ga_b, v_c_gx_w, v_c_gx_b, v_c_lambda, v_c_out_w, v_c_out_b, v_d_in_w, v_d_in_b, v_d_ln_g, v_d_ln_b, v_d_sp_w, v_d_sp_b, v_d_out_w, v_d_out_b, v_f_up_w, v_f_conv_w, v_f_conv_b, v_f_down_w):
    args = locals()
    weights = {n: args[n] for n in WEIGHTS}
    moments_m = {n: args['m_' + n] for n in WEIGHTS}
    moments_v = {n: args['v_' + n] for n in WEIGHTS}
    loss, dx, grad, delta, new_m, new_v = _step(x, loss_target, weights, moments_m, moments_v)
    return (loss, dx, *[grad[n] for n in WEIGHTS], *[delta[n] for n in WEIGHTS],
            *[new_m[n] for n in WEIGHTS], *[new_v[n] for n in WEIGHTS])
```

```python
import functools
import math

import jax
import jax.numpy as jnp
import numpy as np
from jax import lax
from jax.experimental import pallas as pl
from jax.experimental.pallas import tpu as pltpu

f32 = jnp.float32
bf16 = jnp.bfloat16
MESH = pl.DeviceIdType.MESH
HIGHEST = lax.Precision.HIGHEST

D_MODEL = 1024
DEPTH = 4
RMS_EPS = 1e-6
LN_EPS = 1e-5
SSD_D_INNER = 2048
SSD_HEADS = 32
SSD_BC = 1024
SSD_CONV_DIM = 4096
SSD_CHUNK = 128
SSD_GROUPS = 8
LRU_W = 1280
LRU_BLOCK = 256
LRU_C = 8.0
SGU_HALF = 2048
SGU_GROUPS = 8
SGU_CHUNK = 128
FFN_H = 2816
ADAM_LR, ADAM_B1, ADAM_B2, ADAM_EPS, ADAM_WD, ADAM_STEP = 0.001, 0.9, 0.999, 1e-08, 0.01, 10

LANE = 128
SUBLANE = 8
VMEM_LIMIT = 56 * 1024 * 1024
FLAT_COLS = 1024

WEIGHTS = ['norm_mix', 'norm_ffn', 'norm_final', 'a_in_proj', 'a_conv_w', 'a_conv_b', 'a_dt_bias', 'a_log',
           'a_d_skip', 'a_norm', 'a_out_proj', 'b_pw1_w', 'b_pw1_b', 'b_dw_w', 'b_dw_b', 'b_ln_g', 'b_ln_b',
           'b_pw2_w', 'b_pw2_b', 'c_in_w', 'c_in_b', 'c_conv_w', 'c_conv_b', 'c_ga_w', 'c_ga_b', 'c_gx_w',
           'c_gx_b', 'c_lambda', 'c_out_w', 'c_out_b', 'd_in_w', 'd_in_b', 'd_ln_g', 'd_ln_b', 'd_sp_w',
           'd_sp_b', 'd_out_w', 'd_out_b', 'f_up_w', 'f_conv_w', 'f_conv_b', 'f_down_w']
SHARD_AXIS = {
    'norm_mix': None, 'norm_ffn': None, 'norm_final': None, 'a_in_proj': 2, 'a_conv_w': 2, 'a_conv_b': None,
    'a_dt_bias': None, 'a_log': None, 'a_d_skip': None, 'a_norm': None, 'a_out_proj': 1, 'b_pw1_w': 2,
    'b_pw1_b': 1, 'b_dw_w': 2, 'b_dw_b': 1, 'b_ln_g': 1, 'b_ln_b': 1, 'b_pw2_w': 1, 'b_pw2_b': 1, 'c_in_w': 2,
    'c_in_b': 1, 'c_conv_w': 2, 'c_conv_b': 1, 'c_ga_w': 2, 'c_ga_b': 2, 'c_gx_w': 2, 'c_gx_b': 2,
    'c_lambda': 1, 'c_out_w': 1, 'c_out_b': 1, 'd_in_w': 2, 'd_in_b': 1, 'd_ln_g': 1, 'd_ln_b': 1,
    'd_sp_w': None, 'd_sp_b': None, 'd_out_w': 1, 'd_out_b': 1, 'f_up_w': 2, 'f_conv_w': 2, 'f_conv_b': None,
    'f_down_w': 1}
MATMUL_WEIGHTS = ['a_in_proj', 'a_out_proj', 'b_pw1_w', 'b_pw2_w', 'c_in_w', 'c_ga_w', 'c_gx_w', 'c_out_w',
                  'd_in_w', 'd_out_w', 'f_up_w', 'f_down_w']
DIRECT_COLS = ['b_pw1_w', 'c_in_w', 'd_in_w', 'f_up_w']
DIRECT_ROWS = ['a_out_proj', 'b_pw2_w', 'c_out_w', 'd_out_w', 'f_down_w']
DIRECT = DIRECT_COLS + DIRECT_ROWS
PACKED_MM = [n for n in MATMUL_WEIGHTS if n not in DIRECT]
SHARDED = [n for n in WEIGHTS if SHARD_AXIS[n] is not None]
SHARDED_VEC = [n for n in SHARDED if n not in MATMUL_WEIGHTS]
REPLICATED = [n for n in WEIGHTS if SHARD_AXIS[n] is None]
N_CHIPS = 4
N_DEV = 8


def _tile(n, cap, mult):
    if n <= cap:
        return n
    t = (cap // mult) * mult
    while t >= mult:
        if n % t == 0:
            return t
        t -= mult
    raise ValueError(f"no tile for {n} under {cap} in steps of {mult}")


def _cparams(sem=None):
    if sem is None:
        return pltpu.CompilerParams(vmem_limit_bytes=VMEM_LIMIT)
    return pltpu.CompilerParams(dimension_semantics=sem, vmem_limit_bytes=VMEM_LIMIT)


def _dg(a, b, ca, cb):
    return lax.dot_general(a.astype(bf16), b.astype(bf16), (((ca,), (cb,)), ((), ())), preferred_element_type=f32)


@jax.custom_vjp
def _dot_nn(a, b):
    return _dg(a, b, 1, 0)


def _dot_nn_fwd(a, b):
    return _dg(a, b, 1, 0), (a, b)


def _dot_nn_bwd(res, g):
    a, b = res
    return _dg(g, b, 1, 1).astype(a.dtype), _dg(a, g, 0, 0).astype(b.dtype)


_dot_nn.defvjp(_dot_nn_fwd, _dot_nn_bwd)


@jax.custom_vjp
def _dot_nt(a, b):
    return _dg(a, b, 1, 1)


def _dot_nt_fwd(a, b):
    return _dg(a, b, 1, 1), (a, b)


def _dot_nt_bwd(res, g):
    a, b = res
    return _dg(g, b, 1, 0).astype(a.dtype), _dg(g, a, 0, 0).astype(b.dtype)


_dot_nt.defvjp(_dot_nt_fwd, _dot_nt_bwd)


@jax.custom_vjp
def _dot_tn(a, b):
    return _dg(a, b, 0, 0)


def _dot_tn_fwd(a, b):
    return _dg(a, b, 0, 0), (a, b)


def _dot_tn_bwd(res, g):
    a, b = res
    return _dg(b, g, 1, 1).astype(a.dtype), _dg(a, g, 1, 0).astype(b.dtype)


_dot_tn.defvjp(_dot_tn_fwd, _dot_tn_bwd)


def _expm1(x):
    small = jnp.abs(x) < 0.03
    xs = jnp.where(small, x, 0.0)
    series = xs * (1.0 + xs * (0.5 + xs * (1.0 / 6.0 + xs * (1.0 / 24.0 + xs * (1.0 / 120.0)))))
    return jnp.where(small, series, jnp.exp(x) - 1.0)


def _rms(x, g):
    return x * lax.rsqrt(jnp.mean(x * x, axis=-1, keepdims=True) + RMS_EPS) * g


def _layer_norm(x, g, b):
    mu = jnp.mean(x, axis=-1, keepdims=True)
    xc = x - mu
    return xc * lax.rsqrt(jnp.mean(xc * xc, axis=-1, keepdims=True) + LN_EPS) * g + b


def _causal_taps(ext, w, halo, rows):
    k_taps = w.shape[0]
    acc = None
    for k in range(k_taps):
        lo = halo - (k_taps - 1) + k
        term = w[k:k + 1, :] * ext[lo:lo + rows, :]
        acc = term if acc is None else acc + term
    return acc


def _mm(a, b, mode, name, *, bias=None, add=None, out_dtype=f32, tm_cap=1408, tn_cap=1408, tk_cap=1408,
        b_cols_sharded=False, b_layer=None, out_cols_sharded=False):
    shard_cols = None
    if b_cols_sharded:
        shard_cols = b.shape[-1]
        b_dims = (b.shape[-2], N_CHIPS * shard_cols)
    else:
        b_dims = b.shape
    if mode == 'nn':
        (m, k), (k2, n) = a.shape, b_dims
    elif mode == 'nt':
        (m, k), (n, k2) = a.shape, b_dims
    else:
        (k, m), (k2, n) = a.shape, b_dims
    assert k == k2, (name, a.shape, b.shape)
    tm = _tile(m, tm_cap, LANE if mode == 'tn' else SUBLANE)
    tn = _tile(n, tn_cap, LANE)
    tk = _tile(k, tk_cap, LANE if mode != 'tn' else SUBLANE)
    if b_cols_sharded and mode == 'nn':
        tn = shard_cols
    if b_cols_sharded and mode == 'nt':
        tk = shard_cols
    if out_cols_sharded:
        assert mode == 'tn' and n % N_CHIPS == 0
        tn = n // N_CHIPS
    nk = k // tk

    def shard_block(rows):
        lead = (None,) * (b.ndim - 2)
        return lead + (rows, shard_cols)

    def shard_index(shard, row_block):
        return (shard, row_block, 0) if b_layer is None else (shard, b_layer, row_block, 0)

    if mode == 'nn':
        a_spec = pl.BlockSpec((tm, tk), lambda i, j, kk: (i, kk))
        if b_cols_sharded:
            b_spec = pl.BlockSpec(shard_block(tk), lambda i, j, kk: shard_index(j, kk))
        else:
            b_spec = pl.BlockSpec((tk, tn), lambda i, j, kk: (kk, j))
        ca, cb = 1, 0
    elif mode == 'nt':
        a_spec = pl.BlockSpec((tm, tk), lambda i, j, kk: (i, kk))
        if b_cols_sharded:
            b_spec = pl.BlockSpec(shard_block(tn), lambda i, j, kk: shard_index(kk, j))
        else:
            b_spec = pl.BlockSpec((tn, tk), lambda i, j, kk: (j, kk))
        ca, cb = 1, 1
    else:
        a_spec = pl.BlockSpec((tk, tm), lambda i, j, kk: (kk, i))
        b_spec = pl.BlockSpec((tk, tn), lambda i, j, kk: (kk, j))
        ca, cb = 0, 0
    in_specs, operands = [a_spec, b_spec], [a, b]
    if bias is not None:
        in_specs.append(pl.BlockSpec((1, tn), lambda i, j, kk: (0, j)))
        operands.append(bias)
    if add is not None:
        in_specs.append(pl.BlockSpec((tm, tn), lambda i, j, kk: (i, j)))
        operands.append(add)

    def body(*refs):
        a_ref, b_ref = refs[0], refs[1]
        pos = 2
        bias_ref = add_ref = None
        if bias is not None:
            bias_ref = refs[pos]
            pos += 1
        if add is not None:
            add_ref = refs[pos]
            pos += 1
        o_ref, acc_ref = refs[pos], refs[pos + 1]
        kk = pl.program_id(2)

        @pl.when(kk == 0)
        def _():
            acc_ref[...] = jnp.zeros_like(acc_ref)

        acc_ref[...] += _dg(a_ref[...], b_ref[...], ca, cb)

        @pl.when(kk == nk - 1)
        def _():
            r = acc_ref[...]
            if bias_ref is not None:
                r = r + bias_ref[...]
            if add_ref is not None:
                r = r + add_ref[...].astype(f32)
            o_ref[...] = r.astype(out_dtype)

    if out_cols_sharded:
        out_shape = jax.ShapeDtypeStruct((N_CHIPS, m, tn), out_dtype)
        out_spec = pl.BlockSpec((None, tm, tn), lambda i, j, kk: (j, i, 0))
    else:
        out_shape = jax.ShapeDtypeStruct((m, n), out_dtype)
        out_spec = pl.BlockSpec((tm, tn), lambda i, j, kk: (i, j))
    return pl.pallas_call(
        body, name=name, out_shape=out_shape,
        grid=(m // tm, n // tn, nk), in_specs=in_specs, out_specs=out_spec,
        scratch_shapes=[pltpu.VMEM((tm, tn), f32)],
        compiler_params=_cparams(("parallel", "parallel", "arbitrary")),
    )(*operands)


def _mm_w(a, w, mode, name, **kw):
    shards, layer = w
    return _mm(a, shards, mode, name, b_cols_sharded=True, b_layer=layer, **kw)


def _row_specs(tiles, halo_of, rows, halo, n_tiles, reverse):
    def tile_index(i):
        return n_tiles - 1 - i if reverse else i

    specs, operands = [], []
    for arr, has_halo in zip(tiles, halo_of):
        cols = arr.shape[1]
        specs.append(pl.BlockSpec((rows, cols), lambda i: (tile_index(i), 0)))
        operands.append(arr)
        if has_halo:
            per = rows // halo
            specs.append(pl.BlockSpec((halo, cols), lambda i: (jnp.maximum(tile_index(i) * per - 1, 0), 0)))
            operands.append(arr)
    return specs, operands, tile_index


def _load_tiles(refs, halo_of, tile_id, rows, halo):
    vals, pos = [], 0
    for has_halo in halo_of:
        cur = refs[pos][...].astype(f32)
        pos += 1
        if has_halo:
            before = refs[pos][...].astype(f32)
            pos += 1
            before = jnp.where(tile_id > 0, before, jnp.zeros_like(before))
            cur = jnp.concatenate([before, cur], axis=0)
        vals.append(cur)
    return vals, pos


def _valid_rows(tile_id, rows, halo):
    r = lax.broadcasted_iota(jnp.int32, (halo + rows, 1), 0)
    return jnp.logical_or(r >= halo, tile_id > 0).astype(f32)


def _row_fwd(f, tiles, params, outs, *, rows, name, halo=0, halo_of=None):
    t_len = tiles[0].shape[0]
    rows = min(rows, t_len)
    n_tiles = t_len // rows
    halo_of = halo_of or [False] * len(tiles)
    specs, operands, _ = _row_specs(tiles, halo_of, rows, halo, n_tiles, False)
    for p in params:
        specs.append(pl.BlockSpec(p.shape, lambda i: (0, 0)))
        operands.append(p)

    def body(*refs):
        i = pl.program_id(0)
        vals, pos = _load_tiles(refs, halo_of, i, rows, halo)
        pvals = [refs[pos + j][...] for j in range(len(params))]
        pos += len(params)
        kw = {'valid': _valid_rows(i, rows, halo)} if halo else {}
        res = f(*vals, *pvals, **kw)
        for o_ref, o in zip(refs[pos:], res):
            o_ref[...] = o.astype(o_ref.dtype)

    return pl.pallas_call(
        body, name=name,
        out_shape=[jax.ShapeDtypeStruct((t_len, c), d) for c, d in outs],
        grid=(n_tiles,), in_specs=specs,
        out_specs=[pl.BlockSpec((rows, c), lambda i: (i, 0)) for c, _ in outs],
        compiler_params=_cparams(("parallel",)),
    )(*operands)


def _row_bwd(f, tiles, params, cots, *, rows, name, halo=0, halo_of=None, tile_dtypes=None):
    t_len = tiles[0].shape[0]
    rows = min(rows, t_len)
    n_tiles = t_len // rows
    halo_of = halo_of or [False] * len(tiles)
    tile_dtypes = tile_dtypes or [f32] * len(tiles)
    specs, operands, tile_index = _row_specs(tiles, halo_of, rows, halo, n_tiles, True)
    for p in params:
        specs.append(pl.BlockSpec(p.shape, lambda i: (0, 0)))
        operands.append(p)
    for ct in cots:
        specs.append(pl.BlockSpec((rows, ct.shape[1]), lambda i: (tile_index(i), 0)))
        operands.append(ct)
    n_t, n_p, n_c = len(tiles), len(params), len(cots)
    out_shape = [jax.ShapeDtypeStruct(t.shape, d) for t, d in zip(tiles, tile_dtypes)]
    out_shape += [jax.ShapeDtypeStruct(p.shape, f32) for p in params]
    out_specs = [pl.BlockSpec((rows, t.shape[1]), lambda i: (tile_index(i), 0)) for t in tiles]
    out_specs += [pl.BlockSpec(p.shape, lambda i: (0, 0)) for p in params]
    scratch = [pltpu.VMEM((halo, t.shape[1]), f32) for t, h in zip(tiles, halo_of) if h]

    def body(*refs):
        i = pl.program_id(0)
        tile_id = tile_index(i)
        vals, pos = _load_tiles(refs, halo_of, tile_id, rows, halo)
        pvals = [refs[pos + j][...] for j in range(n_p)]
        pos += n_p
        cvals = [refs[pos + j][...].astype(f32) for j in range(n_c)]
        pos += n_c
        d_tile_refs = refs[pos:pos + n_t]
        d_param_refs = refs[pos + n_t:pos + n_t + n_p]
        carries = list(refs[pos + n_t + n_p:])
        kw = {'valid': _valid_rows(tile_id, rows, halo)} if halo else {}
        _, vjp = jax.vjp(lambda *args: tuple(f(*args, **kw)), *vals, *pvals)
        grads = vjp(tuple(cvals))

        @pl.when(i == 0)
        def _():
            for cr in carries:
                cr[...] = jnp.zeros_like(cr)
            for dp in d_param_refs:
                dp[...] = jnp.zeros_like(dp)

        ci = 0
        for t in range(n_t):
            g = grads[t]
            if halo_of[t]:
                cr = carries[ci]
                ci += 1
                d_tile_refs[t][0:rows - halo, :] = g[halo:rows, :].astype(d_tile_refs[t].dtype)
                d_tile_refs[t][rows - halo:rows, :] = (g[rows:rows + halo, :] + cr[...]).astype(d_tile_refs[t].dtype)
                cr[...] = g[0:halo, :]
            else:
                d_tile_refs[t][...] = g.astype(d_tile_refs[t].dtype)
        for j in range(n_p):
            d_param_refs[j][...] += grads[n_t + j]

    return pl.pallas_call(
        body, name=name, out_shape=out_shape, grid=(n_tiles,), in_specs=specs, out_specs=out_specs,
        scratch_shapes=scratch, compiler_params=_cparams(("arbitrary",)),
    )(*operands)


def _strip_specs(tiles, rows, halo, n_tiles, reverse):
    def tile_index(i):
        return n_tiles - 1 - i if reverse else i

    specs, operands = [], []
    for arr, _, has_halo in tiles:
        cols = arr.shape[1]
        specs.append(pl.BlockSpec((rows, cols), lambda i: (tile_index(i), 0)))
        operands.append(arr)
        if has_halo:
            per = rows // halo
            specs.append(pl.BlockSpec((halo, cols), lambda i: (jnp.maximum(tile_index(i) * per - 1, 0), 0)))
            operands.append(arr)
    return specs, operands, tile_index


def _strip_sources(refs, tiles, ext_scratch, tile_id, rows, halo):
    srcs, pos, si = [], 0, 0
    for _, _, has_halo in tiles:
        cur = refs[pos]
        pos += 1
        if has_halo:
            before = refs[pos]
            pos += 1
            scr = ext_scratch[si]
            si += 1
            scr[0:halo, :] = jnp.where(tile_id > 0, before[...].astype(f32), 0.0)
            scr[halo:halo + rows, :] = cur[...].astype(f32)
            srcs.append(scr)
        else:
            srcs.append(cur)
    return srcs, pos


def _cols(base, c0, cs):
    return pl.ds(pl.multiple_of(base + c0, LANE), cs)


def _strip_inputs(tiles, srcs, params, p_refs, r0, c0, rs, cs, halo):
    vals = []
    for (_, bases, has_halo), src in zip(tiles, srcs):
        n_rows = halo + rs if has_halo else rs
        for b in bases:
            vals.append(src[pl.ds(r0, n_rows), _cols(b, c0, cs)].astype(f32))
    for (_, bases), p_ref in zip(params, p_refs):
        for b in bases:
            vals.append(p_ref[:, _cols(b, c0, cs)])
    return vals


def _strip_valid(tile_id, r0, rs, halo):
    r = lax.broadcasted_iota(jnp.int32, (halo + rs, 1), 0) + r0
    return jnp.logical_or(r >= halo, tile_id > 0).astype(f32)


def _strip_fwd(f, tiles, params, outs, *, rows, rs, cs, width, name, halo=0):
    t_len = tiles[0][0].shape[0]
    rows = min(rows, t_len)
    n_tiles, n_rs, n_cs = t_len // rows, rows // rs, width // cs
    specs, operands, _ = _strip_specs(tiles, rows, halo, n_tiles, False)
    for p, _ in params:
        specs.append(pl.BlockSpec(p.shape, lambda i: (0, 0)))
        operands.append(p)
    n_p, n_o = len(params), len(outs)
    scratch = [pltpu.VMEM((halo + rows, arr.shape[1]), f32) for arr, _, hh in tiles if hh]

    def body(*refs):
        i = pl.program_id(0)
        ext_scratch = refs[len(refs) - len(scratch):]
        srcs, pos = _strip_sources(refs, tiles, ext_scratch, i, rows, halo)
        p_refs = refs[pos:pos + n_p]
        o_refs = refs[pos + n_p:pos + n_p + n_o]

        def row_loop(r, carry):
            r0 = pl.multiple_of(r * rs, rs)
            kw = {'valid': _strip_valid(i, r0, rs, halo)} if halo else {}

            def col_loop(c, carry2):
                c0 = c * cs
                res = f(*_strip_inputs(tiles, srcs, params, p_refs, r0, c0, rs, cs, halo), **kw)
                k = 0
                for (_, dt, bases), o_ref in zip(outs, o_refs):
                    for b in bases:
                        o_ref[pl.ds(r0, rs), _cols(b, c0, cs)] = res[k].astype(dt)
                        k += 1
                return carry2

            return lax.fori_loop(0, n_cs, col_loop, carry)

        lax.fori_loop(0, n_rs, row_loop, 0)

    return pl.pallas_call(
        body, name=name,
        out_shape=[jax.ShapeDtypeStruct((t_len, c), d) for c, d, _ in outs],
        grid=(n_tiles,), in_specs=specs,
        out_specs=[pl.BlockSpec((rows, c), lambda i: (i, 0)) for c, _, _ in outs],
        scratch_shapes=scratch, compiler_params=_cparams(("parallel",)),
    )(*operands)


def _strip_bwd(f, tiles, params, cots, *, rows, rs, cs, width, name, halo=0, tile_dtypes=None):
    t_len = tiles[0][0].shape[0]
    rows = min(rows, t_len)
    n_tiles, n_rs, n_cs = t_len // rows, rows // rs, width // cs
    tile_dtypes = tile_dtypes or [f32] * len(tiles)
    specs, operands, tile_index = _strip_specs(tiles, rows, halo, n_tiles, True)
    for p, _ in params:
        specs.append(pl.BlockSpec(p.shape, lambda i: (0, 0)))
        operands.append(p)
    for ct, _ in cots:
        specs.append(pl.BlockSpec((rows, ct.shape[1]), lambda i: (tile_index(i), 0)))
        operands.append(ct)
    n_t, n_p, n_c = len(tiles), len(params), len(cots)
    out_shape = [jax.ShapeDtypeStruct(t[0].shape, d) for t, d in zip(tiles, tile_dtypes)]
    out_shape += [jax.ShapeDtypeStruct(p.shape, f32) for p, _ in params]
    out_specs = [pl.BlockSpec((rows, t[0].shape[1]), lambda i: (tile_index(i), 0)) for t in tiles]
    out_specs += [pl.BlockSpec(p.shape, lambda i: (0, 0)) for p, _ in params]
    halo_tiles = [t for t in tiles if t[2]]
    scratch = [pltpu.VMEM((halo + rows, arr.shape[1]), f32) for arr, _, _ in halo_tiles]
    scratch += [pltpu.VMEM((halo + rows, arr.shape[1]), f32) for arr, _, _ in halo_tiles]
    scratch += [pltpu.VMEM((halo, arr.shape[1]), f32) for arr, _, _ in halo_tiles]
    n_h = len(halo_tiles)

    def body(*refs):
        i = pl.program_id(0)
        tile_id = tile_index(i)
        scr = refs[len(refs) - 3 * n_h:]
        ext_scratch, grad_scratch, carries = scr[:n_h], scr[n_h:2 * n_h], scr[2 * n_h:]
        srcs, pos = _strip_sources(refs, tiles, ext_scratch, tile_id, rows, halo)
        p_refs = refs[pos:pos + n_p]
        c_refs = refs[pos + n_p:pos + n_p + n_c]
        pos += n_p + n_c
        d_tile_refs = refs[pos:pos + n_t]
        d_param_refs = refs[pos + n_t:pos + n_t + n_p]

        @pl.when(i == 0)
        def _():
            for cr in carries:
                cr[...] = jnp.zeros_like(cr)
            for dp in d_param_refs:
                dp[...] = jnp.zeros_like(dp)

        for gs in grad_scratch:
            gs[...] = jnp.zeros_like(gs)

        def row_loop(r, carry):
            r0 = pl.multiple_of(r * rs, rs)
            kw = {'valid': _strip_valid(tile_id, r0, rs, halo)} if halo else {}

            def col_loop(c, carry2):
                c0 = c * cs
                vals = _strip_inputs(tiles, srcs, params, p_refs, r0, c0, rs, cs, halo)
                cvals = []
                for (_, bases), c_ref in zip(cots, c_refs):
                    for b in bases:
                        cvals.append(c_ref[pl.ds(r0, rs), _cols(b, c0, cs)].astype(f32))
                _, vjp = jax.vjp(lambda *args: tuple(f(*args, **kw)), *vals)
                grads = vjp(tuple(cvals))
                k, hi = 0, 0
                for t, (_, bases, has_halo) in enumerate(tiles):
                    for b in bases:
                        if has_halo:
                            grad_scratch[hi][pl.ds(r0, halo + rs), _cols(b, c0, cs)] += grads[k]
                        else:
                            d_tile_refs[t][pl.ds(r0, rs), _cols(b, c0, cs)] = grads[k].astype(d_tile_refs[t].dtype)
                        k += 1
                    hi += has_halo
                for (_, bases), dp in zip(params, d_param_refs):
                    for b in bases:
                        dp[:, _cols(b, c0, cs)] += grads[k]
                        k += 1
                return carry2

            return lax.fori_loop(0, n_cs, col_loop, carry)

        lax.fori_loop(0, n_rs, row_loop, 0)

        hi = 0
        for t, (_, _, has_halo) in enumerate(tiles):
            if has_halo:
                gs, cr, d_ref = grad_scratch[hi], carries[hi], d_tile_refs[t]
                hi += 1
                d_ref[0:rows - halo, :] = gs[halo:rows, :].astype(d_ref.dtype)
                d_ref[rows - halo:rows, :] = (gs[rows:rows + halo, :] + cr[...]).astype(d_ref.dtype)
                cr[...] = gs[0:halo, :]

    return pl.pallas_call(
        body, name=name, out_shape=out_shape, grid=(n_tiles,), in_specs=specs, out_specs=out_specs,
        scratch_shapes=scratch, compiler_params=_cparams(("arbitrary",)),
    )(*operands)


def _fold8(v):
    acc = v[0:SUBLANE, :]
    for m in range(1, v.shape[0] // SUBLANE):
        acc = acc + v[m * SUBLANE:(m + 1) * SUBLANE, :]
    return acc


class _ConvPlan:
    def __init__(self, x, w, b, *, in_bases, mid_bases, width, rows, rs, pre=None, pre_params=(), post=None):
        self.x, self.w, self.b = x, w, b
        self.in_bases, self.mid_bases, self.width = in_bases, mid_bases, width
        self.pre, self.pre_params, self.post = pre, list(pre_params), post
        self.k_taps = w.shape[0]
        self.halo = -(-(self.k_taps - 1) // SUBLANE) * SUBLANE
        self.t_len = x.shape[0]
        self.rows = min(rows, self.t_len)
        self.rs = rs
        self.n_tiles, self.n_rs, self.n_cs = self.t_len // self.rows, self.rows // rs, width // LANE
        self.n_mid = len(mid_bases)
        if pre is None:
            assert len(in_bases) == self.n_mid

    def in_specs(self, tile_index):
        cols = self.x.shape[1]
        per = self.rows // self.halo
        specs = [pl.BlockSpec((self.rows, cols), lambda i: (tile_index(i), 0)),
                 pl.BlockSpec((self.halo, cols), lambda i: (jnp.maximum(tile_index(i) * per - 1, 0), 0)),
                 pl.BlockSpec(self.w.shape, lambda i: (0, 0)), pl.BlockSpec(self.b.shape, lambda i: (0, 0))]
        operands = [self.x, self.x, self.w, self.b]
        for p, _ in self.pre_params:
            specs.append(pl.BlockSpec(p.shape, lambda i: (0, 0)))
            operands.append(p)
        return specs, operands

    def pre_strips(self, pp_refs, c0):
        return [p_ref[:, _cols(b, c0, LANE)] for (_, bases), p_ref in zip(self.pre_params, pp_refs) for b in bases]

    def fill_conv_input(self, cur_ref, before_ref, pp_refs, u_ref, tile_id):
        started = (tile_id > 0).astype(f32)

        def col_loop(c, carry):
            c0 = c * LANE
            pps = self.pre_strips(pp_refs, c0)
            xs = [before_ref[:, _cols(b, c0, LANE)].astype(f32) for b in self.in_bases]
            for j, u in enumerate(self.pre(*xs, *pps, valid=started)):
                u_ref[0:self.halo, _cols(j * self.width, c0, LANE)] = u
            for r in range(self.n_rs):
                xs = [cur_ref[r * self.rs:(r + 1) * self.rs, _cols(b, c0, LANE)].astype(f32) for b in self.in_bases]
                for j, u in enumerate(self.pre(*xs, *pps, valid=1.0)):
                    u_ref[self.halo + r * self.rs:self.halo + (r + 1) * self.rs, _cols(j * self.width, c0, LANE)] = u
            return carry

        lax.fori_loop(0, self.n_cs, col_loop, 0)

    def tap(self, cur_ref, before_ref, u_ref, tile_id, r, j, k, c0):
        lo = r * self.rs - (self.k_taps - 1) + k
        if u_ref is not None:
            return u_ref[self.halo + lo:self.halo + lo + self.rs, _cols(j * self.width, c0, LANE)]
        cols = _cols(self.in_bases[j], c0, LANE)
        if lo >= 0:
            return cur_ref[lo:lo + self.rs, cols].astype(f32)
        head = before_ref[self.halo + lo:self.halo, cols].astype(f32)
        head = jnp.where(tile_id > 0, head, 0.0)
        return jnp.concatenate([head, cur_ref[0:self.rs + lo, cols].astype(f32)], axis=0)

    def conv(self, cur_ref, before_ref, u_ref, w_ref, b_ref, tile_id, r, c0):
        hcs = []
        for j, mb in enumerate(self.mid_bases):
            cols = _cols(mb, c0, LANE)
            acc = b_ref[:, cols]
            for k in range(self.k_taps):
                acc = acc + w_ref[k:k + 1, cols] * self.tap(cur_ref, before_ref, u_ref, tile_id, r, j, k, c0)
            hcs.append(acc)
        return hcs


def _conv_fwd(plan, outs, name):
    n_pp = len(plan.pre_params)
    specs, operands = plan.in_specs(lambda i: i)
    scratch = [pltpu.VMEM((plan.halo + plan.rows, plan.n_mid * plan.width), f32)] if plan.pre else []

    def body(*refs):
        cur_ref, before_ref, w_ref, b_ref = refs[:4]
        pp_refs = refs[4:4 + n_pp]
        o_refs = refs[4 + n_pp:4 + n_pp + len(outs)]
        u_ref = refs[-1] if plan.pre else None
        i = pl.program_id(0)
        if plan.pre:
            plan.fill_conv_input(cur_ref, before_ref, pp_refs, u_ref, i)

        def col_loop(c, carry):
            c0 = c * LANE
            for r in range(plan.n_rs):
                res = plan.post(*plan.conv(cur_ref, before_ref, u_ref, w_ref, b_ref, i, r, c0))
                n = 0
                for (_, dt, bases), o_ref in zip(outs, o_refs):
                    for ob in bases:
                        o_ref[r * plan.rs:(r + 1) * plan.rs, _cols(ob, c0, LANE)] = res[n].astype(dt)
                        n += 1
            return carry

        lax.fori_loop(0, plan.n_cs, col_loop, 0)

    return pl.pallas_call(
        body, name=name, out_shape=[jax.ShapeDtypeStruct((plan.t_len, c), d) for c, d, _ in outs],
        grid=(plan.n_tiles,), in_specs=specs,
        out_specs=[pl.BlockSpec((plan.rows, c), lambda i: (i, 0)) for c, _, _ in outs],
        scratch_shapes=scratch, compiler_params=_cparams(("parallel",)),
    )(*operands)


def _conv_bwd(plan, cots, dx_dtype, name):
    n_pp, n_c = len(plan.pre_params), len(cots)
    n_tiles, rows, rs, halo, k_taps = plan.n_tiles, plan.rows, plan.rs, plan.halo, plan.k_taps

    def tile_index(i):
        return n_tiles - 1 - i

    specs, operands = plan.in_specs(tile_index)
    for ct, _ in cots:
        specs.append(pl.BlockSpec((rows, ct.shape[1]), lambda i: (tile_index(i), 0)))
        operands.append(ct)
    mid_cols = plan.n_mid * plan.width
    out_shape = [jax.ShapeDtypeStruct(plan.x.shape, dx_dtype), jax.ShapeDtypeStruct(plan.w.shape, f32),
                 jax.ShapeDtypeStruct(plan.b.shape, f32)]
    out_shape += [jax.ShapeDtypeStruct(p.shape, f32) for p, _ in plan.pre_params]
    out_specs = [pl.BlockSpec((rows, plan.x.shape[1]), lambda i: (tile_index(i), 0)),
                 pl.BlockSpec(plan.w.shape, lambda i: (0, 0)), pl.BlockSpec(plan.b.shape, lambda i: (0, 0))]
    out_specs += [pl.BlockSpec(p.shape, lambda i: (0, 0)) for p, _ in plan.pre_params]
    w_cols = plan.w.shape[1]
    scratch = [pltpu.VMEM((rows + halo, mid_cols), f32),
               pltpu.VMEM((halo, mid_cols), f32),
               pltpu.VMEM(((k_taps + 1) * SUBLANE, w_cols), f32)]
    if plan.pre:
        scratch.append(pltpu.VMEM((halo + rows, mid_cols), f32))

    def body(*refs):
        cur_ref, before_ref, w_ref, b_ref = refs[:4]
        pp_refs = refs[4:4 + n_pp]
        c_refs = refs[4 + n_pp:4 + n_pp + n_c]
        pos = 4 + n_pp + n_c
        dx_ref, dw_ref, db_ref = refs[pos:pos + 3]
        dpp_refs = refs[pos + 3:pos + 3 + n_pp]
        g_ref, carry_ref, acc_ref = refs[pos + 3 + n_pp:pos + 6 + n_pp]
        u_ref = refs[-1] if plan.pre else None
        i = pl.program_id(0)
        tile_id = tile_index(i)

        @pl.when(i == 0)
        def _():
            carry_ref[...] = jnp.zeros_like(carry_ref)
            acc_ref[...] = jnp.zeros_like(acc_ref)
            for dp in dpp_refs:
                dp[...] = jnp.zeros_like(dp)

        g_ref[rows:rows + halo, :] = carry_ref[...]
        if plan.pre:
            plan.fill_conv_input(cur_ref, before_ref, pp_refs, u_ref, tile_id)

        def col_loop(c, carry):
            c0 = c * LANE
            for r in range(plan.n_rs):
                hcs = plan.conv(cur_ref, before_ref, u_ref, w_ref, b_ref, tile_id, r, c0)
                _, vjp = jax.vjp(lambda *a: tuple(plan.post(*a)), *hcs)
                cvals = [c_ref[r * rs:(r + 1) * rs, _cols(cb, c0, LANE)].astype(f32)
                         for (_, bases), c_ref in zip(cots, c_refs) for cb in bases]
                d_hcs = vjp(tuple(cvals))
                for j, mb in enumerate(plan.mid_bases):
                    g_ref[r * rs:(r + 1) * rs, _cols(j * plan.width, c0, LANE)] = d_hcs[j]
                    wc = _cols(mb, c0, LANE)
                    acc_ref[k_taps * SUBLANE:(k_taps + 1) * SUBLANE, wc] += _fold8(d_hcs[j])
                    for k in range(k_taps):
                        x_k = plan.tap(cur_ref, before_ref, u_ref, tile_id, r, j, k, c0)
                        acc_ref[k * SUBLANE:(k + 1) * SUBLANE, wc] += _fold8(d_hcs[j] * x_k)
            pps = plan.pre_strips(pp_refs, c0)
            for r in range(plan.n_rs):
                d_us = []
                for j, mb in enumerate(plan.mid_bases):
                    wc = _cols(mb, c0, LANE)
                    acc = None
                    for k in range(k_taps):
                        lo = r * rs + (k_taps - 1) - k
                        term = w_ref[k:k + 1, wc] * g_ref[lo:lo + rs, _cols(j * plan.width, c0, LANE)]
                        acc = term if acc is None else acc + term
                    d_us.append(acc)
                if plan.pre is None:
                    d_xs = d_us
                else:
                    xs = [cur_ref[r * rs:(r + 1) * rs, _cols(b, c0, LANE)].astype(f32) for b in plan.in_bases]
                    _, vjp_pre = jax.vjp(lambda *a: tuple(plan.pre(*a, valid=1.0)), *xs, *pps)
                    grads = vjp_pre(tuple(d_us))
                    d_xs = grads[:len(xs)]
                    n = len(xs)
                    for (_, bases), dp in zip(plan.pre_params, dpp_refs):
                        for pb in bases:
                            dp[:, _cols(pb, c0, LANE)] += grads[n]
                            n += 1
                for b, d_x in zip(plan.in_bases, d_xs):
                    dx_ref[r * rs:(r + 1) * rs, _cols(b, c0, LANE)] = d_x.astype(dx_dtype)
            return carry

        lax.fori_loop(0, plan.n_cs, col_loop, 0)
        carry_ref[...] = g_ref[0:halo, :]

        @pl.when(i == n_tiles - 1)
        def _():
            for k in range(k_taps):
                dw_ref[k:k + 1, :] = jnp.sum(acc_ref[k * SUBLANE:(k + 1) * SUBLANE, :], axis=0, keepdims=True)
            db_ref[...] = jnp.sum(acc_ref[k_taps * SUBLANE:(k_taps + 1) * SUBLANE, :], axis=0, keepdims=True)

    return pl.pallas_call(
        body, name=name, out_shape=out_shape, grid=(n_tiles,), in_specs=specs, out_specs=out_specs,
        scratch_shapes=scratch, compiler_params=_cparams(("arbitrary",)),
    )(*operands)


def _f_rms(h, g):
    return (_rms(h, g),)


def _f_rms_res(h, g, bz):
    hh = h + bz
    return _rms(hh, g), hh


def _post_ffn_gate(gate, val):
    return (jax.nn.silu(gate) * val,)


def _post_silu(h):
    return (jax.nn.silu(h),)


def _post_identity(h):
    return (h,)


def _pre_glu(g_a, g_b, b_a, b_b, *, valid):
    return ((g_a + b_a) * jax.nn.sigmoid(g_b + b_b) * valid,)


def _f_ssd_dt(dtr, dtb):
    real = lax.broadcasted_iota(jnp.int32, (1, LANE), 1) < SSD_HEADS
    return (jnp.where(real, jax.nn.softplus(dtr + dtb), 0.0),)


def _f_ssd_post(y, z, g):
    return (_rms(y * jax.nn.silu(z), g),)


CONF_HALO = 32
FFN_STRIP_ROWS = 64
CONF_STRIP_ROWS = 128
SSD_STRIP_ROWS = 64


def _f_ln_silu(x, g, b):
    return (jax.nn.silu(_layer_norm(x, g, b)),)


def _f_lru(io_ext, in_b, cw, cb, ga_w, ga_b, gx_w, gx_b, lam, *, valid):
    rows = io_ext.shape[0] - SUBLANE
    io = (io_ext + in_b) * valid
    gate = io[SUBLANE:, :LRU_W]
    xr = _causal_taps(io[:, LRU_W:], cw, SUBLANE, rows) + cb
    rs, iis = [], []
    for blk in range(LRU_W // LRU_BLOCK):
        sl = slice(blk * LRU_BLOCK, (blk + 1) * LRU_BLOCK)
        xb = xr[:, sl]
        rs.append(jax.nn.sigmoid(_dot_nn(xb, ga_w[sl, :]) + ga_b[:, sl]))
        iis.append(jax.nn.sigmoid(_dot_nn(xb, gx_w[sl, :]) + gx_b[:, sl]))
    r = jnp.concatenate(rs, axis=1)
    ig = jnp.concatenate(iis, axis=1)
    log_a = -LRU_C * r * jax.nn.softplus(-lam)
    a = jnp.exp(log_a)
    bterm = jnp.sqrt(-_expm1(2.0 * log_a)) * (ig * xr)
    return a, bterm, jax.nn.gelu(gate)


def _f_sgu(z, in_b, ln_g, ln_b, sp_w, sp_bt):
    rows = z.shape[0]
    zz = jax.nn.gelu(z + in_b)
    u, v = zz[:, :SGU_HALF], zz[:, SGU_HALF:]
    v = _layer_norm(v, ln_g, ln_b)
    tri = lax.broadcasted_iota(jnp.int32, (SGU_CHUNK, SGU_CHUNK), 0) >= lax.broadcasted_iota(
        jnp.int32, (SGU_CHUNK, SGU_CHUNK), 1)
    gdim = SGU_HALF // SGU_GROUPS
    row_blocks = []
    for ci in range(rows // SGU_CHUNK):
        col_blocks = []
        for g in range(SGU_GROUPS):
            w = jnp.where(tri, sp_w[g * SGU_CHUNK:(g + 1) * SGU_CHUNK, :], 0.0)
            vb = v[ci * SGU_CHUNK:(ci + 1) * SGU_CHUNK, g * gdim:(g + 1) * gdim]
            col_blocks.append(_dot_nn(w, vb) + sp_bt[:, g:g + 1])
        row_blocks.append(jnp.concatenate(col_blocks, axis=1))
    mixed = row_blocks[0] if len(row_blocks) == 1 else jnp.concatenate(row_blocks, axis=0)
    return (u * mixed,)


HEADS_PER_GROUP = 4
GROUP_COLS = 256
HEAD_DIM = 64


def _ssd_group(x, bm, cm, dt, st, a_log, dsk, g):
    q = x.shape[0]
    tri = lax.broadcasted_iota(jnp.int32, (q, q), 0) >= lax.broadcasted_iota(jnp.int32, (q, q), 1)
    d_a = dt * (-jnp.exp(a_log))
    acs = jnp.dot(tri.astype(f32), d_a, precision=HIGHEST, preferred_element_type=f32)
    acs_t = acs.T
    lane = lax.broadcasted_iota(jnp.int32, (1, LANE), 1)
    sub = lax.broadcasted_iota(jnp.int32, (LANE, 1), 0)
    col_idx = lax.broadcasted_iota(jnp.int32, (1, GROUP_COLS), 1)
    last_row = (lax.broadcasted_iota(jnp.int32, (q, 1), 0) == q - 1).astype(f32)
    cb = _dot_nt(cm, bm)
    y = jnp.zeros((q, GROUP_COLS), f32)
    e_in = jnp.zeros((q, GROUP_COLS), f32)
    d_end = jnp.zeros((q, GROUP_COLS), f32)
    d_last = jnp.zeros((1, GROUP_COLS), f32)
    d_skip = jnp.zeros((1, GROUP_COLS), f32)
    for j in range(HEADS_PER_GROUP):
        head = HEADS_PER_GROUP * g + j
        on_lane = (lane == head).astype(f32)
        on_sub = (sub == head).astype(f32)
        col = jnp.sum(acs * on_lane, axis=1, keepdims=True)
        row = jnp.sum(acs_t * on_sub, axis=0, keepdims=True)
        dtc = jnp.sum(dt * on_lane, axis=1, keepdims=True)
        last = jnp.sum(col * last_row, axis=0, keepdims=True)
        dsk_j = jnp.sum(dsk * on_lane, axis=1, keepdims=True)
        decay = jnp.where(tri, jnp.exp(jnp.where(tri, col - row, 0.0)), 0.0)
        mine = jnp.logical_and(col_idx >= j * HEAD_DIM, col_idx < (j + 1) * HEAD_DIM)
        y = y + _dot_nn(cb * decay, jnp.where(mine, x * dtc, 0.0))
        e_in = e_in + jnp.where(mine, jnp.exp(col), 0.0)
        d_end = d_end + jnp.where(mine, jnp.exp(last - col) * dtc, 0.0)
        d_last = d_last + jnp.where(mine, jnp.exp(last), 0.0)
        d_skip = d_skip + jnp.where(mine, dsk_j, 0.0)
    y = y + _dot_nn(cm, st) * e_in + x * d_skip
    st_new = st * d_last + _dot_tn(bm, x * d_end)
    return y, st_new


GROUPS_PER_STEP = 2


def _ssd_specs(rev, nc):
    def ch(c):
        return nc - 1 - c if rev else c

    gps = GROUPS_PER_STEP
    x_spec = pl.BlockSpec((SSD_CHUNK, gps * GROUP_COLS), lambda c, g: (ch(c), g))
    b_spec = pl.BlockSpec((SSD_CHUNK, gps * LANE), lambda c, g: (ch(c), SSD_D_INNER // (gps * LANE) + g))
    c_spec = pl.BlockSpec((SSD_CHUNK, gps * LANE), lambda c, g: (ch(c), (SSD_D_INNER + SSD_BC) // (gps * LANE) + g))
    dt_spec = pl.BlockSpec((SSD_CHUNK, LANE), lambda c, g: (ch(c), 0))
    row_spec = pl.BlockSpec((1, LANE), lambda c, g: (0, 0))
    st_spec = pl.BlockSpec((1, gps, LANE, GROUP_COLS), lambda c, g: (ch(c), g, 0, 0))
    wide_spec = pl.BlockSpec((SSD_CHUNK, SSD_CONV_DIM), lambda c, g: (ch(c), 0))
    return x_spec, b_spec, c_spec, dt_spec, row_spec, st_spec, wide_spec


def _ssd_fwd(xc, dt, a_log, dsk):
    t_len = xc.shape[0]
    nc = t_len // SSD_CHUNK
    gps = GROUPS_PER_STEP
    x_spec, b_spec, c_spec, dt_spec, row_spec, st_spec, _ = _ssd_specs(False, nc)

    def body(x_ref, b_ref, c_ref, dt_ref, al_ref, dk_ref, y_ref, st_out_ref, st_ref):
        c, gp = pl.program_id(0), pl.program_id(1)
        for q in range(gps):
            g = gp * gps + q

            @pl.when(c == 0)
            def _():
                st_ref[g] = jnp.zeros((LANE, GROUP_COLS), f32)

            st = st_ref[g]
            st_out_ref[0, q] = st
            xq = slice(q * GROUP_COLS, (q + 1) * GROUP_COLS)
            bq = slice(q * LANE, (q + 1) * LANE)
            y, st_new = _ssd_group(x_ref[:, xq], b_ref[:, bq], c_ref[:, bq], dt_ref[...], st, al_ref[...],
                                   dk_ref[...], g)
            y_ref[:, xq] = y
            st_ref[g] = st_new

    return pl.pallas_call(
        body, name="ssd_scan_fwd",
        out_shape=[jax.ShapeDtypeStruct((t_len, SSD_D_INNER), f32),
                   jax.ShapeDtypeStruct((nc, SSD_GROUPS, LANE, GROUP_COLS), f32)],
        grid=(nc, SSD_GROUPS // gps), in_specs=[x_spec, b_spec, c_spec, dt_spec, row_spec, row_spec],
        out_specs=[x_spec, st_spec],
        scratch_shapes=[pltpu.VMEM((SSD_GROUPS, LANE, GROUP_COLS), f32)],
        compiler_params=_cparams(("arbitrary", "arbitrary")),
    )(xc, xc, xc, dt, a_log, dsk)


def _ssd_bwd(xc, dt, a_log, dsk, states, dy):
    t_len = xc.shape[0]
    nc = t_len // SSD_CHUNK
    gps = GROUPS_PER_STEP
    x_spec, b_spec, c_spec, dt_spec, row_spec, st_spec, wide_spec = _ssd_specs(True, nc)

    def body(x_ref, b_ref, c_ref, dt_ref, al_ref, dk_ref, st_in_ref, dy_ref,
             dxc_ref, ddt_ref, dal_ref, ddk_ref, dst_ref):
        c, gp = pl.program_id(0), pl.program_id(1)

        @pl.when(jnp.logical_and(c == 0, gp == 0))
        def _():
            dal_ref[...] = jnp.zeros_like(dal_ref)
            ddk_ref[...] = jnp.zeros_like(ddk_ref)

        @pl.when(gp == 0)
        def _():
            ddt_ref[...] = jnp.zeros_like(ddt_ref)

        for q in range(gps):
            g = gp * gps + q

            @pl.when(c == 0)
            def _():
                dst_ref[g] = jnp.zeros((LANE, GROUP_COLS), f32)

            xq = slice(q * GROUP_COLS, (q + 1) * GROUP_COLS)
            bq = slice(q * LANE, (q + 1) * LANE)
            _, vjp = jax.vjp(lambda *args: _ssd_group(*args, g), x_ref[:, xq], b_ref[:, bq], c_ref[:, bq],
                             dt_ref[...], st_in_ref[0, q], al_ref[...], dk_ref[...])
            dx, db, dc, ddt, dst, dal, ddk = vjp((dy_ref[:, xq], dst_ref[g]))
            dxc_ref[:, pl.ds(pl.multiple_of(g * GROUP_COLS, GROUP_COLS), GROUP_COLS)] = dx
            dxc_ref[:, pl.ds(pl.multiple_of(SSD_D_INNER + g * LANE, LANE), LANE)] = db
            dxc_ref[:, pl.ds(pl.multiple_of(SSD_D_INNER + SSD_BC + g * LANE, LANE), LANE)] = dc
            ddt_ref[...] += ddt
            dst_ref[g] = dst
            dal_ref[...] += dal
            ddk_ref[...] += ddk

    return pl.pallas_call(
        body, name="ssd_scan_bwd",
        out_shape=[jax.ShapeDtypeStruct((t_len, SSD_CONV_DIM), f32), jax.ShapeDtypeStruct((t_len, LANE), f32),
                   jax.ShapeDtypeStruct((1, LANE), f32), jax.ShapeDtypeStruct((1, LANE), f32)],
        grid=(nc, SSD_GROUPS // gps),
        in_specs=[x_spec, b_spec, c_spec, dt_spec, row_spec, row_spec, st_spec, x_spec],
        out_specs=[wide_spec, dt_spec, row_spec, row_spec],
        scratch_shapes=[pltpu.VMEM((SSD_GROUPS, LANE, GROUP_COLS), f32)],
        compiler_params=_cparams(("arbitrary", "arbitrary")),
    )(xc, xc, xc, dt, a_log, dsk, states, dy)


LRU_ROWS = 256


def _lru_fwd(a, b, gg):
    t_len, cols = a.shape
    rows = min(LRU_ROWS, t_len)
    spec = pl.BlockSpec((rows, cols), lambda i: (i, 0))

    def body(a_ref, b_ref, g_ref, y_ref, h_ref, carry):
        i = pl.program_id(0)

        @pl.when(i == 0)
        def _():
            carry[...] = jnp.zeros_like(carry)

        av, bv = a_ref[...], b_ref[...]
        row = lax.broadcasted_iota(jnp.int32, av.shape, 0)
        s = 1
        while s < rows:
            a_prev = pltpu.roll(av, s, axis=0)
            b_prev = pltpu.roll(bv, s, axis=0)
            m = row >= s
            bv = jnp.where(m, av * b_prev + bv, bv)
            av = jnp.where(m, av * a_prev, av)
            s *= 2
        h = av * carry[0:1, :] + bv
        h_ref[...] = h
        y_ref[...] = g_ref[...] * h
        carry[0:1, :] = h[rows - 1:rows, :]

    return pl.pallas_call(
        body, name="lru_scan_fwd",
        out_shape=[jax.ShapeDtypeStruct((t_len, cols), f32), jax.ShapeDtypeStruct((t_len, cols), f32)],
        grid=(t_len // rows,), in_specs=[spec, spec, spec], out_specs=[spec, spec],
        scratch_shapes=[pltpu.VMEM((SUBLANE, cols), f32)],
        compiler_params=_cparams(("arbitrary",)),
    )(a, b, gg)


def _lru_bwd(dy, gg, a, h):
    t_len, cols = a.shape
    rows = min(LRU_ROWS, t_len)
    n_tiles = t_len // rows
    per = rows // SUBLANE
    spec = pl.BlockSpec((rows, cols), lambda i: (n_tiles - 1 - i, 0))
    prev_spec = pl.BlockSpec((SUBLANE, cols), lambda i: (jnp.maximum((n_tiles - 1 - i) * per - 1, 0), 0))

    def body(dy_ref, g_ref, a_ref, h_ref, hp_ref, da_ref, db_ref, dg_ref, carry_dh, carry_a):
        i = pl.program_id(0)
        tile_id = n_tiles - 1 - i

        @pl.when(i == 0)
        def _():
            carry_dh[...] = jnp.zeros_like(carry_dh)
            carry_a[...] = jnp.zeros_like(carry_a)

        av, hv, dyv = a_ref[...], h_ref[...], dy_ref[...]
        row = lax.broadcasted_iota(jnp.int32, av.shape, 0)
        dg_ref[...] = dyv * hv
        bv = dyv * g_ref[...]
        cv = jnp.where(row == rows - 1, carry_a[0:1, :], pltpu.roll(av, rows - 1, axis=0))
        s = 1
        while s < rows:
            c_next = pltpu.roll(cv, rows - s, axis=0)
            b_next = pltpu.roll(bv, rows - s, axis=0)
            m = row < rows - s
            bv = jnp.where(m, cv * b_next + bv, bv)
            cv = jnp.where(m, cv * c_next, cv)
            s *= 2
        dh = cv * carry_dh[0:1, :] + bv
        h_before = jnp.where(tile_id > 0, hp_ref[SUBLANE - 1:SUBLANE, :], jnp.zeros((1, cols), f32))
        h_prev = jnp.where(row == 0, h_before, pltpu.roll(hv, 1, axis=0))
        da_ref[...] = dh * h_prev
        db_ref[...] = dh
        carry_dh[0:1, :] = dh[0:1, :]
        carry_a[0:1, :] = av[0:1, :]

    return pl.pallas_call(
        body, name="lru_scan_bwd",
        out_shape=[jax.ShapeDtypeStruct((t_len, cols), f32)] * 3,
        grid=(n_tiles,), in_specs=[spec, spec, spec, spec, prev_spec], out_specs=[spec, spec, spec],
        scratch_shapes=[pltpu.VMEM((SUBLANE, cols), f32), pltpu.VMEM((SUBLANE, cols), f32)],
        compiler_params=_cparams(("arbitrary",)),
    )(dy, gg, a, h, h)


def _loss_head(h, target, g):
    t_len = h.shape[0]
    rows = min(512, t_len)

    def f(hv, gv, tv):
        err = _rms(hv, gv) - tv
        return 0.5 * jnp.sum(jnp.mean(err * err, axis=-1, keepdims=True), axis=0, keepdims=True)

    def body(h_ref, t_ref, g_ref, dh_ref, dg_ref, loss_ref):
        i = pl.program_id(0)

        @pl.when(i == 0)
        def _():
            dg_ref[...] = jnp.zeros_like(dg_ref)
            loss_ref[...] = jnp.zeros_like(loss_ref)

        tv = t_ref[...]
        part, vjp = jax.vjp(lambda hv, gv: f(hv, gv, tv), h_ref[...], g_ref[...])
        dh, dg = vjp(jnp.ones((1, 1), f32))
        dh_ref[...] = dh
        dg_ref[...] += dg
        loss_ref[...] += jnp.broadcast_to(part, loss_ref.shape)

    spec = pl.BlockSpec((rows, D_MODEL), lambda i: (i, 0))
    return pl.pallas_call(
        body, name="loss_head",
        out_shape=[jax.ShapeDtypeStruct((t_len, D_MODEL), f32), jax.ShapeDtypeStruct((1, D_MODEL), f32),
                   jax.ShapeDtypeStruct((1, LANE), f32)],
        grid=(t_len // rows,), in_specs=[spec, spec, pl.BlockSpec((1, D_MODEL), lambda i: (0, 0))],
        out_specs=[spec, pl.BlockSpec((1, D_MODEL), lambda i: (0, 0)), pl.BlockSpec((1, LANE), lambda i: (0, 0))],
        compiler_params=_cparams(("arbitrary",)),
    )(h, target, g)


def _as2d(a):
    return a.reshape((-1, a.shape[-1])) if a.ndim > 1 else a.reshape((1, -1))


def _row_block(rows, cols, bytes_cap=1 << 20):
    if rows * cols * 4 <= bytes_cap or rows % SUBLANE:
        return rows
    return _tile(rows, max(SUBLANE, (bytes_cap // (cols * 4)) // SUBLANE * SUBLANE), SUBLANE)


def _adamw(w, g, m, v, name):
    shape = w.shape
    w2, g2, m2, v2 = _as2d(w), _as2d(g), _as2d(m), _as2d(v)
    rows, cols = w2.shape
    rb = _row_block(rows, cols)

    def body(w_ref, g_ref, m_ref, v_ref, d_ref, nm_ref, nv_ref):
        gv = g_ref[...]
        nm = ADAM_B1 * m_ref[...] + (1.0 - ADAM_B1) * gv
        nv = ADAM_B2 * v_ref[...] + (1.0 - ADAM_B2) * jnp.square(gv)
        m_hat = nm / (1.0 - ADAM_B1 ** ADAM_STEP)
        v_hat = nv / (1.0 - ADAM_B2 ** ADAM_STEP)
        d_ref[...] = -ADAM_LR * (m_hat / (jnp.sqrt(v_hat) + ADAM_EPS) + ADAM_WD * w_ref[...])
        nm_ref[...] = nm
        nv_ref[...] = nv

    spec = pl.BlockSpec((rb, cols), lambda i: (i, 0))
    d, nm, nv = pl.pallas_call(
        body, name=name, out_shape=[jax.ShapeDtypeStruct((rows, cols), f32)] * 3,
        grid=(rows // rb,), in_specs=[spec] * 4, out_specs=[spec] * 3,
        compiler_params=_cparams(("parallel",)),
    )(w2, g2, m2, v2)
    return d.reshape(shape), nm.reshape(shape), nv.reshape(shape)


def _sum_with_sibling(g_halves, theirs, c_idx):
    n_sh, _, rows, cols = g_halves.shape
    rb = _tile(rows, 512, 2 * SUBLANE)

    def body(c_ref, mine_ref, theirs_ref, o_ref):
        o_ref[...] = (mine_ref[...] + theirs_ref[...]).astype(bf16)

    grid_spec = pltpu.PrefetchScalarGridSpec(
        num_scalar_prefetch=1, grid=(n_sh, rows // rb),
        in_specs=[pl.BlockSpec((None, None, rb, cols), lambda k, i, c_ref: (k, c_ref[0], i, 0)),
                  pl.BlockSpec((None, rb, cols), lambda k, i, c_ref: (k, i, 0))],
        out_specs=pl.BlockSpec((None, rb, cols), lambda k, i, c_ref: (k, i, 0)))
    return pl.pallas_call(
        body, name="grad_sum_sibling", out_shape=jax.ShapeDtypeStruct((n_sh, rows, cols), bf16),
        grid_spec=grid_spec, compiler_params=_cparams(("parallel", "parallel")),
    )(c_idx, g_halves, theirs)


def _sum_chips(partial, received, k_idx):
    _, rows, cols = partial.shape
    rb = _tile(rows, 512, 2 * SUBLANE)

    def body(k_ref, mine_ref, r_ref, o_ref):
        acc = mine_ref[...].astype(f32)
        for j in range(N_CHIPS - 1):
            acc = acc + r_ref[j].astype(f32)
        o_ref[...] = acc

    grid_spec = pltpu.PrefetchScalarGridSpec(
        num_scalar_prefetch=1, grid=(rows // rb,),
        in_specs=[pl.BlockSpec((None, rb, cols), lambda i, k_ref: (k_ref[0], i, 0)),
                  pl.BlockSpec((N_CHIPS - 1, rb, cols), lambda i, k_ref: (0, i, 0))],
        out_specs=pl.BlockSpec((rb, cols), lambda i, k_ref: (i, 0)))
    return pl.pallas_call(
        body, name="grad_sum_chips", out_shape=jax.ShapeDtypeStruct((rows, cols), f32),
        grid_spec=grid_spec, compiler_params=_cparams(("parallel",)),
    )(k_idx, partial, received)


HBM_SPEC = pl.BlockSpec(memory_space=pltpu.HBM)
CHIP_FLIPS = ((0, 1), (1, 0), (1, 1))


def _position():
    return lax.axis_index("x"), lax.axis_index("y"), lax.axis_index("c")


def _own_slot(gathered, mine, index):
    return [lax.dynamic_update_index_in_dim(g, m, index, 0) for g, m in zip(gathered, mine)]


def _gather_weights(blocks):
    n = len(blocks)
    n_far = len(CHIP_FLIPS)

    def body(*refs):
        srcs, outs = refs[:n], refs[n:2 * n]
        send_sems, recv_sems = refs[2 * n:]
        x, y, c = _position()
        k = 2 * x + y
        sibling = (x, y, 1 - c)
        first, passed = [], []
        for a in range(n):
            for j, (fx, fy) in enumerate(CHIP_FLIPS):
                s = a * 2 * n_far + j
                cp = pltpu.make_async_remote_copy(
                    src_ref=srcs[a].at[c], dst_ref=outs[a].at[k, c], send_sem=send_sems.at[s],
                    recv_sem=recv_sems.at[s], device_id=(x ^ fx, y ^ fy, c), device_id_type=MESH)
                cp.start()
                first.append(cp)
        for a in range(n):
            for j, (fx, fy) in enumerate(CHIP_FLIPS):
                s = a * 2 * n_far + j
                kk = 2 * (x ^ fx) + (y ^ fy)
                first[a * n_far + j].wait_recv()
                cp = pltpu.make_async_remote_copy(
                    src_ref=outs[a].at[kk, c], dst_ref=outs[a].at[kk, c], send_sem=send_sems.at[s + n_far],
                    recv_sem=recv_sems.at[s + n_far], device_id=sibling, device_id_type=MESH)
                cp.start()
                passed.append(cp)
        for cp in passed:
            cp.wait_recv()
        for cp in first + passed:
            cp.wait_send()

    return pl.pallas_call(
        body, name="gather_weights",
        out_shape=[jax.ShapeDtypeStruct((N_CHIPS,) + b.shape, b.dtype) for b in blocks],
        in_specs=[HBM_SPEC] * n, out_specs=[HBM_SPEC] * n,
        scratch_shapes=[pltpu.SemaphoreType.DMA((2 * n_far * n,)), pltpu.SemaphoreType.DMA((2 * n_far * n,))],
    )(*blocks)


def _swap_with_sibling(grads):
    n = len(grads)

    def body(*refs):
        srcs, outs = refs[:n], refs[n:2 * n]
        send_sems, recv_sems = refs[2 * n:]
        x, y, c = _position()
        copies = []
        for a in range(n):
            for kk in range(N_CHIPS):
                s = a * N_CHIPS + kk
                cp = pltpu.make_async_remote_copy(
                    src_ref=srcs[a].at[kk, 1 - c], dst_ref=outs[a].at[kk], send_sem=send_sems.at[s],
                    recv_sem=recv_sems.at[s], device_id=(x, y, 1 - c), device_id_type=MESH)
                cp.start()
                copies.append(cp)
        for cp in copies:
            cp.wait()

    return pl.pallas_call(
        body, name="grad_swap_sibling",
        out_shape=[jax.ShapeDtypeStruct((N_CHIPS,) + g.shape[2:], g.dtype) for g in grads],
        in_specs=[HBM_SPEC] * n, out_specs=[HBM_SPEC] * n,
        scratch_shapes=[pltpu.SemaphoreType.DMA((N_CHIPS * n,)), pltpu.SemaphoreType.DMA((N_CHIPS * n,))],
    )(*grads)


def _send_to_chips(partials):
    n = len(partials)
    n_far = len(CHIP_FLIPS)

    def body(*refs):
        srcs, outs = refs[:n], refs[n:2 * n]
        send_sems, recv_sems = refs[2 * n:]
        x, y, c = _position()
        copies = []
        for a in range(n):
            for j, (fx, fy) in enumerate(CHIP_FLIPS):
                s = a * n_far + j
                kk = 2 * (x ^ fx) + (y ^ fy)
                cp = pltpu.make_async_remote_copy(
                    src_ref=srcs[a].at[kk], dst_ref=outs[a].at[j], send_sem=send_sems.at[s],
                    recv_sem=recv_sems.at[s], device_id=(x ^ fx, y ^ fy, c), device_id_type=MESH)
                cp.start()
                copies.append(cp)
        for cp in copies:
            cp.wait()

    return pl.pallas_call(
        body, name="grad_to_chips",
        out_shape=[jax.ShapeDtypeStruct((n_far,) + p.shape[1:], p.dtype) for p in partials],
        in_specs=[HBM_SPEC] * n, out_specs=[HBM_SPEC] * n,
        scratch_shapes=[pltpu.SemaphoreType.DMA((n_far * n,)), pltpu.SemaphoreType.DMA((n_far * n,))],
    )(*partials)


def _join_halves(halves):
    n = len(halves)

    def body(*refs):
        srcs, outs = refs[:n], refs[n:2 * n]
        send_sems, recv_sems = refs[2 * n:]
        x, y, c = _position()
        copies = []
        for a in range(n):
            cp = pltpu.make_async_remote_copy(
                src_ref=srcs[a], dst_ref=outs[a].at[c], send_sem=send_sems.at[a], recv_sem=recv_sems.at[a],
                device_id=(x, y, 1 - c), device_id_type=MESH)
            cp.start()
            copies.append(cp)
        for cp in copies:
            cp.wait()

    return pl.pallas_call(
        body, name="grad_join_halves",
        out_shape=[jax.ShapeDtypeStruct((2,) + h.shape, h.dtype) for h in halves],
        in_specs=[HBM_SPEC] * n, out_specs=[HBM_SPEC] * n,
        scratch_shapes=[pltpu.SemaphoreType.DMA((n,)), pltpu.SemaphoreType.DMA((n,))],
    )(*halves)


def _all_sum_small(vec):
    rows, cols = vec.shape

    def body(v_ref, o_ref, buf, send_sems, recv_sems):
        x, y, c = _position()
        me = 4 * x + 2 * y + c
        buf[me] = v_ref[...]
        copies = []
        for m in range(1, N_DEV):
            fx, fy, fc = (m >> 2) & 1, (m >> 1) & 1, m & 1
            cp = pltpu.make_async_remote_copy(
                src_ref=v_ref, dst_ref=buf.at[me], send_sem=send_sems.at[m - 1], recv_sem=recv_sems.at[m - 1],
                device_id=(x ^ fx, y ^ fy, c ^ fc), device_id_type=MESH)
            cp.start()
            copies.append(cp)
        for cp in copies:
            cp.wait()
        acc = buf[0]
        for d in range(1, N_DEV):
            acc = acc + buf[d]
        o_ref[...] = acc

    return pl.pallas_call(
        body, name="all_sum_small", out_shape=jax.ShapeDtypeStruct((rows, cols), f32),
        in_specs=[pl.BlockSpec(memory_space=pltpu.VMEM)], out_specs=pl.BlockSpec(memory_space=pltpu.VMEM),
        scratch_shapes=[pltpu.VMEM((N_DEV, rows, cols), f32), pltpu.SemaphoreType.DMA((N_DEV - 1,)),
                        pltpu.SemaphoreType.DMA((N_DEV - 1,))],
        compiler_params=_cparams(),
    )(vec)


FLAT_QUANTUM = 2 * 2 * SUBLANE * FLAT_COLS


def _pack(arrays, dtype):
    flat = jnp.concatenate([a.astype(dtype).reshape(-1) for a in arrays])
    n = flat.shape[0]
    n_pad = -(-n // FLAT_QUANTUM) * FLAT_QUANTUM
    return jnp.pad(flat, (0, n_pad - n))


def _unpack(flat, shapes):
    out, off = [], 0
    for s in shapes:
        n = int(np.prod(s))
        out.append(flat[..., off:off + n].reshape(flat.shape[:-1] + tuple(s)))
        off += n
    return out


def _full_from_shards(stacked, axis):
    return jnp.concatenate([stacked[k] for k in range(N_CHIPS)], axis=axis)


def _shards_of(full, axis):
    return jnp.stack(jnp.split(full, N_CHIPS, axis=axis))


def _ffn_fwd(h, p):
    u = _row_fwd(_f_rms, [h], [p['g']], [(D_MODEL, bf16)], rows=512, name="ffn_norm")[0]
    a = _mm_w(u, p['up'], 'nn', "ffn_up")
    gated = _conv_fwd(_ffn_conv_plan(a, p), [(FFN_H, bf16, (0,))], "ffn_gate")[0]
    h_out = _mm(gated, p['down'], 'nn', "ffn_down", add=h)
    return h_out, (h, u, a, gated)


def _ffn_conv_plan(a, p):
    both = (0, FFN_H)
    return _ConvPlan(a, p['cw'], p['cb'], in_bases=both, mid_bases=both, width=FFN_H, rows=256, rs=FFN_STRIP_ROWS,
                     post=_post_ffn_gate)


def _ffn_bwd(dh_out, p, saved, bias_zero):
    h, u, a, gated = saved
    d_gated = _mm(dh_out, p['down'], 'nt', "ffn_down_dx", out_dtype=bf16)
    d_down = _mm(gated, dh_out, 'tn', "ffn_down_dw")
    da, d_cw, d_cb = _conv_bwd(_ffn_conv_plan(a, p), [(d_gated, (0,))], bf16, "ffn_gate_bwd")
    d_up = _mm(u, da, 'tn', "ffn_up_dw", out_cols_sharded=True)
    du = _mm_w(da, p['up'], 'nt', "ffn_up_dx", out_dtype=bf16)
    dh, d_g, d_bias = _row_bwd(_f_rms_res, [h], [p['g'], bias_zero], [du, dh_out], rows=512, name="ffn_norm_bwd")
    return dh, {'g': d_g, 'up': d_up, 'down': d_down, 'cw': d_cw, 'cb': d_cb}, d_bias


def _mixer_norm_bwd(h, g, du, dh_res, name):
    def f(hv, gv):
        return _rms(hv, gv), hv

    dh, d_g = _row_bwd(f, [h], [g], [du, dh_res], rows=512, name=name)
    return dh, d_g


def _ssd_layer_fwd(h, p):
    u = _row_fwd(_f_rms, [h], [p['g']], [(D_MODEL, bf16)], rows=512, name="ssd_norm")[0]
    z = _mm(u, p['w_z'], 'nn', "ssd_in_z")
    xbc = _mm(u, p['w_xbc'], 'nn', "ssd_in_xbc")
    dtr = _mm(u, p['w_dt'], 'nn', "ssd_in_dt")
    xc = _conv_fwd(_ssd_conv_plan(xbc, p), [(SSD_CONV_DIM, f32, (0,))], "ssd_conv")[0]
    dt = _row_fwd(_f_ssd_dt, [dtr], [p['dtb']], [(LANE, f32)], rows=1024, name="ssd_dt")[0]
    y, states = _ssd_fwd(xc, dt, p['a_log'], p['dsk'])
    yn = _row_fwd(_f_ssd_post, [y, z], [p['norm']], [(SSD_D_INNER, bf16)], rows=256, name="ssd_gate_norm")[0]
    h_out = _mm(yn, p['out'], 'nn', "ssd_out", add=h)
    return h_out, (h, u, z, xbc, dtr, xc, dt, states, y, yn)


def _ssd_conv_plan(xbc, p):
    return _ConvPlan(xbc, p['cw'], p['cb'], in_bases=(0,), mid_bases=(0,), width=SSD_CONV_DIM, rows=256,
                     rs=SSD_STRIP_ROWS, post=_post_silu)


def _ssd_layer_bwd(dh_out, p, saved):
    h, u, z, xbc, dtr, xc, dt, states, y, yn = saved
    d_yn = _mm(dh_out, p['out'], 'nt', "ssd_out_dx", out_dtype=bf16)
    d_out = _mm(yn, dh_out, 'tn', "ssd_out_dw")
    dy, dz, d_norm = _row_bwd(_f_ssd_post, [y, z], [p['norm']], [d_yn], rows=256, name="ssd_gate_norm_bwd",
                              tile_dtypes=[f32, bf16])
    dxc, ddt, d_alog, d_dsk = _ssd_bwd(xc, dt, p['a_log'], p['dsk'], states, dy)
    dxbc, d_cw, d_cb = _conv_bwd(_ssd_conv_plan(xbc, p), [(dxc, (0,))], bf16, "ssd_conv_bwd")
    ddtr, d_dtb = _row_bwd(_f_ssd_dt, [dtr], [p['dtb']], [ddt], rows=1024, name="ssd_dt_bwd", tile_dtypes=[bf16])
    d_wz = _mm(u, dz, 'tn', "ssd_in_z_dw")
    d_wxbc = _mm(u, dxbc, 'tn', "ssd_in_xbc_dw")
    d_wdt = _mm(u, ddtr, 'tn', "ssd_in_dt_dw")
    du = _mm(dz, p['w_z'], 'nt', "ssd_in_z_dx")
    du = _mm(dxbc, p['w_xbc'], 'nt', "ssd_in_xbc_dx", add=du)
    du = _mm(ddtr, p['w_dt'], 'nt', "ssd_in_dt_dx", add=du, out_dtype=bf16)
    dh, d_g = _mixer_norm_bwd(h, p['g'], du, dh_out, "ssd_norm_bwd")
    grads = {'g': d_g, 'w_z': d_wz, 'w_xbc': d_wxbc, 'w_dt': d_wdt, 'cw': d_cw, 'cb': d_cb, 'dtb': d_dtb,
             'a_log': d_alog, 'dsk': d_dsk, 'norm': d_norm, 'out': d_out}
    return dh, grads


def _conf_layer_fwd(h, p):
    u = _row_fwd(_f_rms, [h], [p['g']], [(D_MODEL, bf16)], rows=512, name="conf_norm")[0]
    g2 = _mm_w(u, p['pw1'], 'nn', "conf_pw1")
    conv = _conv_fwd(_conf_conv_plan(g2, p), [(D_MODEL, f32, (0,))], "conf_conv")[0]
    s = _row_fwd(_f_ln_silu, [conv], [p['ln_g'], p['ln_b']], [(D_MODEL, bf16)], rows=256, name="conf_ln")[0]
    h_out = _mm(s, p['pw2'], 'nn', "conf_pw2", bias=p['b2'], add=h)
    return h_out, (h, u, g2, conv, s)


def _conf_conv_plan(g2, p):
    halves = (0, D_MODEL)
    return _ConvPlan(g2, p['dw_w'], p['dw_b'], in_bases=halves, mid_bases=(0,), width=D_MODEL, rows=256,
                     rs=CONF_STRIP_ROWS, pre=_pre_glu, pre_params=[(p['b1'], halves)], post=_post_identity)


def _conf_layer_bwd(dh_out, p, saved):
    h, u, g2, conv, s = saved
    ds = _mm(dh_out, p['pw2'], 'nt', "conf_pw2_dx", out_dtype=bf16)
    d_pw2 = _mm(s, dh_out, 'tn', "conf_pw2_dw")
    d_conv, d_lng, d_lnb = _row_bwd(_f_ln_silu, [conv], [p['ln_g'], p['ln_b']], [ds], rows=256, name="conf_ln_bwd")
    dg2, d_dww, d_dwb, d_b1 = _conv_bwd(_conf_conv_plan(g2, p), [(d_conv, (0,))], bf16, "conf_conv_bwd")
    d_pw1 = _mm(u, dg2, 'tn', "conf_pw1_dw", out_cols_sharded=True)
    du = _mm_w(dg2, p['pw1'], 'nt', "conf_pw1_dx", out_dtype=bf16)
    dh, d_g = _mixer_norm_bwd(h, p['g'], du, dh_out, "conf_norm_bwd")
    grads = {'g': d_g, 'pw1': d_pw1, 'b1': d_b1, 'dw_w': d_dww, 'dw_b': d_dwb, 'ln_g': d_lng, 'ln_b': d_lnb,
             'pw2': d_pw2}
    return dh, grads


def _lru_params(p):
    return [p['in_b'], p['cw'], p['cb'], p['ga_w'], p['ga_b'], p['gx_w'], p['gx_b'], p['lam']]


def _lru_layer_fwd(h, p):
    u = _row_fwd(_f_rms, [h], [p['g']], [(D_MODEL, bf16)], rows=512, name="lru_norm")[0]
    io = _mm_w(u, p['in_w'], 'nn', "lru_in")
    a, b, gg = _row_fwd(_f_lru, [io], _lru_params(p), [(LRU_W, f32)] * 3, rows=256, name="lru_gates",
                        halo=SUBLANE, halo_of=[True])
    y, hs = _lru_fwd(a, b, gg)
    h_out = _mm(y, p['out'], 'nn', "lru_out", bias=p['out_b'], add=h)
    return h_out, (h, u, io, a, gg, hs, y)


def _lru_layer_bwd(dh_out, p, saved):
    h, u, io, a, gg, hs, y = saved
    dy = _mm(dh_out, p['out'], 'nt', "lru_out_dx")
    d_out = _mm(y, dh_out, 'tn', "lru_out_dw")
    da, db, dgg = _lru_bwd(dy, gg, a, hs)
    res = _row_bwd(_f_lru, [io], _lru_params(p), [da, db, dgg], rows=256, name="lru_gates_bwd",
                   halo=SUBLANE, halo_of=[True], tile_dtypes=[bf16])
    dio, d_inb, d_cw, d_cb, d_gaw, d_gab, d_gxw, d_gxb, d_lam = res
    d_inw = _mm(u, dio, 'tn', "lru_in_dw", out_cols_sharded=True)
    du = _mm_w(dio, p['in_w'], 'nt', "lru_in_dx", out_dtype=bf16)
    dh, d_g = _mixer_norm_bwd(h, p['g'], du, dh_out, "lru_norm_bwd")
    grads = {'g': d_g, 'in_w': d_inw, 'in_b': d_inb, 'cw': d_cw, 'cb': d_cb, 'ga_w': d_gaw, 'ga_b': d_gab,
             'gx_w': d_gxw, 'gx_b': d_gxb, 'lam': d_lam, 'out': d_out}
    return dh, grads


def _sgu_params(p):
    return [p['in_b'], p['ln_g'], p['ln_b'], p['sp_w'], p['sp_bt']]


def _sgu_layer_fwd(h, p):
    u = _row_fwd(_f_rms, [h], [p['g']], [(D_MODEL, bf16)], rows=512, name="sgu_norm")[0]
    z = _mm_w(u, p['in_w'], 'nn', "sgu_in")
    s = _row_fwd(_f_sgu, [z], _sgu_params(p), [(SGU_HALF, bf16)], rows=SGU_CHUNK, name="sgu_mix")[0]
    h_out = _mm(s, p['out'], 'nn', "sgu_out", bias=p['out_b'], add=h)
    return h_out, (h, u, z, s)


def _sgu_layer_bwd(dh_out, p, saved):
    h, u, z, s = saved
    ds = _mm(dh_out, p['out'], 'nt', "sgu_out_dx", out_dtype=bf16)
    d_out = _mm(s, dh_out, 'tn', "sgu_out_dw")
    dz, d_inb, d_lng, d_lnb, d_spw, d_spbt = _row_bwd(_f_sgu, [z], _sgu_params(p), [ds], rows=SGU_CHUNK,
                                                      name="sgu_mix_bwd", tile_dtypes=[bf16])
    d_inw = _mm(u, dz, 'tn', "sgu_in_dw", out_cols_sharded=True)
    du = _mm_w(dz, p['in_w'], 'nt', "sgu_in_dx", out_dtype=bf16)
    dh, d_g = _mixer_norm_bwd(h, p['g'], du, dh_out, "sgu_norm_bwd")
    grads = {'g': d_g, 'in_w': d_inw, 'in_b': d_inb, 'ln_g': d_lng, 'ln_b': d_lnb, 'sp_w': d_spw, 'sp_bt': d_spbt,
             'out': d_out}
    return dh, grads


def _row(v):
    return v.reshape((1, -1)).astype(f32)


def _pad_lanes(v, n=LANE):
    v = _row(v)
    return jnp.pad(v, ((0, 0), (0, n - v.shape[1])))


def _local_step(x, target, w):
    ffn = [{'g': _row(w['norm_ffn'][i]), 'up': (w['f_up_w'], i), 'down': w['f_down_w'][i],
            'cw': w['f_conv_w'][i].astype(f32), 'cb': _row(w['f_conv_b'][i])} for i in range(DEPTH)]
    a_in = w['a_in_proj'][0]
    pa = {'g': _row(w['norm_mix'][0]), 'w_z': a_in[:, :SSD_D_INNER],
          'w_xbc': a_in[:, SSD_D_INNER:SSD_D_INNER + SSD_CONV_DIM],
          'w_dt': jnp.pad(a_in[:, SSD_D_INNER + SSD_CONV_DIM:], ((0, 0), (0, LANE - SSD_HEADS))),
          'cw': w['a_conv_w'][0].astype(f32), 'cb': _row(w['a_conv_b'][0]), 'dtb': _pad_lanes(w['a_dt_bias'][0]),
          'a_log': _pad_lanes(w['a_log'][0]), 'dsk': _pad_lanes(w['a_d_skip'][0]), 'norm': _row(w['a_norm'][0]),
          'out': w['a_out_proj']}
    pb = {'g': _row(w['norm_mix'][1]), 'pw1': (w['b_pw1_w'], 0), 'b1': _row(w['b_pw1_b'][0]),
          'dw_w': w['b_dw_w'][0].astype(f32), 'dw_b': _row(w['b_dw_b'][0]), 'ln_g': _row(w['b_ln_g'][0]),
          'ln_b': _row(w['b_ln_b'][0]), 'pw2': w['b_pw2_w'], 'b2': _row(w['b_pw2_b'][0])}
    pc = {'g': _row(w['norm_mix'][2]), 'in_w': (w['c_in_w'], 0), 'in_b': _row(w['c_in_b'][0]),
          'cw': w['c_conv_w'][0].astype(f32), 'cb': _row(w['c_conv_b'][0]),
          'ga_w': w['c_ga_w'][0].reshape(LRU_W, LRU_BLOCK).astype(f32), 'ga_b': _row(w['c_ga_b'][0]),
          'gx_w': w['c_gx_w'][0].reshape(LRU_W, LRU_BLOCK).astype(f32), 'gx_b': _row(w['c_gx_b'][0]),
          'lam': _row(w['c_lambda'][0]), 'out': w['c_out_w'], 'out_b': _row(w['c_out_b'][0])}
    pd = {'g': _row(w['norm_mix'][3]), 'in_w': (w['d_in_w'], 0), 'in_b': _row(w['d_in_b'][0]),
          'ln_g': _row(w['d_ln_g'][0]), 'ln_b': _row(w['d_ln_b'][0]),
          'sp_w': w['d_sp_w'][0].reshape(SGU_GROUPS * SGU_CHUNK, SGU_CHUNK).astype(f32),
          'sp_bt': w['d_sp_b'][0].astype(f32).T, 'out': w['d_out_w'], 'out_b': _row(w['d_out_b'][0])}
    mixers = [(_ssd_layer_fwd, _ssd_layer_bwd, pa), (_conf_layer_fwd, _conf_layer_bwd, pb),
              (_lru_layer_fwd, _lru_layer_bwd, pc), (_sgu_layer_fwd, _sgu_layer_bwd, pd)]

    h = x
    saved = []
    for i in range(DEPTH):
        fwd, _, p = mixers[i]
        h, s_mix = fwd(h, p)
        h, s_ffn = _ffn_fwd(h, ffn[i])
        saved.append((s_mix, s_ffn))
    dh, d_final, loss = _loss_head(h, target, _row(w['norm_final']))

    bias_zero = jnp.zeros((1, D_MODEL), f32)
    g_ffn, g_mix, d_out_bias = [None] * DEPTH, [None] * DEPTH, [None] * DEPTH
    for i in reversed(range(DEPTH)):
        _, bwd, p = mixers[i]
        dh, g_ffn[i], d_out_bias[i] = _ffn_bwd(dh, ffn[i], saved[i][1], bias_zero)
        dh, g_mix[i] = bwd(dh, p, saved[i][0])
    ga, gb, gc, gd = g_mix

    def rows_sharded(g):
        return g.reshape(N_CHIPS, g.shape[0] // N_CHIPS, g.shape[1])

    grads = {
        'norm_mix': jnp.concatenate([g['g'] for g in g_mix], axis=0),
        'norm_ffn': jnp.concatenate([g['g'] for g in g_ffn], axis=0),
        'norm_final': d_final.reshape(-1),
        'a_in_proj': jnp.concatenate([ga['w_z'], ga['w_xbc'], ga['w_dt'][:, :SSD_HEADS]], axis=1)[None],
        'a_conv_w': ga['cw'][None], 'a_conv_b': ga['cb'], 'a_dt_bias': ga['dtb'][:, :SSD_HEADS],
        'a_log': ga['a_log'][:, :SSD_HEADS], 'a_d_skip': ga['dsk'][:, :SSD_HEADS], 'a_norm': ga['norm'],
        'a_out_proj': rows_sharded(ga['out']),
        'b_pw1_w': gb['pw1'], 'b_pw1_b': gb['b1'], 'b_dw_w': gb['dw_w'][None], 'b_dw_b': gb['dw_b'],
        'b_ln_g': gb['ln_g'], 'b_ln_b': gb['ln_b'], 'b_pw2_w': rows_sharded(gb['pw2']), 'b_pw2_b': d_out_bias[1],
        'c_in_w': gc['in_w'], 'c_in_b': gc['in_b'], 'c_conv_w': gc['cw'][None], 'c_conv_b': gc['cb'],
        'c_ga_w': gc['ga_w'].reshape(1, LRU_W // LRU_BLOCK, LRU_BLOCK, LRU_BLOCK),
        'c_ga_b': gc['ga_b'].reshape(1, LRU_W // LRU_BLOCK, LRU_BLOCK),
        'c_gx_w': gc['gx_w'].reshape(1, LRU_W // LRU_BLOCK, LRU_BLOCK, LRU_BLOCK),
        'c_gx_b': gc['gx_b'].reshape(1, LRU_W // LRU_BLOCK, LRU_BLOCK),
        'c_lambda': gc['lam'], 'c_out_w': rows_sharded(gc['out']), 'c_out_b': d_out_bias[2],
        'd_in_w': gd['in_w'], 'd_in_b': gd['in_b'], 'd_ln_g': gd['ln_g'], 'd_ln_b': gd['ln_b'],
        'd_sp_w': gd['sp_w'].reshape(1, SGU_GROUPS, SGU_CHUNK, SGU_CHUNK), 'd_sp_b': gd['sp_bt'].T[None],
        'd_out_w': rows_sharded(gd['out']), 'd_out_b': d_out_bias[3],
        'f_up_w': [g['up'] for g in g_ffn], 'f_conv_w': jnp.stack([g['cw'] for g in g_ffn]),
        'f_conv_b': jnp.concatenate([g['cb'] for g in g_ffn], axis=0),
        'f_down_w': [rows_sharded(g['down']) for g in g_ffn],
    }
    return loss, dh, grads


def _global_shape(name, shard_shape):
    ax = SHARD_AXIS[name]
    if ax is None:
        return tuple(shard_shape)
    s = list(shard_shape)
    s[ax] *= N_CHIPS
    return tuple(s)


def _step(x, target, weights, moments_m, moments_v):
    x2, t2 = x[0], target[0]
    shard_shapes = {n: weights[n].shape for n in WEIGHTS}
    c_pos = lax.axis_index("c")
    k_pos = 2 * lax.axis_index("x") + lax.axis_index("y")
    c_idx = c_pos.astype(jnp.int32).reshape(1)
    k_idx = k_pos.astype(jnp.int32).reshape(1)

    def halves_of(a):
        a2 = _as2d(a)
        return a2.reshape(2, a2.shape[0] // 2, a2.shape[1])

    mine = [halves_of(weights[n].astype(bf16)) for n in DIRECT]
    mine.append(_pack([weights[n] for n in PACKED_MM], bf16).reshape(2, -1, FLAT_COLS))
    mine.append(_pack([weights[n] for n in SHARDED_VEC], f32).reshape(2, -1, FLAT_COLS))
    gathered = _own_slot(_gather_weights(mine), mine, k_pos)
    w = {n: weights[n] for n in REPLICATED}
    for n, g in zip(DIRECT, gathered):
        g = g.reshape((N_CHIPS,) + shard_shapes[n])
        if n in DIRECT_COLS:
            w[n] = g
        elif n == 'f_down_w':
            w[n] = [g[:, i].reshape(-1, g.shape[-1]) for i in range(DEPTH)]
        else:
            w[n] = g.reshape(-1, g.shape[-1])
    all_mm = _unpack(gathered[-2].reshape(N_CHIPS, -1), [shard_shapes[n] for n in PACKED_MM])
    all_vec = _unpack(gathered[-1].reshape(N_CHIPS, -1), [shard_shapes[n] for n in SHARDED_VEC])
    for n, st in zip(PACKED_MM + SHARDED_VEC, all_mm + all_vec):
        w[n] = _full_from_shards(st, SHARD_AXIS[n])

    loss_part, dx, grads = _local_step(x2, t2, w)

    direct = []
    for n in DIRECT:
        direct += grads[n] if isinstance(grads[n], list) else [grads[n]]
    packed = [_shards_of(grads[n].reshape(_global_shape(n, shard_shapes[n])), SHARD_AXIS[n]).reshape(N_CHIPS, -1)
              for n in PACKED_MM + SHARDED_VEC]
    flat = jnp.concatenate(packed, axis=1)
    n_flat = flat.shape[1]
    n_pad = -(-n_flat // FLAT_QUANTUM) * FLAT_QUANTUM
    flat = jnp.pad(flat, ((0, 0), (0, n_pad - n_flat))).reshape(N_CHIPS, -1, FLAT_COLS)
    mine_g = [g.reshape(N_CHIPS, 2, g.shape[1] // 2, g.shape[2]) for g in direct + [flat]]
    theirs = _swap_with_sibling(mine_g)
    partials = [_sum_with_sibling(g, t, c_idx) for g, t in zip(mine_g, theirs)]
    received = _send_to_chips(partials)
    my_halves = [_sum_chips(p, r, k_idx) for p, r in zip(partials, received)]
    joined = _own_slot(_join_halves(my_halves), my_halves, c_pos)
    g_shard, pos = {}, 0
    for n in DIRECT:
        layers = shard_shapes[n][0]
        g_shard[n] = jnp.stack([j.reshape(shard_shapes[n][1:]) for j in joined[pos:pos + layers]])
        pos += layers
    flat_shapes = [shard_shapes[n] for n in PACKED_MM + SHARDED_VEC]
    g_shard.update(zip(PACKED_MM + SHARDED_VEC, _unpack(joined[-1].reshape(-1), flat_shapes)))

    small = jnp.concatenate([grads[n].reshape(-1) for n in REPLICATED] + [loss_part.reshape(-1)[:1]])
    n_small = small.shape[0]
    n_small_pad = -(-n_small // (SUBLANE * FLAT_COLS)) * (SUBLANE * FLAT_COLS)
    small = jnp.pad(small, (0, n_small_pad - n_small)).reshape(-1, FLAT_COLS)
    small = _all_sum_small(small).reshape(-1)
    g_rep = dict(zip(REPLICATED, _unpack(small, [shard_shapes[n] for n in REPLICATED])))
    loss = small[n_small - 1]

    g_all = {**g_shard, **g_rep}
    delta, new_m, new_v = {}, {}, {}
    for n in WEIGHTS:
        delta[n], new_m[n], new_v[n] = _adamw(weights[n], g_all[n], moments_m[n], moments_v[n], "adamw_" + n)
    return loss, dx[None], g_all, delta, new_m, new_v


def kernel(x, norm_mix, norm_ffn, norm_final, a_in_proj, a_conv_w, a_conv_b, a_dt_bias, a_log, a_d_skip, a_norm, a_out_proj, b_pw1_w, b_pw1_b, b_dw_w, b_dw_b, b_ln_g, b_ln_b, b_pw2_w, b_pw2_b, c_in_w, c_in_b, c_conv_w, c_conv_b, c_ga_w, c_ga_b, c_gx_w, c_gx_b, c_lambda, c_out_w, c_out_b, d_in_w, d_in_b, d_ln_g, d_ln_b, d_sp_w, d_sp_b, d_out_w, d_out_b, f_up_w, f_conv_w, f_conv_b, f_down_w, loss_target, m_norm_mix, m_norm_ffn, m_norm_final, m_a_in_proj, m_a_conv_w, m_a_conv_b, m_a_dt_bias, m_a_log, m_a_d_skip, m_a_norm, m_a_out_proj, m_b_pw1_w, m_b_pw1_b, m_b_dw_w, m_b_dw_b, m_b_ln_g, m_b_ln_b, m_b_pw2_w, m_b_pw2_b, m_c_in_w, m_c_in_b, m_c_conv_w, m_c_conv_b, m_c_ga_w, m_c_ga_b, m_c_gx_w, m_c_gx_b, m_c_lambda, m_c_out_w, m_c_out_b, m_d_in_w, m_d_in_b, m_d_ln_g, m_d_ln_b, m_d_sp_w, m_d_sp_b, m_d_out_w, m_d_out_b, m_f_up_w, m_f_conv_w, m_f_conv_b, m_f_down_w, v_norm_mix, v_norm_ffn, v_norm_final, v_a_in_proj, v_a_conv_w, v_a_conv_b, v_a_dt_bias, v_a_log, v_a_d_skip, v_a_norm, v_a_out_proj, v_b_pw1_w, v_b_pw1_b, v_b_dw_w, v_b_dw_b, v_b_ln_g, v_b_ln_b, v_b_pw2_w, v_b_pw2_b, v_c_in_w, v_c_in_b, v_c_conv_w, v_c_conv_b, v_c_ga_w, v_c_ga_b, v_c_gx_w, v_c_gx_b, v_c_lambda, v_c_out_w, v_c_out_b, v_d_in_w, v_d_in_b, v_d_ln_g, v_d_ln_b, v_d_sp_w, v_d_sp_b, v_d_out_w, v_d_out_b, v_f_up_w, v_f_conv_w, v_f_conv_b, v_f_down_w):
    args = locals()
    weights = {n: args[n] for n in WEIGHTS}
    moments_m = {n: args['m_' + n] for n in WEIGHTS}
    moments_v = {n: args['v_' + n] for n in WEIGHTS}
    loss, dx, grad, delta, new_m, new_v = _step(x, loss_target, weights, moments_m, moments_v)
    return (loss, dx, *[grad[n] for n in WEIGHTS], *[delta[n] for n in WEIGHTS],
            *[new_m[n] for n in WEIGHTS], *[new_v[n] for n in WEIGHTS])
```

```python
import functools
import math

import jax
import jax.numpy as jnp
import numpy as np
from jax import lax
from jax.experimental import pallas as pl
from jax.experimental.pallas import tpu as pltpu

f32 = jnp.float32
bf16 = jnp.bfloat16
MESH = pl.DeviceIdType.MESH
HIGHEST = lax.Precision.HIGHEST

D_MODEL = 1024
DEPTH = 4
RMS_EPS = 1e-6
LN_EPS = 1e-5
SSD_D_INNER = 2048
SSD_HEADS = 32
SSD_BC = 1024
SSD_CONV_DIM = 4096
SSD_CHUNK = 128
SSD_GROUPS = 8
LRU_W = 1280
LRU_BLOCK = 256
LRU_C = 8.0
SGU_HALF = 2048
SGU_GROUPS = 8
SGU_CHUNK = 128
FFN_H = 2816
ADAM_LR, ADAM_B1, ADAM_B2, ADAM_EPS, ADAM_WD, ADAM_STEP = 0.001, 0.9, 0.999, 1e-08, 0.01, 10

LANE = 128
SUBLANE = 8
VMEM_LIMIT = 56 * 1024 * 1024
FLAT_COLS = 1024

WEIGHTS = ['norm_mix', 'norm_ffn', 'norm_final', 'a_in_proj', 'a_conv_w', 'a_conv_b', 'a_dt_bias', 'a_log',
           'a_d_skip', 'a_norm', 'a_out_proj', 'b_pw1_w', 'b_pw1_b', 'b_dw_w', 'b_dw_b', 'b_ln_g', 'b_ln_b',
           'b_pw2_w', 'b_pw2_b', 'c_in_w', 'c_in_b', 'c_conv_w', 'c_conv_b', 'c_ga_w', 'c_ga_b', 'c_gx_w',
           'c_gx_b', 'c_lambda', 'c_out_w', 'c_out_b', 'd_in_w', 'd_in_b', 'd_ln_g', 'd_ln_b', 'd_sp_w',
           'd_sp_b', 'd_out_w', 'd_out_b', 'f_up_w', 'f_conv_w', 'f_conv_b', 'f_down_w']
SHARD_AXIS = {
    'norm_mix': None, 'norm_ffn': None, 'norm_final': None, 'a_in_proj': 2, 'a_conv_w': 2, 'a_conv_b': None,
    'a_dt_bias': None, 'a_log': None, 'a_d_skip': None, 'a_norm': None, 'a_out_proj': 1, 'b_pw1_w': 2,
    'b_pw1_b': 1, 'b_dw_w': 2, 'b_dw_b': 1, 'b_ln_g': 1, 'b_ln_b': 1, 'b_pw2_w': 1, 'b_pw2_b': 1, 'c_in_w': 2,
    'c_in_b': 1, 'c_conv_w': 2, 'c_conv_b': 1, 'c_ga_w': 2, 'c_ga_b': 2, 'c_gx_w': 2, 'c_gx_b': 2,
    'c_lambda': 1, 'c_out_w': 1, 'c_out_b': 1, 'd_in_w': 2, 'd_in_b': 1, 'd_ln_g': 1, 'd_ln_b': 1,
    'd_sp_w': None, 'd_sp_b': None, 'd_out_w': 1, 'd_out_b': 1, 'f_up_w': 2, 'f_conv_w': 2, 'f_conv_b': None,
    'f_down_w': 1}
MATMUL_WEIGHTS = ['a_in_proj', 'a_out_proj', 'b_pw1_w', 'b_pw2_w', 'c_in_w', 'c_ga_w', 'c_gx_w', 'c_out_w',
                  'd_in_w', 'd_out_w', 'f_up_w', 'f_down_w']
DIRECT_COLS = ['b_pw1_w', 'c_in_w', 'd_in_w', 'f_up_w']
DIRECT_ROWS = ['a_out_proj', 'b_pw2_w', 'c_out_w', 'd_out_w', 'f_down_w']
DIRECT = DIRECT_COLS + DIRECT_ROWS
EARLY_DIRECT = ['a_out_proj']
LATE_DIRECT = [n for n in DIRECT if n not in EARLY_DIRECT]
PACKED_MM = [n for n in MATMUL_WEIGHTS if n not in DIRECT]
SHARDED = [n for n in WEIGHTS if SHARD_AXIS[n] is not None]
SHARDED_VEC = [n for n in SHARDED if n not in MATMUL_WEIGHTS]
REPLICATED = [n for n in WEIGHTS if SHARD_AXIS[n] is None]
N_CHIPS = 4
N_DEV = 8


def _tile(n, cap, mult):
    if n <= cap:
        return n
    t = (cap // mult) * mult
    while t >= mult:
        if n % t == 0:
            return t
        t -= mult
    raise ValueError(f"no tile for {n} under {cap} in steps of {mult}")


def _cparams(sem=None):
    if sem is None:
        return pltpu.CompilerParams(vmem_limit_bytes=VMEM_LIMIT)
    return pltpu.CompilerParams(dimension_semantics=sem, vmem_limit_bytes=VMEM_LIMIT)


def _dg(a, b, ca, cb):
    return lax.dot_general(a.astype(bf16), b.astype(bf16), (((ca,), (cb,)), ((), ())), preferred_element_type=f32)


@jax.custom_vjp
def _dot_nn(a, b):
    return _dg(a, b, 1, 0)


def _dot_nn_fwd(a, b):
    return _dg(a, b, 1, 0), (a, b)


def _dot_nn_bwd(res, g):
    a, b = res
    return _dg(g, b, 1, 1).astype(a.dtype), _dg(a, g, 0, 0).astype(b.dtype)


_dot_nn.defvjp(_dot_nn_fwd, _dot_nn_bwd)


@jax.custom_vjp
def _dot_nt(a, b):
    return _dg(a, b, 1, 1)


def _dot_nt_fwd(a, b):
    return _dg(a, b, 1, 1), (a, b)


def _dot_nt_bwd(res, g):
    a, b = res
    return _dg(g, b, 1, 0).astype(a.dtype), _dg(g, a, 0, 0).astype(b.dtype)


_dot_nt.defvjp(_dot_nt_fwd, _dot_nt_bwd)


@jax.custom_vjp
def _dot_tn(a, b):
    return _dg(a, b, 0, 0)


def _dot_tn_fwd(a, b):
    return _dg(a, b, 0, 0), (a, b)


def _dot_tn_bwd(res, g):
    a, b = res
    return _dg(b, g, 1, 1).astype(a.dtype), _dg(a, g, 1, 0).astype(b.dtype)


_dot_tn.defvjp(_dot_tn_fwd, _dot_tn_bwd)


def _expm1(x):
    small = jnp.abs(x) < 0.03
    xs = jnp.where(small, x, 0.0)
    series = xs * (1.0 + xs * (0.5 + xs * (1.0 / 6.0 + xs * (1.0 / 24.0 + xs * (1.0 / 120.0)))))
    return jnp.where(small, series, jnp.exp(x) - 1.0)


def _rms(x, g):
    return x * lax.rsqrt(jnp.mean(x * x, axis=-1, keepdims=True) + RMS_EPS) * g


def _layer_norm(x, g, b):
    mu = jnp.mean(x, axis=-1, keepdims=True)
    xc = x - mu
    return xc * lax.rsqrt(jnp.mean(xc * xc, axis=-1, keepdims=True) + LN_EPS) * g + b


def _causal_taps(ext, w, halo, rows):
    k_taps = w.shape[0]
    acc = None
    for k in range(k_taps):
        lo = halo - (k_taps - 1) + k
        term = w[k:k + 1, :] * ext[lo:lo + rows, :]
        acc = term if acc is None else acc + term
    return acc


def _mm(a, b, mode, name, *, bias=None, add=None, out_dtype=f32, tm_cap=1408, tn_cap=1408, tk_cap=1408,
        b_cols_sharded=False, b_layer=None, out_cols_sharded=False):
    shard_cols = None
    if b_cols_sharded:
        shard_cols = b.shape[-1]
        b_dims = (b.shape[-2], N_CHIPS * shard_cols)
    else:
        b_dims = b.shape
    if mode == 'nn':
        (m, k), (k2, n) = a.shape, b_dims
    elif mode == 'nt':
        (m, k), (n, k2) = a.shape, b_dims
    else:
        (k, m), (k2, n) = a.shape, b_dims
    assert k == k2, (name, a.shape, b.shape)
    tm = _tile(m, tm_cap, LANE if mode == 'tn' else SUBLANE)
    tn = _tile(n, tn_cap, LANE)
    tk = _tile(k, tk_cap, LANE if mode != 'tn' else SUBLANE)
    if b_cols_sharded and mode == 'nn':
        tn = shard_cols
    if b_cols_sharded and mode == 'nt':
        tk = shard_cols
    if out_cols_sharded:
        assert mode == 'tn' and n % N_CHIPS == 0
        tn = n // N_CHIPS
    nk = k // tk

    def shard_block(rows):
        lead = (None,) * (b.ndim - 2)
        return lead + (rows, shard_cols)

    def shard_index(shard, row_block):
        return (shard, row_block, 0) if b_layer is None else (shard, b_layer, row_block, 0)

    if mode == 'nn':
        a_spec = pl.BlockSpec((tm, tk), lambda i, j, kk: (i, kk))
        if b_cols_sharded:
            b_spec = pl.BlockSpec(shard_block(tk), lambda i, j, kk: shard_index(j, kk))
        else:
            b_spec = pl.BlockSpec((tk, tn), lambda i, j, kk: (kk, j))
        ca, cb = 1, 0
    elif mode == 'nt':
        a_spec = pl.BlockSpec((tm, tk), lambda i, j, kk: (i, kk))
        if b_cols_sharded:
            b_spec = pl.BlockSpec(shard_block(tn), lambda i, j, kk: shard_index(kk, j))
        else:
            b_spec = pl.BlockSpec((tn, tk), lambda i, j, kk: (j, kk))
        ca, cb = 1, 1
    else:
        a_spec = pl.BlockSpec((tk, tm), lambda i, j, kk: (kk, i))
        b_spec = pl.BlockSpec((tk, tn), lambda i, j, kk: (kk, j))
        ca, cb = 0, 0
    in_specs, operands = [a_spec, b_spec], [a, b]
    if bias is not None:
        in_specs.append(pl.BlockSpec((1, tn), lambda i, j, kk: (0, j)))
        operands.append(bias)
    if add is not None:
        in_specs.append(pl.BlockSpec((tm, tn), lambda i, j, kk: (i, j)))
        operands.append(add)

    def body(*refs):
        a_ref, b_ref = refs[0], refs[1]
        pos = 2
        bias_ref = add_ref = None
        if bias is not None:
            bias_ref = refs[pos]
            pos += 1
        if add is not None:
            add_ref = refs[pos]
            pos += 1
        o_ref, acc_ref = refs[pos], refs[pos + 1]
        kk = pl.program_id(2)

        @pl.when(kk == 0)
        def _():
            acc_ref[...] = jnp.zeros_like(acc_ref)

        acc_ref[...] += _dg(a_ref[...], b_ref[...], ca, cb)

        @pl.when(kk == nk - 1)
        def _():
            r = acc_ref[...]
            if bias_ref is not None:
                r = r + bias_ref[...]
            if add_ref is not None:
                r = r + add_ref[...].astype(f32)
            o_ref[...] = r.astype(out_dtype)

    if out_cols_sharded:
        out_shape = jax.ShapeDtypeStruct((N_CHIPS, m, tn), out_dtype)
        out_spec = pl.BlockSpec((None, tm, tn), lambda i, j, kk: (j, i, 0))
    else:
        out_shape = jax.ShapeDtypeStruct((m, n), out_dtype)
        out_spec = pl.BlockSpec((tm, tn), lambda i, j, kk: (i, j))
    return pl.pallas_call(
        body, name=name, out_shape=out_shape,
        grid=(m // tm, n // tn, nk), in_specs=in_specs, out_specs=out_spec,
        scratch_shapes=[pltpu.VMEM((tm, tn), f32)],
        compiler_params=_cparams(("parallel", "parallel", "arbitrary")),
    )(*operands)


def _mm_w(a, w, mode, name, **kw):
    shards, layer = w
    return _mm(a, shards, mode, name, b_cols_sharded=True, b_layer=layer, **kw)


def _row_specs(tiles, halo_of, rows, halo, n_tiles, reverse):
    def tile_index(i):
        return n_tiles - 1 - i if reverse else i

    specs, operands = [], []
    for arr, has_halo in zip(tiles, halo_of):
        cols = arr.shape[1]
        specs.append(pl.BlockSpec((rows, cols), lambda i: (tile_index(i), 0)))
        operands.append(arr)
        if has_halo:
            per = rows // halo
            specs.append(pl.BlockSpec((halo, cols), lambda i: (jnp.maximum(tile_index(i) * per - 1, 0), 0)))
            operands.append(arr)
    return specs, operands, tile_index


def _load_tiles(refs, halo_of, tile_id, rows, halo):
    vals, pos = [], 0
    for has_halo in halo_of:
        cur = refs[pos][...].astype(f32)
        pos += 1
        if has_halo:
            before = refs[pos][...].astype(f32)
            pos += 1
            before = jnp.where(tile_id > 0, before, jnp.zeros_like(before))
            cur = jnp.concatenate([before, cur], axis=0)
        vals.append(cur)
    return vals, pos


def _valid_rows(tile_id, rows, halo):
    r = lax.broadcasted_iota(jnp.int32, (halo + rows, 1), 0)
    return jnp.logical_or(r >= halo, tile_id > 0).astype(f32)


def _row_fwd(f, tiles, params, outs, *, rows, name, halo=0, halo_of=None):
    t_len = tiles[0].shape[0]
    rows = min(rows, t_len)
    n_tiles = t_len // rows
    halo_of = halo_of or [False] * len(tiles)
    specs, operands, _ = _row_specs(tiles, halo_of, rows, halo, n_tiles, False)
    for p in params:
        specs.append(pl.BlockSpec(p.shape, lambda i: (0, 0)))
        operands.append(p)

    def body(*refs):
        i = pl.program_id(0)
        vals, pos = _load_tiles(refs, halo_of, i, rows, halo)
        pvals = [refs[pos + j][...] for j in range(len(params))]
        pos += len(params)
        kw = {'valid': _valid_rows(i, rows, halo)} if halo else {}
        res = f(*vals, *pvals, **kw)
        for o_ref, o in zip(refs[pos:], res):
            o_ref[...] = o.astype(o_ref.dtype)

    return pl.pallas_call(
        body, name=name,
        out_shape=[jax.ShapeDtypeStruct((t_len, c), d) for c, d in outs],
        grid=(n_tiles,), in_specs=specs,
        out_specs=[pl.BlockSpec((rows, c), lambda i: (i, 0)) for c, _ in outs],
        compiler_params=_cparams(("parallel",)),
    )(*operands)


def _row_bwd(f, tiles, params, cots, *, rows, name, halo=0, halo_of=None, tile_dtypes=None):
    t_len = tiles[0].shape[0]
    rows = min(rows, t_len)
    n_tiles = t_len // rows
    halo_of = halo_of or [False] * len(tiles)
    tile_dtypes = tile_dtypes or [f32] * len(tiles)
    specs, operands, tile_index = _row_specs(tiles, halo_of, rows, halo, n_tiles, True)
    for p in params:
        specs.append(pl.BlockSpec(p.shape, lambda i: (0, 0)))
        operands.append(p)
    for ct in cots:
        specs.append(pl.BlockSpec((rows, ct.shape[1]), lambda i: (tile_index(i), 0)))
        operands.append(ct)
    n_t, n_p, n_c = len(tiles), len(params), len(cots)
    out_shape = [jax.ShapeDtypeStruct(t.shape, d) for t, d in zip(tiles, tile_dtypes)]
    out_shape += [jax.ShapeDtypeStruct(p.shape, f32) for p in params]
    out_specs = [pl.BlockSpec((rows, t.shape[1]), lambda i: (tile_index(i), 0)) for t in tiles]
    out_specs += [pl.BlockSpec(p.shape, lambda i: (0, 0)) for p in params]
    scratch = [pltpu.VMEM((halo, t.shape[1]), f32) for t, h in zip(tiles, halo_of) if h]

    def body(*refs):
        i = pl.program_id(0)
        tile_id = tile_index(i)
        vals, pos = _load_tiles(refs, halo_of, tile_id, rows, halo)
        pvals = [refs[pos + j][...] for j in range(n_p)]
        pos += n_p
        cvals = [refs[pos + j][...].astype(f32) for j in range(n_c)]
        pos += n_c
        d_tile_refs = refs[pos:pos + n_t]
        d_param_refs = refs[pos + n_t:pos + n_t + n_p]
        carries = list(refs[pos + n_t + n_p:])
        kw = {'valid': _valid_rows(tile_id, rows, halo)} if halo else {}
        _, vjp = jax.vjp(lambda *args: tuple(f(*args, **kw)), *vals, *pvals)
        grads = vjp(tuple(cvals))

        @pl.when(i == 0)
        def _():
            for cr in carries:
                cr[...] = jnp.zeros_like(cr)
            for dp in d_param_refs:
                dp[...] = jnp.zeros_like(dp)

        ci = 0
        for t in range(n_t):
            g = grads[t]
            if halo_of[t]:
                cr = carries[ci]
                ci += 1
                d_tile_refs[t][0:rows - halo, :] = g[halo:rows, :].astype(d_tile_refs[t].dtype)
                d_tile_refs[t][rows - halo:rows, :] = (g[rows:rows + halo, :] + cr[...]).astype(d_tile_refs[t].dtype)
                cr[...] = g[0:halo, :]
            else:
                d_tile_refs[t][...] = g.astype(d_tile_refs[t].dtype)
        for j in range(n_p):
            d_param_refs[j][...] += grads[n_t + j]

    return pl.pallas_call(
        body, name=name, out_shape=out_shape, grid=(n_tiles,), in_specs=specs, out_specs=out_specs,
        scratch_shapes=scratch, compiler_params=_cparams(("arbitrary",)),
    )(*operands)


def _strip_specs(tiles, rows, halo, n_tiles, reverse):
    def tile_index(i):
        return n_tiles - 1 - i if reverse else i

    specs, operands = [], []
    for arr, _, has_halo in tiles:
        cols = arr.shape[1]
        specs.append(pl.BlockSpec((rows, cols), lambda i: (tile_index(i), 0)))
        operands.append(arr)
        if has_halo:
            per = rows // halo
            specs.append(pl.BlockSpec((halo, cols), lambda i: (jnp.maximum(tile_index(i) * per - 1, 0), 0)))
            operands.append(arr)
    return specs, operands, tile_index


def _strip_sources(refs, tiles, ext_scratch, tile_id, rows, halo):
    srcs, pos, si = [], 0, 0
    for _, _, has_halo in tiles:
        cur = refs[pos]
        pos += 1
        if has_halo:
            before = refs[pos]
            pos += 1
            scr = ext_scratch[si]
            si += 1
            scr[0:halo, :] = jnp.where(tile_id > 0, before[...].astype(f32), 0.0)
            scr[halo:halo + rows, :] = cur[...].astype(f32)
            srcs.append(scr)
        else:
            srcs.append(cur)
    return srcs, pos


def _cols(base, c0, cs):
    return pl.ds(pl.multiple_of(base + c0, LANE), cs)


def _strip_inputs(tiles, srcs, params, p_refs, r0, c0, rs, cs, halo):
    vals = []
    for (_, bases, has_halo), src in zip(tiles, srcs):
        n_rows = halo + rs if has_halo else rs
        for b in bases:
            vals.append(src[pl.ds(r0, n_rows), _cols(b, c0, cs)].astype(f32))
    for (_, bases), p_ref in zip(params, p_refs):
        for b in bases:
            vals.append(p_ref[:, _cols(b, c0, cs)])
    return vals


def _strip_valid(tile_id, r0, rs, halo):
    r = lax.broadcasted_iota(jnp.int32, (halo + rs, 1), 0) + r0
    return jnp.logical_or(r >= halo, tile_id > 0).astype(f32)


def _strip_fwd(f, tiles, params, outs, *, rows, rs, cs, width, name, halo=0):
    t_len = tiles[0][0].shape[0]
    rows = min(rows, t_len)
    n_tiles, n_rs, n_cs = t_len // rows, rows // rs, width // cs
    specs, operands, _ = _strip_specs(tiles, rows, halo, n_tiles, False)
    for p, _ in params:
        specs.append(pl.BlockSpec(p.shape, lambda i: (0, 0)))
        operands.append(p)
    n_p, n_o = len(params), len(outs)
    scratch = [pltpu.VMEM((halo + rows, arr.shape[1]), f32) for arr, _, hh in tiles if hh]

    def body(*refs):
        i = pl.program_id(0)
        ext_scratch = refs[len(refs) - len(scratch):]
        srcs, pos = _strip_sources(refs, tiles, ext_scratch, i, rows, halo)
        p_refs = refs[pos:pos + n_p]
        o_refs = refs[pos + n_p:pos + n_p + n_o]

        def row_loop(r, carry):
            r0 = pl.multiple_of(r * rs, rs)
            kw = {'valid': _strip_valid(i, r0, rs, halo)} if halo else {}

            def col_loop(c, carry2):
                c0 = c * cs
                res = f(*_strip_inputs(tiles, srcs, params, p_refs, r0, c0, rs, cs, halo), **kw)
                k = 0
                for (_, dt, bases), o_ref in zip(outs, o_refs):
                    for b in bases:
                        o_ref[pl.ds(r0, rs), _cols(b, c0, cs)] = res[k].astype(dt)
                        k += 1
                return carry2

            return lax.fori_loop(0, n_cs, col_loop, carry)

        lax.fori_loop(0, n_rs, row_loop, 0)

    return pl.pallas_call(
        body, name=name,
        out_shape=[jax.ShapeDtypeStruct((t_len, c), d) for c, d, _ in outs],
        grid=(n_tiles,), in_specs=specs,
        out_specs=[pl.BlockSpec((rows, c), lambda i: (i, 0)) for c, _, _ in outs],
        scratch_shapes=scratch, compiler_params=_cparams(("parallel",)),
    )(*operands)


def _strip_bwd(f, tiles, params, cots, *, rows, rs, cs, width, name, halo=0, tile_dtypes=None):
    t_len = tiles[0][0].shape[0]
    rows = min(rows, t_len)
    n_tiles, n_rs, n_cs = t_len // rows, rows // rs, width // cs
    tile_dtypes = tile_dtypes or [f32] * len(tiles)
    specs, operands, tile_index = _strip_specs(tiles, rows, halo, n_tiles, True)
    for p, _ in params:
        specs.append(pl.BlockSpec(p.shape, lambda i: (0, 0)))
        operands.append(p)
    for ct, _ in cots:
        specs.append(pl.BlockSpec((rows, ct.shape[1]), lambda i: (tile_index(i), 0)))
        operands.append(ct)
    n_t, n_p, n_c = len(tiles), len(params), len(cots)
    out_shape = [jax.ShapeDtypeStruct(t[0].shape, d) for t, d in zip(tiles, tile_dtypes)]
    out_shape += [jax.ShapeDtypeStruct(p.shape, f32) for p, _ in params]
    out_specs = [pl.BlockSpec((rows, t[0].shape[1]), lambda i: (tile_index(i), 0)) for t in tiles]
    out_specs += [pl.BlockSpec(p.shape, lambda i: (0, 0)) for p, _ in params]
    halo_tiles = [t for t in tiles if t[2]]
    scratch = [pltpu.VMEM((halo + rows, arr.shape[1]), f32) for arr, _, _ in halo_tiles]
    scratch += [pltpu.VMEM((halo + rows, arr.shape[1]), f32) for arr, _, _ in halo_tiles]
    scratch += [pltpu.VMEM((halo, arr.shape[1]), f32) for arr, _, _ in halo_tiles]
    n_h = len(halo_tiles)

    def body(*refs):
        i = pl.program_id(0)
        tile_id = tile_index(i)
        scr = refs[len(refs) - 3 * n_h:]
        ext_scratch, grad_scratch, carries = scr[:n_h], scr[n_h:2 * n_h], scr[2 * n_h:]
        srcs, pos = _strip_sources(refs, tiles, ext_scratch, tile_id, rows, halo)
        p_refs = refs[pos:pos + n_p]
        c_refs = refs[pos + n_p:pos + n_p + n_c]
        pos += n_p + n_c
        d_tile_refs = refs[pos:pos + n_t]
        d_param_refs = refs[pos + n_t:pos + n_t + n_p]

        @pl.when(i == 0)
        def _():
            for cr in carries:
                cr[...] = jnp.zeros_like(cr)
            for dp in d_param_refs:
                dp[...] = jnp.zeros_like(dp)

        for gs in grad_scratch:
            gs[...] = jnp.zeros_like(gs)

        def row_loop(r, carry):
            r0 = pl.multiple_of(r * rs, rs)
            kw = {'valid': _strip_valid(tile_id, r0, rs, halo)} if halo else {}

            def col_loop(c, carry2):
                c0 = c * cs
                vals = _strip_inputs(tiles, srcs, params, p_refs, r0, c0, rs, cs, halo)
                cvals = []
                for (_, bases), c_ref in zip(cots, c_refs):
                    for b in bases:
                        cvals.append(c_ref[pl.ds(r0, rs), _cols(b, c0, cs)].astype(f32))
                _, vjp = jax.vjp(lambda *args: tuple(f(*args, **kw)), *vals)
                grads = vjp(tuple(cvals))
                k, hi = 0, 0
                for t, (_, bases, has_halo) in enumerate(tiles):
                    for b in bases:
                        if has_halo:
                            grad_scratch[hi][pl.ds(r0, halo + rs), _cols(b, c0, cs)] += grads[k]
                        else:
                            d_tile_refs[t][pl.ds(r0, rs), _cols(b, c0, cs)] = grads[k].astype(d_tile_refs[t].dtype)
                        k += 1
                    hi += has_halo
                for (_, bases), dp in zip(params, d_param_refs):
                    for b in bases:
                        dp[:, _cols(b, c0, cs)] += grads[k]
                        k += 1
                return carry2

            return lax.fori_loop(0, n_cs, col_loop, carry)

        lax.fori_loop(0, n_rs, row_loop, 0)

        hi = 0
        for t, (_, _, has_halo) in enumerate(tiles):
            if has_halo:
                gs, cr, d_ref = grad_scratch[hi], carries[hi], d_tile_refs[t]
                hi += 1
                d_ref[0:rows - halo, :] = gs[halo:rows, :].astype(d_ref.dtype)
                d_ref[rows - halo:rows, :] = (gs[rows:rows + halo, :] + cr[...]).astype(d_ref.dtype)
                cr[...] = gs[0:halo, :]

    return pl.pallas_call(
        body, name=name, out_shape=out_shape, grid=(n_tiles,), in_specs=specs, out_specs=out_specs,
        scratch_shapes=scratch, compiler_params=_cparams(("arbitrary",)),
    )(*operands)


def _fold8(v):
    acc = v[0:SUBLANE, :]
    for m in range(1, v.shape[0] // SUBLANE):
        acc = acc + v[m * SUBLANE:(m + 1) * SUBLANE, :]
    return acc


class _ConvPlan:
    def __init__(self, x, w, b, *, in_bases, mid_bases, width, rows, rs, pre=None, pre_params=(), post=None):
        self.x, self.w, self.b = x, w, b
        self.in_bases, self.mid_bases, self.width = in_bases, mid_bases, width
        self.pre, self.pre_params, self.post = pre, list(pre_params), post
        self.k_taps = w.shape[0]
        tile_rows = SUBLANE * (4 // x.dtype.itemsize)
        self.halo = -(-(self.k_taps - 1) // tile_rows) * tile_rows
        self.t_len = x.shape[0]
        self.rows = min(rows, self.t_len)
        self.rs = rs
        self.n_tiles, self.n_rs, self.n_cs = self.t_len // self.rows, self.rows // rs, width // LANE
        self.n_mid = len(mid_bases)
        if pre is None:
            assert len(in_bases) == self.n_mid

    def in_specs(self, tile_index):
        cols = self.x.shape[1]
        per = self.rows // self.halo
        specs = [pl.BlockSpec((self.rows, cols), lambda i: (tile_index(i), 0)),
                 pl.BlockSpec((self.halo, cols), lambda i: (jnp.maximum(tile_index(i) * per - 1, 0), 0)),
                 pl.BlockSpec(self.w.shape, lambda i: (0, 0)), pl.BlockSpec(self.b.shape, lambda i: (0, 0))]
        operands = [self.x, self.x, self.w, self.b]
        for p, _ in self.pre_params:
            specs.append(pl.BlockSpec(p.shape, lambda i: (0, 0)))
            operands.append(p)
        return specs, operands

    def pre_strips(self, pp_refs, c0):
        return [p_ref[:, _cols(b, c0, LANE)] for (_, bases), p_ref in zip(self.pre_params, pp_refs) for b in bases]

    def fill_conv_input(self, cur_ref, before_ref, pp_refs, u_ref, tile_id):
        started = (tile_id > 0).astype(f32)

        def col_loop(c, carry):
            c0 = c * LANE
            pps = self.pre_strips(pp_refs, c0)
            xs = [before_ref[:, _cols(b, c0, LANE)].astype(f32) for b in self.in_bases]
            for j, u in enumerate(self.pre(*xs, *pps, valid=started)):
                u_ref[0:self.halo, _cols(j * self.width, c0, LANE)] = u
            for r in range(self.n_rs):
                xs = [cur_ref[r * self.rs:(r + 1) * self.rs, _cols(b, c0, LANE)].astype(f32) for b in self.in_bases]
                for j, u in enumerate(self.pre(*xs, *pps, valid=1.0)):
                    u_ref[self.halo + r * self.rs:self.halo + (r + 1) * self.rs, _cols(j * self.width, c0, LANE)] = u
            return carry

        lax.fori_loop(0, self.n_cs, col_loop, 0)

    def tap(self, cur_ref, before_ref, u_ref, tile_id, r, j, k, c0):
        lo = r * self.rs - (self.k_taps - 1) + k
        if u_ref is not None:
            return u_ref[self.halo + lo:self.halo + lo + self.rs, _cols(j * self.width, c0, LANE)]
        cols = _cols(self.in_bases[j], c0, LANE)
        if lo >= 0:
            return cur_ref[lo:lo + self.rs, cols].astype(f32)
        head = before_ref[self.halo + lo:self.halo, cols].astype(f32)
        head = jnp.where(tile_id > 0, head, 0.0)
        return jnp.concatenate([head, cur_ref[0:self.rs + lo, cols].astype(f32)], axis=0)

    def conv(self, cur_ref, before_ref, u_ref, w_ref, b_ref, tile_id, r, c0):
        hcs = []
        for j, mb in enumerate(self.mid_bases):
            cols = _cols(mb, c0, LANE)
            acc = b_ref[:, cols]
            for k in range(self.k_taps):
                acc = acc + w_ref[k:k + 1, cols] * self.tap(cur_ref, before_ref, u_ref, tile_id, r, j, k, c0)
            hcs.append(acc)
        return hcs


def _conv_fwd(plan, outs, name):
    n_pp = len(plan.pre_params)
    specs, operands = plan.in_specs(lambda i: i)
    scratch = [pltpu.VMEM((plan.halo + plan.rows, plan.n_mid * plan.width), f32)] if plan.pre else []

    def body(*refs):
        cur_ref, before_ref, w_ref, b_ref = refs[:4]
        pp_refs = refs[4:4 + n_pp]
        o_refs = refs[4 + n_pp:4 + n_pp + len(outs)]
        u_ref = refs[-1] if plan.pre else None
        i = pl.program_id(0)
        if plan.pre:
            plan.fill_conv_input(cur_ref, before_ref, pp_refs, u_ref, i)

        def col_loop(c, carry):
            c0 = c * LANE
            for r in range(plan.n_rs):
                res = plan.post(*plan.conv(cur_ref, before_ref, u_ref, w_ref, b_ref, i, r, c0))
                n = 0
                for (_, dt, bases), o_ref in zip(outs, o_refs):
                    for ob in bases:
                        o_ref[r * plan.rs:(r + 1) * plan.rs, _cols(ob, c0, LANE)] = res[n].astype(dt)
                        n += 1
            return carry

        lax.fori_loop(0, plan.n_cs, col_loop, 0)

    return pl.pallas_call(
        body, name=name, out_shape=[jax.ShapeDtypeStruct((plan.t_len, c), d) for c, d, _ in outs],
        grid=(plan.n_tiles,), in_specs=specs,
        out_specs=[pl.BlockSpec((plan.rows, c), lambda i: (i, 0)) for c, _, _ in outs],
        scratch_shapes=scratch, compiler_params=_cparams(("parallel",)),
    )(*operands)


def _conv_bwd(plan, cots, dx_dtype, name):
    n_pp, n_c = len(plan.pre_params), len(cots)
    n_tiles, rows, rs, halo, k_taps = plan.n_tiles, plan.rows, plan.rs, plan.halo, plan.k_taps

    def tile_index(i):
        return n_tiles - 1 - i

    specs, operands = plan.in_specs(tile_index)
    for ct, _ in cots:
        specs.append(pl.BlockSpec((rows, ct.shape[1]), lambda i: (tile_index(i), 0)))
        operands.append(ct)
    mid_cols = plan.n_mid * plan.width
    out_shape = [jax.ShapeDtypeStruct(plan.x.shape, dx_dtype), jax.ShapeDtypeStruct(plan.w.shape, f32),
                 jax.ShapeDtypeStruct(plan.b.shape, f32)]
    out_shape += [jax.ShapeDtypeStruct(p.shape, f32) for p, _ in plan.pre_params]
    out_specs = [pl.BlockSpec((rows, plan.x.shape[1]), lambda i: (tile_index(i), 0)),
                 pl.BlockSpec(plan.w.shape, lambda i: (0, 0)), pl.BlockSpec(plan.b.shape, lambda i: (0, 0))]
    out_specs += [pl.BlockSpec(p.shape, lambda i: (0, 0)) for p, _ in plan.pre_params]
    w_cols = plan.w.shape[1]
    scratch = [pltpu.VMEM((rows + halo, mid_cols), f32),
               pltpu.VMEM((halo, mid_cols), f32),
               pltpu.VMEM(((k_taps + 1) * SUBLANE, w_cols), f32)]
    if plan.pre:
        scratch.append(pltpu.VMEM((halo + rows, mid_cols), f32))

    def body(*refs):
        cur_ref, before_ref, w_ref, b_ref = refs[:4]
        pp_refs = refs[4:4 + n_pp]
        c_refs = refs[4 + n_pp:4 + n_pp + n_c]
        pos = 4 + n_pp + n_c
        dx_ref, dw_ref, db_ref = refs[pos:pos + 3]
        dpp_refs = refs[pos + 3:pos + 3 + n_pp]
        g_ref, carry_ref, acc_ref = refs[pos + 3 + n_pp:pos + 6 + n_pp]
        u_ref = refs[-1] if plan.pre else None
        i = pl.program_id(0)
        tile_id = tile_index(i)

        @pl.when(i == 0)
        def _():
            carry_ref[...] = jnp.zeros_like(carry_ref)
            acc_ref[...] = jnp.zeros_like(acc_ref)
            for dp in dpp_refs:
                dp[...] = jnp.zeros_like(dp)

        g_ref[rows:rows + halo, :] = carry_ref[...]
        if plan.pre:
            plan.fill_conv_input(cur_ref, before_ref, pp_refs, u_ref, tile_id)

        def col_loop(c, carry):
            c0 = c * LANE
            for r in range(plan.n_rs):
                hcs = plan.conv(cur_ref, before_ref, u_ref, w_ref, b_ref, tile_id, r, c0)
                _, vjp = jax.vjp(lambda *a: tuple(plan.post(*a)), *hcs)
                cvals = [c_ref[r * rs:(r + 1) * rs, _cols(cb, c0, LANE)].astype(f32)
                         for (_, bases), c_ref in zip(cots, c_refs) for cb in bases]
                d_hcs = vjp(tuple(cvals))
                for j, mb in enumerate(plan.mid_bases):
                    g_ref[r * rs:(r + 1) * rs, _cols(j * plan.width, c0, LANE)] = d_hcs[j]
                    wc = _cols(mb, c0, LANE)
                    acc_ref[k_taps * SUBLANE:(k_taps + 1) * SUBLANE, wc] += _fold8(d_hcs[j])
                    for k in range(k_taps):
                        x_k = plan.tap(cur_ref, before_ref, u_ref, tile_id, r, j, k, c0)
                        acc_ref[k * SUBLANE:(k + 1) * SUBLANE, wc] += _fold8(d_hcs[j] * x_k)
            pps = plan.pre_strips(pp_refs, c0)
            for r in range(plan.n_rs):
                d_us = []
                for j, mb in enumerate(plan.mid_bases):
                    wc = _cols(mb, c0, LANE)
                    acc = None
                    for k in range(k_taps):
                        lo = r * rs + (k_taps - 1) - k
                        term = w_ref[k:k + 1, wc] * g_ref[lo:lo + rs, _cols(j * plan.width, c0, LANE)]
                        acc = term if acc is None else acc + term
                    d_us.append(acc)
                if plan.pre is None:
                    d_xs = d_us
                else:
                    xs = [cur_ref[r * rs:(r + 1) * rs, _cols(b, c0, LANE)].astype(f32) for b in plan.in_bases]
                    _, vjp_pre = jax.vjp(lambda *a: tuple(plan.pre(*a, valid=1.0)), *xs, *pps)
                    grads = vjp_pre(tuple(d_us))
                    d_xs = grads[:len(xs)]
                    n = len(xs)
                    for (_, bases), dp in zip(plan.pre_params, dpp_refs):
                        for pb in bases:
                            dp[:, _cols(pb, c0, LANE)] += grads[n]
                            n += 1
                for b, d_x in zip(plan.in_bases, d_xs):
                    dx_ref[r * rs:(r + 1) * rs, _cols(b, c0, LANE)] = d_x.astype(dx_dtype)
            return carry

        lax.fori_loop(0, plan.n_cs, col_loop, 0)
        carry_ref[...] = g_ref[0:halo, :]

        @pl.when(i == n_tiles - 1)
        def _():
            for k in range(k_taps):
                dw_ref[k:k + 1, :] = jnp.sum(acc_ref[k * SUBLANE:(k + 1) * SUBLANE, :], axis=0, keepdims=True)
            db_ref[...] = jnp.sum(acc_ref[k_taps * SUBLANE:(k_taps + 1) * SUBLANE, :], axis=0, keepdims=True)

    return pl.pallas_call(
        body, name=name, out_shape=out_shape, grid=(n_tiles,), in_specs=specs, out_specs=out_specs,
        scratch_shapes=scratch, compiler_params=_cparams(("arbitrary",)),
    )(*operands)


def _f_rms(h, g):
    return (_rms(h, g),)


def _f_rms_res(h, g, bz):
    hh = h + bz
    return _rms(hh, g), hh


def _post_ffn_gate(gate, val):
    return (jax.nn.silu(gate) * val,)


def _post_silu(h):
    return (jax.nn.silu(h),)


def _post_identity(h):
    return (h,)


def _pre_glu(g_a, g_b, b_a, b_b, *, valid):
    return ((g_a + b_a) * jax.nn.sigmoid(g_b + b_b) * valid,)


def _f_ssd_dt(dtr, dtb):
    real = lax.broadcasted_iota(jnp.int32, (1, LANE), 1) < SSD_HEADS
    return (jnp.where(real, jax.nn.softplus(dtr + dtb), 0.0),)


def _f_ssd_post(y, z, g):
    return (_rms(y * jax.nn.silu(z), g),)


CONF_HALO = 32
FFN_STRIP_ROWS = 64
CONF_STRIP_ROWS = 128
SSD_STRIP_ROWS = 64


def _f_ln_silu(x, g, b):
    return (jax.nn.silu(_layer_norm(x, g, b)),)


def _f_lru(io_ext, in_b, cw, cb, ga_w, ga_b, gx_w, gx_b, lam, *, valid):
    rows = io_ext.shape[0] - SUBLANE
    io = (io_ext + in_b) * valid
    gate = io[SUBLANE:, :LRU_W]
    xr = _causal_taps(io[:, LRU_W:], cw, SUBLANE, rows) + cb
    rs, iis = [], []
    for blk in range(LRU_W // LRU_BLOCK):
        sl = slice(blk * LRU_BLOCK, (blk + 1) * LRU_BLOCK)
        xb = xr[:, sl]
        rs.append(jax.nn.sigmoid(_dot_nn(xb, ga_w[sl, :]) + ga_b[:, sl]))
        iis.append(jax.nn.sigmoid(_dot_nn(xb, gx_w[sl, :]) + gx_b[:, sl]))
    r = jnp.concatenate(rs, axis=1)
    ig = jnp.concatenate(iis, axis=1)
    log_a = -LRU_C * r * jax.nn.softplus(-lam)
    a = jnp.exp(log_a)
    bterm = jnp.sqrt(-_expm1(2.0 * log_a)) * (ig * xr)
    return a, bterm, jax.nn.gelu(gate)


def _f_sgu(z, in_b, ln_g, ln_b, sp_w, sp_bt):
    rows = z.shape[0]
    zz = jax.nn.gelu(z + in_b)
    u, v = zz[:, :SGU_HALF], zz[:, SGU_HALF:]
    v = _layer_norm(v, ln_g, ln_b)
    tri = lax.broadcasted_iota(jnp.int32, (SGU_CHUNK, SGU_CHUNK), 0) >= lax.broadcasted_iota(
        jnp.int32, (SGU_CHUNK, SGU_CHUNK), 1)
    gdim = SGU_HALF // SGU_GROUPS
    row_blocks = []
    for ci in range(rows // SGU_CHUNK):
        col_blocks = []
        for g in range(SGU_GROUPS):
            w = jnp.where(tri, sp_w[g * SGU_CHUNK:(g + 1) * SGU_CHUNK, :], 0.0)
            vb = v[ci * SGU_CHUNK:(ci + 1) * SGU_CHUNK, g * gdim:(g + 1) * gdim]
            col_blocks.append(_dot_nn(w, vb) + sp_bt[:, g:g + 1])
        row_blocks.append(jnp.concatenate(col_blocks, axis=1))
    mixed = row_blocks[0] if len(row_blocks) == 1 else jnp.concatenate(row_blocks, axis=0)
    return (u * mixed,)


HEADS_PER_GROUP = 4
GROUP_COLS = 256
HEAD_DIM = 64


def _ssd_group(x, bm, cm, dt, st, a_log, dsk, g):
    q = x.shape[0]
    tri = lax.broadcasted_iota(jnp.int32, (q, q), 0) >= lax.broadcasted_iota(jnp.int32, (q, q), 1)
    d_a = dt * (-jnp.exp(a_log))
    acs = jnp.dot(tri.astype(f32), d_a, precision=HIGHEST, preferred_element_type=f32)
    acs_t = acs.T
    lane = lax.broadcasted_iota(jnp.int32, (1, LANE), 1)
    sub = lax.broadcasted_iota(jnp.int32, (LANE, 1), 0)
    col_idx = lax.broadcasted_iota(jnp.int32, (1, GROUP_COLS), 1)
    last_row = (lax.broadcasted_iota(jnp.int32, (q, 1), 0) == q - 1).astype(f32)
    cb = _dot_nt(cm, bm)
    y = jnp.zeros((q, GROUP_COLS), f32)
    e_in = jnp.zeros((q, GROUP_COLS), f32)
    d_end = jnp.zeros((q, GROUP_COLS), f32)
    d_last = jnp.zeros((1, GROUP_COLS), f32)
    d_skip = jnp.zeros((1, GROUP_COLS), f32)
    for j in range(HEADS_PER_GROUP):
        head = HEADS_PER_GROUP * g + j
        on_lane = (lane == head).astype(f32)
        on_sub = (sub == head).astype(f32)
        col = jnp.sum(acs * on_lane, axis=1, keepdims=True)
        row = jnp.sum(acs_t * on_sub, axis=0, keepdims=True)
        dtc = jnp.sum(dt * on_lane, axis=1, keepdims=True)
        last = jnp.sum(col * last_row, axis=0, keepdims=True)
        dsk_j = jnp.sum(dsk * on_lane, axis=1, keepdims=True)
        decay = jnp.where(tri, jnp.exp(jnp.where(tri, col - row, 0.0)), 0.0)
        mine = jnp.logical_and(col_idx >= j * HEAD_DIM, col_idx < (j + 1) * HEAD_DIM)
        y = y + _dot_nn(cb * decay, jnp.where(mine, x * dtc, 0.0))
        e_in = e_in + jnp.where(mine, jnp.exp(col), 0.0)
        d_end = d_end + jnp.where(mine, jnp.exp(last - col) * dtc, 0.0)
        d_last = d_last + jnp.where(mine, jnp.exp(last), 0.0)
        d_skip = d_skip + jnp.where(mine, dsk_j, 0.0)
    y = y + _dot_nn(cm, st) * e_in + x * d_skip
    st_new = st * d_last + _dot_tn(bm, x * d_end)
    return y, st_new


GROUPS_PER_STEP = 2


def _ssd_specs(rev, nc):
    def ch(c):
        return nc - 1 - c if rev else c

    gps = GROUPS_PER_STEP
    x_spec = pl.BlockSpec((SSD_CHUNK, gps * GROUP_COLS), lambda c, g: (ch(c), g))
    b_spec = pl.BlockSpec((SSD_CHUNK, gps * LANE), lambda c, g: (ch(c), SSD_D_INNER // (gps * LANE) + g))
    c_spec = pl.BlockSpec((SSD_CHUNK, gps * LANE), lambda c, g: (ch(c), (SSD_D_INNER + SSD_BC) // (gps * LANE) + g))
    dt_spec = pl.BlockSpec((SSD_CHUNK, LANE), lambda c, g: (ch(c), 0))
    row_spec = pl.BlockSpec((1, LANE), lambda c, g: (0, 0))
    st_spec = pl.BlockSpec((1, gps, LANE, GROUP_COLS), lambda c, g: (ch(c), g, 0, 0))
    wide_spec = pl.BlockSpec((SSD_CHUNK, SSD_CONV_DIM), lambda c, g: (ch(c), 0))
    return x_spec, b_spec, c_spec, dt_spec, row_spec, st_spec, wide_spec


def _ssd_fwd(xc, dt, a_log, dsk, gather=()):
    t_len = xc.shape[0]
    nc = t_len // SSD_CHUNK
    gps = GROUPS_PER_STEP
    n_gp = SSD_GROUPS // gps
    n_g = len(gather)
    x_spec, b_spec, c_spec, dt_spec, row_spec, st_spec, _ = _ssd_specs(False, nc)

    def body(*refs):
        x_ref, b_ref, c_ref, dt_ref, al_ref, dk_ref = refs[:6]
        g_srcs = refs[6:6 + n_g]
        y_ref, st_out_ref = refs[6 + n_g:8 + n_g]
        g_outs = refs[8 + n_g:8 + 2 * n_g]
        st_ref = refs[8 + 2 * n_g]
        g_sems = refs[9 + 2 * n_g:]
        c, gp = pl.program_id(0), pl.program_id(1)
        if n_g:
            @pl.when(jnp.logical_and(c == 0, gp == 0))
            def _():
                _gather_start(g_srcs, g_outs, *g_sems)

        for q in range(gps):
            g = gp * gps + q

            @pl.when(c == 0)
            def _():
                st_ref[g] = jnp.zeros((LANE, GROUP_COLS), f32)

            st = st_ref[g]
            st_out_ref[0, q] = st
            xq = slice(q * GROUP_COLS, (q + 1) * GROUP_COLS)
            bq = slice(q * LANE, (q + 1) * LANE)
            y, st_new = _ssd_group(x_ref[:, xq], b_ref[:, bq], c_ref[:, bq], dt_ref[...], st, al_ref[...],
                                   dk_ref[...], g)
            y_ref[:, xq] = y
            st_ref[g] = st_new

        if n_g:
            @pl.when(jnp.logical_and(c == nc - 1, gp == n_gp - 1))
            def _():
                _gather_finish(g_srcs, g_outs, *g_sems)

    res = pl.pallas_call(
        body, name="ssd_scan_fwd",
        out_shape=[jax.ShapeDtypeStruct((t_len, SSD_D_INNER), f32),
                   jax.ShapeDtypeStruct((nc, SSD_GROUPS, LANE, GROUP_COLS), f32)] + _gather_out_shapes(gather),
        grid=(nc, n_gp), in_specs=[x_spec, b_spec, c_spec, dt_spec, row_spec, row_spec] + [HBM_SPEC] * n_g,
        out_specs=[x_spec, st_spec] + [HBM_SPEC] * n_g,
        scratch_shapes=[pltpu.VMEM((SSD_GROUPS, LANE, GROUP_COLS), f32)] + (_gather_semaphores(n_g) if n_g else []),
        compiler_params=_cparams(("arbitrary", "arbitrary")),
    )(xc, xc, xc, dt, a_log, dsk, *gather)
    return res[0], res[1], list(res[2:])


def _ssd_bwd(xc, dt, a_log, dsk, states, dy, to_chips=()):
    t_len = xc.shape[0]
    nc = t_len // SSD_CHUNK
    gps = GROUPS_PER_STEP
    n_gp = SSD_GROUPS // gps
    n_s = len(to_chips)
    x_spec, b_spec, c_spec, dt_spec, row_spec, st_spec, wide_spec = _ssd_specs(True, nc)

    def body(*refs):
        x_ref, b_ref, c_ref, dt_ref, al_ref, dk_ref, st_in_ref, dy_ref = refs[:8]
        s_srcs = refs[8:8 + n_s]
        dxc_ref, ddt_ref, dal_ref, ddk_ref = refs[8 + n_s:12 + n_s]
        s_outs = refs[12 + n_s:12 + 2 * n_s]
        dst_ref = refs[12 + 2 * n_s]
        s_sems = refs[13 + 2 * n_s:]
        c, gp = pl.program_id(0), pl.program_id(1)

        @pl.when(jnp.logical_and(c == 0, gp == 0))
        def _():
            dal_ref[...] = jnp.zeros_like(dal_ref)
            ddk_ref[...] = jnp.zeros_like(ddk_ref)
            for cp in _to_chips_copies(s_srcs, s_outs, *s_sems) if n_s else []:
                cp.start()

        @pl.when(gp == 0)
        def _():
            ddt_ref[...] = jnp.zeros_like(ddt_ref)

        for q in range(gps):
            g = gp * gps + q

            @pl.when(c == 0)
            def _():
                dst_ref[g] = jnp.zeros((LANE, GROUP_COLS), f32)

            xq = slice(q * GROUP_COLS, (q + 1) * GROUP_COLS)
            bq = slice(q * LANE, (q + 1) * LANE)
            _, vjp = jax.vjp(lambda *args: _ssd_group(*args, g), x_ref[:, xq], b_ref[:, bq], c_ref[:, bq],
                             dt_ref[...], st_in_ref[0, q], al_ref[...], dk_ref[...])
            dx, db, dc, ddt, dst, dal, ddk = vjp((dy_ref[:, xq], dst_ref[g]))
            dxc_ref[:, pl.ds(pl.multiple_of(g * GROUP_COLS, GROUP_COLS), GROUP_COLS)] = dx
            dxc_ref[:, pl.ds(pl.multiple_of(SSD_D_INNER + g * LANE, LANE), LANE)] = db
            dxc_ref[:, pl.ds(pl.multiple_of(SSD_D_INNER + SSD_BC + g * LANE, LANE), LANE)] = dc
            ddt_ref[...] += ddt
            dst_ref[g] = dst
            dal_ref[...] += dal
            ddk_ref[...] += ddk

        if n_s:
            @pl.when(jnp.logical_and(c == nc - 1, gp == n_gp - 1))
            def _():
                for cp in _to_chips_copies(s_srcs, s_outs, *s_sems):
                    cp.wait()

    res = pl.pallas_call(
        body, name="ssd_scan_bwd",
        out_shape=[jax.ShapeDtypeStruct((t_len, SSD_CONV_DIM), f32), jax.ShapeDtypeStruct((t_len, LANE), f32),
                   jax.ShapeDtypeStruct((1, LANE), f32), jax.ShapeDtypeStruct((1, LANE), f32)]
        + _to_chips_out_shapes(to_chips),
        grid=(nc, n_gp),
        in_specs=[x_spec, b_spec, c_spec, dt_spec, row_spec, row_spec, st_spec, x_spec] + [HBM_SPEC] * n_s,
        out_specs=[wide_spec, dt_spec, row_spec, row_spec] + [HBM_SPEC] * n_s,
        scratch_shapes=[pltpu.VMEM((SSD_GROUPS, LANE, GROUP_COLS), f32)] + (_to_chips_semaphores(n_s) if n_s else []),
        compiler_params=_cparams(("arbitrary", "arbitrary")),
    )(xc, xc, xc, dt, a_log, dsk, states, dy, *to_chips)
    return res[0], res[1], res[2], res[3], list(res[4:])


LRU_ROWS = 256


def _lru_fwd(a, b, gg):
    t_len, cols = a.shape
    rows = min(LRU_ROWS, t_len)
    spec = pl.BlockSpec((rows, cols), lambda i: (i, 0))

    def body(a_ref, b_ref, g_ref, y_ref, h_ref, carry):
        i = pl.program_id(0)

        @pl.when(i == 0)
        def _():
            carry[...] = jnp.zeros_like(carry)

        av, bv = a_ref[...], b_ref[...]
        row = lax.broadcasted_iota(jnp.int32, av.shape, 0)
        s = 1
        while s < rows:
            a_prev = pltpu.roll(av, s, axis=0)
            b_prev = pltpu.roll(bv, s, axis=0)
            m = row >= s
            bv = jnp.where(m, av * b_prev + bv, bv)
            av = jnp.where(m, av * a_prev, av)
            s *= 2
        h = av * carry[0:1, :] + bv
        h_ref[...] = h
        y_ref[...] = g_ref[...] * h
        carry[0:1, :] = h[rows - 1:rows, :]

    return pl.pallas_call(
        body, name="lru_scan_fwd",
        out_shape=[jax.ShapeDtypeStruct((t_len, cols), f32), jax.ShapeDtypeStruct((t_len, cols), f32)],
        grid=(t_len // rows,), in_specs=[spec, spec, spec], out_specs=[spec, spec],
        scratch_shapes=[pltpu.VMEM((SUBLANE, cols), f32)],
        compiler_params=_cparams(("arbitrary",)),
    )(a, b, gg)


def _lru_bwd(dy, gg, a, h):
    t_len, cols = a.shape
    rows = min(LRU_ROWS, t_len)
    n_tiles = t_len // rows
    per = rows // SUBLANE
    spec = pl.BlockSpec((rows, cols), lambda i: (n_tiles - 1 - i, 0))
    prev_spec = pl.BlockSpec((SUBLANE, cols), lambda i: (jnp.maximum((n_tiles - 1 - i) * per - 1, 0), 0))

    def body(dy_ref, g_ref, a_ref, h_ref, hp_ref, da_ref, db_ref, dg_ref, carry_dh, carry_a):
        i = pl.program_id(0)
        tile_id = n_tiles - 1 - i

        @pl.when(i == 0)
        def _():
            carry_dh[...] = jnp.zeros_like(carry_dh)
            carry_a[...] = jnp.zeros_like(carry_a)

        av, hv, dyv = a_ref[...], h_ref[...], dy_ref[...]
        row = lax.broadcasted_iota(jnp.int32, av.shape, 0)
        dg_ref[...] = dyv * hv
        bv = dyv * g_ref[...]
        cv = jnp.where(row == rows - 1, carry_a[0:1, :], pltpu.roll(av, rows - 1, axis=0))
        s = 1
        while s < rows:
            c_next = pltpu.roll(cv, rows - s, axis=0)
            b_next = pltpu.roll(bv, rows - s, axis=0)
            m = row < rows - s
            bv = jnp.where(m, cv * b_next + bv, bv)
            cv = jnp.where(m, cv * c_next, cv)
            s *= 2
        dh = cv * carry_dh[0:1, :] + bv
        h_before = jnp.where(tile_id > 0, hp_ref[SUBLANE - 1:SUBLANE, :], jnp.zeros((1, cols), f32))
        h_prev = jnp.where(row == 0, h_before, pltpu.roll(hv, 1, axis=0))
        da_ref[...] = dh * h_prev
        db_ref[...] = dh
        carry_dh[0:1, :] = dh[0:1, :]
        carry_a[0:1, :] = av[0:1, :]

    return pl.pallas_call(
        body, name="lru_scan_bwd",
        out_shape=[jax.ShapeDtypeStruct((t_len, cols), f32)] * 3,
        grid=(n_tiles,), in_specs=[spec, spec, spec, spec, prev_spec], out_specs=[spec, spec, spec],
        scratch_shapes=[pltpu.VMEM((SUBLANE, cols), f32), pltpu.VMEM((SUBLANE, cols), f32)],
        compiler_params=_cparams(("arbitrary",)),
    )(dy, gg, a, h, h)


def _loss_head(h, target, g):
    t_len = h.shape[0]
    rows = min(512, t_len)

    def f(hv, gv, tv):
        err = _rms(hv, gv) - tv
        return 0.5 * jnp.sum(jnp.mean(err * err, axis=-1, keepdims=True), axis=0, keepdims=True)

    def body(h_ref, t_ref, g_ref, dh_ref, dg_ref, loss_ref):
        i = pl.program_id(0)

        @pl.when(i == 0)
        def _():
            dg_ref[...] = jnp.zeros_like(dg_ref)
            loss_ref[...] = jnp.zeros_like(loss_ref)

        tv = t_ref[...]
        part, vjp = jax.vjp(lambda hv, gv: f(hv, gv, tv), h_ref[...], g_ref[...])
        dh, dg = vjp(jnp.ones((1, 1), f32))
        dh_ref[...] = dh
        dg_ref[...] += dg
        loss_ref[...] += jnp.broadcast_to(part, loss_ref.shape)

    spec = pl.BlockSpec((rows, D_MODEL), lambda i: (i, 0))
    return pl.pallas_call(
        body, name="loss_head",
        out_shape=[jax.ShapeDtypeStruct((t_len, D_MODEL), f32), jax.ShapeDtypeStruct((1, D_MODEL), f32),
                   jax.ShapeDtypeStruct((1, LANE), f32)],
        grid=(t_len // rows,), in_specs=[spec, spec, pl.BlockSpec((1, D_MODEL), lambda i: (0, 0))],
        out_specs=[spec, pl.BlockSpec((1, D_MODEL), lambda i: (0, 0)), pl.BlockSpec((1, LANE), lambda i: (0, 0))],
        compiler_params=_cparams(("arbitrary",)),
    )(h, target, g)


def _as2d(a):
    return a.reshape((-1, a.shape[-1])) if a.ndim > 1 else a.reshape((1, -1))


def _row_block(rows, cols, bytes_cap=1 << 20):
    if rows * cols * 4 <= bytes_cap or rows % SUBLANE:
        return rows
    return _tile(rows, max(SUBLANE, (bytes_cap // (cols * 4)) // SUBLANE * SUBLANE), SUBLANE)


def _adamw(w, g, m, v, name):
    shape = w.shape
    w2, g2, m2, v2 = _as2d(w), _as2d(g), _as2d(m), _as2d(v)
    rows, cols = w2.shape
    rb = _row_block(rows, cols)

    def body(w_ref, g_ref, m_ref, v_ref, d_ref, nm_ref, nv_ref):
        gv = g_ref[...]
        nm = ADAM_B1 * m_ref[...] + (1.0 - ADAM_B1) * gv
        nv = ADAM_B2 * v_ref[...] + (1.0 - ADAM_B2) * jnp.square(gv)
        m_hat = nm / (1.0 - ADAM_B1 ** ADAM_STEP)
        v_hat = nv / (1.0 - ADAM_B2 ** ADAM_STEP)
        d_ref[...] = -ADAM_LR * (m_hat / (jnp.sqrt(v_hat) + ADAM_EPS) + ADAM_WD * w_ref[...])
        nm_ref[...] = nm
        nv_ref[...] = nv

    spec = pl.BlockSpec((rb, cols), lambda i: (i, 0))
    d, nm, nv = pl.pallas_call(
        body, name=name, out_shape=[jax.ShapeDtypeStruct((rows, cols), f32)] * 3,
        grid=(rows // rb,), in_specs=[spec] * 4, out_specs=[spec] * 3,
        compiler_params=_cparams(("parallel",)),
    )(w2, g2, m2, v2)
    return d.reshape(shape), nm.reshape(shape), nv.reshape(shape)


def _sum_with_sibling(g_halves, theirs, c_idx):
    n_sh, _, rows, cols = g_halves.shape
    rb = _tile(rows, 512, 2 * SUBLANE)

    def body(c_ref, mine_ref, theirs_ref, o_ref):
        o_ref[...] = (mine_ref[...] + theirs_ref[...]).astype(bf16)

    grid_spec = pltpu.PrefetchScalarGridSpec(
        num_scalar_prefetch=1, grid=(n_sh, rows // rb),
        in_specs=[pl.BlockSpec((None, None, rb, cols), lambda k, i, c_ref: (k, c_ref[0], i, 0)),
                  pl.BlockSpec((None, rb, cols), lambda k, i, c_ref: (k, i, 0))],
        out_specs=pl.BlockSpec((None, rb, cols), lambda k, i, c_ref: (k, i, 0)))
    return pl.pallas_call(
        body, name="grad_sum_sibling", out_shape=jax.ShapeDtypeStruct((n_sh, rows, cols), bf16),
        grid_spec=grid_spec, compiler_params=_cparams(("parallel", "parallel")),
    )(c_idx, g_halves, theirs)


def _sum_chips(partial, received, k_idx):
    _, rows, cols = partial.shape
    rb = _tile(rows, 512, 2 * SUBLANE)

    def body(k_ref, mine_ref, r_ref, o_ref):
        acc = mine_ref[...].astype(f32)
        for j in range(N_CHIPS - 1):
            acc = acc + r_ref[j].astype(f32)
        o_ref[...] = acc

    grid_spec = pltpu.PrefetchScalarGridSpec(
        num_scalar_prefetch=1, grid=(rows // rb,),
        in_specs=[pl.BlockSpec((None, rb, cols), lambda i, k_ref: (k_ref[0], i, 0)),
                  pl.BlockSpec((N_CHIPS - 1, rb, cols), lambda i, k_ref: (0, i, 0))],
        out_specs=pl.BlockSpec((rb, cols), lambda i, k_ref: (i, 0)))
    return pl.pallas_call(
        body, name="grad_sum_chips", out_shape=jax.ShapeDtypeStruct((rows, cols), f32),
        grid_spec=grid_spec, compiler_params=_cparams(("parallel",)),
    )(k_idx, partial, received)


HBM_SPEC = pl.BlockSpec(memory_space=pltpu.HBM)
CHIP_FLIPS = ((0, 1), (1, 0), (1, 1))


def _position():
    return lax.axis_index("x"), lax.axis_index("y"), lax.axis_index("c")


def _own_slot(gathered, mine, index):
    return [lax.dynamic_update_index_in_dim(g, m, index, 0) for g, m in zip(gathered, mine)]


def _gather_weights(blocks):
    n = len(blocks)

    def body(*refs):
        srcs, outs = refs[:n], refs[n:2 * n]
        send_sems, recv_sems = refs[2 * n:]
        _gather_start(srcs, outs, send_sems, recv_sems)
        _gather_finish(srcs, outs, send_sems, recv_sems)

    return pl.pallas_call(
        body, name="gather_weights", out_shape=_gather_out_shapes(blocks),
        in_specs=[HBM_SPEC] * n, out_specs=[HBM_SPEC] * n, scratch_shapes=_gather_semaphores(n),
    )(*blocks)


def _gather_out_shapes(blocks):
    return [jax.ShapeDtypeStruct((N_CHIPS,) + b.shape, b.dtype) for b in blocks]


def _gather_semaphores(n):
    n_sem = 2 * len(CHIP_FLIPS) * n
    return [pltpu.SemaphoreType.DMA((n_sem,)), pltpu.SemaphoreType.DMA((n_sem,))]


def _gather_copies(srcs, outs, send_sems, recv_sems):
    n_far = len(CHIP_FLIPS)
    x, y, c = _position()
    k = 2 * x + y
    first, passed = [], []
    for a in range(len(srcs)):
        for j, (fx, fy) in enumerate(CHIP_FLIPS):
            s = a * 2 * n_far + j
            kk = 2 * (x ^ fx) + (y ^ fy)
            first.append(pltpu.make_async_remote_copy(
                src_ref=srcs[a].at[c], dst_ref=outs[a].at[k, c], send_sem=send_sems.at[s],
                recv_sem=recv_sems.at[s], device_id=(x ^ fx, y ^ fy, c), device_id_type=MESH))
            passed.append(pltpu.make_async_remote_copy(
                src_ref=outs[a].at[kk, c], dst_ref=outs[a].at[kk, c], send_sem=send_sems.at[s + n_far],
                recv_sem=recv_sems.at[s + n_far], device_id=(x, y, 1 - c), device_id_type=MESH))
    return first, passed


def _gather_start(srcs, outs, send_sems, recv_sems):
    first, _ = _gather_copies(srcs, outs, send_sems, recv_sems)
    for cp in first:
        cp.start()


def _gather_finish(srcs, outs, send_sems, recv_sems):
    first, passed = _gather_copies(srcs, outs, send_sems, recv_sems)
    for arrived, onward in zip(first, passed):
        arrived.wait_recv()
        onward.start()
    for cp in passed:
        cp.wait_recv()
    for cp in first + passed:
        cp.wait_send()


def _swap_with_sibling(grads):
    n = len(grads)

    def body(*refs):
        srcs, outs = refs[:n], refs[n:2 * n]
        send_sems, recv_sems = refs[2 * n:]
        x, y, c = _position()
        copies = []
        for a in range(n):
            for kk in range(N_CHIPS):
                s = a * N_CHIPS + kk
                cp = pltpu.make_async_remote_copy(
                    src_ref=srcs[a].at[kk, 1 - c], dst_ref=outs[a].at[kk], send_sem=send_sems.at[s],
                    recv_sem=recv_sems.at[s], device_id=(x, y, 1 - c), device_id_type=MESH)
                cp.start()
                copies.append(cp)
        for cp in copies:
            cp.wait()

    return pl.pallas_call(
        body, name="grad_swap_sibling",
        out_shape=[jax.ShapeDtypeStruct((N_CHIPS,) + g.shape[2:], g.dtype) for g in grads],
        in_specs=[HBM_SPEC] * n, out_specs=[HBM_SPEC] * n,
        scratch_shapes=[pltpu.SemaphoreType.DMA((N_CHIPS * n,)), pltpu.SemaphoreType.DMA((N_CHIPS * n,))],
    )(*grads)


def _send_to_chips(partials):
    n = len(partials)

    def body(*refs):
        srcs, outs = refs[:n], refs[n:2 * n]
        send_sems, recv_sems = refs[2 * n:]
        for cp in _to_chips_copies(srcs, outs, send_sems, recv_sems):
            cp.start()
        for cp in _to_chips_copies(srcs, outs, send_sems, recv_sems):
            cp.wait()

    return pl.pallas_call(
        body, name="grad_to_chips", out_shape=_to_chips_out_shapes(partials),
        in_specs=[HBM_SPEC] * n, out_specs=[HBM_SPEC] * n, scratch_shapes=_to_chips_semaphores(n),
    )(*partials)


def _to_chips_out_shapes(partials):
    return [jax.ShapeDtypeStruct((len(CHIP_FLIPS),) + p.shape[1:], p.dtype) for p in partials]


def _to_chips_semaphores(n):
    n_sem = len(CHIP_FLIPS) * n
    return [pltpu.SemaphoreType.DMA((n_sem,)), pltpu.SemaphoreType.DMA((n_sem,))]


def _to_chips_copies(srcs, outs, send_sems, recv_sems):
    n_far = len(CHIP_FLIPS)
    x, y, c = _position()
    copies = []
    for a in range(len(srcs)):
        for j, (fx, fy) in enumerate(CHIP_FLIPS):
            s = a * n_far + j
            kk = 2 * (x ^ fx) + (y ^ fy)
            copies.append(pltpu.make_async_remote_copy(
                src_ref=srcs[a].at[kk], dst_ref=outs[a].at[j], send_sem=send_sems.at[s],
                recv_sem=recv_sems.at[s], device_id=(x ^ fx, y ^ fy, c), device_id_type=MESH))
    return copies


def _join_halves(halves):
    n = len(halves)

    def body(*refs):
        srcs, outs = refs[:n], refs[n:2 * n]
        send_sems, recv_sems = refs[2 * n:]
        x, y, c = _position()
        copies = []
        for a in range(n):
            cp = pltpu.make_async_remote_copy(
                src_ref=srcs[a], dst_ref=outs[a].at[c], send_sem=send_sems.at[a], recv_sem=recv_sems.at[a],
                device_id=(x, y, 1 - c), device_id_type=MESH)
            cp.start()
            copies.append(cp)
        for cp in copies:
            cp.wait()

    return pl.pallas_call(
        body, name="grad_join_halves",
        out_shape=[jax.ShapeDtypeStruct((2,) + h.shape, h.dtype) for h in halves],
        in_specs=[HBM_SPEC] * n, out_specs=[HBM_SPEC] * n,
        scratch_shapes=[pltpu.SemaphoreType.DMA((n,)), pltpu.SemaphoreType.DMA((n,))],
    )(*halves)


def _all_sum_small(vec):
    rows, cols = vec.shape

    def body(v_ref, o_ref, buf, send_sems, recv_sems):
        x, y, c = _position()
        me = 4 * x + 2 * y + c
        buf[me] = v_ref[...]
        copies = []
        for m in range(1, N_DEV):
            fx, fy, fc = (m >> 2) & 1, (m >> 1) & 1, m & 1
            cp = pltpu.make_async_remote_copy(
                src_ref=v_ref, dst_ref=buf.at[me], send_sem=send_sems.at[m - 1], recv_sem=recv_sems.at[m - 1],
                device_id=(x ^ fx, y ^ fy, c ^ fc), device_id_type=MESH)
            cp.start()
            copies.append(cp)
        for cp in copies:
            cp.wait()
        acc = buf[0]
        for d in range(1, N_DEV):
            acc = acc + buf[d]
        o_ref[...] = acc

    return pl.pallas_call(
        body, name="all_sum_small", out_shape=jax.ShapeDtypeStruct((rows, cols), f32),
        in_specs=[pl.BlockSpec(memory_space=pltpu.VMEM)], out_specs=pl.BlockSpec(memory_space=pltpu.VMEM),
        scratch_shapes=[pltpu.VMEM((N_DEV, rows, cols), f32), pltpu.SemaphoreType.DMA((N_DEV - 1,)),
                        pltpu.SemaphoreType.DMA((N_DEV - 1,))],
        compiler_params=_cparams(),
    )(vec)


FLAT_QUANTUM = 2 * 2 * SUBLANE * FLAT_COLS


def _pack(arrays, dtype):
    flat = jnp.concatenate([a.astype(dtype).reshape(-1) for a in arrays])
    n = flat.shape[0]
    n_pad = -(-n // FLAT_QUANTUM) * FLAT_QUANTUM
    return jnp.pad(flat, (0, n_pad - n))


def _unpack(flat, shapes):
    out, off = [], 0
    for s in shapes:
        n = int(np.prod(s))
        out.append(flat[..., off:off + n].reshape(flat.shape[:-1] + tuple(s)))
        off += n
    return out


def _full_from_shards(stacked, axis):
    return jnp.concatenate([stacked[k] for k in range(N_CHIPS)], axis=axis)


def _shards_of(full, axis):
    return jnp.stack(jnp.split(full, N_CHIPS, axis=axis))


def _ffn_fwd(h, p):
    u = _row_fwd(_f_rms, [h], [p['g']], [(D_MODEL, bf16)], rows=512, name="ffn_norm")[0]
    a = _mm_w(u, p['up'], 'nn', "ffn_up")
    gated = _conv_fwd(_ffn_conv_plan(a, p), [(FFN_H, bf16, (0,))], "ffn_gate")[0]
    h_out = _mm(gated, p['down'], 'nn', "ffn_down", add=h)
    return h_out, (h, u, a, gated)


def _ffn_conv_plan(a, p):
    both = (0, FFN_H)
    return _ConvPlan(a, p['cw'], p['cb'], in_bases=both, mid_bases=both, width=FFN_H, rows=256, rs=FFN_STRIP_ROWS,
                     post=_post_ffn_gate)


def _ffn_bwd(dh_out, p, saved, bias_zero):
    h, u, a, gated = saved
    d_gated = _mm(dh_out, p['down'], 'nt', "ffn_down_dx", out_dtype=bf16)
    d_down = _mm(gated, dh_out, 'tn', "ffn_down_dw")
    da, d_cw, d_cb = _conv_bwd(_ffn_conv_plan(a, p), [(d_gated, (0,))], bf16, "ffn_gate_bwd")
    d_up = _mm(u, da, 'tn', "ffn_up_dw", out_cols_sharded=True)
    du = _mm_w(da, p['up'], 'nt', "ffn_up_dx", out_dtype=bf16)
    dh, d_g, d_bias = _row_bwd(_f_rms_res, [h], [p['g'], bias_zero], [du, dh_out], rows=512, name="ffn_norm_bwd")
    return dh, {'g': d_g, 'up': d_up, 'down': d_down, 'cw': d_cw, 'cb': d_cb}, d_bias


def _mixer_norm_bwd(h, g, du, dh_res, name):
    def f(hv, gv):
        return _rms(hv, gv), hv

    dh, d_g = _row_bwd(f, [h], [g], [du, dh_res], rows=512, name=name)
    return dh, d_g


def _ssd_layer_fwd(h, p, gather=()):
    u = _row_fwd(_f_rms, [h], [p['g']], [(D_MODEL, bf16)], rows=512, name="ssd_norm")[0]
    z = _mm(u, p['w_z'], 'nn', "ssd_in_z")
    xbc = _mm(u, p['w_xbc'], 'nn', "ssd_in_xbc")
    dtr = _mm(u, p['w_dt'], 'nn', "ssd_in_dt")
    xc = _conv_fwd(_ssd_conv_plan(xbc, p), [(SSD_CONV_DIM, f32, (0,))], "ssd_conv")[0]
    dt = _row_fwd(_f_ssd_dt, [dtr], [p['dtb']], [(LANE, f32)], rows=1024, name="ssd_dt")[0]
    y, states, gathered = _ssd_fwd(xc, dt, p['a_log'], p['dsk'], gather)
    yn = _row_fwd(_f_ssd_post, [y, z], [p['norm']], [(SSD_D_INNER, bf16)], rows=256, name="ssd_gate_norm")[0]
    h_out = _mm(yn, p['out'], 'nn', "ssd_out", add=h)
    return h_out, (h, u, z, xbc, dtr, xc, dt, states, y, yn), gathered


def _ssd_conv_plan(xbc, p):
    return _ConvPlan(xbc, p['cw'], p['cb'], in_bases=(0,), mid_bases=(0,), width=SSD_CONV_DIM, rows=256,
                     rs=SSD_STRIP_ROWS, post=_post_silu)


def _ssd_layer_bwd(dh_out, p, saved, to_chips=()):
    h, u, z, xbc, dtr, xc, dt, states, y, yn = saved
    d_yn = _mm(dh_out, p['out'], 'nt', "ssd_out_dx", out_dtype=bf16)
    d_out = _mm(yn, dh_out, 'tn', "ssd_out_dw")
    dy, dz, d_norm = _row_bwd(_f_ssd_post, [y, z], [p['norm']], [d_yn], rows=256, name="ssd_gate_norm_bwd",
                              tile_dtypes=[f32, bf16])
    dxc, ddt, d_alog, d_dsk, received = _ssd_bwd(xc, dt, p['a_log'], p['dsk'], states, dy, to_chips)
    dxbc, d_cw, d_cb = _conv_bwd(_ssd_conv_plan(xbc, p), [(dxc, (0,))], bf16, "ssd_conv_bwd")
    ddtr, d_dtb = _row_bwd(_f_ssd_dt, [dtr], [p['dtb']], [ddt], rows=1024, name="ssd_dt_bwd", tile_dtypes=[bf16])
    d_wz = _mm(u, dz, 'tn', "ssd_in_z_dw")
    d_wxbc = _mm(u, dxbc, 'tn', "ssd_in_xbc_dw")
    d_wdt = _mm(u, ddtr, 'tn', "ssd_in_dt_dw")
    du = _mm(dz, p['w_z'], 'nt', "ssd_in_z_dx")
    du = _mm(dxbc, p['w_xbc'], 'nt', "ssd_in_xbc_dx", add=du)
    du = _mm(ddtr, p['w_dt'], 'nt', "ssd_in_dt_dx", add=du, out_dtype=bf16)
    dh, d_g = _mixer_norm_bwd(h, p['g'], du, dh_out, "ssd_norm_bwd")
    grads = {'g': d_g, 'w_z': d_wz, 'w_xbc': d_wxbc, 'w_dt': d_wdt, 'cw': d_cw, 'cb': d_cb, 'dtb': d_dtb,
             'a_log': d_alog, 'dsk': d_dsk, 'norm': d_norm, 'out': d_out}
    return dh, grads, received


def _conf_layer_fwd(h, p):
    u = _row_fwd(_f_rms, [h], [p['g']], [(D_MODEL, bf16)], rows=512, name="conf_norm")[0]
    g2 = _mm_w(u, p['pw1'], 'nn', "conf_pw1")
    conv = _conv_fwd(_conf_conv_plan(g2, p), [(D_MODEL, f32, (0,))], "conf_conv")[0]
    s = _row_fwd(_f_ln_silu, [conv], [p['ln_g'], p['ln_b']], [(D_MODEL, bf16)], rows=256, name="conf_ln")[0]
    h_out = _mm(s, p['pw2'], 'nn', "conf_pw2", bias=p['b2'], add=h)
    return h_out, (h, u, g2, conv, s)


def _conf_conv_plan(g2, p):
    halves = (0, D_MODEL)
    return _ConvPlan(g2, p['dw_w'], p['dw_b'], in_bases=halves, mid_bases=(0,), width=D_MODEL, rows=256,
                     rs=CONF_STRIP_ROWS, pre=_pre_glu, pre_params=[(p['b1'], halves)], post=_post_identity)


def _conf_layer_bwd(dh_out, p, saved):
    h, u, g2, conv, s = saved
    ds = _mm(dh_out, p['pw2'], 'nt', "conf_pw2_dx", out_dtype=bf16)
    d_pw2 = _mm(s, dh_out, 'tn', "conf_pw2_dw")
    d_conv, d_lng, d_lnb = _row_bwd(_f_ln_silu, [conv], [p['ln_g'], p['ln_b']], [ds], rows=256, name="conf_ln_bwd")
    dg2, d_dww, d_dwb, d_b1 = _conv_bwd(_conf_conv_plan(g2, p), [(d_conv, (0,))], bf16, "conf_conv_bwd")
    d_pw1 = _mm(u, dg2, 'tn', "conf_pw1_dw", out_cols_sharded=True)
    du = _mm_w(dg2, p['pw1'], 'nt', "conf_pw1_dx", out_dtype=bf16)
    dh, d_g = _mixer_norm_bwd(h, p['g'], du, dh_out, "conf_norm_bwd")
    grads = {'g': d_g, 'pw1': d_pw1, 'b1': d_b1, 'dw_w': d_dww, 'dw_b': d_dwb, 'ln_g': d_lng, 'ln_b': d_lnb,
             'pw2': d_pw2}
    return dh, grads


def _lru_params(p):
    return [p['in_b'], p['cw'], p['cb'], p['ga_w'], p['ga_b'], p['gx_w'], p['gx_b'], p['lam']]


def _lru_layer_fwd(h, p):
    u = _row_fwd(_f_rms, [h], [p['g']], [(D_MODEL, bf16)], rows=512, name="lru_norm")[0]
    io = _mm_w(u, p['in_w'], 'nn', "lru_in")
    a, b, gg = _row_fwd(_f_lru, [io], _lru_params(p), [(LRU_W, f32)] * 3, rows=256, name="lru_gates",
                        halo=SUBLANE, halo_of=[True])
    y, hs = _lru_fwd(a, b, gg)
    h_out = _mm(y, p['out'], 'nn', "lru_out", bias=p['out_b'], add=h)
    return h_out, (h, u, io, a, gg, hs, y)


def _lru_layer_bwd(dh_out, p, saved):
    h, u, io, a, gg, hs, y = saved
    dy = _mm(dh_out, p['out'], 'nt', "lru_out_dx")
    d_out = _mm(y, dh_out, 'tn', "lru_out_dw")
    da, db, dgg = _lru_bwd(dy, gg, a, hs)
    res = _row_bwd(_f_lru, [io], _lru_params(p), [da, db, dgg], rows=256, name="lru_gates_bwd",
                   halo=SUBLANE, halo_of=[True], tile_dtypes=[bf16])
    dio, d_inb, d_cw, d_cb, d_gaw, d_gab, d_gxw, d_gxb, d_lam = res
    d_inw = _mm(u, dio, 'tn', "lru_in_dw", out_cols_sharded=True)
    du = _mm_w(dio, p['in_w'], 'nt', "lru_in_dx", out_dtype=bf16)
    dh, d_g = _mixer_norm_bwd(h, p['g'], du, dh_out, "lru_norm_bwd")
    grads = {'g': d_g, 'in_w': d_inw, 'in_b': d_inb, 'cw': d_cw, 'cb': d_cb, 'ga_w': d_gaw, 'ga_b': d_gab,
             'gx_w': d_gxw, 'gx_b': d_gxb, 'lam': d_lam, 'out': d_out}
    return dh, grads


def _sgu_params(p):
    return [p['in_b'], p['ln_g'], p['ln_b'], p['sp_w'], p['sp_bt']]


def _sgu_layer_fwd(h, p):
    u = _row_fwd(_f_rms, [h], [p['g']], [(D_MODEL, bf16)], rows=512, name="sgu_norm")[0]
    z = _mm_w(u, p['in_w'], 'nn', "sgu_in")
    s = _row_fwd(_f_sgu, [z], _sgu_params(p), [(SGU_HALF, bf16)], rows=SGU_CHUNK, name="sgu_mix")[0]
    h_out = _mm(s, p['out'], 'nn', "sgu_out", bias=p['out_b'], add=h)
    return h_out, (h, u, z, s)


def _sgu_layer_bwd(dh_out, p, saved):
    h, u, z, s = saved
    ds = _mm(dh_out, p['out'], 'nt', "sgu_out_dx", out_dtype=bf16)
    d_out = _mm(s, dh_out, 'tn', "sgu_out_dw")
    dz, d_inb, d_lng, d_lnb, d_spw, d_spbt = _row_bwd(_f_sgu, [z], _sgu_params(p), [ds], rows=SGU_CHUNK,
                                                      name="sgu_mix_bwd", tile_dtypes=[bf16])
    d_inw = _mm(u, dz, 'tn', "sgu_in_dw", out_cols_sharded=True)
    du = _mm_w(dz, p['in_w'], 'nt', "sgu_in_dx", out_dtype=bf16)
    dh, d_g = _mixer_norm_bwd(h, p['g'], du, dh_out, "sgu_norm_bwd")
    grads = {'g': d_g, 'in_w': d_inw, 'in_b': d_inb, 'ln_g': d_lng, 'ln_b': d_lnb, 'sp_w': d_spw, 'sp_bt': d_spbt,
             'out': d_out}
    return dh, grads


def _row(v):
    return v.reshape((1, -1)).astype(f32)


def _pad_lanes(v, n=LANE):
    v = _row(v)
    return jnp.pad(v, ((0, 0), (0, n - v.shape[1])))


def _local_step(x, target, w, comm=None):
    a_in = w['a_in_proj'][0]
    pa = {'g': _row(w['norm_mix'][0]), 'w_z': a_in[:, :SSD_D_INNER],
          'w_xbc': a_in[:, SSD_D_INNER:SSD_D_INNER + SSD_CONV_DIM],
          'w_dt': jnp.pad(a_in[:, SSD_D_INNER + SSD_CONV_DIM:], ((0, 0), (0, LANE - SSD_HEADS))),
          'cw': w['a_conv_w'][0].astype(f32), 'cb': _row(w['a_conv_b'][0]), 'dtb': _pad_lanes(w['a_dt_bias'][0]),
          'a_log': _pad_lanes(w['a_log'][0]), 'dsk': _pad_lanes(w['a_d_skip'][0]), 'norm': _row(w['a_norm'][0]),
          'out': w['a_out_proj']}
    h, s_mix0, gathered = _ssd_layer_fwd(x, pa, gather=comm.late_blocks if comm else ())
    if comm:
        w = {**w, **comm.late_weights(gathered)}
    ffn = [{'g': _row(w['norm_ffn'][i]), 'up': (w['f_up_w'], i), 'down': w['f_down_w'][i],
            'cw': w['f_conv_w'][i].astype(f32), 'cb': _row(w['f_conv_b'][i])} for i in range(DEPTH)]
    pb = {'g': _row(w['norm_mix'][1]), 'pw1': (w['b_pw1_w'], 0), 'b1': _row(w['b_pw1_b'][0]),
          'dw_w': w['b_dw_w'][0].astype(f32), 'dw_b': _row(w['b_dw_b'][0]), 'ln_g': _row(w['b_ln_g'][0]),
          'ln_b': _row(w['b_ln_b'][0]), 'pw2': w['b_pw2_w'], 'b2': _row(w['b_pw2_b'][0])}
    pc = {'g': _row(w['norm_mix'][2]), 'in_w': (w['c_in_w'], 0), 'in_b': _row(w['c_in_b'][0]),
          'cw': w['c_conv_w'][0].astype(f32), 'cb': _row(w['c_conv_b'][0]),
          'ga_w': w['c_ga_w'][0].reshape(LRU_W, LRU_BLOCK).astype(f32), 'ga_b': _row(w['c_ga_b'][0]),
          'gx_w': w['c_gx_w'][0].reshape(LRU_W, LRU_BLOCK).astype(f32), 'gx_b': _row(w['c_gx_b'][0]),
          'lam': _row(w['c_lambda'][0]), 'out': w['c_out_w'], 'out_b': _row(w['c_out_b'][0])}
    pd = {'g': _row(w['norm_mix'][3]), 'in_w': (w['d_in_w'], 0), 'in_b': _row(w['d_in_b'][0]),
          'ln_g': _row(w['d_ln_g'][0]), 'ln_b': _row(w['d_ln_b'][0]),
          'sp_w': w['d_sp_w'][0].reshape(SGU_GROUPS * SGU_CHUNK, SGU_CHUNK).astype(f32),
          'sp_bt': w['d_sp_b'][0].astype(f32).T, 'out': w['d_out_w'], 'out_b': _row(w['d_out_b'][0])}
    mixers = [(None, None, pa), (_conf_layer_fwd, _conf_layer_bwd, pb),
              (_lru_layer_fwd, _lru_layer_bwd, pc), (_sgu_layer_fwd, _sgu_layer_bwd, pd)]

    saved = []
    for i in range(DEPTH):
        fwd, _, p = mixers[i]
        if i > 0:
            h, s_mix = fwd(h, p)
        else:
            s_mix = s_mix0
        h, s_ffn = _ffn_fwd(h, ffn[i])
        saved.append((s_mix, s_ffn))
    dh, d_final, loss = _loss_head(h, target, _row(w['norm_final']))

    def rows_sharded(g):
        return g.reshape(N_CHIPS, g.shape[0] // N_CHIPS, g.shape[1])

    bias_zero = jnp.zeros((1, D_MODEL), f32)
    g_ffn, g_mix, d_out_bias = [None] * DEPTH, [None] * DEPTH, [None] * DEPTH
    for i in reversed(range(1, DEPTH)):
        _, bwd, p = mixers[i]
        dh, g_ffn[i], d_out_bias[i] = _ffn_bwd(dh, ffn[i], saved[i][1], bias_zero)
        dh, g_mix[i] = bwd(dh, p, saved[i][0])
    dh, g_ffn[0], d_out_bias[0] = _ffn_bwd(dh, ffn[0], saved[0][1], bias_zero)
    _, gb, gc, gd = g_mix
    late_direct = {
        'b_pw1_w': gb['pw1'], 'b_pw2_w': rows_sharded(gb['pw2']), 'c_in_w': gc['in_w'],
        'c_out_w': rows_sharded(gc['out']), 'd_in_w': gd['in_w'], 'd_out_w': rows_sharded(gd['out']),
        'f_up_w': [g['up'] for g in g_ffn], 'f_down_w': [rows_sharded(g['down']) for g in g_ffn]}
    partials = comm.early_partials(late_direct) if comm else []
    dh, g_mix[0], received = _ssd_layer_bwd(dh, pa, saved[0][0], to_chips=partials)
    ga = g_mix[0]

    grads = {**late_direct,
        'norm_mix': jnp.concatenate([g['g'] for g in g_mix], axis=0),
        'norm_ffn': jnp.concatenate([g['g'] for g in g_ffn], axis=0),
        'norm_final': d_final.reshape(-1),
        'a_in_proj': jnp.concatenate([ga['w_z'], ga['w_xbc'], ga['w_dt'][:, :SSD_HEADS]], axis=1)[None],
        'a_conv_w': ga['cw'][None], 'a_conv_b': ga['cb'], 'a_dt_bias': ga['dtb'][:, :SSD_HEADS],
        'a_log': ga['a_log'][:, :SSD_HEADS], 'a_d_skip': ga['dsk'][:, :SSD_HEADS], 'a_norm': ga['norm'],
        'a_out_proj': rows_sharded(ga['out']),
        'b_pw1_b': gb['b1'], 'b_dw_w': gb['dw_w'][None], 'b_dw_b': gb['dw_b'],
        'b_ln_g': gb['ln_g'], 'b_ln_b': gb['ln_b'], 'b_pw2_b': d_out_bias[1],
        'c_in_b': gc['in_b'], 'c_conv_w': gc['cw'][None], 'c_conv_b': gc['cb'],
        'c_ga_w': gc['ga_w'].reshape(1, LRU_W // LRU_BLOCK, LRU_BLOCK, LRU_BLOCK),
        'c_ga_b': gc['ga_b'].reshape(1, LRU_W // LRU_BLOCK, LRU_BLOCK),
        'c_gx_w': gc['gx_w'].reshape(1, LRU_W // LRU_BLOCK, LRU_BLOCK, LRU_BLOCK),
        'c_gx_b': gc['gx_b'].reshape(1, LRU_W // LRU_BLOCK, LRU_BLOCK),
        'c_lambda': gc['lam'], 'c_out_b': d_out_bias[2],
        'd_in_b': gd['in_b'], 'd_ln_g': gd['ln_g'], 'd_ln_b': gd['ln_b'],
        'd_sp_w': gd['sp_w'].reshape(1, SGU_GROUPS, SGU_CHUNK, SGU_CHUNK), 'd_sp_b': gd['sp_bt'].T[None],
        'd_out_b': d_out_bias[3],
        'f_conv_w': jnp.stack([g['cw'] for g in g_ffn]),
        'f_conv_b': jnp.concatenate([g['cb'] for g in g_ffn], axis=0),
    }
    return loss, dh, grads, (partials, received)


def _global_shape(name, shard_shape):
    ax = SHARD_AXIS[name]
    if ax is None:
        return tuple(shard_shape)
    s = list(shard_shape)
    s[ax] *= N_CHIPS
    return tuple(s)


def _step(x, target, weights, moments_m, moments_v):
    x2, t2 = x[0], target[0]
    shard_shapes = {n: weights[n].shape for n in WEIGHTS}
    c_pos = lax.axis_index("c")
    k_pos = 2 * lax.axis_index("x") + lax.axis_index("y")
    c_idx = c_pos.astype(jnp.int32).reshape(1)
    k_idx = k_pos.astype(jnp.int32).reshape(1)

    def halves_of(a):
        a2 = _as2d(a)
        return a2.reshape(2, a2.shape[0] // 2, a2.shape[1])

    def view_direct(n, g):
        g = g.reshape((N_CHIPS,) + shard_shapes[n])
        if n in DIRECT_COLS:
            return g
        if n == 'f_down_w':
            return [g[:, i].reshape(-1, g.shape[-1]) for i in range(DEPTH)]
        return g.reshape(-1, g.shape[-1])

    def sibling_sums(grads):
        mine_g = [g.reshape(N_CHIPS, 2, g.shape[1] // 2, g.shape[2]) for g in grads]
        theirs = _swap_with_sibling(mine_g)
        return [_sum_with_sibling(g, t, c_idx) for g, t in zip(mine_g, theirs)]

    def flatten_direct(grads, names):
        out = []
        for n in names:
            out += grads[n] if isinstance(grads[n], list) else [grads[n]]
        return out

    first = [halves_of(weights[n].astype(bf16)) for n in EARLY_DIRECT]
    first.append(_pack([weights[n] for n in PACKED_MM], bf16).reshape(2, -1, FLAT_COLS))
    first.append(_pack([weights[n] for n in SHARDED_VEC], f32).reshape(2, -1, FLAT_COLS))
    gathered = _own_slot(_gather_weights(first), first, k_pos)
    w = {n: weights[n] for n in REPLICATED}
    for n, g in zip(EARLY_DIRECT, gathered):
        w[n] = view_direct(n, g)
    all_mm = _unpack(gathered[-2].reshape(N_CHIPS, -1), [shard_shapes[n] for n in PACKED_MM])
    all_vec = _unpack(gathered[-1].reshape(N_CHIPS, -1), [shard_shapes[n] for n in SHARDED_VEC])
    for n, st in zip(PACKED_MM + SHARDED_VEC, all_mm + all_vec):
        w[n] = _full_from_shards(st, SHARD_AXIS[n])

    class Comm:
        late_blocks = [halves_of(weights[n].astype(bf16)) for n in LATE_DIRECT]

        @staticmethod
        def late_weights(arrived):
            arrived = _own_slot(arrived, Comm.late_blocks, k_pos)
            return {n: view_direct(n, g) for n, g in zip(LATE_DIRECT, arrived)}

        @staticmethod
        def early_partials(grads):
            return sibling_sums(flatten_direct(grads, LATE_DIRECT))

    loss_part, dx, grads, (partials, received) = _local_step(x2, t2, w, Comm)

    packed = [_shards_of(grads[n].reshape(_global_shape(n, shard_shapes[n])), SHARD_AXIS[n]).reshape(N_CHIPS, -1)
              for n in PACKED_MM + SHARDED_VEC]
    flat = jnp.concatenate(packed, axis=1)
    n_flat = flat.shape[1]
    n_pad = -(-n_flat // FLAT_QUANTUM) * FLAT_QUANTUM
    flat = jnp.pad(flat, ((0, 0), (0, n_pad - n_flat))).reshape(N_CHIPS, -1, FLAT_COLS)
    last_partials = sibling_sums(flatten_direct(grads, EARLY_DIRECT) + [flat])
    last_received = _send_to_chips(last_partials)
    partials, received = list(partials) + last_partials, list(received) + list(last_received)
    my_halves = [_sum_chips(p, r, k_idx) for p, r in zip(partials, received)]
    joined = _own_slot(_join_halves(my_halves), my_halves, c_pos)
    g_shard, pos = {}, 0
    for n in LATE_DIRECT + EARLY_DIRECT:
        layers = shard_shapes[n][0]
        g_shard[n] = jnp.stack([j.reshape(shard_shapes[n][1:]) for j in joined[pos:pos + layers]])
        pos += layers
    flat_shapes = [shard_shapes[n] for n in PACKED_MM + SHARDED_VEC]
    g_shard.update(zip(PACKED_MM + SHARDED_VEC, _unpack(joined[-1].reshape(-1), flat_shapes)))

    small = jnp.concatenate([grads[n].reshape(-1) for n in REPLICATED] + [loss_part.reshape(-1)[:1]])
    n_small = small.shape[0]
    n_small_pad = -(-n_small // (SUBLANE * FLAT_COLS)) * (SUBLANE * FLAT_COLS)
    small = jnp.pad(small, (0, n_small_pad - n_small)).reshape(-1, FLAT_COLS)
    small = _all_sum_small(small).reshape(-1)
    g_rep = dict(zip(REPLICATED, _unpack(small, [shard_shapes[n] for n in REPLICATED])))
    loss = small[n_small - 1]

    g_all = {**g_shard, **g_rep}
    delta, new_m, new_v = {}, {}, {}
    for n in WEIGHTS:
        delta[n], new_m[n], new_v[n] = _adamw(weights[n], g_all[n], moments_m[n], moments_v[n], "adamw_" + n)
    return loss, dx[None], g_all, delta, new_m, new_v


def kernel(x, norm_mix, norm_ffn, norm_final, a_in_proj, a_conv_w, a_conv_b, a_dt_bias, a_log, a_d_skip, a_norm, a_out_proj, b_pw1_w, b_pw1_b, b_dw_w, b_dw_b, b_ln_g, b_ln_b, b_pw2_w, b_pw2_b, c_in_w, c_in_b, c_conv_w, c_conv_b, c_ga_w, c_ga_b, c_gx_w, c_gx_b, c_lambda, c_out_w, c_out_b, d_in_w, d_in_b, d_ln_g, d_ln_b, d_sp_w, d_sp_b, d_out_w, d_out_b, f_up_w, f_conv_w, f_conv_b, f_down_w, loss_target, m_norm_mix, m_norm_ffn, m_norm_final, m_a_in_proj, m_a_conv_w, m_a_conv_b, m_a_dt_bias, m_a_log, m_a_d_skip, m_a_norm, m_a_out_proj, m_b_pw1_w, m_b_pw1_b, m_b_dw_w, m_b_dw_b, m_b_ln_g, m_b_ln_b, m_b_pw2_w, m_b_pw2_b, m_c_in_w, m_c_in_b, m_c_conv_w, m_c_conv_b, m_c_ga_w, m_c_ga_b, m_c_gx_w, m_c_gx_b, m_c_lambda, m_c_out_w, m_c_out_b, m_d_in_w, m_d_in_b, m_d_ln_g, m_d_ln_b, m_d_sp_w, m_d_sp_b, m_d_out_w, m_d_out_b, m_f_up_w, m_f_conv_w, m_f_conv_b, m_f_down_w, v_norm_mix, v_norm_ffn, v_norm_final, v_a_in_proj, v_a_conv_w, v_a_conv_b, v_a_dt_bias, v_a_log, v_a_d_skip, v_a_norm, v_a_out_proj, v_b_pw1_w, v_b_pw1_b, v_b_dw_w, v_b_dw_b, v_b_ln_g, v_b_ln_b, v_b_pw2_w, v_b_pw2_b, v_c_in_w, v_c_in_b, v_c_conv_w, v_c_conv_b, v_c_ga_w, v_c_ga_b, v_c_gx_w, v_c_gx_b, v_c_lambda, v_c_out_w, v_c_out_b, v_d_in_w, v_d_in_b, v_d_ln_g, v_d_ln_b, v_d_sp_w, v_d_sp_b, v_d_out_w, v_d_out_b, v_f_up_w, v_f_conv_w, v_f_conv_b, v_f_down_w):
    args = locals()
    weights = {n: args[n] for n in WEIGHTS}
    moments_m = {n: args['m_' + n] for n in WEIGHTS}
    moments_v = {n: args['v_' + n] for n in WEIGHTS}
    loss, dx, grad, delta, new_m, new_v = _step(x, loss_target, weights, moments_m, moments_v)
    return (loss, dx, *[grad[n] for n in WEIGHTS], *[delta[n] for n in WEIGHTS],
            *[new_m[n] for n in WEIGHTS], *[new_v[n] for n in WEIGHTS])
```

```python
import functools
import math

import jax
import jax.numpy as jnp
import numpy as np
from jax import lax
from jax.experimental import pallas as pl
from jax.experimental.pallas import tpu as pltpu

f32 = jnp.float32
bf16 = jnp.bfloat16
MESH = pl.DeviceIdType.MESH
HIGHEST = lax.Precision.HIGHEST

D_MODEL = 1024
DEPTH = 4
RMS_EPS = 1e-6
LN_EPS = 1e-5
SSD_D_INNER = 2048
SSD_HEADS = 32
SSD_BC = 1024
SSD_CONV_DIM = 4096
SSD_CHUNK = 128
SSD_GROUPS = 8
LRU_W = 1280
LRU_BLOCK = 256
LRU_C = 8.0
SGU_HALF = 2048
SGU_GROUPS = 8
SGU_CHUNK = 128
FFN_H = 2816
ADAM_LR, ADAM_B1, ADAM_B2, ADAM_EPS, ADAM_WD, ADAM_STEP = 0.001, 0.9, 0.999, 1e-08, 0.01, 10

LANE = 128
SUBLANE = 8
VMEM_LIMIT = 56 * 1024 * 1024
FLAT_COLS = 1024

WEIGHTS = ['norm_mix', 'norm_ffn', 'norm_final', 'a_in_proj', 'a_conv_w', 'a_conv_b', 'a_dt_bias', 'a_log',
           'a_d_skip', 'a_norm', 'a_out_proj', 'b_pw1_w', 'b_pw1_b', 'b_dw_w', 'b_dw_b', 'b_ln_g', 'b_ln_b',
           'b_pw2_w', 'b_pw2_b', 'c_in_w', 'c_in_b', 'c_conv_w', 'c_conv_b', 'c_ga_w', 'c_ga_b', 'c_gx_w',
           'c_gx_b', 'c_lambda', 'c_out_w', 'c_out_b', 'd_in_w', 'd_in_b', 'd_ln_g', 'd_ln_b', 'd_sp_w',
           'd_sp_b', 'd_out_w', 'd_out_b', 'f_up_w', 'f_conv_w', 'f_conv_b', 'f_down_w']
SHARD_AXIS = {
    'norm_mix': None, 'norm_ffn': None, 'norm_final': None, 'a_in_proj': 2, 'a_conv_w': 2, 'a_conv_b': None,
    'a_dt_bias': None, 'a_log': None, 'a_d_skip': None, 'a_norm': None, 'a_out_proj': 1, 'b_pw1_w': 2,
    'b_pw1_b': 1, 'b_dw_w': 2, 'b_dw_b': 1, 'b_ln_g': 1, 'b_ln_b': 1, 'b_pw2_w': 1, 'b_pw2_b': 1, 'c_in_w': 2,
    'c_in_b': 1, 'c_conv_w': 2, 'c_conv_b': 1, 'c_ga_w': 2, 'c_ga_b': 2, 'c_gx_w': 2, 'c_gx_b': 2,
    'c_lambda': 1, 'c_out_w': 1, 'c_out_b': 1, 'd_in_w': 2, 'd_in_b': 1, 'd_ln_g': 1, 'd_ln_b': 1,
    'd_sp_w': None, 'd_sp_b': None, 'd_out_w': 1, 'd_out_b': 1, 'f_up_w': 2, 'f_conv_w': 2, 'f_conv_b': None,
    'f_down_w': 1}
MATMUL_WEIGHTS = ['a_in_proj', 'a_out_proj', 'b_pw1_w', 'b_pw2_w', 'c_in_w', 'c_ga_w', 'c_gx_w', 'c_out_w',
                  'd_in_w', 'd_out_w', 'f_up_w', 'f_down_w']
DIRECT_COLS = ['b_pw1_w', 'c_in_w', 'd_in_w', 'f_up_w']
DIRECT_ROWS = ['a_out_proj', 'b_pw2_w', 'c_out_w', 'd_out_w', 'f_down_w']
DIRECT = DIRECT_COLS + DIRECT_ROWS
EARLY_DIRECT = ['a_out_proj']
LATE_DIRECT = [n for n in DIRECT if n not in EARLY_DIRECT]
LATE_SECOND = ['c_in_w', 'c_out_w', 'd_in_w', 'd_out_w']
LATE_FIRST = [n for n in LATE_DIRECT if n not in LATE_SECOND]
LAST_SWAPPED_INDEX = LATE_DIRECT.index('f_up_w')
PACKED_MM = [n for n in MATMUL_WEIGHTS if n not in DIRECT]
SHARDED = [n for n in WEIGHTS if SHARD_AXIS[n] is not None]
SHARDED_VEC = [n for n in SHARDED if n not in MATMUL_WEIGHTS]
REPLICATED = [n for n in WEIGHTS if SHARD_AXIS[n] is None]
N_CHIPS = 4
N_DEV = 8


def _tile(n, cap, mult):
    if n <= cap:
        return n
    t = (cap // mult) * mult
    while t >= mult:
        if n % t == 0:
            return t
        t -= mult
    raise ValueError(f"no tile for {n} under {cap} in steps of {mult}")


def _cparams(sem=None):
    if sem is None:
        return pltpu.CompilerParams(vmem_limit_bytes=VMEM_LIMIT)
    return pltpu.CompilerParams(dimension_semantics=sem, vmem_limit_bytes=VMEM_LIMIT)


def _dg(a, b, ca, cb):
    return lax.dot_general(a.astype(bf16), b.astype(bf16), (((ca,), (cb,)), ((), ())), preferred_element_type=f32)


@jax.custom_vjp
def _dot_nn(a, b):
    return _dg(a, b, 1, 0)


def _dot_nn_fwd(a, b):
    return _dg(a, b, 1, 0), (a, b)


def _dot_nn_bwd(res, g):
    a, b = res
    return _dg(g, b, 1, 1).astype(a.dtype), _dg(a, g, 0, 0).astype(b.dtype)


_dot_nn.defvjp(_dot_nn_fwd, _dot_nn_bwd)


@jax.custom_vjp
def _dot_nt(a, b):
    return _dg(a, b, 1, 1)


def _dot_nt_fwd(a, b):
    return _dg(a, b, 1, 1), (a, b)


def _dot_nt_bwd(res, g):
    a, b = res
    return _dg(g, b, 1, 0).astype(a.dtype), _dg(g, a, 0, 0).astype(b.dtype)


_dot_nt.defvjp(_dot_nt_fwd, _dot_nt_bwd)


@jax.custom_vjp
def _dot_tn(a, b):
    return _dg(a, b, 0, 0)


def _dot_tn_fwd(a, b):
    return _dg(a, b, 0, 0), (a, b)


def _dot_tn_bwd(res, g):
    a, b = res
    return _dg(b, g, 1, 1).astype(a.dtype), _dg(a, g, 1, 0).astype(b.dtype)


_dot_tn.defvjp(_dot_tn_fwd, _dot_tn_bwd)


def _expm1(x):
    small = jnp.abs(x) < 0.03
    xs = jnp.where(small, x, 0.0)
    series = xs * (1.0 + xs * (0.5 + xs * (1.0 / 6.0 + xs * (1.0 / 24.0 + xs * (1.0 / 120.0)))))
    return jnp.where(small, series, jnp.exp(x) - 1.0)


def _rms(x, g):
    return x * lax.rsqrt(jnp.mean(x * x, axis=-1, keepdims=True) + RMS_EPS) * g


def _layer_norm(x, g, b):
    mu = jnp.mean(x, axis=-1, keepdims=True)
    xc = x - mu
    return xc * lax.rsqrt(jnp.mean(xc * xc, axis=-1, keepdims=True) + LN_EPS) * g + b


def _causal_taps(ext, w, halo, rows):
    k_taps = w.shape[0]
    acc = None
    for k in range(k_taps):
        lo = halo - (k_taps - 1) + k
        term = w[k:k + 1, :] * ext[lo:lo + rows, :]
        acc = term if acc is None else acc + term
    return acc


def _mm(a, b, mode, name, *, bias=None, add=None, out_dtype=f32, tm_cap=1408, tn_cap=1408, tk_cap=1408,
        b_cols_sharded=False, b_layer=None, out_cols_sharded=False):
    shard_cols = None
    if b_cols_sharded:
        shard_cols = b.shape[-1]
        b_dims = (b.shape[-2], N_CHIPS * shard_cols)
    else:
        b_dims = b.shape
    if mode == 'nn':
        (m, k), (k2, n) = a.shape, b_dims
    elif mode == 'nt':
        (m, k), (n, k2) = a.shape, b_dims
    else:
        (k, m), (k2, n) = a.shape, b_dims
    assert k == k2, (name, a.shape, b.shape)
    tm = _tile(m, tm_cap, LANE if mode == 'tn' else SUBLANE)
    tn = _tile(n, tn_cap, LANE)
    tk = _tile(k, tk_cap, LANE if mode != 'tn' else SUBLANE)
    if b_cols_sharded and mode == 'nn':
        tn = shard_cols
    if b_cols_sharded and mode == 'nt':
        tk = shard_cols
    if out_cols_sharded:
        assert mode == 'tn' and n % N_CHIPS == 0
        tn = n // N_CHIPS
    nk = k // tk

    def shard_block(rows):
        lead = (None,) * (b.ndim - 2)
        return lead + (rows, shard_cols)

    def shard_index(shard, row_block):
        return (shard, row_block, 0) if b_layer is None else (shard, b_layer, row_block, 0)

    if mode == 'nn':
        a_spec = pl.BlockSpec((tm, tk), lambda i, j, kk: (i, kk))
        if b_cols_sharded:
            b_spec = pl.BlockSpec(shard_block(tk), lambda i, j, kk: shard_index(j, kk))
        else:
            b_spec = pl.BlockSpec((tk, tn), lambda i, j, kk: (kk, j))
        ca, cb = 1, 0
    elif mode == 'nt':
        a_spec = pl.BlockSpec((tm, tk), lambda i, j, kk: (i, kk))
        if b_cols_sharded:
            b_spec = pl.BlockSpec(shard_block(tn), lambda i, j, kk: shard_index(kk, j))
        else:
            b_spec = pl.BlockSpec((tn, tk), lambda i, j, kk: (j, kk))
        ca, cb = 1, 1
    else:
        a_spec = pl.BlockSpec((tk, tm), lambda i, j, kk: (kk, i))
        b_spec = pl.BlockSpec((tk, tn), lambda i, j, kk: (kk, j))
        ca, cb = 0, 0
    in_specs, operands = [a_spec, b_spec], [a, b]
    if bias is not None:
        in_specs.append(pl.BlockSpec((1, tn), lambda i, j, kk: (0, j)))
        operands.append(bias)
    if add is not None:
        in_specs.append(pl.BlockSpec((tm, tn), lambda i, j, kk: (i, j)))
        operands.append(add)

    def body(*refs):
        a_ref, b_ref = refs[0], refs[1]
        pos = 2
        bias_ref = add_ref = None
        if bias is not None:
            bias_ref = refs[pos]
            pos += 1
        if add is not None:
            add_ref = refs[pos]
            pos += 1
        o_ref, acc_ref = refs[pos], refs[pos + 1]
        kk = pl.program_id(2)

        @pl.when(kk == 0)
        def _():
            acc_ref[...] = jnp.zeros_like(acc_ref)

        acc_ref[...] += _dg(a_ref[...], b_ref[...], ca, cb)

        @pl.when(kk == nk - 1)
        def _():
            r = acc_ref[...]
            if bias_ref is not None:
                r = r + bias_ref[...]
            if add_ref is not None:
                r = r + add_ref[...].astype(f32)
            o_ref[...] = r.astype(out_dtype)

    if out_cols_sharded:
        out_shape = jax.ShapeDtypeStruct((N_CHIPS, m, tn), out_dtype)
        out_spec = pl.BlockSpec((None, tm, tn), lambda i, j, kk: (j, i, 0))
    else:
        out_shape = jax.ShapeDtypeStruct((m, n), out_dtype)
        out_spec = pl.BlockSpec((tm, tn), lambda i, j, kk: (i, j))
    return pl.pallas_call(
        body, name=name, out_shape=out_shape,
        grid=(m // tm, n // tn, nk), in_specs=in_specs, out_specs=out_spec,
        scratch_shapes=[pltpu.VMEM((tm, tn), f32)],
        compiler_params=_cparams(("parallel", "parallel", "arbitrary")),
    )(*operands)


def _mm_w(a, w, mode, name, **kw):
    shards, layer = w
    return _mm(a, shards, mode, name, b_cols_sharded=True, b_layer=layer, **kw)


def _row_specs(tiles, halo_of, rows, halo, n_tiles, reverse):
    def tile_index(i):
        return n_tiles - 1 - i if reverse else i

    specs, operands = [], []
    for arr, has_halo in zip(tiles, halo_of):
        cols = arr.shape[1]
        specs.append(pl.BlockSpec((rows, cols), lambda i: (tile_index(i), 0)))
        operands.append(arr)
        if has_halo:
            per = rows // halo
            specs.append(pl.BlockSpec((halo, cols), lambda i: (jnp.maximum(tile_index(i) * per - 1, 0), 0)))
            operands.append(arr)
    return specs, operands, tile_index


def _load_tiles(refs, halo_of, tile_id, rows, halo):
    vals, pos = [], 0
    for has_halo in halo_of:
        cur = refs[pos][...].astype(f32)
        pos += 1
        if has_halo:
            before = refs[pos][...].astype(f32)
            pos += 1
            before = jnp.where(tile_id > 0, before, jnp.zeros_like(before))
            cur = jnp.concatenate([before, cur], axis=0)
        vals.append(cur)
    return vals, pos


def _valid_rows(tile_id, rows, halo):
    r = lax.broadcasted_iota(jnp.int32, (halo + rows, 1), 0)
    return jnp.logical_or(r >= halo, tile_id > 0).astype(f32)


def _row_fwd(f, tiles, params, outs, *, rows, name, halo=0, halo_of=None):
    t_len = tiles[0].shape[0]
    rows = min(rows, t_len)
    n_tiles = t_len // rows
    halo_of = halo_of or [False] * len(tiles)
    specs, operands, _ = _row_specs(tiles, halo_of, rows, halo, n_tiles, False)
    for p in params:
        specs.append(pl.BlockSpec(p.shape, lambda i: (0, 0)))
        operands.append(p)

    def body(*refs):
        i = pl.program_id(0)
        vals, pos = _load_tiles(refs, halo_of, i, rows, halo)
        pvals = [refs[pos + j][...] for j in range(len(params))]
        pos += len(params)
        kw = {'valid': _valid_rows(i, rows, halo)} if halo else {}
        res = f(*vals, *pvals, **kw)
        for o_ref, o in zip(refs[pos:], res):
            o_ref[...] = o.astype(o_ref.dtype)

    return pl.pallas_call(
        body, name=name,
        out_shape=[jax.ShapeDtypeStruct((t_len, c), d) for c, d in outs],
        grid=(n_tiles,), in_specs=specs,
        out_specs=[pl.BlockSpec((rows, c), lambda i: (i, 0)) for c, _ in outs],
        compiler_params=_cparams(("parallel",)),
    )(*operands)


def _row_bwd(f, tiles, params, cots, *, rows, name, halo=0, halo_of=None, tile_dtypes=None):
    t_len = tiles[0].shape[0]
    rows = min(rows, t_len)
    n_tiles = t_len // rows
    halo_of = halo_of or [False] * len(tiles)
    tile_dtypes = tile_dtypes or [f32] * len(tiles)
    specs, operands, tile_index = _row_specs(tiles, halo_of, rows, halo, n_tiles, True)
    for p in params:
        specs.append(pl.BlockSpec(p.shape, lambda i: (0, 0)))
        operands.append(p)
    for ct in cots:
        specs.append(pl.BlockSpec((rows, ct.shape[1]), lambda i: (tile_index(i), 0)))
        operands.append(ct)
    n_t, n_p, n_c = len(tiles), len(params), len(cots)
    out_shape = [jax.ShapeDtypeStruct(t.shape, d) for t, d in zip(tiles, tile_dtypes)]
    out_shape += [jax.ShapeDtypeStruct(p.shape, f32) for p in params]
    out_specs = [pl.BlockSpec((rows, t.shape[1]), lambda i: (tile_index(i), 0)) for t in tiles]
    out_specs += [pl.BlockSpec(p.shape, lambda i: (0, 0)) for p in params]
    scratch = [pltpu.VMEM((halo, t.shape[1]), f32) for t, h in zip(tiles, halo_of) if h]

    def body(*refs):
        i = pl.program_id(0)
        tile_id = tile_index(i)
        vals, pos = _load_tiles(refs, halo_of, tile_id, rows, halo)
        pvals = [refs[pos + j][...] for j in range(n_p)]
        pos += n_p
        cvals = [refs[pos + j][...].astype(f32) for j in range(n_c)]
        pos += n_c
        d_tile_refs = refs[pos:pos + n_t]
        d_param_refs = refs[pos + n_t:pos + n_t + n_p]
        carries = list(refs[pos + n_t + n_p:])
        kw = {'valid': _valid_rows(tile_id, rows, halo)} if halo else {}
        _, vjp = jax.vjp(lambda *args: tuple(f(*args, **kw)), *vals, *pvals)
        grads = vjp(tuple(cvals))

        @pl.when(i == 0)
        def _():
            for cr in carries:
                cr[...] = jnp.zeros_like(cr)
            for dp in d_param_refs:
                dp[...] = jnp.zeros_like(dp)

        ci = 0
        for t in range(n_t):
            g = grads[t]
            if halo_of[t]:
                cr = carries[ci]
                ci += 1
                d_tile_refs[t][0:rows - halo, :] = g[halo:rows, :].astype(d_tile_refs[t].dtype)
                d_tile_refs[t][rows - halo:rows, :] = (g[rows:rows + halo, :] + cr[...]).astype(d_tile_refs[t].dtype)
                cr[...] = g[0:halo, :]
            else:
                d_tile_refs[t][...] = g.astype(d_tile_refs[t].dtype)
        for j in range(n_p):
            d_param_refs[j][...] += grads[n_t + j]

    return pl.pallas_call(
        body, name=name, out_shape=out_shape, grid=(n_tiles,), in_specs=specs, out_specs=out_specs,
        scratch_shapes=scratch, compiler_params=_cparams(("arbitrary",)),
    )(*operands)


def _strip_specs(tiles, rows, halo, n_tiles, reverse):
    def tile_index(i):
        return n_tiles - 1 - i if reverse else i

    specs, operands = [], []
    for arr, _, has_halo in tiles:
        cols = arr.shape[1]
        specs.append(pl.BlockSpec((rows, cols), lambda i: (tile_index(i), 0)))
        operands.append(arr)
        if has_halo:
            per = rows // halo
            specs.append(pl.BlockSpec((halo, cols), lambda i: (jnp.maximum(tile_index(i) * per - 1, 0), 0)))
            operands.append(arr)
    return specs, operands, tile_index


def _strip_sources(refs, tiles, ext_scratch, tile_id, rows, halo):
    srcs, pos, si = [], 0, 0
    for _, _, has_halo in tiles:
        cur = refs[pos]
        pos += 1
        if has_halo:
            before = refs[pos]
            pos += 1
            scr = ext_scratch[si]
            si += 1
            scr[0:halo, :] = jnp.where(tile_id > 0, before[...].astype(f32), 0.0)
            scr[halo:halo + rows, :] = cur[...].astype(f32)
            srcs.append(scr)
        else:
            srcs.append(cur)
    return srcs, pos


def _cols(base, c0, cs):
    return pl.ds(pl.multiple_of(base + c0, LANE), cs)


def _strip_inputs(tiles, srcs, params, p_refs, r0, c0, rs, cs, halo):
    vals = []
    for (_, bases, has_halo), src in zip(tiles, srcs):
        n_rows = halo + rs if has_halo else rs
        for b in bases:
            vals.append(src[pl.ds(r0, n_rows), _cols(b, c0, cs)].astype(f32))
    for (_, bases), p_ref in zip(params, p_refs):
        for b in bases:
            vals.append(p_ref[:, _cols(b, c0, cs)])
    return vals


def _strip_valid(tile_id, r0, rs, halo):
    r = lax.broadcasted_iota(jnp.int32, (halo + rs, 1), 0) + r0
    return jnp.logical_or(r >= halo, tile_id > 0).astype(f32)


def _strip_fwd(f, tiles, params, outs, *, rows, rs, cs, width, name, halo=0):
    t_len = tiles[0][0].shape[0]
    rows = min(rows, t_len)
    n_tiles, n_rs, n_cs = t_len // rows, rows // rs, width // cs
    specs, operands, _ = _strip_specs(tiles, rows, halo, n_tiles, False)
    for p, _ in params:
        specs.append(pl.BlockSpec(p.shape, lambda i: (0, 0)))
        operands.append(p)
    n_p, n_o = len(params), len(outs)
    scratch = [pltpu.VMEM((halo + rows, arr.shape[1]), f32) for arr, _, hh in tiles if hh]

    def body(*refs):
        i = pl.program_id(0)
        ext_scratch = refs[len(refs) - len(scratch):]
        srcs, pos = _strip_sources(refs, tiles, ext_scratch, i, rows, halo)
        p_refs = refs[pos:pos + n_p]
        o_refs = refs[pos + n_p:pos + n_p + n_o]

        def row_loop(r, carry):
            r0 = pl.multiple_of(r * rs, rs)
            kw = {'valid': _strip_valid(i, r0, rs, halo)} if halo else {}

            def col_loop(c, carry2):
                c0 = c * cs
                res = f(*_strip_inputs(tiles, srcs, params, p_refs, r0, c0, rs, cs, halo), **kw)
                k = 0
                for (_, dt, bases), o_ref in zip(outs, o_refs):
                    for b in bases:
                        o_ref[pl.ds(r0, rs), _cols(b, c0, cs)] = res[k].astype(dt)
                        k += 1
                return carry2

            return lax.fori_loop(0, n_cs, col_loop, carry)

        lax.fori_loop(0, n_rs, row_loop, 0)

    return pl.pallas_call(
        body, name=name,
        out_shape=[jax.ShapeDtypeStruct((t_len, c), d) for c, d, _ in outs],
        grid=(n_tiles,), in_specs=specs,
        out_specs=[pl.BlockSpec((rows, c), lambda i: (i, 0)) for c, _, _ in outs],
        scratch_shapes=scratch, compiler_params=_cparams(("parallel",)),
    )(*operands)


def _strip_bwd(f, tiles, params, cots, *, rows, rs, cs, width, name, halo=0, tile_dtypes=None):
    t_len = tiles[0][0].shape[0]
    rows = min(rows, t_len)
    n_tiles, n_rs, n_cs = t_len // rows, rows // rs, width // cs
    tile_dtypes = tile_dtypes or [f32] * len(tiles)
    specs, operands, tile_index = _strip_specs(tiles, rows, halo, n_tiles, True)
    for p, _ in params:
        specs.append(pl.BlockSpec(p.shape, lambda i: (0, 0)))
        operands.append(p)
    for ct, _ in cots:
        specs.append(pl.BlockSpec((rows, ct.shape[1]), lambda i: (tile_index(i), 0)))
        operands.append(ct)
    n_t, n_p, n_c = len(tiles), len(params), len(cots)
    out_shape = [jax.ShapeDtypeStruct(t[0].shape, d) for t, d in zip(tiles, tile_dtypes)]
    out_shape += [jax.ShapeDtypeStruct(p.shape, f32) for p, _ in params]
    out_specs = [pl.BlockSpec((rows, t[0].shape[1]), lambda i: (tile_index(i), 0)) for t in tiles]
    out_specs += [pl.BlockSpec(p.shape, lambda i: (0, 0)) for p, _ in params]
    halo_tiles = [t for t in tiles if t[2]]
    scratch = [pltpu.VMEM((halo + rows, arr.shape[1]), f32) for arr, _, _ in halo_tiles]
    scratch += [pltpu.VMEM((halo + rows, arr.shape[1]), f32) for arr, _, _ in halo_tiles]
    scratch += [pltpu.VMEM((halo, arr.shape[1]), f32) for arr, _, _ in halo_tiles]
    n_h = len(halo_tiles)

    def body(*refs):
        i = pl.program_id(0)
        tile_id = tile_index(i)
        scr = refs[len(refs) - 3 * n_h:]
        ext_scratch, grad_scratch, carries = scr[:n_h], scr[n_h:2 * n_h], scr[2 * n_h:]
        srcs, pos = _strip_sources(refs, tiles, ext_scratch, tile_id, rows, halo)
        p_refs = refs[pos:pos + n_p]
        c_refs = refs[pos + n_p:pos + n_p + n_c]
        pos += n_p + n_c
        d_tile_refs = refs[pos:pos + n_t]
        d_param_refs = refs[pos + n_t:pos + n_t + n_p]

        @pl.when(i == 0)
        def _():
            for cr in carries:
                cr[...] = jnp.zeros_like(cr)
            for dp in d_param_refs:
                dp[...] = jnp.zeros_like(dp)

        for gs in grad_scratch:
            gs[...] = jnp.zeros_like(gs)

        def row_loop(r, carry):
            r0 = pl.multiple_of(r * rs, rs)
            kw = {'valid': _strip_valid(tile_id, r0, rs, halo)} if halo else {}

            def col_loop(c, carry2):
                c0 = c * cs
                vals = _strip_inputs(tiles, srcs, params, p_refs, r0, c0, rs, cs, halo)
                cvals = []
                for (_, bases), c_ref in zip(cots, c_refs):
                    for b in bases:
                        cvals.append(c_ref[pl.ds(r0, rs), _cols(b, c0, cs)].astype(f32))
                _, vjp = jax.vjp(lambda *args: tuple(f(*args, **kw)), *vals)
                grads = vjp(tuple(cvals))
                k, hi = 0, 0
                for t, (_, bases, has_halo) in enumerate(tiles):
                    for b in bases:
                        if has_halo:
                            grad_scratch[hi][pl.ds(r0, halo + rs), _cols(b, c0, cs)] += grads[k]
                        else:
                            d_tile_refs[t][pl.ds(r0, rs), _cols(b, c0, cs)] = grads[k].astype(d_tile_refs[t].dtype)
                        k += 1
                    hi += has_halo
                for (_, bases), dp in zip(params, d_param_refs):
                    for b in bases:
                        dp[:, _cols(b, c0, cs)] += grads[k]
                        k += 1
                return carry2

            return lax.fori_loop(0, n_cs, col_loop, carry)

        lax.fori_loop(0, n_rs, row_loop, 0)

        hi = 0
        for t, (_, _, has_halo) in enumerate(tiles):
            if has_halo:
                gs, cr, d_ref = grad_scratch[hi], carries[hi], d_tile_refs[t]
                hi += 1
                d_ref[0:rows - halo, :] = gs[halo:rows, :].astype(d_ref.dtype)
                d_ref[rows - halo:rows, :] = (gs[rows:rows + halo, :] + cr[...]).astype(d_ref.dtype)
                cr[...] = gs[0:halo, :]

    return pl.pallas_call(
        body, name=name, out_shape=out_shape, grid=(n_tiles,), in_specs=specs, out_specs=out_specs,
        scratch_shapes=scratch, compiler_params=_cparams(("arbitrary",)),
    )(*operands)


def _fold8(v):
    acc = v[0:SUBLANE, :]
    for m in range(1, v.shape[0] // SUBLANE):
        acc = acc + v[m * SUBLANE:(m + 1) * SUBLANE, :]
    return acc


class _ConvPlan:
    def __init__(self, x, w, b, *, in_bases, mid_bases, width, rows, rs, pre=None, pre_params=(), post=None):
        self.x, self.w, self.b = x, w, b
        self.in_bases, self.mid_bases, self.width = in_bases, mid_bases, width
        self.pre, self.pre_params, self.post = pre, list(pre_params), post
        self.k_taps = w.shape[0]
        tile_rows = SUBLANE * (4 // x.dtype.itemsize)
        self.halo = -(-(self.k_taps - 1) // tile_rows) * tile_rows
        self.t_len = x.shape[0]
        self.rows = min(rows, self.t_len)
        self.rs = rs
        self.n_tiles, self.n_rs, self.n_cs = self.t_len // self.rows, self.rows // rs, width // LANE
        self.n_mid = len(mid_bases)
        if pre is None:
            assert len(in_bases) == self.n_mid

    def in_specs(self, tile_index):
        cols = self.x.shape[1]
        per = self.rows // self.halo
        specs = [pl.BlockSpec((self.rows, cols), lambda i: (tile_index(i), 0)),
                 pl.BlockSpec((self.halo, cols), lambda i: (jnp.maximum(tile_index(i) * per - 1, 0), 0)),
                 pl.BlockSpec(self.w.shape, lambda i: (0, 0)), pl.BlockSpec(self.b.shape, lambda i: (0, 0))]
        operands = [self.x, self.x, self.w, self.b]
        for p, _ in self.pre_params:
            specs.append(pl.BlockSpec(p.shape, lambda i: (0, 0)))
            operands.append(p)
        return specs, operands

    def pre_strips(self, pp_refs, c0):
        return [p_ref[:, _cols(b, c0, LANE)] for (_, bases), p_ref in zip(self.pre_params, pp_refs) for b in bases]

    def fill_conv_input(self, cur_ref, before_ref, pp_refs, u_ref, tile_id):
        started = (tile_id > 0).astype(f32)

        def col_loop(c, carry):
            c0 = c * LANE
            pps = self.pre_strips(pp_refs, c0)
            xs = [before_ref[:, _cols(b, c0, LANE)].astype(f32) for b in self.in_bases]
            for j, u in enumerate(self.pre(*xs, *pps, valid=started)):
                u_ref[0:self.halo, _cols(j * self.width, c0, LANE)] = u
            for r in range(self.n_rs):
                xs = [cur_ref[r * self.rs:(r + 1) * self.rs, _cols(b, c0, LANE)].astype(f32) for b in self.in_bases]
                for j, u in enumerate(self.pre(*xs, *pps, valid=1.0)):
                    u_ref[self.halo + r * self.rs:self.halo + (r + 1) * self.rs, _cols(j * self.width, c0, LANE)] = u
            return carry

        lax.fori_loop(0, self.n_cs, col_loop, 0)

    def tap(self, cur_ref, before_ref, u_ref, tile_id, r, j, k, c0):
        lo = r * self.rs - (self.k_taps - 1) + k
        if u_ref is not None:
            return u_ref[self.halo + lo:self.halo + lo + self.rs, _cols(j * self.width, c0, LANE)]
        cols = _cols(self.in_bases[j], c0, LANE)
        if lo >= 0:
            return cur_ref[lo:lo + self.rs, cols].astype(f32)
        head = before_ref[self.halo + lo:self.halo, cols].astype(f32)
        head = jnp.where(tile_id > 0, head, 0.0)
        return jnp.concatenate([head, cur_ref[0:self.rs + lo, cols].astype(f32)], axis=0)

    def conv(self, cur_ref, before_ref, u_ref, w_ref, b_ref, tile_id, r, c0):
        hcs = []
        for j, mb in enumerate(self.mid_bases):
            cols = _cols(mb, c0, LANE)
            acc = b_ref[:, cols]
            for k in range(self.k_taps):
                acc = acc + w_ref[k:k + 1, cols] * self.tap(cur_ref, before_ref, u_ref, tile_id, r, j, k, c0)
            hcs.append(acc)
        return hcs


def _conv_fwd(plan, outs, name, rider=None):
    n_pp = len(plan.pre_params)
    n_in, n_out = 4 + n_pp, len(outs)
    r_n = rider.n if rider else 0
    specs, operands = plan.in_specs(lambda i: i)
    scratch = [pltpu.VMEM((plan.halo + plan.rows, plan.n_mid * plan.width), f32)] if plan.pre else []
    n_scr = len(scratch)

    def body(*refs):
        cur_ref, before_ref, w_ref, b_ref = refs[:4]
        pp_refs = refs[4:n_in]
        r_srcs = refs[n_in:n_in + r_n]
        o_refs = refs[n_in + r_n:n_in + r_n + n_out]
        r_outs = refs[n_in + r_n + n_out:n_in + 2 * r_n + n_out]
        scr = refs[n_in + 2 * r_n + n_out:]
        u_ref = scr[0] if plan.pre else None
        i = pl.program_id(0)
        if rider:
            @pl.when(i == 0)
            def _():
                rider.start(r_srcs, r_outs, scr[n_scr:])

        if plan.pre:
            plan.fill_conv_input(cur_ref, before_ref, pp_refs, u_ref, i)

        def col_loop(c, carry):
            c0 = c * LANE
            for r in range(plan.n_rs):
                res = plan.post(*plan.conv(cur_ref, before_ref, u_ref, w_ref, b_ref, i, r, c0))
                n = 0
                for (_, dt, bases), o_ref in zip(outs, o_refs):
                    for ob in bases:
                        o_ref[r * plan.rs:(r + 1) * plan.rs, _cols(ob, c0, LANE)] = res[n].astype(dt)
                        n += 1
            return carry

        lax.fori_loop(0, plan.n_cs, col_loop, 0)
        if rider:
            @pl.when(i == plan.n_tiles - 1)
            def _():
                rider.finish(r_srcs, r_outs, scr[n_scr:])

    return pl.pallas_call(
        body, name=name,
        out_shape=[jax.ShapeDtypeStruct((plan.t_len, c), d) for c, d, _ in outs] + (rider.out_shapes if rider else []),
        grid=(plan.n_tiles,), in_specs=specs + [HBM_SPEC] * r_n,
        out_specs=[pl.BlockSpec((plan.rows, c), lambda i: (i, 0)) for c, _, _ in outs] + [HBM_SPEC] * r_n,
        scratch_shapes=scratch + (rider.semaphores if rider else []),
        compiler_params=_cparams(("arbitrary",) if rider else ("parallel",)),
    )(*operands, *(rider.operands if rider else []))


def _conv_bwd(plan, cots, dx_dtype, name, rider=None):
    n_pp, n_c = len(plan.pre_params), len(cots)
    r_n = rider.n if rider else 0
    n_tiles, rows, rs, halo, k_taps = plan.n_tiles, plan.rows, plan.rs, plan.halo, plan.k_taps

    def tile_index(i):
        return n_tiles - 1 - i

    specs, operands = plan.in_specs(tile_index)
    for ct, _ in cots:
        specs.append(pl.BlockSpec((rows, ct.shape[1]), lambda i: (tile_index(i), 0)))
        operands.append(ct)
    mid_cols = plan.n_mid * plan.width
    out_shape = [jax.ShapeDtypeStruct(plan.x.shape, dx_dtype), jax.ShapeDtypeStruct(plan.w.shape, f32),
                 jax.ShapeDtypeStruct(plan.b.shape, f32)]
    out_shape += [jax.ShapeDtypeStruct(p.shape, f32) for p, _ in plan.pre_params]
    out_specs = [pl.BlockSpec((rows, plan.x.shape[1]), lambda i: (tile_index(i), 0)),
                 pl.BlockSpec(plan.w.shape, lambda i: (0, 0)), pl.BlockSpec(plan.b.shape, lambda i: (0, 0))]
    out_specs += [pl.BlockSpec(p.shape, lambda i: (0, 0)) for p, _ in plan.pre_params]
    w_cols = plan.w.shape[1]
    scratch = [pltpu.VMEM((rows + halo, mid_cols), f32),
               pltpu.VMEM((halo, mid_cols), f32),
               pltpu.VMEM(((k_taps + 1) * SUBLANE, w_cols), f32)]
    if plan.pre:
        scratch.append(pltpu.VMEM((halo + rows, mid_cols), f32))

    def body(*refs):
        cur_ref, before_ref, w_ref, b_ref = refs[:4]
        pp_refs = refs[4:4 + n_pp]
        c_refs = refs[4 + n_pp:4 + n_pp + n_c]
        pos = 4 + n_pp + n_c
        r_srcs = refs[pos:pos + r_n]
        pos += r_n
        dx_ref, dw_ref, db_ref = refs[pos:pos + 3]
        dpp_refs = refs[pos + 3:pos + 3 + n_pp]
        r_outs = refs[pos + 3 + n_pp:pos + 3 + n_pp + r_n]
        pos += 3 + n_pp + r_n
        g_ref, carry_ref, acc_ref = refs[pos:pos + 3]
        u_ref = refs[pos + 3] if plan.pre else None
        r_sems = refs[pos + (4 if plan.pre else 3):]
        i = pl.program_id(0)
        tile_id = tile_index(i)
        if rider:
            @pl.when(i == 0)
            def _():
                rider.start(r_srcs, r_outs, r_sems)

        @pl.when(i == 0)
        def _():
            carry_ref[...] = jnp.zeros_like(carry_ref)
            acc_ref[...] = jnp.zeros_like(acc_ref)
            for dp in dpp_refs:
                dp[...] = jnp.zeros_like(dp)

        g_ref[rows:rows + halo, :] = carry_ref[...]
        if plan.pre:
            plan.fill_conv_input(cur_ref, before_ref, pp_refs, u_ref, tile_id)

        def col_loop(c, carry):
            c0 = c * LANE
            for r in range(plan.n_rs):
                hcs = plan.conv(cur_ref, before_ref, u_ref, w_ref, b_ref, tile_id, r, c0)
                _, vjp = jax.vjp(lambda *a: tuple(plan.post(*a)), *hcs)
                cvals = [c_ref[r * rs:(r + 1) * rs, _cols(cb, c0, LANE)].astype(f32)
                         for (_, bases), c_ref in zip(cots, c_refs) for cb in bases]
                d_hcs = vjp(tuple(cvals))
                for j, mb in enumerate(plan.mid_bases):
                    g_ref[r * rs:(r + 1) * rs, _cols(j * plan.width, c0, LANE)] = d_hcs[j]
                    wc = _cols(mb, c0, LANE)
                    acc_ref[k_taps * SUBLANE:(k_taps + 1) * SUBLANE, wc] += _fold8(d_hcs[j])
                    for k in range(k_taps):
                        x_k = plan.tap(cur_ref, before_ref, u_ref, tile_id, r, j, k, c0)
                        acc_ref[k * SUBLANE:(k + 1) * SUBLANE, wc] += _fold8(d_hcs[j] * x_k)
            pps = plan.pre_strips(pp_refs, c0)
            for r in range(plan.n_rs):
                d_us = []
                for j, mb in enumerate(plan.mid_bases):
                    wc = _cols(mb, c0, LANE)
                    acc = None
                    for k in range(k_taps):
                        lo = r * rs + (k_taps - 1) - k
                        term = w_ref[k:k + 1, wc] * g_ref[lo:lo + rs, _cols(j * plan.width, c0, LANE)]
                        acc = term if acc is None else acc + term
                    d_us.append(acc)
                if plan.pre is None:
                    d_xs = d_us
                else:
                    xs = [cur_ref[r * rs:(r + 1) * rs, _cols(b, c0, LANE)].astype(f32) for b in plan.in_bases]
                    _, vjp_pre = jax.vjp(lambda *a: tuple(plan.pre(*a, valid=1.0)), *xs, *pps)
                    grads = vjp_pre(tuple(d_us))
                    d_xs = grads[:len(xs)]
                    n = len(xs)
                    for (_, bases), dp in zip(plan.pre_params, dpp_refs):
                        for pb in bases:
                            dp[:, _cols(pb, c0, LANE)] += grads[n]
                            n += 1
                for b, d_x in zip(plan.in_bases, d_xs):
                    dx_ref[r * rs:(r + 1) * rs, _cols(b, c0, LANE)] = d_x.astype(dx_dtype)
            return carry

        lax.fori_loop(0, plan.n_cs, col_loop, 0)
        carry_ref[...] = g_ref[0:halo, :]

        @pl.when(i == n_tiles - 1)
        def _():
            for k in range(k_taps):
                dw_ref[k:k + 1, :] = jnp.sum(acc_ref[k * SUBLANE:(k + 1) * SUBLANE, :], axis=0, keepdims=True)
            db_ref[...] = jnp.sum(acc_ref[k_taps * SUBLANE:(k_taps + 1) * SUBLANE, :], axis=0, keepdims=True)
            if rider:
                rider.finish(r_srcs, r_outs, r_sems)

    return pl.pallas_call(
        body, name=name, out_shape=out_shape + (rider.out_shapes if rider else []), grid=(n_tiles,),
        in_specs=specs + [HBM_SPEC] * r_n, out_specs=out_specs + [HBM_SPEC] * r_n,
        scratch_shapes=scratch + (rider.semaphores if rider else []), compiler_params=_cparams(("arbitrary",)),
    )(*operands, *(rider.operands if rider else []))


def _f_rms(h, g):
    return (_rms(h, g),)


def _f_rms_res(h, g, bz):
    hh = h + bz
    return _rms(hh, g), hh


@jax.custom_vjp
def _silu_gate(gate, val):
    return jax.nn.silu(gate) * val


def _silu_gate_fwd(gate, val):
    s = jax.nn.sigmoid(gate)
    return gate * s * val, (gate, val, s)


def _silu_gate_bwd(res, d):
    gate, val, s = res
    silu = gate * s
    return d * val * (s + silu * (1.0 - s)), d * silu


_silu_gate.defvjp(_silu_gate_fwd, _silu_gate_bwd)


def _post_ffn_gate(gate, val):
    return (_silu_gate(gate, val),)


def _post_silu(h):
    return (jax.nn.silu(h),)


def _post_identity(h):
    return (h,)


def _pre_glu(g_a, g_b, b_a, b_b, *, valid):
    return ((g_a + b_a) * jax.nn.sigmoid(g_b + b_b) * valid,)


def _f_ssd_dt(dtr, dtb):
    real = lax.broadcasted_iota(jnp.int32, (1, LANE), 1) < SSD_HEADS
    return (jnp.where(real, jax.nn.softplus(dtr + dtb), 0.0),)


def _f_ssd_post(y, z, g):
    return (_rms(y * jax.nn.silu(z), g),)


CONF_HALO = 32
FFN_STRIP_ROWS = 64
CONF_STRIP_ROWS = 128
SSD_STRIP_ROWS = 64


def _f_ln_silu(x, g, b):
    return (jax.nn.silu(_layer_norm(x, g, b)),)


def _f_lru(io_ext, in_b, cw, cb, ga_w, ga_b, gx_w, gx_b, lam, *, valid):
    rows = io_ext.shape[0] - SUBLANE
    io = (io_ext + in_b) * valid
    gate = io[SUBLANE:, :LRU_W]
    xr = _causal_taps(io[:, LRU_W:], cw, SUBLANE, rows) + cb
    rs, iis = [], []
    for blk in range(LRU_W // LRU_BLOCK):
        sl = slice(blk * LRU_BLOCK, (blk + 1) * LRU_BLOCK)
        xb = xr[:, sl]
        rs.append(jax.nn.sigmoid(_dot_nn(xb, ga_w[sl, :]) + ga_b[:, sl]))
        iis.append(jax.nn.sigmoid(_dot_nn(xb, gx_w[sl, :]) + gx_b[:, sl]))
    r = jnp.concatenate(rs, axis=1)
    ig = jnp.concatenate(iis, axis=1)
    log_a = -LRU_C * r * jax.nn.softplus(-lam)
    a = jnp.exp(log_a)
    bterm = jnp.sqrt(-_expm1(2.0 * log_a)) * (ig * xr)
    return a, bterm, jax.nn.gelu(gate)


def _f_sgu(z, in_b, ln_g, ln_b, sp_w, sp_bt):
    rows = z.shape[0]
    zz = jax.nn.gelu(z + in_b)
    u, v = zz[:, :SGU_HALF], zz[:, SGU_HALF:]
    v = _layer_norm(v, ln_g, ln_b)
    tri = lax.broadcasted_iota(jnp.int32, (SGU_CHUNK, SGU_CHUNK), 0) >= lax.broadcasted_iota(
        jnp.int32, (SGU_CHUNK, SGU_CHUNK), 1)
    gdim = SGU_HALF // SGU_GROUPS
    row_blocks = []
    for ci in range(rows // SGU_CHUNK):
        col_blocks = []
        for g in range(SGU_GROUPS):
            w = jnp.where(tri, sp_w[g * SGU_CHUNK:(g + 1) * SGU_CHUNK, :], 0.0)
            vb = v[ci * SGU_CHUNK:(ci + 1) * SGU_CHUNK, g * gdim:(g + 1) * gdim]
            col_blocks.append(_dot_nn(w, vb) + sp_bt[:, g:g + 1])
        row_blocks.append(jnp.concatenate(col_blocks, axis=1))
    mixed = row_blocks[0] if len(row_blocks) == 1 else jnp.concatenate(row_blocks, axis=0)
    return (u * mixed,)


HEADS_PER_GROUP = 4
GROUP_COLS = 256
HEAD_DIM = 64


def _ssd_group(x, bm, cm, dt, st, a_log, dsk, g):
    q = x.shape[0]
    tri = lax.broadcasted_iota(jnp.int32, (q, q), 0) >= lax.broadcasted_iota(jnp.int32, (q, q), 1)
    d_a = dt * (-jnp.exp(a_log))
    acs = jnp.dot(tri.astype(f32), d_a, precision=HIGHEST, preferred_element_type=f32)
    acs_t = acs.T
    lane = lax.broadcasted_iota(jnp.int32, (1, LANE), 1)
    sub = lax.broadcasted_iota(jnp.int32, (LANE, 1), 0)
    col_idx = lax.broadcasted_iota(jnp.int32, (1, GROUP_COLS), 1)
    last_row = (lax.broadcasted_iota(jnp.int32, (q, 1), 0) == q - 1).astype(f32)
    cb = _dot_nt(cm, bm)
    y = jnp.zeros((q, GROUP_COLS), f32)
    e_in = jnp.zeros((q, GROUP_COLS), f32)
    d_end = jnp.zeros((q, GROUP_COLS), f32)
    d_last = jnp.zeros((1, GROUP_COLS), f32)
    d_skip = jnp.zeros((1, GROUP_COLS), f32)
    for j in range(HEADS_PER_GROUP):
        head = HEADS_PER_GROUP * g + j
        on_lane = (lane == head).astype(f32)
        on_sub = (sub == head).astype(f32)
        col = jnp.sum(acs * on_lane, axis=1, keepdims=True)
        row = jnp.sum(acs_t * on_sub, axis=0, keepdims=True)
        dtc = jnp.sum(dt * on_lane, axis=1, keepdims=True)
        last = jnp.sum(col * last_row, axis=0, keepdims=True)
        dsk_j = jnp.sum(dsk * on_lane, axis=1, keepdims=True)
        decay = jnp.where(tri, jnp.exp(jnp.where(tri, col - row, 0.0)), 0.0)
        mine = jnp.logical_and(col_idx >= j * HEAD_DIM, col_idx < (j + 1) * HEAD_DIM)
        y = y + _dot_nn(cb * decay, jnp.where(mine, x * dtc, 0.0))
        e_in = e_in + jnp.where(mine, jnp.exp(col), 0.0)
        d_end = d_end + jnp.where(mine, jnp.exp(last - col) * dtc, 0.0)
        d_last = d_last + jnp.where(mine, jnp.exp(last), 0.0)
        d_skip = d_skip + jnp.where(mine, dsk_j, 0.0)
    y = y + _dot_nn(cm, st) * e_in + x * d_skip
    st_new = st * d_last + _dot_tn(bm, x * d_end)
    return y, st_new


GROUPS_PER_STEP = 2


def _ssd_specs(rev, nc):
    def ch(c):
        return nc - 1 - c if rev else c

    gps = GROUPS_PER_STEP
    x_spec = pl.BlockSpec((SSD_CHUNK, gps * GROUP_COLS), lambda c, g: (ch(c), g))
    b_spec = pl.BlockSpec((SSD_CHUNK, gps * LANE), lambda c, g: (ch(c), SSD_D_INNER // (gps * LANE) + g))
    c_spec = pl.BlockSpec((SSD_CHUNK, gps * LANE), lambda c, g: (ch(c), (SSD_D_INNER + SSD_BC) // (gps * LANE) + g))
    dt_spec = pl.BlockSpec((SSD_CHUNK, LANE), lambda c, g: (ch(c), 0))
    row_spec = pl.BlockSpec((1, LANE), lambda c, g: (0, 0))
    st_spec = pl.BlockSpec((1, gps, LANE, GROUP_COLS), lambda c, g: (ch(c), g, 0, 0))
    wide_spec = pl.BlockSpec((SSD_CHUNK, SSD_CONV_DIM), lambda c, g: (ch(c), 0))
    return x_spec, b_spec, c_spec, dt_spec, row_spec, st_spec, wide_spec


def _ssd_fwd(xc, dt, a_log, dsk, gather=()):
    t_len = xc.shape[0]
    nc = t_len // SSD_CHUNK
    gps = GROUPS_PER_STEP
    n_gp = SSD_GROUPS // gps
    n_g = len(gather)
    x_spec, b_spec, c_spec, dt_spec, row_spec, st_spec, _ = _ssd_specs(False, nc)

    def body(*refs):
        x_ref, b_ref, c_ref, dt_ref, al_ref, dk_ref = refs[:6]
        g_srcs = refs[6:6 + n_g]
        y_ref, st_out_ref = refs[6 + n_g:8 + n_g]
        g_outs = refs[8 + n_g:8 + 2 * n_g]
        st_ref = refs[8 + 2 * n_g]
        g_sems = refs[9 + 2 * n_g:]
        c, gp = pl.program_id(0), pl.program_id(1)
        if n_g:
            @pl.when(jnp.logical_and(c == 0, gp == 0))
            def _():
                _gather_start(g_srcs, g_outs, *g_sems)

        for q in range(gps):
            g = gp * gps + q

            @pl.when(c == 0)
            def _():
                st_ref[g] = jnp.zeros((LANE, GROUP_COLS), f32)

            st = st_ref[g]
            st_out_ref[0, q] = st
            xq = slice(q * GROUP_COLS, (q + 1) * GROUP_COLS)
            bq = slice(q * LANE, (q + 1) * LANE)
            y, st_new = _ssd_group(x_ref[:, xq], b_ref[:, bq], c_ref[:, bq], dt_ref[...], st, al_ref[...],
                                   dk_ref[...], g)
            y_ref[:, xq] = y
            st_ref[g] = st_new

        if n_g:
            @pl.when(jnp.logical_and(c == nc - 1, gp == n_gp - 1))
            def _():
                _gather_finish(g_srcs, g_outs, *g_sems)

    res = pl.pallas_call(
        body, name="ssd_scan_fwd",
        out_shape=[jax.ShapeDtypeStruct((t_len, SSD_D_INNER), f32),
                   jax.ShapeDtypeStruct((nc, SSD_GROUPS, LANE, GROUP_COLS), f32)] + _gather_out_shapes(gather),
        grid=(nc, n_gp), in_specs=[x_spec, b_spec, c_spec, dt_spec, row_spec, row_spec] + [HBM_SPEC] * n_g,
        out_specs=[x_spec, st_spec] + [HBM_SPEC] * n_g,
        scratch_shapes=[pltpu.VMEM((SSD_GROUPS, LANE, GROUP_COLS), f32)] + (_gather_semaphores(n_g) if n_g else []),
        compiler_params=_cparams(("arbitrary", "arbitrary")),
    )(xc, xc, xc, dt, a_log, dsk, *gather)
    return res[0], res[1], list(res[2:])


def _ssd_bwd(xc, dt, a_log, dsk, states, dy, to_chips=()):
    t_len = xc.shape[0]
    nc = t_len // SSD_CHUNK
    gps = GROUPS_PER_STEP
    n_gp = SSD_GROUPS // gps
    n_s = len(to_chips)
    x_spec, b_spec, c_spec, dt_spec, row_spec, st_spec, wide_spec = _ssd_specs(True, nc)

    def body(*refs):
        x_ref, b_ref, c_ref, dt_ref, al_ref, dk_ref, st_in_ref, dy_ref = refs[:8]
        s_srcs = refs[8:8 + n_s]
        dxc_ref, ddt_ref, dal_ref, ddk_ref = refs[8 + n_s:12 + n_s]
        s_outs = refs[12 + n_s:12 + 2 * n_s]
        dst_ref = refs[12 + 2 * n_s]
        s_sems = refs[13 + 2 * n_s:]
        c, gp = pl.program_id(0), pl.program_id(1)

        @pl.when(jnp.logical_and(c == 0, gp == 0))
        def _():
            dal_ref[...] = jnp.zeros_like(dal_ref)
            ddk_ref[...] = jnp.zeros_like(ddk_ref)
            for cp in _to_chips_copies(s_srcs, s_outs, *s_sems) if n_s else []:
                cp.start()

        @pl.when(gp == 0)
        def _():
            ddt_ref[...] = jnp.zeros_like(ddt_ref)

        for q in range(gps):
            g = gp * gps + q

            @pl.when(c == 0)
            def _():
                dst_ref[g] = jnp.zeros((LANE, GROUP_COLS), f32)

            xq = slice(q * GROUP_COLS, (q + 1) * GROUP_COLS)
            bq = slice(q * LANE, (q + 1) * LANE)
            _, vjp = jax.vjp(lambda *args: _ssd_group(*args, g), x_ref[:, xq], b_ref[:, bq], c_ref[:, bq],
                             dt_ref[...], st_in_ref[0, q], al_ref[...], dk_ref[...])
            dx, db, dc, ddt, dst, dal, ddk = vjp((dy_ref[:, xq], dst_ref[g]))
            dxc_ref[:, pl.ds(pl.multiple_of(g * GROUP_COLS, GROUP_COLS), GROUP_COLS)] = dx
            dxc_ref[:, pl.ds(pl.multiple_of(SSD_D_INNER + g * LANE, LANE), LANE)] = db
            dxc_ref[:, pl.ds(pl.multiple_of(SSD_D_INNER + SSD_BC + g * LANE, LANE), LANE)] = dc
            ddt_ref[...] += ddt
            dst_ref[g] = dst
            dal_ref[...] += dal
            ddk_ref[...] += ddk

        if n_s:
            @pl.when(jnp.logical_and(c == nc - 1, gp == n_gp - 1))
            def _():
                for cp in _to_chips_copies(s_srcs, s_outs, *s_sems):
                    cp.wait()

    res = pl.pallas_call(
        body, name="ssd_scan_bwd",
        out_shape=[jax.ShapeDtypeStruct((t_len, SSD_CONV_DIM), f32), jax.ShapeDtypeStruct((t_len, LANE), f32),
                   jax.ShapeDtypeStruct((1, LANE), f32), jax.ShapeDtypeStruct((1, LANE), f32)]
        + _to_chips_out_shapes(to_chips),
        grid=(nc, n_gp),
        in_specs=[x_spec, b_spec, c_spec, dt_spec, row_spec, row_spec, st_spec, x_spec] + [HBM_SPEC] * n_s,
        out_specs=[wide_spec, dt_spec, row_spec, row_spec] + [HBM_SPEC] * n_s,
        scratch_shapes=[pltpu.VMEM((SSD_GROUPS, LANE, GROUP_COLS), f32)] + (_to_chips_semaphores(n_s) if n_s else []),
        compiler_params=_cparams(("arbitrary", "arbitrary")),
    )(xc, xc, xc, dt, a_log, dsk, states, dy, *to_chips)
    return res[0], res[1], res[2], res[3], list(res[4:])


LRU_ROWS = 256


def _lru_fwd(a, b, gg):
    t_len, cols = a.shape
    rows = min(LRU_ROWS, t_len)
    spec = pl.BlockSpec((rows, cols), lambda i: (i, 0))

    def body(a_ref, b_ref, g_ref, y_ref, h_ref, carry):
        i = pl.program_id(0)

        @pl.when(i == 0)
        def _():
            carry[...] = jnp.zeros_like(carry)

        av, bv = a_ref[...], b_ref[...]
        row = lax.broadcasted_iota(jnp.int32, av.shape, 0)
        s = 1
        while s < rows:
            a_prev = pltpu.roll(av, s, axis=0)
            b_prev = pltpu.roll(bv, s, axis=0)
            m = row >= s
            bv = jnp.where(m, av * b_prev + bv, bv)
            av = jnp.where(m, av * a_prev, av)
            s *= 2
        h = av * carry[0:1, :] + bv
        h_ref[...] = h
        y_ref[...] = g_ref[...] * h
        carry[0:1, :] = h[rows - 1:rows, :]

    return pl.pallas_call(
        body, name="lru_scan_fwd",
        out_shape=[jax.ShapeDtypeStruct((t_len, cols), f32), jax.ShapeDtypeStruct((t_len, cols), f32)],
        grid=(t_len // rows,), in_specs=[spec, spec, spec], out_specs=[spec, spec],
        scratch_shapes=[pltpu.VMEM((SUBLANE, cols), f32)],
        compiler_params=_cparams(("arbitrary",)),
    )(a, b, gg)


def _lru_bwd(dy, gg, a, h):
    t_len, cols = a.shape
    rows = min(LRU_ROWS, t_len)
    n_tiles = t_len // rows
    per = rows // SUBLANE
    spec = pl.BlockSpec((rows, cols), lambda i: (n_tiles - 1 - i, 0))
    prev_spec = pl.BlockSpec((SUBLANE, cols), lambda i: (jnp.maximum((n_tiles - 1 - i) * per - 1, 0), 0))

    def body(dy_ref, g_ref, a_ref, h_ref, hp_ref, da_ref, db_ref, dg_ref, carry_dh, carry_a):
        i = pl.program_id(0)
        tile_id = n_tiles - 1 - i

        @pl.when(i == 0)
        def _():
            carry_dh[...] = jnp.zeros_like(carry_dh)
            carry_a[...] = jnp.zeros_like(carry_a)

        av, hv, dyv = a_ref[...], h_ref[...], dy_ref[...]
        row = lax.broadcasted_iota(jnp.int32, av.shape, 0)
        dg_ref[...] = dyv * hv
        bv = dyv * g_ref[...]
        cv = jnp.where(row == rows - 1, carry_a[0:1, :], pltpu.roll(av, rows - 1, axis=0))
        s = 1
        while s < rows:
            c_next = pltpu.roll(cv, rows - s, axis=0)
            b_next = pltpu.roll(bv, rows - s, axis=0)
            m = row < rows - s
            bv = jnp.where(m, cv * b_next + bv, bv)
            cv = jnp.where(m, cv * c_next, cv)
            s *= 2
        dh = cv * carry_dh[0:1, :] + bv
        h_before = jnp.where(tile_id > 0, hp_ref[SUBLANE - 1:SUBLANE, :], jnp.zeros((1, cols), f32))
        h_prev = jnp.where(row == 0, h_before, pltpu.roll(hv, 1, axis=0))
        da_ref[...] = dh * h_prev
        db_ref[...] = dh
        carry_dh[0:1, :] = dh[0:1, :]
        carry_a[0:1, :] = av[0:1, :]

    return pl.pallas_call(
        body, name="lru_scan_bwd",
        out_shape=[jax.ShapeDtypeStruct((t_len, cols), f32)] * 3,
        grid=(n_tiles,), in_specs=[spec, spec, spec, spec, prev_spec], out_specs=[spec, spec, spec],
        scratch_shapes=[pltpu.VMEM((SUBLANE, cols), f32), pltpu.VMEM((SUBLANE, cols), f32)],
        compiler_params=_cparams(("arbitrary",)),
    )(dy, gg, a, h, h)


def _loss_head(h, target, g):
    t_len = h.shape[0]
    rows = min(512, t_len)

    def f(hv, gv, tv):
        err = _rms(hv, gv) - tv
        return 0.5 * jnp.sum(jnp.mean(err * err, axis=-1, keepdims=True), axis=0, keepdims=True)

    def body(h_ref, t_ref, g_ref, dh_ref, dg_ref, loss_ref):
        i = pl.program_id(0)

        @pl.when(i == 0)
        def _():
            dg_ref[...] = jnp.zeros_like(dg_ref)
            loss_ref[...] = jnp.zeros_like(loss_ref)

        tv = t_ref[...]
        part, vjp = jax.vjp(lambda hv, gv: f(hv, gv, tv), h_ref[...], g_ref[...])
        dh, dg = vjp(jnp.ones((1, 1), f32))
        dh_ref[...] = dh
        dg_ref[...] += dg
        loss_ref[...] += jnp.broadcast_to(part, loss_ref.shape)

    spec = pl.BlockSpec((rows, D_MODEL), lambda i: (i, 0))
    return pl.pallas_call(
        body, name="loss_head",
        out_shape=[jax.ShapeDtypeStruct((t_len, D_MODEL), f32), jax.ShapeDtypeStruct((1, D_MODEL), f32),
                   jax.ShapeDtypeStruct((1, LANE), f32)],
        grid=(t_len // rows,), in_specs=[spec, spec, pl.BlockSpec((1, D_MODEL), lambda i: (0, 0))],
        out_specs=[spec, pl.BlockSpec((1, D_MODEL), lambda i: (0, 0)), pl.BlockSpec((1, LANE), lambda i: (0, 0))],
        compiler_params=_cparams(("arbitrary",)),
    )(h, target, g)


def _as2d(a):
    return a.reshape((-1, a.shape[-1])) if a.ndim > 1 else a.reshape((1, -1))


def _row_block(rows, cols, bytes_cap=1 << 20):
    if rows * cols * 4 <= bytes_cap or rows % SUBLANE:
        return rows
    return _tile(rows, max(SUBLANE, (bytes_cap // (cols * 4)) // SUBLANE * SUBLANE), SUBLANE)


def _adamw(w, g, m, v, name):
    shape = w.shape
    w2, g2, m2, v2 = _as2d(w), _as2d(g), _as2d(m), _as2d(v)
    rows, cols = w2.shape
    rb = _row_block(rows, cols)

    def body(w_ref, g_ref, m_ref, v_ref, d_ref, nm_ref, nv_ref):
        gv = g_ref[...]
        nm = ADAM_B1 * m_ref[...] + (1.0 - ADAM_B1) * gv
        nv = ADAM_B2 * v_ref[...] + (1.0 - ADAM_B2) * jnp.square(gv)
        m_hat = nm / (1.0 - ADAM_B1 ** ADAM_STEP)
        v_hat = nv / (1.0 - ADAM_B2 ** ADAM_STEP)
        d_ref[...] = -ADAM_LR * (m_hat / (jnp.sqrt(v_hat) + ADAM_EPS) + ADAM_WD * w_ref[...])
        nm_ref[...] = nm
        nv_ref[...] = nv

    spec = pl.BlockSpec((rb, cols), lambda i: (i, 0))
    d, nm, nv = pl.pallas_call(
        body, name=name, out_shape=[jax.ShapeDtypeStruct((rows, cols), f32)] * 3,
        grid=(rows // rb,), in_specs=[spec] * 4, out_specs=[spec] * 3,
        compiler_params=_cparams(("parallel",)),
    )(w2, g2, m2, v2)
    return d.reshape(shape), nm.reshape(shape), nv.reshape(shape)


def _sum_with_sibling(g_halves, theirs, c_idx):
    n_sh, _, rows, cols = g_halves.shape
    rb = _tile(rows, 512, 2 * SUBLANE)

    def body(c_ref, mine_ref, theirs_ref, o_ref):
        o_ref[...] = (mine_ref[...] + theirs_ref[...]).astype(bf16)

    grid_spec = pltpu.PrefetchScalarGridSpec(
        num_scalar_prefetch=1, grid=(n_sh, rows // rb),
        in_specs=[pl.BlockSpec((None, None, rb, cols), lambda k, i, c_ref: (k, c_ref[0], i, 0)),
                  pl.BlockSpec((None, rb, cols), lambda k, i, c_ref: (k, i, 0))],
        out_specs=pl.BlockSpec((None, rb, cols), lambda k, i, c_ref: (k, i, 0)))
    return pl.pallas_call(
        body, name="grad_sum_sibling", out_shape=jax.ShapeDtypeStruct((n_sh, rows, cols), bf16),
        grid_spec=grid_spec, compiler_params=_cparams(("parallel", "parallel")),
    )(c_idx, g_halves, theirs)


def _sum_chips(partial, received, k_idx):
    _, rows, cols = partial.shape
    rb = _tile(rows, 512, 2 * SUBLANE)

    def body(k_ref, mine_ref, r_ref, o_ref):
        acc = mine_ref[...].astype(f32)
        for j in range(N_CHIPS - 1):
            acc = acc + r_ref[j].astype(f32)
        o_ref[...] = acc

    grid_spec = pltpu.PrefetchScalarGridSpec(
        num_scalar_prefetch=1, grid=(rows // rb,),
        in_specs=[pl.BlockSpec((None, rb, cols), lambda i, k_ref: (k_ref[0], i, 0)),
                  pl.BlockSpec((N_CHIPS - 1, rb, cols), lambda i, k_ref: (0, i, 0))],
        out_specs=pl.BlockSpec((rb, cols), lambda i, k_ref: (i, 0)))
    return pl.pallas_call(
        body, name="grad_sum_chips", out_shape=jax.ShapeDtypeStruct((rows, cols), f32),
        grid_spec=grid_spec, compiler_params=_cparams(("parallel",)),
    )(k_idx, partial, received)


HBM_SPEC = pl.BlockSpec(memory_space=pltpu.HBM)
CHIP_FLIPS = ((0, 1), (1, 0), (1, 1))


def _position():
    return lax.axis_index("x"), lax.axis_index("y"), lax.axis_index("c")


def _own_slot(gathered, mine, index):
    return [lax.dynamic_update_index_in_dim(g, m, index, 0) for g, m in zip(gathered, mine)]


def _gather_weights(blocks):
    n = len(blocks)

    def body(*refs):
        srcs, outs = refs[:n], refs[n:2 * n]
        send_sems, recv_sems = refs[2 * n:]
        _gather_start(srcs, outs, send_sems, recv_sems)
        _gather_finish(srcs, outs, send_sems, recv_sems)

    return pl.pallas_call(
        body, name="gather_weights", out_shape=_gather_out_shapes(blocks),
        in_specs=[HBM_SPEC] * n, out_specs=[HBM_SPEC] * n, scratch_shapes=_gather_semaphores(n),
    )(*blocks)


def _gather_out_shapes(blocks):
    return [jax.ShapeDtypeStruct((N_CHIPS,) + b.shape, b.dtype) for b in blocks]


def _gather_semaphores(n):
    n_sem = 2 * len(CHIP_FLIPS) * n
    return [pltpu.SemaphoreType.DMA((n_sem,)), pltpu.SemaphoreType.DMA((n_sem,))]


def _gather_copies(srcs, outs, send_sems, recv_sems):
    n_far = len(CHIP_FLIPS)
    x, y, c = _position()
    k = 2 * x + y
    first, passed = [], []
    for a in range(len(srcs)):
        for j, (fx, fy) in enumerate(CHIP_FLIPS):
            s = a * 2 * n_far + j
            kk = 2 * (x ^ fx) + (y ^ fy)
            first.append(pltpu.make_async_remote_copy(
                src_ref=srcs[a].at[c], dst_ref=outs[a].at[k, c], send_sem=send_sems.at[s],
                recv_sem=recv_sems.at[s], device_id=(x ^ fx, y ^ fy, c), device_id_type=MESH))
            passed.append(pltpu.make_async_remote_copy(
                src_ref=outs[a].at[kk, c], dst_ref=outs[a].at[kk, c], send_sem=send_sems.at[s + n_far],
                recv_sem=recv_sems.at[s + n_far], device_id=(x, y, 1 - c), device_id_type=MESH))
    return first, passed


def _gather_start(srcs, outs, send_sems, recv_sems):
    first, _ = _gather_copies(srcs, outs, send_sems, recv_sems)
    for cp in first:
        cp.start()


def _gather_finish(srcs, outs, send_sems, recv_sems):
    first, passed = _gather_copies(srcs, outs, send_sems, recv_sems)
    for arrived, onward in zip(first, passed):
        arrived.wait_recv()
        onward.start()
    for cp in passed:
        cp.wait_recv()
    for cp in first + passed:
        cp.wait_send()


def _swap_with_sibling(grads):
    rider = _swap_rider(grads)
    n = len(grads)

    def body(*refs):
        rider.start(refs[:n], refs[n:2 * n], refs[2 * n:])
        rider.finish(refs[:n], refs[n:2 * n], refs[2 * n:])

    return pl.pallas_call(
        body, name="grad_swap_sibling", out_shape=rider.out_shapes,
        in_specs=[HBM_SPEC] * n, out_specs=[HBM_SPEC] * n, scratch_shapes=rider.semaphores,
    )(*grads)


class _Rider:
    def __init__(self, operands, out_shapes, semaphores, start, finish):
        self.operands, self.out_shapes, self.semaphores = list(operands), list(out_shapes), list(semaphores)
        self.start, self.finish = start, finish
        self.n = len(self.operands)


def _swap_copies(srcs, outs, send_sems, recv_sems):
    x, y, c = _position()
    copies = []
    for a in range(len(srcs)):
        for kk in range(N_CHIPS):
            s = a * N_CHIPS + kk
            copies.append(pltpu.make_async_remote_copy(
                src_ref=srcs[a].at[kk, 1 - c], dst_ref=outs[a].at[kk], send_sem=send_sems.at[s],
                recv_sem=recv_sems.at[s], device_id=(x, y, 1 - c), device_id_type=MESH))
    return copies


def _swap_rider(grads):
    n_sem = N_CHIPS * len(grads)

    def start(srcs, outs, sems):
        for cp in _swap_copies(srcs, outs, *sems):
            cp.start()

    def finish(srcs, outs, sems):
        for cp in _swap_copies(srcs, outs, *sems):
            cp.wait()

    return _Rider(grads, [jax.ShapeDtypeStruct((N_CHIPS,) + g.shape[2:], g.dtype) for g in grads],
                  [pltpu.SemaphoreType.DMA((n_sem,)), pltpu.SemaphoreType.DMA((n_sem,))], start, finish)


def _gather_rider(blocks):
    def start(srcs, outs, sems):
        _gather_start(srcs, outs, *sems)

    def finish(srcs, outs, sems):
        _gather_finish(srcs, outs, *sems)

    return _Rider(blocks, _gather_out_shapes(blocks), _gather_semaphores(len(blocks)), start, finish)


def _send_to_chips(partials):
    n = len(partials)

    def body(*refs):
        srcs, outs = refs[:n], refs[n:2 * n]
        send_sems, recv_sems = refs[2 * n:]
        for cp in _to_chips_copies(srcs, outs, send_sems, recv_sems):
            cp.start()
        for cp in _to_chips_copies(srcs, outs, send_sems, recv_sems):
            cp.wait()

    return pl.pallas_call(
        body, name="grad_to_chips", out_shape=_to_chips_out_shapes(partials),
        in_specs=[HBM_SPEC] * n, out_specs=[HBM_SPEC] * n, scratch_shapes=_to_chips_semaphores(n),
    )(*partials)


def _to_chips_out_shapes(partials):
    return [jax.ShapeDtypeStruct((len(CHIP_FLIPS),) + p.shape[1:], p.dtype) for p in partials]


def _to_chips_semaphores(n):
    n_sem = len(CHIP_FLIPS) * n
    return [pltpu.SemaphoreType.DMA((n_sem,)), pltpu.SemaphoreType.DMA((n_sem,))]


def _to_chips_copies(srcs, outs, send_sems, recv_sems):
    n_far = len(CHIP_FLIPS)
    x, y, c = _position()
    copies = []
    for a in range(len(srcs)):
        for j, (fx, fy) in enumerate(CHIP_FLIPS):
            s = a * n_far + j
            kk = 2 * (x ^ fx) + (y ^ fy)
            copies.append(pltpu.make_async_remote_copy(
                src_ref=srcs[a].at[kk], dst_ref=outs[a].at[j], send_sem=send_sems.at[s],
                recv_sem=recv_sems.at[s], device_id=(x ^ fx, y ^ fy, c), device_id_type=MESH))
    return copies


def _join_halves(halves):
    n = len(halves)

    def body(*refs):
        srcs, outs = refs[:n], refs[n:2 * n]
        send_sems, recv_sems = refs[2 * n:]
        x, y, c = _position()
        copies = []
        for a in range(n):
            cp = pltpu.make_async_remote_copy(
                src_ref=srcs[a], dst_ref=outs[a].at[c], send_sem=send_sems.at[a], recv_sem=recv_sems.at[a],
                device_id=(x, y, 1 - c), device_id_type=MESH)
            cp.start()
            copies.append(cp)
        for cp in copies:
            cp.wait()

    return pl.pallas_call(
        body, name="grad_join_halves",
        out_shape=[jax.ShapeDtypeStruct((2,) + h.shape, h.dtype) for h in halves],
        in_specs=[HBM_SPEC] * n, out_specs=[HBM_SPEC] * n,
        scratch_shapes=[pltpu.SemaphoreType.DMA((n,)), pltpu.SemaphoreType.DMA((n,))],
    )(*halves)


def _all_sum_small(vec):
    rows, cols = vec.shape

    def body(v_ref, o_ref, buf, send_sems, recv_sems):
        x, y, c = _position()
        me = 4 * x + 2 * y + c
        buf[me] = v_ref[...]
        copies = []
        for m in range(1, N_DEV):
            fx, fy, fc = (m >> 2) & 1, (m >> 1) & 1, m & 1
            cp = pltpu.make_async_remote_copy(
                src_ref=v_ref, dst_ref=buf.at[me], send_sem=send_sems.at[m - 1], recv_sem=recv_sems.at[m - 1],
                device_id=(x ^ fx, y ^ fy, c ^ fc), device_id_type=MESH)
            cp.start()
            copies.append(cp)
        for cp in copies:
            cp.wait()
        acc = buf[0]
        for d in range(1, N_DEV):
            acc = acc + buf[d]
        o_ref[...] = acc

    return pl.pallas_call(
        body, name="all_sum_small", out_shape=jax.ShapeDtypeStruct((rows, cols), f32),
        in_specs=[pl.BlockSpec(memory_space=pltpu.VMEM)], out_specs=pl.BlockSpec(memory_space=pltpu.VMEM),
        scratch_shapes=[pltpu.VMEM((N_DEV, rows, cols), f32), pltpu.SemaphoreType.DMA((N_DEV - 1,)),
                        pltpu.SemaphoreType.DMA((N_DEV - 1,))],
        compiler_params=_cparams(),
    )(vec)


FLAT_QUANTUM = 2 * 2 * SUBLANE * FLAT_COLS


def _pack(arrays, dtype):
    flat = jnp.concatenate([a.astype(dtype).reshape(-1) for a in arrays])
    n = flat.shape[0]
    n_pad = -(-n // FLAT_QUANTUM) * FLAT_QUANTUM
    return jnp.pad(flat, (0, n_pad - n))


def _unpack(flat, shapes):
    out, off = [], 0
    for s in shapes:
        n = int(np.prod(s))
        out.append(flat[..., off:off + n].reshape(flat.shape[:-1] + tuple(s)))
        off += n
    return out


def _full_from_shards(stacked, axis):
    return jnp.concatenate([stacked[k] for k in range(N_CHIPS)], axis=axis)


def _shards_of(full, axis):
    return jnp.stack(jnp.split(full, N_CHIPS, axis=axis))


def _ffn_fwd(h, p):
    u = _row_fwd(_f_rms, [h], [p['g']], [(D_MODEL, bf16)], rows=512, name="ffn_norm")[0]
    a = _mm_w(u, p['up'], 'nn', "ffn_up")
    gated = _conv_fwd(_ffn_conv_plan(a, p), [(FFN_H, bf16, (0,))], "ffn_gate")[0]
    h_out = _mm(gated, p['down'], 'nn', "ffn_down", add=h)
    return h_out, (h, u, a, gated)


def _ffn_conv_plan(a, p):
    both = (0, FFN_H)
    return _ConvPlan(a, p['cw'], p['cb'], in_bases=both, mid_bases=both, width=FFN_H, rows=256, rs=FFN_STRIP_ROWS,
                     post=_post_ffn_gate)


def _ffn_bwd(dh_out, p, saved, bias_zero, make_rider=None):
    h, u, a, gated = saved
    d_gated = _mm(dh_out, p['down'], 'nt', "ffn_down_dx", out_dtype=bf16)
    d_down = _mm(gated, dh_out, 'tn', "ffn_down_dw")
    rider = make_rider(d_down) if make_rider else None
    res = _conv_bwd(_ffn_conv_plan(a, p), [(d_gated, (0,))], bf16, "ffn_gate_bwd", rider=rider)
    da, d_cw, d_cb = res[:3]
    d_up = _mm(u, da, 'tn', "ffn_up_dw", out_cols_sharded=True)
    du = _mm_w(da, p['up'], 'nt', "ffn_up_dx", out_dtype=bf16)
    dh, d_g, d_bias = _row_bwd(_f_rms_res, [h], [p['g'], bias_zero], [du, dh_out], rows=512, name="ffn_norm_bwd")
    return dh, {'g': d_g, 'up': d_up, 'down': d_down, 'cw': d_cw, 'cb': d_cb}, d_bias, list(res[3:])


def _mixer_norm_bwd(h, g, du, dh_res, name):
    def f(hv, gv):
        return _rms(hv, gv), hv

    dh, d_g = _row_bwd(f, [h], [g], [du, dh_res], rows=512, name=name)
    return dh, d_g


def _ssd_layer_fwd(h, p, gather=()):
    u = _row_fwd(_f_rms, [h], [p['g']], [(D_MODEL, bf16)], rows=512, name="ssd_norm")[0]
    z = _mm(u, p['w_z'], 'nn', "ssd_in_z")
    xbc = _mm(u, p['w_xbc'], 'nn', "ssd_in_xbc")
    dtr = _mm(u, p['w_dt'], 'nn', "ssd_in_dt")
    xc = _conv_fwd(_ssd_conv_plan(xbc, p), [(SSD_CONV_DIM, f32, (0,))], "ssd_conv")[0]
    dt = _row_fwd(_f_ssd_dt, [dtr], [p['dtb']], [(LANE, f32)], rows=1024, name="ssd_dt")[0]
    y, states, gathered = _ssd_fwd(xc, dt, p['a_log'], p['dsk'], gather)
    yn = _row_fwd(_f_ssd_post, [y, z], [p['norm']], [(SSD_D_INNER, bf16)], rows=256, name="ssd_gate_norm")[0]
    h_out = _mm(yn, p['out'], 'nn', "ssd_out", add=h)
    return h_out, (h, u, z, xbc, dtr, xc, dt, states, y, yn), gathered


def _ssd_conv_plan(xbc, p):
    return _ConvPlan(xbc, p['cw'], p['cb'], in_bases=(0,), mid_bases=(0,), width=SSD_CONV_DIM, rows=256,
                     rs=SSD_STRIP_ROWS, post=_post_silu)


def _ssd_layer_bwd(dh_out, p, saved, to_chips=()):
    h, u, z, xbc, dtr, xc, dt, states, y, yn = saved
    d_yn = _mm(dh_out, p['out'], 'nt', "ssd_out_dx", out_dtype=bf16)
    d_out = _mm(yn, dh_out, 'tn', "ssd_out_dw")
    dy, dz, d_norm = _row_bwd(_f_ssd_post, [y, z], [p['norm']], [d_yn], rows=256, name="ssd_gate_norm_bwd",
                              tile_dtypes=[f32, bf16])
    dxc, ddt, d_alog, d_dsk, received = _ssd_bwd(xc, dt, p['a_log'], p['dsk'], states, dy, to_chips)
    dxbc, d_cw, d_cb = _conv_bwd(_ssd_conv_plan(xbc, p), [(dxc, (0,))], bf16, "ssd_conv_bwd")
    ddtr, d_dtb = _row_bwd(_f_ssd_dt, [dtr], [p['dtb']], [ddt], rows=1024, name="ssd_dt_bwd", tile_dtypes=[bf16])
    d_wz = _mm(u, dz, 'tn', "ssd_in_z_dw")
    d_wxbc = _mm(u, dxbc, 'tn', "ssd_in_xbc_dw")
    d_wdt = _mm(u, ddtr, 'tn', "ssd_in_dt_dw")
    du = _mm(dz, p['w_z'], 'nt', "ssd_in_z_dx")
    du = _mm(dxbc, p['w_xbc'], 'nt', "ssd_in_xbc_dx", add=du)
    du = _mm(ddtr, p['w_dt'], 'nt', "ssd_in_dt_dx", add=du, out_dtype=bf16)
    dh, d_g = _mixer_norm_bwd(h, p['g'], du, dh_out, "ssd_norm_bwd")
    grads = {'g': d_g, 'w_z': d_wz, 'w_xbc': d_wxbc, 'w_dt': d_wdt, 'cw': d_cw, 'cb': d_cb, 'dtb': d_dtb,
             'a_log': d_alog, 'dsk': d_dsk, 'norm': d_norm, 'out': d_out}
    return dh, grads, received


def _conf_layer_fwd(h, p, rider=None):
    u = _row_fwd(_f_rms, [h], [p['g']], [(D_MODEL, bf16)], rows=512, name="conf_norm")[0]
    g2 = _mm_w(u, p['pw1'], 'nn', "conf_pw1")
    res = _conv_fwd(_conf_conv_plan(g2, p), [(D_MODEL, f32, (0,))], "conf_conv", rider=rider)
    conv = res[0]
    s = _row_fwd(_f_ln_silu, [conv], [p['ln_g'], p['ln_b']], [(D_MODEL, bf16)], rows=256, name="conf_ln")[0]
    h_out = _mm(s, p['pw2'], 'nn', "conf_pw2", bias=p['b2'], add=h)
    return h_out, (h, u, g2, conv, s), list(res[1:])


def _conf_conv_plan(g2, p):
    halves = (0, D_MODEL)
    return _ConvPlan(g2, p['dw_w'], p['dw_b'], in_bases=halves, mid_bases=(0,), width=D_MODEL, rows=256,
                     rs=CONF_STRIP_ROWS, pre=_pre_glu, pre_params=[(p['b1'], halves)], post=_post_identity)


def _conf_layer_bwd(dh_out, p, saved):
    h, u, g2, conv, s = saved
    ds = _mm(dh_out, p['pw2'], 'nt', "conf_pw2_dx", out_dtype=bf16)
    d_pw2 = _mm(s, dh_out, 'tn', "conf_pw2_dw")
    d_conv, d_lng, d_lnb = _row_bwd(_f_ln_silu, [conv], [p['ln_g'], p['ln_b']], [ds], rows=256, name="conf_ln_bwd")
    dg2, d_dww, d_dwb, d_b1 = _conv_bwd(_conf_conv_plan(g2, p), [(d_conv, (0,))], bf16, "conf_conv_bwd")
    d_pw1 = _mm(u, dg2, 'tn', "conf_pw1_dw", out_cols_sharded=True)
    du = _mm_w(dg2, p['pw1'], 'nt', "conf_pw1_dx", out_dtype=bf16)
    dh, d_g = _mixer_norm_bwd(h, p['g'], du, dh_out, "conf_norm_bwd")
    grads = {'g': d_g, 'pw1': d_pw1, 'b1': d_b1, 'dw_w': d_dww, 'dw_b': d_dwb, 'ln_g': d_lng, 'ln_b': d_lnb,
             'pw2': d_pw2}
    return dh, grads


def _lru_params(p):
    return [p['in_b'], p['cw'], p['cb'], p['ga_w'], p['ga_b'], p['gx_w'], p['gx_b'], p['lam']]


def _lru_layer_fwd(h, p):
    u = _row_fwd(_f_rms, [h], [p['g']], [(D_MODEL, bf16)], rows=512, name="lru_norm")[0]
    io = _mm_w(u, p['in_w'], 'nn', "lru_in")
    a, b, gg = _row_fwd(_f_lru, [io], _lru_params(p), [(LRU_W, f32)] * 3, rows=256, name="lru_gates",
                        halo=SUBLANE, halo_of=[True])
    y, hs = _lru_fwd(a, b, gg)
    h_out = _mm(y, p['out'], 'nn', "lru_out", bias=p['out_b'], add=h)
    return h_out, (h, u, io, a, gg, hs, y)


def _lru_layer_bwd(dh_out, p, saved):
    h, u, io, a, gg, hs, y = saved
    dy = _mm(dh_out, p['out'], 'nt', "lru_out_dx")
    d_out = _mm(y, dh_out, 'tn', "lru_out_dw")
    da, db, dgg = _lru_bwd(dy, gg, a, hs)
    res = _row_bwd(_f_lru, [io], _lru_params(p), [da, db, dgg], rows=256, name="lru_gates_bwd",
                   halo=SUBLANE, halo_of=[True], tile_dtypes=[bf16])
    dio, d_inb, d_cw, d_cb, d_gaw, d_gab, d_gxw, d_gxb, d_lam = res
    d_inw = _mm(u, dio, 'tn', "lru_in_dw", out_cols_sharded=True)
    du = _mm_w(dio, p['in_w'], 'nt', "lru_in_dx", out_dtype=bf16)
    dh, d_g = _mixer_norm_bwd(h, p['g'], du, dh_out, "lru_norm_bwd")
    grads = {'g': d_g, 'in_w': d_inw, 'in_b': d_inb, 'cw': d_cw, 'cb': d_cb, 'ga_w': d_gaw, 'ga_b': d_gab,
             'gx_w': d_gxw, 'gx_b': d_gxb, 'lam': d_lam, 'out': d_out}
    return dh, grads


def _sgu_params(p):
    return [p['in_b'], p['ln_g'], p['ln_b'], p['sp_w'], p['sp_bt']]


def _sgu_layer_fwd(h, p):
    u = _row_fwd(_f_rms, [h], [p['g']], [(D_MODEL, bf16)], rows=512, name="sgu_norm")[0]
    z = _mm_w(u, p['in_w'], 'nn', "sgu_in")
    s = _row_fwd(_f_sgu, [z], _sgu_params(p), [(SGU_HALF, bf16)], rows=SGU_CHUNK, name="sgu_mix")[0]
    h_out = _mm(s, p['out'], 'nn', "sgu_out", bias=p['out_b'], add=h)
    return h_out, (h, u, z, s)


def _sgu_layer_bwd(dh_out, p, saved):
    h, u, z, s = saved
    ds = _mm(dh_out, p['out'], 'nt', "sgu_out_dx", out_dtype=bf16)
    d_out = _mm(s, dh_out, 'tn', "sgu_out_dw")
    dz, d_inb, d_lng, d_lnb, d_spw, d_spbt = _row_bwd(_f_sgu, [z], _sgu_params(p), [ds], rows=SGU_CHUNK,
                                                      name="sgu_mix_bwd", tile_dtypes=[bf16])
    d_inw = _mm(u, dz, 'tn', "sgu_in_dw", out_cols_sharded=True)
    du = _mm_w(dz, p['in_w'], 'nt', "sgu_in_dx", out_dtype=bf16)
    dh, d_g = _mixer_norm_bwd(h, p['g'], du, dh_out, "sgu_norm_bwd")
    grads = {'g': d_g, 'in_w': d_inw, 'in_b': d_inb, 'ln_g': d_lng, 'ln_b': d_lnb, 'sp_w': d_spw, 'sp_bt': d_spbt,
             'out': d_out}
    return dh, grads


def _row(v):
    return v.reshape((1, -1)).astype(f32)


def _pad_lanes(v, n=LANE):
    v = _row(v)
    return jnp.pad(v, ((0, 0), (0, n - v.shape[1])))


def _local_step(x, target, w, comm=None):
    a_in = w['a_in_proj'][0]
    pa = {'g': _row(w['norm_mix'][0]), 'w_z': a_in[:, :SSD_D_INNER],
          'w_xbc': a_in[:, SSD_D_INNER:SSD_D_INNER + SSD_CONV_DIM],
          'w_dt': jnp.pad(a_in[:, SSD_D_INNER + SSD_CONV_DIM:], ((0, 0), (0, LANE - SSD_HEADS))),
          'cw': w['a_conv_w'][0].astype(f32), 'cb': _row(w['a_conv_b'][0]), 'dtb': _pad_lanes(w['a_dt_bias'][0]),
          'a_log': _pad_lanes(w['a_log'][0]), 'dsk': _pad_lanes(w['a_d_skip'][0]), 'norm': _row(w['a_norm'][0]),
          'out': w['a_out_proj']}
    h, s_mix0, gathered = _ssd_layer_fwd(x, pa, gather=comm.late_blocks if comm else ())
    if comm:
        w = {**w, **comm.late_weights(gathered)}
    ffn = [{'g': _row(w['norm_ffn'][i]), 'up': (w['f_up_w'], i), 'down': w['f_down_w'][i],
            'cw': w['f_conv_w'][i].astype(f32), 'cb': _row(w['f_conv_b'][i])} for i in range(DEPTH)]
    pb = {'g': _row(w['norm_mix'][1]), 'pw1': (w['b_pw1_w'], 0), 'b1': _row(w['b_pw1_b'][0]),
          'dw_w': w['b_dw_w'][0].astype(f32), 'dw_b': _row(w['b_dw_b'][0]), 'ln_g': _row(w['b_ln_g'][0]),
          'ln_b': _row(w['b_ln_b'][0]), 'pw2': w['b_pw2_w'], 'b2': _row(w['b_pw2_b'][0])}
    h, s_ffn0 = _ffn_fwd(h, ffn[0])
    h, s_mix1, gathered = _conf_layer_fwd(h, pb, rider=comm.second_rider() if comm else None)
    if comm:
        w = {**w, **comm.second_weights(gathered)}
    pc = {'g': _row(w['norm_mix'][2]), 'in_w': (w['c_in_w'], 0), 'in_b': _row(w['c_in_b'][0]),
          'cw': w['c_conv_w'][0].astype(f32), 'cb': _row(w['c_conv_b'][0]),
          'ga_w': w['c_ga_w'][0].reshape(LRU_W, LRU_BLOCK).astype(f32), 'ga_b': _row(w['c_ga_b'][0]),
          'gx_w': w['c_gx_w'][0].reshape(LRU_W, LRU_BLOCK).astype(f32), 'gx_b': _row(w['c_gx_b'][0]),
          'lam': _row(w['c_lambda'][0]), 'out': w['c_out_w'], 'out_b': _row(w['c_out_b'][0])}
    pd = {'g': _row(w['norm_mix'][3]), 'in_w': (w['d_in_w'], 0), 'in_b': _row(w['d_in_b'][0]),
          'ln_g': _row(w['d_ln_g'][0]), 'ln_b': _row(w['d_ln_b'][0]),
          'sp_w': w['d_sp_w'][0].reshape(SGU_GROUPS * SGU_CHUNK, SGU_CHUNK).astype(f32),
          'sp_bt': w['d_sp_b'][0].astype(f32).T, 'out': w['d_out_w'], 'out_b': _row(w['d_out_b'][0])}
    mixers = [(None, None, pa), (None, _conf_layer_bwd, pb),
              (_lru_layer_fwd, _lru_layer_bwd, pc), (_sgu_layer_fwd, _sgu_layer_bwd, pd)]

    h, s_ffn1 = _ffn_fwd(h, ffn[1])
    saved = [(s_mix0, s_ffn0), (s_mix1, s_ffn1)]
    for i in range(2, DEPTH):
        fwd, _, p = mixers[i]
        h, s_mix = fwd(h, p)
        h, s_ffn = _ffn_fwd(h, ffn[i])
        saved.append((s_mix, s_ffn))
    dh, d_final, loss = _loss_head(h, target, _row(w['norm_final']))

    def rows_sharded(g):
        return g.reshape(N_CHIPS, g.shape[0] // N_CHIPS, g.shape[1])

    bias_zero = jnp.zeros((1, D_MODEL), f32)
    g_ffn, g_mix, d_out_bias = [None] * DEPTH, [None] * DEPTH, [None] * DEPTH
    for i in reversed(range(1, DEPTH)):
        _, bwd, p = mixers[i]
        dh, g_ffn[i], d_out_bias[i], _ = _ffn_bwd(dh, ffn[i], saved[i][1], bias_zero)
        dh, g_mix[i] = bwd(dh, p, saved[i][0])
    _, gb, gc, gd = g_mix

    def late_direct_grads(d_up0, d_down0):
        return {'b_pw1_w': gb['pw1'], 'b_pw2_w': rows_sharded(gb['pw2']), 'c_in_w': gc['in_w'],
                'c_out_w': rows_sharded(gc['out']), 'd_in_w': gd['in_w'], 'd_out_w': rows_sharded(gd['out']),
                'f_up_w': [d_up0] + [g['up'] for g in g_ffn[1:]],
                'f_down_w': [rows_sharded(d_down0)] + [rows_sharded(g['down']) for g in g_ffn[1:]]}

    make_rider = (lambda d_down0: comm.swap_rider(late_direct_grads(None, d_down0))) if comm else None
    dh, g_ffn[0], d_out_bias[0], swapped = _ffn_bwd(dh, ffn[0], saved[0][1], bias_zero, make_rider)
    late_direct = late_direct_grads(g_ffn[0]['up'], g_ffn[0]['down'])
    partials = comm.early_partials(late_direct, swapped) if comm else []
    dh, g_mix[0], received = _ssd_layer_bwd(dh, pa, saved[0][0], to_chips=partials)
    ga = g_mix[0]

    grads = {**late_direct,
        'norm_mix': jnp.concatenate([g['g'] for g in g_mix], axis=0),
        'norm_ffn': jnp.concatenate([g['g'] for g in g_ffn], axis=0),
        'norm_final': d_final.reshape(-1),
        'a_in_proj': jnp.concatenate([ga['w_z'], ga['w_xbc'], ga['w_dt'][:, :SSD_HEADS]], axis=1)[None],
        'a_conv_w': ga['cw'][None], 'a_conv_b': ga['cb'], 'a_dt_bias': ga['dtb'][:, :SSD_HEADS],
        'a_log': ga['a_log'][:, :SSD_HEADS], 'a_d_skip': ga['dsk'][:, :SSD_HEADS], 'a_norm': ga['norm'],
        'a_out_proj': rows_sharded(ga['out']),
        'b_pw1_b': gb['b1'], 'b_dw_w': gb['dw_w'][None], 'b_dw_b': gb['dw_b'],
        'b_ln_g': gb['ln_g'], 'b_ln_b': gb['ln_b'], 'b_pw2_b': d_out_bias[1],
        'c_in_b': gc['in_b'], 'c_conv_w': gc['cw'][None], 'c_conv_b': gc['cb'],
        'c_ga_w': gc['ga_w'].reshape(1, LRU_W // LRU_BLOCK, LRU_BLOCK, LRU_BLOCK),
        'c_ga_b': gc['ga_b'].reshape(1, LRU_W // LRU_BLOCK, LRU_BLOCK),
        'c_gx_w': gc['gx_w'].reshape(1, LRU_W // LRU_BLOCK, LRU_BLOCK, LRU_BLOCK),
        'c_gx_b': gc['gx_b'].reshape(1, LRU_W // LRU_BLOCK, LRU_BLOCK),
        'c_lambda': gc['lam'], 'c_out_b': d_out_bias[2],
        'd_in_b': gd['in_b'], 'd_ln_g': gd['ln_g'], 'd_ln_b': gd['ln_b'],
        'd_sp_w': gd['sp_w'].reshape(1, SGU_GROUPS, SGU_CHUNK, SGU_CHUNK), 'd_sp_b': gd['sp_bt'].T[None],
        'd_out_b': d_out_bias[3],
        'f_conv_w': jnp.stack([g['cw'] for g in g_ffn]),
        'f_conv_b': jnp.concatenate([g['cb'] for g in g_ffn], axis=0),
    }
    return loss, dh, grads, (partials, received)


def _global_shape(name, shard_shape):
    ax = SHARD_AXIS[name]
    if ax is None:
        return tuple(shard_shape)
    s = list(shard_shape)
    s[ax] *= N_CHIPS
    return tuple(s)


def _step(x, target, weights, moments_m, moments_v):
    x2, t2 = x[0], target[0]
    shard_shapes = {n: weights[n].shape for n in WEIGHTS}
    c_pos = lax.axis_index("c")
    k_pos = 2 * lax.axis_index("x") + lax.axis_index("y")
    c_idx = c_pos.astype(jnp.int32).reshape(1)
    k_idx = k_pos.astype(jnp.int32).reshape(1)

    def halves_of(a):
        a2 = _as2d(a)
        return a2.reshape(2, a2.shape[0] // 2, a2.shape[1])

    def view_direct(n, g):
        g = g.reshape((N_CHIPS,) + shard_shapes[n])
        if n in DIRECT_COLS:
            return g
        if n == 'f_down_w':
            return [g[:, i].reshape(-1, g.shape[-1]) for i in range(DEPTH)]
        return g.reshape(-1, g.shape[-1])

    def sibling_sums(grads):
        mine_g = [g.reshape(N_CHIPS, 2, g.shape[1] // 2, g.shape[2]) for g in grads]
        theirs = _swap_with_sibling(mine_g)
        return [_sum_with_sibling(g, t, c_idx) for g, t in zip(mine_g, theirs)]

    def flatten_direct(grads, names):
        out = []
        for n in names:
            out += grads[n] if isinstance(grads[n], list) else [grads[n]]
        return out

    first = [halves_of(weights[n].astype(bf16)) for n in EARLY_DIRECT]
    first.append(_pack([weights[n] for n in PACKED_MM], bf16).reshape(2, -1, FLAT_COLS))
    first.append(_pack([weights[n] for n in SHARDED_VEC], f32).reshape(2, -1, FLAT_COLS))
    gathered = _own_slot(_gather_weights(first), first, k_pos)
    w = {n: weights[n] for n in REPLICATED}
    for n, g in zip(EARLY_DIRECT, gathered):
        w[n] = view_direct(n, g)
    all_mm = _unpack(gathered[-2].reshape(N_CHIPS, -1), [shard_shapes[n] for n in PACKED_MM])
    all_vec = _unpack(gathered[-1].reshape(N_CHIPS, -1), [shard_shapes[n] for n in SHARDED_VEC])
    for n, st in zip(PACKED_MM + SHARDED_VEC, all_mm + all_vec):
        w[n] = _full_from_shards(st, SHARD_AXIS[n])

    def halved(g):
        return g.reshape(N_CHIPS, 2, g.shape[1] // 2, g.shape[2])

    class Comm:
        late_blocks = [halves_of(weights[n].astype(bf16)) for n in LATE_FIRST]
        second_blocks = [halves_of(weights[n].astype(bf16)) for n in LATE_SECOND]

        @staticmethod
        def late_weights(arrived):
            arrived = _own_slot(arrived, Comm.late_blocks, k_pos)
            return {n: view_direct(n, g) for n, g in zip(LATE_FIRST, arrived)}

        @staticmethod
        def second_rider():
            return _gather_rider(Comm.second_blocks)

        @staticmethod
        def second_weights(arrived):
            arrived = _own_slot(arrived, Comm.second_blocks, k_pos)
            return {n: view_direct(n, g) for n, g in zip(LATE_SECOND, arrived)}

        @staticmethod
        def swap_rider(grads):
            return _swap_rider([halved(g) for g in flatten_direct(grads, LATE_DIRECT) if g is not None])

        @staticmethod
        def early_partials(grads, swapped):
            mine_g = [halved(g) for g in flatten_direct(grads, LATE_DIRECT)]
            missing = LAST_SWAPPED_INDEX
            theirs = swapped[:missing] + list(_swap_with_sibling([mine_g[missing]])) + swapped[missing:]
            return [_sum_with_sibling(g, t, c_idx) for g, t in zip(mine_g, theirs)]

    loss_part, dx, grads, (partials, received) = _local_step(x2, t2, w, Comm)

    packed = [_shards_of(grads[n].reshape(_global_shape(n, shard_shapes[n])), SHARD_AXIS[n]).reshape(N_CHIPS, -1)
              for n in PACKED_MM + SHARDED_VEC]
    flat = jnp.concatenate(packed, axis=1)
    n_flat = flat.shape[1]
    n_pad = -(-n_flat // FLAT_QUANTUM) * FLAT_QUANTUM
    flat = jnp.pad(flat, ((0, 0), (0, n_pad - n_flat))).reshape(N_CHIPS, -1, FLAT_COLS)
    last_partials = sibling_sums(flatten_direct(grads, EARLY_DIRECT) + [flat])
    last_received = _send_to_chips(last_partials)
    partials, received = list(partials) + last_partials, list(received) + list(last_received)
    my_halves = [_sum_chips(p, r, k_idx) for p, r in zip(partials, received)]
    joined = _own_slot(_join_halves(my_halves), my_halves, c_pos)
    g_shard, pos = {}, 0
    for n in LATE_DIRECT + EARLY_DIRECT:
        layers = shard_shapes[n][0]
        g_shard[n] = jnp.stack([j.reshape(shard_shapes[n][1:]) for j in joined[pos:pos + layers]])
        pos += layers
    flat_shapes = [shard_shapes[n] for n in PACKED_MM + SHARDED_VEC]
    g_shard.update(zip(PACKED_MM + SHARDED_VEC, _unpack(joined[-1].reshape(-1), flat_shapes)))

    small = jnp.concatenate([grads[n].reshape(-1) for n in REPLICATED] + [loss_part.reshape(-1)[:1]])
    n_small = small.shape[0]
    n_small_pad = -(-n_small // (SUBLANE * FLAT_COLS)) * (SUBLANE * FLAT_COLS)
    small = jnp.pad(small, (0, n_small_pad - n_small)).reshape(-1, FLAT_COLS)
    small = _all_sum_small(small).reshape(-1)
    g_rep = dict(zip(REPLICATED, _unpack(small, [shard_shapes[n] for n in REPLICATED])))
    loss = small[n_small - 1]

    g_all = {**g_shard, **g_rep}
    delta, new_m, new_v = {}, {}, {}
    for n in WEIGHTS:
        delta[n], new_m[n], new_v[n] = _adamw(weights[n], g_all[n], moments_m[n], moments_v[n], "adamw_" + n)
    return loss, dx[None], g_all, delta, new_m, new_v


def kernel(x, norm_mix, norm_ffn, norm_final, a_in_proj, a_conv_w, a_conv_b, a_dt_bias, a_log, a_d_skip, a_norm, a_out_proj, b_pw1_w, b_pw1_b, b_dw_w, b_dw_b, b_ln_g, b_ln_b, b_pw2_w, b_pw2_b, c_in_w, c_in_b, c_conv_w, c_conv_b, c_ga_w, c_ga_b, c_gx_w, c_gx_b, c_lambda, c_out_w, c_out_b, d_in_w, d_in_b, d_ln_g, d_ln_b, d_sp_w, d_sp_b, d_out_w, d_out_b, f_up_w, f_conv_w, f_conv_b, f_down_w, loss_target, m_norm_mix, m_norm_ffn, m_norm_final, m_a_in_proj, m_a_conv_w, m_a_conv_b, m_a_dt_bias, m_a_log, m_a_d_skip, m_a_norm, m_a_out_proj, m_b_pw1_w, m_b_pw1_b, m_b_dw_w, m_b_dw_b, m_b_ln_g, m_b_ln_b, m_b_pw2_w, m_b_pw2_b, m_c_in_w, m_c_in_b, m_c_conv_w, m_c_conv_b, m_c_ga_w, m_c_ga_b, m_c_gx_w, m_c_gx_b, m_c_lambda, m_c_out_w, m_c_out_b, m_d_in_w, m_d_in_b, m_d_ln_g, m_d_ln_b, m_d_sp_w, m_d_sp_b, m_d_out_w, m_d_out_b, m_f_up_w, m_f_conv_w, m_f_conv_b, m_f_down_w, v_norm_mix, v_norm_ffn, v_norm_final, v_a_in_proj, v_a_conv_w, v_a_conv_b, v_a_dt_bias, v_a_log, v_a_d_skip, v_a_norm, v_a_out_proj, v_b_pw1_w, v_b_pw1_b, v_b_dw_w, v_b_dw_b, v_b_ln_g, v_b_ln_b, v_b_pw2_w, v_b_pw2_b, v_c_in_w, v_c_in_b, v_c_conv_w, v_c_conv_b, v_c_ga_w, v_c_ga_b, v_c_gx_w, v_c_gx_b, v_c_lambda, v_c_out_w, v_c_out_b, v_d_in_w, v_d_in_b, v_d_ln_g, v_d_ln_b, v_d_sp_w, v_d_sp_b, v_d_out_w, v_d_out_b, v_f_up_w, v_f_conv_w, v_f_conv_b, v_f_down_w):
    args = locals()
    weights = {n: args[n] for n in WEIGHTS}
    moments_m = {n: args['m_' + n] for n in WEIGHTS}
    moments_v = {n: args['v_' + n] for n in WEIGHTS}
    loss, dx, grad, delta, new_m, new_v = _step(x, loss_target, weights, moments_m, moments_v)
    return (loss, dx, *[grad[n] for n in WEIGHTS], *[delta[n] for n in WEIGHTS],
            *[new_m[n] for n in WEIGHTS], *[new_v[n] for n in WEIGHTS])
```

```python
import functools
import math

import jax
import jax.numpy as jnp
import numpy as np
from jax import lax
from jax.experimental import pallas as pl
from jax.experimental.pallas import tpu as pltpu

f32 = jnp.float32
bf16 = jnp.bfloat16
MESH = pl.DeviceIdType.MESH
HIGHEST = lax.Precision.HIGHEST

D_MODEL = 1024
DEPTH = 4
RMS_EPS = 1e-6
LN_EPS = 1e-5
SSD_D_INNER = 2048
SSD_HEADS = 32
SSD_BC = 1024
SSD_CONV_DIM = 4096
SSD_CHUNK = 128
SSD_GROUPS = 8
LRU_W = 1280
LRU_BLOCK = 256
LRU_C = 8.0
SGU_HALF = 2048
SGU_GROUPS = 8
SGU_CHUNK = 128
FFN_H = 2816
ADAM_LR, ADAM_B1, ADAM_B2, ADAM_EPS, ADAM_WD, ADAM_STEP = 0.001, 0.9, 0.999, 1e-08, 0.01, 10

LANE = 128
SUBLANE = 8
VMEM_LIMIT = 56 * 1024 * 1024
FLAT_COLS = 1024

WEIGHTS = ['norm_mix', 'norm_ffn', 'norm_final', 'a_in_proj', 'a_conv_w', 'a_conv_b', 'a_dt_bias', 'a_log',
           'a_d_skip', 'a_norm', 'a_out_proj', 'b_pw1_w', 'b_pw1_b', 'b_dw_w', 'b_dw_b', 'b_ln_g', 'b_ln_b',
           'b_pw2_w', 'b_pw2_b', 'c_in_w', 'c_in_b', 'c_conv_w', 'c_conv_b', 'c_ga_w', 'c_ga_b', 'c_gx_w',
           'c_gx_b', 'c_lambda', 'c_out_w', 'c_out_b', 'd_in_w', 'd_in_b', 'd_ln_g', 'd_ln_b', 'd_sp_w',
           'd_sp_b', 'd_out_w', 'd_out_b', 'f_up_w', 'f_conv_w', 'f_conv_b', 'f_down_w']
SHARD_AXIS = {
    'norm_mix': None, 'norm_ffn': None, 'norm_final': None, 'a_in_proj': 2, 'a_conv_w': 2, 'a_conv_b': None,
    'a_dt_bias': None, 'a_log': None, 'a_d_skip': None, 'a_norm': None, 'a_out_proj': 1, 'b_pw1_w': 2,
    'b_pw1_b': 1, 'b_dw_w': 2, 'b_dw_b': 1, 'b_ln_g': 1, 'b_ln_b': 1, 'b_pw2_w': 1, 'b_pw2_b': 1, 'c_in_w': 2,
    'c_in_b': 1, 'c_conv_w': 2, 'c_conv_b': 1, 'c_ga_w': 2, 'c_ga_b': 2, 'c_gx_w': 2, 'c_gx_b': 2,
    'c_lambda': 1, 'c_out_w': 1, 'c_out_b': 1, 'd_in_w': 2, 'd_in_b': 1, 'd_ln_g': 1, 'd_ln_b': 1,
    'd_sp_w': None, 'd_sp_b': None, 'd_out_w': 1, 'd_out_b': 1, 'f_up_w': 2, 'f_conv_w': 2, 'f_conv_b': None,
    'f_down_w': 1}
MATMUL_WEIGHTS = ['a_in_proj', 'a_out_proj', 'b_pw1_w', 'b_pw2_w', 'c_in_w', 'c_ga_w', 'c_gx_w', 'c_out_w',
                  'd_in_w', 'd_out_w', 'f_up_w', 'f_down_w']
DIRECT_COLS = ['b_pw1_w', 'c_in_w', 'd_in_w', 'f_up_w']
DIRECT_ROWS = ['a_out_proj', 'b_pw2_w', 'c_out_w', 'd_out_w', 'f_down_w']
DIRECT = DIRECT_COLS + DIRECT_ROWS
EARLY_DIRECT = ['a_out_proj']
LATE_DIRECT = [n for n in DIRECT if n not in EARLY_DIRECT]
LATE_SECOND = ['c_in_w', 'c_out_w', 'd_in_w', 'd_out_w']
LATE_FIRST = [n for n in LATE_DIRECT if n not in LATE_SECOND]
LAST_SWAPPED_INDEX = LATE_DIRECT.index('f_up_w')
PACKED_MM = [n for n in MATMUL_WEIGHTS if n not in DIRECT]
SHARDED = [n for n in WEIGHTS if SHARD_AXIS[n] is not None]
SHARDED_VEC = [n for n in SHARDED if n not in MATMUL_WEIGHTS]
REPLICATED = [n for n in WEIGHTS if SHARD_AXIS[n] is None]
N_CHIPS = 4
N_DEV = 8


def _tile(n, cap, mult):
    if n <= cap:
        return n
    t = (cap // mult) * mult
    while t >= mult:
        if n % t == 0:
            return t
        t -= mult
    raise ValueError(f"no tile for {n} under {cap} in steps of {mult}")


def _cparams(sem=None):
    if sem is None:
        return pltpu.CompilerParams(vmem_limit_bytes=VMEM_LIMIT)
    return pltpu.CompilerParams(dimension_semantics=sem, vmem_limit_bytes=VMEM_LIMIT)


def _dg(a, b, ca, cb):
    return lax.dot_general(a.astype(bf16), b.astype(bf16), (((ca,), (cb,)), ((), ())), preferred_element_type=f32)


@jax.custom_vjp
def _dot_nn(a, b):
    return _dg(a, b, 1, 0)


def _dot_nn_fwd(a, b):
    return _dg(a, b, 1, 0), (a, b)


def _dot_nn_bwd(res, g):
    a, b = res
    return _dg(g, b, 1, 1).astype(a.dtype), _dg(a, g, 0, 0).astype(b.dtype)


_dot_nn.defvjp(_dot_nn_fwd, _dot_nn_bwd)


@jax.custom_vjp
def _dot_nt(a, b):
    return _dg(a, b, 1, 1)


def _dot_nt_fwd(a, b):
    return _dg(a, b, 1, 1), (a, b)


def _dot_nt_bwd(res, g):
    a, b = res
    return _dg(g, b, 1, 0).astype(a.dtype), _dg(g, a, 0, 0).astype(b.dtype)


_dot_nt.defvjp(_dot_nt_fwd, _dot_nt_bwd)


@jax.custom_vjp
def _dot_tn(a, b):
    return _dg(a, b, 0, 0)


def _dot_tn_fwd(a, b):
    return _dg(a, b, 0, 0), (a, b)


def _dot_tn_bwd(res, g):
    a, b = res
    return _dg(b, g, 1, 1).astype(a.dtype), _dg(a, g, 1, 0).astype(b.dtype)


_dot_tn.defvjp(_dot_tn_fwd, _dot_tn_bwd)


def _expm1(x):
    small = jnp.abs(x) < 0.03
    xs = jnp.where(small, x, 0.0)
    series = xs * (1.0 + xs * (0.5 + xs * (1.0 / 6.0 + xs * (1.0 / 24.0 + xs * (1.0 / 120.0)))))
    return jnp.where(small, series, jnp.exp(x) - 1.0)


def _rms(x, g):
    return x * lax.rsqrt(jnp.mean(x * x, axis=-1, keepdims=True) + RMS_EPS) * g


def _layer_norm(x, g, b):
    mu = jnp.mean(x, axis=-1, keepdims=True)
    xc = x - mu
    return xc * lax.rsqrt(jnp.mean(xc * xc, axis=-1, keepdims=True) + LN_EPS) * g + b


def _causal_taps(ext, w, halo, rows):
    k_taps = w.shape[0]
    acc = None
    for k in range(k_taps):
        lo = halo - (k_taps - 1) + k
        term = w[k:k + 1, :] * ext[lo:lo + rows, :]
        acc = term if acc is None else acc + term
    return acc


def _mm(a, b, mode, name, *, bias=None, add=None, out_dtype=f32, tm_cap=1408, tn_cap=1408, tk_cap=1408,
        b_cols_sharded=False, b_layer=None, out_cols_sharded=False, rms_gain=None):
    shard_cols = None
    if b_cols_sharded:
        shard_cols = b.shape[-1]
        b_dims = (b.shape[-2], N_CHIPS * shard_cols)
    else:
        b_dims = b.shape
    if mode == 'nn':
        (m, k), (k2, n) = a.shape, b_dims
    elif mode == 'nt':
        (m, k), (n, k2) = a.shape, b_dims
    else:
        (k, m), (k2, n) = a.shape, b_dims
    assert k == k2, (name, a.shape, b.shape)
    tm = _tile(m, tm_cap, LANE if mode == 'tn' else SUBLANE)
    tn = _tile(n, tn_cap, LANE)
    tk = _tile(k, tk_cap, LANE if mode != 'tn' else SUBLANE)
    if b_cols_sharded and mode == 'nn':
        tn = shard_cols
    if b_cols_sharded and mode == 'nt':
        tk = shard_cols
    if out_cols_sharded:
        assert mode == 'tn' and n % N_CHIPS == 0
        tn = n // N_CHIPS
    nk = k // tk

    def shard_block(rows):
        lead = (None,) * (b.ndim - 2)
        return lead + (rows, shard_cols)

    def shard_index(shard, row_block):
        return (shard, row_block, 0) if b_layer is None else (shard, b_layer, row_block, 0)

    if mode == 'nn':
        a_spec = pl.BlockSpec((tm, tk), lambda i, j, kk: (i, kk))
        if b_cols_sharded:
            b_spec = pl.BlockSpec(shard_block(tk), lambda i, j, kk: shard_index(j, kk))
        else:
            b_spec = pl.BlockSpec((tk, tn), lambda i, j, kk: (kk, j))
        ca, cb = 1, 0
    elif mode == 'nt':
        a_spec = pl.BlockSpec((tm, tk), lambda i, j, kk: (i, kk))
        if b_cols_sharded:
            b_spec = pl.BlockSpec(shard_block(tn), lambda i, j, kk: shard_index(kk, j))
        else:
            b_spec = pl.BlockSpec((tn, tk), lambda i, j, kk: (j, kk))
        ca, cb = 1, 1
    else:
        a_spec = pl.BlockSpec((tk, tm), lambda i, j, kk: (kk, i))
        b_spec = pl.BlockSpec((tk, tn), lambda i, j, kk: (kk, j))
        ca, cb = 0, 0
    in_specs, operands = [a_spec, b_spec], [a, b]
    if bias is not None:
        in_specs.append(pl.BlockSpec((1, tn), lambda i, j, kk: (0, j)))
        operands.append(bias)
    if add is not None:
        in_specs.append(pl.BlockSpec((tm, tn), lambda i, j, kk: (i, j)))
        operands.append(add)
    if rms_gain is not None:
        in_specs.append(pl.BlockSpec((1, tn), lambda i, j, kk: (0, j)))
        operands.append(rms_gain)

    def body(*refs):
        a_ref, b_ref = refs[0], refs[1]
        pos = 2
        bias_ref = add_ref = None
        if bias is not None:
            bias_ref = refs[pos]
            pos += 1
        if add is not None:
            add_ref = refs[pos]
            pos += 1
        norm_ref = None
        if rms_gain is not None:
            norm_ref = refs[pos]
            pos += 1
        o_ref = refs[pos]
        normed_ref = refs[pos + 1] if rms_gain is not None else None
        acc_ref = refs[-1]
        kk = pl.program_id(2)

        @pl.when(kk == 0)
        def _():
            acc_ref[...] = jnp.zeros_like(acc_ref)

        acc_ref[...] += _dg(a_ref[...], b_ref[...], ca, cb)

        @pl.when(kk == nk - 1)
        def _():
            r = acc_ref[...]
            if bias_ref is not None:
                r = r + bias_ref[...]
            if add_ref is not None:
                r = r + add_ref[...].astype(f32)
            o_ref[...] = r.astype(out_dtype)
            if normed_ref is not None:
                normed_ref[...] = _rms(r, norm_ref[...]).astype(bf16)

    if out_cols_sharded:
        out_shape = jax.ShapeDtypeStruct((N_CHIPS, m, tn), out_dtype)
        out_spec = pl.BlockSpec((None, tm, tn), lambda i, j, kk: (j, i, 0))
    else:
        out_shape = jax.ShapeDtypeStruct((m, n), out_dtype)
        out_spec = pl.BlockSpec((tm, tn), lambda i, j, kk: (i, j))
    if rms_gain is not None:
        assert tn == n and not out_cols_sharded, "the norm needs whole rows in a tile"
        out_shape = [out_shape, jax.ShapeDtypeStruct((m, n), bf16)]
        out_spec = [out_spec, pl.BlockSpec((tm, tn), lambda i, j, kk: (i, j))]
    return pl.pallas_call(
        body, name=name, out_shape=out_shape,
        grid=(m // tm, n // tn, nk), in_specs=in_specs, out_specs=out_spec,
        scratch_shapes=[pltpu.VMEM((tm, tn), f32)],
        compiler_params=_cparams(("parallel", "parallel", "arbitrary")),
    )(*operands)


def _mm_normed(a, b, mode, name, *, rms_gain, **kw):
    if rms_gain is None:
        return _mm(a, b, mode, name, **kw), None
    return _mm(a, b, mode, name, rms_gain=rms_gain, **kw)


def _mm_w(a, w, mode, name, **kw):
    shards, layer = w
    return _mm(a, shards, mode, name, b_cols_sharded=True, b_layer=layer, **kw)


def _row_specs(tiles, halo_of, rows, halo, n_tiles, reverse):
    def tile_index(i):
        return n_tiles - 1 - i if reverse else i

    specs, operands = [], []
    for arr, has_halo in zip(tiles, halo_of):
        cols = arr.shape[1]
        specs.append(pl.BlockSpec((rows, cols), lambda i: (tile_index(i), 0)))
        operands.append(arr)
        if has_halo:
            per = rows // halo
            specs.append(pl.BlockSpec((halo, cols), lambda i: (jnp.maximum(tile_index(i) * per - 1, 0), 0)))
            operands.append(arr)
    return specs, operands, tile_index


def _load_tiles(refs, halo_of, tile_id, rows, halo):
    vals, pos = [], 0
    for has_halo in halo_of:
        cur = refs[pos][...].astype(f32)
        pos += 1
        if has_halo:
            before = refs[pos][...].astype(f32)
            pos += 1
            before = jnp.where(tile_id > 0, before, jnp.zeros_like(before))
            cur = jnp.concatenate([before, cur], axis=0)
        vals.append(cur)
    return vals, pos


def _valid_rows(tile_id, rows, halo):
    r = lax.broadcasted_iota(jnp.int32, (halo + rows, 1), 0)
    return jnp.logical_or(r >= halo, tile_id > 0).astype(f32)


def _row_fwd(f, tiles, params, outs, *, rows, name, halo=0, halo_of=None):
    t_len = tiles[0].shape[0]
    rows = min(rows, t_len)
    n_tiles = t_len // rows
    halo_of = halo_of or [False] * len(tiles)
    specs, operands, _ = _row_specs(tiles, halo_of, rows, halo, n_tiles, False)
    for p in params:
        specs.append(pl.BlockSpec(p.shape, lambda i: (0, 0)))
        operands.append(p)

    def body(*refs):
        i = pl.program_id(0)
        vals, pos = _load_tiles(refs, halo_of, i, rows, halo)
        pvals = [refs[pos + j][...] for j in range(len(params))]
        pos += len(params)
        kw = {'valid': _valid_rows(i, rows, halo)} if halo else {}
        res = f(*vals, *pvals, **kw)
        for o_ref, o in zip(refs[pos:], res):
            o_ref[...] = o.astype(o_ref.dtype)

    return pl.pallas_call(
        body, name=name,
        out_shape=[jax.ShapeDtypeStruct((t_len, c), d) for c, d in outs],
        grid=(n_tiles,), in_specs=specs,
        out_specs=[pl.BlockSpec((rows, c), lambda i: (i, 0)) for c, _ in outs],
        compiler_params=_cparams(("parallel",)),
    )(*operands)


def _row_bwd(f, tiles, params, cots, *, rows, name, halo=0, halo_of=None, tile_dtypes=None):
    t_len = tiles[0].shape[0]
    rows = min(rows, t_len)
    n_tiles = t_len // rows
    halo_of = halo_of or [False] * len(tiles)
    tile_dtypes = tile_dtypes or [f32] * len(tiles)
    specs, operands, tile_index = _row_specs(tiles, halo_of, rows, halo, n_tiles, True)
    for p in params:
        specs.append(pl.BlockSpec(p.shape, lambda i: (0, 0)))
        operands.append(p)
    for ct in cots:
        specs.append(pl.BlockSpec((rows, ct.shape[1]), lambda i: (tile_index(i), 0)))
        operands.append(ct)
    n_t, n_p, n_c = len(tiles), len(params), len(cots)
    out_shape = [jax.ShapeDtypeStruct(t.shape, d) for t, d in zip(tiles, tile_dtypes)]
    out_shape += [jax.ShapeDtypeStruct(p.shape, f32) for p in params]
    out_specs = [pl.BlockSpec((rows, t.shape[1]), lambda i: (tile_index(i), 0)) for t in tiles]
    out_specs += [pl.BlockSpec(p.shape, lambda i: (0, 0)) for p in params]
    scratch = [pltpu.VMEM((halo, t.shape[1]), f32) for t, h in zip(tiles, halo_of) if h]

    def body(*refs):
        i = pl.program_id(0)
        tile_id = tile_index(i)
        vals, pos = _load_tiles(refs, halo_of, tile_id, rows, halo)
        pvals = [refs[pos + j][...] for j in range(n_p)]
        pos += n_p
        cvals = [refs[pos + j][...].astype(f32) for j in range(n_c)]
        pos += n_c
        d_tile_refs = refs[pos:pos + n_t]
        d_param_refs = refs[pos + n_t:pos + n_t + n_p]
        carries = list(refs[pos + n_t + n_p:])
        kw = {'valid': _valid_rows(tile_id, rows, halo)} if halo else {}
        _, vjp = jax.vjp(lambda *args: tuple(f(*args, **kw)), *vals, *pvals)
        grads = vjp(tuple(cvals))

        @pl.when(i == 0)
        def _():
            for cr in carries:
                cr[...] = jnp.zeros_like(cr)
            for dp in d_param_refs:
                dp[...] = jnp.zeros_like(dp)

        ci = 0
        for t in range(n_t):
            g = grads[t]
            if halo_of[t]:
                cr = carries[ci]
                ci += 1
                d_tile_refs[t][0:rows - halo, :] = g[halo:rows, :].astype(d_tile_refs[t].dtype)
                d_tile_refs[t][rows - halo:rows, :] = (g[rows:rows + halo, :] + cr[...]).astype(d_tile_refs[t].dtype)
                cr[...] = g[0:halo, :]
            else:
                d_tile_refs[t][...] = g.astype(d_tile_refs[t].dtype)
        for j in range(n_p):
            d_param_refs[j][...] += grads[n_t + j]

    return pl.pallas_call(
        body, name=name, out_shape=out_shape, grid=(n_tiles,), in_specs=specs, out_specs=out_specs,
        scratch_shapes=scratch, compiler_params=_cparams(("arbitrary",)),
    )(*operands)


def _strip_specs(tiles, rows, halo, n_tiles, reverse):
    def tile_index(i):
        return n_tiles - 1 - i if reverse else i

    specs, operands = [], []
    for arr, _, has_halo in tiles:
        cols = arr.shape[1]
        specs.append(pl.BlockSpec((rows, cols), lambda i: (tile_index(i), 0)))
        operands.append(arr)
        if has_halo:
            per = rows // halo
            specs.append(pl.BlockSpec((halo, cols), lambda i: (jnp.maximum(tile_index(i) * per - 1, 0), 0)))
            operands.append(arr)
    return specs, operands, tile_index


def _strip_sources(refs, tiles, ext_scratch, tile_id, rows, halo):
    srcs, pos, si = [], 0, 0
    for _, _, has_halo in tiles:
        cur = refs[pos]
        pos += 1
        if has_halo:
            before = refs[pos]
            pos += 1
            scr = ext_scratch[si]
            si += 1
            scr[0:halo, :] = jnp.where(tile_id > 0, before[...].astype(f32), 0.0)
            scr[halo:halo + rows, :] = cur[...].astype(f32)
            srcs.append(scr)
        else:
            srcs.append(cur)
    return srcs, pos


def _cols(base, c0, cs):
    return pl.ds(pl.multiple_of(base + c0, LANE), cs)


def _strip_inputs(tiles, srcs, params, p_refs, r0, c0, rs, cs, halo):
    vals = []
    for (_, bases, has_halo), src in zip(tiles, srcs):
        n_rows = halo + rs if has_halo else rs
        for b in bases:
            vals.append(src[pl.ds(r0, n_rows), _cols(b, c0, cs)].astype(f32))
    for (_, bases), p_ref in zip(params, p_refs):
        for b in bases:
            vals.append(p_ref[:, _cols(b, c0, cs)])
    return vals


def _strip_valid(tile_id, r0, rs, halo):
    r = lax.broadcasted_iota(jnp.int32, (halo + rs, 1), 0) + r0
    return jnp.logical_or(r >= halo, tile_id > 0).astype(f32)


def _strip_fwd(f, tiles, params, outs, *, rows, rs, cs, width, name, halo=0):
    t_len = tiles[0][0].shape[0]
    rows = min(rows, t_len)
    n_tiles, n_rs, n_cs = t_len // rows, rows // rs, width // cs
    specs, operands, _ = _strip_specs(tiles, rows, halo, n_tiles, False)
    for p, _ in params:
        specs.append(pl.BlockSpec(p.shape, lambda i: (0, 0)))
        operands.append(p)
    n_p, n_o = len(params), len(outs)
    scratch = [pltpu.VMEM((halo + rows, arr.shape[1]), f32) for arr, _, hh in tiles if hh]

    def body(*refs):
        i = pl.program_id(0)
        ext_scratch = refs[len(refs) - len(scratch):]
        srcs, pos = _strip_sources(refs, tiles, ext_scratch, i, rows, halo)
        p_refs = refs[pos:pos + n_p]
        o_refs = refs[pos + n_p:pos + n_p + n_o]

        def row_loop(r, carry):
            r0 = pl.multiple_of(r * rs, rs)
            kw = {'valid': _strip_valid(i, r0, rs, halo)} if halo else {}

            def col_loop(c, carry2):
                c0 = c * cs
                res = f(*_strip_inputs(tiles, srcs, params, p_refs, r0, c0, rs, cs, halo), **kw)
                k = 0
                for (_, dt, bases), o_ref in zip(outs, o_refs):
                    for b in bases:
                        o_ref[pl.ds(r0, rs), _cols(b, c0, cs)] = res[k].astype(dt)
                        k += 1
                return carry2

            return lax.fori_loop(0, n_cs, col_loop, carry)

        lax.fori_loop(0, n_rs, row_loop, 0)

    return pl.pallas_call(
        body, name=name,
        out_shape=[jax.ShapeDtypeStruct((t_len, c), d) for c, d, _ in outs],
        grid=(n_tiles,), in_specs=specs,
        out_specs=[pl.BlockSpec((rows, c), lambda i: (i, 0)) for c, _, _ in outs],
        scratch_shapes=scratch, compiler_params=_cparams(("parallel",)),
    )(*operands)


def _strip_bwd(f, tiles, params, cots, *, rows, rs, cs, width, name, halo=0, tile_dtypes=None):
    t_len = tiles[0][0].shape[0]
    rows = min(rows, t_len)
    n_tiles, n_rs, n_cs = t_len // rows, rows // rs, width // cs
    tile_dtypes = tile_dtypes or [f32] * len(tiles)
    specs, operands, tile_index = _strip_specs(tiles, rows, halo, n_tiles, True)
    for p, _ in params:
        specs.append(pl.BlockSpec(p.shape, lambda i: (0, 0)))
        operands.append(p)
    for ct, _ in cots:
        specs.append(pl.BlockSpec((rows, ct.shape[1]), lambda i: (tile_index(i), 0)))
        operands.append(ct)
    n_t, n_p, n_c = len(tiles), len(params), len(cots)
    out_shape = [jax.ShapeDtypeStruct(t[0].shape, d) for t, d in zip(tiles, tile_dtypes)]
    out_shape += [jax.ShapeDtypeStruct(p.shape, f32) for p, _ in params]
    out_specs = [pl.BlockSpec((rows, t[0].shape[1]), lambda i: (tile_index(i), 0)) for t in tiles]
    out_specs += [pl.BlockSpec(p.shape, lambda i: (0, 0)) for p, _ in params]
    halo_tiles = [t for t in tiles if t[2]]
    scratch = [pltpu.VMEM((halo + rows, arr.shape[1]), f32) for arr, _, _ in halo_tiles]
    scratch += [pltpu.VMEM((halo + rows, arr.shape[1]), f32) for arr, _, _ in halo_tiles]
    scratch += [pltpu.VMEM((halo, arr.shape[1]), f32) for arr, _, _ in halo_tiles]
    n_h = len(halo_tiles)

    def body(*refs):
        i = pl.program_id(0)
        tile_id = tile_index(i)
        scr = refs[len(refs) - 3 * n_h:]
        ext_scratch, grad_scratch, carries = scr[:n_h], scr[n_h:2 * n_h], scr[2 * n_h:]
        srcs, pos = _strip_sources(refs, tiles, ext_scratch, tile_id, rows, halo)
        p_refs = refs[pos:pos + n_p]
        c_refs = refs[pos + n_p:pos + n_p + n_c]
        pos += n_p + n_c
        d_tile_refs = refs[pos:pos + n_t]
        d_param_refs = refs[pos + n_t:pos + n_t + n_p]

        @pl.when(i == 0)
        def _():
            for cr in carries:
                cr[...] = jnp.zeros_like(cr)
            for dp in d_param_refs:
                dp[...] = jnp.zeros_like(dp)

        for gs in grad_scratch:
            gs[...] = jnp.zeros_like(gs)

        def row_loop(r, carry):
            r0 = pl.multiple_of(r * rs, rs)
            kw = {'valid': _strip_valid(tile_id, r0, rs, halo)} if halo else {}

            def col_loop(c, carry2):
                c0 = c * cs
                vals = _strip_inputs(tiles, srcs, params, p_refs, r0, c0, rs, cs, halo)
                cvals = []
                for (_, bases), c_ref in zip(cots, c_refs):
                    for b in bases:
                        cvals.append(c_ref[pl.ds(r0, rs), _cols(b, c0, cs)].astype(f32))
                _, vjp = jax.vjp(lambda *args: tuple(f(*args, **kw)), *vals)
                grads = vjp(tuple(cvals))
                k, hi = 0, 0
                for t, (_, bases, has_halo) in enumerate(tiles):
                    for b in bases:
                        if has_halo:
                            grad_scratch[hi][pl.ds(r0, halo + rs), _cols(b, c0, cs)] += grads[k]
                        else:
                            d_tile_refs[t][pl.ds(r0, rs), _cols(b, c0, cs)] = grads[k].astype(d_tile_refs[t].dtype)
                        k += 1
                    hi += has_halo
                for (_, bases), dp in zip(params, d_param_refs):
                    for b in bases:
                        dp[:, _cols(b, c0, cs)] += grads[k]
                        k += 1
                return carry2

            return lax.fori_loop(0, n_cs, col_loop, carry)

        lax.fori_loop(0, n_rs, row_loop, 0)

        hi = 0
        for t, (_, _, has_halo) in enumerate(tiles):
            if has_halo:
                gs, cr, d_ref = grad_scratch[hi], carries[hi], d_tile_refs[t]
                hi += 1
                d_ref[0:rows - halo, :] = gs[halo:rows, :].astype(d_ref.dtype)
                d_ref[rows - halo:rows, :] = (gs[rows:rows + halo, :] + cr[...]).astype(d_ref.dtype)
                cr[...] = gs[0:halo, :]

    return pl.pallas_call(
        body, name=name, out_shape=out_shape, grid=(n_tiles,), in_specs=specs, out_specs=out_specs,
        scratch_shapes=scratch, compiler_params=_cparams(("arbitrary",)),
    )(*operands)


def _fold8(v):
    acc = v[0:SUBLANE, :]
    for m in range(1, v.shape[0] // SUBLANE):
        acc = acc + v[m * SUBLANE:(m + 1) * SUBLANE, :]
    return acc


class _ConvPlan:
    def __init__(self, x, w, b, *, in_bases, mid_bases, width, rows, rs, pre=None, pre_params=(), post=None):
        self.x, self.w, self.b = x, w, b
        self.in_bases, self.mid_bases, self.width = in_bases, mid_bases, width
        self.pre, self.pre_params, self.post = pre, list(pre_params), post
        self.k_taps = w.shape[0]
        tile_rows = SUBLANE * (4 // x.dtype.itemsize)
        self.halo = -(-(self.k_taps - 1) // tile_rows) * tile_rows
        self.t_len = x.shape[0]
        self.rows = min(rows, self.t_len)
        self.rs = rs
        self.n_tiles, self.n_rs, self.n_cs = self.t_len // self.rows, self.rows // rs, width // LANE
        self.n_mid = len(mid_bases)
        if pre is None:
            assert len(in_bases) == self.n_mid

    def in_specs(self, tile_index):
        cols = self.x.shape[1]
        per = self.rows // self.halo
        specs = [pl.BlockSpec((self.rows, cols), lambda i: (tile_index(i), 0)),
                 pl.BlockSpec((self.halo, cols), lambda i: (jnp.maximum(tile_index(i) * per - 1, 0), 0)),
                 pl.BlockSpec(self.w.shape, lambda i: (0, 0)), pl.BlockSpec(self.b.shape, lambda i: (0, 0))]
        operands = [self.x, self.x, self.w, self.b]
        for p, _ in self.pre_params:
            specs.append(pl.BlockSpec(p.shape, lambda i: (0, 0)))
            operands.append(p)
        return specs, operands

    def pre_strips(self, pp_refs, c0):
        return [p_ref[:, _cols(b, c0, LANE)] for (_, bases), p_ref in zip(self.pre_params, pp_refs) for b in bases]

    def fill_conv_input(self, cur_ref, before_ref, pp_refs, u_ref, tile_id):
        started = (tile_id > 0).astype(f32)

        def col_loop(c, carry):
            c0 = c * LANE
            pps = self.pre_strips(pp_refs, c0)
            xs = [before_ref[:, _cols(b, c0, LANE)].astype(f32) for b in self.in_bases]
            for j, u in enumerate(self.pre(*xs, *pps, valid=started)):
                u_ref[0:self.halo, _cols(j * self.width, c0, LANE)] = u
            for r in range(self.n_rs):
                xs = [cur_ref[r * self.rs:(r + 1) * self.rs, _cols(b, c0, LANE)].astype(f32) for b in self.in_bases]
                for j, u in enumerate(self.pre(*xs, *pps, valid=1.0)):
                    u_ref[self.halo + r * self.rs:self.halo + (r + 1) * self.rs, _cols(j * self.width, c0, LANE)] = u
            return carry

        lax.fori_loop(0, self.n_cs, col_loop, 0)

    def tap(self, cur_ref, before_ref, u_ref, tile_id, r, j, k, c0):
        lo = r * self.rs - (self.k_taps - 1) + k
        if u_ref is not None:
            return u_ref[self.halo + lo:self.halo + lo + self.rs, _cols(j * self.width, c0, LANE)]
        cols = _cols(self.in_bases[j], c0, LANE)
        if lo >= 0:
            return cur_ref[lo:lo + self.rs, cols].astype(f32)
        head = before_ref[self.halo + lo:self.halo, cols].astype(f32)
        head = jnp.where(tile_id > 0, head, 0.0)
        return jnp.concatenate([head, cur_ref[0:self.rs + lo, cols].astype(f32)], axis=0)

    def conv(self, cur_ref, before_ref, u_ref, w_ref, b_ref, tile_id, r, c0):
        hcs = []
        for j, mb in enumerate(self.mid_bases):
            cols = _cols(mb, c0, LANE)
            acc = b_ref[:, cols]
            for k in range(self.k_taps):
                acc = acc + w_ref[k:k + 1, cols] * self.tap(cur_ref, before_ref, u_ref, tile_id, r, j, k, c0)
            hcs.append(acc)
        return hcs


def _conv_fwd(plan, outs, name, rider=None):
    n_pp = len(plan.pre_params)
    n_in, n_out = 4 + n_pp, len(outs)
    r_n = rider.n if rider else 0
    specs, operands = plan.in_specs(lambda i: i)
    scratch = [pltpu.VMEM((plan.halo + plan.rows, plan.n_mid * plan.width), f32)] if plan.pre else []
    n_scr = len(scratch)

    def body(*refs):
        cur_ref, before_ref, w_ref, b_ref = refs[:4]
        pp_refs = refs[4:n_in]
        r_srcs = refs[n_in:n_in + r_n]
        o_refs = refs[n_in + r_n:n_in + r_n + n_out]
        r_outs = refs[n_in + r_n + n_out:n_in + 2 * r_n + n_out]
        scr = refs[n_in + 2 * r_n + n_out:]
        u_ref = scr[0] if plan.pre else None
        i = pl.program_id(0)
        if rider:
            @pl.when(i == 0)
            def _():
                rider.start(r_srcs, r_outs, scr[n_scr:])

        if plan.pre:
            plan.fill_conv_input(cur_ref, before_ref, pp_refs, u_ref, i)

        def col_loop(c, carry):
            c0 = c * LANE
            for r in range(plan.n_rs):
                res = plan.post(*plan.conv(cur_ref, before_ref, u_ref, w_ref, b_ref, i, r, c0))
                n = 0
                for (_, dt, bases), o_ref in zip(outs, o_refs):
                    for ob in bases:
                        o_ref[r * plan.rs:(r + 1) * plan.rs, _cols(ob, c0, LANE)] = res[n].astype(dt)
                        n += 1
            return carry

        lax.fori_loop(0, plan.n_cs, col_loop, 0)
        if rider:
            @pl.when(i == plan.n_tiles - 1)
            def _():
                rider.finish(r_srcs, r_outs, scr[n_scr:])

    return pl.pallas_call(
        body, name=name,
        out_shape=[jax.ShapeDtypeStruct((plan.t_len, c), d) for c, d, _ in outs] + (rider.out_shapes if rider else []),
        grid=(plan.n_tiles,), in_specs=specs + [HBM_SPEC] * r_n,
        out_specs=[pl.BlockSpec((plan.rows, c), lambda i: (i, 0)) for c, _, _ in outs] + [HBM_SPEC] * r_n,
        scratch_shapes=scratch + (rider.semaphores if rider else []),
        compiler_params=_cparams(("arbitrary",) if rider else ("parallel",)),
    )(*operands, *(rider.operands if rider else []))


def _conv_bwd(plan, cots, dx_dtype, name, rider=None):
    n_pp, n_c = len(plan.pre_params), len(cots)
    r_n = rider.n if rider else 0
    n_tiles, rows, rs, halo, k_taps = plan.n_tiles, plan.rows, plan.rs, plan.halo, plan.k_taps

    def tile_index(i):
        return n_tiles - 1 - i

    specs, operands = plan.in_specs(tile_index)
    for ct, _ in cots:
        specs.append(pl.BlockSpec((rows, ct.shape[1]), lambda i: (tile_index(i), 0)))
        operands.append(ct)
    mid_cols = plan.n_mid * plan.width
    out_shape = [jax.ShapeDtypeStruct(plan.x.shape, dx_dtype), jax.ShapeDtypeStruct(plan.w.shape, f32),
                 jax.ShapeDtypeStruct(plan.b.shape, f32)]
    out_shape += [jax.ShapeDtypeStruct(p.shape, f32) for p, _ in plan.pre_params]
    out_specs = [pl.BlockSpec((rows, plan.x.shape[1]), lambda i: (tile_index(i), 0)),
                 pl.BlockSpec(plan.w.shape, lambda i: (0, 0)), pl.BlockSpec(plan.b.shape, lambda i: (0, 0))]
    out_specs += [pl.BlockSpec(p.shape, lambda i: (0, 0)) for p, _ in plan.pre_params]
    w_cols = plan.w.shape[1]
    scratch = [pltpu.VMEM((rows + halo, mid_cols), f32),
               pltpu.VMEM((halo, mid_cols), f32),
               pltpu.VMEM(((k_taps + 1) * SUBLANE, w_cols), f32)]
    if plan.pre:
        scratch.append(pltpu.VMEM((halo + rows, mid_cols), f32))

    def body(*refs):
        cur_ref, before_ref, w_ref, b_ref = refs[:4]
        pp_refs = refs[4:4 + n_pp]
        c_refs = refs[4 + n_pp:4 + n_pp + n_c]
        pos = 4 + n_pp + n_c
        r_srcs = refs[pos:pos + r_n]
        pos += r_n
        dx_ref, dw_ref, db_ref = refs[pos:pos + 3]
        dpp_refs = refs[pos + 3:pos + 3 + n_pp]
        r_outs = refs[pos + 3 + n_pp:pos + 3 + n_pp + r_n]
        pos += 3 + n_pp + r_n
        g_ref, carry_ref, acc_ref = refs[pos:pos + 3]
        u_ref = refs[pos + 3] if plan.pre else None
        r_sems = refs[pos + (4 if plan.pre else 3):]
        i = pl.program_id(0)
        tile_id = tile_index(i)
        if rider:
            @pl.when(i == 0)
            def _():
                rider.start(r_srcs, r_outs, r_sems)

        @pl.when(i == 0)
        def _():
            carry_ref[...] = jnp.zeros_like(carry_ref)
            acc_ref[...] = jnp.zeros_like(acc_ref)
            for dp in dpp_refs:
                dp[...] = jnp.zeros_like(dp)

        g_ref[rows:rows + halo, :] = carry_ref[...]
        if plan.pre:
            plan.fill_conv_input(cur_ref, before_ref, pp_refs, u_ref, tile_id)

        def col_loop(c, carry):
            c0 = c * LANE
            for r in range(plan.n_rs):
                hcs = plan.conv(cur_ref, before_ref, u_ref, w_ref, b_ref, tile_id, r, c0)
                _, vjp = jax.vjp(lambda *a: tuple(plan.post(*a)), *hcs)
                cvals = [c_ref[r * rs:(r + 1) * rs, _cols(cb, c0, LANE)].astype(f32)
                         for (_, bases), c_ref in zip(cots, c_refs) for cb in bases]
                d_hcs = vjp(tuple(cvals))
                for j, mb in enumerate(plan.mid_bases):
                    g_ref[r * rs:(r + 1) * rs, _cols(j * plan.width, c0, LANE)] = d_hcs[j]
                    wc = _cols(mb, c0, LANE)
                    acc_ref[k_taps * SUBLANE:(k_taps + 1) * SUBLANE, wc] += _fold8(d_hcs[j])
                    for k in range(k_taps):
                        x_k = plan.tap(cur_ref, before_ref, u_ref, tile_id, r, j, k, c0)
                        acc_ref[k * SUBLANE:(k + 1) * SUBLANE, wc] += _fold8(d_hcs[j] * x_k)
            pps = plan.pre_strips(pp_refs, c0)
            for r in range(plan.n_rs):
                d_us = []
                for j, mb in enumerate(plan.mid_bases):
                    wc = _cols(mb, c0, LANE)
                    acc = None
                    for k in range(k_taps):
                        lo = r * rs + (k_taps - 1) - k
                        term = w_ref[k:k + 1, wc] * g_ref[lo:lo + rs, _cols(j * plan.width, c0, LANE)]
                        acc = term if acc is None else acc + term
                    d_us.append(acc)
                if plan.pre is None:
                    d_xs = d_us
                else:
                    xs = [cur_ref[r * rs:(r + 1) * rs, _cols(b, c0, LANE)].astype(f32) for b in plan.in_bases]
                    _, vjp_pre = jax.vjp(lambda *a: tuple(plan.pre(*a, valid=1.0)), *xs, *pps)
                    grads = vjp_pre(tuple(d_us))
                    d_xs = grads[:len(xs)]
                    n = len(xs)
                    for (_, bases), dp in zip(plan.pre_params, dpp_refs):
                        for pb in bases:
                            dp[:, _cols(pb, c0, LANE)] += grads[n]
                            n += 1
                for b, d_x in zip(plan.in_bases, d_xs):
                    dx_ref[r * rs:(r + 1) * rs, _cols(b, c0, LANE)] = d_x.astype(dx_dtype)
            return carry

        lax.fori_loop(0, plan.n_cs, col_loop, 0)
        carry_ref[...] = g_ref[0:halo, :]

        @pl.when(i == n_tiles - 1)
        def _():
            for k in range(k_taps):
                dw_ref[k:k + 1, :] = jnp.sum(acc_ref[k * SUBLANE:(k + 1) * SUBLANE, :], axis=0, keepdims=True)
            db_ref[...] = jnp.sum(acc_ref[k_taps * SUBLANE:(k_taps + 1) * SUBLANE, :], axis=0, keepdims=True)
            if rider:
                rider.finish(r_srcs, r_outs, r_sems)

    return pl.pallas_call(
        body, name=name, out_shape=out_shape + (rider.out_shapes if rider else []), grid=(n_tiles,),
        in_specs=specs + [HBM_SPEC] * r_n, out_specs=out_specs + [HBM_SPEC] * r_n,
        scratch_shapes=scratch + (rider.semaphores if rider else []), compiler_params=_cparams(("arbitrary",)),
    )(*operands, *(rider.operands if rider else []))


def _f_rms(h, g):
    return (_rms(h, g),)


def _f_rms_res(h, g, bz):
    hh = h + bz
    return _rms(hh, g), hh


@jax.custom_vjp
def _silu_gate(gate, val):
    return jax.nn.silu(gate) * val


def _silu_gate_fwd(gate, val):
    s = jax.nn.sigmoid(gate)
    return gate * s * val, (gate, val, s)


def _silu_gate_bwd(res, d):
    gate, val, s = res
    silu = gate * s
    return d * val * (s + silu * (1.0 - s)), d * silu


_silu_gate.defvjp(_silu_gate_fwd, _silu_gate_bwd)


def _post_ffn_gate(gate, val):
    return (_silu_gate(gate, val),)


def _post_silu(h):
    return (jax.nn.silu(h),)


def _post_identity(h):
    return (h,)


def _pre_glu(g_a, g_b, b_a, b_b, *, valid):
    return ((g_a + b_a) * jax.nn.sigmoid(g_b + b_b) * valid,)


def _f_ssd_dt(dtr, dtb):
    real = lax.broadcasted_iota(jnp.int32, (1, LANE), 1) < SSD_HEADS
    return (jnp.where(real, jax.nn.softplus(dtr + dtb), 0.0),)


def _f_ssd_post(y, z, g):
    return (_rms(y * jax.nn.silu(z), g),)


CONF_HALO = 32
FFN_STRIP_ROWS = 64
CONF_STRIP_ROWS = 128
SSD_STRIP_ROWS = 64


def _f_ln_silu(x, g, b):
    return (jax.nn.silu(_layer_norm(x, g, b)),)


def _f_lru(io_ext, in_b, cw, cb, ga_w, ga_b, gx_w, gx_b, lam, *, valid):
    rows = io_ext.shape[0] - SUBLANE
    io = (io_ext + in_b) * valid
    gate = io[SUBLANE:, :LRU_W]
    xr = _causal_taps(io[:, LRU_W:], cw, SUBLANE, rows) + cb
    rs, iis = [], []
    for blk in range(LRU_W // LRU_BLOCK):
        sl = slice(blk * LRU_BLOCK, (blk + 1) * LRU_BLOCK)
        xb = xr[:, sl]
        rs.append(jax.nn.sigmoid(_dot_nn(xb, ga_w[sl, :]) + ga_b[:, sl]))
        iis.append(jax.nn.sigmoid(_dot_nn(xb, gx_w[sl, :]) + gx_b[:, sl]))
    r = jnp.concatenate(rs, axis=1)
    ig = jnp.concatenate(iis, axis=1)
    log_a = -LRU_C * r * jax.nn.softplus(-lam)
    a = jnp.exp(log_a)
    bterm = jnp.sqrt(-_expm1(2.0 * log_a)) * (ig * xr)
    return a, bterm, jax.nn.gelu(gate)


def _f_sgu(z, in_b, ln_g, ln_b, sp_w, sp_bt):
    rows = z.shape[0]
    zz = jax.nn.gelu(z + in_b)
    u, v = zz[:, :SGU_HALF], zz[:, SGU_HALF:]
    v = _layer_norm(v, ln_g, ln_b)
    tri = lax.broadcasted_iota(jnp.int32, (SGU_CHUNK, SGU_CHUNK), 0) >= lax.broadcasted_iota(
        jnp.int32, (SGU_CHUNK, SGU_CHUNK), 1)
    gdim = SGU_HALF // SGU_GROUPS
    row_blocks = []
    for ci in range(rows // SGU_CHUNK):
        col_blocks = []
        for g in range(SGU_GROUPS):
            w = jnp.where(tri, sp_w[g * SGU_CHUNK:(g + 1) * SGU_CHUNK, :], 0.0)
            vb = v[ci * SGU_CHUNK:(ci + 1) * SGU_CHUNK, g * gdim:(g + 1) * gdim]
            col_blocks.append(_dot_nn(w, vb) + sp_bt[:, g:g + 1])
        row_blocks.append(jnp.concatenate(col_blocks, axis=1))
    mixed = row_blocks[0] if len(row_blocks) == 1 else jnp.concatenate(row_blocks, axis=0)
    return (u * mixed,)


HEADS_PER_GROUP = 4
GROUP_COLS = 256
HEAD_DIM = 64


def _ssd_group(x, bm, cm, dt, st, a_log, dsk, g):
    q = x.shape[0]
    tri = lax.broadcasted_iota(jnp.int32, (q, q), 0) >= lax.broadcasted_iota(jnp.int32, (q, q), 1)
    d_a = dt * (-jnp.exp(a_log))
    acs = jnp.dot(tri.astype(f32), d_a, precision=HIGHEST, preferred_element_type=f32)
    acs_t = acs.T
    lane = lax.broadcasted_iota(jnp.int32, (1, LANE), 1)
    sub = lax.broadcasted_iota(jnp.int32, (LANE, 1), 0)
    col_idx = lax.broadcasted_iota(jnp.int32, (1, GROUP_COLS), 1)
    last_row = (lax.broadcasted_iota(jnp.int32, (q, 1), 0) == q - 1).astype(f32)
    cb = _dot_nt(cm, bm)
    y = jnp.zeros((q, GROUP_COLS), f32)
    e_in = jnp.zeros((q, GROUP_COLS), f32)
    d_end = jnp.zeros((q, GROUP_COLS), f32)
    d_last = jnp.zeros((1, GROUP_COLS), f32)
    d_skip = jnp.zeros((1, GROUP_COLS), f32)
    for j in range(HEADS_PER_GROUP):
        head = HEADS_PER_GROUP * g + j
        on_lane = (lane == head).astype(f32)
        on_sub = (sub == head).astype(f32)
        col = jnp.sum(acs * on_lane, axis=1, keepdims=True)
        row = jnp.sum(acs_t * on_sub, axis=0, keepdims=True)
        dtc = jnp.sum(dt * on_lane, axis=1, keepdims=True)
        last = jnp.sum(col * last_row, axis=0, keepdims=True)
        dsk_j = jnp.sum(dsk * on_lane, axis=1, keepdims=True)
        decay = jnp.where(tri, jnp.exp(jnp.where(tri, col - row, 0.0)), 0.0)
        mine = jnp.logical_and(col_idx >= j * HEAD_DIM, col_idx < (j + 1) * HEAD_DIM)
        y = y + _dot_nn(cb * decay, jnp.where(mine, x * dtc, 0.0))
        e_in = e_in + jnp.where(mine, jnp.exp(col), 0.0)
        d_end = d_end + jnp.where(mine, jnp.exp(last - col) * dtc, 0.0)
        d_last = d_last + jnp.where(mine, jnp.exp(last), 0.0)
        d_skip = d_skip + jnp.where(mine, dsk_j, 0.0)
    y = y + _dot_nn(cm, st) * e_in + x * d_skip
    st_new = st * d_last + _dot_tn(bm, x * d_end)
    return y, st_new


GROUPS_PER_STEP = 2


def _ssd_specs(rev, nc):
    def ch(c):
        return nc - 1 - c if rev else c

    gps = GROUPS_PER_STEP
    x_spec = pl.BlockSpec((SSD_CHUNK, gps * GROUP_COLS), lambda c, g: (ch(c), g))
    b_spec = pl.BlockSpec((SSD_CHUNK, gps * LANE), lambda c, g: (ch(c), SSD_D_INNER // (gps * LANE) + g))
    c_spec = pl.BlockSpec((SSD_CHUNK, gps * LANE), lambda c, g: (ch(c), (SSD_D_INNER + SSD_BC) // (gps * LANE) + g))
    dt_spec = pl.BlockSpec((SSD_CHUNK, LANE), lambda c, g: (ch(c), 0))
    row_spec = pl.BlockSpec((1, LANE), lambda c, g: (0, 0))
    st_spec = pl.BlockSpec((1, gps, LANE, GROUP_COLS), lambda c, g: (ch(c), g, 0, 0))
    wide_spec = pl.BlockSpec((SSD_CHUNK, SSD_CONV_DIM), lambda c, g: (ch(c), 0))
    return x_spec, b_spec, c_spec, dt_spec, row_spec, st_spec, wide_spec


def _ssd_fwd(xc, dt, a_log, dsk, gather=()):
    t_len = xc.shape[0]
    nc = t_len // SSD_CHUNK
    gps = GROUPS_PER_STEP
    n_gp = SSD_GROUPS // gps
    n_g = len(gather)
    x_spec, b_spec, c_spec, dt_spec, row_spec, st_spec, _ = _ssd_specs(False, nc)

    def body(*refs):
        x_ref, b_ref, c_ref, dt_ref, al_ref, dk_ref = refs[:6]
        g_srcs = refs[6:6 + n_g]
        y_ref, st_out_ref = refs[6 + n_g:8 + n_g]
        g_outs = refs[8 + n_g:8 + 2 * n_g]
        st_ref = refs[8 + 2 * n_g]
        g_sems = refs[9 + 2 * n_g:]
        c, gp = pl.program_id(0), pl.program_id(1)
        if n_g:
            @pl.when(jnp.logical_and(c == 0, gp == 0))
            def _():
                _gather_start(g_srcs, g_outs, *g_sems)

        for q in range(gps):
            g = gp * gps + q

            @pl.when(c == 0)
            def _():
                st_ref[g] = jnp.zeros((LANE, GROUP_COLS), f32)

            st = st_ref[g]
            st_out_ref[0, q] = st
            xq = slice(q * GROUP_COLS, (q + 1) * GROUP_COLS)
            bq = slice(q * LANE, (q + 1) * LANE)
            y, st_new = _ssd_group(x_ref[:, xq], b_ref[:, bq], c_ref[:, bq], dt_ref[...], st, al_ref[...],
                                   dk_ref[...], g)
            y_ref[:, xq] = y
            st_ref[g] = st_new

        if n_g:
            @pl.when(jnp.logical_and(c == nc - 1, gp == n_gp - 1))
            def _():
                _gather_finish(g_srcs, g_outs, *g_sems)

    res = pl.pallas_call(
        body, name="ssd_scan_fwd",
        out_shape=[jax.ShapeDtypeStruct((t_len, SSD_D_INNER), f32),
                   jax.ShapeDtypeStruct((nc, SSD_GROUPS, LANE, GROUP_COLS), f32)] + _gather_out_shapes(gather),
        grid=(nc, n_gp), in_specs=[x_spec, b_spec, c_spec, dt_spec, row_spec, row_spec] + [HBM_SPEC] * n_g,
        out_specs=[x_spec, st_spec] + [HBM_SPEC] * n_g,
        scratch_shapes=[pltpu.VMEM((SSD_GROUPS, LANE, GROUP_COLS), f32)] + (_gather_semaphores(n_g) if n_g else []),
        compiler_params=_cparams(("arbitrary", "arbitrary")),
    )(xc, xc, xc, dt, a_log, dsk, *gather)
    return res[0], res[1], list(res[2:])


def _ssd_bwd(xc, dt, a_log, dsk, states, dy, to_chips=()):
    t_len = xc.shape[0]
    nc = t_len // SSD_CHUNK
    gps = GROUPS_PER_STEP
    n_gp = SSD_GROUPS // gps
    n_s = len(to_chips)
    x_spec, b_spec, c_spec, dt_spec, row_spec, st_spec, wide_spec = _ssd_specs(True, nc)

    def body(*refs):
        x_ref, b_ref, c_ref, dt_ref, al_ref, dk_ref, st_in_ref, dy_ref = refs[:8]
        s_srcs = refs[8:8 + n_s]
        dxc_ref, ddt_ref, dal_ref, ddk_ref = refs[8 + n_s:12 + n_s]
        s_outs = refs[12 + n_s:12 + 2 * n_s]
        dst_ref = refs[12 + 2 * n_s]
        s_sems = refs[13 + 2 * n_s:]
        c, gp = pl.program_id(0), pl.program_id(1)

        @pl.when(jnp.logical_and(c == 0, gp == 0))
        def _():
            dal_ref[...] = jnp.zeros_like(dal_ref)
            ddk_ref[...] = jnp.zeros_like(ddk_ref)
            for cp in _to_chips_copies(s_srcs, s_outs, *s_sems) if n_s else []:
                cp.start()

        @pl.when(gp == 0)
        def _():
            ddt_ref[...] = jnp.zeros_like(ddt_ref)

        for q in range(gps):
            g = gp * gps + q

            @pl.when(c == 0)
            def _():
                dst_ref[g] = jnp.zeros((LANE, GROUP_COLS), f32)

            xq = slice(q * GROUP_COLS, (q + 1) * GROUP_COLS)
            bq = slice(q * LANE, (q + 1) * LANE)
            _, vjp = jax.vjp(lambda *args: _ssd_group(*args, g), x_ref[:, xq], b_ref[:, bq], c_ref[:, bq],
                             dt_ref[...], st_in_ref[0, q], al_ref[...], dk_ref[...])
            dx, db, dc, ddt, dst, dal, ddk = vjp((dy_ref[:, xq], dst_ref[g]))
            dxc_ref[:, pl.ds(pl.multiple_of(g * GROUP_COLS, GROUP_COLS), GROUP_COLS)] = dx
            dxc_ref[:, pl.ds(pl.multiple_of(SSD_D_INNER + g * LANE, LANE), LANE)] = db
            dxc_ref[:, pl.ds(pl.multiple_of(SSD_D_INNER + SSD_BC + g * LANE, LANE), LANE)] = dc
            ddt_ref[...] += ddt
            dst_ref[g] = dst
            dal_ref[...] += dal
            ddk_ref[...] += ddk

        if n_s:
            @pl.when(jnp.logical_and(c == nc - 1, gp == n_gp - 1))
            def _():
                for cp in _to_chips_copies(s_srcs, s_outs, *s_sems):
                    cp.wait()

    res = pl.pallas_call(
        body, name="ssd_scan_bwd",
        out_shape=[jax.ShapeDtypeStruct((t_len, SSD_CONV_DIM), f32), jax.ShapeDtypeStruct((t_len, LANE), f32),
                   jax.ShapeDtypeStruct((1, LANE), f32), jax.ShapeDtypeStruct((1, LANE), f32)]
        + _to_chips_out_shapes(to_chips),
        grid=(nc, n_gp),
        in_specs=[x_spec, b_spec, c_spec, dt_spec, row_spec, row_spec, st_spec, x_spec] + [HBM_SPEC] * n_s,
        out_specs=[wide_spec, dt_spec, row_spec, row_spec] + [HBM_SPEC] * n_s,
        scratch_shapes=[pltpu.VMEM((SSD_GROUPS, LANE, GROUP_COLS), f32)] + (_to_chips_semaphores(n_s) if n_s else []),
        compiler_params=_cparams(("arbitrary", "arbitrary")),
    )(xc, xc, xc, dt, a_log, dsk, states, dy, *to_chips)
    return res[0], res[1], res[2], res[3], list(res[4:])


LRU_ROWS = 256


def _lru_fwd(a, b, gg):
    t_len, cols = a.shape
    rows = min(LRU_ROWS, t_len)
    spec = pl.BlockSpec((rows, cols), lambda i: (i, 0))

    def body(a_ref, b_ref, g_ref, y_ref, h_ref, carry):
        i = pl.program_id(0)

        @pl.when(i == 0)
        def _():
            carry[...] = jnp.zeros_like(carry)

        av, bv = a_ref[...], b_ref[...]
        row = lax.broadcasted_iota(jnp.int32, av.shape, 0)
        s = 1
        while s < rows:
            a_prev = pltpu.roll(av, s, axis=0)
            b_prev = pltpu.roll(bv, s, axis=0)
            m = row >= s
            bv = jnp.where(m, av * b_prev + bv, bv)
            av = jnp.where(m, av * a_prev, av)
            s *= 2
        h = av * carry[0:1, :] + bv
        h_ref[...] = h
        y_ref[...] = g_ref[...] * h
        carry[0:1, :] = h[rows - 1:rows, :]

    return pl.pallas_call(
        body, name="lru_scan_fwd",
        out_shape=[jax.ShapeDtypeStruct((t_len, cols), f32), jax.ShapeDtypeStruct((t_len, cols), f32)],
        grid=(t_len // rows,), in_specs=[spec, spec, spec], out_specs=[spec, spec],
        scratch_shapes=[pltpu.VMEM((SUBLANE, cols), f32)],
        compiler_params=_cparams(("arbitrary",)),
    )(a, b, gg)


def _lru_bwd(dy, gg, a, h):
    t_len, cols = a.shape
    rows = min(LRU_ROWS, t_len)
    n_tiles = t_len // rows
    per = rows // SUBLANE
    spec = pl.BlockSpec((rows, cols), lambda i: (n_tiles - 1 - i, 0))
    prev_spec = pl.BlockSpec((SUBLANE, cols), lambda i: (jnp.maximum((n_tiles - 1 - i) * per - 1, 0), 0))

    def body(dy_ref, g_ref, a_ref, h_ref, hp_ref, da_ref, db_ref, dg_ref, carry_dh, carry_a):
        i = pl.program_id(0)
        tile_id = n_tiles - 1 - i

        @pl.when(i == 0)
        def _():
            carry_dh[...] = jnp.zeros_like(carry_dh)
            carry_a[...] = jnp.zeros_like(carry_a)

        av, hv, dyv = a_ref[...], h_ref[...], dy_ref[...]
        row = lax.broadcasted_iota(jnp.int32, av.shape, 0)
        dg_ref[...] = dyv * hv
        bv = dyv * g_ref[...]
        cv = jnp.where(row == rows - 1, carry_a[0:1, :], pltpu.roll(av, rows - 1, axis=0))
        s = 1
        while s < rows:
            c_next = pltpu.roll(cv, rows - s, axis=0)
            b_next = pltpu.roll(bv, rows - s, axis=0)
            m = row < rows - s
            bv = jnp.where(m, cv * b_next + bv, bv)
            cv = jnp.where(m, cv * c_next, cv)
            s *= 2
        dh = cv * carry_dh[0:1, :] + bv
        h_before = jnp.where(tile_id > 0, hp_ref[SUBLANE - 1:SUBLANE, :], jnp.zeros((1, cols), f32))
        h_prev = jnp.where(row == 0, h_before, pltpu.roll(hv, 1, axis=0))
        da_ref[...] = dh * h_prev
        db_ref[...] = dh
        carry_dh[0:1, :] = dh[0:1, :]
        carry_a[0:1, :] = av[0:1, :]

    return pl.pallas_call(
        body, name="lru_scan_bwd",
        out_shape=[jax.ShapeDtypeStruct((t_len, cols), f32)] * 3,
        grid=(n_tiles,), in_specs=[spec, spec, spec, spec, prev_spec], out_specs=[spec, spec, spec],
        scratch_shapes=[pltpu.VMEM((SUBLANE, cols), f32), pltpu.VMEM((SUBLANE, cols), f32)],
        compiler_params=_cparams(("arbitrary",)),
    )(dy, gg, a, h, h)


def _loss_head(h, target, g):
    t_len = h.shape[0]
    rows = min(512, t_len)

    def f(hv, gv, tv):
        err = _rms(hv, gv) - tv
        return 0.5 * jnp.sum(jnp.mean(err * err, axis=-1, keepdims=True), axis=0, keepdims=True)

    def body(h_ref, t_ref, g_ref, dh_ref, dg_ref, loss_ref):
        i = pl.program_id(0)

        @pl.when(i == 0)
        def _():
            dg_ref[...] = jnp.zeros_like(dg_ref)
            loss_ref[...] = jnp.zeros_like(loss_ref)

        tv = t_ref[...]
        part, vjp = jax.vjp(lambda hv, gv: f(hv, gv, tv), h_ref[...], g_ref[...])
        dh, dg = vjp(jnp.ones((1, 1), f32))
        dh_ref[...] = dh
        dg_ref[...] += dg
        loss_ref[...] += jnp.broadcast_to(part, loss_ref.shape)

    spec = pl.BlockSpec((rows, D_MODEL), lambda i: (i, 0))
    return pl.pallas_call(
        body, name="loss_head",
        out_shape=[jax.ShapeDtypeStruct((t_len, D_MODEL), f32), jax.ShapeDtypeStruct((1, D_MODEL), f32),
                   jax.ShapeDtypeStruct((1, LANE), f32)],
        grid=(t_len // rows,), in_specs=[spec, spec, pl.BlockSpec((1, D_MODEL), lambda i: (0, 0))],
        out_specs=[spec, pl.BlockSpec((1, D_MODEL), lambda i: (0, 0)), pl.BlockSpec((1, LANE), lambda i: (0, 0))],
        compiler_params=_cparams(("arbitrary",)),
    )(h, target, g)


def _as2d(a):
    return a.reshape((-1, a.shape[-1])) if a.ndim > 1 else a.reshape((1, -1))


def _row_block(rows, cols, bytes_cap=1 << 20):
    if rows * cols * 4 <= bytes_cap or rows % SUBLANE:
        return rows
    return _tile(rows, max(SUBLANE, (bytes_cap // (cols * 4)) // SUBLANE * SUBLANE), SUBLANE)


def _adamw(w, g, m, v, name):
    shape = w.shape
    w2, g2, m2, v2 = _as2d(w), _as2d(g), _as2d(m), _as2d(v)
    rows, cols = w2.shape
    rb = _row_block(rows, cols)

    def body(w_ref, g_ref, m_ref, v_ref, d_ref, nm_ref, nv_ref):
        gv = g_ref[...]
        nm = ADAM_B1 * m_ref[...] + (1.0 - ADAM_B1) * gv
        nv = ADAM_B2 * v_ref[...] + (1.0 - ADAM_B2) * jnp.square(gv)
        m_hat = nm / (1.0 - ADAM_B1 ** ADAM_STEP)
        v_hat = nv / (1.0 - ADAM_B2 ** ADAM_STEP)
        d_ref[...] = -ADAM_LR * (m_hat / (jnp.sqrt(v_hat) + ADAM_EPS) + ADAM_WD * w_ref[...])
        nm_ref[...] = nm
        nv_ref[...] = nv

    spec = pl.BlockSpec((rb, cols), lambda i: (i, 0))
    d, nm, nv = pl.pallas_call(
        body, name=name, out_shape=[jax.ShapeDtypeStruct((rows, cols), f32)] * 3,
        grid=(rows // rb,), in_specs=[spec] * 4, out_specs=[spec] * 3,
        compiler_params=_cparams(("parallel",)),
    )(w2, g2, m2, v2)
    return d.reshape(shape), nm.reshape(shape), nv.reshape(shape)


def _sum_with_sibling(g_halves, theirs, c_idx):
    n_sh, _, rows, cols = g_halves.shape
    rb = _tile(rows, 512, 2 * SUBLANE)

    def body(c_ref, mine_ref, theirs_ref, o_ref):
        o_ref[...] = (mine_ref[...] + theirs_ref[...]).astype(bf16)

    grid_spec = pltpu.PrefetchScalarGridSpec(
        num_scalar_prefetch=1, grid=(n_sh, rows // rb),
        in_specs=[pl.BlockSpec((None, None, rb, cols), lambda k, i, c_ref: (k, c_ref[0], i, 0)),
                  pl.BlockSpec((None, rb, cols), lambda k, i, c_ref: (k, i, 0))],
        out_specs=pl.BlockSpec((None, rb, cols), lambda k, i, c_ref: (k, i, 0)))
    return pl.pallas_call(
        body, name="grad_sum_sibling", out_shape=jax.ShapeDtypeStruct((n_sh, rows, cols), bf16),
        grid_spec=grid_spec, compiler_params=_cparams(("parallel", "parallel")),
    )(c_idx, g_halves, theirs)


def _sum_chips(partial, received, k_idx):
    _, rows, cols = partial.shape
    rb = _tile(rows, 512, 2 * SUBLANE)

    def body(k_ref, mine_ref, r_ref, o_ref):
        acc = mine_ref[...].astype(f32)
        for j in range(N_CHIPS - 1):
            acc = acc + r_ref[j].astype(f32)
        o_ref[...] = acc

    grid_spec = pltpu.PrefetchScalarGridSpec(
        num_scalar_prefetch=1, grid=(rows // rb,),
        in_specs=[pl.BlockSpec((None, rb, cols), lambda i, k_ref: (k_ref[0], i, 0)),
                  pl.BlockSpec((N_CHIPS - 1, rb, cols), lambda i, k_ref: (0, i, 0))],
        out_specs=pl.BlockSpec((rb, cols), lambda i, k_ref: (i, 0)))
    return pl.pallas_call(
        body, name="grad_sum_chips", out_shape=jax.ShapeDtypeStruct((rows, cols), f32),
        grid_spec=grid_spec, compiler_params=_cparams(("parallel",)),
    )(k_idx, partial, received)


HBM_SPEC = pl.BlockSpec(memory_space=pltpu.HBM)
CHIP_FLIPS = ((0, 1), (1, 0), (1, 1))


def _position():
    return lax.axis_index("x"), lax.axis_index("y"), lax.axis_index("c")


def _own_slot(gathered, mine, index):
    return [lax.dynamic_update_index_in_dim(g, m, index, 0) for g, m in zip(gathered, mine)]


def _gather_weights(blocks):
    n = len(blocks)

    def body(*refs):
        srcs, outs = refs[:n], refs[n:2 * n]
        send_sems, recv_sems = refs[2 * n:]
        _gather_start(srcs, outs, send_sems, recv_sems)
        _gather_finish(srcs, outs, send_sems, recv_sems)

    return pl.pallas_call(
        body, name="gather_weights", out_shape=_gather_out_shapes(blocks),
        in_specs=[HBM_SPEC] * n, out_specs=[HBM_SPEC] * n, scratch_shapes=_gather_semaphores(n),
    )(*blocks)


def _gather_out_shapes(blocks):
    return [jax.ShapeDtypeStruct((N_CHIPS,) + b.shape, b.dtype) for b in blocks]


def _gather_semaphores(n):
    n_sem = 2 * len(CHIP_FLIPS) * n
    return [pltpu.SemaphoreType.DMA((n_sem,)), pltpu.SemaphoreType.DMA((n_sem,))]


def _gather_copies(srcs, outs, send_sems, recv_sems):
    n_far = len(CHIP_FLIPS)
    x, y, c = _position()
    k = 2 * x + y
    first, passed = [], []
    for a in range(len(srcs)):
        for j, (fx, fy) in enumerate(CHIP_FLIPS):
            s = a * 2 * n_far + j
            kk = 2 * (x ^ fx) + (y ^ fy)
            first.append(pltpu.make_async_remote_copy(
                src_ref=srcs[a].at[c], dst_ref=outs[a].at[k, c], send_sem=send_sems.at[s],
                recv_sem=recv_sems.at[s], device_id=(x ^ fx, y ^ fy, c), device_id_type=MESH))
            passed.append(pltpu.make_async_remote_copy(
                src_ref=outs[a].at[kk, c], dst_ref=outs[a].at[kk, c], send_sem=send_sems.at[s + n_far],
                recv_sem=recv_sems.at[s + n_far], device_id=(x, y, 1 - c), device_id_type=MESH))
    return first, passed


def _gather_start(srcs, outs, send_sems, recv_sems):
    first, _ = _gather_copies(srcs, outs, send_sems, recv_sems)
    for cp in first:
        cp.start()


def _gather_finish(srcs, outs, send_sems, recv_sems):
    first, passed = _gather_copies(srcs, outs, send_sems, recv_sems)
    for arrived, onward in zip(first, passed):
        arrived.wait_recv()
        onward.start()
    for cp in passed:
        cp.wait_recv()
    for cp in first + passed:
        cp.wait_send()


def _swap_with_sibling(grads):
    rider = _swap_rider(grads)
    n = len(grads)

    def body(*refs):
        rider.start(refs[:n], refs[n:2 * n], refs[2 * n:])
        rider.finish(refs[:n], refs[n:2 * n], refs[2 * n:])

    return pl.pallas_call(
        body, name="grad_swap_sibling", out_shape=rider.out_shapes,
        in_specs=[HBM_SPEC] * n, out_specs=[HBM_SPEC] * n, scratch_shapes=rider.semaphores,
    )(*grads)


class _Rider:
    def __init__(self, operands, out_shapes, semaphores, start, finish):
        self.operands, self.out_shapes, self.semaphores = list(operands), list(out_shapes), list(semaphores)
        self.start, self.finish = start, finish
        self.n = len(self.operands)


def _swap_copies(srcs, outs, send_sems, recv_sems):
    x, y, c = _position()
    copies = []
    for a in range(len(srcs)):
        for kk in range(N_CHIPS):
            s = a * N_CHIPS + kk
            copies.append(pltpu.make_async_remote_copy(
                src_ref=srcs[a].at[kk, 1 - c], dst_ref=outs[a].at[kk], send_sem=send_sems.at[s],
                recv_sem=recv_sems.at[s], device_id=(x, y, 1 - c), device_id_type=MESH))
    return copies


def _swap_rider(grads):
    n_sem = N_CHIPS * len(grads)

    def start(srcs, outs, sems):
        for cp in _swap_copies(srcs, outs, *sems):
            cp.start()

    def finish(srcs, outs, sems):
        for cp in _swap_copies(srcs, outs, *sems):
            cp.wait()

    return _Rider(grads, [jax.ShapeDtypeStruct((N_CHIPS,) + g.shape[2:], g.dtype) for g in grads],
                  [pltpu.SemaphoreType.DMA((n_sem,)), pltpu.SemaphoreType.DMA((n_sem,))], start, finish)


def _gather_rider(blocks):
    def start(srcs, outs, sems):
        _gather_start(srcs, outs, *sems)

    def finish(srcs, outs, sems):
        _gather_finish(srcs, outs, *sems)

    return _Rider(blocks, _gather_out_shapes(blocks), _gather_semaphores(len(blocks)), start, finish)


def _send_to_chips(partials):
    n = len(partials)

    def body(*refs):
        srcs, outs = refs[:n], refs[n:2 * n]
        send_sems, recv_sems = refs[2 * n:]
        for cp in _to_chips_copies(srcs, outs, send_sems, recv_sems):
            cp.start()
        for cp in _to_chips_copies(srcs, outs, send_sems, recv_sems):
            cp.wait()

    return pl.pallas_call(
        body, name="grad_to_chips", out_shape=_to_chips_out_shapes(partials),
        in_specs=[HBM_SPEC] * n, out_specs=[HBM_SPEC] * n, scratch_shapes=_to_chips_semaphores(n),
    )(*partials)


def _to_chips_out_shapes(partials):
    return [jax.ShapeDtypeStruct((len(CHIP_FLIPS),) + p.shape[1:], p.dtype) for p in partials]


def _to_chips_semaphores(n):
    n_sem = len(CHIP_FLIPS) * n
    return [pltpu.SemaphoreType.DMA((n_sem,)), pltpu.SemaphoreType.DMA((n_sem,))]


def _to_chips_copies(srcs, outs, send_sems, recv_sems):
    n_far = len(CHIP_FLIPS)
    x, y, c = _position()
    copies = []
    for a in range(len(srcs)):
        for j, (fx, fy) in enumerate(CHIP_FLIPS):
            s = a * n_far + j
            kk = 2 * (x ^ fx) + (y ^ fy)
            copies.append(pltpu.make_async_remote_copy(
                src_ref=srcs[a].at[kk], dst_ref=outs[a].at[j], send_sem=send_sems.at[s],
                recv_sem=recv_sems.at[s], device_id=(x ^ fx, y ^ fy, c), device_id_type=MESH))
    return copies


def _join_halves(halves):
    n = len(halves)

    def body(*refs):
        srcs, outs = refs[:n], refs[n:2 * n]
        send_sems, recv_sems = refs[2 * n:]
        x, y, c = _position()
        copies = []
        for a in range(n):
            cp = pltpu.make_async_remote_copy(
                src_ref=srcs[a], dst_ref=outs[a].at[c], send_sem=send_sems.at[a], recv_sem=recv_sems.at[a],
                device_id=(x, y, 1 - c), device_id_type=MESH)
            cp.start()
            copies.append(cp)
        for cp in copies:
            cp.wait()

    return pl.pallas_call(
        body, name="grad_join_halves",
        out_shape=[jax.ShapeDtypeStruct((2,) + h.shape, h.dtype) for h in halves],
        in_specs=[HBM_SPEC] * n, out_specs=[HBM_SPEC] * n,
        scratch_shapes=[pltpu.SemaphoreType.DMA((n,)), pltpu.SemaphoreType.DMA((n,))],
    )(*halves)


def _all_sum_small(vec):
    rows, cols = vec.shape

    def body(v_ref, o_ref, buf, send_sems, recv_sems):
        x, y, c = _position()
        me = 4 * x + 2 * y + c
        buf[me] = v_ref[...]
        copies = []
        for m in range(1, N_DEV):
            fx, fy, fc = (m >> 2) & 1, (m >> 1) & 1, m & 1
            cp = pltpu.make_async_remote_copy(
                src_ref=v_ref, dst_ref=buf.at[me], send_sem=send_sems.at[m - 1], recv_sem=recv_sems.at[m - 1],
                device_id=(x ^ fx, y ^ fy, c ^ fc), device_id_type=MESH)
            cp.start()
            copies.append(cp)
        for cp in copies:
            cp.wait()
        acc = buf[0]
        for d in range(1, N_DEV):
            acc = acc + buf[d]
        o_ref[...] = acc

    return pl.pallas_call(
        body, name="all_sum_small", out_shape=jax.ShapeDtypeStruct((rows, cols), f32),
        in_specs=[pl.BlockSpec(memory_space=pltpu.VMEM)], out_specs=pl.BlockSpec(memory_space=pltpu.VMEM),
        scratch_shapes=[pltpu.VMEM((N_DEV, rows, cols), f32), pltpu.SemaphoreType.DMA((N_DEV - 1,)),
                        pltpu.SemaphoreType.DMA((N_DEV - 1,))],
        compiler_params=_cparams(),
    )(vec)


FLAT_QUANTUM = 2 * 2 * SUBLANE * FLAT_COLS


def _pack(arrays, dtype):
    flat = jnp.concatenate([a.astype(dtype).reshape(-1) for a in arrays])
    n = flat.shape[0]
    n_pad = -(-n // FLAT_QUANTUM) * FLAT_QUANTUM
    return jnp.pad(flat, (0, n_pad - n))


def _unpack(flat, shapes):
    out, off = [], 0
    for s in shapes:
        n = int(np.prod(s))
        out.append(flat[..., off:off + n].reshape(flat.shape[:-1] + tuple(s)))
        off += n
    return out


def _full_from_shards(stacked, axis):
    return jnp.concatenate([stacked[k] for k in range(N_CHIPS)], axis=axis)


def _shards_of(full, axis):
    return jnp.stack(jnp.split(full, N_CHIPS, axis=axis))


def _ffn_fwd(h, u, p, next_gain):
    a = _mm_w(u, p['up'], 'nn', "ffn_up")
    gated = _conv_fwd(_ffn_conv_plan(a, p), [(FFN_H, bf16, (0,))], "ffn_gate")[0]
    h_out, u_next = _mm_normed(gated, p['down'], 'nn', "ffn_down", add=h, rms_gain=next_gain)
    return h_out, u_next, (h, u, a, gated)


def _ffn_conv_plan(a, p):
    both = (0, FFN_H)
    return _ConvPlan(a, p['cw'], p['cb'], in_bases=both, mid_bases=both, width=FFN_H, rows=256, rs=FFN_STRIP_ROWS,
                     post=_post_ffn_gate)


def _ffn_bwd(dh_out, p, saved, bias_zero, make_rider=None):
    h, u, a, gated = saved
    d_gated = _mm(dh_out, p['down'], 'nt', "ffn_down_dx", out_dtype=bf16)
    d_down = _mm(gated, dh_out, 'tn', "ffn_down_dw")
    rider = make_rider(d_down) if make_rider else None
    res = _conv_bwd(_ffn_conv_plan(a, p), [(d_gated, (0,))], bf16, "ffn_gate_bwd", rider=rider)
    da, d_cw, d_cb = res[:3]
    d_up = _mm(u, da, 'tn', "ffn_up_dw", out_cols_sharded=True)
    du = _mm_w(da, p['up'], 'nt', "ffn_up_dx", out_dtype=bf16)
    dh, d_g, d_bias = _row_bwd(_f_rms_res, [h], [p['g'], bias_zero], [du, dh_out], rows=512, name="ffn_norm_bwd")
    return dh, {'g': d_g, 'up': d_up, 'down': d_down, 'cw': d_cw, 'cb': d_cb}, d_bias, list(res[3:])


def _mixer_norm_bwd(h, g, du, dh_res, name):
    def f(hv, gv):
        return _rms(hv, gv), hv

    dh, d_g = _row_bwd(f, [h], [g], [du, dh_res], rows=512, name=name)
    return dh, d_g


def _ssd_layer_fwd(h, u, p, next_gain, gather=()):
    z = _mm(u, p['w_z'], 'nn', "ssd_in_z")
    xbc = _mm(u, p['w_xbc'], 'nn', "ssd_in_xbc")
    dtr = _mm(u, p['w_dt'], 'nn', "ssd_in_dt")
    xc = _conv_fwd(_ssd_conv_plan(xbc, p), [(SSD_CONV_DIM, f32, (0,))], "ssd_conv")[0]
    dt = _row_fwd(_f_ssd_dt, [dtr], [p['dtb']], [(LANE, f32)], rows=1024, name="ssd_dt")[0]
    y, states, gathered = _ssd_fwd(xc, dt, p['a_log'], p['dsk'], gather)
    yn = _row_fwd(_f_ssd_post, [y, z], [p['norm']], [(SSD_D_INNER, bf16)], rows=256, name="ssd_gate_norm")[0]
    h_out, u_next = _mm_normed(yn, p['out'], 'nn', "ssd_out", add=h, rms_gain=next_gain)
    return h_out, u_next, (h, u, z, xbc, dtr, xc, dt, states, y, yn), gathered


def _ssd_conv_plan(xbc, p):
    return _ConvPlan(xbc, p['cw'], p['cb'], in_bases=(0,), mid_bases=(0,), width=SSD_CONV_DIM, rows=256,
                     rs=SSD_STRIP_ROWS, post=_post_silu)


def _ssd_layer_bwd(dh_out, p, saved, to_chips=()):
    h, u, z, xbc, dtr, xc, dt, states, y, yn = saved
    d_yn = _mm(dh_out, p['out'], 'nt', "ssd_out_dx", out_dtype=bf16)
    d_out = _mm(yn, dh_out, 'tn', "ssd_out_dw")
    dy, dz, d_norm = _row_bwd(_f_ssd_post, [y, z], [p['norm']], [d_yn], rows=256, name="ssd_gate_norm_bwd",
                              tile_dtypes=[f32, bf16])
    dxc, ddt, d_alog, d_dsk, received = _ssd_bwd(xc, dt, p['a_log'], p['dsk'], states, dy, to_chips)
    dxbc, d_cw, d_cb = _conv_bwd(_ssd_conv_plan(xbc, p), [(dxc, (0,))], bf16, "ssd_conv_bwd")
    ddtr, d_dtb = _row_bwd(_f_ssd_dt, [dtr], [p['dtb']], [ddt], rows=1024, name="ssd_dt_bwd", tile_dtypes=[bf16])
    d_wz = _mm(u, dz, 'tn', "ssd_in_z_dw")
    d_wxbc = _mm(u, dxbc, 'tn', "ssd_in_xbc_dw")
    d_wdt = _mm(u, ddtr, 'tn', "ssd_in_dt_dw")
    du = _mm(dz, p['w_z'], 'nt', "ssd_in_z_dx")
    du = _mm(dxbc, p['w_xbc'], 'nt', "ssd_in_xbc_dx", add=du)
    du = _mm(ddtr, p['w_dt'], 'nt', "ssd_in_dt_dx", add=du, out_dtype=bf16)
    dh, d_g = _mixer_norm_bwd(h, p['g'], du, dh_out, "ssd_norm_bwd")
    grads = {'g': d_g, 'w_z': d_wz, 'w_xbc': d_wxbc, 'w_dt': d_wdt, 'cw': d_cw, 'cb': d_cb, 'dtb': d_dtb,
             'a_log': d_alog, 'dsk': d_dsk, 'norm': d_norm, 'out': d_out}
    return dh, grads, received


def _conf_layer_fwd(h, u, p, next_gain, rider=None):
    g2 = _mm_w(u, p['pw1'], 'nn', "conf_pw1")
    res = _conv_fwd(_conf_conv_plan(g2, p), [(D_MODEL, f32, (0,))], "conf_conv", rider=rider)
    conv = res[0]
    s = _row_fwd(_f_ln_silu, [conv], [p['ln_g'], p['ln_b']], [(D_MODEL, bf16)], rows=256, name="conf_ln")[0]
    h_out, u_next = _mm_normed(s, p['pw2'], 'nn', "conf_pw2", bias=p['b2'], add=h, rms_gain=next_gain)
    return h_out, u_next, (h, u, g2, conv, s), list(res[1:])


def _conf_conv_plan(g2, p):
    halves = (0, D_MODEL)
    return _ConvPlan(g2, p['dw_w'], p['dw_b'], in_bases=halves, mid_bases=(0,), width=D_MODEL, rows=256,
                     rs=CONF_STRIP_ROWS, pre=_pre_glu, pre_params=[(p['b1'], halves)], post=_post_identity)


def _conf_layer_bwd(dh_out, p, saved):
    h, u, g2, conv, s = saved
    ds = _mm(dh_out, p['pw2'], 'nt', "conf_pw2_dx", out_dtype=bf16)
    d_pw2 = _mm(s, dh_out, 'tn', "conf_pw2_dw")
    d_conv, d_lng, d_lnb = _row_bwd(_f_ln_silu, [conv], [p['ln_g'], p['ln_b']], [ds], rows=256, name="conf_ln_bwd")
    dg2, d_dww, d_dwb, d_b1 = _conv_bwd(_conf_conv_plan(g2, p), [(d_conv, (0,))], bf16, "conf_conv_bwd")
    d_pw1 = _mm(u, dg2, 'tn', "conf_pw1_dw", out_cols_sharded=True)
    du = _mm_w(dg2, p['pw1'], 'nt', "conf_pw1_dx", out_dtype=bf16)
    dh, d_g = _mixer_norm_bwd(h, p['g'], du, dh_out, "conf_norm_bwd")
    grads = {'g': d_g, 'pw1': d_pw1, 'b1': d_b1, 'dw_w': d_dww, 'dw_b': d_dwb, 'ln_g': d_lng, 'ln_b': d_lnb,
             'pw2': d_pw2}
    return dh, grads


def _lru_params(p):
    return [p['in_b'], p['cw'], p['cb'], p['ga_w'], p['ga_b'], p['gx_w'], p['gx_b'], p['lam']]


def _lru_layer_fwd(h, u, p, next_gain):
    io = _mm_w(u, p['in_w'], 'nn', "lru_in")
    a, b, gg = _row_fwd(_f_lru, [io], _lru_params(p), [(LRU_W, f32)] * 3, rows=256, name="lru_gates",
                        halo=SUBLANE, halo_of=[True])
    y, hs = _lru_fwd(a, b, gg)
    h_out, u_next = _mm_normed(y, p['out'], 'nn', "lru_out", bias=p['out_b'], add=h, rms_gain=next_gain)
    return h_out, u_next, (h, u, io, a, gg, hs, y)


def _lru_layer_bwd(dh_out, p, saved):
    h, u, io, a, gg, hs, y = saved
    dy = _mm(dh_out, p['out'], 'nt', "lru_out_dx")
    d_out = _mm(y, dh_out, 'tn', "lru_out_dw")
    da, db, dgg = _lru_bwd(dy, gg, a, hs)
    res = _row_bwd(_f_lru, [io], _lru_params(p), [da, db, dgg], rows=256, name="lru_gates_bwd",
                   halo=SUBLANE, halo_of=[True], tile_dtypes=[bf16])
    dio, d_inb, d_cw, d_cb, d_gaw, d_gab, d_gxw, d_gxb, d_lam = res
    d_inw = _mm(u, dio, 'tn', "lru_in_dw", out_cols_sharded=True)
    du = _mm_w(dio, p['in_w'], 'nt', "lru_in_dx", out_dtype=bf16)
    dh, d_g = _mixer_norm_bwd(h, p['g'], du, dh_out, "lru_norm_bwd")
    grads = {'g': d_g, 'in_w': d_inw, 'in_b': d_inb, 'cw': d_cw, 'cb': d_cb, 'ga_w': d_gaw, 'ga_b': d_gab,
             'gx_w': d_gxw, 'gx_b': d_gxb, 'lam': d_lam, 'out': d_out}
    return dh, grads


def _sgu_params(p):
    return [p['in_b'], p['ln_g'], p['ln_b'], p['sp_w'], p['sp_bt']]


def _sgu_layer_fwd(h, u, p, next_gain):
    z = _mm_w(u, p['in_w'], 'nn', "sgu_in")
    s = _row_fwd(_f_sgu, [z], _sgu_params(p), [(SGU_HALF, bf16)], rows=SGU_CHUNK, name="sgu_mix")[0]
    h_out, u_next = _mm_normed(s, p['out'], 'nn', "sgu_out", bias=p['out_b'], add=h, rms_gain=next_gain)
    return h_out, u_next, (h, u, z, s)


def _sgu_layer_bwd(dh_out, p, saved):
    h, u, z, s = saved
    ds = _mm(dh_out, p['out'], 'nt', "sgu_out_dx", out_dtype=bf16)
    d_out = _mm(s, dh_out, 'tn', "sgu_out_dw")
    dz, d_inb, d_lng, d_lnb, d_spw, d_spbt = _row_bwd(_f_sgu, [z], _sgu_params(p), [ds], rows=SGU_CHUNK,
                                                      name="sgu_mix_bwd", tile_dtypes=[bf16])
    d_inw = _mm(u, dz, 'tn', "sgu_in_dw", out_cols_sharded=True)
    du = _mm_w(dz, p['in_w'], 'nt', "sgu_in_dx", out_dtype=bf16)
    dh, d_g = _mixer_norm_bwd(h, p['g'], du, dh_out, "sgu_norm_bwd")
    grads = {'g': d_g, 'in_w': d_inw, 'in_b': d_inb, 'ln_g': d_lng, 'ln_b': d_lnb, 'sp_w': d_spw, 'sp_bt': d_spbt,
             'out': d_out}
    return dh, grads


def _row(v):
    return v.reshape((1, -1)).astype(f32)


def _pad_lanes(v, n=LANE):
    v = _row(v)
    return jnp.pad(v, ((0, 0), (0, n - v.shape[1])))


def _local_step(x, target, w, comm=None):
    a_in = w['a_in_proj'][0]
    pa = {'g': _row(w['norm_mix'][0]), 'w_z': a_in[:, :SSD_D_INNER],
          'w_xbc': a_in[:, SSD_D_INNER:SSD_D_INNER + SSD_CONV_DIM],
          'w_dt': jnp.pad(a_in[:, SSD_D_INNER + SSD_CONV_DIM:], ((0, 0), (0, LANE - SSD_HEADS))),
          'cw': w['a_conv_w'][0].astype(f32), 'cb': _row(w['a_conv_b'][0]), 'dtb': _pad_lanes(w['a_dt_bias'][0]),
          'a_log': _pad_lanes(w['a_log'][0]), 'dsk': _pad_lanes(w['a_d_skip'][0]), 'norm': _row(w['a_norm'][0]),
          'out': w['a_out_proj']}
    mix_gain = [_row(w['norm_mix'][i]) for i in range(DEPTH)] + [None]
    ffn_gain = [_row(w['norm_ffn'][i]) for i in range(DEPTH)]
    u = _row_fwd(_f_rms, [x], [mix_gain[0]], [(D_MODEL, bf16)], rows=512, name="first_norm")[0]
    h, u, s_mix0, gathered = _ssd_layer_fwd(x, u, pa, ffn_gain[0], gather=comm.late_blocks if comm else ())
    if comm:
        w = {**w, **comm.late_weights(gathered)}
    ffn = [{'g': _row(w['norm_ffn'][i]), 'up': (w['f_up_w'], i), 'down': w['f_down_w'][i],
            'cw': w['f_conv_w'][i].astype(f32), 'cb': _row(w['f_conv_b'][i])} for i in range(DEPTH)]
    pb = {'g': _row(w['norm_mix'][1]), 'pw1': (w['b_pw1_w'], 0), 'b1': _row(w['b_pw1_b'][0]),
          'dw_w': w['b_dw_w'][0].astype(f32), 'dw_b': _row(w['b_dw_b'][0]), 'ln_g': _row(w['b_ln_g'][0]),
          'ln_b': _row(w['b_ln_b'][0]), 'pw2': w['b_pw2_w'], 'b2': _row(w['b_pw2_b'][0])}
    h, u, s_ffn0 = _ffn_fwd(h, u, ffn[0], mix_gain[1])
    h, u, s_mix1, gathered = _conf_layer_fwd(h, u, pb, ffn_gain[1], rider=comm.second_rider() if comm else None)
    if comm:
        w = {**w, **comm.second_weights(gathered)}
    pc = {'g': _row(w['norm_mix'][2]), 'in_w': (w['c_in_w'], 0), 'in_b': _row(w['c_in_b'][0]),
          'cw': w['c_conv_w'][0].astype(f32), 'cb': _row(w['c_conv_b'][0]),
          'ga_w': w['c_ga_w'][0].reshape(LRU_W, LRU_BLOCK).astype(f32), 'ga_b': _row(w['c_ga_b'][0]),
          'gx_w': w['c_gx_w'][0].reshape(LRU_W, LRU_BLOCK).astype(f32), 'gx_b': _row(w['c_gx_b'][0]),
          'lam': _row(w['c_lambda'][0]), 'out': w['c_out_w'], 'out_b': _row(w['c_out_b'][0])}
    pd = {'g': _row(w['norm_mix'][3]), 'in_w': (w['d_in_w'], 0), 'in_b': _row(w['d_in_b'][0]),
          'ln_g': _row(w['d_ln_g'][0]), 'ln_b': _row(w['d_ln_b'][0]),
          'sp_w': w['d_sp_w'][0].reshape(SGU_GROUPS * SGU_CHUNK, SGU_CHUNK).astype(f32),
          'sp_bt': w['d_sp_b'][0].astype(f32).T, 'out': w['d_out_w'], 'out_b': _row(w['d_out_b'][0])}
    mixers = [(None, None, pa), (None, _conf_layer_bwd, pb),
              (_lru_layer_fwd, _lru_layer_bwd, pc), (_sgu_layer_fwd, _sgu_layer_bwd, pd)]

    h, u, s_ffn1 = _ffn_fwd(h, u, ffn[1], mix_gain[2])
    saved = [(s_mix0, s_ffn0), (s_mix1, s_ffn1)]
    for i in range(2, DEPTH):
        fwd, _, p = mixers[i]
        h, u, s_mix = fwd(h, u, p, ffn_gain[i])
        h, u, s_ffn = _ffn_fwd(h, u, ffn[i], mix_gain[i + 1])
        saved.append((s_mix, s_ffn))
    dh, d_final, loss = _loss_head(h, target, _row(w['norm_final']))

    def rows_sharded(g):
        return g.reshape(N_CHIPS, g.shape[0] // N_CHIPS, g.shape[1])

    bias_zero = jnp.zeros((1, D_MODEL), f32)
    g_ffn, g_mix, d_out_bias = [None] * DEPTH, [None] * DEPTH, [None] * DEPTH
    for i in reversed(range(1, DEPTH)):
        _, bwd, p = mixers[i]
        dh, g_ffn[i], d_out_bias[i], _ = _ffn_bwd(dh, ffn[i], saved[i][1], bias_zero)
        dh, g_mix[i] = bwd(dh, p, saved[i][0])
    _, gb, gc, gd = g_mix

    def late_direct_grads(d_up0, d_down0):
        return {'b_pw1_w': gb['pw1'], 'b_pw2_w': rows_sharded(gb['pw2']), 'c_in_w': gc['in_w'],
                'c_out_w': rows_sharded(gc['out']), 'd_in_w': gd['in_w'], 'd_out_w': rows_sharded(gd['out']),
                'f_up_w': [d_up0] + [g['up'] for g in g_ffn[1:]],
                'f_down_w': [rows_sharded(d_down0)] + [rows_sharded(g['down']) for g in g_ffn[1:]]}

    make_rider = (lambda d_down0: comm.swap_rider(late_direct_grads(None, d_down0))) if comm else None
    dh, g_ffn[0], d_out_bias[0], swapped = _ffn_bwd(dh, ffn[0], saved[0][1], bias_zero, make_rider)
    late_direct = late_direct_grads(g_ffn[0]['up'], g_ffn[0]['down'])
    partials = comm.early_partials(late_direct, swapped) if comm else []
    dh, g_mix[0], received = _ssd_layer_bwd(dh, pa, saved[0][0], to_chips=partials)
    ga = g_mix[0]

    grads = {**late_direct,
        'norm_mix': jnp.concatenate([g['g'] for g in g_mix], axis=0),
        'norm_ffn': jnp.concatenate([g['g'] for g in g_ffn], axis=0),
        'norm_final': d_final.reshape(-1),
        'a_in_proj': jnp.concatenate([ga['w_z'], ga['w_xbc'], ga['w_dt'][:, :SSD_HEADS]], axis=1)[None],
        'a_conv_w': ga['cw'][None], 'a_conv_b': ga['cb'], 'a_dt_bias': ga['dtb'][:, :SSD_HEADS],
        'a_log': ga['a_log'][:, :SSD_HEADS], 'a_d_skip': ga['dsk'][:, :SSD_HEADS], 'a_norm': ga['norm'],
        'a_out_proj': rows_sharded(ga['out']),
        'b_pw1_b': gb['b1'], 'b_dw_w': gb['dw_w'][None], 'b_dw_b': gb['dw_b'],
        'b_ln_g': gb['ln_g'], 'b_ln_b': gb['ln_b'], 'b_pw2_b': d_out_bias[1],
        'c_in_b': gc['in_b'], 'c_conv_w': gc['cw'][None], 'c_conv_b': gc['cb'],
        'c_ga_w': gc['ga_w'].reshape(1, LRU_W // LRU_BLOCK, LRU_BLOCK, LRU_BLOCK),
        'c_ga_b': gc['ga_b'].reshape(1, LRU_W // LRU_BLOCK, LRU_BLOCK),
        'c_gx_w': gc['gx_w'].reshape(1, LRU_W // LRU_BLOCK, LRU_BLOCK, LRU_BLOCK),
        'c_gx_b': gc['gx_b'].reshape(1, LRU_W // LRU_BLOCK, LRU_BLOCK),
        'c_lambda': gc['lam'], 'c_out_b': d_out_bias[2],
        'd_in_b': gd['in_b'], 'd_ln_g': gd['ln_g'], 'd_ln_b': gd['ln_b'],
        'd_sp_w': gd['sp_w'].reshape(1, SGU_GROUPS, SGU_CHUNK, SGU_CHUNK), 'd_sp_b': gd['sp_bt'].T[None],
        'd_out_b': d_out_bias[3],
        'f_conv_w': jnp.stack([g['cw'] for g in g_ffn]),
        'f_conv_b': jnp.concatenate([g['cb'] for g in g_ffn], axis=0),
    }
    return loss, dh, grads, (partials, received)


def _global_shape(name, shard_shape):
    ax = SHARD_AXIS[name]
    if ax is None:
        return tuple(shard_shape)
    s = list(shard_shape)
    s[ax] *= N_CHIPS
    return tuple(s)


def _step(x, target, weights, moments_m, moments_v):
    x2, t2 = x[0], target[0]
    shard_shapes = {n: weights[n].shape for n in WEIGHTS}
    c_pos = lax.axis_index("c")
    k_pos = 2 * lax.axis_index("x") + lax.axis_index("y")
    c_idx = c_pos.astype(jnp.int32).reshape(1)
    k_idx = k_pos.astype(jnp.int32).reshape(1)

    def halves_of(a):
        a2 = _as2d(a)
        return a2.reshape(2, a2.shape[0] // 2, a2.shape[1])

    def view_direct(n, g):
        g = g.reshape((N_CHIPS,) + shard_shapes[n])
        if n in DIRECT_COLS:
            return g
        if n == 'f_down_w':
            return [g[:, i].reshape(-1, g.shape[-1]) for i in range(DEPTH)]
        return g.reshape(-1, g.shape[-1])

    def sibling_sums(grads):
        mine_g = [g.reshape(N_CHIPS, 2, g.shape[1] // 2, g.shape[2]) for g in grads]
        theirs = _swap_with_sibling(mine_g)
        return [_sum_with_sibling(g, t, c_idx) for g, t in zip(mine_g, theirs)]

    def flatten_direct(grads, names):
        out = []
        for n in names:
            out += grads[n] if isinstance(grads[n], list) else [grads[n]]
        return out

    first = [halves_of(weights[n].astype(bf16)) for n in EARLY_DIRECT]
    first.append(_pack([weights[n] for n in PACKED_MM], bf16).reshape(2, -1, FLAT_COLS))
    first.append(_pack([weights[n] for n in SHARDED_VEC], f32).reshape(2, -1, FLAT_COLS))
    gathered = _own_slot(_gather_weights(first), first, k_pos)
    w = {n: weights[n] for n in REPLICATED}
    for n, g in zip(EARLY_DIRECT, gathered):
        w[n] = view_direct(n, g)
    all_mm = _unpack(gathered[-2].reshape(N_CHIPS, -1), [shard_shapes[n] for n in PACKED_MM])
    all_vec = _unpack(gathered[-1].reshape(N_CHIPS, -1), [shard_shapes[n] for n in SHARDED_VEC])
    for n, st in zip(PACKED_MM + SHARDED_VEC, all_mm + all_vec):
        w[n] = _full_from_shards(st, SHARD_AXIS[n])

    def halved(g):
        return g.reshape(N_CHIPS, 2, g.shape[1] // 2, g.shape[2])

    class Comm:
        late_blocks = [halves_of(weights[n].astype(bf16)) for n in LATE_FIRST]
        second_blocks = [halves_of(weights[n].astype(bf16)) for n in LATE_SECOND]

        @staticmethod
        def late_weights(arrived):
            arrived = _own_slot(arrived, Comm.late_blocks, k_pos)
            return {n: view_direct(n, g) for n, g in zip(LATE_FIRST, arrived)}

        @staticmethod
        def second_rider():
            return _gather_rider(Comm.second_blocks)

        @staticmethod
        def second_weights(arrived):
            arrived = _own_slot(arrived, Comm.second_blocks, k_pos)
            return {n: view_direct(n, g) for n, g in zip(LATE_SECOND, arrived)}

        @staticmethod
        def swap_rider(grads):
            return _swap_rider([halved(g) for g in flatten_direct(grads, LATE_DIRECT) if g is not None])

        @staticmethod
        def early_partials(grads, swapped):
            mine_g = [halved(g) for g in flatten_direct(grads, LATE_DIRECT)]
            missing = LAST_SWAPPED_INDEX
            theirs = swapped[:missing] + list(_swap_with_sibling([mine_g[missing]])) + swapped[missing:]
            return [_sum_with_sibling(g, t, c_idx) for g, t in zip(mine_g, theirs)]

    loss_part, dx, grads, (partials, received) = _local_step(x2, t2, w, Comm)

    packed = [_shards_of(grads[n].reshape(_global_shape(n, shard_shapes[n])), SHARD_AXIS[n]).reshape(N_CHIPS, -1)
              for n in PACKED_MM + SHARDED_VEC]
    flat = jnp.concatenate(packed, axis=1)
    n_flat = flat.shape[1]
    n_pad = -(-n_flat // FLAT_QUANTUM) * FLAT_QUANTUM
    flat = jnp.pad(flat, ((0, 0), (0, n_pad - n_flat))).reshape(N_CHIPS, -1, FLAT_COLS)
    last_partials = sibling_sums(flatten_direct(grads, EARLY_DIRECT) + [flat])
    last_received = _send_to_chips(last_partials)
    partials, received = list(partials) + last_partials, list(received) + list(last_received)
    my_halves = [_sum_chips(p, r, k_idx) for p, r in zip(partials, received)]
    joined = _own_slot(_join_halves(my_halves), my_halves, c_pos)
    g_shard, pos = {}, 0
    for n in LATE_DIRECT + EARLY_DIRECT:
        layers = shard_shapes[n][0]
        g_shard[n] = jnp.stack([j.reshape(shard_shapes[n][1:]) for j in joined[pos:pos + layers]])
        pos += layers
    flat_shapes = [shard_shapes[n] for n in PACKED_MM + SHARDED_VEC]
    g_shard.update(zip(PACKED_MM + SHARDED_VEC, _unpack(joined[-1].reshape(-1), flat_shapes)))

    small = jnp.concatenate([grads[n].reshape(-1) for n in REPLICATED] + [loss_part.reshape(-1)[:1]])
    n_small = small.shape[0]
    n_small_pad = -(-n_small // (SUBLANE * FLAT_COLS)) * (SUBLANE * FLAT_COLS)
    small = jnp.pad(small, (0, n_small_pad - n_small)).reshape(-1, FLAT_COLS)
    small = _all_sum_small(small).reshape(-1)
    g_rep = dict(zip(REPLICATED, _unpack(small, [shard_shapes[n] for n in REPLICATED])))
    loss = small[n_small - 1]

    g_all = {**g_shard, **g_rep}
    delta, new_m, new_v = {}, {}, {}
    for n in WEIGHTS:
        delta[n], new_m[n], new_v[n] = _adamw(weights[n], g_all[n], moments_m[n], moments_v[n], "adamw_" + n)
    return loss, dx[None], g_all, delta, new_m, new_v


def kernel(x, norm_mix, norm_ffn, norm_final, a_in_proj, a_conv_w, a_conv_b, a_dt_bias, a_log, a_d_skip, a_norm, a_out_proj, b_pw1_w, b_pw1_b, b_dw_w, b_dw_b, b_ln_g, b_ln_b, b_pw2_w, b_pw2_b, c_in_w, c_in_b, c_conv_w, c_conv_b, c_ga_w, c_ga_b, c_gx_w, c_gx_b, c_lambda, c_out_w, c_out_b, d_in_w, d_in_b, d_ln_g, d_ln_b, d_sp_w, d_sp_b, d_out_w, d_out_b, f_up_w, f_conv_w, f_conv_b, f_down_w, loss_target, m_norm_mix, m_norm_ffn, m_norm_final, m_a_in_proj, m_a_conv_w, m_a_conv_b, m_a_dt_bias, m_a_log, m_a_d_skip, m_a_norm, m_a_out_proj, m_b_pw1_w, m_b_pw1_b, m_b_dw_w, m_b_dw_b, m_b_ln_g, m_b_ln_b, m_b_pw2_w, m_b_pw2_b, m_c_in_w, m_c_in_b, m_c_conv_w, m_c_conv_b, m_c_ga_w, m_c_ga_b, m_c_gx_w, m_c_gx_b, m_c_lambda, m_c_out_w, m_c_out_b, m_d_in_w, m_d_in_b, m_d_ln_g, m_d_ln_b, m_d_sp_w, m_d_sp_b, m_d_out_w, m_d_out_b, m_f_up_w, m_f_conv_w, m_f_conv_b, m_f_down_w, v_norm_mix, v_norm_ffn, v_norm_final, v_a_in_proj, v_a_conv_w, v_a_conv_b, v_a_dt_bias, v_a_log, v_a_d_skip, v_a_norm, v_a_out_proj, v_b_pw1_w, v_b_pw1_b, v_b_dw_w, v_b_dw_b, v_b_ln_g, v_b_ln_b, v_b_pw2_w, v_b_pw2_b, v_c_in_w, v_c_in_b, v_c_conv_w, v_c_conv_b, v_c_ga_w, v_c_ga_b, v_c_gx_w, v_c_gx_b, v_c_lambda, v_c_out_w, v_c_out_b, v_d_in_w, v_d_in_b, v_d_ln_g, v_d_ln_b, v_d_sp_w, v_d_sp_b, v_d_out_w, v_d_out_b, v_f_up_w, v_f_conv_w, v_f_conv_b, v_f_down_w):
    args = locals()
    weights = {n: args[n] for n in WEIGHTS}
    moments_m = {n: args['m_' + n] for n in WEIGHTS}
    moments_v = {n: args['v_' + n] for n in WEIGHTS}
    loss, dx, grad, delta, new_m, new_v = _step(x, loss_target, weights, moments_m, moments_v)
    return (loss, dx, *[grad[n] for n in WEIGHTS], *[delta[n] for n in WEIGHTS],
            *[new_m[n] for n in WEIGHTS], *[new_v[n] for n in WEIGHTS])
```

```python
import functools
import math

import jax
import jax.numpy as jnp
import numpy as np
from jax import lax
from jax.experimental import pallas as pl
from jax.experimental.pallas import tpu as pltpu

f32 = jnp.float32
bf16 = jnp.bfloat16
MESH = pl.DeviceIdType.MESH
HIGHEST = lax.Precision.HIGHEST

D_MODEL = 1024
DEPTH = 4
RMS_EPS = 1e-6
LN_EPS = 1e-5
SSD_D_INNER = 2048
SSD_HEADS = 32
SSD_BC = 1024
SSD_CONV_DIM = 4096
SSD_CHUNK = 128
SSD_GROUPS = 8
LRU_W = 1280
LRU_BLOCK = 256
LRU_C = 8.0
SGU_HALF = 2048
SGU_GROUPS = 8
SGU_CHUNK = 128
FFN_H = 2816
ADAM_LR, ADAM_B1, ADAM_B2, ADAM_EPS, ADAM_WD, ADAM_STEP = 0.001, 0.9, 0.999, 1e-08, 0.01, 10

LANE = 128
SUBLANE = 8
VMEM_LIMIT = 56 * 1024 * 1024
FLAT_COLS = 1024

WEIGHTS = ['norm_mix', 'norm_ffn', 'norm_final', 'a_in_proj', 'a_conv_w', 'a_conv_b', 'a_dt_bias', 'a_log',
           'a_d_skip', 'a_norm', 'a_out_proj', 'b_pw1_w', 'b_pw1_b', 'b_dw_w', 'b_dw_b', 'b_ln_g', 'b_ln_b',
           'b_pw2_w', 'b_pw2_b', 'c_in_w', 'c_in_b', 'c_conv_w', 'c_conv_b', 'c_ga_w', 'c_ga_b', 'c_gx_w',
           'c_gx_b', 'c_lambda', 'c_out_w', 'c_out_b', 'd_in_w', 'd_in_b', 'd_ln_g', 'd_ln_b', 'd_sp_w',
           'd_sp_b', 'd_out_w', 'd_out_b', 'f_up_w', 'f_conv_w', 'f_conv_b', 'f_down_w']
SHARD_AXIS = {
    'norm_mix': None, 'norm_ffn': None, 'norm_final': None, 'a_in_proj': 2, 'a_conv_w': 2, 'a_conv_b': None,
    'a_dt_bias': None, 'a_log': None, 'a_d_skip': None, 'a_norm': None, 'a_out_proj': 1, 'b_pw1_w': 2,
    'b_pw1_b': 1, 'b_dw_w': 2, 'b_dw_b': 1, 'b_ln_g': 1, 'b_ln_b': 1, 'b_pw2_w': 1, 'b_pw2_b': 1, 'c_in_w': 2,
    'c_in_b': 1, 'c_conv_w': 2, 'c_conv_b': 1, 'c_ga_w': 2, 'c_ga_b': 2, 'c_gx_w': 2, 'c_gx_b': 2,
    'c_lambda': 1, 'c_out_w': 1, 'c_out_b': 1, 'd_in_w': 2, 'd_in_b': 1, 'd_ln_g': 1, 'd_ln_b': 1,
    'd_sp_w': None, 'd_sp_b': None, 'd_out_w': 1, 'd_out_b': 1, 'f_up_w': 2, 'f_conv_w': 2, 'f_conv_b': None,
    'f_down_w': 1}
MATMUL_WEIGHTS = ['a_in_proj', 'a_out_proj', 'b_pw1_w', 'b_pw2_w', 'c_in_w', 'c_ga_w', 'c_gx_w', 'c_out_w',
                  'd_in_w', 'd_out_w', 'f_up_w', 'f_down_w']
DIRECT_COLS = ['b_pw1_w', 'c_in_w', 'd_in_w', 'f_up_w']
DIRECT_ROWS = ['a_out_proj', 'b_pw2_w', 'c_out_w', 'd_out_w', 'f_down_w']
DIRECT = DIRECT_COLS + DIRECT_ROWS
EARLY_DIRECT = ['a_out_proj']
LATE_DIRECT = [n for n in DIRECT if n not in EARLY_DIRECT]
LATE_SECOND = ['c_in_w', 'c_out_w', 'd_in_w', 'd_out_w']
LATE_FIRST = [n for n in LATE_DIRECT if n not in LATE_SECOND]
LAST_SWAPPED_INDEX = LATE_DIRECT.index('f_up_w')
PACKED_MM = [n for n in MATMUL_WEIGHTS if n not in DIRECT]
SHARDED = [n for n in WEIGHTS if SHARD_AXIS[n] is not None]
SHARDED_VEC = [n for n in SHARDED if n not in MATMUL_WEIGHTS]
REPLICATED = [n for n in WEIGHTS if SHARD_AXIS[n] is None]
N_CHIPS = 4
N_DEV = 8


def _tile(n, cap, mult):
    if n <= cap:
        return n
    t = (cap // mult) * mult
    while t >= mult:
        if n % t == 0:
            return t
        t -= mult
    raise ValueError(f"no tile for {n} under {cap} in steps of {mult}")


def _cparams(sem=None):
    if sem is None:
        return pltpu.CompilerParams(vmem_limit_bytes=VMEM_LIMIT)
    return pltpu.CompilerParams(dimension_semantics=sem, vmem_limit_bytes=VMEM_LIMIT)


def _dg(a, b, ca, cb):
    return lax.dot_general(a.astype(bf16), b.astype(bf16), (((ca,), (cb,)), ((), ())), preferred_element_type=f32)


@jax.custom_vjp
def _dot_nn(a, b):
    return _dg(a, b, 1, 0)


def _dot_nn_fwd(a, b):
    return _dg(a, b, 1, 0), (a, b)


def _dot_nn_bwd(res, g):
    a, b = res
    return _dg(g, b, 1, 1).astype(a.dtype), _dg(a, g, 0, 0).astype(b.dtype)


_dot_nn.defvjp(_dot_nn_fwd, _dot_nn_bwd)


@jax.custom_vjp
def _dot_nt(a, b):
    return _dg(a, b, 1, 1)


def _dot_nt_fwd(a, b):
    return _dg(a, b, 1, 1), (a, b)


def _dot_nt_bwd(res, g):
    a, b = res
    return _dg(g, b, 1, 0).astype(a.dtype), _dg(g, a, 0, 0).astype(b.dtype)


_dot_nt.defvjp(_dot_nt_fwd, _dot_nt_bwd)


@jax.custom_vjp
def _dot_tn(a, b):
    return _dg(a, b, 0, 0)


def _dot_tn_fwd(a, b):
    return _dg(a, b, 0, 0), (a, b)


def _dot_tn_bwd(res, g):
    a, b = res
    return _dg(b, g, 1, 1).astype(a.dtype), _dg(a, g, 1, 0).astype(b.dtype)


_dot_tn.defvjp(_dot_tn_fwd, _dot_tn_bwd)


def _expm1(x):
    small = jnp.abs(x) < 0.03
    xs = jnp.where(small, x, 0.0)
    series = xs * (1.0 + xs * (0.5 + xs * (1.0 / 6.0 + xs * (1.0 / 24.0 + xs * (1.0 / 120.0)))))
    return jnp.where(small, series, jnp.exp(x) - 1.0)


def _rms(x, g):
    return x * lax.rsqrt(jnp.mean(x * x, axis=-1, keepdims=True) + RMS_EPS) * g


def _layer_norm(x, g, b):
    mu = jnp.mean(x, axis=-1, keepdims=True)
    xc = x - mu
    return xc * lax.rsqrt(jnp.mean(xc * xc, axis=-1, keepdims=True) + LN_EPS) * g + b


def _causal_taps(ext, w, halo, rows):
    k_taps = w.shape[0]
    acc = None
    for k in range(k_taps):
        lo = halo - (k_taps - 1) + k
        term = w[k:k + 1, :] * ext[lo:lo + rows, :]
        acc = term if acc is None else acc + term
    return acc


def _mm(a, b, mode, name, *, bias=None, add=None, out_dtype=f32, tm_cap=1408, tn_cap=1408, tk_cap=1408,
        b_cols_sharded=False, b_layer=None, out_cols_sharded=False, rms_gain=None, rms_bwd=None):
    shard_cols = None
    if b_cols_sharded:
        shard_cols = b.shape[-1]
        b_dims = (b.shape[-2], N_CHIPS * shard_cols)
    else:
        b_dims = b.shape
    if mode == 'nn':
        (m, k), (k2, n) = a.shape, b_dims
    elif mode == 'nt':
        (m, k), (n, k2) = a.shape, b_dims
    else:
        (k, m), (k2, n) = a.shape, b_dims
    assert k == k2, (name, a.shape, b.shape)
    if rms_bwd is not None:
        assert rms_gain is None
        tm_cap = min(tm_cap, 512)
    tm = _tile(m, tm_cap, LANE if mode == 'tn' else SUBLANE)
    tn = _tile(n, tn_cap, LANE)
    tk = _tile(k, tk_cap, LANE if mode != 'tn' else SUBLANE)
    if b_cols_sharded and mode == 'nn':
        tn = shard_cols
    if b_cols_sharded and mode == 'nt':
        tk = shard_cols
    if out_cols_sharded:
        assert mode == 'tn' and n % N_CHIPS == 0
        tn = n // N_CHIPS
    nk = k // tk

    def shard_block(rows):
        lead = (None,) * (b.ndim - 2)
        return lead + (rows, shard_cols)

    def shard_index(shard, row_block):
        return (shard, row_block, 0) if b_layer is None else (shard, b_layer, row_block, 0)

    if mode == 'nn':
        a_spec = pl.BlockSpec((tm, tk), lambda i, j, kk: (i, kk))
        if b_cols_sharded:
            b_spec = pl.BlockSpec(shard_block(tk), lambda i, j, kk: shard_index(j, kk))
        else:
            b_spec = pl.BlockSpec((tk, tn), lambda i, j, kk: (kk, j))
        ca, cb = 1, 0
    elif mode == 'nt':
        a_spec = pl.BlockSpec((tm, tk), lambda i, j, kk: (i, kk))
        if b_cols_sharded:
            b_spec = pl.BlockSpec(shard_block(tn), lambda i, j, kk: shard_index(kk, j))
        else:
            b_spec = pl.BlockSpec((tn, tk), lambda i, j, kk: (j, kk))
        ca, cb = 1, 1
    else:
        a_spec = pl.BlockSpec((tk, tm), lambda i, j, kk: (kk, i))
        b_spec = pl.BlockSpec((tk, tn), lambda i, j, kk: (kk, j))
        ca, cb = 0, 0
    in_specs, operands = [a_spec, b_spec], [a, b]
    if bias is not None:
        in_specs.append(pl.BlockSpec((1, tn), lambda i, j, kk: (0, j)))
        operands.append(bias)
    if add is not None:
        in_specs.append(pl.BlockSpec((tm, tn), lambda i, j, kk: (i, j)))
        operands.append(add)
    if rms_gain is not None:
        in_specs.append(pl.BlockSpec((1, tn), lambda i, j, kk: (0, j)))
        operands.append(rms_gain)
    if rms_bwd is not None:
        bwd_h, bwd_gain, res_grad, want_bias = rms_bwd
        assert tn == n and not out_cols_sharded, "the norm needs whole rows in a tile"
        in_specs += [pl.BlockSpec((tm, tn), lambda i, j, kk: (i, j)), pl.BlockSpec((1, tn), lambda i, j, kk: (0, j)),
                     pl.BlockSpec((tm, tn), lambda i, j, kk: (i, j))]
        operands += [bwd_h, bwd_gain, res_grad]

    def body(*refs):
        a_ref, b_ref = refs[0], refs[1]
        pos = 2
        bias_ref = add_ref = None
        if bias is not None:
            bias_ref = refs[pos]
            pos += 1
        if add is not None:
            add_ref = refs[pos]
            pos += 1
        norm_ref = None
        if rms_gain is not None:
            norm_ref = refs[pos]
            pos += 1
        if rms_bwd is not None:
            bh_ref, bg_ref, res_ref = refs[pos:pos + 3]
            pos += 3
        o_ref = refs[pos]
        normed_ref = refs[pos + 1] if rms_gain is not None else None
        acc_ref = refs[-1]
        kk = pl.program_id(2)

        @pl.when(kk == 0)
        def _():
            acc_ref[...] = jnp.zeros_like(acc_ref)

        acc_ref[...] += _dg(a_ref[...], b_ref[...], ca, cb)

        @pl.when(kk == nk - 1)
        def _():
            r = acc_ref[...]
            if bias_ref is not None:
                r = r + bias_ref[...]
            if add_ref is not None:
                r = r + add_ref[...].astype(f32)
            if rms_bwd is not None:
                sum_refs = refs[pos + 1:pos + (3 if want_bias else 2)]

                @pl.when(pl.program_id(0) == 0)
                def _():
                    for s_ref in sum_refs:
                        s_ref[...] = jnp.zeros_like(s_ref)

                if want_bias:
                    _, vjp = jax.vjp(_f_rms_res, bh_ref[...], bg_ref[...], jnp.zeros((1, tn), f32))
                else:
                    _, vjp = jax.vjp(lambda hv, gv: (_rms(hv, gv), hv), bh_ref[...], bg_ref[...])
                grads = vjp((r, res_ref[...]))
                for s_ref, g in zip(sum_refs, grads[1:]):
                    s_ref[...] += g
                r = grads[0]
            o_ref[...] = r.astype(out_dtype)
            if normed_ref is not None:
                normed_ref[...] = _rms(r, norm_ref[...]).astype(bf16)

    if out_cols_sharded:
        out_shape = jax.ShapeDtypeStruct((N_CHIPS, m, tn), out_dtype)
        out_spec = pl.BlockSpec((None, tm, tn), lambda i, j, kk: (j, i, 0))
    else:
        out_shape = jax.ShapeDtypeStruct((m, n), out_dtype)
        out_spec = pl.BlockSpec((tm, tn), lambda i, j, kk: (i, j))
    if rms_gain is not None:
        assert tn == n and not out_cols_sharded, "the norm needs whole rows in a tile"
        out_shape = [out_shape, jax.ShapeDtypeStruct((m, n), bf16)]
        out_spec = [out_spec, pl.BlockSpec((tm, tn), lambda i, j, kk: (i, j))]
    semantics = ("parallel", "parallel", "arbitrary")
    if rms_bwd is not None:
        n_sums = 2 if rms_bwd[3] else 1
        out_shape = [out_shape] + [jax.ShapeDtypeStruct((1, n), f32)] * n_sums
        out_spec = [out_spec] + [pl.BlockSpec((1, tn), lambda i, j, kk: (0, j))] * n_sums
        semantics = ("arbitrary", "arbitrary", "arbitrary")
    return pl.pallas_call(
        body, name=name, out_shape=out_shape,
        grid=(m // tm, n // tn, nk), in_specs=in_specs, out_specs=out_spec,
        scratch_shapes=[pltpu.VMEM((tm, tn), f32)],
        compiler_params=_cparams(semantics),
    )(*operands)


def _mm_normed(a, b, mode, name, *, rms_gain, **kw):
    if rms_gain is None:
        return _mm(a, b, mode, name, **kw), None
    return _mm(a, b, mode, name, rms_gain=rms_gain, **kw)


def _mm_w(a, w, mode, name, **kw):
    shards, layer = w
    return _mm(a, shards, mode, name, b_cols_sharded=True, b_layer=layer, **kw)


def _row_specs(tiles, halo_of, rows, halo, n_tiles, reverse):
    def tile_index(i):
        return n_tiles - 1 - i if reverse else i

    specs, operands = [], []
    for arr, has_halo in zip(tiles, halo_of):
        cols = arr.shape[1]
        specs.append(pl.BlockSpec((rows, cols), lambda i: (tile_index(i), 0)))
        operands.append(arr)
        if has_halo:
            per = rows // halo
            specs.append(pl.BlockSpec((halo, cols), lambda i: (jnp.maximum(tile_index(i) * per - 1, 0), 0)))
            operands.append(arr)
    return specs, operands, tile_index


def _load_tiles(refs, halo_of, tile_id, rows, halo):
    vals, pos = [], 0
    for has_halo in halo_of:
        cur = refs[pos][...].astype(f32)
        pos += 1
        if has_halo:
            before = refs[pos][...].astype(f32)
            pos += 1
            before = jnp.where(tile_id > 0, before, jnp.zeros_like(before))
            cur = jnp.concatenate([before, cur], axis=0)
        vals.append(cur)
    return vals, pos


def _valid_rows(tile_id, rows, halo):
    r = lax.broadcasted_iota(jnp.int32, (halo + rows, 1), 0)
    return jnp.logical_or(r >= halo, tile_id > 0).astype(f32)


def _row_fwd(f, tiles, params, outs, *, rows, name, halo=0, halo_of=None):
    t_len = tiles[0].shape[0]
    rows = min(rows, t_len)
    n_tiles = t_len // rows
    halo_of = halo_of or [False] * len(tiles)
    specs, operands, _ = _row_specs(tiles, halo_of, rows, halo, n_tiles, False)
    for p in params:
        specs.append(pl.BlockSpec(p.shape, lambda i: (0, 0)))
        operands.append(p)

    def body(*refs):
        i = pl.program_id(0)
        vals, pos = _load_tiles(refs, halo_of, i, rows, halo)
        pvals = [refs[pos + j][...] for j in range(len(params))]
        pos += len(params)
        kw = {'valid': _valid_rows(i, rows, halo)} if halo else {}
        res = f(*vals, *pvals, **kw)
        for o_ref, o in zip(refs[pos:], res):
            o_ref[...] = o.astype(o_ref.dtype)

    return pl.pallas_call(
        body, name=name,
        out_shape=[jax.ShapeDtypeStruct((t_len, c), d) for c, d in outs],
        grid=(n_tiles,), in_specs=specs,
        out_specs=[pl.BlockSpec((rows, c), lambda i: (i, 0)) for c, _ in outs],
        compiler_params=_cparams(("parallel",)),
    )(*operands)


def _row_bwd(f, tiles, params, cots, *, rows, name, halo=0, halo_of=None, tile_dtypes=None):
    t_len = tiles[0].shape[0]
    rows = min(rows, t_len)
    n_tiles = t_len // rows
    halo_of = halo_of or [False] * len(tiles)
    tile_dtypes = tile_dtypes or [f32] * len(tiles)
    specs, operands, tile_index = _row_specs(tiles, halo_of, rows, halo, n_tiles, True)
    for p in params:
        specs.append(pl.BlockSpec(p.shape, lambda i: (0, 0)))
        operands.append(p)
    for ct in cots:
        specs.append(pl.BlockSpec((rows, ct.shape[1]), lambda i: (tile_index(i), 0)))
        operands.append(ct)
    n_t, n_p, n_c = len(tiles), len(params), len(cots)
    out_shape = [jax.ShapeDtypeStruct(t.shape, d) for t, d in zip(tiles, tile_dtypes)]
    out_shape += [jax.ShapeDtypeStruct(p.shape, f32) for p in params]
    out_specs = [pl.BlockSpec((rows, t.shape[1]), lambda i: (tile_index(i), 0)) for t in tiles]
    out_specs += [pl.BlockSpec(p.shape, lambda i: (0, 0)) for p in params]
    scratch = [pltpu.VMEM((halo, t.shape[1]), f32) for t, h in zip(tiles, halo_of) if h]

    def body(*refs):
        i = pl.program_id(0)
        tile_id = tile_index(i)
        vals, pos = _load_tiles(refs, halo_of, tile_id, rows, halo)
        pvals = [refs[pos + j][...] for j in range(n_p)]
        pos += n_p
        cvals = [refs[pos + j][...].astype(f32) for j in range(n_c)]
        pos += n_c
        d_tile_refs = refs[pos:pos + n_t]
        d_param_refs = refs[pos + n_t:pos + n_t + n_p]
        carries = list(refs[pos + n_t + n_p:])
        kw = {'valid': _valid_rows(tile_id, rows, halo)} if halo else {}
        _, vjp = jax.vjp(lambda *args: tuple(f(*args, **kw)), *vals, *pvals)
        grads = vjp(tuple(cvals))

        @pl.when(i == 0)
        def _():
            for cr in carries:
                cr[...] = jnp.zeros_like(cr)
            for dp in d_param_refs:
                dp[...] = jnp.zeros_like(dp)

        ci = 0
        for t in range(n_t):
            g = grads[t]
            if halo_of[t]:
                cr = carries[ci]
                ci += 1
                d_tile_refs[t][0:rows - halo, :] = g[halo:rows, :].astype(d_tile_refs[t].dtype)
                d_tile_refs[t][rows - halo:rows, :] = (g[rows:rows + halo, :] + cr[...]).astype(d_tile_refs[t].dtype)
                cr[...] = g[0:halo, :]
            else:
                d_tile_refs[t][...] = g.astype(d_tile_refs[t].dtype)
        for j in range(n_p):
            d_param_refs[j][...] += grads[n_t + j]

    return pl.pallas_call(
        body, name=name, out_shape=out_shape, grid=(n_tiles,), in_specs=specs, out_specs=out_specs,
        scratch_shapes=scratch, compiler_params=_cparams(("arbitrary",)),
    )(*operands)


def _strip_specs(tiles, rows, halo, n_tiles, reverse):
    def tile_index(i):
        return n_tiles - 1 - i if reverse else i

    specs, operands = [], []
    for arr, _, has_halo in tiles:
        cols = arr.shape[1]
        specs.append(pl.BlockSpec((rows, cols), lambda i: (tile_index(i), 0)))
        operands.append(arr)
        if has_halo:
            per = rows // halo
            specs.append(pl.BlockSpec((halo, cols), lambda i: (jnp.maximum(tile_index(i) * per - 1, 0), 0)))
            operands.append(arr)
    return specs, operands, tile_index


def _strip_sources(refs, tiles, ext_scratch, tile_id, rows, halo):
    srcs, pos, si = [], 0, 0
    for _, _, has_halo in tiles:
        cur = refs[pos]
        pos += 1
        if has_halo:
            before = refs[pos]
            pos += 1
            scr = ext_scratch[si]
            si += 1
            scr[0:halo, :] = jnp.where(tile_id > 0, before[...].astype(f32), 0.0)
            scr[halo:halo + rows, :] = cur[...].astype(f32)
            srcs.append(scr)
        else:
            srcs.append(cur)
    return srcs, pos


def _cols(base, c0, cs):
    return pl.ds(pl.multiple_of(base + c0, LANE), cs)


def _strip_inputs(tiles, srcs, params, p_refs, r0, c0, rs, cs, halo):
    vals = []
    for (_, bases, has_halo), src in zip(tiles, srcs):
        n_rows = halo + rs if has_halo else rs
        for b in bases:
            vals.append(src[pl.ds(r0, n_rows), _cols(b, c0, cs)].astype(f32))
    for (_, bases), p_ref in zip(params, p_refs):
        for b in bases:
            vals.append(p_ref[:, _cols(b, c0, cs)])
    return vals


def _strip_valid(tile_id, r0, rs, halo):
    r = lax.broadcasted_iota(jnp.int32, (halo + rs, 1), 0) + r0
    return jnp.logical_or(r >= halo, tile_id > 0).astype(f32)


def _strip_fwd(f, tiles, params, outs, *, rows, rs, cs, width, name, halo=0):
    t_len = tiles[0][0].shape[0]
    rows = min(rows, t_len)
    n_tiles, n_rs, n_cs = t_len // rows, rows // rs, width // cs
    specs, operands, _ = _strip_specs(tiles, rows, halo, n_tiles, False)
    for p, _ in params:
        specs.append(pl.BlockSpec(p.shape, lambda i: (0, 0)))
        operands.append(p)
    n_p, n_o = len(params), len(outs)
    scratch = [pltpu.VMEM((halo + rows, arr.shape[1]), f32) for arr, _, hh in tiles if hh]

    def body(*refs):
        i = pl.program_id(0)
        ext_scratch = refs[len(refs) - len(scratch):]
        srcs, pos = _strip_sources(refs, tiles, ext_scratch, i, rows, halo)
        p_refs = refs[pos:pos + n_p]
        o_refs = refs[pos + n_p:pos + n_p + n_o]

        def row_loop(r, carry):
            r0 = pl.multiple_of(r * rs, rs)
            kw = {'valid': _strip_valid(i, r0, rs, halo)} if halo else {}

            def col_loop(c, carry2):
                c0 = c * cs
                res = f(*_strip_inputs(tiles, srcs, params, p_refs, r0, c0, rs, cs, halo), **kw)
                k = 0
                for (_, dt, bases), o_ref in zip(outs, o_refs):
                    for b in bases:
                        o_ref[pl.ds(r0, rs), _cols(b, c0, cs)] = res[k].astype(dt)
                        k += 1
                return carry2

            return lax.fori_loop(0, n_cs, col_loop, carry)

        lax.fori_loop(0, n_rs, row_loop, 0)

    return pl.pallas_call(
        body, name=name,
        out_shape=[jax.ShapeDtypeStruct((t_len, c), d) for c, d, _ in outs],
        grid=(n_tiles,), in_specs=specs,
        out_specs=[pl.BlockSpec((rows, c), lambda i: (i, 0)) for c, _, _ in outs],
        scratch_shapes=scratch, compiler_params=_cparams(("parallel",)),
    )(*operands)


def _strip_bwd(f, tiles, params, cots, *, rows, rs, cs, width, name, halo=0, tile_dtypes=None):
    t_len = tiles[0][0].shape[0]
    rows = min(rows, t_len)
    n_tiles, n_rs, n_cs = t_len // rows, rows // rs, width // cs
    tile_dtypes = tile_dtypes or [f32] * len(tiles)
    specs, operands, tile_index = _strip_specs(tiles, rows, halo, n_tiles, True)
    for p, _ in params:
        specs.append(pl.BlockSpec(p.shape, lambda i: (0, 0)))
        operands.append(p)
    for ct, _ in cots:
        specs.append(pl.BlockSpec((rows, ct.shape[1]), lambda i: (tile_index(i), 0)))
        operands.append(ct)
    n_t, n_p, n_c = len(tiles), len(params), len(cots)
    out_shape = [jax.ShapeDtypeStruct(t[0].shape, d) for t, d in zip(tiles, tile_dtypes)]
    out_shape += [jax.ShapeDtypeStruct(p.shape, f32) for p, _ in params]
    out_specs = [pl.BlockSpec((rows, t[0].shape[1]), lambda i: (tile_index(i), 0)) for t in tiles]
    out_specs += [pl.BlockSpec(p.shape, lambda i: (0, 0)) for p, _ in params]
    halo_tiles = [t for t in tiles if t[2]]
    scratch = [pltpu.VMEM((halo + rows, arr.shape[1]), f32) for arr, _, _ in halo_tiles]
    scratch += [pltpu.VMEM((halo + rows, arr.shape[1]), f32) for arr, _, _ in halo_tiles]
    scratch += [pltpu.VMEM((halo, arr.shape[1]), f32) for arr, _, _ in halo_tiles]
    n_h = len(halo_tiles)

    def body(*refs):
        i = pl.program_id(0)
        tile_id = tile_index(i)
        scr = refs[len(refs) - 3 * n_h:]
        ext_scratch, grad_scratch, carries = scr[:n_h], scr[n_h:2 * n_h], scr[2 * n_h:]
        srcs, pos = _strip_sources(refs, tiles, ext_scratch, tile_id, rows, halo)
        p_refs = refs[pos:pos + n_p]
        c_refs = refs[pos + n_p:pos + n_p + n_c]
        pos += n_p + n_c
        d_tile_refs = refs[pos:pos + n_t]
        d_param_refs = refs[pos + n_t:pos + n_t + n_p]

        @pl.when(i == 0)
        def _():
            for cr in carries:
                cr[...] = jnp.zeros_like(cr)
            for dp in d_param_refs:
                dp[...] = jnp.zeros_like(dp)

        for gs in grad_scratch:
            gs[...] = jnp.zeros_like(gs)

        def row_loop(r, carry):
            r0 = pl.multiple_of(r * rs, rs)
            kw = {'valid': _strip_valid(tile_id, r0, rs, halo)} if halo else {}

            def col_loop(c, carry2):
                c0 = c * cs
                vals = _strip_inputs(tiles, srcs, params, p_refs, r0, c0, rs, cs, halo)
                cvals = []
                for (_, bases), c_ref in zip(cots, c_refs):
                    for b in bases:
                        cvals.append(c_ref[pl.ds(r0, rs), _cols(b, c0, cs)].astype(f32))
                _, vjp = jax.vjp(lambda *args: tuple(f(*args, **kw)), *vals)
                grads = vjp(tuple(cvals))
                k, hi = 0, 0
                for t, (_, bases, has_halo) in enumerate(tiles):
                    for b in bases:
                        if has_halo:
                            grad_scratch[hi][pl.ds(r0, halo + rs), _cols(b, c0, cs)] += grads[k]
                        else:
                            d_tile_refs[t][pl.ds(r0, rs), _cols(b, c0, cs)] = grads[k].astype(d_tile_refs[t].dtype)
                        k += 1
                    hi += has_halo
                for (_, bases), dp in zip(params, d_param_refs):
                    for b in bases:
                        dp[:, _cols(b, c0, cs)] += grads[k]
                        k += 1
                return carry2

            return lax.fori_loop(0, n_cs, col_loop, carry)

        lax.fori_loop(0, n_rs, row_loop, 0)

        hi = 0
        for t, (_, _, has_halo) in enumerate(tiles):
            if has_halo:
                gs, cr, d_ref = grad_scratch[hi], carries[hi], d_tile_refs[t]
                hi += 1
                d_ref[0:rows - halo, :] = gs[halo:rows, :].astype(d_ref.dtype)
                d_ref[rows - halo:rows, :] = (gs[rows:rows + halo, :] + cr[...]).astype(d_ref.dtype)
                cr[...] = gs[0:halo, :]

    return pl.pallas_call(
        body, name=name, out_shape=out_shape, grid=(n_tiles,), in_specs=specs, out_specs=out_specs,
        scratch_shapes=scratch, compiler_params=_cparams(("arbitrary",)),
    )(*operands)


def _fold8(v):
    acc = v[0:SUBLANE, :]
    for m in range(1, v.shape[0] // SUBLANE):
        acc = acc + v[m * SUBLANE:(m + 1) * SUBLANE, :]
    return acc


class _ConvPlan:
    def __init__(self, x, w, b, *, in_bases, mid_bases, width, rows, rs, pre=None, pre_params=(), post=None):
        self.x, self.w, self.b = x, w, b
        self.in_bases, self.mid_bases, self.width = in_bases, mid_bases, width
        self.pre, self.pre_params, self.post = pre, list(pre_params), post
        self.k_taps = w.shape[0]
        tile_rows = SUBLANE * (4 // x.dtype.itemsize)
        self.halo = -(-(self.k_taps - 1) // tile_rows) * tile_rows
        self.t_len = x.shape[0]
        self.rows = min(rows, self.t_len)
        self.rs = rs
        self.n_tiles, self.n_rs, self.n_cs = self.t_len // self.rows, self.rows // rs, width // LANE
        self.n_mid = len(mid_bases)
        if pre is None:
            assert len(in_bases) == self.n_mid

    def in_specs(self, tile_index):
        cols = self.x.shape[1]
        per = self.rows // self.halo
        specs = [pl.BlockSpec((self.rows, cols), lambda i: (tile_index(i), 0)),
                 pl.BlockSpec((self.halo, cols), lambda i: (jnp.maximum(tile_index(i) * per - 1, 0), 0)),
                 pl.BlockSpec(self.w.shape, lambda i: (0, 0)), pl.BlockSpec(self.b.shape, lambda i: (0, 0))]
        operands = [self.x, self.x, self.w, self.b]
        for p, _ in self.pre_params:
            specs.append(pl.BlockSpec(p.shape, lambda i: (0, 0)))
            operands.append(p)
        return specs, operands

    def pre_strips(self, pp_refs, c0):
        return [p_ref[:, _cols(b, c0, LANE)] for (_, bases), p_ref in zip(self.pre_params, pp_refs) for b in bases]

    def fill_conv_input(self, cur_ref, before_ref, pp_refs, u_ref, tile_id):
        started = (tile_id > 0).astype(f32)

        def col_loop(c, carry):
            c0 = c * LANE
            pps = self.pre_strips(pp_refs, c0)
            xs = [before_ref[:, _cols(b, c0, LANE)].astype(f32) for b in self.in_bases]
            for j, u in enumerate(self.pre(*xs, *pps, valid=started)):
                u_ref[0:self.halo, _cols(j * self.width, c0, LANE)] = u
            for r in range(self.n_rs):
                xs = [cur_ref[r * self.rs:(r + 1) * self.rs, _cols(b, c0, LANE)].astype(f32) for b in self.in_bases]
                for j, u in enumerate(self.pre(*xs, *pps, valid=1.0)):
                    u_ref[self.halo + r * self.rs:self.halo + (r + 1) * self.rs, _cols(j * self.width, c0, LANE)] = u
            return carry

        lax.fori_loop(0, self.n_cs, col_loop, 0)

    def tap(self, cur_ref, before_ref, u_ref, tile_id, r, j, k, c0):
        lo = r * self.rs - (self.k_taps - 1) + k
        if u_ref is not None:
            return u_ref[self.halo + lo:self.halo + lo + self.rs, _cols(j * self.width, c0, LANE)]
        cols = _cols(self.in_bases[j], c0, LANE)
        if lo >= 0:
            return cur_ref[lo:lo + self.rs, cols].astype(f32)
        head = before_ref[self.halo + lo:self.halo, cols].astype(f32)
        head = jnp.where(tile_id > 0, head, 0.0)
        return jnp.concatenate([head, cur_ref[0:self.rs + lo, cols].astype(f32)], axis=0)

    def conv(self, cur_ref, before_ref, u_ref, w_ref, b_ref, tile_id, r, c0):
        hcs = []
        for j, mb in enumerate(self.mid_bases):
            cols = _cols(mb, c0, LANE)
            acc = b_ref[:, cols]
            for k in range(self.k_taps):
                acc = acc + w_ref[k:k + 1, cols] * self.tap(cur_ref, before_ref, u_ref, tile_id, r, j, k, c0)
            hcs.append(acc)
        return hcs


def _conv_fwd(plan, outs, name, rider=None):
    n_pp = len(plan.pre_params)
    n_in, n_out = 4 + n_pp, len(outs)
    r_n = rider.n if rider else 0
    specs, operands = plan.in_specs(lambda i: i)
    scratch = [pltpu.VMEM((plan.halo + plan.rows, plan.n_mid * plan.width), f32)] if plan.pre else []
    n_scr = len(scratch)

    def body(*refs):
        cur_ref, before_ref, w_ref, b_ref = refs[:4]
        pp_refs = refs[4:n_in]
        r_srcs = refs[n_in:n_in + r_n]
        o_refs = refs[n_in + r_n:n_in + r_n + n_out]
        r_outs = refs[n_in + r_n + n_out:n_in + 2 * r_n + n_out]
        scr = refs[n_in + 2 * r_n + n_out:]
        u_ref = scr[0] if plan.pre else None
        i = pl.program_id(0)
        if rider:
            @pl.when(i == 0)
            def _():
                rider.start(r_srcs, r_outs, scr[n_scr:])

        if plan.pre:
            plan.fill_conv_input(cur_ref, before_ref, pp_refs, u_ref, i)

        def col_loop(c, carry):
            c0 = c * LANE
            for r in range(plan.n_rs):
                res = plan.post(*plan.conv(cur_ref, before_ref, u_ref, w_ref, b_ref, i, r, c0))
                n = 0
                for (_, dt, bases), o_ref in zip(outs, o_refs):
                    for ob in bases:
                        o_ref[r * plan.rs:(r + 1) * plan.rs, _cols(ob, c0, LANE)] = res[n].astype(dt)
                        n += 1
            return carry

        lax.fori_loop(0, plan.n_cs, col_loop, 0)
        if rider:
            @pl.when(i == plan.n_tiles - 1)
            def _():
                rider.finish(r_srcs, r_outs, scr[n_scr:])

    return pl.pallas_call(
        body, name=name,
        out_shape=[jax.ShapeDtypeStruct((plan.t_len, c), d) for c, d, _ in outs] + (rider.out_shapes if rider else []),
        grid=(plan.n_tiles,), in_specs=specs + [HBM_SPEC] * r_n,
        out_specs=[pl.BlockSpec((plan.rows, c), lambda i: (i, 0)) for c, _, _ in outs] + [HBM_SPEC] * r_n,
        scratch_shapes=scratch + (rider.semaphores if rider else []),
        compiler_params=_cparams(("arbitrary",) if rider else ("parallel",)),
    )(*operands, *(rider.operands if rider else []))


def _conv_bwd(plan, cots, dx_dtype, name, rider=None):
    n_pp, n_c = len(plan.pre_params), len(cots)
    r_n = rider.n if rider else 0
    n_tiles, rows, rs, halo, k_taps = plan.n_tiles, plan.rows, plan.rs, plan.halo, plan.k_taps

    def tile_index(i):
        return n_tiles - 1 - i

    specs, operands = plan.in_specs(tile_index)
    for ct, _ in cots:
        specs.append(pl.BlockSpec((rows, ct.shape[1]), lambda i: (tile_index(i), 0)))
        operands.append(ct)
    mid_cols = plan.n_mid * plan.width
    out_shape = [jax.ShapeDtypeStruct(plan.x.shape, dx_dtype), jax.ShapeDtypeStruct(plan.w.shape, f32),
                 jax.ShapeDtypeStruct(plan.b.shape, f32)]
    out_shape += [jax.ShapeDtypeStruct(p.shape, f32) for p, _ in plan.pre_params]
    out_specs = [pl.BlockSpec((rows, plan.x.shape[1]), lambda i: (tile_index(i), 0)),
                 pl.BlockSpec(plan.w.shape, lambda i: (0, 0)), pl.BlockSpec(plan.b.shape, lambda i: (0, 0))]
    out_specs += [pl.BlockSpec(p.shape, lambda i: (0, 0)) for p, _ in plan.pre_params]
    w_cols = plan.w.shape[1]
    scratch = [pltpu.VMEM((rows + halo, mid_cols), f32),
               pltpu.VMEM((halo, mid_cols), f32),
               pltpu.VMEM(((k_taps + 1) * SUBLANE, w_cols), f32)]
    if plan.pre:
        scratch.append(pltpu.VMEM((halo + rows, mid_cols), f32))

    def body(*refs):
        cur_ref, before_ref, w_ref, b_ref = refs[:4]
        pp_refs = refs[4:4 + n_pp]
        c_refs = refs[4 + n_pp:4 + n_pp + n_c]
        pos = 4 + n_pp + n_c
        r_srcs = refs[pos:pos + r_n]
        pos += r_n
        dx_ref, dw_ref, db_ref = refs[pos:pos + 3]
        dpp_refs = refs[pos + 3:pos + 3 + n_pp]
        r_outs = refs[pos + 3 + n_pp:pos + 3 + n_pp + r_n]
        pos += 3 + n_pp + r_n
        g_ref, carry_ref, acc_ref = refs[pos:pos + 3]
        u_ref = refs[pos + 3] if plan.pre else None
        r_sems = refs[pos + (4 if plan.pre else 3):]
        i = pl.program_id(0)
        tile_id = tile_index(i)
        if rider:
            @pl.when(i == 0)
            def _():
                rider.start(r_srcs, r_outs, r_sems)

        @pl.when(i == 0)
        def _():
            carry_ref[...] = jnp.zeros_like(carry_ref)
            acc_ref[...] = jnp.zeros_like(acc_ref)
            for dp in dpp_refs:
                dp[...] = jnp.zeros_like(dp)

        g_ref[rows:rows + halo, :] = carry_ref[...]
        if plan.pre:
            plan.fill_conv_input(cur_ref, before_ref, pp_refs, u_ref, tile_id)

        def col_loop(c, carry):
            c0 = c * LANE
            for r in range(plan.n_rs):
                hcs = plan.conv(cur_ref, before_ref, u_ref, w_ref, b_ref, tile_id, r, c0)
                _, vjp = jax.vjp(lambda *a: tuple(plan.post(*a)), *hcs)
                cvals = [c_ref[r * rs:(r + 1) * rs, _cols(cb, c0, LANE)].astype(f32)
                         for (_, bases), c_ref in zip(cots, c_refs) for cb in bases]
                d_hcs = vjp(tuple(cvals))
                for j, mb in enumerate(plan.mid_bases):
                    g_ref[r * rs:(r + 1) * rs, _cols(j * plan.width, c0, LANE)] = d_hcs[j]
                    wc = _cols(mb, c0, LANE)
                    acc_ref[k_taps * SUBLANE:(k_taps + 1) * SUBLANE, wc] += _fold8(d_hcs[j])
                    for k in range(k_taps):
                        x_k = plan.tap(cur_ref, before_ref, u_ref, tile_id, r, j, k, c0)
                        acc_ref[k * SUBLANE:(k + 1) * SUBLANE, wc] += _fold8(d_hcs[j] * x_k)
            pps = plan.pre_strips(pp_refs, c0)
            for r in range(plan.n_rs):
                d_us = []
                for j, mb in enumerate(plan.mid_bases):
                    wc = _cols(mb, c0, LANE)
                    acc = None
                    for k in range(k_taps):
                        lo = r * rs + (k_taps - 1) - k
                        term = w_ref[k:k + 1, wc] * g_ref[lo:lo + rs, _cols(j * plan.width, c0, LANE)]
                        acc = term if acc is None else acc + term
                    d_us.append(acc)
                if plan.pre is None:
                    d_xs = d_us
                else:
                    xs = [cur_ref[r * rs:(r + 1) * rs, _cols(b, c0, LANE)].astype(f32) for b in plan.in_bases]
                    _, vjp_pre = jax.vjp(lambda *a: tuple(plan.pre(*a, valid=1.0)), *xs, *pps)
                    grads = vjp_pre(tuple(d_us))
                    d_xs = grads[:len(xs)]
                    n = len(xs)
                    for (_, bases), dp in zip(plan.pre_params, dpp_refs):
                        for pb in bases:
                            dp[:, _cols(pb, c0, LANE)] += grads[n]
                            n += 1
                for b, d_x in zip(plan.in_bases, d_xs):
                    dx_ref[r * rs:(r + 1) * rs, _cols(b, c0, LANE)] = d_x.astype(dx_dtype)
            return carry

        lax.fori_loop(0, plan.n_cs, col_loop, 0)
        carry_ref[...] = g_ref[0:halo, :]

        @pl.when(i == n_tiles - 1)
        def _():
            for k in range(k_taps):
                dw_ref[k:k + 1, :] = jnp.sum(acc_ref[k * SUBLANE:(k + 1) * SUBLANE, :], axis=0, keepdims=True)
            db_ref[...] = jnp.sum(acc_ref[k_taps * SUBLANE:(k_taps + 1) * SUBLANE, :], axis=0, keepdims=True)
            if rider:
                rider.finish(r_srcs, r_outs, r_sems)

    return pl.pallas_call(
        body, name=name, out_shape=out_shape + (rider.out_shapes if rider else []), grid=(n_tiles,),
        in_specs=specs + [HBM_SPEC] * r_n, out_specs=out_specs + [HBM_SPEC] * r_n,
        scratch_shapes=scratch + (rider.semaphores if rider else []), compiler_params=_cparams(("arbitrary",)),
    )(*operands, *(rider.operands if rider else []))


def _f_rms(h, g):
    return (_rms(h, g),)


def _f_rms_res(h, g, bz):
    hh = h + bz
    return _rms(hh, g), hh


@jax.custom_vjp
def _silu_gate(gate, val):
    return jax.nn.silu(gate) * val


def _silu_gate_fwd(gate, val):
    s = jax.nn.sigmoid(gate)
    return gate * s * val, (gate, val, s)


def _silu_gate_bwd(res, d):
    gate, val, s = res
    silu = gate * s
    return d * val * (s + silu * (1.0 - s)), d * silu


_silu_gate.defvjp(_silu_gate_fwd, _silu_gate_bwd)


def _post_ffn_gate(gate, val):
    return (_silu_gate(gate, val),)


def _post_silu(h):
    return (jax.nn.silu(h),)


def _post_identity(h):
    return (h,)


def _pre_glu(g_a, g_b, b_a, b_b, *, valid):
    return ((g_a + b_a) * jax.nn.sigmoid(g_b + b_b) * valid,)


def _f_ssd_dt(dtr, dtb):
    real = lax.broadcasted_iota(jnp.int32, (1, LANE), 1) < SSD_HEADS
    return (jnp.where(real, jax.nn.softplus(dtr + dtb), 0.0),)


def _f_ssd_post(y, z, g):
    return (_rms(y * jax.nn.silu(z), g),)


CONF_HALO = 32
FFN_STRIP_ROWS = 64
CONF_STRIP_ROWS = 128
SSD_STRIP_ROWS = 64


def _f_ln_silu(x, g, b):
    return (jax.nn.silu(_layer_norm(x, g, b)),)


def _f_lru(io_ext, in_b, cw, cb, ga_w, ga_b, gx_w, gx_b, lam, *, valid):
    rows = io_ext.shape[0] - SUBLANE
    io = (io_ext + in_b) * valid
    gate = io[SUBLANE:, :LRU_W]
    xr = _causal_taps(io[:, LRU_W:], cw, SUBLANE, rows) + cb
    rs, iis = [], []
    for blk in range(LRU_W // LRU_BLOCK):
        sl = slice(blk * LRU_BLOCK, (blk + 1) * LRU_BLOCK)
        xb = xr[:, sl]
        rs.append(jax.nn.sigmoid(_dot_nn(xb, ga_w[sl, :]) + ga_b[:, sl]))
        iis.append(jax.nn.sigmoid(_dot_nn(xb, gx_w[sl, :]) + gx_b[:, sl]))
    r = jnp.concatenate(rs, axis=1)
    ig = jnp.concatenate(iis, axis=1)
    log_a = -LRU_C * r * jax.nn.softplus(-lam)
    a = jnp.exp(log_a)
    bterm = jnp.sqrt(-_expm1(2.0 * log_a)) * (ig * xr)
    return a, bterm, jax.nn.gelu(gate)


def _f_sgu(z, in_b, ln_g, ln_b, sp_w, sp_bt):
    rows = z.shape[0]
    zz = jax.nn.gelu(z + in_b)
    u, v = zz[:, :SGU_HALF], zz[:, SGU_HALF:]
    v = _layer_norm(v, ln_g, ln_b)
    tri = lax.broadcasted_iota(jnp.int32, (SGU_CHUNK, SGU_CHUNK), 0) >= lax.broadcasted_iota(
        jnp.int32, (SGU_CHUNK, SGU_CHUNK), 1)
    gdim = SGU_HALF // SGU_GROUPS
    row_blocks = []
    for ci in range(rows // SGU_CHUNK):
        col_blocks = []
        for g in range(SGU_GROUPS):
            w = jnp.where(tri, sp_w[g * SGU_CHUNK:(g + 1) * SGU_CHUNK, :], 0.0)
            vb = v[ci * SGU_CHUNK:(ci + 1) * SGU_CHUNK, g * gdim:(g + 1) * gdim]
            col_blocks.append(_dot_nn(w, vb) + sp_bt[:, g:g + 1])
        row_blocks.append(jnp.concatenate(col_blocks, axis=1))
    mixed = row_blocks[0] if len(row_blocks) == 1 else jnp.concatenate(row_blocks, axis=0)
    return (u * mixed,)


HEADS_PER_GROUP = 4
GROUP_COLS = 256
HEAD_DIM = 64


def _ssd_group(x, bm, cm, dt, st, a_log, dsk, g):
    q = x.shape[0]
    tri = lax.broadcasted_iota(jnp.int32, (q, q), 0) >= lax.broadcasted_iota(jnp.int32, (q, q), 1)
    d_a = dt * (-jnp.exp(a_log))
    acs = jnp.dot(tri.astype(f32), d_a, precision=HIGHEST, preferred_element_type=f32)
    acs_t = acs.T
    lane = lax.broadcasted_iota(jnp.int32, (1, LANE), 1)
    sub = lax.broadcasted_iota(jnp.int32, (LANE, 1), 0)
    col_idx = lax.broadcasted_iota(jnp.int32, (1, GROUP_COLS), 1)
    last_row = (lax.broadcasted_iota(jnp.int32, (q, 1), 0) == q - 1).astype(f32)
    cb = _dot_nt(cm, bm)
    y = jnp.zeros((q, GROUP_COLS), f32)
    e_in = jnp.zeros((q, GROUP_COLS), f32)
    d_end = jnp.zeros((q, GROUP_COLS), f32)
    d_last = jnp.zeros((1, GROUP_COLS), f32)
    d_skip = jnp.zeros((1, GROUP_COLS), f32)
    for j in range(HEADS_PER_GROUP):
        head = HEADS_PER_GROUP * g + j
        on_lane = (lane == head).astype(f32)
        on_sub = (sub == head).astype(f32)
        col = jnp.sum(acs * on_lane, axis=1, keepdims=True)
        row = jnp.sum(acs_t * on_sub, axis=0, keepdims=True)
        dtc = jnp.sum(dt * on_lane, axis=1, keepdims=True)
        last = jnp.sum(col * last_row, axis=0, keepdims=True)
        dsk_j = jnp.sum(dsk * on_lane, axis=1, keepdims=True)
        decay = jnp.where(tri, jnp.exp(jnp.where(tri, col - row, 0.0)), 0.0)
        mine = jnp.logical_and(col_idx >= j * HEAD_DIM, col_idx < (j + 1) * HEAD_DIM)
        y = y + _dot_nn(cb * decay, jnp.where(mine, x * dtc, 0.0))
        e_in = e_in + jnp.where(mine, jnp.exp(col), 0.0)
        d_end = d_end + jnp.where(mine, jnp.exp(last - col) * dtc, 0.0)
        d_last = d_last + jnp.where(mine, jnp.exp(last), 0.0)
        d_skip = d_skip + jnp.where(mine, dsk_j, 0.0)
    y = y + _dot_nn(cm, st) * e_in + x * d_skip
    st_new = st * d_last + _dot_tn(bm, x * d_end)
    return y, st_new


GROUPS_PER_STEP = 4


def _ssd_specs(rev, nc):
    def ch(c):
        return nc - 1 - c if rev else c

    gps = GROUPS_PER_STEP
    x_spec = pl.BlockSpec((SSD_CHUNK, gps * GROUP_COLS), lambda c, g: (ch(c), g))
    b_spec = pl.BlockSpec((SSD_CHUNK, gps * LANE), lambda c, g: (ch(c), SSD_D_INNER // (gps * LANE) + g))
    c_spec = pl.BlockSpec((SSD_CHUNK, gps * LANE), lambda c, g: (ch(c), (SSD_D_INNER + SSD_BC) // (gps * LANE) + g))
    dt_spec = pl.BlockSpec((SSD_CHUNK, LANE), lambda c, g: (ch(c), 0))
    row_spec = pl.BlockSpec((1, LANE), lambda c, g: (0, 0))
    st_spec = pl.BlockSpec((1, gps, LANE, GROUP_COLS), lambda c, g: (ch(c), g, 0, 0))
    wide_spec = pl.BlockSpec((SSD_CHUNK, SSD_CONV_DIM), lambda c, g: (ch(c), 0))
    return x_spec, b_spec, c_spec, dt_spec, row_spec, st_spec, wide_spec


def _ssd_fwd(xc, dt, a_log, dsk, gather=()):
    t_len = xc.shape[0]
    nc = t_len // SSD_CHUNK
    gps = GROUPS_PER_STEP
    n_gp = SSD_GROUPS // gps
    n_g = len(gather)
    x_spec, b_spec, c_spec, dt_spec, row_spec, st_spec, _ = _ssd_specs(False, nc)

    def body(*refs):
        x_ref, b_ref, c_ref, dt_ref, al_ref, dk_ref = refs[:6]
        g_srcs = refs[6:6 + n_g]
        y_ref, st_out_ref = refs[6 + n_g:8 + n_g]
        g_outs = refs[8 + n_g:8 + 2 * n_g]
        st_ref = refs[8 + 2 * n_g]
        g_sems = refs[9 + 2 * n_g:]
        c, gp = pl.program_id(0), pl.program_id(1)
        if n_g:
            @pl.when(jnp.logical_and(c == 0, gp == 0))
            def _():
                _gather_start(g_srcs, g_outs, *g_sems)

        for q in range(gps):
            g = gp * gps + q

            @pl.when(c == 0)
            def _():
                st_ref[g] = jnp.zeros((LANE, GROUP_COLS), f32)

            st = st_ref[g]
            st_out_ref[0, q] = st
            xq = slice(q * GROUP_COLS, (q + 1) * GROUP_COLS)
            bq = slice(q * LANE, (q + 1) * LANE)
            y, st_new = _ssd_group(x_ref[:, xq], b_ref[:, bq], c_ref[:, bq], dt_ref[...], st, al_ref[...],
                                   dk_ref[...], g)
            y_ref[:, xq] = y
            st_ref[g] = st_new

        if n_g:
            @pl.when(jnp.logical_and(c == nc - 1, gp == n_gp - 1))
            def _():
                _gather_finish(g_srcs, g_outs, *g_sems)

    res = pl.pallas_call(
        body, name="ssd_scan_fwd",
        out_shape=[jax.ShapeDtypeStruct((t_len, SSD_D_INNER), f32),
                   jax.ShapeDtypeStruct((nc, SSD_GROUPS, LANE, GROUP_COLS), f32)] + _gather_out_shapes(gather),
        grid=(nc, n_gp), in_specs=[x_spec, b_spec, c_spec, dt_spec, row_spec, row_spec] + [HBM_SPEC] * n_g,
        out_specs=[x_spec, st_spec] + [HBM_SPEC] * n_g,
        scratch_shapes=[pltpu.VMEM((SSD_GROUPS, LANE, GROUP_COLS), f32)] + (_gather_semaphores(n_g) if n_g else []),
        compiler_params=_cparams(("arbitrary", "arbitrary")),
    )(xc, xc, xc, dt, a_log, dsk, *gather)
    return res[0], res[1], list(res[2:])


def _ssd_bwd(xc, dt, a_log, dsk, states, dy, to_chips=()):
    t_len = xc.shape[0]
    nc = t_len // SSD_CHUNK
    gps = GROUPS_PER_STEP
    n_gp = SSD_GROUPS // gps
    n_s = len(to_chips)
    x_spec, b_spec, c_spec, dt_spec, row_spec, st_spec, wide_spec = _ssd_specs(True, nc)

    def body(*refs):
        x_ref, b_ref, c_ref, dt_ref, al_ref, dk_ref, st_in_ref, dy_ref = refs[:8]
        s_srcs = refs[8:8 + n_s]
        dxc_ref, ddt_ref, dal_ref, ddk_ref = refs[8 + n_s:12 + n_s]
        s_outs = refs[12 + n_s:12 + 2 * n_s]
        dst_ref = refs[12 + 2 * n_s]
        s_sems = refs[13 + 2 * n_s:]
        c, gp = pl.program_id(0), pl.program_id(1)

        @pl.when(jnp.logical_and(c == 0, gp == 0))
        def _():
            dal_ref[...] = jnp.zeros_like(dal_ref)
            ddk_ref[...] = jnp.zeros_like(ddk_ref)
            for cp in _to_chips_copies(s_srcs, s_outs, *s_sems) if n_s else []:
                cp.start()

        @pl.when(gp == 0)
        def _():
            ddt_ref[...] = jnp.zeros_like(ddt_ref)

        for q in range(gps):
            g = gp * gps + q

            @pl.when(c == 0)
            def _():
                dst_ref[g] = jnp.zeros((LANE, GROUP_COLS), f32)

            xq = slice(q * GROUP_COLS, (q + 1) * GROUP_COLS)
            bq = slice(q * LANE, (q + 1) * LANE)
            _, vjp = jax.vjp(lambda *args: _ssd_group(*args, g), x_ref[:, xq], b_ref[:, bq], c_ref[:, bq],
                             dt_ref[...], st_in_ref[0, q], al_ref[...], dk_ref[...])
            dx, db, dc, ddt, dst, dal, ddk = vjp((dy_ref[:, xq], dst_ref[g]))
            dxc_ref[:, pl.ds(pl.multiple_of(g * GROUP_COLS, GROUP_COLS), GROUP_COLS)] = dx
            dxc_ref[:, pl.ds(pl.multiple_of(SSD_D_INNER + g * LANE, LANE), LANE)] = db
            dxc_ref[:, pl.ds(pl.multiple_of(SSD_D_INNER + SSD_BC + g * LANE, LANE), LANE)] = dc
            ddt_ref[...] += ddt
            dst_ref[g] = dst
            dal_ref[...] += dal
            ddk_ref[...] += ddk

        if n_s:
            @pl.when(jnp.logical_and(c == nc - 1, gp == n_gp - 1))
            def _():
                for cp in _to_chips_copies(s_srcs, s_outs, *s_sems):
                    cp.wait()

    res = pl.pallas_call(
        body, name="ssd_scan_bwd",
        out_shape=[jax.ShapeDtypeStruct((t_len, SSD_CONV_DIM), f32), jax.ShapeDtypeStruct((t_len, LANE), f32),
                   jax.ShapeDtypeStruct((1, LANE), f32), jax.ShapeDtypeStruct((1, LANE), f32)]
        + _to_chips_out_shapes(to_chips),
        grid=(nc, n_gp),
        in_specs=[x_spec, b_spec, c_spec, dt_spec, row_spec, row_spec, st_spec, x_spec] + [HBM_SPEC] * n_s,
        out_specs=[wide_spec, dt_spec, row_spec, row_spec] + [HBM_SPEC] * n_s,
        scratch_shapes=[pltpu.VMEM((SSD_GROUPS, LANE, GROUP_COLS), f32)] + (_to_chips_semaphores(n_s) if n_s else []),
        compiler_params=_cparams(("arbitrary", "arbitrary")),
    )(xc, xc, xc, dt, a_log, dsk, states, dy, *to_chips)
    return res[0], res[1], res[2], res[3], list(res[4:])


LRU_ROWS = 256


def _lru_fwd(a, b, gg):
    t_len, cols = a.shape
    rows = min(LRU_ROWS, t_len)
    spec = pl.BlockSpec((rows, cols), lambda i: (i, 0))

    def body(a_ref, b_ref, g_ref, y_ref, h_ref, carry):
        i = pl.program_id(0)

        @pl.when(i == 0)
        def _():
            carry[...] = jnp.zeros_like(carry)

        av, bv = a_ref[...], b_ref[...]
        row = lax.broadcasted_iota(jnp.int32, av.shape, 0)
        s = 1
        while s < rows:
            a_prev = pltpu.roll(av, s, axis=0)
            b_prev = pltpu.roll(bv, s, axis=0)
            m = row >= s
            bv = jnp.where(m, av * b_prev + bv, bv)
            av = jnp.where(m, av * a_prev, av)
            s *= 2
        h = av * carry[0:1, :] + bv
        h_ref[...] = h
        y_ref[...] = g_ref[...] * h
        carry[0:1, :] = h[rows - 1:rows, :]

    return pl.pallas_call(
        body, name="lru_scan_fwd",
        out_shape=[jax.ShapeDtypeStruct((t_len, cols), f32), jax.ShapeDtypeStruct((t_len, cols), f32)],
        grid=(t_len // rows,), in_specs=[spec, spec, spec], out_specs=[spec, spec],
        scratch_shapes=[pltpu.VMEM((SUBLANE, cols), f32)],
        compiler_params=_cparams(("arbitrary",)),
    )(a, b, gg)


def _lru_bwd(dy, gg, a, h):
    t_len, cols = a.shape
    rows = min(LRU_ROWS, t_len)
    n_tiles = t_len // rows
    per = rows // SUBLANE
    spec = pl.BlockSpec((rows, cols), lambda i: (n_tiles - 1 - i, 0))
    prev_spec = pl.BlockSpec((SUBLANE, cols), lambda i: (jnp.maximum((n_tiles - 1 - i) * per - 1, 0), 0))

    def body(dy_ref, g_ref, a_ref, h_ref, hp_ref, da_ref, db_ref, dg_ref, carry_dh, carry_a):
        i = pl.program_id(0)
        tile_id = n_tiles - 1 - i

        @pl.when(i == 0)
        def _():
            carry_dh[...] = jnp.zeros_like(carry_dh)
            carry_a[...] = jnp.zeros_like(carry_a)

        av, hv, dyv = a_ref[...], h_ref[...], dy_ref[...]
        row = lax.broadcasted_iota(jnp.int32, av.shape, 0)
        dg_ref[...] = dyv * hv
        bv = dyv * g_ref[...]
        cv = jnp.where(row == rows - 1, carry_a[0:1, :], pltpu.roll(av, rows - 1, axis=0))
        s = 1
        while s < rows:
            c_next = pltpu.roll(cv, rows - s, axis=0)
            b_next = pltpu.roll(bv, rows - s, axis=0)
            m = row < rows - s
            bv = jnp.where(m, cv * b_next + bv, bv)
            cv = jnp.where(m, cv * c_next, cv)
            s *= 2
        dh = cv * carry_dh[0:1, :] + bv
        h_before = jnp.where(tile_id > 0, hp_ref[SUBLANE - 1:SUBLANE, :], jnp.zeros((1, cols), f32))
        h_prev = jnp.where(row == 0, h_before, pltpu.roll(hv, 1, axis=0))
        da_ref[...] = dh * h_prev
        db_ref[...] = dh
        carry_dh[0:1, :] = dh[0:1, :]
        carry_a[0:1, :] = av[0:1, :]

    return pl.pallas_call(
        body, name="lru_scan_bwd",
        out_shape=[jax.ShapeDtypeStruct((t_len, cols), f32)] * 3,
        grid=(n_tiles,), in_specs=[spec, spec, spec, spec, prev_spec], out_specs=[spec, spec, spec],
        scratch_shapes=[pltpu.VMEM((SUBLANE, cols), f32), pltpu.VMEM((SUBLANE, cols), f32)],
        compiler_params=_cparams(("arbitrary",)),
    )(dy, gg, a, h, h)


def _loss_head(h, target, g):
    t_len = h.shape[0]
    rows = min(512, t_len)

    def f(hv, gv, tv):
        err = _rms(hv, gv) - tv
        return 0.5 * jnp.sum(jnp.mean(err * err, axis=-1, keepdims=True), axis=0, keepdims=True)

    def body(h_ref, t_ref, g_ref, dh_ref, dg_ref, loss_ref):
        i = pl.program_id(0)

        @pl.when(i == 0)
        def _():
            dg_ref[...] = jnp.zeros_like(dg_ref)
            loss_ref[...] = jnp.zeros_like(loss_ref)

        tv = t_ref[...]
        part, vjp = jax.vjp(lambda hv, gv: f(hv, gv, tv), h_ref[...], g_ref[...])
        dh, dg = vjp(jnp.ones((1, 1), f32))
        dh_ref[...] = dh
        dg_ref[...] += dg
        loss_ref[...] += jnp.broadcast_to(part, loss_ref.shape)

    spec = pl.BlockSpec((rows, D_MODEL), lambda i: (i, 0))
    return pl.pallas_call(
        body, name="loss_head",
        out_shape=[jax.ShapeDtypeStruct((t_len, D_MODEL), f32), jax.ShapeDtypeStruct((1, D_MODEL), f32),
                   jax.ShapeDtypeStruct((1, LANE), f32)],
        grid=(t_len // rows,), in_specs=[spec, spec, pl.BlockSpec((1, D_MODEL), lambda i: (0, 0))],
        out_specs=[spec, pl.BlockSpec((1, D_MODEL), lambda i: (0, 0)), pl.BlockSpec((1, LANE), lambda i: (0, 0))],
        compiler_params=_cparams(("arbitrary",)),
    )(h, target, g)


def _as2d(a):
    return a.reshape((-1, a.shape[-1])) if a.ndim > 1 else a.reshape((1, -1))


def _row_block(rows, cols, bytes_cap=1 << 20):
    if rows * cols * 4 <= bytes_cap or rows % SUBLANE:
        return rows
    return _tile(rows, max(SUBLANE, (bytes_cap // (cols * 4)) // SUBLANE * SUBLANE), SUBLANE)


def _adamw(w, g, m, v, name):
    shape = w.shape
    w2, g2, m2, v2 = _as2d(w), _as2d(g), _as2d(m), _as2d(v)
    rows, cols = w2.shape
    rb = _row_block(rows, cols)

    def body(w_ref, g_ref, m_ref, v_ref, d_ref, nm_ref, nv_ref):
        gv = g_ref[...]
        nm = ADAM_B1 * m_ref[...] + (1.0 - ADAM_B1) * gv
        nv = ADAM_B2 * v_ref[...] + (1.0 - ADAM_B2) * jnp.square(gv)
        m_hat = nm / (1.0 - ADAM_B1 ** ADAM_STEP)
        v_hat = nv / (1.0 - ADAM_B2 ** ADAM_STEP)
        d_ref[...] = -ADAM_LR * (m_hat / (jnp.sqrt(v_hat) + ADAM_EPS) + ADAM_WD * w_ref[...])
        nm_ref[...] = nm
        nv_ref[...] = nv

    spec = pl.BlockSpec((rb, cols), lambda i: (i, 0))
    d, nm, nv = pl.pallas_call(
        body, name=name, out_shape=[jax.ShapeDtypeStruct((rows, cols), f32)] * 3,
        grid=(rows // rb,), in_specs=[spec] * 4, out_specs=[spec] * 3,
        compiler_params=_cparams(("parallel",)),
    )(w2, g2, m2, v2)
    return d.reshape(shape), nm.reshape(shape), nv.reshape(shape)


def _sum_with_sibling(g_halves, theirs, c_idx):
    n_sh, _, rows, cols = g_halves.shape
    rb = _tile(rows, 512, 2 * SUBLANE)

    def body(c_ref, mine_ref, theirs_ref, o_ref):
        o_ref[...] = (mine_ref[...] + theirs_ref[...]).astype(bf16)

    grid_spec = pltpu.PrefetchScalarGridSpec(
        num_scalar_prefetch=1, grid=(n_sh, rows // rb),
        in_specs=[pl.BlockSpec((None, None, rb, cols), lambda k, i, c_ref: (k, c_ref[0], i, 0)),
                  pl.BlockSpec((None, rb, cols), lambda k, i, c_ref: (k, i, 0))],
        out_specs=pl.BlockSpec((None, rb, cols), lambda k, i, c_ref: (k, i, 0)))
    return pl.pallas_call(
        body, name="grad_sum_sibling", out_shape=jax.ShapeDtypeStruct((n_sh, rows, cols), bf16),
        grid_spec=grid_spec, compiler_params=_cparams(("parallel", "parallel")),
    )(c_idx, g_halves, theirs)


def _sum_chips(partial, received, k_idx):
    _, rows, cols = partial.shape
    rb = _tile(rows, 512, 2 * SUBLANE)

    def body(k_ref, mine_ref, r_ref, o_ref):
        acc = mine_ref[...].astype(f32)
        for j in range(N_CHIPS - 1):
            acc = acc + r_ref[j].astype(f32)
        o_ref[...] = acc

    grid_spec = pltpu.PrefetchScalarGridSpec(
        num_scalar_prefetch=1, grid=(rows // rb,),
        in_specs=[pl.BlockSpec((None, rb, cols), lambda i, k_ref: (k_ref[0], i, 0)),
                  pl.BlockSpec((N_CHIPS - 1, rb, cols), lambda i, k_ref: (0, i, 0))],
        out_specs=pl.BlockSpec((rb, cols), lambda i, k_ref: (i, 0)))
    return pl.pallas_call(
        body, name="grad_sum_chips", out_shape=jax.ShapeDtypeStruct((rows, cols), f32),
        grid_spec=grid_spec, compiler_params=_cparams(("parallel",)),
    )(k_idx, partial, received)


HBM_SPEC = pl.BlockSpec(memory_space=pltpu.HBM)
CHIP_FLIPS = ((0, 1), (1, 0), (1, 1))


def _position():
    return lax.axis_index("x"), lax.axis_index("y"), lax.axis_index("c")


def _own_slot(gathered, mine, index):
    return [lax.dynamic_update_index_in_dim(g, m, index, 0) for g, m in zip(gathered, mine)]


def _gather_weights(blocks):
    n = len(blocks)

    def body(*refs):
        srcs, outs = refs[:n], refs[n:2 * n]
        send_sems, recv_sems = refs[2 * n:]
        _gather_start(srcs, outs, send_sems, recv_sems)
        _gather_finish(srcs, outs, send_sems, recv_sems)

    return pl.pallas_call(
        body, name="gather_weights", out_shape=_gather_out_shapes(blocks),
        in_specs=[HBM_SPEC] * n, out_specs=[HBM_SPEC] * n, scratch_shapes=_gather_semaphores(n),
    )(*blocks)


def _gather_out_shapes(blocks):
    return [jax.ShapeDtypeStruct((N_CHIPS,) + b.shape, b.dtype) for b in blocks]


def _gather_semaphores(n):
    n_sem = 2 * len(CHIP_FLIPS) * n
    return [pltpu.SemaphoreType.DMA((n_sem,)), pltpu.SemaphoreType.DMA((n_sem,))]


def _gather_copies(srcs, outs, send_sems, recv_sems):
    n_far = len(CHIP_FLIPS)
    x, y, c = _position()
    k = 2 * x + y
    first, passed = [], []
    for a in range(len(srcs)):
        for j, (fx, fy) in enumerate(CHIP_FLIPS):
            s = a * 2 * n_far + j
            kk = 2 * (x ^ fx) + (y ^ fy)
            first.append(pltpu.make_async_remote_copy(
                src_ref=srcs[a].at[c], dst_ref=outs[a].at[k, c], send_sem=send_sems.at[s],
                recv_sem=recv_sems.at[s], device_id=(x ^ fx, y ^ fy, c), device_id_type=MESH))
            passed.append(pltpu.make_async_remote_copy(
                src_ref=outs[a].at[kk, c], dst_ref=outs[a].at[kk, c], send_sem=send_sems.at[s + n_far],
                recv_sem=recv_sems.at[s + n_far], device_id=(x, y, 1 - c), device_id_type=MESH))
    return first, passed


def _gather_start(srcs, outs, send_sems, recv_sems):
    first, _ = _gather_copies(srcs, outs, send_sems, recv_sems)
    for cp in first:
        cp.start()


def _gather_finish(srcs, outs, send_sems, recv_sems):
    first, passed = _gather_copies(srcs, outs, send_sems, recv_sems)
    for arrived, onward in zip(first, passed):
        arrived.wait_recv()
        onward.start()
    for cp in passed:
        cp.wait_recv()
    for cp in first + passed:
        cp.wait_send()


def _swap_with_sibling(grads):
    rider = _swap_rider(grads)
    n = len(grads)

    def body(*refs):
        rider.start(refs[:n], refs[n:2 * n], refs[2 * n:])
        rider.finish(refs[:n], refs[n:2 * n], refs[2 * n:])

    return pl.pallas_call(
        body, name="grad_swap_sibling", out_shape=rider.out_shapes,
        in_specs=[HBM_SPEC] * n, out_specs=[HBM_SPEC] * n, scratch_shapes=rider.semaphores,
    )(*grads)


class _Rider:
    def __init__(self, operands, out_shapes, semaphores, start, finish):
        self.operands, self.out_shapes, self.semaphores = list(operands), list(out_shapes), list(semaphores)
        self.start, self.finish = start, finish
        self.n = len(self.operands)


def _swap_copies(srcs, outs, send_sems, recv_sems):
    x, y, c = _position()
    copies = []
    for a in range(len(srcs)):
        for kk in range(N_CHIPS):
            s = a * N_CHIPS + kk
            copies.append(pltpu.make_async_remote_copy(
                src_ref=srcs[a].at[kk, 1 - c], dst_ref=outs[a].at[kk], send_sem=send_sems.at[s],
                recv_sem=recv_sems.at[s], device_id=(x, y, 1 - c), device_id_type=MESH))
    return copies


def _swap_rider(grads):
    n_sem = N_CHIPS * len(grads)

    def start(srcs, outs, sems):
        for cp in _swap_copies(srcs, outs, *sems):
            cp.start()

    def finish(srcs, outs, sems):
        for cp in _swap_copies(srcs, outs, *sems):
            cp.wait()

    return _Rider(grads, [jax.ShapeDtypeStruct((N_CHIPS,) + g.shape[2:], g.dtype) for g in grads],
                  [pltpu.SemaphoreType.DMA((n_sem,)), pltpu.SemaphoreType.DMA((n_sem,))], start, finish)


def _gather_rider(blocks):
    def start(srcs, outs, sems):
        _gather_start(srcs, outs, *sems)

    def finish(srcs, outs, sems):
        _gather_finish(srcs, outs, *sems)

    return _Rider(blocks, _gather_out_shapes(blocks), _gather_semaphores(len(blocks)), start, finish)


def _send_to_chips(partials):
    n = len(partials)

    def body(*refs):
        srcs, outs = refs[:n], refs[n:2 * n]
        send_sems, recv_sems = refs[2 * n:]
        for cp in _to_chips_copies(srcs, outs, send_sems, recv_sems):
            cp.start()
        for cp in _to_chips_copies(srcs, outs, send_sems, recv_sems):
            cp.wait()

    return pl.pallas_call(
        body, name="grad_to_chips", out_shape=_to_chips_out_shapes(partials),
        in_specs=[HBM_SPEC] * n, out_specs=[HBM_SPEC] * n, scratch_shapes=_to_chips_semaphores(n),
    )(*partials)


def _to_chips_out_shapes(partials):
    return [jax.ShapeDtypeStruct((len(CHIP_FLIPS),) + p.shape[1:], p.dtype) for p in partials]


def _to_chips_semaphores(n):
    n_sem = len(CHIP_FLIPS) * n
    return [pltpu.SemaphoreType.DMA((n_sem,)), pltpu.SemaphoreType.DMA((n_sem,))]


def _to_chips_copies(srcs, outs, send_sems, recv_sems):
    n_far = len(CHIP_FLIPS)
    x, y, c = _position()
    copies = []
    for a in range(len(srcs)):
        for j, (fx, fy) in enumerate(CHIP_FLIPS):
            s = a * n_far + j
            kk = 2 * (x ^ fx) + (y ^ fy)
            copies.append(pltpu.make_async_remote_copy(
                src_ref=srcs[a].at[kk], dst_ref=outs[a].at[j], send_sem=send_sems.at[s],
                recv_sem=recv_sems.at[s], device_id=(x ^ fx, y ^ fy, c), device_id_type=MESH))
    return copies


def _join_halves(halves):
    n = len(halves)

    def body(*refs):
        srcs, outs = refs[:n], refs[n:2 * n]
        send_sems, recv_sems = refs[2 * n:]
        x, y, c = _position()
        copies = []
        for a in range(n):
            cp = pltpu.make_async_remote_copy(
                src_ref=srcs[a], dst_ref=outs[a].at[c], send_sem=send_sems.at[a], recv_sem=recv_sems.at[a],
                device_id=(x, y, 1 - c), device_id_type=MESH)
            cp.start()
            copies.append(cp)
        for cp in copies:
            cp.wait()

    return pl.pallas_call(
        body, name="grad_join_halves",
        out_shape=[jax.ShapeDtypeStruct((2,) + h.shape, h.dtype) for h in halves],
        in_specs=[HBM_SPEC] * n, out_specs=[HBM_SPEC] * n,
        scratch_shapes=[pltpu.SemaphoreType.DMA((n,)), pltpu.SemaphoreType.DMA((n,))],
    )(*halves)


def _all_sum_small(vec):
    rows, cols = vec.shape

    def body(v_ref, o_ref, buf, send_sems, recv_sems):
        x, y, c = _position()
        me = 4 * x + 2 * y + c
        buf[me] = v_ref[...]
        copies = []
        for m in range(1, N_DEV):
            fx, fy, fc = (m >> 2) & 1, (m >> 1) & 1, m & 1
            cp = pltpu.make_async_remote_copy(
                src_ref=v_ref, dst_ref=buf.at[me], send_sem=send_sems.at[m - 1], recv_sem=recv_sems.at[m - 1],
                device_id=(x ^ fx, y ^ fy, c ^ fc), device_id_type=MESH)
            cp.start()
            copies.append(cp)
        for cp in copies:
            cp.wait()
        acc = buf[0]
        for d in range(1, N_DEV):
            acc = acc + buf[d]
        o_ref[...] = acc

    return pl.pallas_call(
        body, name="all_sum_small", out_shape=jax.ShapeDtypeStruct((rows, cols), f32),
        in_specs=[pl.BlockSpec(memory_space=pltpu.VMEM)], out_specs=pl.BlockSpec(memory_space=pltpu.VMEM),
        scratch_shapes=[pltpu.VMEM((N_DEV, rows, cols), f32), pltpu.SemaphoreType.DMA((N_DEV - 1,)),
                        pltpu.SemaphoreType.DMA((N_DEV - 1,))],
        compiler_params=_cparams(),
    )(vec)


FLAT_QUANTUM = 2 * 2 * SUBLANE * FLAT_COLS


def _pack(arrays, dtype):
    flat = jnp.concatenate([a.astype(dtype).reshape(-1) for a in arrays])
    n = flat.shape[0]
    n_pad = -(-n // FLAT_QUANTUM) * FLAT_QUANTUM
    return jnp.pad(flat, (0, n_pad - n))


def _unpack(flat, shapes):
    out, off = [], 0
    for s in shapes:
        n = int(np.prod(s))
        out.append(flat[..., off:off + n].reshape(flat.shape[:-1] + tuple(s)))
        off += n
    return out


def _full_from_shards(stacked, axis):
    return jnp.concatenate([stacked[k] for k in range(N_CHIPS)], axis=axis)


def _shards_of(full, axis):
    return jnp.stack(jnp.split(full, N_CHIPS, axis=axis))


def _ffn_fwd(h, u, p, next_gain):
    a = _mm_w(u, p['up'], 'nn', "ffn_up")
    gated = _conv_fwd(_ffn_conv_plan(a, p), [(FFN_H, bf16, (0,))], "ffn_gate")[0]
    h_out, u_next = _mm_normed(gated, p['down'], 'nn', "ffn_down", add=h, rms_gain=next_gain)
    return h_out, u_next, (h, u, a, gated)


def _ffn_conv_plan(a, p):
    both = (0, FFN_H)
    return _ConvPlan(a, p['cw'], p['cb'], in_bases=both, mid_bases=both, width=FFN_H, rows=256, rs=FFN_STRIP_ROWS,
                     post=_post_ffn_gate)


def _ffn_bwd(dh_out, p, saved, bias_zero, make_rider=None):
    h, u, a, gated = saved
    d_gated = _mm(dh_out, p['down'], 'nt', "ffn_down_dx", out_dtype=bf16)
    d_down = _mm(gated, dh_out, 'tn', "ffn_down_dw")
    rider = make_rider(d_down) if make_rider else None
    res = _conv_bwd(_ffn_conv_plan(a, p), [(d_gated, (0,))], bf16, "ffn_gate_bwd", rider=rider)
    da, d_cw, d_cb = res[:3]
    d_up = _mm(u, da, 'tn', "ffn_up_dw", out_cols_sharded=True)
    dh, d_g, d_bias = _mm_w(da, p['up'], 'nt', "ffn_up_dx", rms_bwd=(h, p['g'], dh_out, True))
    return dh, {'g': d_g, 'up': d_up, 'down': d_down, 'cw': d_cw, 'cb': d_cb}, d_bias, list(res[3:])


def _mixer_norm_bwd(h, g, du, dh_res, name):
    def f(hv, gv):
        return _rms(hv, gv), hv

    dh, d_g = _row_bwd(f, [h], [g], [du, dh_res], rows=512, name=name)
    return dh, d_g


def _ssd_layer_fwd(h, u, p, next_gain, gather=()):
    z = _mm(u, p['w_z'], 'nn', "ssd_in_z")
    xbc = _mm(u, p['w_xbc'], 'nn', "ssd_in_xbc")
    dtr = _mm(u, p['w_dt'], 'nn', "ssd_in_dt")
    xc = _conv_fwd(_ssd_conv_plan(xbc, p), [(SSD_CONV_DIM, f32, (0,))], "ssd_conv")[0]
    dt = _row_fwd(_f_ssd_dt, [dtr], [p['dtb']], [(LANE, f32)], rows=1024, name="ssd_dt")[0]
    y, states, gathered = _ssd_fwd(xc, dt, p['a_log'], p['dsk'], gather)
    yn = _row_fwd(_f_ssd_post, [y, z], [p['norm']], [(SSD_D_INNER, bf16)], rows=256, name="ssd_gate_norm")[0]
    h_out, u_next = _mm_normed(yn, p['out'], 'nn', "ssd_out", add=h, rms_gain=next_gain)
    return h_out, u_next, (h, u, z, xbc, dtr, xc, dt, states, y, yn), gathered


def _ssd_conv_plan(xbc, p):
    return _ConvPlan(xbc, p['cw'], p['cb'], in_bases=(0,), mid_bases=(0,), width=SSD_CONV_DIM, rows=256,
                     rs=SSD_STRIP_ROWS, post=_post_silu)


def _ssd_layer_bwd(dh_out, p, saved, to_chips=()):
    h, u, z, xbc, dtr, xc, dt, states, y, yn = saved
    d_yn = _mm(dh_out, p['out'], 'nt', "ssd_out_dx", out_dtype=bf16)
    d_out = _mm(yn, dh_out, 'tn', "ssd_out_dw")
    dy, dz, d_norm = _row_bwd(_f_ssd_post, [y, z], [p['norm']], [d_yn], rows=256, name="ssd_gate_norm_bwd",
                              tile_dtypes=[f32, bf16])
    dxc, ddt, d_alog, d_dsk, received = _ssd_bwd(xc, dt, p['a_log'], p['dsk'], states, dy, to_chips)
    dxbc, d_cw, d_cb = _conv_bwd(_ssd_conv_plan(xbc, p), [(dxc, (0,))], bf16, "ssd_conv_bwd")
    ddtr, d_dtb = _row_bwd(_f_ssd_dt, [dtr], [p['dtb']], [ddt], rows=1024, name="ssd_dt_bwd", tile_dtypes=[bf16])
    d_wz = _mm(u, dz, 'tn', "ssd_in_z_dw")
    d_wxbc = _mm(u, dxbc, 'tn', "ssd_in_xbc_dw")
    d_wdt = _mm(u, ddtr, 'tn', "ssd_in_dt_dw")
    du = _mm(dz, p['w_z'], 'nt', "ssd_in_z_dx")
    du = _mm(dxbc, p['w_xbc'], 'nt', "ssd_in_xbc_dx", add=du)
    dh, d_g = _mm(ddtr, p['w_dt'], 'nt', "ssd_in_dt_dx", add=du, rms_bwd=(h, p['g'], dh_out, False))
    grads = {'g': d_g, 'w_z': d_wz, 'w_xbc': d_wxbc, 'w_dt': d_wdt, 'cw': d_cw, 'cb': d_cb, 'dtb': d_dtb,
             'a_log': d_alog, 'dsk': d_dsk, 'norm': d_norm, 'out': d_out}
    return dh, grads, received


def _conf_layer_fwd(h, u, p, next_gain, rider=None):
    g2 = _mm_w(u, p['pw1'], 'nn', "conf_pw1")
    res = _conv_fwd(_conf_conv_plan(g2, p), [(D_MODEL, f32, (0,))], "conf_conv", rider=rider)
    conv = res[0]
    s = _row_fwd(_f_ln_silu, [conv], [p['ln_g'], p['ln_b']], [(D_MODEL, bf16)], rows=256, name="conf_ln")[0]
    h_out, u_next = _mm_normed(s, p['pw2'], 'nn', "conf_pw2", bias=p['b2'], add=h, rms_gain=next_gain)
    return h_out, u_next, (h, u, g2, conv, s), list(res[1:])


def _conf_conv_plan(g2, p):
    halves = (0, D_MODEL)
    return _ConvPlan(g2, p['dw_w'], p['dw_b'], in_bases=halves, mid_bases=(0,), width=D_MODEL, rows=256,
                     rs=CONF_STRIP_ROWS, pre=_pre_glu, pre_params=[(p['b1'], halves)], post=_post_identity)


def _conf_layer_bwd(dh_out, p, saved):
    h, u, g2, conv, s = saved
    ds = _mm(dh_out, p['pw2'], 'nt', "conf_pw2_dx", out_dtype=bf16)
    d_pw2 = _mm(s, dh_out, 'tn', "conf_pw2_dw")
    d_conv, d_lng, d_lnb = _row_bwd(_f_ln_silu, [conv], [p['ln_g'], p['ln_b']], [ds], rows=256, name="conf_ln_bwd")
    dg2, d_dww, d_dwb, d_b1 = _conv_bwd(_conf_conv_plan(g2, p), [(d_conv, (0,))], bf16, "conf_conv_bwd")
    d_pw1 = _mm(u, dg2, 'tn', "conf_pw1_dw", out_cols_sharded=True)
    dh, d_g = _mm_w(dg2, p['pw1'], 'nt', "conf_pw1_dx", rms_bwd=(h, p['g'], dh_out, False))
    grads = {'g': d_g, 'pw1': d_pw1, 'b1': d_b1, 'dw_w': d_dww, 'dw_b': d_dwb, 'ln_g': d_lng, 'ln_b': d_lnb,
             'pw2': d_pw2}
    return dh, grads


def _lru_params(p):
    return [p['in_b'], p['cw'], p['cb'], p['ga_w'], p['ga_b'], p['gx_w'], p['gx_b'], p['lam']]


def _lru_layer_fwd(h, u, p, next_gain):
    io = _mm_w(u, p['in_w'], 'nn', "lru_in")
    a, b, gg = _row_fwd(_f_lru, [io], _lru_params(p), [(LRU_W, f32)] * 3, rows=256, name="lru_gates",
                        halo=SUBLANE, halo_of=[True])
    y, hs = _lru_fwd(a, b, gg)
    h_out, u_next = _mm_normed(y, p['out'], 'nn', "lru_out", bias=p['out_b'], add=h, rms_gain=next_gain)
    return h_out, u_next, (h, u, io, a, gg, hs, y)


def _lru_layer_bwd(dh_out, p, saved):
    h, u, io, a, gg, hs, y = saved
    dy = _mm(dh_out, p['out'], 'nt', "lru_out_dx")
    d_out = _mm(y, dh_out, 'tn', "lru_out_dw")
    da, db, dgg = _lru_bwd(dy, gg, a, hs)
    res = _row_bwd(_f_lru, [io], _lru_params(p), [da, db, dgg], rows=256, name="lru_gates_bwd",
                   halo=SUBLANE, halo_of=[True], tile_dtypes=[bf16])
    dio, d_inb, d_cw, d_cb, d_gaw, d_gab, d_gxw, d_gxb, d_lam = res
    d_inw = _mm(u, dio, 'tn', "lru_in_dw", out_cols_sharded=True)
    dh, d_g = _mm_w(dio, p['in_w'], 'nt', "lru_in_dx", rms_bwd=(h, p['g'], dh_out, False))
    grads = {'g': d_g, 'in_w': d_inw, 'in_b': d_inb, 'cw': d_cw, 'cb': d_cb, 'ga_w': d_gaw, 'ga_b': d_gab,
             'gx_w': d_gxw, 'gx_b': d_gxb, 'lam': d_lam, 'out': d_out}
    return dh, grads


def _sgu_params(p):
    return [p['in_b'], p['ln_g'], p['ln_b'], p['sp_w'], p['sp_bt']]


def _sgu_layer_fwd(h, u, p, next_gain):
    z = _mm_w(u, p['in_w'], 'nn', "sgu_in")
    s = _row_fwd(_f_sgu, [z], _sgu_params(p), [(SGU_HALF, bf16)], rows=SGU_CHUNK, name="sgu_mix")[0]
    h_out, u_next = _mm_normed(s, p['out'], 'nn', "sgu_out", bias=p['out_b'], add=h, rms_gain=next_gain)
    return h_out, u_next, (h, u, z, s)


def _sgu_layer_bwd(dh_out, p, saved):
    h, u, z, s = saved
    ds = _mm(dh_out, p['out'], 'nt', "sgu_out_dx", out_dtype=bf16)
    d_out = _mm(s, dh_out, 'tn', "sgu_out_dw")
    dz, d_inb, d_lng, d_lnb, d_spw, d_spbt = _row_bwd(_f_sgu, [z], _sgu_params(p), [ds], rows=SGU_CHUNK,
                                                      name="sgu_mix_bwd", tile_dtypes=[bf16])
    d_inw = _mm(u, dz, 'tn', "sgu_in_dw", out_cols_sharded=True)
    dh, d_g = _mm_w(dz, p['in_w'], 'nt', "sgu_in_dx", rms_bwd=(h, p['g'], dh_out, False))
    grads = {'g': d_g, 'in_w': d_inw, 'in_b': d_inb, 'ln_g': d_lng, 'ln_b': d_lnb, 'sp_w': d_spw, 'sp_bt': d_spbt,
             'out': d_out}
    return dh, grads


def _row(v):
    return v.reshape((1, -1)).astype(f32)


def _pad_lanes(v, n=LANE):
    v = _row(v)
    return jnp.pad(v, ((0, 0), (0, n - v.shape[1])))


def _local_step(x, target, w, comm=None):
    a_in = w['a_in_proj'][0]
    pa = {'g': _row(w['norm_mix'][0]), 'w_z': a_in[:, :SSD_D_INNER],
          'w_xbc': a_in[:, SSD_D_INNER:SSD_D_INNER + SSD_CONV_DIM],
          'w_dt': jnp.pad(a_in[:, SSD_D_INNER + SSD_CONV_DIM:], ((0, 0), (0, LANE - SSD_HEADS))),
          'cw': w['a_conv_w'][0].astype(f32), 'cb': _row(w['a_conv_b'][0]), 'dtb': _pad_lanes(w['a_dt_bias'][0]),
          'a_log': _pad_lanes(w['a_log'][0]), 'dsk': _pad_lanes(w['a_d_skip'][0]), 'norm': _row(w['a_norm'][0]),
          'out': w['a_out_proj']}
    mix_gain = [_row(w['norm_mix'][i]) for i in range(DEPTH)] + [None]
    ffn_gain = [_row(w['norm_ffn'][i]) for i in range(DEPTH)]
    u = _row_fwd(_f_rms, [x], [mix_gain[0]], [(D_MODEL, bf16)], rows=512, name="first_norm")[0]
    h, u, s_mix0, gathered = _ssd_layer_fwd(x, u, pa, ffn_gain[0], gather=comm.late_blocks if comm else ())
    if comm:
        w = {**w, **comm.late_weights(gathered)}
    ffn = [{'g': _row(w['norm_ffn'][i]), 'up': (w['f_up_w'], i), 'down': w['f_down_w'][i],
            'cw': w['f_conv_w'][i].astype(f32), 'cb': _row(w['f_conv_b'][i])} for i in range(DEPTH)]
    pb = {'g': _row(w['norm_mix'][1]), 'pw1': (w['b_pw1_w'], 0), 'b1': _row(w['b_pw1_b'][0]),
          'dw_w': w['b_dw_w'][0].astype(f32), 'dw_b': _row(w['b_dw_b'][0]), 'ln_g': _row(w['b_ln_g'][0]),
          'ln_b': _row(w['b_ln_b'][0]), 'pw2': w['b_pw2_w'], 'b2': _row(w['b_pw2_b'][0])}
    h, u, s_ffn0 = _ffn_fwd(h, u, ffn[0], mix_gain[1])
    h, u, s_mix1, gathered = _conf_layer_fwd(h, u, pb, ffn_gain[1], rider=comm.second_rider() if comm else None)
    if comm:
        w = {**w, **comm.second_weights(gathered)}
    pc = {'g': _row(w['norm_mix'][2]), 'in_w': (w['c_in_w'], 0), 'in_b': _row(w['c_in_b'][0]),
          'cw': w['c_conv_w'][0].astype(f32), 'cb': _row(w['c_conv_b'][0]),
          'ga_w': w['c_ga_w'][0].reshape(LRU_W, LRU_BLOCK).astype(f32), 'ga_b': _row(w['c_ga_b'][0]),
          'gx_w': w['c_gx_w'][0].reshape(LRU_W, LRU_BLOCK).astype(f32), 'gx_b': _row(w['c_gx_b'][0]),
          'lam': _row(w['c_lambda'][0]), 'out': w['c_out_w'], 'out_b': _row(w['c_out_b'][0])}
    pd = {'g': _row(w['norm_mix'][3]), 'in_w': (w['d_in_w'], 0), 'in_b': _row(w['d_in_b'][0]),
          'ln_g': _row(w['d_ln_g'][0]), 'ln_b': _row(w['d_ln_b'][0]),
          'sp_w': w['d_sp_w'][0].reshape(SGU_GROUPS * SGU_CHUNK, SGU_CHUNK).astype(f32),
          'sp_bt': w['d_sp_b'][0].astype(f32).T, 'out': w['d_out_w'], 'out_b': _row(w['d_out_b'][0])}
    mixers = [(None, None, pa), (None, _conf_layer_bwd, pb),
              (_lru_layer_fwd, _lru_layer_bwd, pc), (_sgu_layer_fwd, _sgu_layer_bwd, pd)]

    h, u, s_ffn1 = _ffn_fwd(h, u, ffn[1], mix_gain[2])
    saved = [(s_mix0, s_ffn0), (s_mix1, s_ffn1)]
    for i in range(2, DEPTH):
        fwd, _, p = mixers[i]
        h, u, s_mix = fwd(h, u, p, ffn_gain[i])
        h, u, s_ffn = _ffn_fwd(h, u, ffn[i], mix_gain[i + 1])
        saved.append((s_mix, s_ffn))
    dh, d_final, loss = _loss_head(h, target, _row(w['norm_final']))

    def rows_sharded(g):
        return g.reshape(N_CHIPS, g.shape[0] // N_CHIPS, g.shape[1])

    bias_zero = jnp.zeros((1, D_MODEL), f32)
    g_ffn, g_mix, d_out_bias = [None] * DEPTH, [None] * DEPTH, [None] * DEPTH
    for i in reversed(range(1, DEPTH)):
        _, bwd, p = mixers[i]
        dh, g_ffn[i], d_out_bias[i], _ = _ffn_bwd(dh, ffn[i], saved[i][1], bias_zero)
        dh, g_mix[i] = bwd(dh, p, saved[i][0])
    _, gb, gc, gd = g_mix

    def late_direct_grads(d_up0, d_down0):
        return {'b_pw1_w': gb['pw1'], 'b_pw2_w': rows_sharded(gb['pw2']), 'c_in_w': gc['in_w'],
                'c_out_w': rows_sharded(gc['out']), 'd_in_w': gd['in_w'], 'd_out_w': rows_sharded(gd['out']),
                'f_up_w': [d_up0] + [g['up'] for g in g_ffn[1:]],
                'f_down_w': [rows_sharded(d_down0)] + [rows_sharded(g['down']) for g in g_ffn[1:]]}

    make_rider = (lambda d_down0: comm.swap_rider(late_direct_grads(None, d_down0))) if comm else None
    dh, g_ffn[0], d_out_bias[0], swapped = _ffn_bwd(dh, ffn[0], saved[0][1], bias_zero, make_rider)
    late_direct = late_direct_grads(g_ffn[0]['up'], g_ffn[0]['down'])
    partials = comm.early_partials(late_direct, swapped) if comm else []
    dh, g_mix[0], received = _ssd_layer_bwd(dh, pa, saved[0][0], to_chips=partials)
    ga = g_mix[0]

    grads = {**late_direct,
        'norm_mix': jnp.concatenate([g['g'] for g in g_mix], axis=0),
        'norm_ffn': jnp.concatenate([g['g'] for g in g_ffn], axis=0),
        'norm_final': d_final.reshape(-1),
        'a_in_proj': jnp.concatenate([ga['w_z'], ga['w_xbc'], ga['w_dt'][:, :SSD_HEADS]], axis=1)[None],
        'a_conv_w': ga['cw'][None], 'a_conv_b': ga['cb'], 'a_dt_bias': ga['dtb'][:, :SSD_HEADS],
        'a_log': ga['a_log'][:, :SSD_HEADS], 'a_d_skip': ga['dsk'][:, :SSD_HEADS], 'a_norm': ga['norm'],
        'a_out_proj': rows_sharded(ga['out']),
        'b_pw1_b': gb['b1'], 'b_dw_w': gb['dw_w'][None], 'b_dw_b': gb['dw_b'],
        'b_ln_g': gb['ln_g'], 'b_ln_b': gb['ln_b'], 'b_pw2_b': d_out_bias[1],
        'c_in_b': gc['in_b'], 'c_conv_w': gc['cw'][None], 'c_conv_b': gc['cb'],
        'c_ga_w': gc['ga_w'].reshape(1, LRU_W // LRU_BLOCK, LRU_BLOCK, LRU_BLOCK),
        'c_ga_b': gc['ga_b'].reshape(1, LRU_W // LRU_BLOCK, LRU_BLOCK),
        'c_gx_w': gc['gx_w'].reshape(1, LRU_W // LRU_BLOCK, LRU_BLOCK, LRU_BLOCK),
        'c_gx_b': gc['gx_b'].reshape(1, LRU_W // LRU_BLOCK, LRU_BLOCK),
        'c_lambda': gc['lam'], 'c_out_b': d_out_bias[2],
        'd_in_b': gd['in_b'], 'd_ln_g': gd['ln_g'], 'd_ln_b': gd['ln_b'],
        'd_sp_w': gd['sp_w'].reshape(1, SGU_GROUPS, SGU_CHUNK, SGU_CHUNK), 'd_sp_b': gd['sp_bt'].T[None],
        'd_out_b': d_out_bias[3],
        'f_conv_w': jnp.stack([g['cw'] for g in g_ffn]),
        'f_conv_b': jnp.concatenate([g['cb'] for g in g_ffn], axis=0),
    }
    return loss, dh, grads, (partials, received)


def _global_shape(name, shard_shape):
    ax = SHARD_AXIS[name]
    if ax is None:
        return tuple(shard_shape)
    s = list(shard_shape)
    s[ax] *= N_CHIPS
    return tuple(s)


def _step(x, target, weights, moments_m, moments_v):
    x2, t2 = x[0], target[0]
    shard_shapes = {n: weights[n].shape for n in WEIGHTS}
    c_pos = lax.axis_index("c")
    k_pos = 2 * lax.axis_index("x") + lax.axis_index("y")
    c_idx = c_pos.astype(jnp.int32).reshape(1)
    k_idx = k_pos.astype(jnp.int32).reshape(1)

    def halves_of(a):
        a2 = _as2d(a)
        return a2.reshape(2, a2.shape[0] // 2, a2.shape[1])

    def view_direct(n, g):
        g = g.reshape((N_CHIPS,) + shard_shapes[n])
        if n in DIRECT_COLS:
            return g
        if n == 'f_down_w':
            return [g[:, i].reshape(-1, g.shape[-1]) for i in range(DEPTH)]
        return g.reshape(-1, g.shape[-1])

    def sibling_sums(grads):
        mine_g = [g.reshape(N_CHIPS, 2, g.shape[1] // 2, g.shape[2]) for g in grads]
        theirs = _swap_with_sibling(mine_g)
        return [_sum_with_sibling(g, t, c_idx) for g, t in zip(mine_g, theirs)]

    def flatten_direct(grads, names):
        out = []
        for n in names:
            out += grads[n] if isinstance(grads[n], list) else [grads[n]]
        return out

    first = [halves_of(weights[n].astype(bf16)) for n in EARLY_DIRECT]
    first.append(_pack([weights[n] for n in PACKED_MM], bf16).reshape(2, -1, FLAT_COLS))
    first.append(_pack([weights[n] for n in SHARDED_VEC], f32).reshape(2, -1, FLAT_COLS))
    gathered = _own_slot(_gather_weights(first), first, k_pos)
    w = {n: weights[n] for n in REPLICATED}
    for n, g in zip(EARLY_DIRECT, gathered):
        w[n] = view_direct(n, g)
    all_mm = _unpack(gathered[-2].reshape(N_CHIPS, -1), [shard_shapes[n] for n in PACKED_MM])
    all_vec = _unpack(gathered[-1].reshape(N_CHIPS, -1), [shard_shapes[n] for n in SHARDED_VEC])
    for n, st in zip(PACKED_MM + SHARDED_VEC, all_mm + all_vec):
        w[n] = _full_from_shards(st, SHARD_AXIS[n])

    def halved(g):
        return g.reshape(N_CHIPS, 2, g.shape[1] // 2, g.shape[2])

    class Comm:
        late_blocks = [halves_of(weights[n].astype(bf16)) for n in LATE_FIRST]
        second_blocks = [halves_of(weights[n].astype(bf16)) for n in LATE_SECOND]

        @staticmethod
        def late_weights(arrived):
            arrived = _own_slot(arrived, Comm.late_blocks, k_pos)
            return {n: view_direct(n, g) for n, g in zip(LATE_FIRST, arrived)}

        @staticmethod
        def second_rider():
            return _gather_rider(Comm.second_blocks)

        @staticmethod
        def second_weights(arrived):
            arrived = _own_slot(arrived, Comm.second_blocks, k_pos)
            return {n: view_direct(n, g) for n, g in zip(LATE_SECOND, arrived)}

        @staticmethod
        def swap_rider(grads):
            return _swap_rider([halved(g) for g in flatten_direct(grads, LATE_DIRECT) if g is not None])

        @staticmethod
        def early_partials(grads, swapped):
            mine_g = [halved(g) for g in flatten_direct(grads, LATE_DIRECT)]
            missing = LAST_SWAPPED_INDEX
            theirs = swapped[:missing] + list(_swap_with_sibling([mine_g[missing]])) + swapped[missing:]
            return [_sum_with_sibling(g, t, c_idx) for g, t in zip(mine_g, theirs)]

    loss_part, dx, grads, (partials, received) = _local_step(x2, t2, w, Comm)

    packed = [_shards_of(grads[n].reshape(_global_shape(n, shard_shapes[n])), SHARD_AXIS[n]).reshape(N_CHIPS, -1)
              for n in PACKED_MM + SHARDED_VEC]
    flat = jnp.concatenate(packed, axis=1)
    n_flat = flat.shape[1]
    n_pad = -(-n_flat // FLAT_QUANTUM) * FLAT_QUANTUM
    flat = jnp.pad(flat, ((0, 0), (0, n_pad - n_flat))).reshape(N_CHIPS, -1, FLAT_COLS)
    last_partials = sibling_sums(flatten_direct(grads, EARLY_DIRECT) + [flat])
    last_received = _send_to_chips(last_partials)
    partials, received = list(partials) + last_partials, list(received) + list(last_received)
    my_halves = [_sum_chips(p, r, k_idx) for p, r in zip(partials, received)]
    joined = _own_slot(_join_halves(my_halves), my_halves, c_pos)
    g_shard, pos = {}, 0
    for n in LATE_DIRECT + EARLY_DIRECT:
        layers = shard_shapes[n][0]
        g_shard[n] = jnp.stack([j.reshape(shard_shapes[n][1:]) for j in joined[pos:pos + layers]])
        pos += layers
    flat_shapes = [shard_shapes[n] for n in PACKED_MM + SHARDED_VEC]
    g_shard.update(zip(PACKED_MM + SHARDED_VEC, _unpack(joined[-1].reshape(-1), flat_shapes)))

    small = jnp.concatenate([grads[n].reshape(-1) for n in REPLICATED] + [loss_part.reshape(-1)[:1]])
    n_small = small.shape[0]
    n_small_pad = -(-n_small // (SUBLANE * FLAT_COLS)) * (SUBLANE * FLAT_COLS)
    small = jnp.pad(small, (0, n_small_pad - n_small)).reshape(-1, FLAT_COLS)
    small = _all_sum_small(small).reshape(-1)
    g_rep = dict(zip(REPLICATED, _unpack(small, [shard_shapes[n] for n in REPLICATED])))
    loss = small[n_small - 1]

    g_all = {**g_shard, **g_rep}
    delta, new_m, new_v = {}, {}, {}
    for n in WEIGHTS:
        delta[n], new_m[n], new_v[n] = _adamw(weights[n], g_all[n], moments_m[n], moments_v[n], "adamw_" + n)
    return loss, dx[None], g_all, delta, new_m, new_v


def kernel(x, norm_mix, norm_ffn, norm_final, a_in_proj, a_conv_w, a_conv_b, a_dt_bias, a_log, a_d_skip, a_norm, a_out_proj, b_pw1_w, b_pw1_b, b_dw_w, b_dw_b, b_ln_g, b_ln_b, b_pw2_w, b_pw2_b, c_in_w, c_in_b, c_conv_w, c_conv_b, c_ga_w, c_ga_b, c_gx_w, c_gx_b, c_lambda, c_out_w, c_out_b, d_in_w, d_in_b, d_ln_g, d_ln_b, d_sp_w, d_sp_b, d_out_w, d_out_b, f_up_w, f_conv_w, f_conv_b, f_down_w, loss_target, m_norm_mix, m_norm_ffn, m_norm_final, m_a_in_proj, m_a_conv_w, m_a_conv_b, m_a_dt_bias, m_a_log, m_a_d_skip, m_a_norm, m_a_out_proj, m_b_pw1_w, m_b_pw1_b, m_b_dw_w, m_b_dw_b, m_b_ln_g, m_b_ln_b, m_b_pw2_w, m_b_pw2_b, m_c_in_w, m_c_in_b, m_c_conv_w, m_c_conv_b, m_c_ga_w, m_c_ga_b, m_c_gx_w, m_c_gx_b, m_c_lambda, m_c_out_w, m_c_out_b, m_d_in_w, m_d_in_b, m_d_ln_g, m_d_ln_b, m_d_sp_w, m_d_sp_b, m_d_out_w, m_d_out_b, m_f_up_w, m_f_conv_w, m_f_conv_b, m_f_down_w, v_norm_mix, v_norm_ffn, v_norm_final, v_a_in_proj, v_a_conv_w, v_a_conv_b, v_a_dt_bias, v_a_log, v_a_d_skip, v_a_norm, v_a_out_proj, v_b_pw1_w, v_b_pw1_b, v_b_dw_w, v_b_dw_b, v_b_ln_g, v_b_ln_b, v_b_pw2_w, v_b_pw2_b, v_c_in_w, v_c_in_b, v_c_conv_w, v_c_conv_b, v_c_ga_w, v_c_ga_b, v_c_gx_w, v_c_gx_b, v_c_lambda, v_c_out_w, v_c_out_b, v_d_in_w, v_d_in_b, v_d_ln_g, v_d_ln_b, v_d_sp_w, v_d_sp_b, v_d_out_w, v_d_out_b, v_f_up_w, v_f_conv_w, v_f_conv_b, v_f_down_w):
    args = locals()
    weights = {n: args[n] for n in WEIGHTS}
    moments_m = {n: args['m_' + n] for n in WEIGHTS}
    moments_v = {n: args['v_' + n] for n in WEIGHTS}
    loss, dx, grad, delta, new_m, new_v = _step(x, loss_target, weights, moments_m, moments_v)
    return (loss, dx, *[grad[n] for n in WEIGHTS], *[delta[n] for n in WEIGHTS],
            *[new_m[n] for n in WEIGHTS], *[new_v[n] for n in WEIGHTS])
```

```python
import functools
import math

import jax
import jax.numpy as jnp
import numpy as np
from jax import lax
from jax.experimental import pallas as pl
from jax.experimental.pallas import tpu as pltpu

f32 = jnp.float32
bf16 = jnp.bfloat16
MESH = pl.DeviceIdType.MESH
HIGHEST = lax.Precision.HIGHEST

D_MODEL = 1024
DEPTH = 4
RMS_EPS = 1e-6
LN_EPS = 1e-5
SSD_D_INNER = 2048
SSD_HEADS = 32
SSD_BC = 1024
SSD_CONV_DIM = 4096
SSD_CHUNK = 128
SSD_GROUPS = 8
LRU_W = 1280
LRU_BLOCK = 256
LRU_C = 8.0
SGU_HALF = 2048
SGU_GROUPS = 8
SGU_CHUNK = 128
FFN_H = 2816
ADAM_LR, ADAM_B1, ADAM_B2, ADAM_EPS, ADAM_WD, ADAM_STEP = 0.001, 0.9, 0.999, 1e-08, 0.01, 10

LANE = 128
SUBLANE = 8
VMEM_LIMIT = 56 * 1024 * 1024
FLAT_COLS = 1024

WEIGHTS = ['norm_mix', 'norm_ffn', 'norm_final', 'a_in_proj', 'a_conv_w', 'a_conv_b', 'a_dt_bias', 'a_log',
           'a_d_skip', 'a_norm', 'a_out_proj', 'b_pw1_w', 'b_pw1_b', 'b_dw_w', 'b_dw_b', 'b_ln_g', 'b_ln_b',
           'b_pw2_w', 'b_pw2_b', 'c_in_w', 'c_in_b', 'c_conv_w', 'c_conv_b', 'c_ga_w', 'c_ga_b', 'c_gx_w',
           'c_gx_b', 'c_lambda', 'c_out_w', 'c_out_b', 'd_in_w', 'd_in_b', 'd_ln_g', 'd_ln_b', 'd_sp_w',
           'd_sp_b', 'd_out_w', 'd_out_b', 'f_up_w', 'f_conv_w', 'f_conv_b', 'f_down_w']
SHARD_AXIS = {
    'norm_mix': None, 'norm_ffn': None, 'norm_final': None, 'a_in_proj': 2, 'a_conv_w': 2, 'a_conv_b': None,
    'a_dt_bias': None, 'a_log': None, 'a_d_skip': None, 'a_norm': None, 'a_out_proj': 1, 'b_pw1_w': 2,
    'b_pw1_b': 1, 'b_dw_w': 2, 'b_dw_b': 1, 'b_ln_g': 1, 'b_ln_b': 1, 'b_pw2_w': 1, 'b_pw2_b': 1, 'c_in_w': 2,
    'c_in_b': 1, 'c_conv_w': 2, 'c_conv_b': 1, 'c_ga_w': 2, 'c_ga_b': 2, 'c_gx_w': 2, 'c_gx_b': 2,
    'c_lambda': 1, 'c_out_w': 1, 'c_out_b': 1, 'd_in_w': 2, 'd_in_b': 1, 'd_ln_g': 1, 'd_ln_b': 1,
    'd_sp_w': None, 'd_sp_b': None, 'd_out_w': 1, 'd_out_b': 1, 'f_up_w': 2, 'f_conv_w': 2, 'f_conv_b': None,
    'f_down_w': 1}
MATMUL_WEIGHTS = ['a_in_proj', 'a_out_proj', 'b_pw1_w', 'b_pw2_w', 'c_in_w', 'c_ga_w', 'c_gx_w', 'c_out_w',
                  'd_in_w', 'd_out_w', 'f_up_w', 'f_down_w']
DIRECT_COLS = ['b_pw1_w', 'c_in_w', 'd_in_w', 'f_up_w']
DIRECT_ROWS = ['a_out_proj', 'b_pw2_w', 'c_out_w', 'd_out_w', 'f_down_w']
DIRECT = DIRECT_COLS + DIRECT_ROWS
EARLY_DIRECT = ['a_out_proj']
LATE_DIRECT = [n for n in DIRECT if n not in EARLY_DIRECT]
LATE_SECOND = ['c_in_w', 'c_out_w', 'd_in_w', 'd_out_w']
LATE_FIRST = [n for n in LATE_DIRECT if n not in LATE_SECOND]
LAST_SWAPPED_INDEX = LATE_DIRECT.index('f_up_w')
PACKED_MM = [n for n in MATMUL_WEIGHTS if n not in DIRECT]
SHARDED = [n for n in WEIGHTS if SHARD_AXIS[n] is not None]
SHARDED_VEC = [n for n in SHARDED if n not in MATMUL_WEIGHTS]
REPLICATED = [n for n in WEIGHTS if SHARD_AXIS[n] is None]
N_CHIPS = 4
N_DEV = 8


def _tile(n, cap, mult):
    if n <= cap:
        return n
    t = (cap // mult) * mult
    while t >= mult:
        if n % t == 0:
            return t
        t -= mult
    raise ValueError(f"no tile for {n} under {cap} in steps of {mult}")


def _cparams(sem=None):
    if sem is None:
        return pltpu.CompilerParams(vmem_limit_bytes=VMEM_LIMIT)
    return pltpu.CompilerParams(dimension_semantics=sem, vmem_limit_bytes=VMEM_LIMIT)


def _dg(a, b, ca, cb):
    return lax.dot_general(a.astype(bf16), b.astype(bf16), (((ca,), (cb,)), ((), ())), preferred_element_type=f32)


@jax.custom_vjp
def _dot_nn(a, b):
    return _dg(a, b, 1, 0)


def _dot_nn_fwd(a, b):
    return _dg(a, b, 1, 0), (a, b)


def _dot_nn_bwd(res, g):
    a, b = res
    return _dg(g, b, 1, 1).astype(a.dtype), _dg(a, g, 0, 0).astype(b.dtype)


_dot_nn.defvjp(_dot_nn_fwd, _dot_nn_bwd)


@jax.custom_vjp
def _dot_nt(a, b):
    return _dg(a, b, 1, 1)


def _dot_nt_fwd(a, b):
    return _dg(a, b, 1, 1), (a, b)


def _dot_nt_bwd(res, g):
    a, b = res
    return _dg(g, b, 1, 0).astype(a.dtype), _dg(g, a, 0, 0).astype(b.dtype)


_dot_nt.defvjp(_dot_nt_fwd, _dot_nt_bwd)


@jax.custom_vjp
def _dot_tn(a, b):
    return _dg(a, b, 0, 0)


def _dot_tn_fwd(a, b):
    return _dg(a, b, 0, 0), (a, b)


def _dot_tn_bwd(res, g):
    a, b = res
    return _dg(b, g, 1, 1).astype(a.dtype), _dg(a, g, 1, 0).astype(b.dtype)


_dot_tn.defvjp(_dot_tn_fwd, _dot_tn_bwd)


def _expm1(x):
    small = jnp.abs(x) < 0.03
    xs = jnp.where(small, x, 0.0)
    series = xs * (1.0 + xs * (0.5 + xs * (1.0 / 6.0 + xs * (1.0 / 24.0 + xs * (1.0 / 120.0)))))
    return jnp.where(small, series, jnp.exp(x) - 1.0)


def _rms(x, g):
    return x * lax.rsqrt(jnp.mean(x * x, axis=-1, keepdims=True) + RMS_EPS) * g


def _layer_norm(x, g, b):
    mu = jnp.mean(x, axis=-1, keepdims=True)
    xc = x - mu
    return xc * lax.rsqrt(jnp.mean(xc * xc, axis=-1, keepdims=True) + LN_EPS) * g + b


def _causal_taps(ext, w, halo, rows):
    k_taps = w.shape[0]
    acc = None
    for k in range(k_taps):
        lo = halo - (k_taps - 1) + k
        term = w[k:k + 1, :] * ext[lo:lo + rows, :]
        acc = term if acc is None else acc + term
    return acc


def _mm(a, b, mode, name, *, bias=None, add=None, out_dtype=f32, tm_cap=1408, tn_cap=1408, tk_cap=1408,
        b_cols_sharded=False, b_layer=None, out_cols_sharded=False, rms_gain=None):
    shard_cols = None
    if b_cols_sharded:
        shard_cols = b.shape[-1]
        b_dims = (b.shape[-2], N_CHIPS * shard_cols)
    else:
        b_dims = b.shape
    if mode == 'nn':
        (m, k), (k2, n) = a.shape, b_dims
    elif mode == 'nt':
        (m, k), (n, k2) = a.shape, b_dims
    else:
        (k, m), (k2, n) = a.shape, b_dims
    assert k == k2, (name, a.shape, b.shape)
    tm = _tile(m, tm_cap, LANE if mode == 'tn' else SUBLANE)
    tn = _tile(n, tn_cap, LANE)
    tk = _tile(k, tk_cap, LANE if mode != 'tn' else SUBLANE)
    if b_cols_sharded and mode == 'nn':
        tn = shard_cols
    if b_cols_sharded and mode == 'nt':
        tk = shard_cols
    if out_cols_sharded:
        assert mode == 'tn' and n % N_CHIPS == 0
        tn = n // N_CHIPS
    nk = k // tk

    def shard_block(rows):
        lead = (None,) * (b.ndim - 2)
        return lead + (rows, shard_cols)

    def shard_index(shard, row_block):
        return (shard, row_block, 0) if b_layer is None else (shard, b_layer, row_block, 0)

    if mode == 'nn':
        a_spec = pl.BlockSpec((tm, tk), lambda i, j, kk: (i, kk))
        if b_cols_sharded:
            b_spec = pl.BlockSpec(shard_block(tk), lambda i, j, kk: shard_index(j, kk))
        else:
            b_spec = pl.BlockSpec((tk, tn), lambda i, j, kk: (kk, j))
        ca, cb = 1, 0
    elif mode == 'nt':
        a_spec = pl.BlockSpec((tm, tk), lambda i, j, kk: (i, kk))
        if b_cols_sharded:
            b_spec = pl.BlockSpec(shard_block(tn), lambda i, j, kk: shard_index(kk, j))
        else:
            b_spec = pl.BlockSpec((tn, tk), lambda i, j, kk: (j, kk))
        ca, cb = 1, 1
    else:
        a_spec = pl.BlockSpec((tk, tm), lambda i, j, kk: (kk, i))
        b_spec = pl.BlockSpec((tk, tn), lambda i, j, kk: (kk, j))
        ca, cb = 0, 0
    in_specs, operands = [a_spec, b_spec], [a, b]
    if bias is not None:
        in_specs.append(pl.BlockSpec((1, tn), lambda i, j, kk: (0, j)))
        operands.append(bias)
    if add is not None:
        in_specs.append(pl.BlockSpec((tm, tn), lambda i, j, kk: (i, j)))
        operands.append(add)
    if rms_gain is not None:
        in_specs.append(pl.BlockSpec((1, tn), lambda i, j, kk: (0, j)))
        operands.append(rms_gain)

    def body(*refs):
        a_ref, b_ref = refs[0], refs[1]
        pos = 2
        bias_ref = add_ref = None
        if bias is not None:
            bias_ref = refs[pos]
            pos += 1
        if add is not None:
            add_ref = refs[pos]
            pos += 1
        norm_ref = None
        if rms_gain is not None:
            norm_ref = refs[pos]
            pos += 1
        o_ref = refs[pos]
        normed_ref = refs[pos + 1] if rms_gain is not None else None
        acc_ref = refs[-1]
        kk = pl.program_id(2)

        @pl.when(kk == 0)
        def _():
            acc_ref[...] = jnp.zeros_like(acc_ref)

        acc_ref[...] += _dg(a_ref[...], b_ref[...], ca, cb)

        @pl.when(kk == nk - 1)
        def _():
            r = acc_ref[...]
            if bias_ref is not None:
                r = r + bias_ref[...]
            if add_ref is not None:
                r = r + add_ref[...].astype(f32)
            o_ref[...] = r.astype(out_dtype)
            if normed_ref is not None:
                normed_ref[...] = _rms(r, norm_ref[...]).astype(bf16)

    if out_cols_sharded:
        out_shape = jax.ShapeDtypeStruct((N_CHIPS, m, tn), out_dtype)
        out_spec = pl.BlockSpec((None, tm, tn), lambda i, j, kk: (j, i, 0))
    else:
        out_shape = jax.ShapeDtypeStruct((m, n), out_dtype)
        out_spec = pl.BlockSpec((tm, tn), lambda i, j, kk: (i, j))
    if rms_gain is not None:
        assert tn == n and not out_cols_sharded, "the norm needs whole rows in a tile"
        out_shape = [out_shape, jax.ShapeDtypeStruct((m, n), bf16)]
        out_spec = [out_spec, pl.BlockSpec((tm, tn), lambda i, j, kk: (i, j))]
    return pl.pallas_call(
        body, name=name, out_shape=out_shape,
        grid=(m // tm, n // tn, nk), in_specs=in_specs, out_specs=out_spec,
        scratch_shapes=[pltpu.VMEM((tm, tn), f32)],
        compiler_params=_cparams(("parallel", "parallel", "arbitrary")),
    )(*operands)


def _mm_normed(a, b, mode, name, *, rms_gain, **kw):
    if rms_gain is None:
        return _mm(a, b, mode, name, **kw), None
    return _mm(a, b, mode, name, rms_gain=rms_gain, **kw)


def _mm_w(a, w, mode, name, **kw):
    shards, layer = w
    return _mm(a, shards, mode, name, b_cols_sharded=True, b_layer=layer, **kw)


def _row_specs(tiles, halo_of, rows, halo, n_tiles, reverse):
    def tile_index(i):
        return n_tiles - 1 - i if reverse else i

    specs, operands = [], []
    for arr, has_halo in zip(tiles, halo_of):
        cols = arr.shape[1]
        specs.append(pl.BlockSpec((rows, cols), lambda i: (tile_index(i), 0)))
        operands.append(arr)
        if has_halo:
            per = rows // halo
            specs.append(pl.BlockSpec((halo, cols), lambda i: (jnp.maximum(tile_index(i) * per - 1, 0), 0)))
            operands.append(arr)
    return specs, operands, tile_index


def _load_tiles(refs, halo_of, tile_id, rows, halo):
    vals, pos = [], 0
    for has_halo in halo_of:
        cur = refs[pos][...].astype(f32)
        pos += 1
        if has_halo:
            before = refs[pos][...].astype(f32)
            pos += 1
            before = jnp.where(tile_id > 0, before, jnp.zeros_like(before))
            cur = jnp.concatenate([before, cur], axis=0)
        vals.append(cur)
    return vals, pos


def _valid_rows(tile_id, rows, halo):
    r = lax.broadcasted_iota(jnp.int32, (halo + rows, 1), 0)
    return jnp.logical_or(r >= halo, tile_id > 0).astype(f32)


def _row_fwd(f, tiles, params, outs, *, rows, name, halo=0, halo_of=None):
    t_len = tiles[0].shape[0]
    rows = min(rows, t_len)
    n_tiles = t_len // rows
    halo_of = halo_of or [False] * len(tiles)
    specs, operands, _ = _row_specs(tiles, halo_of, rows, halo, n_tiles, False)
    for p in params:
        specs.append(pl.BlockSpec(p.shape, lambda i: (0, 0)))
        operands.append(p)

    def body(*refs):
        i = pl.program_id(0)
        vals, pos = _load_tiles(refs, halo_of, i, rows, halo)
        pvals = [refs[pos + j][...] for j in range(len(params))]
        pos += len(params)
        kw = {'valid': _valid_rows(i, rows, halo)} if halo else {}
        res = f(*vals, *pvals, **kw)
        for o_ref, o in zip(refs[pos:], res):
            o_ref[...] = o.astype(o_ref.dtype)

    return pl.pallas_call(
        body, name=name,
        out_shape=[jax.ShapeDtypeStruct((t_len, c), d) for c, d in outs],
        grid=(n_tiles,), in_specs=specs,
        out_specs=[pl.BlockSpec((rows, c), lambda i: (i, 0)) for c, _ in outs],
        compiler_params=_cparams(("parallel",)),
    )(*operands)


def _row_bwd(f, tiles, params, cots, *, rows, name, halo=0, halo_of=None, tile_dtypes=None):
    t_len = tiles[0].shape[0]
    rows = min(rows, t_len)
    n_tiles = t_len // rows
    halo_of = halo_of or [False] * len(tiles)
    tile_dtypes = tile_dtypes or [f32] * len(tiles)
    specs, operands, tile_index = _row_specs(tiles, halo_of, rows, halo, n_tiles, True)
    for p in params:
        specs.append(pl.BlockSpec(p.shape, lambda i: (0, 0)))
        operands.append(p)
    for ct in cots:
        specs.append(pl.BlockSpec((rows, ct.shape[1]), lambda i: (tile_index(i), 0)))
        operands.append(ct)
    n_t, n_p, n_c = len(tiles), len(params), len(cots)
    out_shape = [jax.ShapeDtypeStruct(t.shape, d) for t, d in zip(tiles, tile_dtypes)]
    out_shape += [jax.ShapeDtypeStruct(p.shape, f32) for p in params]
    out_specs = [pl.BlockSpec((rows, t.shape[1]), lambda i: (tile_index(i), 0)) for t in tiles]
    out_specs += [pl.BlockSpec(p.shape, lambda i: (0, 0)) for p in params]
    scratch = [pltpu.VMEM((halo, t.shape[1]), f32) for t, h in zip(tiles, halo_of) if h]

    def body(*refs):
        i = pl.program_id(0)
        tile_id = tile_index(i)
        vals, pos = _load_tiles(refs, halo_of, tile_id, rows, halo)
        pvals = [refs[pos + j][...] for j in range(n_p)]
        pos += n_p
        cvals = [refs[pos + j][...].astype(f32) for j in range(n_c)]
        pos += n_c
        d_tile_refs = refs[pos:pos + n_t]
        d_param_refs = refs[pos + n_t:pos + n_t + n_p]
        carries = list(refs[pos + n_t + n_p:])
        kw = {'valid': _valid_rows(tile_id, rows, halo)} if halo else {}
        _, vjp = jax.vjp(lambda *args: tuple(f(*args, **kw)), *vals, *pvals)
        grads = vjp(tuple(cvals))

        @pl.when(i == 0)
        def _():
            for cr in carries:
                cr[...] = jnp.zeros_like(cr)
            for dp in d_param_refs:
                dp[...] = jnp.zeros_like(dp)

        ci = 0
        for t in range(n_t):
            g = grads[t]
            if halo_of[t]:
                cr = carries[ci]
                ci += 1
                d_tile_refs[t][0:rows - halo, :] = g[halo:rows, :].astype(d_tile_refs[t].dtype)
                d_tile_refs[t][rows - halo:rows, :] = (g[rows:rows + halo, :] + cr[...]).astype(d_tile_refs[t].dtype)
                cr[...] = g[0:halo, :]
            else:
                d_tile_refs[t][...] = g.astype(d_tile_refs[t].dtype)
        for j in range(n_p):
            d_param_refs[j][...] += grads[n_t + j]

    return pl.pallas_call(
        body, name=name, out_shape=out_shape, grid=(n_tiles,), in_specs=specs, out_specs=out_specs,
        scratch_shapes=scratch, compiler_params=_cparams(("arbitrary",)),
    )(*operands)


def _cols(base, c0, cs):
    return pl.ds(pl.multiple_of(base + c0, LANE), cs)


def _fold8(v):
    acc = v[0:SUBLANE, :]
    for m in range(1, v.shape[0] // SUBLANE):
        acc = acc + v[m * SUBLANE:(m + 1) * SUBLANE, :]
    return acc


class _ConvPlan:
    def __init__(self, x, w, b, *, in_bases, mid_bases, width, rows, rs, pre=None, pre_params=(), post=None):
        self.x, self.w, self.b = x, w, b
        self.in_bases, self.mid_bases, self.width = in_bases, mid_bases, width
        self.pre, self.pre_params, self.post = pre, list(pre_params), post
        self.k_taps = w.shape[0]
        tile_rows = SUBLANE * (4 // x.dtype.itemsize)
        self.halo = -(-(self.k_taps - 1) // tile_rows) * tile_rows
        self.t_len = x.shape[0]
        self.rows = min(rows, self.t_len)
        self.rs = rs
        self.n_tiles, self.n_rs, self.n_cs = self.t_len // self.rows, self.rows // rs, width // LANE
        self.n_mid = len(mid_bases)
        if pre is None:
            assert len(in_bases) == self.n_mid

    def in_specs(self, tile_index):
        cols = self.x.shape[1]
        per = self.rows // self.halo
        specs = [pl.BlockSpec((self.rows, cols), lambda i: (tile_index(i), 0)),
                 pl.BlockSpec((self.halo, cols), lambda i: (jnp.maximum(tile_index(i) * per - 1, 0), 0)),
                 pl.BlockSpec(self.w.shape, lambda i: (0, 0)), pl.BlockSpec(self.b.shape, lambda i: (0, 0))]
        operands = [self.x, self.x, self.w, self.b]
        for p, _ in self.pre_params:
            specs.append(pl.BlockSpec(p.shape, lambda i: (0, 0)))
            operands.append(p)
        return specs, operands

    def pre_strips(self, pp_refs, c0):
        return [p_ref[:, _cols(b, c0, LANE)] for (_, bases), p_ref in zip(self.pre_params, pp_refs) for b in bases]

    def fill_conv_input(self, cur_ref, before_ref, pp_refs, u_ref, tile_id):
        started = (tile_id > 0).astype(f32)

        def col_loop(c, carry):
            c0 = c * LANE
            pps = self.pre_strips(pp_refs, c0)
            xs = [before_ref[:, _cols(b, c0, LANE)].astype(f32) for b in self.in_bases]
            for j, u in enumerate(self.pre(*xs, *pps, valid=started)):
                u_ref[0:self.halo, _cols(j * self.width, c0, LANE)] = u
            for r in range(self.n_rs):
                xs = [cur_ref[r * self.rs:(r + 1) * self.rs, _cols(b, c0, LANE)].astype(f32) for b in self.in_bases]
                for j, u in enumerate(self.pre(*xs, *pps, valid=1.0)):
                    u_ref[self.halo + r * self.rs:self.halo + (r + 1) * self.rs, _cols(j * self.width, c0, LANE)] = u
            return carry

        lax.fori_loop(0, self.n_cs, col_loop, 0)

    def tap(self, cur_ref, before_ref, u_ref, tile_id, r, j, k, c0):
        lo = r * self.rs - (self.k_taps - 1) + k
        if u_ref is not None:
            return u_ref[self.halo + lo:self.halo + lo + self.rs, _cols(j * self.width, c0, LANE)]
        cols = _cols(self.in_bases[j], c0, LANE)
        if lo >= 0:
            return cur_ref[lo:lo + self.rs, cols].astype(f32)
        head = before_ref[self.halo + lo:self.halo, cols].astype(f32)
        head = jnp.where(tile_id > 0, head, 0.0)
        return jnp.concatenate([head, cur_ref[0:self.rs + lo, cols].astype(f32)], axis=0)

    def conv(self, cur_ref, before_ref, u_ref, w_ref, b_ref, tile_id, r, c0):
        hcs = []
        for j, mb in enumerate(self.mid_bases):
            cols = _cols(mb, c0, LANE)
            acc = b_ref[:, cols]
            for k in range(self.k_taps):
                acc = acc + w_ref[k:k + 1, cols] * self.tap(cur_ref, before_ref, u_ref, tile_id, r, j, k, c0)
            hcs.append(acc)
        return hcs


def _conv_fwd(plan, outs, name, rider=None):
    n_pp = len(plan.pre_params)
    n_in, n_out = 4 + n_pp, len(outs)
    r_n = rider.n if rider else 0
    specs, operands = plan.in_specs(lambda i: i)
    scratch = [pltpu.VMEM((plan.halo + plan.rows, plan.n_mid * plan.width), f32)] if plan.pre else []
    n_scr = len(scratch)

    def body(*refs):
        cur_ref, before_ref, w_ref, b_ref = refs[:4]
        pp_refs = refs[4:n_in]
        r_srcs = refs[n_in:n_in + r_n]
        o_refs = refs[n_in + r_n:n_in + r_n + n_out]
        r_outs = refs[n_in + r_n + n_out:n_in + 2 * r_n + n_out]
        scr = refs[n_in + 2 * r_n + n_out:]
        u_ref = scr[0] if plan.pre else None
        i = pl.program_id(0)
        if rider:
            @pl.when(i == 0)
            def _():
                rider.start(r_srcs, r_outs, scr[n_scr:])

        if plan.pre:
            plan.fill_conv_input(cur_ref, before_ref, pp_refs, u_ref, i)

        def col_loop(c, carry):
            c0 = c * LANE
            for r in range(plan.n_rs):
                res = plan.post(*plan.conv(cur_ref, before_ref, u_ref, w_ref, b_ref, i, r, c0))
                n = 0
                for (_, dt, bases), o_ref in zip(outs, o_refs):
                    for ob in bases:
                        o_ref[r * plan.rs:(r + 1) * plan.rs, _cols(ob, c0, LANE)] = res[n].astype(dt)
                        n += 1
            return carry

        lax.fori_loop(0, plan.n_cs, col_loop, 0)
        if rider:
            @pl.when(i == plan.n_tiles - 1)
            def _():
                rider.finish(r_srcs, r_outs, scr[n_scr:])

    return pl.pallas_call(
        body, name=name,
        out_shape=[jax.ShapeDtypeStruct((plan.t_len, c), d) for c, d, _ in outs] + (rider.out_shapes if rider else []),
        grid=(plan.n_tiles,), in_specs=specs + [HBM_SPEC] * r_n,
        out_specs=[pl.BlockSpec((plan.rows, c), lambda i: (i, 0)) for c, _, _ in outs] + [HBM_SPEC] * r_n,
        scratch_shapes=scratch + (rider.semaphores if rider else []),
        compiler_params=_cparams(("arbitrary",) if rider else ("parallel",)),
    )(*operands, *(rider.operands if rider else []))


def _conv_bwd(plan, cots, dx_dtype, name, rider=None):
    n_pp, n_c = len(plan.pre_params), len(cots)
    r_n = rider.n if rider else 0
    n_tiles, rows, rs, halo, k_taps = plan.n_tiles, plan.rows, plan.rs, plan.halo, plan.k_taps

    def tile_index(i):
        return n_tiles - 1 - i

    specs, operands = plan.in_specs(tile_index)
    for ct, _ in cots:
        specs.append(pl.BlockSpec((rows, ct.shape[1]), lambda i: (tile_index(i), 0)))
        operands.append(ct)
    mid_cols = plan.n_mid * plan.width
    out_shape = [jax.ShapeDtypeStruct(plan.x.shape, dx_dtype), jax.ShapeDtypeStruct(plan.w.shape, f32),
                 jax.ShapeDtypeStruct(plan.b.shape, f32)]
    out_shape += [jax.ShapeDtypeStruct(p.shape, f32) for p, _ in plan.pre_params]
    out_specs = [pl.BlockSpec((rows, plan.x.shape[1]), lambda i: (tile_index(i), 0)),
                 pl.BlockSpec(plan.w.shape, lambda i: (0, 0)), pl.BlockSpec(plan.b.shape, lambda i: (0, 0))]
    out_specs += [pl.BlockSpec(p.shape, lambda i: (0, 0)) for p, _ in plan.pre_params]
    w_cols = plan.w.shape[1]
    scratch = [pltpu.VMEM((rows + halo, mid_cols), f32),
               pltpu.VMEM((halo, mid_cols), f32),
               pltpu.VMEM(((k_taps + 1) * SUBLANE, w_cols), f32)]
    if plan.pre:
        scratch.append(pltpu.VMEM((halo + rows, mid_cols), f32))

    def body(*refs):
        cur_ref, before_ref, w_ref, b_ref = refs[:4]
        pp_refs = refs[4:4 + n_pp]
        c_refs = refs[4 + n_pp:4 + n_pp + n_c]
        pos = 4 + n_pp + n_c
        r_srcs = refs[pos:pos + r_n]
        pos += r_n
        dx_ref, dw_ref, db_ref = refs[pos:pos + 3]
        dpp_refs = refs[pos + 3:pos + 3 + n_pp]
        r_outs = refs[pos + 3 + n_pp:pos + 3 + n_pp + r_n]
        pos += 3 + n_pp + r_n
        g_ref, carry_ref, acc_ref = refs[pos:pos + 3]
        u_ref = refs[pos + 3] if plan.pre else None
        r_sems = refs[pos + (4 if plan.pre else 3):]
        i = pl.program_id(0)
        tile_id = tile_index(i)
        if rider:
            @pl.when(i == 0)
            def _():
                rider.start(r_srcs, r_outs, r_sems)

        @pl.when(i == 0)
        def _():
            carry_ref[...] = jnp.zeros_like(carry_ref)
            acc_ref[...] = jnp.zeros_like(acc_ref)
            for dp in dpp_refs:
                dp[...] = jnp.zeros_like(dp)

        g_ref[rows:rows + halo, :] = carry_ref[...]
        if plan.pre:
            plan.fill_conv_input(cur_ref, before_ref, pp_refs, u_ref, tile_id)

        def col_loop(c, carry):
            c0 = c * LANE
            for r in range(plan.n_rs):
                hcs = plan.conv(cur_ref, before_ref, u_ref, w_ref, b_ref, tile_id, r, c0)
                _, vjp = jax.vjp(lambda *a: tuple(plan.post(*a)), *hcs)
                cvals = [c_ref[r * rs:(r + 1) * rs, _cols(cb, c0, LANE)].astype(f32)
                         for (_, bases), c_ref in zip(cots, c_refs) for cb in bases]
                d_hcs = vjp(tuple(cvals))
                for j, mb in enumerate(plan.mid_bases):
                    g_ref[r * rs:(r + 1) * rs, _cols(j * plan.width, c0, LANE)] = d_hcs[j]
                    wc = _cols(mb, c0, LANE)
                    acc_ref[k_taps * SUBLANE:(k_taps + 1) * SUBLANE, wc] += _fold8(d_hcs[j])
                    for k in range(k_taps):
                        x_k = plan.tap(cur_ref, before_ref, u_ref, tile_id, r, j, k, c0)
                        acc_ref[k * SUBLANE:(k + 1) * SUBLANE, wc] += _fold8(d_hcs[j] * x_k)
            pps = plan.pre_strips(pp_refs, c0)
            for r in range(plan.n_rs):
                d_us = []
                for j, mb in enumerate(plan.mid_bases):
                    wc = _cols(mb, c0, LANE)
                    acc = None
                    for k in range(k_taps):
                        lo = r * rs + (k_taps - 1) - k
                        term = w_ref[k:k + 1, wc] * g_ref[lo:lo + rs, _cols(j * plan.width, c0, LANE)]
                        acc = term if acc is None else acc + term
                    d_us.append(acc)
                if plan.pre is None:
                    d_xs = d_us
                else:
                    xs = [cur_ref[r * rs:(r + 1) * rs, _cols(b, c0, LANE)].astype(f32) for b in plan.in_bases]
                    _, vjp_pre = jax.vjp(lambda *a: tuple(plan.pre(*a, valid=1.0)), *xs, *pps)
                    grads = vjp_pre(tuple(d_us))
                    d_xs = grads[:len(xs)]
                    n = len(xs)
                    for (_, bases), dp in zip(plan.pre_params, dpp_refs):
                        for pb in bases:
                            dp[:, _cols(pb, c0, LANE)] += grads[n]
                            n += 1
                for b, d_x in zip(plan.in_bases, d_xs):
                    dx_ref[r * rs:(r + 1) * rs, _cols(b, c0, LANE)] = d_x.astype(dx_dtype)
            return carry

        lax.fori_loop(0, plan.n_cs, col_loop, 0)
        carry_ref[...] = g_ref[0:halo, :]

        @pl.when(i == n_tiles - 1)
        def _():
            for k in range(k_taps):
                dw_ref[k:k + 1, :] = jnp.sum(acc_ref[k * SUBLANE:(k + 1) * SUBLANE, :], axis=0, keepdims=True)
            db_ref[...] = jnp.sum(acc_ref[k_taps * SUBLANE:(k_taps + 1) * SUBLANE, :], axis=0, keepdims=True)
            if rider:
                rider.finish(r_srcs, r_outs, r_sems)

    return pl.pallas_call(
        body, name=name, out_shape=out_shape + (rider.out_shapes if rider else []), grid=(n_tiles,),
        in_specs=specs + [HBM_SPEC] * r_n, out_specs=out_specs + [HBM_SPEC] * r_n,
        scratch_shapes=scratch + (rider.semaphores if rider else []), compiler_params=_cparams(("arbitrary",)),
    )(*operands, *(rider.operands if rider else []))


def _f_rms(h, g):
    return (_rms(h, g),)


def _f_rms_res(h, g, bz):
    hh = h + bz
    return _rms(hh, g), hh


@jax.custom_vjp
def _silu_gate(gate, val):
    return jax.nn.silu(gate) * val


def _silu_gate_fwd(gate, val):
    s = jax.nn.sigmoid(gate)
    return gate * s * val, (gate, val, s)


def _silu_gate_bwd(res, d):
    gate, val, s = res
    silu = gate * s
    return d * val * (s + silu * (1.0 - s)), d * silu


_silu_gate.defvjp(_silu_gate_fwd, _silu_gate_bwd)


def _post_ffn_gate(gate, val):
    return (_silu_gate(gate, val),)


def _post_silu(h):
    return (jax.nn.silu(h),)


def _post_identity(h):
    return (h,)


def _pre_glu(g_a, g_b, b_a, b_b, *, valid):
    return ((g_a + b_a) * jax.nn.sigmoid(g_b + b_b) * valid,)


def _f_ssd_dt(dtr, dtb):
    real = lax.broadcasted_iota(jnp.int32, (1, LANE), 1) < SSD_HEADS
    return (jnp.where(real, jax.nn.softplus(dtr + dtb), 0.0),)


def _f_ssd_post(y, z, g):
    return (_rms(y * jax.nn.silu(z), g),)


FFN_STRIP_ROWS = 64
CONF_STRIP_ROWS = 128
SSD_STRIP_ROWS = 64


def _f_ln_silu(x, g, b):
    return (jax.nn.silu(_layer_norm(x, g, b)),)


def _f_lru(io_ext, in_b, cw, cb, ga_w, ga_b, gx_w, gx_b, lam, *, valid):
    rows = io_ext.shape[0] - SUBLANE
    io = (io_ext + in_b) * valid
    gate = io[SUBLANE:, :LRU_W]
    xr = _causal_taps(io[:, LRU_W:], cw, SUBLANE, rows) + cb
    rs, iis = [], []
    for blk in range(LRU_W // LRU_BLOCK):
        sl = slice(blk * LRU_BLOCK, (blk + 1) * LRU_BLOCK)
        xb = xr[:, sl]
        rs.append(jax.nn.sigmoid(_dot_nn(xb, ga_w[sl, :]) + ga_b[:, sl]))
        iis.append(jax.nn.sigmoid(_dot_nn(xb, gx_w[sl, :]) + gx_b[:, sl]))
    r = jnp.concatenate(rs, axis=1)
    ig = jnp.concatenate(iis, axis=1)
    log_a = -LRU_C * r * jax.nn.softplus(-lam)
    a = jnp.exp(log_a)
    bterm = jnp.sqrt(-_expm1(2.0 * log_a)) * (ig * xr)
    return a, bterm, jax.nn.gelu(gate)


def _f_sgu(z, in_b, ln_g, ln_b, sp_w, sp_bt):
    rows = z.shape[0]
    zz = jax.nn.gelu(z + in_b)
    u, v = zz[:, :SGU_HALF], zz[:, SGU_HALF:]
    v = _layer_norm(v, ln_g, ln_b)
    tri = lax.broadcasted_iota(jnp.int32, (SGU_CHUNK, SGU_CHUNK), 0) >= lax.broadcasted_iota(
        jnp.int32, (SGU_CHUNK, SGU_CHUNK), 1)
    gdim = SGU_HALF // SGU_GROUPS
    row_blocks = []
    for ci in range(rows // SGU_CHUNK):
        col_blocks = []
        for g in range(SGU_GROUPS):
            w = jnp.where(tri, sp_w[g * SGU_CHUNK:(g + 1) * SGU_CHUNK, :], 0.0)
            vb = v[ci * SGU_CHUNK:(ci + 1) * SGU_CHUNK, g * gdim:(g + 1) * gdim]
            col_blocks.append(_dot_nn(w, vb) + sp_bt[:, g:g + 1])
        row_blocks.append(jnp.concatenate(col_blocks, axis=1))
    mixed = row_blocks[0] if len(row_blocks) == 1 else jnp.concatenate(row_blocks, axis=0)
    return (u * mixed,)


HEADS_PER_GROUP = 4
GROUP_COLS = 256
HEAD_DIM = 64


def _ssd_group(x, bm, cm, dt, st, a_log, dsk, g):
    q = x.shape[0]
    tri = lax.broadcasted_iota(jnp.int32, (q, q), 0) >= lax.broadcasted_iota(jnp.int32, (q, q), 1)
    d_a = dt * (-jnp.exp(a_log))
    acs = jnp.dot(tri.astype(f32), d_a, precision=HIGHEST, preferred_element_type=f32)
    acs_t = acs.T
    lane = lax.broadcasted_iota(jnp.int32, (1, LANE), 1)
    sub = lax.broadcasted_iota(jnp.int32, (LANE, 1), 0)
    col_idx = lax.broadcasted_iota(jnp.int32, (1, GROUP_COLS), 1)
    last_row = (lax.broadcasted_iota(jnp.int32, (q, 1), 0) == q - 1).astype(f32)
    cb = _dot_nt(cm, bm)
    y = jnp.zeros((q, GROUP_COLS), f32)
    e_in = jnp.zeros((q, GROUP_COLS), f32)
    d_end = jnp.zeros((q, GROUP_COLS), f32)
    d_last = jnp.zeros((1, GROUP_COLS), f32)
    d_skip = jnp.zeros((1, GROUP_COLS), f32)
    for j in range(HEADS_PER_GROUP):
        head = HEADS_PER_GROUP * g + j
        on_lane = (lane == head).astype(f32)
        on_sub = (sub == head).astype(f32)
        col = jnp.sum(acs * on_lane, axis=1, keepdims=True)
        row = jnp.sum(acs_t * on_sub, axis=0, keepdims=True)
        dtc = jnp.sum(dt * on_lane, axis=1, keepdims=True)
        last = jnp.sum(col * last_row, axis=0, keepdims=True)
        dsk_j = jnp.sum(dsk * on_lane, axis=1, keepdims=True)
        decay = jnp.where(tri, jnp.exp(jnp.where(tri, col - row, 0.0)), 0.0)
        mine = jnp.logical_and(col_idx >= j * HEAD_DIM, col_idx < (j + 1) * HEAD_DIM)
        y = y + _dot_nn(cb * decay, jnp.where(mine, x * dtc, 0.0))
        e_in = e_in + jnp.where(mine, jnp.exp(col), 0.0)
        d_end = d_end + jnp.where(mine, jnp.exp(last - col) * dtc, 0.0)
        d_last = d_last + jnp.where(mine, jnp.exp(last), 0.0)
        d_skip = d_skip + jnp.where(mine, dsk_j, 0.0)
    y = y + _dot_nn(cm, st) * e_in + x * d_skip
    st_new = st * d_last + _dot_tn(bm, x * d_end)
    return y, st_new


GROUPS_PER_STEP = 4


def _ssd_specs(rev, nc):
    def ch(c):
        return nc - 1 - c if rev else c

    gps = GROUPS_PER_STEP
    x_spec = pl.BlockSpec((SSD_CHUNK, gps * GROUP_COLS), lambda c, g: (ch(c), g))
    b_spec = pl.BlockSpec((SSD_CHUNK, gps * LANE), lambda c, g: (ch(c), SSD_D_INNER // (gps * LANE) + g))
    c_spec = pl.BlockSpec((SSD_CHUNK, gps * LANE), lambda c, g: (ch(c), (SSD_D_INNER + SSD_BC) // (gps * LANE) + g))
    dt_spec = pl.BlockSpec((SSD_CHUNK, LANE), lambda c, g: (ch(c), 0))
    row_spec = pl.BlockSpec((1, LANE), lambda c, g: (0, 0))
    st_spec = pl.BlockSpec((1, gps, LANE, GROUP_COLS), lambda c, g: (ch(c), g, 0, 0))
    wide_spec = pl.BlockSpec((SSD_CHUNK, SSD_CONV_DIM), lambda c, g: (ch(c), 0))
    return x_spec, b_spec, c_spec, dt_spec, row_spec, st_spec, wide_spec


def _ssd_fwd(xc, dt, a_log, dsk, gather=()):
    t_len = xc.shape[0]
    nc = t_len // SSD_CHUNK
    gps = GROUPS_PER_STEP
    n_gp = SSD_GROUPS // gps
    n_g = len(gather)
    x_spec, b_spec, c_spec, dt_spec, row_spec, st_spec, _ = _ssd_specs(False, nc)

    def body(*refs):
        x_ref, b_ref, c_ref, dt_ref, al_ref, dk_ref = refs[:6]
        g_srcs = refs[6:6 + n_g]
        y_ref, st_out_ref = refs[6 + n_g:8 + n_g]
        g_outs = refs[8 + n_g:8 + 2 * n_g]
        st_ref = refs[8 + 2 * n_g]
        g_sems = refs[9 + 2 * n_g:]
        c, gp = pl.program_id(0), pl.program_id(1)
        if n_g:
            @pl.when(jnp.logical_and(c == 0, gp == 0))
            def _():
                _gather_start(g_srcs, g_outs, *g_sems)

        for q in range(gps):
            g = gp * gps + q

            @pl.when(c == 0)
            def _():
                st_ref[g] = jnp.zeros((LANE, GROUP_COLS), f32)

            st = st_ref[g]
            st_out_ref[0, q] = st
            xq = slice(q * GROUP_COLS, (q + 1) * GROUP_COLS)
            bq = slice(q * LANE, (q + 1) * LANE)
            y, st_new = _ssd_group(x_ref[:, xq], b_ref[:, bq], c_ref[:, bq], dt_ref[...], st, al_ref[...],
                                   dk_ref[...], g)
            y_ref[:, xq] = y
            st_ref[g] = st_new

        if n_g:
            @pl.when(jnp.logical_and(c == nc - 1, gp == n_gp - 1))
            def _():
                _gather_finish(g_srcs, g_outs, *g_sems)

    res = pl.pallas_call(
        body, name="ssd_scan_fwd",
        out_shape=[jax.ShapeDtypeStruct((t_len, SSD_D_INNER), f32),
                   jax.ShapeDtypeStruct((nc, SSD_GROUPS, LANE, GROUP_COLS), f32)] + _gather_out_shapes(gather),
        grid=(nc, n_gp), in_specs=[x_spec, b_spec, c_spec, dt_spec, row_spec, row_spec] + [HBM_SPEC] * n_g,
        out_specs=[x_spec, st_spec] + [HBM_SPEC] * n_g,
        scratch_shapes=[pltpu.VMEM((SSD_GROUPS, LANE, GROUP_COLS), f32)] + (_gather_semaphores(n_g) if n_g else []),
        compiler_params=_cparams(("arbitrary", "arbitrary")),
    )(xc, xc, xc, dt, a_log, dsk, *gather)
    return res[0], res[1], list(res[2:])


def _ssd_bwd(xc, dt, a_log, dsk, states, dy, to_chips=()):
    t_len = xc.shape[0]
    nc = t_len // SSD_CHUNK
    gps = GROUPS_PER_STEP
    n_gp = SSD_GROUPS // gps
    n_s = len(to_chips)
    x_spec, b_spec, c_spec, dt_spec, row_spec, st_spec, wide_spec = _ssd_specs(True, nc)

    def body(*refs):
        x_ref, b_ref, c_ref, dt_ref, al_ref, dk_ref, st_in_ref, dy_ref = refs[:8]
        s_srcs = refs[8:8 + n_s]
        dxc_ref, ddt_ref, dal_ref, ddk_ref = refs[8 + n_s:12 + n_s]
        s_outs = refs[12 + n_s:12 + 2 * n_s]
        dst_ref = refs[12 + 2 * n_s]
        s_sems = refs[13 + 2 * n_s:]
        c, gp = pl.program_id(0), pl.program_id(1)

        @pl.when(jnp.logical_and(c == 0, gp == 0))
        def _():
            dal_ref[...] = jnp.zeros_like(dal_ref)
            ddk_ref[...] = jnp.zeros_like(ddk_ref)
            for cp in _to_chips_copies(s_srcs, s_outs, *s_sems) if n_s else []:
                cp.start()

        @pl.when(gp == 0)
        def _():
            ddt_ref[...] = jnp.zeros_like(ddt_ref)

        for q in range(gps):
            g = gp * gps + q

            @pl.when(c == 0)
            def _():
                dst_ref[g] = jnp.zeros((LANE, GROUP_COLS), f32)

            xq = slice(q * GROUP_COLS, (q + 1) * GROUP_COLS)
            bq = slice(q * LANE, (q + 1) * LANE)
            _, vjp = jax.vjp(lambda *args: _ssd_group(*args, g), x_ref[:, xq], b_ref[:, bq], c_ref[:, bq],
                             dt_ref[...], st_in_ref[0, q], al_ref[...], dk_ref[...])
            dx, db, dc, ddt, dst, dal, ddk = vjp((dy_ref[:, xq], dst_ref[g]))
            dxc_ref[:, pl.ds(pl.multiple_of(g * GROUP_COLS, GROUP_COLS), GROUP_COLS)] = dx
            dxc_ref[:, pl.ds(pl.multiple_of(SSD_D_INNER + g * LANE, LANE), LANE)] = db
            dxc_ref[:, pl.ds(pl.multiple_of(SSD_D_INNER + SSD_BC + g * LANE, LANE), LANE)] = dc
            ddt_ref[...] += ddt
            dst_ref[g] = dst
            dal_ref[...] += dal
            ddk_ref[...] += ddk

        if n_s:
            @pl.when(jnp.logical_and(c == nc - 1, gp == n_gp - 1))
            def _():
                for cp in _to_chips_copies(s_srcs, s_outs, *s_sems):
                    cp.wait()

    res = pl.pallas_call(
        body, name="ssd_scan_bwd",
        out_shape=[jax.ShapeDtypeStruct((t_len, SSD_CONV_DIM), f32), jax.ShapeDtypeStruct((t_len, LANE), f32),
                   jax.ShapeDtypeStruct((1, LANE), f32), jax.ShapeDtypeStruct((1, LANE), f32)]
        + _to_chips_out_shapes(to_chips),
        grid=(nc, n_gp),
        in_specs=[x_spec, b_spec, c_spec, dt_spec, row_spec, row_spec, st_spec, x_spec] + [HBM_SPEC] * n_s,
        out_specs=[wide_spec, dt_spec, row_spec, row_spec] + [HBM_SPEC] * n_s,
        scratch_shapes=[pltpu.VMEM((SSD_GROUPS, LANE, GROUP_COLS), f32)] + (_to_chips_semaphores(n_s) if n_s else []),
        compiler_params=_cparams(("arbitrary", "arbitrary")),
    )(xc, xc, xc, dt, a_log, dsk, states, dy, *to_chips)
    return res[0], res[1], res[2], res[3], list(res[4:])


LRU_ROWS = 256


def _lru_fwd(a, b, gg):
    t_len, cols = a.shape
    rows = min(LRU_ROWS, t_len)
    spec = pl.BlockSpec((rows, cols), lambda i: (i, 0))

    def body(a_ref, b_ref, g_ref, y_ref, h_ref, carry):
        i = pl.program_id(0)

        @pl.when(i == 0)
        def _():
            carry[...] = jnp.zeros_like(carry)

        av, bv = a_ref[...], b_ref[...]
        row = lax.broadcasted_iota(jnp.int32, av.shape, 0)
        s = 1
        while s < rows:
            a_prev = pltpu.roll(av, s, axis=0)
            b_prev = pltpu.roll(bv, s, axis=0)
            m = row >= s
            bv = jnp.where(m, av * b_prev + bv, bv)
            av = jnp.where(m, av * a_prev, av)
            s *= 2
        h = av * carry[0:1, :] + bv
        h_ref[...] = h
        y_ref[...] = g_ref[...] * h
        carry[0:1, :] = h[rows - 1:rows, :]

    return pl.pallas_call(
        body, name="lru_scan_fwd",
        out_shape=[jax.ShapeDtypeStruct((t_len, cols), f32), jax.ShapeDtypeStruct((t_len, cols), f32)],
        grid=(t_len // rows,), in_specs=[spec, spec, spec], out_specs=[spec, spec],
        scratch_shapes=[pltpu.VMEM((SUBLANE, cols), f32)],
        compiler_params=_cparams(("arbitrary",)),
    )(a, b, gg)


def _lru_bwd(dy, gg, a, h):
    t_len, cols = a.shape
    rows = min(LRU_ROWS, t_len)
    n_tiles = t_len // rows
    per = rows // SUBLANE
    spec = pl.BlockSpec((rows, cols), lambda i: (n_tiles - 1 - i, 0))
    prev_spec = pl.BlockSpec((SUBLANE, cols), lambda i: (jnp.maximum((n_tiles - 1 - i) * per - 1, 0), 0))

    def body(dy_ref, g_ref, a_ref, h_ref, hp_ref, da_ref, db_ref, dg_ref, carry_dh, carry_a):
        i = pl.program_id(0)
        tile_id = n_tiles - 1 - i

        @pl.when(i == 0)
        def _():
            carry_dh[...] = jnp.zeros_like(carry_dh)
            carry_a[...] = jnp.zeros_like(carry_a)

        av, hv, dyv = a_ref[...], h_ref[...], dy_ref[...]
        row = lax.broadcasted_iota(jnp.int32, av.shape, 0)
        dg_ref[...] = dyv * hv
        bv = dyv * g_ref[...]
        cv = jnp.where(row == rows - 1, carry_a[0:1, :], pltpu.roll(av, rows - 1, axis=0))
        s = 1
        while s < rows:
            c_next = pltpu.roll(cv, rows - s, axis=0)
            b_next = pltpu.roll(bv, rows - s, axis=0)
            m = row < rows - s
            bv = jnp.where(m, cv * b_next + bv, bv)
            cv = jnp.where(m, cv * c_next, cv)
            s *= 2
        dh = cv * carry_dh[0:1, :] + bv
        h_before = jnp.where(tile_id > 0, hp_ref[SUBLANE - 1:SUBLANE, :], jnp.zeros((1, cols), f32))
        h_prev = jnp.where(row == 0, h_before, pltpu.roll(hv, 1, axis=0))
        da_ref[...] = dh * h_prev
        db_ref[...] = dh
        carry_dh[0:1, :] = dh[0:1, :]
        carry_a[0:1, :] = av[0:1, :]

    return pl.pallas_call(
        body, name="lru_scan_bwd",
        out_shape=[jax.ShapeDtypeStruct((t_len, cols), f32)] * 3,
        grid=(n_tiles,), in_specs=[spec, spec, spec, spec, prev_spec], out_specs=[spec, spec, spec],
        scratch_shapes=[pltpu.VMEM((SUBLANE, cols), f32), pltpu.VMEM((SUBLANE, cols), f32)],
        compiler_params=_cparams(("arbitrary",)),
    )(dy, gg, a, h, h)


def _loss_head(h, target, g):
    t_len = h.shape[0]
    rows = min(512, t_len)

    def f(hv, gv, tv):
        err = _rms(hv, gv) - tv
        return 0.5 * jnp.sum(jnp.mean(err * err, axis=-1, keepdims=True), axis=0, keepdims=True)

    def body(h_ref, t_ref, g_ref, dh_ref, dg_ref, loss_ref):
        i = pl.program_id(0)

        @pl.when(i == 0)
        def _():
            dg_ref[...] = jnp.zeros_like(dg_ref)
            loss_ref[...] = jnp.zeros_like(loss_ref)

        tv = t_ref[...]
        part, vjp = jax.vjp(lambda hv, gv: f(hv, gv, tv), h_ref[...], g_ref[...])
        dh, dg = vjp(jnp.ones((1, 1), f32))
        dh_ref[...] = dh
        dg_ref[...] += dg
        loss_ref[...] += jnp.broadcast_to(part, loss_ref.shape)

    spec = pl.BlockSpec((rows, D_MODEL), lambda i: (i, 0))
    return pl.pallas_call(
        body, name="loss_head",
        out_shape=[jax.ShapeDtypeStruct((t_len, D_MODEL), f32), jax.ShapeDtypeStruct((1, D_MODEL), f32),
                   jax.ShapeDtypeStruct((1, LANE), f32)],
        grid=(t_len // rows,), in_specs=[spec, spec, pl.BlockSpec((1, D_MODEL), lambda i: (0, 0))],
        out_specs=[spec, pl.BlockSpec((1, D_MODEL), lambda i: (0, 0)), pl.BlockSpec((1, LANE), lambda i: (0, 0))],
        compiler_params=_cparams(("arbitrary",)),
    )(h, target, g)


def _as2d(a):
    return a.reshape((-1, a.shape[-1])) if a.ndim > 1 else a.reshape((1, -1))


def _row_block(rows, cols, bytes_cap=1 << 20):
    if rows * cols * 4 <= bytes_cap or rows % SUBLANE:
        return rows
    return _tile(rows, max(SUBLANE, (bytes_cap // (cols * 4)) // SUBLANE * SUBLANE), SUBLANE)


def _adamw(w, g, m, v, name):
    shape = w.shape
    w2, g2, m2, v2 = _as2d(w), _as2d(g), _as2d(m), _as2d(v)
    rows, cols = w2.shape
    rb = _row_block(rows, cols)

    def body(w_ref, g_ref, m_ref, v_ref, d_ref, nm_ref, nv_ref):
        gv = g_ref[...]
        nm = ADAM_B1 * m_ref[...] + (1.0 - ADAM_B1) * gv
        nv = ADAM_B2 * v_ref[...] + (1.0 - ADAM_B2) * jnp.square(gv)
        m_hat = nm / (1.0 - ADAM_B1 ** ADAM_STEP)
        v_hat = nv / (1.0 - ADAM_B2 ** ADAM_STEP)
        d_ref[...] = -ADAM_LR * (m_hat / (jnp.sqrt(v_hat) + ADAM_EPS) + ADAM_WD * w_ref[...])
        nm_ref[...] = nm
        nv_ref[...] = nv

    spec = pl.BlockSpec((rb, cols), lambda i: (i, 0))
    d, nm, nv = pl.pallas_call(
        body, name=name, out_shape=[jax.ShapeDtypeStruct((rows, cols), f32)] * 3,
        grid=(rows // rb,), in_specs=[spec] * 4, out_specs=[spec] * 3,
        compiler_params=_cparams(("parallel",)),
    )(w2, g2, m2, v2)
    return d.reshape(shape), nm.reshape(shape), nv.reshape(shape)


def _sum_with_sibling(g_halves, theirs, c_idx):
    n_sh, _, rows, cols = g_halves.shape
    rb = _tile(rows, 512, 2 * SUBLANE)

    def body(c_ref, mine_ref, theirs_ref, o_ref):
        o_ref[...] = (mine_ref[...] + theirs_ref[...]).astype(bf16)

    grid_spec = pltpu.PrefetchScalarGridSpec(
        num_scalar_prefetch=1, grid=(n_sh, rows // rb),
        in_specs=[pl.BlockSpec((None, None, rb, cols), lambda k, i, c_ref: (k, c_ref[0], i, 0)),
                  pl.BlockSpec((None, rb, cols), lambda k, i, c_ref: (k, i, 0))],
        out_specs=pl.BlockSpec((None, rb, cols), lambda k, i, c_ref: (k, i, 0)))
    return pl.pallas_call(
        body, name="grad_sum_sibling", out_shape=jax.ShapeDtypeStruct((n_sh, rows, cols), bf16),
        grid_spec=grid_spec, compiler_params=_cparams(("parallel", "parallel")),
    )(c_idx, g_halves, theirs)


def _sum_chips(partial, received, k_idx):
    _, rows, cols = partial.shape
    rb = _tile(rows, 512, 2 * SUBLANE)

    def body(k_ref, mine_ref, r_ref, o_ref):
        acc = mine_ref[...].astype(f32)
        for j in range(N_CHIPS - 1):
            acc = acc + r_ref[j].astype(f32)
        o_ref[...] = acc

    grid_spec = pltpu.PrefetchScalarGridSpec(
        num_scalar_prefetch=1, grid=(rows // rb,),
        in_specs=[pl.BlockSpec((None, rb, cols), lambda i, k_ref: (k_ref[0], i, 0)),
                  pl.BlockSpec((N_CHIPS - 1, rb, cols), lambda i, k_ref: (0, i, 0))],
        out_specs=pl.BlockSpec((rb, cols), lambda i, k_ref: (i, 0)))
    return pl.pallas_call(
        body, name="grad_sum_chips", out_shape=jax.ShapeDtypeStruct((rows, cols), f32),
        grid_spec=grid_spec, compiler_params=_cparams(("parallel",)),
    )(k_idx, partial, received)


HBM_SPEC = pl.BlockSpec(memory_space=pltpu.HBM)
CHIP_FLIPS = ((0, 1), (1, 0), (1, 1))


def _position():
    return lax.axis_index("x"), lax.axis_index("y"), lax.axis_index("c")


def _own_slot(gathered, mine, index):
    return [lax.dynamic_update_index_in_dim(g, m, index, 0) for g, m in zip(gathered, mine)]


def _gather_weights(blocks):
    n = len(blocks)

    def body(*refs):
        srcs, outs = refs[:n], refs[n:2 * n]
        send_sems, recv_sems = refs[2 * n:]
        _gather_start(srcs, outs, send_sems, recv_sems)
        _gather_finish(srcs, outs, send_sems, recv_sems)

    return pl.pallas_call(
        body, name="gather_weights", out_shape=_gather_out_shapes(blocks),
        in_specs=[HBM_SPEC] * n, out_specs=[HBM_SPEC] * n, scratch_shapes=_gather_semaphores(n),
    )(*blocks)


def _gather_out_shapes(blocks):
    return [jax.ShapeDtypeStruct((N_CHIPS,) + b.shape, b.dtype) for b in blocks]


def _gather_semaphores(n):
    n_sem = 2 * len(CHIP_FLIPS) * n
    return [pltpu.SemaphoreType.DMA((n_sem,)), pltpu.SemaphoreType.DMA((n_sem,))]


def _gather_copies(srcs, outs, send_sems, recv_sems):
    n_far = len(CHIP_FLIPS)
    x, y, c = _position()
    k = 2 * x + y
    first, passed = [], []
    for a in range(len(srcs)):
        for j, (fx, fy) in enumerate(CHIP_FLIPS):
            s = a * 2 * n_far + j
            kk = 2 * (x ^ fx) + (y ^ fy)
            first.append(pltpu.make_async_remote_copy(
                src_ref=srcs[a].at[c], dst_ref=outs[a].at[k, c], send_sem=send_sems.at[s],
                recv_sem=recv_sems.at[s], device_id=(x ^ fx, y ^ fy, c), device_id_type=MESH))
            passed.append(pltpu.make_async_remote_copy(
                src_ref=outs[a].at[kk, c], dst_ref=outs[a].at[kk, c], send_sem=send_sems.at[s + n_far],
                recv_sem=recv_sems.at[s + n_far], device_id=(x, y, 1 - c), device_id_type=MESH))
    return first, passed


def _gather_start(srcs, outs, send_sems, recv_sems):
    first, _ = _gather_copies(srcs, outs, send_sems, recv_sems)
    for cp in first:
        cp.start()


def _gather_finish(srcs, outs, send_sems, recv_sems):
    first, passed = _gather_copies(srcs, outs, send_sems, recv_sems)
    for arrived, onward in zip(first, passed):
        arrived.wait_recv()
        onward.start()
    for cp in passed:
        cp.wait_recv()
    for cp in first + passed:
        cp.wait_send()


def _swap_with_sibling(grads):
    rider = _swap_rider(grads)
    n = len(grads)

    def body(*refs):
        rider.start(refs[:n], refs[n:2 * n], refs[2 * n:])
        rider.finish(refs[:n], refs[n:2 * n], refs[2 * n:])

    return pl.pallas_call(
        body, name="grad_swap_sibling", out_shape=rider.out_shapes,
        in_specs=[HBM_SPEC] * n, out_specs=[HBM_SPEC] * n, scratch_shapes=rider.semaphores,
    )(*grads)


class _Rider:
    def __init__(self, operands, out_shapes, semaphores, start, finish):
        self.operands, self.out_shapes, self.semaphores = list(operands), list(out_shapes), list(semaphores)
        self.start, self.finish = start, finish
        self.n = len(self.operands)


def _swap_copies(srcs, outs, send_sems, recv_sems):
    x, y, c = _position()
    copies = []
    for a in range(len(srcs)):
        for kk in range(N_CHIPS):
            s = a * N_CHIPS + kk
            copies.append(pltpu.make_async_remote_copy(
                src_ref=srcs[a].at[kk, 1 - c], dst_ref=outs[a].at[kk], send_sem=send_sems.at[s],
                recv_sem=recv_sems.at[s], device_id=(x, y, 1 - c), device_id_type=MESH))
    return copies


def _swap_rider(grads):
    n_sem = N_CHIPS * len(grads)

    def start(srcs, outs, sems):
        for cp in _swap_copies(srcs, outs, *sems):
            cp.start()

    def finish(srcs, outs, sems):
        for cp in _swap_copies(srcs, outs, *sems):
            cp.wait()

    return _Rider(grads, [jax.ShapeDtypeStruct((N_CHIPS,) + g.shape[2:], g.dtype) for g in grads],
                  [pltpu.SemaphoreType.DMA((n_sem,)), pltpu.SemaphoreType.DMA((n_sem,))], start, finish)


def _gather_rider(blocks):
    def start(srcs, outs, sems):
        _gather_start(srcs, outs, *sems)

    def finish(srcs, outs, sems):
        _gather_finish(srcs, outs, *sems)

    return _Rider(blocks, _gather_out_shapes(blocks), _gather_semaphores(len(blocks)), start, finish)


def _send_to_chips(partials):
    n = len(partials)

    def body(*refs):
        srcs, outs = refs[:n], refs[n:2 * n]
        send_sems, recv_sems = refs[2 * n:]
        for cp in _to_chips_copies(srcs, outs, send_sems, recv_sems):
            cp.start()
        for cp in _to_chips_copies(srcs, outs, send_sems, recv_sems):
            cp.wait()

    return pl.pallas_call(
        body, name="grad_to_chips", out_shape=_to_chips_out_shapes(partials),
        in_specs=[HBM_SPEC] * n, out_specs=[HBM_SPEC] * n, scratch_shapes=_to_chips_semaphores(n),
    )(*partials)


def _to_chips_out_shapes(partials):
    return [jax.ShapeDtypeStruct((len(CHIP_FLIPS),) + p.shape[1:], p.dtype) for p in partials]


def _to_chips_semaphores(n):
    n_sem = len(CHIP_FLIPS) * n
    return [pltpu.SemaphoreType.DMA((n_sem,)), pltpu.SemaphoreType.DMA((n_sem,))]


def _to_chips_copies(srcs, outs, send_sems, recv_sems):
    n_far = len(CHIP_FLIPS)
    x, y, c = _position()
    copies = []
    for a in range(len(srcs)):
        for j, (fx, fy) in enumerate(CHIP_FLIPS):
            s = a * n_far + j
            kk = 2 * (x ^ fx) + (y ^ fy)
            copies.append(pltpu.make_async_remote_copy(
                src_ref=srcs[a].at[kk], dst_ref=outs[a].at[j], send_sem=send_sems.at[s],
                recv_sem=recv_sems.at[s], device_id=(x ^ fx, y ^ fy, c), device_id_type=MESH))
    return copies


def _join_halves(halves):
    n = len(halves)

    def body(*refs):
        srcs, outs = refs[:n], refs[n:2 * n]
        send_sems, recv_sems = refs[2 * n:]
        x, y, c = _position()
        copies = []
        for a in range(n):
            cp = pltpu.make_async_remote_copy(
                src_ref=srcs[a], dst_ref=outs[a].at[c], send_sem=send_sems.at[a], recv_sem=recv_sems.at[a],
                device_id=(x, y, 1 - c), device_id_type=MESH)
            cp.start()
            copies.append(cp)
        for cp in copies:
            cp.wait()

    return pl.pallas_call(
        body, name="grad_join_halves",
        out_shape=[jax.ShapeDtypeStruct((2,) + h.shape, h.dtype) for h in halves],
        in_specs=[HBM_SPEC] * n, out_specs=[HBM_SPEC] * n,
        scratch_shapes=[pltpu.SemaphoreType.DMA((n,)), pltpu.SemaphoreType.DMA((n,))],
    )(*halves)


def _all_sum_small(vec):
    rows, cols = vec.shape

    def body(v_ref, o_ref, buf, send_sems, recv_sems):
        x, y, c = _position()
        me = 4 * x + 2 * y + c
        buf[me] = v_ref[...]
        copies = []
        for m in range(1, N_DEV):
            fx, fy, fc = (m >> 2) & 1, (m >> 1) & 1, m & 1
            cp = pltpu.make_async_remote_copy(
                src_ref=v_ref, dst_ref=buf.at[me], send_sem=send_sems.at[m - 1], recv_sem=recv_sems.at[m - 1],
                device_id=(x ^ fx, y ^ fy, c ^ fc), device_id_type=MESH)
            cp.start()
            copies.append(cp)
        for cp in copies:
            cp.wait()
        acc = buf[0]
        for d in range(1, N_DEV):
            acc = acc + buf[d]
        o_ref[...] = acc

    return pl.pallas_call(
        body, name="all_sum_small", out_shape=jax.ShapeDtypeStruct((rows, cols), f32),
        in_specs=[pl.BlockSpec(memory_space=pltpu.VMEM)], out_specs=pl.BlockSpec(memory_space=pltpu.VMEM),
        scratch_shapes=[pltpu.VMEM((N_DEV, rows, cols), f32), pltpu.SemaphoreType.DMA((N_DEV - 1,)),
                        pltpu.SemaphoreType.DMA((N_DEV - 1,))],
        compiler_params=_cparams(),
    )(vec)


FLAT_QUANTUM = 2 * 2 * SUBLANE * FLAT_COLS


def _pack(arrays, dtype):
    flat = jnp.concatenate([a.astype(dtype).reshape(-1) for a in arrays])
    n = flat.shape[0]
    n_pad = -(-n // FLAT_QUANTUM) * FLAT_QUANTUM
    return jnp.pad(flat, (0, n_pad - n))


def _unpack(flat, shapes):
    out, off = [], 0
    for s in shapes:
        n = int(np.prod(s))
        out.append(flat[..., off:off + n].reshape(flat.shape[:-1] + tuple(s)))
        off += n
    return out


def _full_from_shards(stacked, axis):
    return jnp.concatenate([stacked[k] for k in range(N_CHIPS)], axis=axis)


def _shards_of(full, axis):
    return jnp.stack(jnp.split(full, N_CHIPS, axis=axis))


def _ffn_fwd(h, u, p, next_gain):
    a = _mm_w(u, p['up'], 'nn', "ffn_up")
    gated = _conv_fwd(_ffn_conv_plan(a, p), [(FFN_H, bf16, (0,))], "ffn_gate")[0]
    h_out, u_next = _mm_normed(gated, p['down'], 'nn', "ffn_down", add=h, rms_gain=next_gain)
    return h_out, u_next, (h, u, a, gated)


def _ffn_conv_plan(a, p):
    both = (0, FFN_H)
    return _ConvPlan(a, p['cw'], p['cb'], in_bases=both, mid_bases=both, width=FFN_H, rows=256, rs=FFN_STRIP_ROWS,
                     post=_post_ffn_gate)


def _ffn_bwd(dh_out, p, saved, bias_zero, make_rider=None):
    h, u, a, gated = saved
    d_gated = _mm(dh_out, p['down'], 'nt', "ffn_down_dx", out_dtype=bf16)
    d_down = _mm(gated, dh_out, 'tn', "ffn_down_dw")
    rider = make_rider(d_down) if make_rider else None
    res = _conv_bwd(_ffn_conv_plan(a, p), [(d_gated, (0,))], bf16, "ffn_gate_bwd", rider=rider)
    da, d_cw, d_cb = res[:3]
    d_up = _mm(u, da, 'tn', "ffn_up_dw", out_cols_sharded=True)
    du = _mm_w(da, p['up'], 'nt', "ffn_up_dx", out_dtype=bf16)
    dh, d_g, d_bias = _row_bwd(_f_rms_res, [h], [p['g'], bias_zero], [du, dh_out], rows=512, name="ffn_norm_bwd")
    return dh, {'g': d_g, 'up': d_up, 'down': d_down, 'cw': d_cw, 'cb': d_cb}, d_bias, list(res[3:])


def _mixer_norm_bwd(h, g, du, dh_res, name):
    def f(hv, gv):
        return _rms(hv, gv), hv

    dh, d_g = _row_bwd(f, [h], [g], [du, dh_res], rows=512, name=name)
    return dh, d_g


def _ssd_layer_fwd(h, u, p, next_gain, gather=()):
    z = _mm(u, p['w_z'], 'nn', "ssd_in_z")
    xbc = _mm(u, p['w_xbc'], 'nn', "ssd_in_xbc")
    dtr = _mm(u, p['w_dt'], 'nn', "ssd_in_dt")
    xc = _conv_fwd(_ssd_conv_plan(xbc, p), [(SSD_CONV_DIM, f32, (0,))], "ssd_conv")[0]
    dt = _row_fwd(_f_ssd_dt, [dtr], [p['dtb']], [(LANE, f32)], rows=1024, name="ssd_dt")[0]
    y, states, gathered = _ssd_fwd(xc, dt, p['a_log'], p['dsk'], gather)
    yn = _row_fwd(_f_ssd_post, [y, z], [p['norm']], [(SSD_D_INNER, bf16)], rows=256, name="ssd_gate_norm")[0]
    h_out, u_next = _mm_normed(yn, p['out'], 'nn', "ssd_out", add=h, rms_gain=next_gain)
    return h_out, u_next, (h, u, z, xbc, dtr, xc, dt, states, y, yn), gathered


def _ssd_conv_plan(xbc, p):
    return _ConvPlan(xbc, p['cw'], p['cb'], in_bases=(0,), mid_bases=(0,), width=SSD_CONV_DIM, rows=256,
                     rs=SSD_STRIP_ROWS, post=_post_silu)


def _ssd_layer_bwd(dh_out, p, saved, to_chips=()):
    h, u, z, xbc, dtr, xc, dt, states, y, yn = saved
    d_yn = _mm(dh_out, p['out'], 'nt', "ssd_out_dx", out_dtype=bf16)
    d_out = _mm(yn, dh_out, 'tn', "ssd_out_dw")
    dy, dz, d_norm = _row_bwd(_f_ssd_post, [y, z], [p['norm']], [d_yn], rows=256, name="ssd_gate_norm_bwd",
                              tile_dtypes=[f32, bf16])
    dxc, ddt, d_alog, d_dsk, received = _ssd_bwd(xc, dt, p['a_log'], p['dsk'], states, dy, to_chips)
    dxbc, d_cw, d_cb = _conv_bwd(_ssd_conv_plan(xbc, p), [(dxc, (0,))], bf16, "ssd_conv_bwd")
    ddtr, d_dtb = _row_bwd(_f_ssd_dt, [dtr], [p['dtb']], [ddt], rows=1024, name="ssd_dt_bwd", tile_dtypes=[bf16])
    d_wz = _mm(u, dz, 'tn', "ssd_in_z_dw")
    d_wxbc = _mm(u, dxbc, 'tn', "ssd_in_xbc_dw")
    d_wdt = _mm(u, ddtr, 'tn', "ssd_in_dt_dw")
    du = _mm(dz, p['w_z'], 'nt', "ssd_in_z_dx")
    du = _mm(dxbc, p['w_xbc'], 'nt', "ssd_in_xbc_dx", add=du)
    du = _mm(ddtr, p['w_dt'], 'nt', "ssd_in_dt_dx", add=du, out_dtype=bf16)
    dh, d_g = _mixer_norm_bwd(h, p['g'], du, dh_out, "ssd_norm_bwd")
    grads = {'g': d_g, 'w_z': d_wz, 'w_xbc': d_wxbc, 'w_dt': d_wdt, 'cw': d_cw, 'cb': d_cb, 'dtb': d_dtb,
             'a_log': d_alog, 'dsk': d_dsk, 'norm': d_norm, 'out': d_out}
    return dh, grads, received


def _conf_layer_fwd(h, u, p, next_gain, rider=None):
    g2 = _mm_w(u, p['pw1'], 'nn', "conf_pw1")
    res = _conv_fwd(_conf_conv_plan(g2, p), [(D_MODEL, f32, (0,))], "conf_conv", rider=rider)
    conv = res[0]
    s = _row_fwd(_f_ln_silu, [conv], [p['ln_g'], p['ln_b']], [(D_MODEL, bf16)], rows=256, name="conf_ln")[0]
    h_out, u_next = _mm_normed(s, p['pw2'], 'nn', "conf_pw2", bias=p['b2'], add=h, rms_gain=next_gain)
    return h_out, u_next, (h, u, g2, conv, s), list(res[1:])


def _conf_conv_plan(g2, p):
    halves = (0, D_MODEL)
    return _ConvPlan(g2, p['dw_w'], p['dw_b'], in_bases=halves, mid_bases=(0,), width=D_MODEL, rows=256,
                     rs=CONF_STRIP_ROWS, pre=_pre_glu, pre_params=[(p['b1'], halves)], post=_post_identity)


def _conf_layer_bwd(dh_out, p, saved):
    h, u, g2, conv, s = saved
    ds = _mm(dh_out, p['pw2'], 'nt', "conf_pw2_dx", out_dtype=bf16)
    d_pw2 = _mm(s, dh_out, 'tn', "conf_pw2_dw")
    d_conv, d_lng, d_lnb = _row_bwd(_f_ln_silu, [conv], [p['ln_g'], p['ln_b']], [ds], rows=256, name="conf_ln_bwd")
    dg2, d_dww, d_dwb, d_b1 = _conv_bwd(_conf_conv_plan(g2, p), [(d_conv, (0,))], bf16, "conf_conv_bwd")
    d_pw1 = _mm(u, dg2, 'tn', "conf_pw1_dw", out_cols_sharded=True)
    du = _mm_w(dg2, p['pw1'], 'nt', "conf_pw1_dx", out_dtype=bf16)
    dh, d_g = _mixer_norm_bwd(h, p['g'], du, dh_out, "conf_norm_bwd")
    grads = {'g': d_g, 'pw1': d_pw1, 'b1': d_b1, 'dw_w': d_dww, 'dw_b': d_dwb, 'ln_g': d_lng, 'ln_b': d_lnb,
             'pw2': d_pw2}
    return dh, grads


def _lru_params(p):
    return [p['in_b'], p['cw'], p['cb'], p['ga_w'], p['ga_b'], p['gx_w'], p['gx_b'], p['lam']]


def _lru_layer_fwd(h, u, p, next_gain):
    io = _mm_w(u, p['in_w'], 'nn', "lru_in")
    a, b, gg = _row_fwd(_f_lru, [io], _lru_params(p), [(LRU_W, f32)] * 3, rows=256, name="lru_gates",
                        halo=SUBLANE, halo_of=[True])
    y, hs = _lru_fwd(a, b, gg)
    h_out, u_next = _mm_normed(y, p['out'], 'nn', "lru_out", bias=p['out_b'], add=h, rms_gain=next_gain)
    return h_out, u_next, (h, u, io, a, gg, hs, y)


def _lru_layer_bwd(dh_out, p, saved):
    h, u, io, a, gg, hs, y = saved
    dy = _mm(dh_out, p['out'], 'nt', "lru_out_dx")
    d_out = _mm(y, dh_out, 'tn', "lru_out_dw")
    da, db, dgg = _lru_bwd(dy, gg, a, hs)
    res = _row_bwd(_f_lru, [io], _lru_params(p), [da, db, dgg], rows=256, name="lru_gates_bwd",
                   halo=SUBLANE, halo_of=[True], tile_dtypes=[bf16])
    dio, d_inb, d_cw, d_cb, d_gaw, d_gab, d_gxw, d_gxb, d_lam = res
    d_inw = _mm(u, dio, 'tn', "lru_in_dw", out_cols_sharded=True)
    du = _mm_w(dio, p['in_w'], 'nt', "lru_in_dx", out_dtype=bf16)
    dh, d_g = _mixer_norm_bwd(h, p['g'], du, dh_out, "lru_norm_bwd")
    grads = {'g': d_g, 'in_w': d_inw, 'in_b': d_inb, 'cw': d_cw, 'cb': d_cb, 'ga_w': d_gaw, 'ga_b': d_gab,
             'gx_w': d_gxw, 'gx_b': d_gxb, 'lam': d_lam, 'out': d_out}
    return dh, grads


def _sgu_params(p):
    return [p['in_b'], p['ln_g'], p['ln_b'], p['sp_w'], p['sp_bt']]


def _sgu_layer_fwd(h, u, p, next_gain):
    z = _mm_w(u, p['in_w'], 'nn', "sgu_in")
    s = _row_fwd(_f_sgu, [z], _sgu_params(p), [(SGU_HALF, bf16)], rows=SGU_CHUNK, name="sgu_mix")[0]
    h_out, u_next = _mm_normed(s, p['out'], 'nn', "sgu_out", bias=p['out_b'], add=h, rms_gain=next_gain)
    return h_out, u_next, (h, u, z, s)


def _sgu_layer_bwd(dh_out, p, saved):
    h, u, z, s = saved
    ds = _mm(dh_out, p['out'], 'nt', "sgu_out_dx", out_dtype=bf16)
    d_out = _mm(s, dh_out, 'tn', "sgu_out_dw")
    dz, d_inb, d_lng, d_lnb, d_spw, d_spbt = _row_bwd(_f_sgu, [z], _sgu_params(p), [ds], rows=SGU_CHUNK,
                                                      name="sgu_mix_bwd", tile_dtypes=[bf16])
    d_inw = _mm(u, dz, 'tn', "sgu_in_dw", out_cols_sharded=True)
    du = _mm_w(dz, p['in_w'], 'nt', "sgu_in_dx", out_dtype=bf16)
    dh, d_g = _mixer_norm_bwd(h, p['g'], du, dh_out, "sgu_norm_bwd")
    grads = {'g': d_g, 'in_w': d_inw, 'in_b': d_inb, 'ln_g': d_lng, 'ln_b': d_lnb, 'sp_w': d_spw, 'sp_bt': d_spbt,
             'out': d_out}
    return dh, grads


def _row(v):
    return v.reshape((1, -1)).astype(f32)


def _pad_lanes(v, n=LANE):
    v = _row(v)
    return jnp.pad(v, ((0, 0), (0, n - v.shape[1])))


def _local_step(x, target, w, comm=None):
    a_in = w['a_in_proj'][0]
    pa = {'g': _row(w['norm_mix'][0]), 'w_z': a_in[:, :SSD_D_INNER],
          'w_xbc': a_in[:, SSD_D_INNER:SSD_D_INNER + SSD_CONV_DIM],
          'w_dt': jnp.pad(a_in[:, SSD_D_INNER + SSD_CONV_DIM:], ((0, 0), (0, LANE - SSD_HEADS))),
          'cw': w['a_conv_w'][0].astype(f32), 'cb': _row(w['a_conv_b'][0]), 'dtb': _pad_lanes(w['a_dt_bias'][0]),
          'a_log': _pad_lanes(w['a_log'][0]), 'dsk': _pad_lanes(w['a_d_skip'][0]), 'norm': _row(w['a_norm'][0]),
          'out': w['a_out_proj']}
    mix_gain = [_row(w['norm_mix'][i]) for i in range(DEPTH)] + [None]
    ffn_gain = [_row(w['norm_ffn'][i]) for i in range(DEPTH)]
    u = _row_fwd(_f_rms, [x], [mix_gain[0]], [(D_MODEL, bf16)], rows=512, name="first_norm")[0]
    h, u, s_mix0, gathered = _ssd_layer_fwd(x, u, pa, ffn_gain[0], gather=comm.late_blocks if comm else ())
    if comm:
        w = {**w, **comm.late_weights(gathered)}
    ffn = [{'g': _row(w['norm_ffn'][i]), 'up': (w['f_up_w'], i), 'down': w['f_down_w'][i],
            'cw': w['f_conv_w'][i].astype(f32), 'cb': _row(w['f_conv_b'][i])} for i in range(DEPTH)]
    pb = {'g': _row(w['norm_mix'][1]), 'pw1': (w['b_pw1_w'], 0), 'b1': _row(w['b_pw1_b'][0]),
          'dw_w': w['b_dw_w'][0].astype(f32), 'dw_b': _row(w['b_dw_b'][0]), 'ln_g': _row(w['b_ln_g'][0]),
          'ln_b': _row(w['b_ln_b'][0]), 'pw2': w['b_pw2_w'], 'b2': _row(w['b_pw2_b'][0])}
    h, u, s_ffn0 = _ffn_fwd(h, u, ffn[0], mix_gain[1])
    h, u, s_mix1, gathered = _conf_layer_fwd(h, u, pb, ffn_gain[1], rider=comm.second_rider() if comm else None)
    if comm:
        w = {**w, **comm.second_weights(gathered)}
    pc = {'g': _row(w['norm_mix'][2]), 'in_w': (w['c_in_w'], 0), 'in_b': _row(w['c_in_b'][0]),
          'cw': w['c_conv_w'][0].astype(f32), 'cb': _row(w['c_conv_b'][0]),
          'ga_w': w['c_ga_w'][0].reshape(LRU_W, LRU_BLOCK).astype(f32), 'ga_b': _row(w['c_ga_b'][0]),
          'gx_w': w['c_gx_w'][0].reshape(LRU_W, LRU_BLOCK).astype(f32), 'gx_b': _row(w['c_gx_b'][0]),
          'lam': _row(w['c_lambda'][0]), 'out': w['c_out_w'], 'out_b': _row(w['c_out_b'][0])}
    pd = {'g': _row(w['norm_mix'][3]), 'in_w': (w['d_in_w'], 0), 'in_b': _row(w['d_in_b'][0]),
          'ln_g': _row(w['d_ln_g'][0]), 'ln_b': _row(w['d_ln_b'][0]),
          'sp_w': w['d_sp_w'][0].reshape(SGU_GROUPS * SGU_CHUNK, SGU_CHUNK).astype(f32),
          'sp_bt': w['d_sp_b'][0].astype(f32).T, 'out': w['d_out_w'], 'out_b': _row(w['d_out_b'][0])}
    mixers = [(None, None, pa), (None, _conf_layer_bwd, pb),
              (_lru_layer_fwd, _lru_layer_bwd, pc), (_sgu_layer_fwd, _sgu_layer_bwd, pd)]

    h, u, s_ffn1 = _ffn_fwd(h, u, ffn[1], mix_gain[2])
    saved = [(s_mix0, s_ffn0), (s_mix1, s_ffn1)]
    for i in range(2, DEPTH):
        fwd, _, p = mixers[i]
        h, u, s_mix = fwd(h, u, p, ffn_gain[i])
        h, u, s_ffn = _ffn_fwd(h, u, ffn[i], mix_gain[i + 1])
        saved.append((s_mix, s_ffn))
    dh, d_final, loss = _loss_head(h, target, _row(w['norm_final']))

    def rows_sharded(g):
        return g.reshape(N_CHIPS, g.shape[0] // N_CHIPS, g.shape[1])

    bias_zero = jnp.zeros((1, D_MODEL), f32)
    g_ffn, g_mix, d_out_bias = [None] * DEPTH, [None] * DEPTH, [None] * DEPTH
    for i in reversed(range(1, DEPTH)):
        _, bwd, p = mixers[i]
        dh, g_ffn[i], d_out_bias[i], _ = _ffn_bwd(dh, ffn[i], saved[i][1], bias_zero)
        dh, g_mix[i] = bwd(dh, p, saved[i][0])
    _, gb, gc, gd = g_mix

    def late_direct_grads(d_up0, d_down0):
        return {'b_pw1_w': gb['pw1'], 'b_pw2_w': rows_sharded(gb['pw2']), 'c_in_w': gc['in_w'],
                'c_out_w': rows_sharded(gc['out']), 'd_in_w': gd['in_w'], 'd_out_w': rows_sharded(gd['out']),
                'f_up_w': [d_up0] + [g['up'] for g in g_ffn[1:]],
                'f_down_w': [rows_sharded(d_down0)] + [rows_sharded(g['down']) for g in g_ffn[1:]]}

    make_rider = (lambda d_down0: comm.swap_rider(late_direct_grads(None, d_down0))) if comm else None
    dh, g_ffn[0], d_out_bias[0], swapped = _ffn_bwd(dh, ffn[0], saved[0][1], bias_zero, make_rider)
    late_direct = late_direct_grads(g_ffn[0]['up'], g_ffn[0]['down'])
    partials = comm.early_partials(late_direct, swapped) if comm else []
    dh, g_mix[0], received = _ssd_layer_bwd(dh, pa, saved[0][0], to_chips=partials)
    ga = g_mix[0]

    grads = {**late_direct,
        'norm_mix': jnp.concatenate([g['g'] for g in g_mix], axis=0),
        'norm_ffn': jnp.concatenate([g['g'] for g in g_ffn], axis=0),
        'norm_final': d_final.reshape(-1),
        'a_in_proj': jnp.concatenate([ga['w_z'], ga['w_xbc'], ga['w_dt'][:, :SSD_HEADS]], axis=1)[None],
        'a_conv_w': ga['cw'][None], 'a_conv_b': ga['cb'], 'a_dt_bias': ga['dtb'][:, :SSD_HEADS],
        'a_log': ga['a_log'][:, :SSD_HEADS], 'a_d_skip': ga['dsk'][:, :SSD_HEADS], 'a_norm': ga['norm'],
        'a_out_proj': rows_sharded(ga['out']),
        'b_pw1_b': gb['b1'], 'b_dw_w': gb['dw_w'][None], 'b_dw_b': gb['dw_b'],
        'b_ln_g': gb['ln_g'], 'b_ln_b': gb['ln_b'], 'b_pw2_b': d_out_bias[1],
        'c_in_b': gc['in_b'], 'c_conv_w': gc['cw'][None], 'c_conv_b': gc['cb'],
        'c_ga_w': gc['ga_w'].reshape(1, LRU_W // LRU_BLOCK, LRU_BLOCK, LRU_BLOCK),
        'c_ga_b': gc['ga_b'].reshape(1, LRU_W // LRU_BLOCK, LRU_BLOCK),
        'c_gx_w': gc['gx_w'].reshape(1, LRU_W // LRU_BLOCK, LRU_BLOCK, LRU_BLOCK),
        'c_gx_b': gc['gx_b'].reshape(1, LRU_W // LRU_BLOCK, LRU_BLOCK),
        'c_lambda': gc['lam'], 'c_out_b': d_out_bias[2],
        'd_in_b': gd['in_b'], 'd_ln_g': gd['ln_g'], 'd_ln_b': gd['ln_b'],
        'd_sp_w': gd['sp_w'].reshape(1, SGU_GROUPS, SGU_CHUNK, SGU_CHUNK), 'd_sp_b': gd['sp_bt'].T[None],
        'd_out_b': d_out_bias[3],
        'f_conv_w': jnp.stack([g['cw'] for g in g_ffn]),
        'f_conv_b': jnp.concatenate([g['cb'] for g in g_ffn], axis=0),
    }
    return loss, dh, grads, (partials, received)


def _global_shape(name, shard_shape):
    ax = SHARD_AXIS[name]
    if ax is None:
        return tuple(shard_shape)
    s = list(shard_shape)
    s[ax] *= N_CHIPS
    return tuple(s)


def _step(x, target, weights, moments_m, moments_v):
    x2, t2 = x[0], target[0]
    shard_shapes = {n: weights[n].shape for n in WEIGHTS}
    c_pos = lax.axis_index("c")
    k_pos = 2 * lax.axis_index("x") + lax.axis_index("y")
    c_idx = c_pos.astype(jnp.int32).reshape(1)
    k_idx = k_pos.astype(jnp.int32).reshape(1)

    def halves_of(a):
        a2 = _as2d(a)
        return a2.reshape(2, a2.shape[0] // 2, a2.shape[1])

    def view_direct(n, g):
        g = g.reshape((N_CHIPS,) + shard_shapes[n])
        if n in DIRECT_COLS:
            return g
        if n == 'f_down_w':
            return [g[:, i].reshape(-1, g.shape[-1]) for i in range(DEPTH)]
        return g.reshape(-1, g.shape[-1])

    def sibling_sums(grads):
        mine_g = [g.reshape(N_CHIPS, 2, g.shape[1] // 2, g.shape[2]) for g in grads]
        theirs = _swap_with_sibling(mine_g)
        return [_sum_with_sibling(g, t, c_idx) for g, t in zip(mine_g, theirs)]

    def flatten_direct(grads, names):
        out = []
        for n in names:
            out += grads[n] if isinstance(grads[n], list) else [grads[n]]
        return out

    first = [halves_of(weights[n].astype(bf16)) for n in EARLY_DIRECT]
    first.append(_pack([weights[n] for n in PACKED_MM], bf16).reshape(2, -1, FLAT_COLS))
    first.append(_pack([weights[n] for n in SHARDED_VEC], f32).reshape(2, -1, FLAT_COLS))
    gathered = _own_slot(_gather_weights(first), first, k_pos)
    w = {n: weights[n] for n in REPLICATED}
    for n, g in zip(EARLY_DIRECT, gathered):
        w[n] = view_direct(n, g)
    all_mm = _unpack(gathered[-2].reshape(N_CHIPS, -1), [shard_shapes[n] for n in PACKED_MM])
    all_vec = _unpack(gathered[-1].reshape(N_CHIPS, -1), [shard_shapes[n] for n in SHARDED_VEC])
    for n, st in zip(PACKED_MM + SHARDED_VEC, all_mm + all_vec):
        w[n] = _full_from_shards(st, SHARD_AXIS[n])

    def halved(g):
        return g.reshape(N_CHIPS, 2, g.shape[1] // 2, g.shape[2])

    class Comm:
        late_blocks = [halves_of(weights[n].astype(bf16)) for n in LATE_FIRST]
        second_blocks = [halves_of(weights[n].astype(bf16)) for n in LATE_SECOND]

        @staticmethod
        def late_weights(arrived):
            arrived = _own_slot(arrived, Comm.late_blocks, k_pos)
            return {n: view_direct(n, g) for n, g in zip(LATE_FIRST, arrived)}

        @staticmethod
        def second_rider():
            return _gather_rider(Comm.second_blocks)

        @staticmethod
        def second_weights(arrived):
            arrived = _own_slot(arrived, Comm.second_blocks, k_pos)
            return {n: view_direct(n, g) for n, g in zip(LATE_SECOND, arrived)}

        @staticmethod
        def swap_rider(grads):
            return _swap_rider([halved(g) for g in flatten_direct(grads, LATE_DIRECT) if g is not None])

        @staticmethod
        def early_partials(grads, swapped):
            mine_g = [halved(g) for g in flatten_direct(grads, LATE_DIRECT)]
            missing = LAST_SWAPPED_INDEX
            theirs = swapped[:missing] + list(_swap_with_sibling([mine_g[missing]])) + swapped[missing:]
            return [_sum_with_sibling(g, t, c_idx) for g, t in zip(mine_g, theirs)]

    loss_part, dx, grads, (partials, received) = _local_step(x2, t2, w, Comm)

    packed = [_shards_of(grads[n].reshape(_global_shape(n, shard_shapes[n])), SHARD_AXIS[n]).reshape(N_CHIPS, -1)
              for n in PACKED_MM + SHARDED_VEC]
    flat = jnp.concatenate(packed, axis=1)
    n_flat = flat.shape[1]
    n_pad = -(-n_flat // FLAT_QUANTUM) * FLAT_QUANTUM
    flat = jnp.pad(flat, ((0, 0), (0, n_pad - n_flat))).reshape(N_CHIPS, -1, FLAT_COLS)
    last_partials = sibling_sums(flatten_direct(grads, EARLY_DIRECT) + [flat])
    last_received = _send_to_chips(last_partials)
    partials, received = list(partials) + last_partials, list(received) + list(last_received)
    my_halves = [_sum_chips(p, r, k_idx) for p, r in zip(partials, received)]
    joined = _own_slot(_join_halves(my_halves), my_halves, c_pos)
    g_shard, pos = {}, 0
    for n in LATE_DIRECT + EARLY_DIRECT:
        layers = shard_shapes[n][0]
        g_shard[n] = jnp.stack([j.reshape(shard_shapes[n][1:]) for j in joined[pos:pos + layers]])
        pos += layers
    flat_shapes = [shard_shapes[n] for n in PACKED_MM + SHARDED_VEC]
    g_shard.update(zip(PACKED_MM + SHARDED_VEC, _unpack(joined[-1].reshape(-1), flat_shapes)))

    small = jnp.concatenate([grads[n].reshape(-1) for n in REPLICATED] + [loss_part.reshape(-1)[:1]])
    n_small = small.shape[0]
    n_small_pad = -(-n_small // (SUBLANE * FLAT_COLS)) * (SUBLANE * FLAT_COLS)
    small = jnp.pad(small, (0, n_small_pad - n_small)).reshape(-1, FLAT_COLS)
    small = _all_sum_small(small).reshape(-1)
    g_rep = dict(zip(REPLICATED, _unpack(small, [shard_shapes[n] for n in REPLICATED])))
    loss = small[n_small - 1]

    g_all = {**g_shard, **g_rep}
    delta, new_m, new_v = {}, {}, {}
    for n in WEIGHTS:
        delta[n], new_m[n], new_v[n] = _adamw(weights[n], g_all[n], moments_m[n], moments_v[n], "adamw_" + n)
    return loss, dx[None], g_all, delta, new_m, new_v


def kernel(x, norm_mix, norm_ffn, norm_final, a_in_proj, a_conv_w, a_conv_b, a_dt_bias, a_log, a_d_skip, a_norm, a_out_proj, b_pw1_w, b_pw1_b, b_dw_w, b_dw_b, b_ln_g, b_ln_b, b_pw2_w, b_pw2_b, c_in_w, c_in_b, c_conv_w, c_conv_b, c_ga_w, c_ga_b, c_gx_w, c_gx_b, c_lambda, c_out_w, c_out_b, d_in_w, d_in_b, d_ln_g, d_ln_b, d_sp_w, d_sp_b, d_out_w, d_out_b, f_up_w, f_conv_w, f_conv_b, f_down_w, loss_target, m_norm_mix, m_norm_ffn, m_norm_final, m_a_in_proj, m_a_conv_w, m_a_conv_b, m_a_dt_bias, m_a_log, m_a_d_skip, m_a_norm, m_a_out_proj, m_b_pw1_w, m_b_pw1_b, m_b_dw_w, m_b_dw_b, m_b_ln_g, m_b_ln_b, m_b_pw2_w, m_b_pw2_b, m_c_in_w, m_c_in_b, m_c_conv_w, m_c_conv_b, m_c_ga_w, m_c_ga_b, m_c_gx_w, m_c_gx_b, m_c_lambda, m_c_out_w, m_c_out_b, m_d_in_w, m_d_in_b, m_d_ln_g, m_d_ln_b, m_d_sp_w, m_d_sp_b, m_d_out_w, m_d_out_b, m_f_up_w, m_f_conv_w, m_f_conv_b, m_f_down_w, v_norm_mix, v_norm_ffn, v_norm_final, v_a_in_proj, v_a_conv_w, v_a_conv_b, v_a_dt_bias, v_a_log, v_a_d_skip, v_a_norm, v_a_out_proj, v_b_pw1_w, v_b_pw1_b, v_b_dw_w, v_b_dw_b, v_b_ln_g, v_b_ln_b, v_b_pw2_w, v_b_pw2_b, v_c_in_w, v_c_in_b, v_c_conv_w, v_c_conv_b, v_c_ga_w, v_c_ga_b, v_c_gx_w, v_c_gx_b, v_c_lambda, v_c_out_w, v_c_out_b, v_d_in_w, v_d_in_b, v_d_ln_g, v_d_ln_b, v_d_sp_w, v_d_sp_b, v_d_out_w, v_d_out_b, v_f_up_w, v_f_conv_w, v_f_conv_b, v_f_down_w):
    args = locals()
    weights = {n: args[n] for n in WEIGHTS}
    moments_m = {n: args['m_' + n] for n in WEIGHTS}
    moments_v = {n: args['v_' + n] for n in WEIGHTS}
    loss, dx, grad, delta, new_m, new_v = _step(x, loss_target, weights, moments_m, moments_v)
    return (loss, dx, *[grad[n] for n in WEIGHTS], *[delta[n] for n in WEIGHTS],
            *[new_m[n] for n in WEIGHTS], *[new_v[n] for n in WEIGHTS])
```

```python
import functools
import math

import jax
import jax.numpy as jnp
import numpy as np
from jax import lax
from jax.experimental import pallas as pl
from jax.experimental.pallas import tpu as pltpu

f32 = jnp.float32
bf16 = jnp.bfloat16
MESH = pl.DeviceIdType.MESH
HIGHEST = lax.Precision.HIGHEST

D_MODEL = 1024
DEPTH = 4
RMS_EPS = 1e-6
LN_EPS = 1e-5
SSD_D_INNER = 2048
SSD_HEADS = 32
SSD_BC = 1024
SSD_CONV_DIM = 4096
SSD_CHUNK = 128
SSD_GROUPS = 8
LRU_W = 1280
LRU_BLOCK = 256
LRU_C = 8.0
SGU_HALF = 2048
SGU_GROUPS = 8
SGU_CHUNK = 128
FFN_H = 2816
ADAM_LR, ADAM_B1, ADAM_B2, ADAM_EPS, ADAM_WD, ADAM_STEP = 0.001, 0.9, 0.999, 1e-08, 0.01, 10

LANE = 128
SUBLANE = 8
VMEM_LIMIT = 56 * 1024 * 1024
FLAT_COLS = 1024

WEIGHTS = ['norm_mix', 'norm_ffn', 'norm_final', 'a_in_proj', 'a_conv_w', 'a_conv_b', 'a_dt_bias', 'a_log',
           'a_d_skip', 'a_norm', 'a_out_proj', 'b_pw1_w', 'b_pw1_b', 'b_dw_w', 'b_dw_b', 'b_ln_g', 'b_ln_b',
           'b_pw2_w', 'b_pw2_b', 'c_in_w', 'c_in_b', 'c_conv_w', 'c_conv_b', 'c_ga_w', 'c_ga_b', 'c_gx_w',
           'c_gx_b', 'c_lambda', 'c_out_w', 'c_out_b', 'd_in_w', 'd_in_b', 'd_ln_g', 'd_ln_b', 'd_sp_w',
           'd_sp_b', 'd_out_w', 'd_out_b', 'f_up_w', 'f_conv_w', 'f_conv_b', 'f_down_w']
SHARD_AXIS = {
    'norm_mix': None, 'norm_ffn': None, 'norm_final': None, 'a_in_proj': 2, 'a_conv_w': 2, 'a_conv_b': None,
    'a_dt_bias': None, 'a_log': None, 'a_d_skip': None, 'a_norm': None, 'a_out_proj': 1, 'b_pw1_w': 2,
    'b_pw1_b': 1, 'b_dw_w': 2, 'b_dw_b': 1, 'b_ln_g': 1, 'b_ln_b': 1, 'b_pw2_w': 1, 'b_pw2_b': 1, 'c_in_w': 2,
    'c_in_b': 1, 'c_conv_w': 2, 'c_conv_b': 1, 'c_ga_w': 2, 'c_ga_b': 2, 'c_gx_w': 2, 'c_gx_b': 2,
    'c_lambda': 1, 'c_out_w': 1, 'c_out_b': 1, 'd_in_w': 2, 'd_in_b': 1, 'd_ln_g': 1, 'd_ln_b': 1,
    'd_sp_w': None, 'd_sp_b': None, 'd_out_w': 1, 'd_out_b': 1, 'f_up_w': 2, 'f_conv_w': 2, 'f_conv_b': None,
    'f_down_w': 1}
MATMUL_WEIGHTS = ['a_in_proj', 'a_out_proj', 'b_pw1_w', 'b_pw2_w', 'c_in_w', 'c_ga_w', 'c_gx_w', 'c_out_w',
                  'd_in_w', 'd_out_w', 'f_up_w', 'f_down_w']
DIRECT_COLS = ['b_pw1_w', 'c_in_w', 'd_in_w', 'f_up_w']
DIRECT_ROWS = ['a_out_proj', 'b_pw2_w', 'c_out_w', 'd_out_w', 'f_down_w']
DIRECT = DIRECT_COLS + DIRECT_ROWS
EARLY_DIRECT = ['a_out_proj']
LATE_DIRECT = [n for n in DIRECT if n not in EARLY_DIRECT]
LATE_SECOND = ['c_in_w', 'c_out_w', 'd_in_w', 'd_out_w']
LATE_FIRST = [n for n in LATE_DIRECT if n not in LATE_SECOND]
LAST_SWAPPED_INDEX = LATE_DIRECT.index('f_up_w')
PACKED_MM = [n for n in MATMUL_WEIGHTS if n not in DIRECT]
SHARDED = [n for n in WEIGHTS if SHARD_AXIS[n] is not None]
SHARDED_VEC = [n for n in SHARDED if n not in MATMUL_WEIGHTS]
REPLICATED = [n for n in WEIGHTS if SHARD_AXIS[n] is None]
N_CHIPS = 4
N_DEV = 8


def _tile(n, cap, mult):
    if n <= cap:
        return n
    t = (cap // mult) * mult
    while t >= mult:
        if n % t == 0:
            return t
        t -= mult
    raise ValueError(f"no tile for {n} under {cap} in steps of {mult}")


def _cparams(sem=None):
    if sem is None:
        return pltpu.CompilerParams(vmem_limit_bytes=VMEM_LIMIT)
    return pltpu.CompilerParams(dimension_semantics=sem, vmem_limit_bytes=VMEM_LIMIT)


def _dg(a, b, ca, cb):
    return lax.dot_general(a.astype(bf16), b.astype(bf16), (((ca,), (cb,)), ((), ())), preferred_element_type=f32)


@jax.custom_vjp
def _dot_nn(a, b):
    return _dg(a, b, 1, 0)


def _dot_nn_fwd(a, b):
    return _dg(a, b, 1, 0), (a, b)


def _dot_nn_bwd(res, g):
    a, b = res
    return _dg(g, b, 1, 1).astype(a.dtype), _dg(a, g, 0, 0).astype(b.dtype)


_dot_nn.defvjp(_dot_nn_fwd, _dot_nn_bwd)


@jax.custom_vjp
def _dot_nt(a, b):
    return _dg(a, b, 1, 1)


def _dot_nt_fwd(a, b):
    return _dg(a, b, 1, 1), (a, b)


def _dot_nt_bwd(res, g):
    a, b = res
    return _dg(g, b, 1, 0).astype(a.dtype), _dg(g, a, 0, 0).astype(b.dtype)


_dot_nt.defvjp(_dot_nt_fwd, _dot_nt_bwd)


@jax.custom_vjp
def _dot_tn(a, b):
    return _dg(a, b, 0, 0)


def _dot_tn_fwd(a, b):
    return _dg(a, b, 0, 0), (a, b)


def _dot_tn_bwd(res, g):
    a, b = res
    return _dg(b, g, 1, 1).astype(a.dtype), _dg(a, g, 1, 0).astype(b.dtype)


_dot_tn.defvjp(_dot_tn_fwd, _dot_tn_bwd)


def _expm1(x):
    small = jnp.abs(x) < 0.03
    xs = jnp.where(small, x, 0.0)
    series = xs * (1.0 + xs * (0.5 + xs * (1.0 / 6.0 + xs * (1.0 / 24.0 + xs * (1.0 / 120.0)))))
    return jnp.where(small, series, jnp.exp(x) - 1.0)


def _rms(x, g):
    return x * lax.rsqrt(jnp.mean(x * x, axis=-1, keepdims=True) + RMS_EPS) * g


def _layer_norm(x, g, b):
    mu = jnp.mean(x, axis=-1, keepdims=True)
    xc = x - mu
    return xc * lax.rsqrt(jnp.mean(xc * xc, axis=-1, keepdims=True) + LN_EPS) * g + b


def _causal_taps(ext, w, halo, rows):
    k_taps = w.shape[0]
    acc = None
    for k in range(k_taps):
        lo = halo - (k_taps - 1) + k
        term = w[k:k + 1, :] * ext[lo:lo + rows, :]
        acc = term if acc is None else acc + term
    return acc


def _mm(a, b, mode, name, *, bias=None, add=None, out_dtype=f32, tm_cap=1408, tn_cap=1408, tk_cap=1408,
        b_cols_sharded=False, b_layer=None, out_cols_sharded=False, rms_gain=None):
    shard_cols = None
    if b_cols_sharded:
        shard_cols = b.shape[-1]
        b_dims = (b.shape[-2], N_CHIPS * shard_cols)
    else:
        b_dims = b.shape
    if mode == 'nn':
        (m, k), (k2, n) = a.shape, b_dims
    elif mode == 'nt':
        (m, k), (n, k2) = a.shape, b_dims
    else:
        (k, m), (k2, n) = a.shape, b_dims
    assert k == k2, (name, a.shape, b.shape)
    tm = _tile(m, tm_cap, LANE if mode == 'tn' else SUBLANE)
    tn = _tile(n, tn_cap, LANE)
    tk = _tile(k, tk_cap, LANE if mode != 'tn' else SUBLANE)
    if b_cols_sharded and mode == 'nn':
        tn = shard_cols
    if b_cols_sharded and mode == 'nt':
        tk = shard_cols
    if out_cols_sharded:
        assert mode == 'tn' and n % N_CHIPS == 0
        tn = n // N_CHIPS
    nk = k // tk

    def shard_block(rows):
        lead = (None,) * (b.ndim - 2)
        return lead + (rows, shard_cols)

    def shard_index(shard, row_block):
        return (shard, row_block, 0) if b_layer is None else (shard, b_layer, row_block, 0)

    if mode == 'nn':
        a_spec = pl.BlockSpec((tm, tk), lambda i, j, kk: (i, kk))
        if b_cols_sharded:
            b_spec = pl.BlockSpec(shard_block(tk), lambda i, j, kk: shard_index(j, kk))
        else:
            b_spec = pl.BlockSpec((tk, tn), lambda i, j, kk: (kk, j))
        ca, cb = 1, 0
    elif mode == 'nt':
        a_spec = pl.BlockSpec((tm, tk), lambda i, j, kk: (i, kk))
        if b_cols_sharded:
            b_spec = pl.BlockSpec(shard_block(tn), lambda i, j, kk: shard_index(kk, j))
        else:
            b_spec = pl.BlockSpec((tn, tk), lambda i, j, kk: (j, kk))
        ca, cb = 1, 1
    else:
        a_spec = pl.BlockSpec((tk, tm), lambda i, j, kk: (kk, i))
        b_spec = pl.BlockSpec((tk, tn), lambda i, j, kk: (kk, j))
        ca, cb = 0, 0
    in_specs, operands = [a_spec, b_spec], [a, b]
    if bias is not None:
        in_specs.append(pl.BlockSpec((1, tn), lambda i, j, kk: (0, j)))
        operands.append(bias)
    if add is not None:
        in_specs.append(pl.BlockSpec((tm, tn), lambda i, j, kk: (i, j)))
        operands.append(add)
    if rms_gain is not None:
        in_specs.append(pl.BlockSpec((1, tn), lambda i, j, kk: (0, j)))
        operands.append(rms_gain)

    def body(*refs):
        a_ref, b_ref = refs[0], refs[1]
        pos = 2
        bias_ref = add_ref = None
        if bias is not None:
            bias_ref = refs[pos]
            pos += 1
        if add is not None:
            add_ref = refs[pos]
            pos += 1
        norm_ref = None
        if rms_gain is not None:
            norm_ref = refs[pos]
            pos += 1
        o_ref = refs[pos]
        normed_ref = refs[pos + 1] if rms_gain is not None else None
        acc_ref = refs[-1]
        kk = pl.program_id(2)

        @pl.when(kk == 0)
        def _():
            acc_ref[...] = jnp.zeros_like(acc_ref)

        acc_ref[...] += _dg(a_ref[...], b_ref[...], ca, cb)

        @pl.when(kk == nk - 1)
        def _():
            r = acc_ref[...]
            if bias_ref is not None:
                r = r + bias_ref[...]
            if add_ref is not None:
                r = r + add_ref[...].astype(f32)
            o_ref[...] = r.astype(out_dtype)
            if normed_ref is not None:
                normed_ref[...] = _rms(r, norm_ref[...]).astype(bf16)

    if out_cols_sharded:
        out_shape = jax.ShapeDtypeStruct((N_CHIPS, m, tn), out_dtype)
        out_spec = pl.BlockSpec((None, tm, tn), lambda i, j, kk: (j, i, 0))
    else:
        out_shape = jax.ShapeDtypeStruct((m, n), out_dtype)
        out_spec = pl.BlockSpec((tm, tn), lambda i, j, kk: (i, j))
    if rms_gain is not None:
        assert tn == n and not out_cols_sharded, "the norm needs whole rows in a tile"
        out_shape = [out_shape, jax.ShapeDtypeStruct((m, n), bf16)]
        out_spec = [out_spec, pl.BlockSpec((tm, tn), lambda i, j, kk: (i, j))]
    return pl.pallas_call(
        body, name=name, out_shape=out_shape,
        grid=(m // tm, n // tn, nk), in_specs=in_specs, out_specs=out_spec,
        scratch_shapes=[pltpu.VMEM((tm, tn), f32)],
        compiler_params=_cparams(("parallel", "parallel", "arbitrary")),
    )(*operands)


def _mm_normed(a, b, mode, name, *, rms_gain, **kw):
    if rms_gain is None:
        return _mm(a, b, mode, name, **kw), None
    return _mm(a, b, mode, name, rms_gain=rms_gain, **kw)


def _mm_w(a, w, mode, name, **kw):
    shards, layer = w
    return _mm(a, shards, mode, name, b_cols_sharded=True, b_layer=layer, **kw)


def _row_specs(tiles, halo_of, rows, halo, n_tiles, reverse):
    def tile_index(i):
        return n_tiles - 1 - i if reverse else i

    specs, operands = [], []
    for arr, has_halo in zip(tiles, halo_of):
        cols = arr.shape[1]
        specs.append(pl.BlockSpec((rows, cols), lambda i: (tile_index(i), 0)))
        operands.append(arr)
        if has_halo:
            per = rows // halo
            specs.append(pl.BlockSpec((halo, cols), lambda i: (jnp.maximum(tile_index(i) * per - 1, 0), 0)))
            operands.append(arr)
    return specs, operands, tile_index


def _load_tiles(refs, halo_of, tile_id, rows, halo):
    vals, pos = [], 0
    for has_halo in halo_of:
        cur = refs[pos][...].astype(f32)
        pos += 1
        if has_halo:
            before = refs[pos][...].astype(f32)
            pos += 1
            before = jnp.where(tile_id > 0, before, jnp.zeros_like(before))
            cur = jnp.concatenate([before, cur], axis=0)
        vals.append(cur)
    return vals, pos


def _valid_rows(tile_id, rows, halo):
    r = lax.broadcasted_iota(jnp.int32, (halo + rows, 1), 0)
    return jnp.logical_or(r >= halo, tile_id > 0).astype(f32)


def _row_fwd(f, tiles, params, outs, *, rows, name, halo=0, halo_of=None):
    t_len = tiles[0].shape[0]
    rows = min(rows, t_len)
    n_tiles = t_len // rows
    halo_of = halo_of or [False] * len(tiles)
    specs, operands, _ = _row_specs(tiles, halo_of, rows, halo, n_tiles, False)
    for p in params:
        specs.append(pl.BlockSpec(p.shape, lambda i: (0, 0)))
        operands.append(p)

    def body(*refs):
        i = pl.program_id(0)
        vals, pos = _load_tiles(refs, halo_of, i, rows, halo)
        pvals = [refs[pos + j][...] for j in range(len(params))]
        pos += len(params)
        kw = {'valid': _valid_rows(i, rows, halo)} if halo else {}
        res = f(*vals, *pvals, **kw)
        for o_ref, o in zip(refs[pos:], res):
            o_ref[...] = o.astype(o_ref.dtype)

    return pl.pallas_call(
        body, name=name,
        out_shape=[jax.ShapeDtypeStruct((t_len, c), d) for c, d in outs],
        grid=(n_tiles,), in_specs=specs,
        out_specs=[pl.BlockSpec((rows, c), lambda i: (i, 0)) for c, _ in outs],
        compiler_params=_cparams(("parallel",)),
    )(*operands)


def _row_bwd(f, tiles, params, cots, *, rows, name, halo=0, halo_of=None, tile_dtypes=None):
    t_len = tiles[0].shape[0]
    rows = min(rows, t_len)
    n_tiles = t_len // rows
    halo_of = halo_of or [False] * len(tiles)
    tile_dtypes = tile_dtypes or [f32] * len(tiles)
    specs, operands, tile_index = _row_specs(tiles, halo_of, rows, halo, n_tiles, True)
    for p in params:
        specs.append(pl.BlockSpec(p.shape, lambda i: (0, 0)))
        operands.append(p)
    for ct in cots:
        specs.append(pl.BlockSpec((rows, ct.shape[1]), lambda i: (tile_index(i), 0)))
        operands.append(ct)
    n_t, n_p, n_c = len(tiles), len(params), len(cots)
    out_shape = [jax.ShapeDtypeStruct(t.shape, d) for t, d in zip(tiles, tile_dtypes)]
    out_shape += [jax.ShapeDtypeStruct(p.shape, f32) for p in params]
    out_specs = [pl.BlockSpec((rows, t.shape[1]), lambda i: (tile_index(i), 0)) for t in tiles]
    out_specs += [pl.BlockSpec(p.shape, lambda i: (0, 0)) for p in params]
    scratch = [pltpu.VMEM((halo, t.shape[1]), f32) for t, h in zip(tiles, halo_of) if h]

    def body(*refs):
        i = pl.program_id(0)
        tile_id = tile_index(i)
        vals, pos = _load_tiles(refs, halo_of, tile_id, rows, halo)
        pvals = [refs[pos + j][...] for j in range(n_p)]
        pos += n_p
        cvals = [refs[pos + j][...].astype(f32) for j in range(n_c)]
        pos += n_c
        d_tile_refs = refs[pos:pos + n_t]
        d_param_refs = refs[pos + n_t:pos + n_t + n_p]
        carries = list(refs[pos + n_t + n_p:])
        kw = {'valid': _valid_rows(tile_id, rows, halo)} if halo else {}
        _, vjp = jax.vjp(lambda *args: tuple(f(*args, **kw)), *vals, *pvals)
        grads = vjp(tuple(cvals))

        @pl.when(i == 0)
        def _():
            for cr in carries:
                cr[...] = jnp.zeros_like(cr)
            for dp in d_param_refs:
                dp[...] = jnp.zeros_like(dp)

        ci = 0
        for t in range(n_t):
            g = grads[t]
            if halo_of[t]:
                cr = carries[ci]
                ci += 1
                d_tile_refs[t][0:rows - halo, :] = g[halo:rows, :].astype(d_tile_refs[t].dtype)
                d_tile_refs[t][rows - halo:rows, :] = (g[rows:rows + halo, :] + cr[...]).astype(d_tile_refs[t].dtype)
                cr[...] = g[0:halo, :]
            else:
                d_tile_refs[t][...] = g.astype(d_tile_refs[t].dtype)
        for j in range(n_p):
            d_param_refs[j][...] += grads[n_t + j]

    return pl.pallas_call(
        body, name=name, out_shape=out_shape, grid=(n_tiles,), in_specs=specs, out_specs=out_specs,
        scratch_shapes=scratch, compiler_params=_cparams(("arbitrary",)),
    )(*operands)


def _cols(base, c0, cs):
    return pl.ds(pl.multiple_of(base + c0, LANE), cs)


def _fold8(v):
    acc = v[0:SUBLANE, :]
    for m in range(1, v.shape[0] // SUBLANE):
        acc = acc + v[m * SUBLANE:(m + 1) * SUBLANE, :]
    return acc


class _ConvPlan:
    def __init__(self, x, w, b, *, in_bases, mid_bases, width, rows, rs, pre=None, pre_params=(), post=None):
        self.x, self.w, self.b = x, w, b
        self.in_bases, self.mid_bases, self.width = in_bases, mid_bases, width
        self.pre, self.pre_params, self.post = pre, list(pre_params), post
        self.k_taps = w.shape[0]
        tile_rows = SUBLANE * (4 // x.dtype.itemsize)
        self.halo = -(-(self.k_taps - 1) // tile_rows) * tile_rows
        self.t_len = x.shape[0]
        self.rows = min(rows, self.t_len)
        self.rs = rs
        self.n_tiles, self.n_rs, self.n_cs = self.t_len // self.rows, self.rows // rs, width // LANE
        self.n_mid = len(mid_bases)
        if pre is None:
            assert len(in_bases) == self.n_mid

    def in_specs(self, tile_index):
        cols = self.x.shape[1]
        per = self.rows // self.halo
        specs = [pl.BlockSpec((self.rows, cols), lambda i: (tile_index(i), 0)),
                 pl.BlockSpec((self.halo, cols), lambda i: (jnp.maximum(tile_index(i) * per - 1, 0), 0)),
                 pl.BlockSpec(self.w.shape, lambda i: (0, 0)), pl.BlockSpec(self.b.shape, lambda i: (0, 0))]
        operands = [self.x, self.x, self.w, self.b]
        for p, _ in self.pre_params:
            specs.append(pl.BlockSpec(p.shape, lambda i: (0, 0)))
            operands.append(p)
        return specs, operands

    def pre_strips(self, pp_refs, c0):
        return [p_ref[:, _cols(b, c0, LANE)] for (_, bases), p_ref in zip(self.pre_params, pp_refs) for b in bases]

    def fill_conv_input(self, cur_ref, before_ref, pp_refs, u_ref, tile_id):
        started = (tile_id > 0).astype(f32)

        def col_loop(c, carry):
            c0 = c * LANE
            pps = self.pre_strips(pp_refs, c0)
            xs = [before_ref[:, _cols(b, c0, LANE)].astype(f32) for b in self.in_bases]
            for j, u in enumerate(self.pre(*xs, *pps, valid=started)):
                u_ref[0:self.halo, _cols(j * self.width, c0, LANE)] = u
            for r in range(self.n_rs):
                xs = [cur_ref[r * self.rs:(r + 1) * self.rs, _cols(b, c0, LANE)].astype(f32) for b in self.in_bases]
                for j, u in enumerate(self.pre(*xs, *pps, valid=1.0)):
                    u_ref[self.halo + r * self.rs:self.halo + (r + 1) * self.rs, _cols(j * self.width, c0, LANE)] = u
            return carry

        lax.fori_loop(0, self.n_cs, col_loop, 0)

    def tap(self, cur_ref, before_ref, u_ref, tile_id, r, j, k, c0):
        lo = r * self.rs - (self.k_taps - 1) + k
        if u_ref is not None:
            return u_ref[self.halo + lo:self.halo + lo + self.rs, _cols(j * self.width, c0, LANE)]
        cols = _cols(self.in_bases[j], c0, LANE)
        if lo >= 0:
            return cur_ref[lo:lo + self.rs, cols].astype(f32)
        head = before_ref[self.halo + lo:self.halo, cols].astype(f32)
        head = jnp.where(tile_id > 0, head, 0.0)
        return jnp.concatenate([head, cur_ref[0:self.rs + lo, cols].astype(f32)], axis=0)

    def conv(self, cur_ref, before_ref, u_ref, w_ref, b_ref, tile_id, r, c0):
        hcs = []
        for j, mb in enumerate(self.mid_bases):
            cols = _cols(mb, c0, LANE)
            acc = b_ref[:, cols]
            for k in range(self.k_taps):
                acc = acc + w_ref[k:k + 1, cols] * self.tap(cur_ref, before_ref, u_ref, tile_id, r, j, k, c0)
            hcs.append(acc)
        return hcs


def _conv_fwd(plan, outs, name, rider=None):
    n_pp = len(plan.pre_params)
    n_in, n_out = 4 + n_pp, len(outs)
    r_n = rider.n if rider else 0
    specs, operands = plan.in_specs(lambda i: i)
    scratch = [pltpu.VMEM((plan.halo + plan.rows, plan.n_mid * plan.width), f32)] if plan.pre else []
    n_scr = len(scratch)

    def body(*refs):
        cur_ref, before_ref, w_ref, b_ref = refs[:4]
        pp_refs = refs[4:n_in]
        r_srcs = refs[n_in:n_in + r_n]
        o_refs = refs[n_in + r_n:n_in + r_n + n_out]
        r_outs = refs[n_in + r_n + n_out:n_in + 2 * r_n + n_out]
        scr = refs[n_in + 2 * r_n + n_out:]
        u_ref = scr[0] if plan.pre else None
        i = pl.program_id(0)
        if rider:
            @pl.when(i == 0)
            def _():
                rider.start(r_srcs, r_outs, scr[n_scr:])

        if plan.pre:
            plan.fill_conv_input(cur_ref, before_ref, pp_refs, u_ref, i)

        def col_loop(c, carry):
            c0 = c * LANE
            for r in range(plan.n_rs):
                res = plan.post(*plan.conv(cur_ref, before_ref, u_ref, w_ref, b_ref, i, r, c0))
                n = 0
                for (_, dt, bases), o_ref in zip(outs, o_refs):
                    for ob in bases:
                        o_ref[r * plan.rs:(r + 1) * plan.rs, _cols(ob, c0, LANE)] = res[n].astype(dt)
                        n += 1
            return carry

        lax.fori_loop(0, plan.n_cs, col_loop, 0)
        if rider:
            @pl.when(i == plan.n_tiles - 1)
            def _():
                rider.finish(r_srcs, r_outs, scr[n_scr:])

    return pl.pallas_call(
        body, name=name,
        out_shape=[jax.ShapeDtypeStruct((plan.t_len, c), d) for c, d, _ in outs] + (rider.out_shapes if rider else []),
        grid=(plan.n_tiles,), in_specs=specs + [HBM_SPEC] * r_n,
        out_specs=[pl.BlockSpec((plan.rows, c), lambda i: (i, 0)) for c, _, _ in outs] + [HBM_SPEC] * r_n,
        scratch_shapes=scratch + (rider.semaphores if rider else []),
        compiler_params=_cparams(("arbitrary",) if rider else ("parallel",)),
    )(*operands, *(rider.operands if rider else []))


def _conv_bwd(plan, cots, dx_dtype, name, rider=None):
    n_pp, n_c = len(plan.pre_params), len(cots)
    r_n = rider.n if rider else 0
    n_tiles, rows, rs, halo, k_taps = plan.n_tiles, plan.rows, plan.rs, plan.halo, plan.k_taps

    def tile_index(i):
        return n_tiles - 1 - i

    specs, operands = plan.in_specs(tile_index)
    for ct, _ in cots:
        specs.append(pl.BlockSpec((rows, ct.shape[1]), lambda i: (tile_index(i), 0)))
        operands.append(ct)
    mid_cols = plan.n_mid * plan.width
    out_shape = [jax.ShapeDtypeStruct(plan.x.shape, dx_dtype), jax.ShapeDtypeStruct(plan.w.shape, f32),
                 jax.ShapeDtypeStruct(plan.b.shape, f32)]
    out_shape += [jax.ShapeDtypeStruct(p.shape, f32) for p, _ in plan.pre_params]
    out_specs = [pl.BlockSpec((rows, plan.x.shape[1]), lambda i: (tile_index(i), 0)),
                 pl.BlockSpec(plan.w.shape, lambda i: (0, 0)), pl.BlockSpec(plan.b.shape, lambda i: (0, 0))]
    out_specs += [pl.BlockSpec(p.shape, lambda i: (0, 0)) for p, _ in plan.pre_params]
    w_cols = plan.w.shape[1]
    scratch = [pltpu.VMEM((rows + halo, mid_cols), f32),
               pltpu.VMEM((halo, mid_cols), f32),
               pltpu.VMEM(((k_taps + 1) * SUBLANE, w_cols), f32)]
    if plan.pre:
        scratch.append(pltpu.VMEM((halo + rows, mid_cols), f32))

    def body(*refs):
        cur_ref, before_ref, w_ref, b_ref = refs[:4]
        pp_refs = refs[4:4 + n_pp]
        c_refs = refs[4 + n_pp:4 + n_pp + n_c]
        pos = 4 + n_pp + n_c
        r_srcs = refs[pos:pos + r_n]
        pos += r_n
        dx_ref, dw_ref, db_ref = refs[pos:pos + 3]
        dpp_refs = refs[pos + 3:pos + 3 + n_pp]
        r_outs = refs[pos + 3 + n_pp:pos + 3 + n_pp + r_n]
        pos += 3 + n_pp + r_n
        g_ref, carry_ref, acc_ref = refs[pos:pos + 3]
        u_ref = refs[pos + 3] if plan.pre else None
        r_sems = refs[pos + (4 if plan.pre else 3):]
        i = pl.program_id(0)
        tile_id = tile_index(i)
        if rider:
            @pl.when(i == 0)
            def _():
                rider.start(r_srcs, r_outs, r_sems)

        @pl.when(i == 0)
        def _():
            carry_ref[...] = jnp.zeros_like(carry_ref)
            acc_ref[...] = jnp.zeros_like(acc_ref)
            for dp in dpp_refs:
                dp[...] = jnp.zeros_like(dp)

        g_ref[rows:rows + halo, :] = carry_ref[...]
        if plan.pre:
            plan.fill_conv_input(cur_ref, before_ref, pp_refs, u_ref, tile_id)

        def col_loop(c, carry):
            c0 = c * LANE
            for r in range(plan.n_rs):
                hcs = plan.conv(cur_ref, before_ref, u_ref, w_ref, b_ref, tile_id, r, c0)
                _, vjp = jax.vjp(lambda *a: tuple(plan.post(*a)), *hcs)
                cvals = [c_ref[r * rs:(r + 1) * rs, _cols(cb, c0, LANE)].astype(f32)
                         for (_, bases), c_ref in zip(cots, c_refs) for cb in bases]
                d_hcs = vjp(tuple(cvals))
                for j, mb in enumerate(plan.mid_bases):
                    g_ref[r * rs:(r + 1) * rs, _cols(j * plan.width, c0, LANE)] = d_hcs[j]
                    wc = _cols(mb, c0, LANE)
                    acc_ref[k_taps * SUBLANE:(k_taps + 1) * SUBLANE, wc] += _fold8(d_hcs[j])
                    for k in range(k_taps):
                        x_k = plan.tap(cur_ref, before_ref, u_ref, tile_id, r, j, k, c0)
                        acc_ref[k * SUBLANE:(k + 1) * SUBLANE, wc] += _fold8(d_hcs[j] * x_k)
            pps = plan.pre_strips(pp_refs, c0)
            for r in range(plan.n_rs):
                d_us = []
                for j, mb in enumerate(plan.mid_bases):
                    wc = _cols(mb, c0, LANE)
                    acc = None
                    for k in range(k_taps):
                        lo = r * rs + (k_taps - 1) - k
                        term = w_ref[k:k + 1, wc] * g_ref[lo:lo + rs, _cols(j * plan.width, c0, LANE)]
                        acc = term if acc is None else acc + term
                    d_us.append(acc)
                if plan.pre is None:
                    d_xs = d_us
                else:
                    xs = [cur_ref[r * rs:(r + 1) * rs, _cols(b, c0, LANE)].astype(f32) for b in plan.in_bases]
                    _, vjp_pre = jax.vjp(lambda *a: tuple(plan.pre(*a, valid=1.0)), *xs, *pps)
                    grads = vjp_pre(tuple(d_us))
                    d_xs = grads[:len(xs)]
                    n = len(xs)
                    for (_, bases), dp in zip(plan.pre_params, dpp_refs):
                        for pb in bases:
                            dp[:, _cols(pb, c0, LANE)] += grads[n]
                            n += 1
                for b, d_x in zip(plan.in_bases, d_xs):
                    dx_ref[r * rs:(r + 1) * rs, _cols(b, c0, LANE)] = d_x.astype(dx_dtype)
            return carry

        lax.fori_loop(0, plan.n_cs, col_loop, 0)
        carry_ref[...] = g_ref[0:halo, :]

        @pl.when(i == n_tiles - 1)
        def _():
            for k in range(k_taps):
                dw_ref[k:k + 1, :] = jnp.sum(acc_ref[k * SUBLANE:(k + 1) * SUBLANE, :], axis=0, keepdims=True)
            db_ref[...] = jnp.sum(acc_ref[k_taps * SUBLANE:(k_taps + 1) * SUBLANE, :], axis=0, keepdims=True)
            if rider:
                rider.finish(r_srcs, r_outs, r_sems)

    return pl.pallas_call(
        body, name=name, out_shape=out_shape + (rider.out_shapes if rider else []), grid=(n_tiles,),
        in_specs=specs + [HBM_SPEC] * r_n, out_specs=out_specs + [HBM_SPEC] * r_n,
        scratch_shapes=scratch + (rider.semaphores if rider else []), compiler_params=_cparams(("arbitrary",)),
    )(*operands, *(rider.operands if rider else []))


def _f_rms(h, g):
    return (_rms(h, g),)


def _f_rms_res(h, g, bz):
    hh = h + bz
    return _rms(hh, g), hh


@jax.custom_vjp
def _silu_gate(gate, val):
    return jax.nn.silu(gate) * val


def _silu_gate_fwd(gate, val):
    s = jax.nn.sigmoid(gate)
    return gate * s * val, (gate, val, s)


def _silu_gate_bwd(res, d):
    gate, val, s = res
    silu = gate * s
    return d * val * (s + silu * (1.0 - s)), d * silu


_silu_gate.defvjp(_silu_gate_fwd, _silu_gate_bwd)


def _post_ffn_gate(gate, val):
    return (_silu_gate(gate, val),)


def _post_silu(h):
    return (jax.nn.silu(h),)


def _post_identity(h):
    return (h,)


def _pre_glu(g_a, g_b, b_a, b_b, *, valid):
    return ((g_a + b_a) * jax.nn.sigmoid(g_b + b_b) * valid,)


def _f_ssd_dt(dtr, dtb):
    real = lax.broadcasted_iota(jnp.int32, (1, LANE), 1) < SSD_HEADS
    return (jnp.where(real, jax.nn.softplus(dtr + dtb), 0.0),)


def _f_ssd_post(y, z, g):
    return (_rms(y * jax.nn.silu(z), g),)


FFN_STRIP_ROWS = 128
CONF_STRIP_ROWS = 128
SSD_STRIP_ROWS = 128


def _f_ln_silu(x, g, b):
    return (jax.nn.silu(_layer_norm(x, g, b)),)


def _f_lru(io_ext, in_b, cw, cb, ga_w, ga_b, gx_w, gx_b, lam, *, valid):
    rows = io_ext.shape[0] - SUBLANE
    io = (io_ext + in_b) * valid
    gate = io[SUBLANE:, :LRU_W]
    xr = _causal_taps(io[:, LRU_W:], cw, SUBLANE, rows) + cb
    rs, iis = [], []
    for blk in range(LRU_W // LRU_BLOCK):
        sl = slice(blk * LRU_BLOCK, (blk + 1) * LRU_BLOCK)
        xb = xr[:, sl]
        rs.append(jax.nn.sigmoid(_dot_nn(xb, ga_w[sl, :]) + ga_b[:, sl]))
        iis.append(jax.nn.sigmoid(_dot_nn(xb, gx_w[sl, :]) + gx_b[:, sl]))
    r = jnp.concatenate(rs, axis=1)
    ig = jnp.concatenate(iis, axis=1)
    log_a = -LRU_C * r * jax.nn.softplus(-lam)
    a = jnp.exp(log_a)
    bterm = jnp.sqrt(-_expm1(2.0 * log_a)) * (ig * xr)
    return a, bterm, jax.nn.gelu(gate)


def _f_sgu(z, in_b, ln_g, ln_b, sp_w, sp_bt):
    rows = z.shape[0]
    zz = jax.nn.gelu(z + in_b)
    u, v = zz[:, :SGU_HALF], zz[:, SGU_HALF:]
    v = _layer_norm(v, ln_g, ln_b)
    tri = lax.broadcasted_iota(jnp.int32, (SGU_CHUNK, SGU_CHUNK), 0) >= lax.broadcasted_iota(
        jnp.int32, (SGU_CHUNK, SGU_CHUNK), 1)
    gdim = SGU_HALF // SGU_GROUPS
    row_blocks = []
    for ci in range(rows // SGU_CHUNK):
        col_blocks = []
        for g in range(SGU_GROUPS):
            w = jnp.where(tri, sp_w[g * SGU_CHUNK:(g + 1) * SGU_CHUNK, :], 0.0)
            vb = v[ci * SGU_CHUNK:(ci + 1) * SGU_CHUNK, g * gdim:(g + 1) * gdim]
            col_blocks.append(_dot_nn(w, vb) + sp_bt[:, g:g + 1])
        row_blocks.append(jnp.concatenate(col_blocks, axis=1))
    mixed = row_blocks[0] if len(row_blocks) == 1 else jnp.concatenate(row_blocks, axis=0)
    return (u * mixed,)


HEADS_PER_GROUP = 4
GROUP_COLS = 256
HEAD_DIM = 64


def _ssd_group(x, bm, cm, dt, st, a_log, dsk, g):
    q = x.shape[0]
    tri = lax.broadcasted_iota(jnp.int32, (q, q), 0) >= lax.broadcasted_iota(jnp.int32, (q, q), 1)
    d_a = dt * (-jnp.exp(a_log))
    acs = jnp.dot(tri.astype(f32), d_a, precision=HIGHEST, preferred_element_type=f32)
    acs_t = acs.T
    lane = lax.broadcasted_iota(jnp.int32, (1, LANE), 1)
    sub = lax.broadcasted_iota(jnp.int32, (LANE, 1), 0)
    col_idx = lax.broadcasted_iota(jnp.int32, (1, GROUP_COLS), 1)
    last_row = (lax.broadcasted_iota(jnp.int32, (q, 1), 0) == q - 1).astype(f32)
    cb = _dot_nt(cm, bm)
    y = jnp.zeros((q, GROUP_COLS), f32)
    e_in = jnp.zeros((q, GROUP_COLS), f32)
    d_end = jnp.zeros((q, GROUP_COLS), f32)
    d_last = jnp.zeros((1, GROUP_COLS), f32)
    d_skip = jnp.zeros((1, GROUP_COLS), f32)
    for j in range(HEADS_PER_GROUP):
        head = HEADS_PER_GROUP * g + j
        on_lane = (lane == head).astype(f32)
        on_sub = (sub == head).astype(f32)
        col = jnp.sum(acs * on_lane, axis=1, keepdims=True)
        row = jnp.sum(acs_t * on_sub, axis=0, keepdims=True)
        dtc = jnp.sum(dt * on_lane, axis=1, keepdims=True)
        last = jnp.sum(col * last_row, axis=0, keepdims=True)
        dsk_j = jnp.sum(dsk * on_lane, axis=1, keepdims=True)
        decay = jnp.where(tri, jnp.exp(jnp.where(tri, col - row, 0.0)), 0.0)
        mine = jnp.logical_and(col_idx >= j * HEAD_DIM, col_idx < (j + 1) * HEAD_DIM)
        y = y + _dot_nn(cb * decay, jnp.where(mine, x * dtc, 0.0))
        e_in = e_in + jnp.where(mine, jnp.exp(col), 0.0)
        d_end = d_end + jnp.where(mine, jnp.exp(last - col) * dtc, 0.0)
        d_last = d_last + jnp.where(mine, jnp.exp(last), 0.0)
        d_skip = d_skip + jnp.where(mine, dsk_j, 0.0)
    y = y + _dot_nn(cm, st) * e_in + x * d_skip
    st_new = st * d_last + _dot_tn(bm, x * d_end)
    return y, st_new


GROUPS_PER_STEP = 8


def _ssd_specs(rev, nc):
    def ch(c):
        return nc - 1 - c if rev else c

    gps = GROUPS_PER_STEP
    x_spec = pl.BlockSpec((SSD_CHUNK, gps * GROUP_COLS), lambda c, g: (ch(c), g))
    b_spec = pl.BlockSpec((SSD_CHUNK, gps * LANE), lambda c, g: (ch(c), SSD_D_INNER // (gps * LANE) + g))
    c_spec = pl.BlockSpec((SSD_CHUNK, gps * LANE), lambda c, g: (ch(c), (SSD_D_INNER + SSD_BC) // (gps * LANE) + g))
    dt_spec = pl.BlockSpec((SSD_CHUNK, LANE), lambda c, g: (ch(c), 0))
    row_spec = pl.BlockSpec((1, LANE), lambda c, g: (0, 0))
    st_spec = pl.BlockSpec((1, gps, LANE, GROUP_COLS), lambda c, g: (ch(c), g, 0, 0))
    wide_spec = pl.BlockSpec((SSD_CHUNK, SSD_CONV_DIM), lambda c, g: (ch(c), 0))
    return x_spec, b_spec, c_spec, dt_spec, row_spec, st_spec, wide_spec


def _ssd_fwd(xc, dt, a_log, dsk, gather=()):
    t_len = xc.shape[0]
    nc = t_len // SSD_CHUNK
    gps = GROUPS_PER_STEP
    n_gp = SSD_GROUPS // gps
    n_g = len(gather)
    x_spec, b_spec, c_spec, dt_spec, row_spec, st_spec, _ = _ssd_specs(False, nc)

    def body(*refs):
        x_ref, b_ref, c_ref, dt_ref, al_ref, dk_ref = refs[:6]
        g_srcs = refs[6:6 + n_g]
        y_ref, st_out_ref = refs[6 + n_g:8 + n_g]
        g_outs = refs[8 + n_g:8 + 2 * n_g]
        st_ref = refs[8 + 2 * n_g]
        g_sems = refs[9 + 2 * n_g:]
        c, gp = pl.program_id(0), pl.program_id(1)
        if n_g:
            @pl.when(jnp.logical_and(c == 0, gp == 0))
            def _():
                _gather_start(g_srcs, g_outs, *g_sems)

        for q in range(gps):
            g = gp * gps + q

            @pl.when(c == 0)
            def _():
                st_ref[g] = jnp.zeros((LANE, GROUP_COLS), f32)

            st = st_ref[g]
            st_out_ref[0, q] = st
            xq = slice(q * GROUP_COLS, (q + 1) * GROUP_COLS)
            bq = slice(q * LANE, (q + 1) * LANE)
            y, st_new = _ssd_group(x_ref[:, xq], b_ref[:, bq], c_ref[:, bq], dt_ref[...], st, al_ref[...],
                                   dk_ref[...], g)
            y_ref[:, xq] = y
            st_ref[g] = st_new

        if n_g:
            @pl.when(jnp.logical_and(c == nc - 1, gp == n_gp - 1))
            def _():
                _gather_finish(g_srcs, g_outs, *g_sems)

    res = pl.pallas_call(
        body, name="ssd_scan_fwd",
        out_shape=[jax.ShapeDtypeStruct((t_len, SSD_D_INNER), f32),
                   jax.ShapeDtypeStruct((nc, SSD_GROUPS, LANE, GROUP_COLS), f32)] + _gather_out_shapes(gather),
        grid=(nc, n_gp), in_specs=[x_spec, b_spec, c_spec, dt_spec, row_spec, row_spec] + [HBM_SPEC] * n_g,
        out_specs=[x_spec, st_spec] + [HBM_SPEC] * n_g,
        scratch_shapes=[pltpu.VMEM((SSD_GROUPS, LANE, GROUP_COLS), f32)] + (_gather_semaphores(n_g) if n_g else []),
        compiler_params=_cparams(("arbitrary", "arbitrary")),
    )(xc, xc, xc, dt, a_log, dsk, *gather)
    return res[0], res[1], list(res[2:])


def _ssd_bwd(xc, dt, a_log, dsk, states, dy, to_chips=()):
    t_len = xc.shape[0]
    nc = t_len // SSD_CHUNK
    gps = GROUPS_PER_STEP
    n_gp = SSD_GROUPS // gps
    n_s = len(to_chips)
    x_spec, b_spec, c_spec, dt_spec, row_spec, st_spec, wide_spec = _ssd_specs(True, nc)

    def body(*refs):
        x_ref, b_ref, c_ref, dt_ref, al_ref, dk_ref, st_in_ref, dy_ref = refs[:8]
        s_srcs = refs[8:8 + n_s]
        dxc_ref, ddt_ref, dal_ref, ddk_ref = refs[8 + n_s:12 + n_s]
        s_outs = refs[12 + n_s:12 + 2 * n_s]
        dst_ref = refs[12 + 2 * n_s]
        s_sems = refs[13 + 2 * n_s:]
        c, gp = pl.program_id(0), pl.program_id(1)

        @pl.when(jnp.logical_and(c == 0, gp == 0))
        def _():
            dal_ref[...] = jnp.zeros_like(dal_ref)
            ddk_ref[...] = jnp.zeros_like(ddk_ref)
            for cp in _to_chips_copies(s_srcs, s_outs, *s_sems) if n_s else []:
                cp.start()

        @pl.when(gp == 0)
        def _():
            ddt_ref[...] = jnp.zeros_like(ddt_ref)

        for q in range(gps):
            g = gp * gps + q

            @pl.when(c == 0)
            def _():
                dst_ref[g] = jnp.zeros((LANE, GROUP_COLS), f32)

            xq = slice(q * GROUP_COLS, (q + 1) * GROUP_COLS)
            bq = slice(q * LANE, (q + 1) * LANE)
            _, vjp = jax.vjp(lambda *args: _ssd_group(*args, g), x_ref[:, xq], b_ref[:, bq], c_ref[:, bq],
                             dt_ref[...], st_in_ref[0, q], al_ref[...], dk_ref[...])
            dx, db, dc, ddt, dst, dal, ddk = vjp((dy_ref[:, xq], dst_ref[g]))
            dxc_ref[:, pl.ds(pl.multiple_of(g * GROUP_COLS, GROUP_COLS), GROUP_COLS)] = dx
            dxc_ref[:, pl.ds(pl.multiple_of(SSD_D_INNER + g * LANE, LANE), LANE)] = db
            dxc_ref[:, pl.ds(pl.multiple_of(SSD_D_INNER + SSD_BC + g * LANE, LANE), LANE)] = dc
            ddt_ref[...] += ddt
            dst_ref[g] = dst
            dal_ref[...] += dal
            ddk_ref[...] += ddk

        if n_s:
            @pl.when(jnp.logical_and(c == nc - 1, gp == n_gp - 1))
            def _():
                for cp in _to_chips_copies(s_srcs, s_outs, *s_sems):
                    cp.wait()

    res = pl.pallas_call(
        body, name="ssd_scan_bwd",
        out_shape=[jax.ShapeDtypeStruct((t_len, SSD_CONV_DIM), f32), jax.ShapeDtypeStruct((t_len, LANE), f32),
                   jax.ShapeDtypeStruct((1, LANE), f32), jax.ShapeDtypeStruct((1, LANE), f32)]
        + _to_chips_out_shapes(to_chips),
        grid=(nc, n_gp),
        in_specs=[x_spec, b_spec, c_spec, dt_spec, row_spec, row_spec, st_spec, x_spec] + [HBM_SPEC] * n_s,
        out_specs=[wide_spec, dt_spec, row_spec, row_spec] + [HBM_SPEC] * n_s,
        scratch_shapes=[pltpu.VMEM((SSD_GROUPS, LANE, GROUP_COLS), f32)] + (_to_chips_semaphores(n_s) if n_s else []),
        compiler_params=_cparams(("arbitrary", "arbitrary")),
    )(xc, xc, xc, dt, a_log, dsk, states, dy, *to_chips)
    return res[0], res[1], res[2], res[3], list(res[4:])


LRU_ROWS = 256


def _lru_fwd(a, b, gg):
    t_len, cols = a.shape
    rows = min(LRU_ROWS, t_len)
    spec = pl.BlockSpec((rows, cols), lambda i: (i, 0))

    def body(a_ref, b_ref, g_ref, y_ref, h_ref, carry):
        i = pl.program_id(0)

        @pl.when(i == 0)
        def _():
            carry[...] = jnp.zeros_like(carry)

        av, bv = a_ref[...], b_ref[...]
        row = lax.broadcasted_iota(jnp.int32, av.shape, 0)
        s = 1
        while s < rows:
            a_prev = pltpu.roll(av, s, axis=0)
            b_prev = pltpu.roll(bv, s, axis=0)
            m = row >= s
            bv = jnp.where(m, av * b_prev + bv, bv)
            av = jnp.where(m, av * a_prev, av)
            s *= 2
        h = av * carry[0:1, :] + bv
        h_ref[...] = h
        y_ref[...] = g_ref[...] * h
        carry[0:1, :] = h[rows - 1:rows, :]

    return pl.pallas_call(
        body, name="lru_scan_fwd",
        out_shape=[jax.ShapeDtypeStruct((t_len, cols), f32), jax.ShapeDtypeStruct((t_len, cols), f32)],
        grid=(t_len // rows,), in_specs=[spec, spec, spec], out_specs=[spec, spec],
        scratch_shapes=[pltpu.VMEM((SUBLANE, cols), f32)],
        compiler_params=_cparams(("arbitrary",)),
    )(a, b, gg)


def _lru_bwd(dy, gg, a, h):
    t_len, cols = a.shape
    rows = min(LRU_ROWS, t_len)
    n_tiles = t_len // rows
    per = rows // SUBLANE
    spec = pl.BlockSpec((rows, cols), lambda i: (n_tiles - 1 - i, 0))
    prev_spec = pl.BlockSpec((SUBLANE, cols), lambda i: (jnp.maximum((n_tiles - 1 - i) * per - 1, 0), 0))

    def body(dy_ref, g_ref, a_ref, h_ref, hp_ref, da_ref, db_ref, dg_ref, carry_dh, carry_a):
        i = pl.program_id(0)
        tile_id = n_tiles - 1 - i

        @pl.when(i == 0)
        def _():
            carry_dh[...] = jnp.zeros_like(carry_dh)
            carry_a[...] = jnp.zeros_like(carry_a)

        av, hv, dyv = a_ref[...], h_ref[...], dy_ref[...]
        row = lax.broadcasted_iota(jnp.int32, av.shape, 0)
        dg_ref[...] = dyv * hv
        bv = dyv * g_ref[...]
        cv = jnp.where(row == rows - 1, carry_a[0:1, :], pltpu.roll(av, rows - 1, axis=0))
        s = 1
        while s < rows:
            c_next = pltpu.roll(cv, rows - s, axis=0)
            b_next = pltpu.roll(bv, rows - s, axis=0)
            m = row < rows - s
            bv = jnp.where(m, cv * b_next + bv, bv)
            cv = jnp.where(m, cv * c_next, cv)
            s *= 2
        dh = cv * carry_dh[0:1, :] + bv
        h_before = jnp.where(tile_id > 0, hp_ref[SUBLANE - 1:SUBLANE, :], jnp.zeros((1, cols), f32))
        h_prev = jnp.where(row == 0, h_before, pltpu.roll(hv, 1, axis=0))
        da_ref[...] = dh * h_prev
        db_ref[...] = dh
        carry_dh[0:1, :] = dh[0:1, :]
        carry_a[0:1, :] = av[0:1, :]

    return pl.pallas_call(
        body, name="lru_scan_bwd",
        out_shape=[jax.ShapeDtypeStruct((t_len, cols), f32)] * 3,
        grid=(n_tiles,), in_specs=[spec, spec, spec, spec, prev_spec], out_specs=[spec, spec, spec],
        scratch_shapes=[pltpu.VMEM((SUBLANE, cols), f32), pltpu.VMEM((SUBLANE, cols), f32)],
        compiler_params=_cparams(("arbitrary",)),
    )(dy, gg, a, h, h)


def _loss_head(h, target, g):
    t_len = h.shape[0]
    rows = min(512, t_len)

    def f(hv, gv, tv):
        err = _rms(hv, gv) - tv
        return 0.5 * jnp.sum(jnp.mean(err * err, axis=-1, keepdims=True), axis=0, keepdims=True)

    def body(h_ref, t_ref, g_ref, dh_ref, dg_ref, loss_ref):
        i = pl.program_id(0)

        @pl.when(i == 0)
        def _():
            dg_ref[...] = jnp.zeros_like(dg_ref)
            loss_ref[...] = jnp.zeros_like(loss_ref)

        tv = t_ref[...]
        part, vjp = jax.vjp(lambda hv, gv: f(hv, gv, tv), h_ref[...], g_ref[...])
        dh, dg = vjp(jnp.ones((1, 1), f32))
        dh_ref[...] = dh
        dg_ref[...] += dg
        loss_ref[...] += jnp.broadcast_to(part, loss_ref.shape)

    spec = pl.BlockSpec((rows, D_MODEL), lambda i: (i, 0))
    return pl.pallas_call(
        body, name="loss_head",
        out_shape=[jax.ShapeDtypeStruct((t_len, D_MODEL), f32), jax.ShapeDtypeStruct((1, D_MODEL), f32),
                   jax.ShapeDtypeStruct((1, LANE), f32)],
        grid=(t_len // rows,), in_specs=[spec, spec, pl.BlockSpec((1, D_MODEL), lambda i: (0, 0))],
        out_specs=[spec, pl.BlockSpec((1, D_MODEL), lambda i: (0, 0)), pl.BlockSpec((1, LANE), lambda i: (0, 0))],
        compiler_params=_cparams(("arbitrary",)),
    )(h, target, g)


def _as2d(a):
    return a.reshape((-1, a.shape[-1])) if a.ndim > 1 else a.reshape((1, -1))


def _row_block(rows, cols, bytes_cap=1 << 20):
    if rows * cols * 4 <= bytes_cap or rows % SUBLANE:
        return rows
    return _tile(rows, max(SUBLANE, (bytes_cap // (cols * 4)) // SUBLANE * SUBLANE), SUBLANE)


def _adamw(w, g, m, v, name):
    shape = w.shape
    w2, g2, m2, v2 = _as2d(w), _as2d(g), _as2d(m), _as2d(v)
    rows, cols = w2.shape
    rb = _row_block(rows, cols)

    def body(w_ref, g_ref, m_ref, v_ref, d_ref, nm_ref, nv_ref):
        gv = g_ref[...]
        nm = ADAM_B1 * m_ref[...] + (1.0 - ADAM_B1) * gv
        nv = ADAM_B2 * v_ref[...] + (1.0 - ADAM_B2) * jnp.square(gv)
        m_hat = nm / (1.0 - ADAM_B1 ** ADAM_STEP)
        v_hat = nv / (1.0 - ADAM_B2 ** ADAM_STEP)
        d_ref[...] = -ADAM_LR * (m_hat / (jnp.sqrt(v_hat) + ADAM_EPS) + ADAM_WD * w_ref[...])
        nm_ref[...] = nm
        nv_ref[...] = nv

    spec = pl.BlockSpec((rb, cols), lambda i: (i, 0))
    d, nm, nv = pl.pallas_call(
        body, name=name, out_shape=[jax.ShapeDtypeStruct((rows, cols), f32)] * 3,
        grid=(rows // rb,), in_specs=[spec] * 4, out_specs=[spec] * 3,
        compiler_params=_cparams(("parallel",)),
    )(w2, g2, m2, v2)
    return d.reshape(shape), nm.reshape(shape), nv.reshape(shape)


def _sum_with_sibling(g_halves, theirs, c_idx):
    n_sh, _, rows, cols = g_halves.shape
    rb = _tile(rows, 512, 2 * SUBLANE)

    def body(c_ref, mine_ref, theirs_ref, o_ref):
        o_ref[...] = (mine_ref[...] + theirs_ref[...]).astype(bf16)

    grid_spec = pltpu.PrefetchScalarGridSpec(
        num_scalar_prefetch=1, grid=(n_sh, rows // rb),
        in_specs=[pl.BlockSpec((None, None, rb, cols), lambda k, i, c_ref: (k, c_ref[0], i, 0)),
                  pl.BlockSpec((None, rb, cols), lambda k, i, c_ref: (k, i, 0))],
        out_specs=pl.BlockSpec((None, rb, cols), lambda k, i, c_ref: (k, i, 0)))
    return pl.pallas_call(
        body, name="grad_sum_sibling", out_shape=jax.ShapeDtypeStruct((n_sh, rows, cols), bf16),
        grid_spec=grid_spec, compiler_params=_cparams(("parallel", "parallel")),
    )(c_idx, g_halves, theirs)


def _sum_chips(partial, received, k_idx):
    _, rows, cols = partial.shape
    rb = _tile(rows, 512, 2 * SUBLANE)

    def body(k_ref, mine_ref, r_ref, o_ref):
        acc = mine_ref[...].astype(f32)
        for j in range(N_CHIPS - 1):
            acc = acc + r_ref[j].astype(f32)
        o_ref[...] = acc

    grid_spec = pltpu.PrefetchScalarGridSpec(
        num_scalar_prefetch=1, grid=(rows // rb,),
        in_specs=[pl.BlockSpec((None, rb, cols), lambda i, k_ref: (k_ref[0], i, 0)),
                  pl.BlockSpec((N_CHIPS - 1, rb, cols), lambda i, k_ref: (0, i, 0))],
        out_specs=pl.BlockSpec((rb, cols), lambda i, k_ref: (i, 0)))
    return pl.pallas_call(
        body, name="grad_sum_chips", out_shape=jax.ShapeDtypeStruct((rows, cols), f32),
        grid_spec=grid_spec, compiler_params=_cparams(("parallel",)),
    )(k_idx, partial, received)


HBM_SPEC = pl.BlockSpec(memory_space=pltpu.HBM)
CHIP_FLIPS = ((0, 1), (1, 0), (1, 1))


def _position():
    return lax.axis_index("x"), lax.axis_index("y"), lax.axis_index("c")


def _own_slot(gathered, mine, index):
    return [lax.dynamic_update_index_in_dim(g, m, index, 0) for g, m in zip(gathered, mine)]


def _gather_weights(blocks):
    n = len(blocks)

    def body(*refs):
        srcs, outs = refs[:n], refs[n:2 * n]
        send_sems, recv_sems = refs[2 * n:]
        _gather_start(srcs, outs, send_sems, recv_sems)
        _gather_finish(srcs, outs, send_sems, recv_sems)

    return pl.pallas_call(
        body, name="gather_weights", out_shape=_gather_out_shapes(blocks),
        in_specs=[HBM_SPEC] * n, out_specs=[HBM_SPEC] * n, scratch_shapes=_gather_semaphores(n),
    )(*blocks)


def _gather_out_shapes(blocks):
    return [jax.ShapeDtypeStruct((N_CHIPS,) + b.shape, b.dtype) for b in blocks]


def _gather_semaphores(n):
    n_sem = 2 * len(CHIP_FLIPS) * n
    return [pltpu.SemaphoreType.DMA((n_sem,)), pltpu.SemaphoreType.DMA((n_sem,))]


def _gather_copies(srcs, outs, send_sems, recv_sems):
    n_far = len(CHIP_FLIPS)
    x, y, c = _position()
    k = 2 * x + y
    first, passed = [], []
    for a in range(len(srcs)):
        for j, (fx, fy) in enumerate(CHIP_FLIPS):
            s = a * 2 * n_far + j
            kk = 2 * (x ^ fx) + (y ^ fy)
            first.append(pltpu.make_async_remote_copy(
                src_ref=srcs[a].at[c], dst_ref=outs[a].at[k, c], send_sem=send_sems.at[s],
                recv_sem=recv_sems.at[s], device_id=(x ^ fx, y ^ fy, c), device_id_type=MESH))
            passed.append(pltpu.make_async_remote_copy(
                src_ref=outs[a].at[kk, c], dst_ref=outs[a].at[kk, c], send_sem=send_sems.at[s + n_far],
                recv_sem=recv_sems.at[s + n_far], device_id=(x, y, 1 - c), device_id_type=MESH))
    return first, passed


def _gather_start(srcs, outs, send_sems, recv_sems):
    first, _ = _gather_copies(srcs, outs, send_sems, recv_sems)
    for cp in first:
        cp.start()


def _gather_finish(srcs, outs, send_sems, recv_sems):
    first, passed = _gather_copies(srcs, outs, send_sems, recv_sems)
    for arrived, onward in zip(first, passed):
        arrived.wait_recv()
        onward.start()
    for cp in passed:
        cp.wait_recv()
    for cp in first + passed:
        cp.wait_send()


def _swap_with_sibling(grads):
    rider = _swap_rider(grads)
    n = len(grads)

    def body(*refs):
        rider.start(refs[:n], refs[n:2 * n], refs[2 * n:])
        rider.finish(refs[:n], refs[n:2 * n], refs[2 * n:])

    return pl.pallas_call(
        body, name="grad_swap_sibling", out_shape=rider.out_shapes,
        in_specs=[HBM_SPEC] * n, out_specs=[HBM_SPEC] * n, scratch_shapes=rider.semaphores,
    )(*grads)


class _Rider:
    def __init__(self, operands, out_shapes, semaphores, start, finish):
        self.operands, self.out_shapes, self.semaphores = list(operands), list(out_shapes), list(semaphores)
        self.start, self.finish = start, finish
        self.n = len(self.operands)


def _swap_copies(srcs, outs, send_sems, recv_sems):
    x, y, c = _position()
    copies = []
    for a in range(len(srcs)):
        for kk in range(N_CHIPS):
            s = a * N_CHIPS + kk
            copies.append(pltpu.make_async_remote_copy(
                src_ref=srcs[a].at[kk, 1 - c], dst_ref=outs[a].at[kk], send_sem=send_sems.at[s],
                recv_sem=recv_sems.at[s], device_id=(x, y, 1 - c), device_id_type=MESH))
    return copies


def _swap_rider(grads):
    n_sem = N_CHIPS * len(grads)

    def start(srcs, outs, sems):
        for cp in _swap_copies(srcs, outs, *sems):
            cp.start()

    def finish(srcs, outs, sems):
        for cp in _swap_copies(srcs, outs, *sems):
            cp.wait()

    return _Rider(grads, [jax.ShapeDtypeStruct((N_CHIPS,) + g.shape[2:], g.dtype) for g in grads],
                  [pltpu.SemaphoreType.DMA((n_sem,)), pltpu.SemaphoreType.DMA((n_sem,))], start, finish)


def _gather_rider(blocks):
    def start(srcs, outs, sems):
        _gather_start(srcs, outs, *sems)

    def finish(srcs, outs, sems):
        _gather_finish(srcs, outs, *sems)

    return _Rider(blocks, _gather_out_shapes(blocks), _gather_semaphores(len(blocks)), start, finish)


def _send_to_chips(partials):
    n = len(partials)

    def body(*refs):
        srcs, outs = refs[:n], refs[n:2 * n]
        send_sems, recv_sems = refs[2 * n:]
        for cp in _to_chips_copies(srcs, outs, send_sems, recv_sems):
            cp.start()
        for cp in _to_chips_copies(srcs, outs, send_sems, recv_sems):
            cp.wait()

    return pl.pallas_call(
        body, name="grad_to_chips", out_shape=_to_chips_out_shapes(partials),
        in_specs=[HBM_SPEC] * n, out_specs=[HBM_SPEC] * n, scratch_shapes=_to_chips_semaphores(n),
    )(*partials)


def _to_chips_out_shapes(partials):
    return [jax.ShapeDtypeStruct((len(CHIP_FLIPS),) + p.shape[1:], p.dtype) for p in partials]


def _to_chips_semaphores(n):
    n_sem = len(CHIP_FLIPS) * n
    return [pltpu.SemaphoreType.DMA((n_sem,)), pltpu.SemaphoreType.DMA((n_sem,))]


def _to_chips_copies(srcs, outs, send_sems, recv_sems):
    n_far = len(CHIP_FLIPS)
    x, y, c = _position()
    copies = []
    for a in range(len(srcs)):
        for j, (fx, fy) in enumerate(CHIP_FLIPS):
            s = a * n_far + j
            kk = 2 * (x ^ fx) + (y ^ fy)
            copies.append(pltpu.make_async_remote_copy(
                src_ref=srcs[a].at[kk], dst_ref=outs[a].at[j], send_sem=send_sems.at[s],
                recv_sem=recv_sems.at[s], device_id=(x ^ fx, y ^ fy, c), device_id_type=MESH))
    return copies


def _join_halves(halves):
    n = len(halves)

    def body(*refs):
        srcs, outs = refs[:n], refs[n:2 * n]
        send_sems, recv_sems = refs[2 * n:]
        x, y, c = _position()
        copies = []
        for a in range(n):
            cp = pltpu.make_async_remote_copy(
                src_ref=srcs[a], dst_ref=outs[a].at[c], send_sem=send_sems.at[a], recv_sem=recv_sems.at[a],
                device_id=(x, y, 1 - c), device_id_type=MESH)
            cp.start()
            copies.append(cp)
        for cp in copies:
            cp.wait()

    return pl.pallas_call(
        body, name="grad_join_halves",
        out_shape=[jax.ShapeDtypeStruct((2,) + h.shape, h.dtype) for h in halves],
        in_specs=[HBM_SPEC] * n, out_specs=[HBM_SPEC] * n,
        scratch_shapes=[pltpu.SemaphoreType.DMA((n,)), pltpu.SemaphoreType.DMA((n,))],
    )(*halves)


def _all_sum_small(vec):
    rows, cols = vec.shape

    def body(v_ref, o_ref, buf, send_sems, recv_sems):
        x, y, c = _position()
        me = 4 * x + 2 * y + c
        buf[me] = v_ref[...]
        copies = []
        for m in range(1, N_DEV):
            fx, fy, fc = (m >> 2) & 1, (m >> 1) & 1, m & 1
            cp = pltpu.make_async_remote_copy(
                src_ref=v_ref, dst_ref=buf.at[me], send_sem=send_sems.at[m - 1], recv_sem=recv_sems.at[m - 1],
                device_id=(x ^ fx, y ^ fy, c ^ fc), device_id_type=MESH)
            cp.start()
            copies.append(cp)
        for cp in copies:
            cp.wait()
        acc = buf[0]
        for d in range(1, N_DEV):
            acc = acc + buf[d]
        o_ref[...] = acc

    return pl.pallas_call(
        body, name="all_sum_small", out_shape=jax.ShapeDtypeStruct((rows, cols), f32),
        in_specs=[pl.BlockSpec(memory_space=pltpu.VMEM)], out_specs=pl.BlockSpec(memory_space=pltpu.VMEM),
        scratch_shapes=[pltpu.VMEM((N_DEV, rows, cols), f32), pltpu.SemaphoreType.DMA((N_DEV - 1,)),
                        pltpu.SemaphoreType.DMA((N_DEV - 1,))],
        compiler_params=_cparams(),
    )(vec)


FLAT_QUANTUM = 2 * 2 * SUBLANE * FLAT_COLS


def _pack(arrays, dtype):
    flat = jnp.concatenate([a.astype(dtype).reshape(-1) for a in arrays])
    n = flat.shape[0]
    n_pad = -(-n // FLAT_QUANTUM) * FLAT_QUANTUM
    return jnp.pad(flat, (0, n_pad - n))


def _unpack(flat, shapes):
    out, off = [], 0
    for s in shapes:
        n = int(np.prod(s))
        out.append(flat[..., off:off + n].reshape(flat.shape[:-1] + tuple(s)))
        off += n
    return out


def _full_from_shards(stacked, axis):
    return jnp.concatenate([stacked[k] for k in range(N_CHIPS)], axis=axis)


def _shards_of(full, axis):
    return jnp.stack(jnp.split(full, N_CHIPS, axis=axis))


def _ffn_fwd(h, u, p, next_gain):
    a = _mm_w(u, p['up'], 'nn', "ffn_up")
    gated = _conv_fwd(_ffn_conv_plan(a, p), [(FFN_H, bf16, (0,))], "ffn_gate")[0]
    h_out, u_next = _mm_normed(gated, p['down'], 'nn', "ffn_down", add=h, rms_gain=next_gain)
    return h_out, u_next, (h, u, a, gated)


def _ffn_conv_plan(a, p):
    both = (0, FFN_H)
    return _ConvPlan(a, p['cw'], p['cb'], in_bases=both, mid_bases=both, width=FFN_H, rows=256, rs=FFN_STRIP_ROWS,
                     post=_post_ffn_gate)


def _ffn_bwd(dh_out, p, saved, bias_zero, make_rider=None):
    h, u, a, gated = saved
    d_gated = _mm(dh_out, p['down'], 'nt', "ffn_down_dx", out_dtype=bf16)
    d_down = _mm(gated, dh_out, 'tn', "ffn_down_dw")
    rider = make_rider(d_down) if make_rider else None
    res = _conv_bwd(_ffn_conv_plan(a, p), [(d_gated, (0,))], bf16, "ffn_gate_bwd", rider=rider)
    da, d_cw, d_cb = res[:3]
    d_up = _mm(u, da, 'tn', "ffn_up_dw", out_cols_sharded=True)
    du = _mm_w(da, p['up'], 'nt', "ffn_up_dx", out_dtype=bf16)
    dh, d_g, d_bias = _row_bwd(_f_rms_res, [h], [p['g'], bias_zero], [du, dh_out], rows=512, name="ffn_norm_bwd")
    return dh, {'g': d_g, 'up': d_up, 'down': d_down, 'cw': d_cw, 'cb': d_cb}, d_bias, list(res[3:])


def _mixer_norm_bwd(h, g, du, dh_res, name):
    def f(hv, gv):
        return _rms(hv, gv), hv

    dh, d_g = _row_bwd(f, [h], [g], [du, dh_res], rows=512, name=name)
    return dh, d_g


def _ssd_layer_fwd(h, u, p, next_gain, gather=()):
    z = _mm(u, p['w_z'], 'nn', "ssd_in_z")
    xbc = _mm(u, p['w_xbc'], 'nn', "ssd_in_xbc")
    dtr = _mm(u, p['w_dt'], 'nn', "ssd_in_dt")
    xc = _conv_fwd(_ssd_conv_plan(xbc, p), [(SSD_CONV_DIM, f32, (0,))], "ssd_conv")[0]
    dt = _row_fwd(_f_ssd_dt, [dtr], [p['dtb']], [(LANE, f32)], rows=1024, name="ssd_dt")[0]
    y, states, gathered = _ssd_fwd(xc, dt, p['a_log'], p['dsk'], gather)
    yn = _row_fwd(_f_ssd_post, [y, z], [p['norm']], [(SSD_D_INNER, bf16)], rows=256, name="ssd_gate_norm")[0]
    h_out, u_next = _mm_normed(yn, p['out'], 'nn', "ssd_out", add=h, rms_gain=next_gain)
    return h_out, u_next, (h, u, z, xbc, dtr, xc, dt, states, y, yn), gathered


def _ssd_conv_plan(xbc, p):
    return _ConvPlan(xbc, p['cw'], p['cb'], in_bases=(0,), mid_bases=(0,), width=SSD_CONV_DIM, rows=256,
                     rs=SSD_STRIP_ROWS, post=_post_silu)


def _ssd_layer_bwd(dh_out, p, saved, to_chips=()):
    h, u, z, xbc, dtr, xc, dt, states, y, yn = saved
    d_yn = _mm(dh_out, p['out'], 'nt', "ssd_out_dx", out_dtype=bf16)
    d_out = _mm(yn, dh_out, 'tn', "ssd_out_dw")
    dy, dz, d_norm = _row_bwd(_f_ssd_post, [y, z], [p['norm']], [d_yn], rows=256, name="ssd_gate_norm_bwd",
                              tile_dtypes=[f32, bf16])
    dxc, ddt, d_alog, d_dsk, received = _ssd_bwd(xc, dt, p['a_log'], p['dsk'], states, dy, to_chips)
    dxbc, d_cw, d_cb = _conv_bwd(_ssd_conv_plan(xbc, p), [(dxc, (0,))], bf16, "ssd_conv_bwd")
    ddtr, d_dtb = _row_bwd(_f_ssd_dt, [dtr], [p['dtb']], [ddt], rows=1024, name="ssd_dt_bwd", tile_dtypes=[bf16])
    d_wz = _mm(u, dz, 'tn', "ssd_in_z_dw")
    d_wxbc = _mm(u, dxbc, 'tn', "ssd_in_xbc_dw")
    d_wdt = _mm(u, ddtr, 'tn', "ssd_in_dt_dw")
    du = _mm(dz, p['w_z'], 'nt', "ssd_in_z_dx")
    du = _mm(dxbc, p['w_xbc'], 'nt', "ssd_in_xbc_dx", add=du)
    du = _mm(ddtr, p['w_dt'], 'nt', "ssd_in_dt_dx", add=du, out_dtype=bf16)
    dh, d_g = _mixer_norm_bwd(h, p['g'], du, dh_out, "ssd_norm_bwd")
    grads = {'g': d_g, 'w_z': d_wz, 'w_xbc': d_wxbc, 'w_dt': d_wdt, 'cw': d_cw, 'cb': d_cb, 'dtb': d_dtb,
             'a_log': d_alog, 'dsk': d_dsk, 'norm': d_norm, 'out': d_out}
    return dh, grads, received


def _conf_layer_fwd(h, u, p, next_gain, rider=None):
    g2 = _mm_w(u, p['pw1'], 'nn', "conf_pw1")
    res = _conv_fwd(_conf_conv_plan(g2, p), [(D_MODEL, f32, (0,))], "conf_conv", rider=rider)
    conv = res[0]
    s = _row_fwd(_f_ln_silu, [conv], [p['ln_g'], p['ln_b']], [(D_MODEL, bf16)], rows=256, name="conf_ln")[0]
    h_out, u_next = _mm_normed(s, p['pw2'], 'nn', "conf_pw2", bias=p['b2'], add=h, rms_gain=next_gain)
    return h_out, u_next, (h, u, g2, conv, s), list(res[1:])


def _conf_conv_plan(g2, p):
    halves = (0, D_MODEL)
    return _ConvPlan(g2, p['dw_w'], p['dw_b'], in_bases=halves, mid_bases=(0,), width=D_MODEL, rows=256,
                     rs=CONF_STRIP_ROWS, pre=_pre_glu, pre_params=[(p['b1'], halves)], post=_post_identity)


def _conf_layer_bwd(dh_out, p, saved):
    h, u, g2, conv, s = saved
    ds = _mm(dh_out, p['pw2'], 'nt', "conf_pw2_dx", out_dtype=bf16)
    d_pw2 = _mm(s, dh_out, 'tn', "conf_pw2_dw")
    d_conv, d_lng, d_lnb = _row_bwd(_f_ln_silu, [conv], [p['ln_g'], p['ln_b']], [ds], rows=256, name="conf_ln_bwd")
    dg2, d_dww, d_dwb, d_b1 = _conv_bwd(_conf_conv_plan(g2, p), [(d_conv, (0,))], bf16, "conf_conv_bwd")
    d_pw1 = _mm(u, dg2, 'tn', "conf_pw1_dw", out_cols_sharded=True)
    du = _mm_w(dg2, p['pw1'], 'nt', "conf_pw1_dx", out_dtype=bf16)
    dh, d_g = _mixer_norm_bwd(h, p['g'], du, dh_out, "conf_norm_bwd")
    grads = {'g': d_g, 'pw1': d_pw1, 'b1': d_b1, 'dw_w': d_dww, 'dw_b': d_dwb, 'ln_g': d_lng, 'ln_b': d_lnb,
             'pw2': d_pw2}
    return dh, grads


def _lru_params(p):
    return [p['in_b'], p['cw'], p['cb'], p['ga_w'], p['ga_b'], p['gx_w'], p['gx_b'], p['lam']]


def _lru_layer_fwd(h, u, p, next_gain):
    io = _mm_w(u, p['in_w'], 'nn', "lru_in")
    a, b, gg = _row_fwd(_f_lru, [io], _lru_params(p), [(LRU_W, f32)] * 3, rows=256, name="lru_gates",
                        halo=SUBLANE, halo_of=[True])
    y, hs = _lru_fwd(a, b, gg)
    h_out, u_next = _mm_normed(y, p['out'], 'nn', "lru_out", bias=p['out_b'], add=h, rms_gain=next_gain)
    return h_out, u_next, (h, u, io, a, gg, hs, y)


def _lru_layer_bwd(dh_out, p, saved):
    h, u, io, a, gg, hs, y = saved
    dy = _mm(dh_out, p['out'], 'nt', "lru_out_dx")
    d_out = _mm(y, dh_out, 'tn', "lru_out_dw")
    da, db, dgg = _lru_bwd(dy, gg, a, hs)
    res = _row_bwd(_f_lru, [io], _lru_params(p), [da, db, dgg], rows=256, name="lru_gates_bwd",
                   halo=SUBLANE, halo_of=[True], tile_dtypes=[bf16])
    dio, d_inb, d_cw, d_cb, d_gaw, d_gab, d_gxw, d_gxb, d_lam = res
    d_inw = _mm(u, dio, 'tn', "lru_in_dw", out_cols_sharded=True)
    du = _mm_w(dio, p['in_w'], 'nt', "lru_in_dx", out_dtype=bf16)
    dh, d_g = _mixer_norm_bwd(h, p['g'], du, dh_out, "lru_norm_bwd")
    grads = {'g': d_g, 'in_w': d_inw, 'in_b': d_inb, 'cw': d_cw, 'cb': d_cb, 'ga_w': d_gaw, 'ga_b': d_gab,
             'gx_w': d_gxw, 'gx_b': d_gxb, 'lam': d_lam, 'out': d_out}
    return dh, grads


def _sgu_params(p):
    return [p['in_b'], p['ln_g'], p['ln_b'], p['sp_w'], p['sp_bt']]


def _sgu_layer_fwd(h, u, p, next_gain):
    z = _mm_w(u, p['in_w'], 'nn', "sgu_in")
    s = _row_fwd(_f_sgu, [z], _sgu_params(p), [(SGU_HALF, bf16)], rows=SGU_CHUNK, name="sgu_mix")[0]
    h_out, u_next = _mm_normed(s, p['out'], 'nn', "sgu_out", bias=p['out_b'], add=h, rms_gain=next_gain)
    return h_out, u_next, (h, u, z, s)


def _sgu_layer_bwd(dh_out, p, saved):
    h, u, z, s = saved
    ds = _mm(dh_out, p['out'], 'nt', "sgu_out_dx", out_dtype=bf16)
    d_out = _mm(s, dh_out, 'tn', "sgu_out_dw")
    dz, d_inb, d_lng, d_lnb, d_spw, d_spbt = _row_bwd(_f_sgu, [z], _sgu_params(p), [ds], rows=SGU_CHUNK,
                                                      name="sgu_mix_bwd", tile_dtypes=[bf16])
    d_inw = _mm(u, dz, 'tn', "sgu_in_dw", out_cols_sharded=True)
    du = _mm_w(dz, p['in_w'], 'nt', "sgu_in_dx", out_dtype=bf16)
    dh, d_g = _mixer_norm_bwd(h, p['g'], du, dh_out, "sgu_norm_bwd")
    grads = {'g': d_g, 'in_w': d_inw, 'in_b': d_inb, 'ln_g': d_lng, 'ln_b': d_lnb, 'sp_w': d_spw, 'sp_bt': d_spbt,
             'out': d_out}
    return dh, grads


def _row(v):
    return v.reshape((1, -1)).astype(f32)


def _pad_lanes(v, n=LANE):
    v = _row(v)
    return jnp.pad(v, ((0, 0), (0, n - v.shape[1])))


def _local_step(x, target, w, comm=None):
    a_in = w['a_in_proj'][0]
    pa = {'g': _row(w['norm_mix'][0]), 'w_z': a_in[:, :SSD_D_INNER],
          'w_xbc': a_in[:, SSD_D_INNER:SSD_D_INNER + SSD_CONV_DIM],
          'w_dt': jnp.pad(a_in[:, SSD_D_INNER + SSD_CONV_DIM:], ((0, 0), (0, LANE - SSD_HEADS))),
          'cw': w['a_conv_w'][0].astype(f32), 'cb': _row(w['a_conv_b'][0]), 'dtb': _pad_lanes(w['a_dt_bias'][0]),
          'a_log': _pad_lanes(w['a_log'][0]), 'dsk': _pad_lanes(w['a_d_skip'][0]), 'norm': _row(w['a_norm'][0]),
          'out': w['a_out_proj']}
    mix_gain = [_row(w['norm_mix'][i]) for i in range(DEPTH)] + [None]
    ffn_gain = [_row(w['norm_ffn'][i]) for i in range(DEPTH)]
    u = _row_fwd(_f_rms, [x], [mix_gain[0]], [(D_MODEL, bf16)], rows=512, name="first_norm")[0]
    h, u, s_mix0, gathered = _ssd_layer_fwd(x, u, pa, ffn_gain[0], gather=comm.late_blocks if comm else ())
    if comm:
        w = {**w, **comm.late_weights(gathered)}
    ffn = [{'g': _row(w['norm_ffn'][i]), 'up': (w['f_up_w'], i), 'down': w['f_down_w'][i],
            'cw': w['f_conv_w'][i].astype(f32), 'cb': _row(w['f_conv_b'][i])} for i in range(DEPTH)]
    pb = {'g': _row(w['norm_mix'][1]), 'pw1': (w['b_pw1_w'], 0), 'b1': _row(w['b_pw1_b'][0]),
          'dw_w': w['b_dw_w'][0].astype(f32), 'dw_b': _row(w['b_dw_b'][0]), 'ln_g': _row(w['b_ln_g'][0]),
          'ln_b': _row(w['b_ln_b'][0]), 'pw2': w['b_pw2_w'], 'b2': _row(w['b_pw2_b'][0])}
    h, u, s_ffn0 = _ffn_fwd(h, u, ffn[0], mix_gain[1])
    h, u, s_mix1, gathered = _conf_layer_fwd(h, u, pb, ffn_gain[1], rider=comm.second_rider() if comm else None)
    if comm:
        w = {**w, **comm.second_weights(gathered)}
    pc = {'g': _row(w['norm_mix'][2]), 'in_w': (w['c_in_w'], 0), 'in_b': _row(w['c_in_b'][0]),
          'cw': w['c_conv_w'][0].astype(f32), 'cb': _row(w['c_conv_b'][0]),
          'ga_w': w['c_ga_w'][0].reshape(LRU_W, LRU_BLOCK).astype(f32), 'ga_b': _row(w['c_ga_b'][0]),
          'gx_w': w['c_gx_w'][0].reshape(LRU_W, LRU_BLOCK).astype(f32), 'gx_b': _row(w['c_gx_b'][0]),
          'lam': _row(w['c_lambda'][0]), 'out': w['c_out_w'], 'out_b': _row(w['c_out_b'][0])}
    pd = {'g': _row(w['norm_mix'][3]), 'in_w': (w['d_in_w'], 0), 'in_b': _row(w['d_in_b'][0]),
          'ln_g': _row(w['d_ln_g'][0]), 'ln_b': _row(w['d_ln_b'][0]),
          'sp_w': w['d_sp_w'][0].reshape(SGU_GROUPS * SGU_CHUNK, SGU_CHUNK).astype(f32),
          'sp_bt': w['d_sp_b'][0].astype(f32).T, 'out': w['d_out_w'], 'out_b': _row(w['d_out_b'][0])}
    mixers = [(None, None, pa), (None, _conf_layer_bwd, pb),
              (_lru_layer_fwd, _lru_layer_bwd, pc), (_sgu_layer_fwd, _sgu_layer_bwd, pd)]

    h, u, s_ffn1 = _ffn_fwd(h, u, ffn[1], mix_gain[2])
    saved = [(s_mix0, s_ffn0), (s_mix1, s_ffn1)]
    for i in range(2, DEPTH):
        fwd, _, p = mixers[i]
        h, u, s_mix = fwd(h, u, p, ffn_gain[i])
        h, u, s_ffn = _ffn_fwd(h, u, ffn[i], mix_gain[i + 1])
        saved.append((s_mix, s_ffn))
    dh, d_final, loss = _loss_head(h, target, _row(w['norm_final']))

    def rows_sharded(g):
        return g.reshape(N_CHIPS, g.shape[0] // N_CHIPS, g.shape[1])

    bias_zero = jnp.zeros((1, D_MODEL), f32)
    g_ffn, g_mix, d_out_bias = [None] * DEPTH, [None] * DEPTH, [None] * DEPTH
    for i in reversed(range(1, DEPTH)):
        _, bwd, p = mixers[i]
        dh, g_ffn[i], d_out_bias[i], _ = _ffn_bwd(dh, ffn[i], saved[i][1], bias_zero)
        dh, g_mix[i] = bwd(dh, p, saved[i][0])
    _, gb, gc, gd = g_mix

    def late_direct_grads(d_up0, d_down0):
        return {'b_pw1_w': gb['pw1'], 'b_pw2_w': rows_sharded(gb['pw2']), 'c_in_w': gc['in_w'],
                'c_out_w': rows_sharded(gc['out']), 'd_in_w': gd['in_w'], 'd_out_w': rows_sharded(gd['out']),
                'f_up_w': [d_up0] + [g['up'] for g in g_ffn[1:]],
                'f_down_w': [rows_sharded(d_down0)] + [rows_sharded(g['down']) for g in g_ffn[1:]]}

    make_rider = (lambda d_down0: comm.swap_rider(late_direct_grads(None, d_down0))) if comm else None
    dh, g_ffn[0], d_out_bias[0], swapped = _ffn_bwd(dh, ffn[0], saved[0][1], bias_zero, make_rider)
    late_direct = late_direct_grads(g_ffn[0]['up'], g_ffn[0]['down'])
    partials = comm.early_partials(late_direct, swapped) if comm else []
    dh, g_mix[0], received = _ssd_layer_bwd(dh, pa, saved[0][0], to_chips=partials)
    ga = g_mix[0]

    grads = {**late_direct,
        'norm_mix': jnp.concatenate([g['g'] for g in g_mix], axis=0),
        'norm_ffn': jnp.concatenate([g['g'] for g in g_ffn], axis=0),
        'norm_final': d_final.reshape(-1),
        'a_in_proj': jnp.concatenate([ga['w_z'], ga['w_xbc'], ga['w_dt'][:, :SSD_HEADS]], axis=1)[None],
        'a_conv_w': ga['cw'][None], 'a_conv_b': ga['cb'], 'a_dt_bias': ga['dtb'][:, :SSD_HEADS],
        'a_log': ga['a_log'][:, :SSD_HEADS], 'a_d_skip': ga['dsk'][:, :SSD_HEADS], 'a_norm': ga['norm'],
        'a_out_proj': rows_sharded(ga['out']),
        'b_pw1_b': gb['b1'], 'b_dw_w': gb['dw_w'][None], 'b_dw_b': gb['dw_b'],
        'b_ln_g': gb['ln_g'], 'b_ln_b': gb['ln_b'], 'b_pw2_b': d_out_bias[1],
        'c_in_b': gc['in_b'], 'c_conv_w': gc['cw'][None], 'c_conv_b': gc['cb'],
        'c_ga_w': gc['ga_w'].reshape(1, LRU_W // LRU_BLOCK, LRU_BLOCK, LRU_BLOCK),
        'c_ga_b': gc['ga_b'].reshape(1, LRU_W // LRU_BLOCK, LRU_BLOCK),
        'c_gx_w': gc['gx_w'].reshape(1, LRU_W // LRU_BLOCK, LRU_BLOCK, LRU_BLOCK),
        'c_gx_b': gc['gx_b'].reshape(1, LRU_W // LRU_BLOCK, LRU_BLOCK),
        'c_lambda': gc['lam'], 'c_out_b': d_out_bias[2],
        'd_in_b': gd['in_b'], 'd_ln_g': gd['ln_g'], 'd_ln_b': gd['ln_b'],
        'd_sp_w': gd['sp_w'].reshape(1, SGU_GROUPS, SGU_CHUNK, SGU_CHUNK), 'd_sp_b': gd['sp_bt'].T[None],
        'd_out_b': d_out_bias[3],
        'f_conv_w': jnp.stack([g['cw'] for g in g_ffn]),
        'f_conv_b': jnp.concatenate([g['cb'] for g in g_ffn], axis=0),
    }
    return loss, dh, grads, (partials, received)


def _global_shape(name, shard_shape):
    ax = SHARD_AXIS[name]
    if ax is None:
        return tuple(shard_shape)
    s = list(shard_shape)
    s[ax] *= N_CHIPS
    return tuple(s)


def _step(x, target, weights, moments_m, moments_v):
    x2, t2 = x[0], target[0]
    shard_shapes = {n: weights[n].shape for n in WEIGHTS}
    c_pos = lax.axis_index("c")
    k_pos = 2 * lax.axis_index("x") + lax.axis_index("y")
    c_idx = c_pos.astype(jnp.int32).reshape(1)
    k_idx = k_pos.astype(jnp.int32).reshape(1)

    def halves_of(a):
        a2 = _as2d(a)
        return a2.reshape(2, a2.shape[0] // 2, a2.shape[1])

    def view_direct(n, g):
        g = g.reshape((N_CHIPS,) + shard_shapes[n])
        if n in DIRECT_COLS:
            return g
        if n == 'f_down_w':
            return [g[:, i].reshape(-1, g.shape[-1]) for i in range(DEPTH)]
        return g.reshape(-1, g.shape[-1])

    def sibling_sums(grads):
        mine_g = [g.reshape(N_CHIPS, 2, g.shape[1] // 2, g.shape[2]) for g in grads]
        theirs = _swap_with_sibling(mine_g)
        return [_sum_with_sibling(g, t, c_idx) for g, t in zip(mine_g, theirs)]

    def flatten_direct(grads, names):
        out = []
        for n in names:
            out += grads[n] if isinstance(grads[n], list) else [grads[n]]
        return out

    first = [halves_of(weights[n].astype(bf16)) for n in EARLY_DIRECT]
    first.append(_pack([weights[n] for n in PACKED_MM], bf16).reshape(2, -1, FLAT_COLS))
    first.append(_pack([weights[n] for n in SHARDED_VEC], f32).reshape(2, -1, FLAT_COLS))
    gathered = _own_slot(_gather_weights(first), first, k_pos)
    w = {n: weights[n] for n in REPLICATED}
    for n, g in zip(EARLY_DIRECT, gathered):
        w[n] = view_direct(n, g)
    all_mm = _unpack(gathered[-2].reshape(N_CHIPS, -1), [shard_shapes[n] for n in PACKED_MM])
    all_vec = _unpack(gathered[-1].reshape(N_CHIPS, -1), [shard_shapes[n] for n in SHARDED_VEC])
    for n, st in zip(PACKED_MM + SHARDED_VEC, all_mm + all_vec):
        w[n] = _full_from_shards(st, SHARD_AXIS[n])

    def halved(g):
        return g.reshape(N_CHIPS, 2, g.shape[1] // 2, g.shape[2])

    class Comm:
        late_blocks = [halves_of(weights[n].astype(bf16)) for n in LATE_FIRST]
        second_blocks = [halves_of(weights[n].astype(bf16)) for n in LATE_SECOND]

        @staticmethod
        def late_weights(arrived):
            arrived = _own_slot(arrived, Comm.late_blocks, k_pos)
            return {n: view_direct(n, g) for n, g in zip(LATE_FIRST, arrived)}

        @staticmethod
        def second_rider():
            return _gather_rider(Comm.second_blocks)

        @staticmethod
        def second_weights(arrived):
            arrived = _own_slot(arrived, Comm.second_blocks, k_pos)
            return {n: view_direct(n, g) for n, g in zip(LATE_SECOND, arrived)}

        @staticmethod
        def swap_rider(grads):
            return _swap_rider([halved(g) for g in flatten_direct(grads, LATE_DIRECT) if g is not None])

        @staticmethod
        def early_partials(grads, swapped):
            mine_g = [halved(g) for g in flatten_direct(grads, LATE_DIRECT)]
            missing = LAST_SWAPPED_INDEX
            theirs = swapped[:missing] + list(_swap_with_sibling([mine_g[missing]])) + swapped[missing:]
            return [_sum_with_sibling(g, t, c_idx) for g, t in zip(mine_g, theirs)]

    loss_part, dx, grads, (partials, received) = _local_step(x2, t2, w, Comm)

    packed = [_shards_of(grads[n].reshape(_global_shape(n, shard_shapes[n])), SHARD_AXIS[n]).reshape(N_CHIPS, -1)
              for n in PACKED_MM + SHARDED_VEC]
    flat = jnp.concatenate(packed, axis=1)
    n_flat = flat.shape[1]
    n_pad = -(-n_flat // FLAT_QUANTUM) * FLAT_QUANTUM
    flat = jnp.pad(flat, ((0, 0), (0, n_pad - n_flat))).reshape(N_CHIPS, -1, FLAT_COLS)
    last_partials = sibling_sums(flatten_direct(grads, EARLY_DIRECT) + [flat])
    last_received = _send_to_chips(last_partials)
    partials, received = list(partials) + last_partials, list(received) + list(last_received)
    my_halves = [_sum_chips(p, r, k_idx) for p, r in zip(partials, received)]
    joined = _own_slot(_join_halves(my_halves), my_halves, c_pos)
    g_shard, pos = {}, 0
    for n in LATE_DIRECT + EARLY_DIRECT:
        layers = shard_shapes[n][0]
        g_shard[n] = jnp.stack([j.reshape(shard_shapes[n][1:]) for j in joined[pos:pos + layers]])
        pos += layers
    flat_shapes = [shard_shapes[n] for n in PACKED_MM + SHARDED_VEC]
    g_shard.update(zip(PACKED_MM + SHARDED_VEC, _unpack(joined[-1].reshape(-1), flat_shapes)))

    small = jnp.concatenate([grads[n].reshape(-1) for n in REPLICATED] + [loss_part.reshape(-1)[:1]])
    n_small = small.shape[0]
    n_small_pad = -(-n_small // (SUBLANE * FLAT_COLS)) * (SUBLANE * FLAT_COLS)
    small = jnp.pad(small, (0, n_small_pad - n_small)).reshape(-1, FLAT_COLS)
    small = _all_sum_small(small).reshape(-1)
    g_rep = dict(zip(REPLICATED, _unpack(small, [shard_shapes[n] for n in REPLICATED])))
    loss = small[n_small - 1]

    g_all = {**g_shard, **g_rep}
    delta, new_m, new_v = {}, {}, {}
    for n in WEIGHTS:
        delta[n], new_m[n], new_v[n] = _adamw(weights[n], g_all[n], moments_m[n], moments_v[n], "adamw_" + n)
    return loss, dx[None], g_all, delta, new_m, new_v


def kernel(x, norm_mix, norm_ffn, norm_final, a_in_proj, a_conv_w, a_conv_b, a_dt_bias, a_log, a_d_skip, a_norm, a_out_proj, b_pw1_w, b_pw1_b, b_dw_w, b_dw_b, b_ln_g, b_ln_b, b_pw2_w, b_pw2_b, c_in_w, c_in_b, c_conv_w, c_conv_b, c_ga_w, c_ga_b, c_gx_w, c_gx_b, c_lambda, c_out_w, c_out_b, d_in_w, d_in_b, d_ln_g, d_ln_b, d_sp_w, d_sp_b, d_out_w, d_out_b, f_up_w, f_conv_w, f_conv_b, f_down_w, loss_target, m_norm_mix, m_norm_ffn, m_norm_final, m_a_in_proj, m_a_conv_w, m_a_conv_b, m_a_dt_bias, m_a_log, m_a_d_skip, m_a_norm, m_a_out_proj, m_b_pw1_w, m_b_pw1_b, m_b_dw_w, m_b_dw_b, m_b_ln_g, m_b_ln_b, m_b_pw2_w, m_b_pw2_b, m_c_in_w, m_c_in_b, m_c_conv_w, m_c_conv_b, m_c_ga_w, m_c_ga_b, m_c_gx_w, m_c_gx_b, m_c_lambda, m_c_out_w, m_c_out_b, m_d_in_w, m_d_in_b, m_d_ln_g, m_d_ln_b, m_d_sp_w, m_d_sp_b, m_d_out_w, m_d_out_b, m_f_up_w, m_f_conv_w, m_f_conv_b, m_f_down_w, v_norm_mix, v_norm_ffn, v_norm_final, v_a_in_proj, v_a_conv_w, v_a_conv_b, v_a_dt_bias, v_a_log, v_a_d_skip, v_a_norm, v_a_out_proj, v_b_pw1_w, v_b_pw1_b, v_b_dw_w, v_b_dw_b, v_b_ln_g, v_b_ln_b, v_b_pw2_w, v_b_pw2_b, v_c_in_w, v_c_in_b, v_c_conv_w, v_c_conv_b, v_c_ga_w, v_c_ga_b, v_c_gx_w, v_c_gx_b, v_c_lambda, v_c_out_w, v_c_out_b, v_d_in_w, v_d_in_b, v_d_ln_g, v_d_ln_b, v_d_sp_w, v_d_sp_b, v_d_out_w, v_d_out_b, v_f_up_w, v_f_conv_w, v_f_conv_b, v_f_down_w):
    args = locals()
    weights = {n: args[n] for n in WEIGHTS}
    moments_m = {n: args['m_' + n] for n in WEIGHTS}
    moments_v = {n: args['v_' + n] for n in WEIGHTS}
    loss, dx, grad, delta, new_m, new_v = _step(x, loss_target, weights, moments_m, moments_v)
    return (loss, dx, *[grad[n] for n in WEIGHTS], *[delta[n] for n in WEIGHTS],
            *[new_m[n] for n in WEIGHTS], *[new_v[n] for n in WEIGHTS])
```

```python
import functools
import math

import jax
import jax.numpy as jnp
import numpy as np
from jax import lax
from jax.experimental import pallas as pl
from jax.experimental.pallas import tpu as pltpu

f32 = jnp.float32
bf16 = jnp.bfloat16
MESH = pl.DeviceIdType.MESH
HIGHEST = lax.Precision.HIGHEST

D_MODEL = 1024
DEPTH = 4
RMS_EPS = 1e-6
LN_EPS = 1e-5
SSD_D_INNER = 2048
SSD_HEADS = 32
SSD_BC = 1024
SSD_CONV_DIM = 4096
SSD_CHUNK = 128
SSD_GROUPS = 8
LRU_W = 1280
LRU_BLOCK = 256
LRU_C = 8.0
SGU_HALF = 2048
SGU_GROUPS = 8
SGU_CHUNK = 128
FFN_H = 2816
ADAM_LR, ADAM_B1, ADAM_B2, ADAM_EPS, ADAM_WD, ADAM_STEP = 0.001, 0.9, 0.999, 1e-08, 0.01, 10

LANE = 128
SUBLANE = 8
VMEM_LIMIT = 56 * 1024 * 1024
FLAT_COLS = 1024

WEIGHTS = ['norm_mix', 'norm_ffn', 'norm_final', 'a_in_proj', 'a_conv_w', 'a_conv_b', 'a_dt_bias', 'a_log',
           'a_d_skip', 'a_norm', 'a_out_proj', 'b_pw1_w', 'b_pw1_b', 'b_dw_w', 'b_dw_b', 'b_ln_g', 'b_ln_b',
           'b_pw2_w', 'b_pw2_b', 'c_in_w', 'c_in_b', 'c_conv_w', 'c_conv_b', 'c_ga_w', 'c_ga_b', 'c_gx_w',
           'c_gx_b', 'c_lambda', 'c_out_w', 'c_out_b', 'd_in_w', 'd_in_b', 'd_ln_g', 'd_ln_b', 'd_sp_w',
           'd_sp_b', 'd_out_w', 'd_out_b', 'f_up_w', 'f_conv_w', 'f_conv_b', 'f_down_w']
SHARD_AXIS = {
    'norm_mix': None, 'norm_ffn': None, 'norm_final': None, 'a_in_proj': 2, 'a_conv_w': 2, 'a_conv_b': None,
    'a_dt_bias': None, 'a_log': None, 'a_d_skip': None, 'a_norm': None, 'a_out_proj': 1, 'b_pw1_w': 2,
    'b_pw1_b': 1, 'b_dw_w': 2, 'b_dw_b': 1, 'b_ln_g': 1, 'b_ln_b': 1, 'b_pw2_w': 1, 'b_pw2_b': 1, 'c_in_w': 2,
    'c_in_b': 1, 'c_conv_w': 2, 'c_conv_b': 1, 'c_ga_w': 2, 'c_ga_b': 2, 'c_gx_w': 2, 'c_gx_b': 2,
    'c_lambda': 1, 'c_out_w': 1, 'c_out_b': 1, 'd_in_w': 2, 'd_in_b': 1, 'd_ln_g': 1, 'd_ln_b': 1,
    'd_sp_w': None, 'd_sp_b': None, 'd_out_w': 1, 'd_out_b': 1, 'f_up_w': 2, 'f_conv_w': 2, 'f_conv_b': None,
    'f_down_w': 1}
MATMUL_WEIGHTS = ['a_in_proj', 'a_out_proj', 'b_pw1_w', 'b_pw2_w', 'c_in_w', 'c_ga_w', 'c_gx_w', 'c_out_w',
                  'd_in_w', 'd_out_w', 'f_up_w', 'f_down_w']
DIRECT_COLS = ['b_pw1_w', 'c_in_w', 'd_in_w', 'f_up_w']
DIRECT_ROWS = ['a_out_proj', 'b_pw2_w', 'c_out_w', 'd_out_w', 'f_down_w']
DIRECT = DIRECT_COLS + DIRECT_ROWS
EARLY_DIRECT = ['a_out_proj']
LATE_DIRECT = [n for n in DIRECT if n not in EARLY_DIRECT]
LATE_SECOND = ['c_in_w', 'c_out_w', 'd_in_w', 'd_out_w']
LATE_FIRST = [n for n in LATE_DIRECT if n not in LATE_SECOND]
LAST_SWAPPED_INDEX = LATE_DIRECT.index('f_up_w')
PACKED_MM = [n for n in MATMUL_WEIGHTS if n not in DIRECT]
SHARDED = [n for n in WEIGHTS if SHARD_AXIS[n] is not None]
SHARDED_VEC = [n for n in SHARDED if n not in MATMUL_WEIGHTS]
REPLICATED = [n for n in WEIGHTS if SHARD_AXIS[n] is None]
N_CHIPS = 4
N_DEV = 8


def _tile(n, cap, mult):
    if n <= cap:
        return n
    t = (cap // mult) * mult
    while t >= mult:
        if n % t == 0:
            return t
        t -= mult
    raise ValueError(f"no tile for {n} under {cap} in steps of {mult}")


def _cparams(sem=None):
    if sem is None:
        return pltpu.CompilerParams(vmem_limit_bytes=VMEM_LIMIT)
    return pltpu.CompilerParams(dimension_semantics=sem, vmem_limit_bytes=VMEM_LIMIT)


def _dg(a, b, ca, cb):
    return lax.dot_general(a.astype(bf16), b.astype(bf16), (((ca,), (cb,)), ((), ())), preferred_element_type=f32)


@jax.custom_vjp
def _dot_nn(a, b):
    return _dg(a, b, 1, 0)


def _dot_nn_fwd(a, b):
    return _dg(a, b, 1, 0), (a, b)


def _dot_nn_bwd(res, g):
    a, b = res
    return _dg(g, b, 1, 1).astype(a.dtype), _dg(a, g, 0, 0).astype(b.dtype)


_dot_nn.defvjp(_dot_nn_fwd, _dot_nn_bwd)


@jax.custom_vjp
def _dot_nt(a, b):
    return _dg(a, b, 1, 1)


def _dot_nt_fwd(a, b):
    return _dg(a, b, 1, 1), (a, b)


def _dot_nt_bwd(res, g):
    a, b = res
    return _dg(g, b, 1, 0).astype(a.dtype), _dg(g, a, 0, 0).astype(b.dtype)


_dot_nt.defvjp(_dot_nt_fwd, _dot_nt_bwd)


@jax.custom_vjp
def _dot_tn(a, b):
    return _dg(a, b, 0, 0)


def _dot_tn_fwd(a, b):
    return _dg(a, b, 0, 0), (a, b)


def _dot_tn_bwd(res, g):
    a, b = res
    return _dg(b, g, 1, 1).astype(a.dtype), _dg(a, g, 1, 0).astype(b.dtype)


_dot_tn.defvjp(_dot_tn_fwd, _dot_tn_bwd)


def _expm1(x):
    small = jnp.abs(x) < 0.03
    xs = jnp.where(small, x, 0.0)
    series = xs * (1.0 + xs * (0.5 + xs * (1.0 / 6.0 + xs * (1.0 / 24.0 + xs * (1.0 / 120.0)))))
    return jnp.where(small, series, jnp.exp(x) - 1.0)


def _rms(x, g):
    return x * lax.rsqrt(jnp.mean(x * x, axis=-1, keepdims=True) + RMS_EPS) * g


def _layer_norm(x, g, b):
    mu = jnp.mean(x, axis=-1, keepdims=True)
    xc = x - mu
    return xc * lax.rsqrt(jnp.mean(xc * xc, axis=-1, keepdims=True) + LN_EPS) * g + b


def _causal_taps(ext, w, halo, rows):
    k_taps = w.shape[0]
    acc = None
    for k in range(k_taps):
        lo = halo - (k_taps - 1) + k
        term = w[k:k + 1, :] * ext[lo:lo + rows, :]
        acc = term if acc is None else acc + term
    return acc


def _mm(a, b, mode, name, *, bias=None, add=None, out_dtype=f32, tm_cap=1408, tn_cap=1408, tk_cap=1408,
        b_cols_sharded=False, b_layer=None, out_cols_sharded=False, rms_gain=None):
    shard_cols = None
    if b_cols_sharded:
        shard_cols = b.shape[-1]
        b_dims = (b.shape[-2], N_CHIPS * shard_cols)
    else:
        b_dims = b.shape
    if mode == 'nn':
        (m, k), (k2, n) = a.shape, b_dims
    elif mode == 'nt':
        (m, k), (n, k2) = a.shape, b_dims
    else:
        (k, m), (k2, n) = a.shape, b_dims
    assert k == k2, (name, a.shape, b.shape)
    tm = _tile(m, tm_cap, LANE if mode == 'tn' else SUBLANE)
    tn = _tile(n, tn_cap, LANE)
    tk = _tile(k, tk_cap, LANE if mode != 'tn' else SUBLANE)
    if b_cols_sharded and mode == 'nn':
        tn = shard_cols
    if b_cols_sharded and mode == 'nt':
        tk = shard_cols
    if out_cols_sharded:
        assert mode == 'tn' and n % N_CHIPS == 0
        tn = n // N_CHIPS
    nk = k // tk

    def shard_block(rows):
        lead = (None,) * (b.ndim - 2)
        return lead + (rows, shard_cols)

    def shard_index(shard, row_block):
        return (shard, row_block, 0) if b_layer is None else (shard, b_layer, row_block, 0)

    if mode == 'nn':
        a_spec = pl.BlockSpec((tm, tk), lambda i, j, kk: (i, kk))
        if b_cols_sharded:
            b_spec = pl.BlockSpec(shard_block(tk), lambda i, j, kk: shard_index(j, kk))
        else:
            b_spec = pl.BlockSpec((tk, tn), lambda i, j, kk: (kk, j))
        ca, cb = 1, 0
    elif mode == 'nt':
        a_spec = pl.BlockSpec((tm, tk), lambda i, j, kk: (i, kk))
        if b_cols_sharded:
            b_spec = pl.BlockSpec(shard_block(tn), lambda i, j, kk: shard_index(kk, j))
        else:
            b_spec = pl.BlockSpec((tn, tk), lambda i, j, kk: (j, kk))
        ca, cb = 1, 1
    else:
        a_spec = pl.BlockSpec((tk, tm), lambda i, j, kk: (kk, i))
        b_spec = pl.BlockSpec((tk, tn), lambda i, j, kk: (kk, j))
        ca, cb = 0, 0
    in_specs, operands = [a_spec, b_spec], [a, b]
    if bias is not None:
        in_specs.append(pl.BlockSpec((1, tn), lambda i, j, kk: (0, j)))
        operands.append(bias)
    if add is not None:
        in_specs.append(pl.BlockSpec((tm, tn), lambda i, j, kk: (i, j)))
        operands.append(add)
    if rms_gain is not None:
        in_specs.append(pl.BlockSpec((1, tn), lambda i, j, kk: (0, j)))
        operands.append(rms_gain)

    def body(*refs):
        a_ref, b_ref = refs[0], refs[1]
        pos = 2
        bias_ref = add_ref = None
        if bias is not None:
            bias_ref = refs[pos]
            pos += 1
        if add is not None:
            add_ref = refs[pos]
            pos += 1
        norm_ref = None
        if rms_gain is not None:
            norm_ref = refs[pos]
            pos += 1
        o_ref = refs[pos]
        normed_ref = refs[pos + 1] if rms_gain is not None else None
        acc_ref = refs[-1]
        kk = pl.program_id(2)

        @pl.when(kk == 0)
        def _():
            acc_ref[...] = jnp.zeros_like(acc_ref)

        acc_ref[...] += _dg(a_ref[...], b_ref[...], ca, cb)

        @pl.when(kk == nk - 1)
        def _():
            r = acc_ref[...]
            if bias_ref is not None:
                r = r + bias_ref[...]
            if add_ref is not None:
                r = r + add_ref[...].astype(f32)
            o_ref[...] = r.astype(out_dtype)
            if normed_ref is not None:
                normed_ref[...] = _rms(r, norm_ref[...]).astype(bf16)

    if out_cols_sharded:
        out_shape = jax.ShapeDtypeStruct((N_CHIPS, m, tn), out_dtype)
        out_spec = pl.BlockSpec((None, tm, tn), lambda i, j, kk: (j, i, 0))
    else:
        out_shape = jax.ShapeDtypeStruct((m, n), out_dtype)
        out_spec = pl.BlockSpec((tm, tn), lambda i, j, kk: (i, j))
    if rms_gain is not None:
        assert tn == n and not out_cols_sharded, "the norm needs whole rows in a tile"
        out_shape = [out_shape, jax.ShapeDtypeStruct((m, n), bf16)]
        out_spec = [out_spec, pl.BlockSpec((tm, tn), lambda i, j, kk: (i, j))]
    return pl.pallas_call(
        body, name=name, out_shape=out_shape,
        grid=(m // tm, n // tn, nk), in_specs=in_specs, out_specs=out_spec,
        scratch_shapes=[pltpu.VMEM((tm, tn), f32)],
        compiler_params=_cparams(("parallel", "parallel", "arbitrary")),
    )(*operands)


def _mm_normed(a, b, mode, name, *, rms_gain, **kw):
    if rms_gain is None:
        return _mm(a, b, mode, name, **kw), None
    return _mm(a, b, mode, name, rms_gain=rms_gain, **kw)


def _mm_w(a, w, mode, name, **kw):
    shards, layer = w
    return _mm(a, shards, mode, name, b_cols_sharded=True, b_layer=layer, **kw)


def _row_specs(tiles, halo_of, rows, halo, n_tiles, reverse):
    def tile_index(i):
        return n_tiles - 1 - i if reverse else i

    specs, operands = [], []
    for arr, has_halo in zip(tiles, halo_of):
        cols = arr.shape[1]
        specs.append(pl.BlockSpec((rows, cols), lambda i: (tile_index(i), 0)))
        operands.append(arr)
        if has_halo:
            per = rows // halo
            specs.append(pl.BlockSpec((halo, cols), lambda i: (jnp.maximum(tile_index(i) * per - 1, 0), 0)))
            operands.append(arr)
    return specs, operands, tile_index


def _load_tiles(refs, halo_of, tile_id, rows, halo):
    vals, pos = [], 0
    for has_halo in halo_of:
        cur = refs[pos][...].astype(f32)
        pos += 1
        if has_halo:
            before = refs[pos][...].astype(f32)
            pos += 1
            before = jnp.where(tile_id > 0, before, jnp.zeros_like(before))
            cur = jnp.concatenate([before, cur], axis=0)
        vals.append(cur)
    return vals, pos


def _valid_rows(tile_id, rows, halo):
    r = lax.broadcasted_iota(jnp.int32, (halo + rows, 1), 0)
    return jnp.logical_or(r >= halo, tile_id > 0).astype(f32)


def _row_fwd(f, tiles, params, outs, *, rows, name, halo=0, halo_of=None):
    t_len = tiles[0].shape[0]
    rows = min(rows, t_len)
    n_tiles = t_len // rows
    halo_of = halo_of or [False] * len(tiles)
    specs, operands, _ = _row_specs(tiles, halo_of, rows, halo, n_tiles, False)
    for p in params:
        specs.append(pl.BlockSpec(p.shape, lambda i: (0, 0)))
        operands.append(p)

    def body(*refs):
        i = pl.program_id(0)
        vals, pos = _load_tiles(refs, halo_of, i, rows, halo)
        pvals = [refs[pos + j][...] for j in range(len(params))]
        pos += len(params)
        kw = {'valid': _valid_rows(i, rows, halo)} if halo else {}
        res = f(*vals, *pvals, **kw)
        for o_ref, o in zip(refs[pos:], res):
            o_ref[...] = o.astype(o_ref.dtype)

    return pl.pallas_call(
        body, name=name,
        out_shape=[jax.ShapeDtypeStruct((t_len, c), d) for c, d in outs],
        grid=(n_tiles,), in_specs=specs,
        out_specs=[pl.BlockSpec((rows, c), lambda i: (i, 0)) for c, _ in outs],
        compiler_params=_cparams(("parallel",)),
    )(*operands)


def _row_bwd(f, tiles, params, cots, *, rows, name, halo=0, halo_of=None, tile_dtypes=None):
    t_len = tiles[0].shape[0]
    rows = min(rows, t_len)
    n_tiles = t_len // rows
    halo_of = halo_of or [False] * len(tiles)
    tile_dtypes = tile_dtypes or [f32] * len(tiles)
    specs, operands, tile_index = _row_specs(tiles, halo_of, rows, halo, n_tiles, True)
    for p in params:
        specs.append(pl.BlockSpec(p.shape, lambda i: (0, 0)))
        operands.append(p)
    for ct in cots:
        specs.append(pl.BlockSpec((rows, ct.shape[1]), lambda i: (tile_index(i), 0)))
        operands.append(ct)
    n_t, n_p, n_c = len(tiles), len(params), len(cots)
    out_shape = [jax.ShapeDtypeStruct(t.shape, d) for t, d in zip(tiles, tile_dtypes)]
    out_shape += [jax.ShapeDtypeStruct(p.shape, f32) for p in params]
    out_specs = [pl.BlockSpec((rows, t.shape[1]), lambda i: (tile_index(i), 0)) for t in tiles]
    out_specs += [pl.BlockSpec(p.shape, lambda i: (0, 0)) for p in params]
    scratch = [pltpu.VMEM((halo, t.shape[1]), f32) for t, h in zip(tiles, halo_of) if h]

    def body(*refs):
        i = pl.program_id(0)
        tile_id = tile_index(i)
        vals, pos = _load_tiles(refs, halo_of, tile_id, rows, halo)
        pvals = [refs[pos + j][...] for j in range(n_p)]
        pos += n_p
        cvals = [refs[pos + j][...].astype(f32) for j in range(n_c)]
        pos += n_c
        d_tile_refs = refs[pos:pos + n_t]
        d_param_refs = refs[pos + n_t:pos + n_t + n_p]
        carries = list(refs[pos + n_t + n_p:])
        kw = {'valid': _valid_rows(tile_id, rows, halo)} if halo else {}
        _, vjp = jax.vjp(lambda *args: tuple(f(*args, **kw)), *vals, *pvals)
        grads = vjp(tuple(cvals))

        @pl.when(i == 0)
        def _():
            for cr in carries:
                cr[...] = jnp.zeros_like(cr)
            for dp in d_param_refs:
                dp[...] = jnp.zeros_like(dp)

        ci = 0
        for t in range(n_t):
            g = grads[t]
            if halo_of[t]:
                cr = carries[ci]
                ci += 1
                d_tile_refs[t][0:rows - halo, :] = g[halo:rows, :].astype(d_tile_refs[t].dtype)
                d_tile_refs[t][rows - halo:rows, :] = (g[rows:rows + halo, :] + cr[...]).astype(d_tile_refs[t].dtype)
                cr[...] = g[0:halo, :]
            else:
                d_tile_refs[t][...] = g.astype(d_tile_refs[t].dtype)
        for j in range(n_p):
            d_param_refs[j][...] += grads[n_t + j]

    return pl.pallas_call(
        body, name=name, out_shape=out_shape, grid=(n_tiles,), in_specs=specs, out_specs=out_specs,
        scratch_shapes=scratch, compiler_params=_cparams(("arbitrary",)),
    )(*operands)


def _cols(base, c0, cs):
    return pl.ds(pl.multiple_of(base + c0, LANE), cs)


def _fold8(v):
    acc = v[0:SUBLANE, :]
    for m in range(1, v.shape[0] // SUBLANE):
        acc = acc + v[m * SUBLANE:(m + 1) * SUBLANE, :]
    return acc


class _ConvPlan:
    def __init__(self, x, w, b, *, in_bases, mid_bases, width, rows, rs, pre=None, pre_params=(), post=None):
        self.x, self.w, self.b = x, w, b
        self.in_bases, self.mid_bases, self.width = in_bases, mid_bases, width
        self.pre, self.pre_params, self.post = pre, list(pre_params), post
        self.k_taps = w.shape[0]
        tile_rows = SUBLANE * (4 // x.dtype.itemsize)
        self.halo = -(-(self.k_taps - 1) // tile_rows) * tile_rows
        self.t_len = x.shape[0]
        self.rows = min(rows, self.t_len)
        self.rs = rs
        self.n_tiles, self.n_rs, self.n_cs = self.t_len // self.rows, self.rows // rs, width // LANE
        self.n_mid = len(mid_bases)
        if pre is None:
            assert len(in_bases) == self.n_mid

    def in_specs(self, tile_index):
        cols = self.x.shape[1]
        per = self.rows // self.halo
        specs = [pl.BlockSpec((self.rows, cols), lambda i: (tile_index(i), 0)),
                 pl.BlockSpec((self.halo, cols), lambda i: (jnp.maximum(tile_index(i) * per - 1, 0), 0)),
                 pl.BlockSpec(self.w.shape, lambda i: (0, 0)), pl.BlockSpec(self.b.shape, lambda i: (0, 0))]
        operands = [self.x, self.x, self.w, self.b]
        for p, _ in self.pre_params:
            specs.append(pl.BlockSpec(p.shape, lambda i: (0, 0)))
            operands.append(p)
        return specs, operands

    def pre_strips(self, pp_refs, c0):
        return [p_ref[:, _cols(b, c0, LANE)] for (_, bases), p_ref in zip(self.pre_params, pp_refs) for b in bases]

    def fill_conv_input(self, cur_ref, before_ref, pp_refs, u_ref, tile_id):
        started = (tile_id > 0).astype(f32)

        def col_loop(c, carry):
            c0 = c * LANE
            pps = self.pre_strips(pp_refs, c0)
            xs = [before_ref[:, _cols(b, c0, LANE)].astype(f32) for b in self.in_bases]
            for j, u in enumerate(self.pre(*xs, *pps, valid=started)):
                u_ref[0:self.halo, _cols(j * self.width, c0, LANE)] = u
            for r in range(self.n_rs):
                xs = [cur_ref[r * self.rs:(r + 1) * self.rs, _cols(b, c0, LANE)].astype(f32) for b in self.in_bases]
                for j, u in enumerate(self.pre(*xs, *pps, valid=1.0)):
                    u_ref[self.halo + r * self.rs:self.halo + (r + 1) * self.rs, _cols(j * self.width, c0, LANE)] = u
            return carry

        lax.fori_loop(0, self.n_cs, col_loop, 0)

    def tap(self, cur_ref, before_ref, u_ref, tile_id, r, j, k, c0):
        lo = r * self.rs - (self.k_taps - 1) + k
        if u_ref is not None:
            return u_ref[self.halo + lo:self.halo + lo + self.rs, _cols(j * self.width, c0, LANE)]
        cols = _cols(self.in_bases[j], c0, LANE)
        if lo >= 0:
            return cur_ref[lo:lo + self.rs, cols].astype(f32)
        head = before_ref[self.halo + lo:self.halo, cols].astype(f32)
        head = jnp.where(tile_id > 0, head, 0.0)
        return jnp.concatenate([head, cur_ref[0:self.rs + lo, cols].astype(f32)], axis=0)

    def conv(self, cur_ref, before_ref, u_ref, w_ref, b_ref, tile_id, r, c0):
        hcs = []
        for j, mb in enumerate(self.mid_bases):
            cols = _cols(mb, c0, LANE)
            acc = b_ref[:, cols]
            for k in range(self.k_taps):
                acc = acc + w_ref[k:k + 1, cols] * self.tap(cur_ref, before_ref, u_ref, tile_id, r, j, k, c0)
            hcs.append(acc)
        return hcs


def _conv_fwd(plan, outs, name, rider=None):
    n_pp = len(plan.pre_params)
    n_in, n_out = 4 + n_pp, len(outs)
    r_n = rider.n if rider else 0
    specs, operands = plan.in_specs(lambda i: i)
    scratch = [pltpu.VMEM((plan.halo + plan.rows, plan.n_mid * plan.width), f32)] if plan.pre else []
    n_scr = len(scratch)

    def body(*refs):
        cur_ref, before_ref, w_ref, b_ref = refs[:4]
        pp_refs = refs[4:n_in]
        r_srcs = refs[n_in:n_in + r_n]
        o_refs = refs[n_in + r_n:n_in + r_n + n_out]
        r_outs = refs[n_in + r_n + n_out:n_in + 2 * r_n + n_out]
        scr = refs[n_in + 2 * r_n + n_out:]
        u_ref = scr[0] if plan.pre else None
        i = pl.program_id(0)
        if rider:
            @pl.when(i == 0)
            def _():
                rider.start(r_srcs, r_outs, scr[n_scr:])

        if plan.pre:
            plan.fill_conv_input(cur_ref, before_ref, pp_refs, u_ref, i)

        def col_loop(c, carry):
            c0 = c * LANE
            for r in range(plan.n_rs):
                res = plan.post(*plan.conv(cur_ref, before_ref, u_ref, w_ref, b_ref, i, r, c0))
                n = 0
                for (_, dt, bases), o_ref in zip(outs, o_refs):
                    for ob in bases:
                        o_ref[r * plan.rs:(r + 1) * plan.rs, _cols(ob, c0, LANE)] = res[n].astype(dt)
                        n += 1
            return carry

        lax.fori_loop(0, plan.n_cs, col_loop, 0)
        if rider:
            @pl.when(i == plan.n_tiles - 1)
            def _():
                rider.finish(r_srcs, r_outs, scr[n_scr:])

    return pl.pallas_call(
        body, name=name,
        out_shape=[jax.ShapeDtypeStruct((plan.t_len, c), d) for c, d, _ in outs] + (rider.out_shapes if rider else []),
        grid=(plan.n_tiles,), in_specs=specs + [HBM_SPEC] * r_n,
        out_specs=[pl.BlockSpec((plan.rows, c), lambda i: (i, 0)) for c, _, _ in outs] + [HBM_SPEC] * r_n,
        scratch_shapes=scratch + (rider.semaphores if rider else []),
        compiler_params=_cparams(("arbitrary",) if rider else ("parallel",)),
    )(*operands, *(rider.operands if rider else []))


def _conv_bwd(plan, cots, dx_dtype, name, rider=None):
    n_pp, n_c = len(plan.pre_params), len(cots)
    r_n = rider.n if rider else 0
    n_tiles, rows, rs, halo, k_taps = plan.n_tiles, plan.rows, plan.rs, plan.halo, plan.k_taps

    def tile_index(i):
        return n_tiles - 1 - i

    specs, operands = plan.in_specs(tile_index)
    for ct, _ in cots:
        specs.append(pl.BlockSpec((rows, ct.shape[1]), lambda i: (tile_index(i), 0)))
        operands.append(ct)
    mid_cols = plan.n_mid * plan.width
    out_shape = [jax.ShapeDtypeStruct(plan.x.shape, dx_dtype), jax.ShapeDtypeStruct(plan.w.shape, f32),
                 jax.ShapeDtypeStruct(plan.b.shape, f32)]
    out_shape += [jax.ShapeDtypeStruct(p.shape, f32) for p, _ in plan.pre_params]
    out_specs = [pl.BlockSpec((rows, plan.x.shape[1]), lambda i: (tile_index(i), 0)),
                 pl.BlockSpec(plan.w.shape, lambda i: (0, 0)), pl.BlockSpec(plan.b.shape, lambda i: (0, 0))]
    out_specs += [pl.BlockSpec(p.shape, lambda i: (0, 0)) for p, _ in plan.pre_params]
    w_cols = plan.w.shape[1]
    scratch = [pltpu.VMEM((rows + halo, mid_cols), f32),
               pltpu.VMEM((halo, mid_cols), f32),
               pltpu.VMEM(((k_taps + 1) * SUBLANE, w_cols), f32)]
    if plan.pre:
        scratch.append(pltpu.VMEM((halo + rows, mid_cols), f32))

    def body(*refs):
        cur_ref, before_ref, w_ref, b_ref = refs[:4]
        pp_refs = refs[4:4 + n_pp]
        c_refs = refs[4 + n_pp:4 + n_pp + n_c]
        pos = 4 + n_pp + n_c
        r_srcs = refs[pos:pos + r_n]
        pos += r_n
        dx_ref, dw_ref, db_ref = refs[pos:pos + 3]
        dpp_refs = refs[pos + 3:pos + 3 + n_pp]
        r_outs = refs[pos + 3 + n_pp:pos + 3 + n_pp + r_n]
        pos += 3 + n_pp + r_n
        g_ref, carry_ref, acc_ref = refs[pos:pos + 3]
        u_ref = refs[pos + 3] if plan.pre else None
        r_sems = refs[pos + (4 if plan.pre else 3):]
        i = pl.program_id(0)
        tile_id = tile_index(i)
        if rider:
            @pl.when(i == 0)
            def _():
                rider.start(r_srcs, r_outs, r_sems)

        @pl.when(i == 0)
        def _():
            carry_ref[...] = jnp.zeros_like(carry_ref)
            acc_ref[...] = jnp.zeros_like(acc_ref)
            for dp in dpp_refs:
                dp[...] = jnp.zeros_like(dp)

        g_ref[rows:rows + halo, :] = carry_ref[...]
        if plan.pre:
            plan.fill_conv_input(cur_ref, before_ref, pp_refs, u_ref, tile_id)

        def col_loop(c, carry):
            c0 = c * LANE
            for r in range(plan.n_rs):
                hcs = plan.conv(cur_ref, before_ref, u_ref, w_ref, b_ref, tile_id, r, c0)
                _, vjp = jax.vjp(lambda *a: tuple(plan.post(*a)), *hcs)
                cvals = [c_ref[r * rs:(r + 1) * rs, _cols(cb, c0, LANE)].astype(f32)
                         for (_, bases), c_ref in zip(cots, c_refs) for cb in bases]
                d_hcs = vjp(tuple(cvals))
                for j, mb in enumerate(plan.mid_bases):
                    g_ref[r * rs:(r + 1) * rs, _cols(j * plan.width, c0, LANE)] = d_hcs[j]
                    wc = _cols(mb, c0, LANE)
                    acc_ref[k_taps * SUBLANE:(k_taps + 1) * SUBLANE, wc] += _fold8(d_hcs[j])
                    for k in range(k_taps):
                        x_k = plan.tap(cur_ref, before_ref, u_ref, tile_id, r, j, k, c0)
                        acc_ref[k * SUBLANE:(k + 1) * SUBLANE, wc] += _fold8(d_hcs[j] * x_k)
            pps = plan.pre_strips(pp_refs, c0)
            for r in range(plan.n_rs):
                d_us = []
                for j, mb in enumerate(plan.mid_bases):
                    wc = _cols(mb, c0, LANE)
                    acc = None
                    for k in range(k_taps):
                        lo = r * rs + (k_taps - 1) - k
                        term = w_ref[k:k + 1, wc] * g_ref[lo:lo + rs, _cols(j * plan.width, c0, LANE)]
                        acc = term if acc is None else acc + term
                    d_us.append(acc)
                if plan.pre is None:
                    d_xs = d_us
                else:
                    xs = [cur_ref[r * rs:(r + 1) * rs, _cols(b, c0, LANE)].astype(f32) for b in plan.in_bases]
                    _, vjp_pre = jax.vjp(lambda *a: tuple(plan.pre(*a, valid=1.0)), *xs, *pps)
                    grads = vjp_pre(tuple(d_us))
                    d_xs = grads[:len(xs)]
                    n = len(xs)
                    for (_, bases), dp in zip(plan.pre_params, dpp_refs):
                        for pb in bases:
                            dp[:, _cols(pb, c0, LANE)] += grads[n]
                            n += 1
                for b, d_x in zip(plan.in_bases, d_xs):
                    dx_ref[r * rs:(r + 1) * rs, _cols(b, c0, LANE)] = d_x.astype(dx_dtype)
            return carry

        lax.fori_loop(0, plan.n_cs, col_loop, 0)
        carry_ref[...] = g_ref[0:halo, :]

        @pl.when(i == n_tiles - 1)
        def _():
            for k in range(k_taps):
                dw_ref[k:k + 1, :] = jnp.sum(acc_ref[k * SUBLANE:(k + 1) * SUBLANE, :], axis=0, keepdims=True)
            db_ref[...] = jnp.sum(acc_ref[k_taps * SUBLANE:(k_taps + 1) * SUBLANE, :], axis=0, keepdims=True)
            if rider:
                rider.finish(r_srcs, r_outs, r_sems)

    return pl.pallas_call(
        body, name=name, out_shape=out_shape + (rider.out_shapes if rider else []), grid=(n_tiles,),
        in_specs=specs + [HBM_SPEC] * r_n, out_specs=out_specs + [HBM_SPEC] * r_n,
        scratch_shapes=scratch + (rider.semaphores if rider else []), compiler_params=_cparams(("arbitrary",)),
    )(*operands, *(rider.operands if rider else []))


def _f_rms(h, g):
    return (_rms(h, g),)


def _f_rms_res(h, g, bz):
    hh = h + bz
    return _rms(hh, g), hh


@jax.custom_vjp
def _silu_gate(gate, val):
    return jax.nn.silu(gate) * val


def _silu_gate_fwd(gate, val):
    s = jax.nn.sigmoid(gate)
    return gate * s * val, (gate, val, s)


def _silu_gate_bwd(res, d):
    gate, val, s = res
    silu = gate * s
    return d * val * (s + silu * (1.0 - s)), d * silu


_silu_gate.defvjp(_silu_gate_fwd, _silu_gate_bwd)


def _post_ffn_gate(gate, val):
    return (_silu_gate(gate, val),)


def _post_silu(h):
    return (jax.nn.silu(h),)


def _post_identity(h):
    return (h,)


def _pre_glu(g_a, g_b, b_a, b_b, *, valid):
    return ((g_a + b_a) * jax.nn.sigmoid(g_b + b_b) * valid,)


def _f_ssd_dt(dtr, dtb):
    real = lax.broadcasted_iota(jnp.int32, (1, LANE), 1) < SSD_HEADS
    return (jnp.where(real, jax.nn.softplus(dtr + dtb), 0.0),)


def _f_ssd_post(y, z, g):
    return (_rms(y * jax.nn.silu(z), g),)


FFN_STRIP_ROWS = 128
CONF_STRIP_ROWS = 128
SSD_STRIP_ROWS = 128


def _f_ln_silu(x, g, b):
    return (jax.nn.silu(_layer_norm(x, g, b)),)


def _f_lru(io_ext, in_b, cw, cb, ga_w, ga_b, gx_w, gx_b, lam, *, valid):
    rows = io_ext.shape[0] - SUBLANE
    io = (io_ext + in_b) * valid
    gate = io[SUBLANE:, :LRU_W]
    xr = _causal_taps(io[:, LRU_W:], cw, SUBLANE, rows) + cb
    rs, iis = [], []
    for blk in range(LRU_W // LRU_BLOCK):
        sl = slice(blk * LRU_BLOCK, (blk + 1) * LRU_BLOCK)
        xb = xr[:, sl]
        rs.append(jax.nn.sigmoid(_dot_nn(xb, ga_w[sl, :]) + ga_b[:, sl]))
        iis.append(jax.nn.sigmoid(_dot_nn(xb, gx_w[sl, :]) + gx_b[:, sl]))
    r = jnp.concatenate(rs, axis=1)
    ig = jnp.concatenate(iis, axis=1)
    log_a = -LRU_C * r * jax.nn.softplus(-lam)
    a = jnp.exp(log_a)
    bterm = jnp.sqrt(-_expm1(2.0 * log_a)) * (ig * xr)
    return a, bterm, jax.nn.gelu(gate)


def _f_sgu(z, in_b, ln_g, ln_b, sp_w, sp_bt):
    rows = z.shape[0]
    zz = jax.nn.gelu(z + in_b)
    u, v = zz[:, :SGU_HALF], zz[:, SGU_HALF:]
    v = _layer_norm(v, ln_g, ln_b)
    tri = lax.broadcasted_iota(jnp.int32, (SGU_CHUNK, SGU_CHUNK), 0) >= lax.broadcasted_iota(
        jnp.int32, (SGU_CHUNK, SGU_CHUNK), 1)
    gdim = SGU_HALF // SGU_GROUPS
    row_blocks = []
    for ci in range(rows // SGU_CHUNK):
        col_blocks = []
        for g in range(SGU_GROUPS):
            w = jnp.where(tri, sp_w[g * SGU_CHUNK:(g + 1) * SGU_CHUNK, :], 0.0)
            vb = v[ci * SGU_CHUNK:(ci + 1) * SGU_CHUNK, g * gdim:(g + 1) * gdim]
            col_blocks.append(_dot_nn(w, vb) + sp_bt[:, g:g + 1])
        row_blocks.append(jnp.concatenate(col_blocks, axis=1))
    mixed = row_blocks[0] if len(row_blocks) == 1 else jnp.concatenate(row_blocks, axis=0)
    return (u * mixed,)


HEADS_PER_GROUP = 4
GROUP_COLS = 256
HEAD_DIM = 64


def _ssd_group(x, bm, cm, dt, st, a_log, dsk, g):
    q = x.shape[0]
    tri = lax.broadcasted_iota(jnp.int32, (q, q), 0) >= lax.broadcasted_iota(jnp.int32, (q, q), 1)
    d_a = dt * (-jnp.exp(a_log))
    acs = jnp.dot(tri.astype(f32), d_a, precision=HIGHEST, preferred_element_type=f32)
    acs_t = acs.T
    lane = lax.broadcasted_iota(jnp.int32, (1, LANE), 1)
    sub = lax.broadcasted_iota(jnp.int32, (LANE, 1), 0)
    col_idx = lax.broadcasted_iota(jnp.int32, (1, GROUP_COLS), 1)
    last_row = (lax.broadcasted_iota(jnp.int32, (q, 1), 0) == q - 1).astype(f32)
    cb = _dot_nt(cm, bm)
    y = jnp.zeros((q, GROUP_COLS), f32)
    e_in = jnp.zeros((q, GROUP_COLS), f32)
    d_end = jnp.zeros((q, GROUP_COLS), f32)
    d_last = jnp.zeros((1, GROUP_COLS), f32)
    d_skip = jnp.zeros((1, GROUP_COLS), f32)
    for j in range(HEADS_PER_GROUP):
        head = HEADS_PER_GROUP * g + j
        on_lane = (lane == head).astype(f32)
        on_sub = (sub == head).astype(f32)
        col = jnp.sum(acs * on_lane, axis=1, keepdims=True)
        row = jnp.sum(acs_t * on_sub, axis=0, keepdims=True)
        dtc = jnp.sum(dt * on_lane, axis=1, keepdims=True)
        last = jnp.sum(col * last_row, axis=0, keepdims=True)
        dsk_j = jnp.sum(dsk * on_lane, axis=1, keepdims=True)
        decay = jnp.where(tri, jnp.exp(jnp.where(tri, col - row, 0.0)), 0.0)
        mine = jnp.logical_and(col_idx >= j * HEAD_DIM, col_idx < (j + 1) * HEAD_DIM)
        y = y + _dot_nn(cb * decay, jnp.where(mine, x * dtc, 0.0))
        e_in = e_in + jnp.where(mine, jnp.exp(col), 0.0)
        d_end = d_end + jnp.where(mine, jnp.exp(last - col) * dtc, 0.0)
        d_last = d_last + jnp.where(mine, jnp.exp(last), 0.0)
        d_skip = d_skip + jnp.where(mine, dsk_j, 0.0)
    y = y + _dot_nn(cm, st) * e_in + x * d_skip
    st_new = st * d_last + _dot_tn(bm, x * d_end)
    return y, st_new


GROUPS_PER_STEP = 8


def _ssd_specs(rev, nc):
    def ch(c):
        return nc - 1 - c if rev else c

    gps = GROUPS_PER_STEP
    x_spec = pl.BlockSpec((SSD_CHUNK, gps * GROUP_COLS), lambda c, g: (ch(c), g))
    b_spec = pl.BlockSpec((SSD_CHUNK, gps * LANE), lambda c, g: (ch(c), SSD_D_INNER // (gps * LANE) + g))
    c_spec = pl.BlockSpec((SSD_CHUNK, gps * LANE), lambda c, g: (ch(c), (SSD_D_INNER + SSD_BC) // (gps * LANE) + g))
    dt_spec = pl.BlockSpec((SSD_CHUNK, LANE), lambda c, g: (ch(c), 0))
    row_spec = pl.BlockSpec((1, LANE), lambda c, g: (0, 0))
    st_spec = pl.BlockSpec((1, gps, LANE, GROUP_COLS), lambda c, g: (ch(c), g, 0, 0))
    wide_spec = pl.BlockSpec((SSD_CHUNK, SSD_CONV_DIM), lambda c, g: (ch(c), 0))
    return x_spec, b_spec, c_spec, dt_spec, row_spec, st_spec, wide_spec


def _ssd_fwd(xc, dt, a_log, dsk, gather=()):
    t_len = xc.shape[0]
    nc = t_len // SSD_CHUNK
    gps = GROUPS_PER_STEP
    n_gp = SSD_GROUPS // gps
    n_g = len(gather)
    x_spec, b_spec, c_spec, dt_spec, row_spec, st_spec, _ = _ssd_specs(False, nc)

    def body(*refs):
        x_ref, b_ref, c_ref, dt_ref, al_ref, dk_ref = refs[:6]
        g_srcs = refs[6:6 + n_g]
        y_ref, st_out_ref = refs[6 + n_g:8 + n_g]
        g_outs = refs[8 + n_g:8 + 2 * n_g]
        st_ref = refs[8 + 2 * n_g]
        g_sems = refs[9 + 2 * n_g:]
        c, gp = pl.program_id(0), pl.program_id(1)
        if n_g:
            @pl.when(jnp.logical_and(c == 0, gp == 0))
            def _():
                _gather_start(g_srcs, g_outs, *g_sems)

        for q in range(gps):
            g = gp * gps + q

            @pl.when(c == 0)
            def _():
                st_ref[g] = jnp.zeros((LANE, GROUP_COLS), f32)

            st = st_ref[g]
            st_out_ref[0, q] = st
            xq = slice(q * GROUP_COLS, (q + 1) * GROUP_COLS)
            bq = slice(q * LANE, (q + 1) * LANE)
            y, st_new = _ssd_group(x_ref[:, xq], b_ref[:, bq], c_ref[:, bq], dt_ref[...], st, al_ref[...],
                                   dk_ref[...], g)
            y_ref[:, xq] = y
            st_ref[g] = st_new

        if n_g:
            @pl.when(jnp.logical_and(c == nc - 1, gp == n_gp - 1))
            def _():
                _gather_finish(g_srcs, g_outs, *g_sems)

    res = pl.pallas_call(
        body, name="ssd_scan_fwd",
        out_shape=[jax.ShapeDtypeStruct((t_len, SSD_D_INNER), f32),
                   jax.ShapeDtypeStruct((nc, SSD_GROUPS, LANE, GROUP_COLS), f32)] + _gather_out_shapes(gather),
        grid=(nc, n_gp), in_specs=[x_spec, b_spec, c_spec, dt_spec, row_spec, row_spec] + [HBM_SPEC] * n_g,
        out_specs=[x_spec, st_spec] + [HBM_SPEC] * n_g,
        scratch_shapes=[pltpu.VMEM((SSD_GROUPS, LANE, GROUP_COLS), f32)] + (_gather_semaphores(n_g) if n_g else []),
        compiler_params=_cparams(("arbitrary", "arbitrary")),
    )(xc, xc, xc, dt, a_log, dsk, *gather)
    return res[0], res[1], list(res[2:])


def _ssd_bwd(xc, dt, a_log, dsk, states, dy, to_chips=()):
    t_len = xc.shape[0]
    nc = t_len // SSD_CHUNK
    gps = GROUPS_PER_STEP
    n_gp = SSD_GROUPS // gps
    n_s = len(to_chips)
    x_spec, b_spec, c_spec, dt_spec, row_spec, st_spec, wide_spec = _ssd_specs(True, nc)

    def body(*refs):
        x_ref, b_ref, c_ref, dt_ref, al_ref, dk_ref, st_in_ref, dy_ref = refs[:8]
        s_srcs = refs[8:8 + n_s]
        dxc_ref, ddt_ref, dal_ref, ddk_ref = refs[8 + n_s:12 + n_s]
        s_outs = refs[12 + n_s:12 + 2 * n_s]
        dst_ref = refs[12 + 2 * n_s]
        s_sems = refs[13 + 2 * n_s:]
        c, gp = pl.program_id(0), pl.program_id(1)

        @pl.when(jnp.logical_and(c == 0, gp == 0))
        def _():
            dal_ref[...] = jnp.zeros_like(dal_ref)
            ddk_ref[...] = jnp.zeros_like(ddk_ref)
            for cp in _to_chips_copies(s_srcs, s_outs, *s_sems) if n_s else []:
                cp.start()

        @pl.when(gp == 0)
        def _():
            ddt_ref[...] = jnp.zeros_like(ddt_ref)

        for q in range(gps):
            g = gp * gps + q

            @pl.when(c == 0)
            def _():
                dst_ref[g] = jnp.zeros((LANE, GROUP_COLS), f32)

            xq = slice(q * GROUP_COLS, (q + 1) * GROUP_COLS)
            bq = slice(q * LANE, (q + 1) * LANE)
            _, vjp = jax.vjp(lambda *args: _ssd_group(*args, g), x_ref[:, xq], b_ref[:, bq], c_ref[:, bq],
                             dt_ref[...], st_in_ref[0, q], al_ref[...], dk_ref[...])
            dx, db, dc, ddt, dst, dal, ddk = vjp((dy_ref[:, xq], dst_ref[g]))
            dxc_ref[:, pl.ds(pl.multiple_of(g * GROUP_COLS, GROUP_COLS), GROUP_COLS)] = dx
            dxc_ref[:, pl.ds(pl.multiple_of(SSD_D_INNER + g * LANE, LANE), LANE)] = db
            dxc_ref[:, pl.ds(pl.multiple_of(SSD_D_INNER + SSD_BC + g * LANE, LANE), LANE)] = dc
            ddt_ref[...] += ddt
            dst_ref[g] = dst
            dal_ref[...] += dal
            ddk_ref[...] += ddk

        if n_s:
            @pl.when(jnp.logical_and(c == nc - 1, gp == n_gp - 1))
            def _():
                for cp in _to_chips_copies(s_srcs, s_outs, *s_sems):
                    cp.wait()

    res = pl.pallas_call(
        body, name="ssd_scan_bwd",
        out_shape=[jax.ShapeDtypeStruct((t_len, SSD_CONV_DIM), f32), jax.ShapeDtypeStruct((t_len, LANE), f32),
                   jax.ShapeDtypeStruct((1, LANE), f32), jax.ShapeDtypeStruct((1, LANE), f32)]
        + _to_chips_out_shapes(to_chips),
        grid=(nc, n_gp),
        in_specs=[x_spec, b_spec, c_spec, dt_spec, row_spec, row_spec, st_spec, x_spec] + [HBM_SPEC] * n_s,
        out_specs=[wide_spec, dt_spec, row_spec, row_spec] + [HBM_SPEC] * n_s,
        scratch_shapes=[pltpu.VMEM((SSD_GROUPS, LANE, GROUP_COLS), f32)] + (_to_chips_semaphores(n_s) if n_s else []),
        compiler_params=_cparams(("arbitrary", "arbitrary")),
    )(xc, xc, xc, dt, a_log, dsk, states, dy, *to_chips)
    return res[0], res[1], res[2], res[3], list(res[4:])


LRU_ROWS = 256


def _lru_fwd(a, b, gg):
    t_len, cols = a.shape
    rows = min(LRU_ROWS, t_len)
    spec = pl.BlockSpec((rows, cols), lambda i: (i, 0))

    def body(a_ref, b_ref, g_ref, y_ref, h_ref, carry):
        i = pl.program_id(0)

        @pl.when(i == 0)
        def _():
            carry[...] = jnp.zeros_like(carry)

        av, bv = a_ref[...], b_ref[...]
        row = lax.broadcasted_iota(jnp.int32, av.shape, 0)
        s = 1
        while s < rows:
            a_prev = pltpu.roll(av, s, axis=0)
            b_prev = pltpu.roll(bv, s, axis=0)
            m = row >= s
            bv = jnp.where(m, av * b_prev + bv, bv)
            av = jnp.where(m, av * a_prev, av)
            s *= 2
        h = av * carry[0:1, :] + bv
        h_ref[...] = h
        y_ref[...] = g_ref[...] * h
        carry[0:1, :] = h[rows - 1:rows, :]

    return pl.pallas_call(
        body, name="lru_scan_fwd",
        out_shape=[jax.ShapeDtypeStruct((t_len, cols), f32), jax.ShapeDtypeStruct((t_len, cols), f32)],
        grid=(t_len // rows,), in_specs=[spec, spec, spec], out_specs=[spec, spec],
        scratch_shapes=[pltpu.VMEM((SUBLANE, cols), f32)],
        compiler_params=_cparams(("arbitrary",)),
    )(a, b, gg)


def _lru_bwd(dy, gg, a, h):
    t_len, cols = a.shape
    rows = min(LRU_ROWS, t_len)
    n_tiles = t_len // rows
    per = rows // SUBLANE
    spec = pl.BlockSpec((rows, cols), lambda i: (n_tiles - 1 - i, 0))
    prev_spec = pl.BlockSpec((SUBLANE, cols), lambda i: (jnp.maximum((n_tiles - 1 - i) * per - 1, 0), 0))

    def body(dy_ref, g_ref, a_ref, h_ref, hp_ref, da_ref, db_ref, dg_ref, carry_dh, carry_a):
        i = pl.program_id(0)
        tile_id = n_tiles - 1 - i

        @pl.when(i == 0)
        def _():
            carry_dh[...] = jnp.zeros_like(carry_dh)
            carry_a[...] = jnp.zeros_like(carry_a)

        av, hv, dyv = a_ref[...], h_ref[...], dy_ref[...]
        row = lax.broadcasted_iota(jnp.int32, av.shape, 0)
        dg_ref[...] = dyv * hv
        bv = dyv * g_ref[...]
        cv = jnp.where(row == rows - 1, carry_a[0:1, :], pltpu.roll(av, rows - 1, axis=0))
        s = 1
        while s < rows:
            c_next = pltpu.roll(cv, rows - s, axis=0)
            b_next = pltpu.roll(bv, rows - s, axis=0)
            m = row < rows - s
            bv = jnp.where(m, cv * b_next + bv, bv)
            cv = jnp.where(m, cv * c_next, cv)
            s *= 2
        dh = cv * carry_dh[0:1, :] + bv
        h_before = jnp.where(tile_id > 0, hp_ref[SUBLANE - 1:SUBLANE, :], jnp.zeros((1, cols), f32))
        h_prev = jnp.where(row == 0, h_before, pltpu.roll(hv, 1, axis=0))
        da_ref[...] = dh * h_prev
        db_ref[...] = dh
        carry_dh[0:1, :] = dh[0:1, :]
        carry_a[0:1, :] = av[0:1, :]

    return pl.pallas_call(
        body, name="lru_scan_bwd",
        out_shape=[jax.ShapeDtypeStruct((t_len, cols), f32)] * 3,
        grid=(n_tiles,), in_specs=[spec, spec, spec, spec, prev_spec], out_specs=[spec, spec, spec],
        scratch_shapes=[pltpu.VMEM((SUBLANE, cols), f32), pltpu.VMEM((SUBLANE, cols), f32)],
        compiler_params=_cparams(("arbitrary",)),
    )(dy, gg, a, h, h)


def _loss_head(h, target, g):
    t_len = h.shape[0]
    rows = min(512, t_len)

    def f(hv, gv, tv):
        err = _rms(hv, gv) - tv
        return 0.5 * jnp.sum(jnp.mean(err * err, axis=-1, keepdims=True), axis=0, keepdims=True)

    def body(h_ref, t_ref, g_ref, dh_ref, dg_ref, loss_ref):
        i = pl.program_id(0)

        @pl.when(i == 0)
        def _():
            dg_ref[...] = jnp.zeros_like(dg_ref)
            loss_ref[...] = jnp.zeros_like(loss_ref)

        tv = t_ref[...]
        part, vjp = jax.vjp(lambda hv, gv: f(hv, gv, tv), h_ref[...], g_ref[...])
        dh, dg = vjp(jnp.ones((1, 1), f32))
        dh_ref[...] = dh
        dg_ref[...] += dg
        loss_ref[...] += jnp.broadcast_to(part, loss_ref.shape)

    spec = pl.BlockSpec((rows, D_MODEL), lambda i: (i, 0))
    return pl.pallas_call(
        body, name="loss_head",
        out_shape=[jax.ShapeDtypeStruct((t_len, D_MODEL), f32), jax.ShapeDtypeStruct((1, D_MODEL), f32),
                   jax.ShapeDtypeStruct((1, LANE), f32)],
        grid=(t_len // rows,), in_specs=[spec, spec, pl.BlockSpec((1, D_MODEL), lambda i: (0, 0))],
        out_specs=[spec, pl.BlockSpec((1, D_MODEL), lambda i: (0, 0)), pl.BlockSpec((1, LANE), lambda i: (0, 0))],
        compiler_params=_cparams(("arbitrary",)),
    )(h, target, g)


def _as2d(a):
    return a.reshape((-1, a.shape[-1])) if a.ndim > 1 else a.reshape((1, -1))


def _row_block(rows, cols, bytes_cap=1 << 20):
    if rows * cols * 4 <= bytes_cap or rows % SUBLANE:
        return rows
    return _tile(rows, max(SUBLANE, (bytes_cap // (cols * 4)) // SUBLANE * SUBLANE), SUBLANE)


def _adamw(w, g, m, v, name):
    shape = w.shape
    w2, g2, m2, v2 = _as2d(w), _as2d(g), _as2d(m), _as2d(v)
    rows, cols = w2.shape
    rb = _row_block(rows, cols)

    def body(w_ref, g_ref, m_ref, v_ref, d_ref, nm_ref, nv_ref):
        gv = g_ref[...]
        nm = ADAM_B1 * m_ref[...] + (1.0 - ADAM_B1) * gv
        nv = ADAM_B2 * v_ref[...] + (1.0 - ADAM_B2) * jnp.square(gv)
        m_hat = nm / (1.0 - ADAM_B1 ** ADAM_STEP)
        v_hat = nv / (1.0 - ADAM_B2 ** ADAM_STEP)
        d_ref[...] = -ADAM_LR * (m_hat / (jnp.sqrt(v_hat) + ADAM_EPS) + ADAM_WD * w_ref[...])
        nm_ref[...] = nm
        nv_ref[...] = nv

    spec = pl.BlockSpec((rb, cols), lambda i: (i, 0))
    d, nm, nv = pl.pallas_call(
        body, name=name, out_shape=[jax.ShapeDtypeStruct((rows, cols), f32)] * 3,
        grid=(rows // rb,), in_specs=[spec] * 4, out_specs=[spec] * 3,
        compiler_params=_cparams(("parallel",)),
    )(w2, g2, m2, v2)
    return d.reshape(shape), nm.reshape(shape), nv.reshape(shape)


def _sum_with_sibling(g_halves, theirs, c_idx):
    n_sh, _, rows, cols = g_halves.shape
    rb = _tile(rows, 512, 2 * SUBLANE)

    def body(c_ref, mine_ref, theirs_ref, o_ref):
        o_ref[...] = (mine_ref[...] + theirs_ref[...]).astype(bf16)

    grid_spec = pltpu.PrefetchScalarGridSpec(
        num_scalar_prefetch=1, grid=(n_sh, rows // rb),
        in_specs=[pl.BlockSpec((None, None, rb, cols), lambda k, i, c_ref: (k, c_ref[0], i, 0)),
                  pl.BlockSpec((None, rb, cols), lambda k, i, c_ref: (k, i, 0))],
        out_specs=pl.BlockSpec((None, rb, cols), lambda k, i, c_ref: (k, i, 0)))
    return pl.pallas_call(
        body, name="grad_sum_sibling", out_shape=jax.ShapeDtypeStruct((n_sh, rows, cols), bf16),
        grid_spec=grid_spec, compiler_params=_cparams(("parallel", "parallel")),
    )(c_idx, g_halves, theirs)


def _sum_chips(partial, received, k_idx):
    _, rows, cols = partial.shape
    rb = _tile(rows, 512, 2 * SUBLANE)

    def body(k_ref, mine_ref, r_ref, o_ref):
        acc = mine_ref[...].astype(f32)
        for j in range(N_CHIPS - 1):
            acc = acc + r_ref[j].astype(f32)
        o_ref[...] = acc

    grid_spec = pltpu.PrefetchScalarGridSpec(
        num_scalar_prefetch=1, grid=(rows // rb,),
        in_specs=[pl.BlockSpec((None, rb, cols), lambda i, k_ref: (k_ref[0], i, 0)),
                  pl.BlockSpec((N_CHIPS - 1, rb, cols), lambda i, k_ref: (0, i, 0))],
        out_specs=pl.BlockSpec((rb, cols), lambda i, k_ref: (i, 0)))
    return pl.pallas_call(
        body, name="grad_sum_chips", out_shape=jax.ShapeDtypeStruct((rows, cols), f32),
        grid_spec=grid_spec, compiler_params=_cparams(("parallel",)),
    )(k_idx, partial, received)


HBM_SPEC = pl.BlockSpec(memory_space=pltpu.HBM)
CHIP_FLIPS = ((0, 1), (1, 0), (1, 1))


def _position():
    return lax.axis_index("x"), lax.axis_index("y"), lax.axis_index("c")


def _own_slot(gathered, mine, index):
    return [lax.dynamic_update_index_in_dim(g, m, index, 0) for g, m in zip(gathered, mine)]


def _gather_weights(blocks):
    n = len(blocks)

    def body(*refs):
        srcs, outs = refs[:n], refs[n:2 * n]
        send_sems, recv_sems = refs[2 * n:]
        _gather_start(srcs, outs, send_sems, recv_sems)
        _gather_finish(srcs, outs, send_sems, recv_sems)

    return pl.pallas_call(
        body, name="gather_weights", out_shape=_gather_out_shapes(blocks),
        in_specs=[HBM_SPEC] * n, out_specs=[HBM_SPEC] * n, scratch_shapes=_gather_semaphores(n),
    )(*blocks)


def _gather_out_shapes(blocks):
    return [jax.ShapeDtypeStruct((N_CHIPS,) + b.shape, b.dtype) for b in blocks]


def _gather_semaphores(n):
    n_sem = 2 * len(CHIP_FLIPS) * n
    return [pltpu.SemaphoreType.DMA((n_sem,)), pltpu.SemaphoreType.DMA((n_sem,))]


def _gather_copies(srcs, outs, send_sems, recv_sems):
    n_far = len(CHIP_FLIPS)
    x, y, c = _position()
    k = 2 * x + y
    first, passed = [], []
    for a in range(len(srcs)):
        for j, (fx, fy) in enumerate(CHIP_FLIPS):
            s = a * 2 * n_far + j
            kk = 2 * (x ^ fx) + (y ^ fy)
            first.append(pltpu.make_async_remote_copy(
                src_ref=srcs[a].at[c], dst_ref=outs[a].at[k, c], send_sem=send_sems.at[s],
                recv_sem=recv_sems.at[s], device_id=(x ^ fx, y ^ fy, c), device_id_type=MESH))
            passed.append(pltpu.make_async_remote_copy(
                src_ref=outs[a].at[kk, c], dst_ref=outs[a].at[kk, c], send_sem=send_sems.at[s + n_far],
                recv_sem=recv_sems.at[s + n_far], device_id=(x, y, 1 - c), device_id_type=MESH))
    return first, passed


def _gather_start(srcs, outs, send_sems, recv_sems):
    first, _ = _gather_copies(srcs, outs, send_sems, recv_sems)
    for cp in first:
        cp.start()


def _gather_finish(srcs, outs, send_sems, recv_sems):
    first, passed = _gather_copies(srcs, outs, send_sems, recv_sems)
    for arrived, onward in zip(first, passed):
        arrived.wait_recv()
        onward.start()
    for cp in passed:
        cp.wait_recv()
    for cp in first + passed:
        cp.wait_send()


def _swap_with_sibling(grads):
    rider = _swap_rider(grads)
    n = len(grads)

    def body(*refs):
        rider.start(refs[:n], refs[n:2 * n], refs[2 * n:])
        rider.finish(refs[:n], refs[n:2 * n], refs[2 * n:])

    return pl.pallas_call(
        body, name="grad_swap_sibling", out_shape=rider.out_shapes,
        in_specs=[HBM_SPEC] * n, out_specs=[HBM_SPEC] * n, scratch_shapes=rider.semaphores,
    )(*grads)


class _Rider:
    def __init__(self, operands, out_shapes, semaphores, start, finish):
        self.operands, self.out_shapes, self.semaphores = list(operands), list(out_shapes), list(semaphores)
        self.start, self.finish = start, finish
        self.n = len(self.operands)


def _swap_copies(srcs, outs, send_sems, recv_sems):
    x, y, c = _position()
    copies = []
    for a in range(len(srcs)):
        for kk in range(N_CHIPS):
            s = a * N_CHIPS + kk
            copies.append(pltpu.make_async_remote_copy(
                src_ref=srcs[a].at[kk, 1 - c], dst_ref=outs[a].at[kk], send_sem=send_sems.at[s],
                recv_sem=recv_sems.at[s], device_id=(x, y, 1 - c), device_id_type=MESH))
    return copies


def _swap_rider(grads):
    n_sem = N_CHIPS * len(grads)

    def start(srcs, outs, sems):
        for cp in _swap_copies(srcs, outs, *sems):
            cp.start()

    def finish(srcs, outs, sems):
        for cp in _swap_copies(srcs, outs, *sems):
            cp.wait()

    return _Rider(grads, [jax.ShapeDtypeStruct((N_CHIPS,) + g.shape[2:], g.dtype) for g in grads],
                  [pltpu.SemaphoreType.DMA((n_sem,)), pltpu.SemaphoreType.DMA((n_sem,))], start, finish)


def _gather_rider(blocks):
    def start(srcs, outs, sems):
        _gather_start(srcs, outs, *sems)

    def finish(srcs, outs, sems):
        _gather_finish(srcs, outs, *sems)

    return _Rider(blocks, _gather_out_shapes(blocks), _gather_semaphores(len(blocks)), start, finish)


def _send_to_chips(partials):
    n = len(partials)

    def body(*refs):
        srcs, outs = refs[:n], refs[n:2 * n]
        send_sems, recv_sems = refs[2 * n:]
        for cp in _to_chips_copies(srcs, outs, send_sems, recv_sems):
            cp.start()
        for cp in _to_chips_copies(srcs, outs, send_sems, recv_sems):
            cp.wait()

    return pl.pallas_call(
        body, name="grad_to_chips", out_shape=_to_chips_out_shapes(partials),
        in_specs=[HBM_SPEC] * n, out_specs=[HBM_SPEC] * n, scratch_shapes=_to_chips_semaphores(n),
    )(*partials)


def _to_chips_out_shapes(partials):
    return [jax.ShapeDtypeStruct((len(CHIP_FLIPS),) + p.shape[1:], p.dtype) for p in partials]


def _to_chips_semaphores(n):
    n_sem = len(CHIP_FLIPS) * n
    return [pltpu.SemaphoreType.DMA((n_sem,)), pltpu.SemaphoreType.DMA((n_sem,))]


def _to_chips_copies(srcs, outs, send_sems, recv_sems):
    n_far = len(CHIP_FLIPS)
    x, y, c = _position()
    copies = []
    for a in range(len(srcs)):
        for j, (fx, fy) in enumerate(CHIP_FLIPS):
            s = a * n_far + j
            kk = 2 * (x ^ fx) + (y ^ fy)
            copies.append(pltpu.make_async_remote_copy(
                src_ref=srcs[a].at[kk], dst_ref=outs[a].at[j], send_sem=send_sems.at[s],
                recv_sem=recv_sems.at[s], device_id=(x ^ fx, y ^ fy, c), device_id_type=MESH))
    return copies


def _join_halves(halves):
    n = len(halves)

    def body(*refs):
        srcs, outs = refs[:n], refs[n:2 * n]
        send_sems, recv_sems = refs[2 * n:]
        x, y, c = _position()
        copies = []
        for a in range(n):
            cp = pltpu.make_async_remote_copy(
                src_ref=srcs[a], dst_ref=outs[a].at[c], send_sem=send_sems.at[a], recv_sem=recv_sems.at[a],
                device_id=(x, y, 1 - c), device_id_type=MESH)
            cp.start()
            copies.append(cp)
        for cp in copies:
            cp.wait()

    return pl.pallas_call(
        body, name="grad_join_halves",
        out_shape=[jax.ShapeDtypeStruct((2,) + h.shape, h.dtype) for h in halves],
        in_specs=[HBM_SPEC] * n, out_specs=[HBM_SPEC] * n,
        scratch_shapes=[pltpu.SemaphoreType.DMA((n,)), pltpu.SemaphoreType.DMA((n,))],
    )(*halves)


def _all_sum_small(vec):
    rows, cols = vec.shape
    n_far = len(CHIP_FLIPS)

    def body(v_ref, o_ref, buf, send_sems, recv_sems):
        x, y, c = _position()
        me = 4 * x + 2 * y + c
        sibling = (x, y, 1 - c)
        buf[me] = v_ref[...]
        own = pltpu.make_async_remote_copy(
            src_ref=v_ref, dst_ref=buf.at[me], send_sem=send_sems.at[2 * n_far], recv_sem=recv_sems.at[2 * n_far],
            device_id=sibling, device_id_type=MESH)
        own.start()
        first, passed = [], []
        for j, (fx, fy) in enumerate(CHIP_FLIPS):
            sender = 4 * (x ^ fx) + 2 * (y ^ fy) + c
            first.append(pltpu.make_async_remote_copy(
                src_ref=v_ref, dst_ref=buf.at[me], send_sem=send_sems.at[j], recv_sem=recv_sems.at[j],
                device_id=(x ^ fx, y ^ fy, c), device_id_type=MESH))
            passed.append(pltpu.make_async_remote_copy(
                src_ref=buf.at[sender], dst_ref=buf.at[sender], send_sem=send_sems.at[n_far + j],
                recv_sem=recv_sems.at[n_far + j], device_id=sibling, device_id_type=MESH))
        for cp in first:
            cp.start()
        for arrived, onward in zip(first, passed):
            arrived.wait_recv()
            onward.start()
        for cp in passed + [own]:
            cp.wait_recv()
        for cp in first + passed + [own]:
            cp.wait_send()
        acc = buf[0]
        for d in range(1, N_DEV):
            acc = acc + buf[d]
        o_ref[...] = acc

    return pl.pallas_call(
        body, name="all_sum_small", out_shape=jax.ShapeDtypeStruct((rows, cols), f32),
        in_specs=[pl.BlockSpec(memory_space=pltpu.VMEM)], out_specs=pl.BlockSpec(memory_space=pltpu.VMEM),
        scratch_shapes=[pltpu.VMEM((N_DEV, rows, cols), f32), pltpu.SemaphoreType.DMA((N_DEV - 1,)),
                        pltpu.SemaphoreType.DMA((N_DEV - 1,))],
        compiler_params=_cparams(),
    )(vec)


FLAT_QUANTUM = 2 * 2 * SUBLANE * FLAT_COLS


def _pack(arrays, dtype):
    flat = jnp.concatenate([a.astype(dtype).reshape(-1) for a in arrays])
    n = flat.shape[0]
    n_pad = -(-n // FLAT_QUANTUM) * FLAT_QUANTUM
    return jnp.pad(flat, (0, n_pad - n))


def _unpack(flat, shapes):
    out, off = [], 0
    for s in shapes:
        n = int(np.prod(s))
        out.append(flat[..., off:off + n].reshape(flat.shape[:-1] + tuple(s)))
        off += n
    return out


def _full_from_shards(stacked, axis):
    return jnp.concatenate([stacked[k] for k in range(N_CHIPS)], axis=axis)


def _shards_of(full, axis):
    return jnp.stack(jnp.split(full, N_CHIPS, axis=axis))


def _ffn_fwd(h, u, p, next_gain):
    a = _mm_w(u, p['up'], 'nn', "ffn_up")
    gated = _conv_fwd(_ffn_conv_plan(a, p), [(FFN_H, bf16, (0,))], "ffn_gate")[0]
    h_out, u_next = _mm_normed(gated, p['down'], 'nn', "ffn_down", add=h, rms_gain=next_gain)
    return h_out, u_next, (h, u, a, gated)


def _ffn_conv_plan(a, p):
    both = (0, FFN_H)
    return _ConvPlan(a, p['cw'], p['cb'], in_bases=both, mid_bases=both, width=FFN_H, rows=256, rs=FFN_STRIP_ROWS,
                     post=_post_ffn_gate)


def _ffn_bwd(dh_out, p, saved, bias_zero, make_rider=None):
    h, u, a, gated = saved
    d_gated = _mm(dh_out, p['down'], 'nt', "ffn_down_dx", out_dtype=bf16)
    d_down = _mm(gated, dh_out, 'tn', "ffn_down_dw")
    rider = make_rider(d_down) if make_rider else None
    res = _conv_bwd(_ffn_conv_plan(a, p), [(d_gated, (0,))], bf16, "ffn_gate_bwd", rider=rider)
    da, d_cw, d_cb = res[:3]
    d_up = _mm(u, da, 'tn', "ffn_up_dw", out_cols_sharded=True)
    du = _mm_w(da, p['up'], 'nt', "ffn_up_dx", out_dtype=bf16)
    dh, d_g, d_bias = _row_bwd(_f_rms_res, [h], [p['g'], bias_zero], [du, dh_out], rows=512, name="ffn_norm_bwd")
    return dh, {'g': d_g, 'up': d_up, 'down': d_down, 'cw': d_cw, 'cb': d_cb}, d_bias, list(res[3:])


def _mixer_norm_bwd(h, g, du, dh_res, name):
    def f(hv, gv):
        return _rms(hv, gv), hv

    dh, d_g = _row_bwd(f, [h], [g], [du, dh_res], rows=512, name=name)
    return dh, d_g


def _ssd_layer_fwd(h, u, p, next_gain, gather=()):
    z = _mm(u, p['w_z'], 'nn', "ssd_in_z")
    xbc = _mm(u, p['w_xbc'], 'nn', "ssd_in_xbc")
    dtr = _mm(u, p['w_dt'], 'nn', "ssd_in_dt")
    xc = _conv_fwd(_ssd_conv_plan(xbc, p), [(SSD_CONV_DIM, f32, (0,))], "ssd_conv")[0]
    dt = _row_fwd(_f_ssd_dt, [dtr], [p['dtb']], [(LANE, f32)], rows=1024, name="ssd_dt")[0]
    y, states, gathered = _ssd_fwd(xc, dt, p['a_log'], p['dsk'], gather)
    yn = _row_fwd(_f_ssd_post, [y, z], [p['norm']], [(SSD_D_INNER, bf16)], rows=256, name="ssd_gate_norm")[0]
    h_out, u_next = _mm_normed(yn, p['out'], 'nn', "ssd_out", add=h, rms_gain=next_gain)
    return h_out, u_next, (h, u, z, xbc, dtr, xc, dt, states, y, yn), gathered


def _ssd_conv_plan(xbc, p):
    return _ConvPlan(xbc, p['cw'], p['cb'], in_bases=(0,), mid_bases=(0,), width=SSD_CONV_DIM, rows=256,
                     rs=SSD_STRIP_ROWS, post=_post_silu)


def _ssd_layer_bwd(dh_out, p, saved, to_chips=()):
    h, u, z, xbc, dtr, xc, dt, states, y, yn = saved
    d_yn = _mm(dh_out, p['out'], 'nt', "ssd_out_dx", out_dtype=bf16)
    d_out = _mm(yn, dh_out, 'tn', "ssd_out_dw")
    dy, dz, d_norm = _row_bwd(_f_ssd_post, [y, z], [p['norm']], [d_yn], rows=256, name="ssd_gate_norm_bwd",
                              tile_dtypes=[f32, bf16])
    dxc, ddt, d_alog, d_dsk, received = _ssd_bwd(xc, dt, p['a_log'], p['dsk'], states, dy, to_chips)
    dxbc, d_cw, d_cb = _conv_bwd(_ssd_conv_plan(xbc, p), [(dxc, (0,))], bf16, "ssd_conv_bwd")
    ddtr, d_dtb = _row_bwd(_f_ssd_dt, [dtr], [p['dtb']], [ddt], rows=1024, name="ssd_dt_bwd", tile_dtypes=[bf16])
    d_wz = _mm(u, dz, 'tn', "ssd_in_z_dw")
    d_wxbc = _mm(u, dxbc, 'tn', "ssd_in_xbc_dw")
    d_wdt = _mm(u, ddtr, 'tn', "ssd_in_dt_dw")
    du = _mm(dz, p['w_z'], 'nt', "ssd_in_z_dx")
    du = _mm(dxbc, p['w_xbc'], 'nt', "ssd_in_xbc_dx", add=du)
    du = _mm(ddtr, p['w_dt'], 'nt', "ssd_in_dt_dx", add=du, out_dtype=bf16)
    dh, d_g = _mixer_norm_bwd(h, p['g'], du, dh_out, "ssd_norm_bwd")
    grads = {'g': d_g, 'w_z': d_wz, 'w_xbc': d_wxbc, 'w_dt': d_wdt, 'cw': d_cw, 'cb': d_cb, 'dtb': d_dtb,
             'a_log': d_alog, 'dsk': d_dsk, 'norm': d_norm, 'out': d_out}
    return dh, grads, received


def _conf_layer_fwd(h, u, p, next_gain, rider=None):
    g2 = _mm_w(u, p['pw1'], 'nn', "conf_pw1")
    res = _conv_fwd(_conf_conv_plan(g2, p), [(D_MODEL, f32, (0,))], "conf_conv", rider=rider)
    conv = res[0]
    s = _row_fwd(_f_ln_silu, [conv], [p['ln_g'], p['ln_b']], [(D_MODEL, bf16)], rows=256, name="conf_ln")[0]
    h_out, u_next = _mm_normed(s, p['pw2'], 'nn', "conf_pw2", bias=p['b2'], add=h, rms_gain=next_gain)
    return h_out, u_next, (h, u, g2, conv, s), list(res[1:])


def _conf_conv_plan(g2, p):
    halves = (0, D_MODEL)
    return _ConvPlan(g2, p['dw_w'], p['dw_b'], in_bases=halves, mid_bases=(0,), width=D_MODEL, rows=256,
                     rs=CONF_STRIP_ROWS, pre=_pre_glu, pre_params=[(p['b1'], halves)], post=_post_identity)


def _conf_layer_bwd(dh_out, p, saved):
    h, u, g2, conv, s = saved
    ds = _mm(dh_out, p['pw2'], 'nt', "conf_pw2_dx", out_dtype=bf16)
    d_pw2 = _mm(s, dh_out, 'tn', "conf_pw2_dw")
    d_conv, d_lng, d_lnb = _row_bwd(_f_ln_silu, [conv], [p['ln_g'], p['ln_b']], [ds], rows=256, name="conf_ln_bwd")
    dg2, d_dww, d_dwb, d_b1 = _conv_bwd(_conf_conv_plan(g2, p), [(d_conv, (0,))], bf16, "conf_conv_bwd")
    d_pw1 = _mm(u, dg2, 'tn', "conf_pw1_dw", out_cols_sharded=True)
    du = _mm_w(dg2, p['pw1'], 'nt', "conf_pw1_dx", out_dtype=bf16)
    dh, d_g = _mixer_norm_bwd(h, p['g'], du, dh_out, "conf_norm_bwd")
    grads = {'g': d_g, 'pw1': d_pw1, 'b1': d_b1, 'dw_w': d_dww, 'dw_b': d_dwb, 'ln_g': d_lng, 'ln_b': d_lnb,
             'pw2': d_pw2}
    return dh, grads


def _lru_params(p):
    return [p['in_b'], p['cw'], p['cb'], p['ga_w'], p['ga_b'], p['gx_w'], p['gx_b'], p['lam']]


def _lru_layer_fwd(h, u, p, next_gain):
    io = _mm_w(u, p['in_w'], 'nn', "lru_in")
    a, b, gg = _row_fwd(_f_lru, [io], _lru_params(p), [(LRU_W, f32)] * 3, rows=256, name="lru_gates",
                        halo=SUBLANE, halo_of=[True])
    y, hs = _lru_fwd(a, b, gg)
    h_out, u_next = _mm_normed(y, p['out'], 'nn', "lru_out", bias=p['out_b'], add=h, rms_gain=next_gain)
    return h_out, u_next, (h, u, io, a, gg, hs, y)


def _lru_layer_bwd(dh_out, p, saved):
    h, u, io, a, gg, hs, y = saved
    dy = _mm(dh_out, p['out'], 'nt', "lru_out_dx")
    d_out = _mm(y, dh_out, 'tn', "lru_out_dw")
    da, db, dgg = _lru_bwd(dy, gg, a, hs)
    res = _row_bwd(_f_lru, [io], _lru_params(p), [da, db, dgg], rows=256, name="lru_gates_bwd",
                   halo=SUBLANE, halo_of=[True], tile_dtypes=[bf16])
    dio, d_inb, d_cw, d_cb, d_gaw, d_gab, d_gxw, d_gxb, d_lam = res
    d_inw = _mm(u, dio, 'tn', "lru_in_dw", out_cols_sharded=True)
    du = _mm_w(dio, p['in_w'], 'nt', "lru_in_dx", out_dtype=bf16)
    dh, d_g = _mixer_norm_bwd(h, p['g'], du, dh_out, "lru_norm_bwd")
    grads = {'g': d_g, 'in_w': d_inw, 'in_b': d_inb, 'cw': d_cw, 'cb': d_cb, 'ga_w': d_gaw, 'ga_b': d_gab,
             'gx_w': d_gxw, 'gx_b': d_gxb, 'lam': d_lam, 'out': d_out}
    return dh, grads


def _sgu_params(p):
    return [p['in_b'], p['ln_g'], p['ln_b'], p['sp_w'], p['sp_bt']]


def _sgu_layer_fwd(h, u, p, next_gain):
    z = _mm_w(u, p['in_w'], 'nn', "sgu_in")
    s = _row_fwd(_f_sgu, [z], _sgu_params(p), [(SGU_HALF, bf16)], rows=SGU_CHUNK, name="sgu_mix")[0]
    h_out, u_next = _mm_normed(s, p['out'], 'nn', "sgu_out", bias=p['out_b'], add=h, rms_gain=next_gain)
    return h_out, u_next, (h, u, z, s)


def _sgu_layer_bwd(dh_out, p, saved):
    h, u, z, s = saved
    ds = _mm(dh_out, p['out'], 'nt', "sgu_out_dx", out_dtype=bf16)
    d_out = _mm(s, dh_out, 'tn', "sgu_out_dw")
    dz, d_inb, d_lng, d_lnb, d_spw, d_spbt = _row_bwd(_f_sgu, [z], _sgu_params(p), [ds], rows=SGU_CHUNK,
                                                      name="sgu_mix_bwd", tile_dtypes=[bf16])
    d_inw = _mm(u, dz, 'tn', "sgu_in_dw", out_cols_sharded=True)
    du = _mm_w(dz, p['in_w'], 'nt', "sgu_in_dx", out_dtype=bf16)
    dh, d_g = _mixer_norm_bwd(h, p['g'], du, dh_out, "sgu_norm_bwd")
    grads = {'g': d_g, 'in_w': d_inw, 'in_b': d_inb, 'ln_g': d_lng, 'ln_b': d_lnb, 'sp_w': d_spw, 'sp_bt': d_spbt,
             'out': d_out}
    return dh, grads


def _row(v):
    return v.reshape((1, -1)).astype(f32)


def _pad_lanes(v, n=LANE):
    v = _row(v)
    return jnp.pad(v, ((0, 0), (0, n - v.shape[1])))


def _local_step(x, target, w, comm=None):
    a_in = w['a_in_proj'][0]
    pa = {'g': _row(w['norm_mix'][0]), 'w_z': a_in[:, :SSD_D_INNER],
          'w_xbc': a_in[:, SSD_D_INNER:SSD_D_INNER + SSD_CONV_DIM],
          'w_dt': jnp.pad(a_in[:, SSD_D_INNER + SSD_CONV_DIM:], ((0, 0), (0, LANE - SSD_HEADS))),
          'cw': w['a_conv_w'][0].astype(f32), 'cb': _row(w['a_conv_b'][0]), 'dtb': _pad_lanes(w['a_dt_bias'][0]),
          'a_log': _pad_lanes(w['a_log'][0]), 'dsk': _pad_lanes(w['a_d_skip'][0]), 'norm': _row(w['a_norm'][0]),
          'out': w['a_out_proj']}
    mix_gain = [_row(w['norm_mix'][i]) for i in range(DEPTH)] + [None]
    ffn_gain = [_row(w['norm_ffn'][i]) for i in range(DEPTH)]
    u = _row_fwd(_f_rms, [x], [mix_gain[0]], [(D_MODEL, bf16)], rows=512, name="first_norm")[0]
    h, u, s_mix0, gathered = _ssd_layer_fwd(x, u, pa, ffn_gain[0], gather=comm.late_blocks if comm else ())
    if comm:
        w = {**w, **comm.late_weights(gathered)}
    ffn = [{'g': _row(w['norm_ffn'][i]), 'up': (w['f_up_w'], i), 'down': w['f_down_w'][i],
            'cw': w['f_conv_w'][i].astype(f32), 'cb': _row(w['f_conv_b'][i])} for i in range(DEPTH)]
    pb = {'g': _row(w['norm_mix'][1]), 'pw1': (w['b_pw1_w'], 0), 'b1': _row(w['b_pw1_b'][0]),
          'dw_w': w['b_dw_w'][0].astype(f32), 'dw_b': _row(w['b_dw_b'][0]), 'ln_g': _row(w['b_ln_g'][0]),
          'ln_b': _row(w['b_ln_b'][0]), 'pw2': w['b_pw2_w'], 'b2': _row(w['b_pw2_b'][0])}
    h, u, s_ffn0 = _ffn_fwd(h, u, ffn[0], mix_gain[1])
    h, u, s_mix1, gathered = _conf_layer_fwd(h, u, pb, ffn_gain[1], rider=comm.second_rider() if comm else None)
    if comm:
        w = {**w, **comm.second_weights(gathered)}
    pc = {'g': _row(w['norm_mix'][2]), 'in_w': (w['c_in_w'], 0), 'in_b': _row(w['c_in_b'][0]),
          'cw': w['c_conv_w'][0].astype(f32), 'cb': _row(w['c_conv_b'][0]),
          'ga_w': w['c_ga_w'][0].reshape(LRU_W, LRU_BLOCK).astype(f32), 'ga_b': _row(w['c_ga_b'][0]),
          'gx_w': w['c_gx_w'][0].reshape(LRU_W, LRU_BLOCK).astype(f32), 'gx_b': _row(w['c_gx_b'][0]),
          'lam': _row(w['c_lambda'][0]), 'out': w['c_out_w'], 'out_b': _row(w['c_out_b'][0])}
    pd = {'g': _row(w['norm_mix'][3]), 'in_w': (w['d_in_w'], 0), 'in_b': _row(w['d_in_b'][0]),
          'ln_g': _row(w['d_ln_g'][0]), 'ln_b': _row(w['d_ln_b'][0]),
          'sp_w': w['d_sp_w'][0].reshape(SGU_GROUPS * SGU_CHUNK, SGU_CHUNK).astype(f32),
          'sp_bt': w['d_sp_b'][0].astype(f32).T, 'out': w['d_out_w'], 'out_b': _row(w['d_out_b'][0])}
    mixers = [(None, None, pa), (None, _conf_layer_bwd, pb),
              (_lru_layer_fwd, _lru_layer_bwd, pc), (_sgu_layer_fwd, _sgu_layer_bwd, pd)]

    h, u, s_ffn1 = _ffn_fwd(h, u, ffn[1], mix_gain[2])
    saved = [(s_mix0, s_ffn0), (s_mix1, s_ffn1)]
    for i in range(2, DEPTH):
        fwd, _, p = mixers[i]
        h, u, s_mix = fwd(h, u, p, ffn_gain[i])
        h, u, s_ffn = _ffn_fwd(h, u, ffn[i], mix_gain[i + 1])
        saved.append((s_mix, s_ffn))
    dh, d_final, loss = _loss_head(h, target, _row(w['norm_final']))

    def rows_sharded(g):
        return g.reshape(N_CHIPS, g.shape[0] // N_CHIPS, g.shape[1])

    bias_zero = jnp.zeros((1, D_MODEL), f32)
    g_ffn, g_mix, d_out_bias = [None] * DEPTH, [None] * DEPTH, [None] * DEPTH
    for i in reversed(range(1, DEPTH)):
        _, bwd, p = mixers[i]
        dh, g_ffn[i], d_out_bias[i], _ = _ffn_bwd(dh, ffn[i], saved[i][1], bias_zero)
        dh, g_mix[i] = bwd(dh, p, saved[i][0])
    _, gb, gc, gd = g_mix

    def late_direct_grads(d_up0, d_down0):
        return {'b_pw1_w': gb['pw1'], 'b_pw2_w': rows_sharded(gb['pw2']), 'c_in_w': gc['in_w'],
                'c_out_w': rows_sharded(gc['out']), 'd_in_w': gd['in_w'], 'd_out_w': rows_sharded(gd['out']),
                'f_up_w': [d_up0] + [g['up'] for g in g_ffn[1:]],
                'f_down_w': [rows_sharded(d_down0)] + [rows_sharded(g['down']) for g in g_ffn[1:]]}

    make_rider = (lambda d_down0: comm.swap_rider(late_direct_grads(None, d_down0))) if comm else None
    dh, g_ffn[0], d_out_bias[0], swapped = _ffn_bwd(dh, ffn[0], saved[0][1], bias_zero, make_rider)
    late_direct = late_direct_grads(g_ffn[0]['up'], g_ffn[0]['down'])
    partials = comm.early_partials(late_direct, swapped) if comm else []
    dh, g_mix[0], received = _ssd_layer_bwd(dh, pa, saved[0][0], to_chips=partials)
    ga = g_mix[0]

    grads = {**late_direct,
        'norm_mix': jnp.concatenate([g['g'] for g in g_mix], axis=0),
        'norm_ffn': jnp.concatenate([g['g'] for g in g_ffn], axis=0),
        'norm_final': d_final.reshape(-1),
        'a_in_proj': jnp.concatenate([ga['w_z'], ga['w_xbc'], ga['w_dt'][:, :SSD_HEADS]], axis=1)[None],
        'a_conv_w': ga['cw'][None], 'a_conv_b': ga['cb'], 'a_dt_bias': ga['dtb'][:, :SSD_HEADS],
        'a_log': ga['a_log'][:, :SSD_HEADS], 'a_d_skip': ga['dsk'][:, :SSD_HEADS], 'a_norm': ga['norm'],
        'a_out_proj': rows_sharded(ga['out']),
        'b_pw1_b': gb['b1'], 'b_dw_w': gb['dw_w'][None], 'b_dw_b': gb['dw_b'],
        'b_ln_g': gb['ln_g'], 'b_ln_b': gb['ln_b'], 'b_pw2_b': d_out_bias[1],
        'c_in_b': gc['in_b'], 'c_conv_w': gc['cw'][None], 'c_conv_b': gc['cb'],
        'c_ga_w': gc['ga_w'].reshape(1, LRU_W // LRU_BLOCK, LRU_BLOCK, LRU_BLOCK),
        'c_ga_b': gc['ga_b'].reshape(1, LRU_W // LRU_BLOCK, LRU_BLOCK),
        'c_gx_w': gc['gx_w'].reshape(1, LRU_W // LRU_BLOCK, LRU_BLOCK, LRU_BLOCK),
        'c_gx_b': gc['gx_b'].reshape(1, LRU_W // LRU_BLOCK, LRU_BLOCK),
        'c_lambda': gc['lam'], 'c_out_b': d_out_bias[2],
        'd_in_b': gd['in_b'], 'd_ln_g': gd['ln_g'], 'd_ln_b': gd['ln_b'],
        'd_sp_w': gd['sp_w'].reshape(1, SGU_GROUPS, SGU_CHUNK, SGU_CHUNK), 'd_sp_b': gd['sp_bt'].T[None],
        'd_out_b': d_out_bias[3],
        'f_conv_w': jnp.stack([g['cw'] for g in g_ffn]),
        'f_conv_b': jnp.concatenate([g['cb'] for g in g_ffn], axis=0),
    }
    return loss, dh, grads, (partials, received)


def _global_shape(name, shard_shape):
    ax = SHARD_AXIS[name]
    if ax is None:
        return tuple(shard_shape)
    s = list(shard_shape)
    s[ax] *= N_CHIPS
    return tuple(s)


def _step(x, target, weights, moments_m, moments_v):
    x2, t2 = x[0], target[0]
    shard_shapes = {n: weights[n].shape for n in WEIGHTS}
    c_pos = lax.axis_index("c")
    k_pos = 2 * lax.axis_index("x") + lax.axis_index("y")
    c_idx = c_pos.astype(jnp.int32).reshape(1)
    k_idx = k_pos.astype(jnp.int32).reshape(1)

    def halves_of(a):
        a2 = _as2d(a)
        return a2.reshape(2, a2.shape[0] // 2, a2.shape[1])

    def view_direct(n, g):
        g = g.reshape((N_CHIPS,) + shard_shapes[n])
        if n in DIRECT_COLS:
            return g
        if n == 'f_down_w':
            return [g[:, i].reshape(-1, g.shape[-1]) for i in range(DEPTH)]
        return g.reshape(-1, g.shape[-1])

    def sibling_sums(grads):
        mine_g = [g.reshape(N_CHIPS, 2, g.shape[1] // 2, g.shape[2]) for g in grads]
        theirs = _swap_with_sibling(mine_g)
        return [_sum_with_sibling(g, t, c_idx) for g, t in zip(mine_g, theirs)]

    def flatten_direct(grads, names):
        out = []
        for n in names:
            out += grads[n] if isinstance(grads[n], list) else [grads[n]]
        return out

    first = [halves_of(weights[n].astype(bf16)) for n in EARLY_DIRECT]
    first.append(_pack([weights[n] for n in PACKED_MM], bf16).reshape(2, -1, FLAT_COLS))
    first.append(_pack([weights[n] for n in SHARDED_VEC], f32).reshape(2, -1, FLAT_COLS))
    gathered = _own_slot(_gather_weights(first), first, k_pos)
    w = {n: weights[n] for n in REPLICATED}
    for n, g in zip(EARLY_DIRECT, gathered):
        w[n] = view_direct(n, g)
    all_mm = _unpack(gathered[-2].reshape(N_CHIPS, -1), [shard_shapes[n] for n in PACKED_MM])
    all_vec = _unpack(gathered[-1].reshape(N_CHIPS, -1), [shard_shapes[n] for n in SHARDED_VEC])
    for n, st in zip(PACKED_MM + SHARDED_VEC, all_mm + all_vec):
        w[n] = _full_from_shards(st, SHARD_AXIS[n])

    def halved(g):
        return g.reshape(N_CHIPS, 2, g.shape[1] // 2, g.shape[2])

    class Comm:
        late_blocks = [halves_of(weights[n].astype(bf16)) for n in LATE_FIRST]
        second_blocks = [halves_of(weights[n].astype(bf16)) for n in LATE_SECOND]

        @staticmethod
        def late_weights(arrived):
            arrived = _own_slot(arrived, Comm.late_blocks, k_pos)
            return {n: view_direct(n, g) for n, g in zip(LATE_FIRST, arrived)}

        @staticmethod
        def second_rider():
            return _gather_rider(Comm.second_blocks)

        @staticmethod
        def second_weights(arrived):
            arrived = _own_slot(arrived, Comm.second_blocks, k_pos)
            return {n: view_direct(n, g) for n, g in zip(LATE_SECOND, arrived)}

        @staticmethod
        def swap_rider(grads):
            return _swap_rider([halved(g) for g in flatten_direct(grads, LATE_DIRECT) if g is not None])

        @staticmethod
        def early_partials(grads, swapped):
            mine_g = [halved(g) for g in flatten_direct(grads, LATE_DIRECT)]
            missing = LAST_SWAPPED_INDEX
            theirs = swapped[:missing] + list(_swap_with_sibling([mine_g[missing]])) + swapped[missing:]
            return [_sum_with_sibling(g, t, c_idx) for g, t in zip(mine_g, theirs)]

    loss_part, dx, grads, (partials, received) = _local_step(x2, t2, w, Comm)

    packed = [_shards_of(grads[n].reshape(_global_shape(n, shard_shapes[n])), SHARD_AXIS[n]).reshape(N_CHIPS, -1)
              for n in PACKED_MM + SHARDED_VEC]
    flat = jnp.concatenate(packed, axis=1)
    n_flat = flat.shape[1]
    n_pad = -(-n_flat // FLAT_QUANTUM) * FLAT_QUANTUM
    flat = jnp.pad(flat, ((0, 0), (0, n_pad - n_flat))).reshape(N_CHIPS, -1, FLAT_COLS)
    last_partials = sibling_sums(flatten_direct(grads, EARLY_DIRECT) + [flat])
    last_received = _send_to_chips(last_partials)
    partials, received = list(partials) + last_partials, list(received) + list(last_received)
    my_halves = [_sum_chips(p, r, k_idx) for p, r in zip(partials, received)]
    joined = _own_slot(_join_halves(my_halves), my_halves, c_pos)
    g_shard, pos = {}, 0
    for n in LATE_DIRECT + EARLY_DIRECT:
        layers = shard_shapes[n][0]
        g_shard[n] = jnp.stack([j.reshape(shard_shapes[n][1:]) for j in joined[pos:pos + layers]])
        pos += layers
    flat_shapes = [shard_shapes[n] for n in PACKED_MM + SHARDED_VEC]
    g_shard.update(zip(PACKED_MM + SHARDED_VEC, _unpack(joined[-1].reshape(-1), flat_shapes)))

    small = jnp.concatenate([grads[n].reshape(-1) for n in REPLICATED] + [loss_part.reshape(-1)[:1]])
    n_small = small.shape[0]
    n_small_pad = -(-n_small // (SUBLANE * FLAT_COLS)) * (SUBLANE * FLAT_COLS)
    small = jnp.pad(small, (0, n_small_pad - n_small)).reshape(-1, FLAT_COLS)
    small = _all_sum_small(small).reshape(-1)
    g_rep = dict(zip(REPLICATED, _unpack(small, [shard_shapes[n] for n in REPLICATED])))
    loss = small[n_small - 1]

    g_all = {**g_shard, **g_rep}
    delta, new_m, new_v = {}, {}, {}
    for n in WEIGHTS:
        delta[n], new_m[n], new_v[n] = _adamw(weights[n], g_all[n], moments_m[n], moments_v[n], "adamw_" + n)
    return loss, dx[None], g_all, delta, new_m, new_v


def kernel(x, norm_mix, norm_ffn, norm_final, a_in_proj, a_conv_w, a_conv_b, a_dt_bias, a_log, a_d_skip, a_norm, a_out_proj, b_pw1_w, b_pw1_b, b_dw_w, b_dw_b, b_ln_g, b_ln_b, b_pw2_w, b_pw2_b, c_in_w, c_in_b, c_conv_w, c_conv_b, c_ga_w, c_ga_b, c_gx_w, c_gx_b, c_lambda, c_out_w, c_out_b, d_in_w, d_in_b, d_ln_g, d_ln_b, d_sp_w, d_sp_b, d_out_w, d_out_b, f_up_w, f_conv_w, f_conv_b, f_down_w, loss_target, m_norm_mix, m_norm_ffn, m_norm_final, m_a_in_proj, m_a_conv_w, m_a_conv_b, m_a_dt_bias, m_a_log, m_a_d_skip, m_a_norm, m_a_out_proj, m_b_pw1_w, m_b_pw1_b, m_b_dw_w, m_b_dw_b, m_b_ln_g, m_b_ln_b, m_b_pw2_w, m_b_pw2_b, m_c_in_w, m_c_in_b, m_c_conv_w, m_c_conv_b, m_c_ga_w, m_c_ga_b, m_c_gx_w, m_c_gx_b, m_c_lambda, m_c_out_w, m_c_out_b, m_d_in_w, m_d_in_b, m_d_ln_g, m_d_ln_b, m_d_sp_w, m_d_sp_b, m_d_out_w, m_d_out_b, m_f_up_w, m_f_conv_w, m_f_conv_b, m_f_down_w, v_norm_mix, v_norm_ffn, v_norm_final, v_a_in_proj, v_a_conv_w, v_a_conv_b, v_a_dt_bias, v_a_log, v_a_d_skip, v_a_norm, v_a_out_proj, v_b_pw1_w, v_b_pw1_b, v_b_dw_w, v_b_dw_b, v_b_ln_g, v_b_ln_b, v_b_pw2_w, v_b_pw2_b, v_c_in_w, v_c_in_b, v_c_conv_w, v_c_conv_b, v_c_ga_w, v_c_ga_b, v_c_gx_w, v_c_gx_b, v_c_lambda, v_c_out_w, v_c_out_b, v_d_in_w, v_d_in_b, v_d_ln_g, v_d_ln_b, v_d_sp_w, v_d_sp_b, v_d_out_w, v_d_out_b, v_f_up_w, v_f_conv_w, v_f_conv_b, v_f_down_w):
    args = locals()
    weights = {n: args[n] for n in WEIGHTS}
    moments_m = {n: args['m_' + n] for n in WEIGHTS}
    moments_v = {n: args['v_' + n] for n in WEIGHTS}
    loss, dx, grad, delta, new_m, new_v = _step(x, loss_target, weights, moments_m, moments_v)
    return (loss, dx, *[grad[n] for n in WEIGHTS], *[delta[n] for n in WEIGHTS],
            *[new_m[n] for n in WEIGHTS], *[new_v[n] for n in WEIGHTS])
```
